```python
import math
import jax, jax.numpy as jnp
from jax import lax
import numpy as np

D_MODEL = 1024
BATCH = 8
SEQ = 16384
DEPTH = 2

N_META = 16
NORM_EPS = 1e-6

ATTN_HEAD_DIM = 64
ATTN_HEADS = D_MODEL // ATTN_HEAD_DIM
ATTN_KV_HEADS = ATTN_HEADS // 8
ATTN_GROUPS = ATTN_HEADS // ATTN_KV_HEADS
ATTN_WIDTH = ATTN_HEADS * ATTN_HEAD_DIM
ATTN_KV_WIDTH = ATTN_KV_HEADS * ATTN_HEAD_DIM
ATTN_IN = 2 * ATTN_WIDTH + 2 * ATTN_KV_WIDTH
WINDOW = 128
ATTN_BLOCK = 128

DN_HEAD_DIM_K = 128
DN_HEAD_DIM_V = 128
DN_K_HEADS = D_MODEL // DN_HEAD_DIM_K
DN_V_HEADS = 2 * DN_K_HEADS
DN_KEY_WIDTH = DN_K_HEADS * DN_HEAD_DIM_K
DN_VALUE_WIDTH = DN_V_HEADS * DN_HEAD_DIM_V
DN_CONV = 4
DN_CHUNK = 64
DN_CONV_WIDTH = 2 * DN_KEY_WIDTH + DN_VALUE_WIDTH
DN_IN = DN_CONV_WIDTH + DN_VALUE_WIDTH + 2 * DN_V_HEADS

N_ATTN_LAYERS = (DEPTH + 1) // 2
N_DN_LAYERS = DEPTH // 2

kernel_name = "hybrid_swa_sink_alibi_gated_deltanet_meta"


def rms_norm(x, w):
    xf = x.astype(jnp.float32)
    y = xf * lax.rsqrt(jnp.mean(xf * xf, axis=-1, keepdims=True) + NORM_EPS)
    return (y * w.astype(jnp.float32)).astype(x.dtype)


def l2_norm(x):
    xf = x.astype(jnp.float32)
    return xf * lax.rsqrt(jnp.sum(xf * xf, axis=-1, keepdims=True) + NORM_EPS)


def alibi_slopes(n_heads):
    return jnp.asarray(np.exp2(-8.0 * np.arange(1, n_heads + 1) / n_heads), dtype=jnp.float32)


def banded_sink_attention(q, k, v, sinks):
    B, L = q.shape[:2]
    pad = ATTN_BLOCK - N_META
    Lp = L + pad
    nb = Lp // ATTN_BLOCK
    padt = lambda t: jnp.pad(t, ((0, 0), (pad, 0), (0, 0), (0, 0)))
    qb = padt(q).reshape(B, nb, ATTN_BLOCK, ATTN_KV_HEADS, ATTN_GROUPS, ATTN_HEAD_DIM)
    kb = padt(k).reshape(B, nb, ATTN_BLOCK, ATTN_KV_HEADS, ATTN_HEAD_DIM)
    vb = padt(v).reshape(B, nb, ATTN_BLOCK, ATTN_KV_HEADS, ATTN_HEAD_DIM)
    prev = lambda t: jnp.pad(t, ((0, 0), (1, 0), (0, 0), (0, 0), (0, 0)))[:, :-1]
    k_band = jnp.concatenate([prev(kb), kb], axis=2)
    v_band = jnp.concatenate([prev(vb), vb], axis=2)
    k_meta = k[:, :N_META]
    v_meta = v[:, :N_META]

    scale = ATTN_HEAD_DIM ** -0.5
    s_band = jnp.einsum('bnqhgd,bnkhd->bnhgqk', qb, k_band,
                        preferred_element_type=jnp.float32) * scale
    s_meta = jnp.einsum('bnqhgd,bmhd->bnhgqm', qb, k_meta,
                        preferred_element_type=jnp.float32) * scale

    pos_q = jnp.arange(Lp, dtype=jnp.int32).reshape(nb, ATTN_BLOCK) - pad
    pos_kb = jnp.concatenate([pos_q - ATTN_BLOCK, pos_q], axis=-1)
    pos_meta = jnp.arange(N_META, dtype=jnp.int32)
    dist_band = pos_q[:, :, None] - pos_kb[:, None, :]
    valid_band = (pos_kb[:, None, :] >= N_META) & (dist_band >= 0) & (dist_band < WINDOW)
    dist_meta = pos_q[:, :, None] - pos_meta[None, None, :]
    valid_meta = dist_meta >= 0

    slopes = alibi_slopes(ATTN_HEADS).reshape(1, 1, ATTN_KV_HEADS, ATTN_GROUPS, 1, 1)
    clipdist = lambda d: jnp.minimum(d, WINDOW).astype(jnp.float32)[None, :, None, None]
    s_band = jnp.where(valid_band[None, :, None, None], s_band - slopes * clipdist(dist_band), -jnp.inf)
    s_meta = jnp.where(valid_meta[None, :, None, None], s_meta - slopes * clipdist(dist_meta), -jnp.inf)

    sink = jnp.broadcast_to(
        sinks.astype(jnp.float32).reshape(1, 1, ATTN_KV_HEADS, ATTN_GROUPS, 1, 1),
        s_band.shape[:-1] + (1,))
    p = jax.nn.softmax(jnp.concatenate([s_band, s_meta, sink], axis=-1), axis=-1)
    p_band = p[..., :2 * ATTN_BLOCK].astype(v.dtype)
    p_meta = p[..., 2 * ATTN_BLOCK:2 * ATTN_BLOCK + N_META].astype(v.dtype)
    o = (jnp.einsum('bnhgqk,bnkhd->bnqhgd', p_band, v_band)
         + jnp.einsum('bnhgqm,bmhd->bnqhgd', p_meta, v_meta))
    return o.reshape(B, Lp, ATTN_WIDTH)[:, pad:]


def attention_mixer(h, norm_w, w_in, q_norm_w, k_norm_w, sinks, w_out):
    B, L, _ = h.shape
    u = rms_norm(h, norm_w) @ w_in
    q, k, v, gate = jnp.split(
        u, [ATTN_WIDTH, ATTN_WIDTH + ATTN_KV_WIDTH, ATTN_WIDTH + 2 * ATTN_KV_WIDTH], axis=-1)
    q = rms_norm(q.reshape(B, L, ATTN_HEADS, ATTN_HEAD_DIM), q_norm_w)
    k = rms_norm(k.reshape(B, L, ATTN_KV_HEADS, ATTN_HEAD_DIM), k_norm_w)
    v = v.reshape(B, L, ATTN_KV_HEADS, ATTN_HEAD_DIM)
    o = banded_sink_attention(q, k, v, sinks)
    return (o * jax.nn.silu(gate)) @ w_out


def causal_depthwise_conv(x, w):
    K, C = w.shape
    return lax.conv_general_dilated(
        x, w[:, None, :], window_strides=(1,), padding=[(K - 1, 0)],
        dimension_numbers=('NWC', 'WIO', 'NWC'), feature_group_count=C)


def chunked_gated_delta_rule(q, k, v, beta, g):
    B, L, H, _ = q.shape
    pad = DN_CHUNK - N_META
    Lc = L + pad
    n = Lc // DN_CHUNK

    def to_chunks(t):
        t = jnp.pad(t.astype(jnp.float32), [(0, 0), (pad, 0)] + [(0, 0)] * (t.ndim - 2))
        t = t.reshape((B, n, DN_CHUNK) + t.shape[2:])
        return jnp.swapaxes(jnp.moveaxis(t, 1, 0), 2, 3)

    xs = (to_chunks(q), to_chunks(k), to_chunks(v), to_chunks(beta), to_chunks(g))
    causal = jnp.tril(jnp.ones((DN_CHUNK, DN_CHUNK), dtype=bool))
    strict = jnp.tril(jnp.ones((DN_CHUNK, DN_CHUNK), dtype=bool), -1)
    eye = jnp.eye(DN_CHUNK, dtype=jnp.float32)

    def step(S, inp):
        qc, kc, vc, bc, gc = inp
        gcum = jnp.cumsum(gc, axis=-1)
        decay = jnp.exp(jnp.where(causal, gcum[..., :, None] - gcum[..., None, :], -jnp.inf))
        kb = kc * bc[..., None]
        m = jnp.where(strict, jnp.einsum('bhcd,bhsd->bhcs', kb, kc) * decay, 0.0)
        rhs = jnp.concatenate([vc * bc[..., None], kb * jnp.exp(gcum)[..., None]], axis=-1)
        sol = lax.linalg.triangular_solve(m + eye, rhs, left_side=True, lower=True,
                                          unit_diagonal=True)
        u, w = sol[..., :DN_HEAD_DIM_V], sol[..., DN_HEAD_DIM_V:]
        v_new = u - jnp.einsum('bhcd,bhdv->bhcv', w, S)
        attn = jnp.einsum('bhcd,bhsd->bhcs', qc, kc) * decay
        o = (jnp.einsum('bhcd,bhdv->bhcv', qc * jnp.exp(gcum)[..., None], S)
             + jnp.einsum('bhcs,bhsv->bhcv', attn, v_new))
        g_last = gcum[..., -1]
        k_state = kc * jnp.exp(g_last[..., None] - gcum)[..., None]
        S = S * jnp.exp(g_last)[..., None, None] + jnp.einsum('bhcd,bhcv->bhdv', k_state, v_new)
        return S, o

    S0 = jnp.zeros((B, H, DN_HEAD_DIM_K, DN_HEAD_DIM_V), jnp.float32)
    _, o = lax.scan(step, S0, xs)
    o = jnp.moveaxis(jnp.swapaxes(o, 2, 3), 0, 1).reshape(B, Lc, H, DN_HEAD_DIM_V)
    return o[:, pad:]


def deltanet_mixer(h, norm_w, w_in, conv_w, a_log, dt_bias, o_norm_w, w_out):
    B, L, _ = h.shape
    u = rms_norm(h, norm_w) @ w_in
    qkv, z, b, a = jnp.split(
        u, [DN_CONV_WIDTH, DN_CONV_WIDTH + DN_VALUE_WIDTH,
            DN_CONV_WIDTH + DN_VALUE_WIDTH + DN_V_HEADS], axis=-1)
    qkv = jax.nn.silu(causal_depthwise_conv(qkv, conv_w))
    q, k, v = jnp.split(qkv, [DN_KEY_WIDTH, 2 * DN_KEY_WIDTH], axis=-1)
    rep = DN_V_HEADS // DN_K_HEADS
    q = jnp.repeat(l2_norm(q.reshape(B, L, DN_K_HEADS, DN_HEAD_DIM_K)), rep, axis=2)
    k = jnp.repeat(l2_norm(k.reshape(B, L, DN_K_HEADS, DN_HEAD_DIM_K)), rep, axis=2)
    q = q * (DN_HEAD_DIM_K ** -0.5)
    v = v.reshape(B, L, DN_V_HEADS, DN_HEAD_DIM_V)
    beta = jax.nn.sigmoid(b.astype(jnp.float32))
    g = -jnp.exp(a_log.astype(jnp.float32)) * jax.nn.softplus(
        a.astype(jnp.float32) + dt_bias.astype(jnp.float32))
    o = chunked_gated_delta_rule(q, k, v, beta, g).astype(h.dtype)
    o = rms_norm(o, o_norm_w) * jax.nn.silu(z.reshape(B, L, DN_V_HEADS, DN_HEAD_DIM_V))
    return o.reshape(B, L, DN_VALUE_WIDTH) @ w_out


def _fwd_setup_inputs(seed: int = 0) -> dict:
    key = jax.random.key(seed)
    ks = jax.random.split(key, 20)
    f32 = jnp.float32
    nA, nB = N_ATTN_LAYERS, N_DN_LAYERS
    out_scale = 0.5
    dt = jnp.exp(jax.random.uniform(ks[13], (nB, DN_V_HEADS), f32,
                                    math.log(1e-3), math.log(1e-1)))
    return {
        "x": jax.random.normal(ks[0], (BATCH, SEQ, D_MODEL), f32),
        "meta_tokens": jax.random.normal(ks[1], (N_META, D_MODEL), f32),
        "attn_norm_w": 1.0 + 0.02 * jax.random.normal(ks[2], (nA, D_MODEL), f32),
        "attn_w_in": jax.random.normal(ks[3], (nA, D_MODEL, ATTN_IN), f32) * D_MODEL ** -0.5,
        "attn_q_norm_w": 1.0 + 0.02 * jax.random.normal(ks[4], (nA, ATTN_HEAD_DIM), f32),
        "attn_k_norm_w": 1.0 + 0.02 * jax.random.normal(ks[5], (nA, ATTN_HEAD_DIM), f32),
        "attn_sinks": 0.5 * jax.random.normal(ks[6], (nA, ATTN_HEADS), f32),
        "attn_w_out": jax.random.normal(ks[7], (nA, ATTN_WIDTH, D_MODEL), f32)
                      * ATTN_WIDTH ** -0.5 * out_scale,
        "dn_norm_w": 1.0 + 0.02 * jax.random.normal(ks[8], (nB, D_MODEL), f32),
        "dn_w_in": jax.random.normal(ks[9], (nB, D_MODEL, DN_IN), f32) * D_MODEL ** -0.5,
        "dn_conv_w": jax.random.normal(ks[10], (nB, DN_CONV, DN_CONV_WIDTH), f32) * DN_CONV ** -0.5,
        "dn_a_log": jnp.log(jax.random.uniform(ks[11], (nB, DN_V_HEADS), f32, 1.0, 16.0)),
        "dn_dt_bias": dt + jnp.log(-jnp.expm1(-dt)),
        "dn_o_norm_w": 1.0 + 0.02 * jax.random.normal(ks[12], (nB, DN_HEAD_DIM_V), f32),
        "dn_w_out": jax.random.normal(ks[14], (nB, DN_VALUE_WIDTH, D_MODEL), f32)
                    * DN_VALUE_WIDTH ** -0.5 * out_scale,
    }


def _fwd_reference(x, meta_tokens, attn_norm_w, attn_w_in, attn_q_norm_w, attn_k_norm_w,
              attn_sinks, attn_w_out, dn_norm_w, dn_w_in, dn_conv_w, dn_a_log,
              dn_dt_bias, dn_o_norm_w, dn_w_out):
    B = x.shape[0]
    meta = jnp.broadcast_to(meta_tokens.astype(x.dtype)[None], (B, N_META, x.shape[-1]))
    h = jnp.concatenate([meta, x], axis=1)
    for i in range(DEPTH):
        j = i // 2
        if i % 2 == 0:
            h = h + attention_mixer(h, attn_norm_w[j], attn_w_in[j], attn_q_norm_w[j],
                                    attn_k_norm_w[j], attn_sinks[j], attn_w_out[j])
        else:
            h = h + deltanet_mixer(h, dn_norm_w[j], dn_w_in[j], dn_conv_w[j], dn_a_log[j],
                                   dn_dt_bias[j], dn_o_norm_w[j], dn_w_out[j])
    return h[:, N_META:]


import jax as _jax
import jax.numpy as _jnp

TWIN_FORMAT = 'train_step'
FWD_PARAMS = ['x', 'meta_tokens', 'attn_norm_w', 'attn_w_in', 'attn_q_norm_w', 'attn_k_norm_w', 'attn_sinks', 'attn_w_out', 'dn_norm_w', 'dn_w_in', 'dn_conv_w', 'dn_a_log', 'dn_dt_bias', 'dn_o_norm_w', 'dn_w_out']
TWIN_WEIGHTS = ['meta_tokens', 'attn_norm_w', 'attn_w_in', 'attn_q_norm_w', 'attn_k_norm_w', 'attn_sinks', 'attn_w_out', 'dn_norm_w', 'dn_w_in', 'dn_conv_w', 'dn_a_log', 'dn_dt_bias', 'dn_o_norm_w', 'dn_w_out']
TWIN_DIFF_INPUT = 'x'
TWIN_INPUTS = ['x', 'meta_tokens', 'attn_norm_w', 'attn_w_in', 'attn_q_norm_w', 'attn_k_norm_w', 'attn_sinks', 'attn_w_out', 'dn_norm_w', 'dn_w_in', 'dn_conv_w', 'dn_a_log', 'dn_dt_bias', 'dn_o_norm_w', 'dn_w_out', 'loss_target', 'm_meta_tokens', 'm_attn_norm_w', 'm_attn_w_in', 'm_attn_q_norm_w', 'm_attn_k_norm_w', 'm_attn_sinks', 'm_attn_w_out', 'm_dn_norm_w', 'm_dn_w_in', 'm_dn_conv_w', 'm_dn_a_log', 'm_dn_dt_bias', 'm_dn_o_norm_w', 'm_dn_w_out', 'v_meta_tokens', 'v_attn_norm_w', 'v_attn_w_in', 'v_attn_q_norm_w', 'v_attn_k_norm_w', 'v_attn_sinks', 'v_attn_w_out', 'v_dn_norm_w', 'v_dn_w_in', 'v_dn_conv_w', 'v_dn_a_log', 'v_dn_dt_bias', 'v_dn_o_norm_w', 'v_dn_w_out']
TWIN_OUTPUTS = ['loss', 'grad_x', 'grad_meta_tokens', 'grad_attn_norm_w', 'grad_attn_w_in', 'grad_attn_q_norm_w', 'grad_attn_k_norm_w', 'grad_attn_sinks', 'grad_attn_w_out', 'grad_dn_norm_w', 'grad_dn_w_in', 'grad_dn_conv_w', 'grad_dn_a_log', 'grad_dn_dt_bias', 'grad_dn_o_norm_w', 'grad_dn_w_out', 'delta_meta_tokens', 'delta_attn_norm_w', 'delta_attn_w_in', 'delta_attn_q_norm_w', 'delta_attn_k_norm_w', 'delta_attn_sinks', 'delta_attn_w_out', 'delta_dn_norm_w', 'delta_dn_w_in', 'delta_dn_conv_w', 'delta_dn_a_log', 'delta_dn_dt_bias', 'delta_dn_o_norm_w', 'delta_dn_w_out', 'new_m_meta_tokens', 'new_m_attn_norm_w', 'new_m_attn_w_in', 'new_m_attn_q_norm_w', 'new_m_attn_k_norm_w', 'new_m_attn_sinks', 'new_m_attn_w_out', 'new_m_dn_norm_w', 'new_m_dn_w_in', 'new_m_dn_conv_w', 'new_m_dn_a_log', 'new_m_dn_dt_bias', 'new_m_dn_o_norm_w', 'new_m_dn_w_out', 'new_v_meta_tokens', 'new_v_attn_norm_w', 'new_v_attn_w_in', 'new_v_attn_q_norm_w', 'new_v_attn_k_norm_w', 'new_v_attn_sinks', 'new_v_attn_w_out', 'new_v_dn_norm_w', 'new_v_dn_w_in', 'new_v_dn_conv_w', 'new_v_dn_a_log', 'new_v_dn_dt_bias', 'new_v_dn_o_norm_w', 'new_v_dn_w_out']
TWIN_LEAF_KINDS = {'loss': 'loss', 'grad_x': 'grad_x', 'grad_meta_tokens': 'grad_w', 'grad_attn_norm_w': 'grad_w', 'grad_attn_w_in': 'grad_w', 'grad_attn_q_norm_w': 'grad_w', 'grad_attn_k_norm_w': 'grad_w', 'grad_attn_sinks': 'grad_w', 'grad_attn_w_out': 'grad_w', 'grad_dn_norm_w': 'grad_w', 'grad_dn_w_in': 'grad_w', 'grad_dn_conv_w': 'grad_w', 'grad_dn_a_log': 'grad_w', 'grad_dn_dt_bias': 'grad_w', 'grad_dn_o_norm_w': 'grad_w', 'grad_dn_w_out': 'grad_w', 'delta_meta_tokens': 'delta_w', 'delta_attn_norm_w': 'delta_w', 'delta_attn_w_in': 'delta_w', 'delta_attn_q_norm_w': 'delta_w', 'delta_attn_k_norm_w': 'delta_w', 'delta_attn_sinks': 'delta_w', 'delta_attn_w_out': 'delta_w', 'delta_dn_norm_w': 'delta_w', 'delta_dn_w_in': 'delta_w', 'delta_dn_conv_w': 'delta_w', 'delta_dn_a_log': 'delta_w', 'delta_dn_dt_bias': 'delta_w', 'delta_dn_o_norm_w': 'delta_w', 'delta_dn_w_out': 'delta_w', 'new_m_meta_tokens': 'new_m', 'new_m_attn_norm_w': 'new_m', 'new_m_attn_w_in': 'new_m', 'new_m_attn_q_norm_w': 'new_m', 'new_m_attn_k_norm_w': 'new_m', 'new_m_attn_sinks': 'new_m', 'new_m_attn_w_out': 'new_m', 'new_m_dn_norm_w': 'new_m', 'new_m_dn_w_in': 'new_m', 'new_m_dn_conv_w': 'new_m', 'new_m_dn_a_log': 'new_m', 'new_m_dn_dt_bias': 'new_m', 'new_m_dn_o_norm_w': 'new_m', 'new_m_dn_w_out': 'new_m', 'new_v_meta_tokens': 'new_v', 'new_v_attn_norm_w': 'new_v', 'new_v_attn_w_in': 'new_v', 'new_v_attn_q_norm_w': 'new_v', 'new_v_attn_k_norm_w': 'new_v', 'new_v_attn_sinks': 'new_v', 'new_v_attn_w_out': 'new_v', 'new_v_dn_norm_w': 'new_v', 'new_v_dn_w_in': 'new_v', 'new_v_dn_conv_w': 'new_v', 'new_v_dn_a_log': 'new_v', 'new_v_dn_dt_bias': 'new_v', 'new_v_dn_o_norm_w': 'new_v', 'new_v_dn_w_out': 'new_v'}


def _forward(args):
    return _fwd_reference(*[args[k] for k in FWD_PARAMS])


def _output_shape():
    def fwd():
        inp = _fwd_setup_inputs(0)
        return _fwd_reference(*[inp[k] for k in FWD_PARAMS])
    out = _jax.eval_shape(fwd)
    return out.shape, out.dtype

N_MICROBATCH = 1
ADAM_LR = 0.001
ADAM_B1 = 0.9
ADAM_B2 = 0.999
ADAM_EPS = 1e-08
ADAM_WD = 0.01
ADAM_STEP = 10
PER_EXAMPLE_BATCH_AXIS = {'x': 0, 'loss_target': 0}
SHARED_INPUTS = []
_WEIGHT_DTYPES = {'meta_tokens': _jnp.float32, 'attn_norm_w': _jnp.float32, 'attn_w_in': _jnp.float32, 'attn_q_norm_w': _jnp.float32, 'attn_k_norm_w': _jnp.float32, 'attn_sinks': _jnp.float32, 'attn_w_out': _jnp.float32, 'dn_norm_w': _jnp.float32, 'dn_w_in': _jnp.float32, 'dn_conv_w': _jnp.float32, 'dn_a_log': _jnp.float32, 'dn_dt_bias': _jnp.float32, 'dn_o_norm_w': _jnp.float32, 'dn_w_out': _jnp.float32}
MOMENT_SCALE = {'meta_tokens': 8.480208e-03, 'attn_norm_w': 1.821528e+00, 'attn_w_in': 7.229077e-02, 'attn_q_norm_w': 7.014269e+00, 'attn_k_norm_w': 6.968341e+00, 'attn_sinks': 1.329501e+01, 'attn_w_out': 1.146448e-01, 'dn_norm_w': 1.312010e+01, 'dn_w_in': 1.780207e-01, 'dn_conv_w': 2.450547e-01, 'dn_a_log': 2.108428e+01, 'dn_dt_bias': 2.017208e+01, 'dn_o_norm_w': 8.646244e+01, 'dn_w_out': 9.883514e-01}


def _to_microbatches(a, axis):
    t = _jnp.moveaxis(a, axis, 0)
    t = t.reshape((N_MICROBATCH, t.shape[0] // N_MICROBATCH) + t.shape[1:])
    return _jnp.moveaxis(t, 1, axis + 1)


def setup_inputs(seed: int = 0) -> dict:
    inp = _fwd_setup_inputs(seed)
    key = _jax.random.fold_in(_jax.random.key(seed), 7919)
    shape, _ = _output_shape()
    out = dict(inp)
    out["loss_target"] = _jax.random.normal(_jax.random.fold_in(key, 0), shape, _jnp.float32)
    for i, name in enumerate(TWIN_WEIGHTS):
        w = inp[name].astype(_jnp.float32)
        if MOMENT_SCALE is None:
            s = _jnp.sqrt(_jnp.mean(_jnp.square(w)) + 1e-30)
        else:
            s = MOMENT_SCALE[name]
        km, kv = _jax.random.split(_jax.random.fold_in(key, i + 1))
        out[name] = w
        out["m_" + name] = s * _jax.random.normal(km, w.shape, _jnp.float32)
        out["v_" + name] = (s * s) * _jax.random.uniform(kv, w.shape, _jnp.float32, 0.5, 1.5)
    if N_MICROBATCH > 1:
        for name, axis in PER_EXAMPLE_BATCH_AXIS.items():
            out[name] = _to_microbatches(out[name], axis)
    return {'x': out['x'], 'meta_tokens': out['meta_tokens'], 'attn_norm_w': out['attn_norm_w'], 'attn_w_in': out['attn_w_in'], 'attn_q_norm_w': out['attn_q_norm_w'], 'attn_k_norm_w': out['attn_k_norm_w'], 'attn_sinks': out['attn_sinks'], 'attn_w_out': out['attn_w_out'], 'dn_norm_w': out['dn_norm_w'], 'dn_w_in': out['dn_w_in'], 'dn_conv_w': out['dn_conv_w'], 'dn_a_log': out['dn_a_log'], 'dn_dt_bias': out['dn_dt_bias'], 'dn_o_norm_w': out['dn_o_norm_w'], 'dn_w_out': out['dn_w_out'], 'loss_target': out['loss_target'], 'm_meta_tokens': out['m_meta_tokens'], 'm_attn_norm_w': out['m_attn_norm_w'], 'm_attn_w_in': out['m_attn_w_in'], 'm_attn_q_norm_w': out['m_attn_q_norm_w'], 'm_attn_k_norm_w': out['m_attn_k_norm_w'], 'm_attn_sinks': out['m_attn_sinks'], 'm_attn_w_out': out['m_attn_w_out'], 'm_dn_norm_w': out['m_dn_norm_w'], 'm_dn_w_in': out['m_dn_w_in'], 'm_dn_conv_w': out['m_dn_conv_w'], 'm_dn_a_log': out['m_dn_a_log'], 'm_dn_dt_bias': out['m_dn_dt_bias'], 'm_dn_o_norm_w': out['m_dn_o_norm_w'], 'm_dn_w_out': out['m_dn_w_out'], 'v_meta_tokens': out['v_meta_tokens'], 'v_attn_norm_w': out['v_attn_norm_w'], 'v_attn_w_in': out['v_attn_w_in'], 'v_attn_q_norm_w': out['v_attn_q_norm_w'], 'v_attn_k_norm_w': out['v_attn_k_norm_w'], 'v_attn_sinks': out['v_attn_sinks'], 'v_attn_w_out': out['v_attn_w_out'], 'v_dn_norm_w': out['v_dn_norm_w'], 'v_dn_w_in': out['v_dn_w_in'], 'v_dn_conv_w': out['v_dn_conv_w'], 'v_dn_a_log': out['v_dn_a_log'], 'v_dn_dt_bias': out['v_dn_dt_bias'], 'v_dn_o_norm_w': out['v_dn_o_norm_w'], 'v_dn_w_out': out['v_dn_w_out']}


def _loss(weights, diff, rest, loss_target):
    with _jax.named_scope("forward"):
        args = {**rest, TWIN_DIFF_INPUT: diff, **{k: w.astype(_WEIGHT_DTYPES[k]) for k, w in weights.items()}}
        y = _forward(args)
    with _jax.named_scope("loss_head"):
        err = _jnp.square(y.astype(_jnp.float32) - loss_target)
        return 0.5 * _jnp.sum(_jnp.mean(err, axis=-1)) if err.ndim else 0.5 * err


def _adamw(w, g, m, v):
    m = ADAM_B1 * m + (1.0 - ADAM_B1) * g
    v = ADAM_B2 * v + (1.0 - ADAM_B2) * _jnp.square(g)
    m_hat = m / (1.0 - ADAM_B1 ** ADAM_STEP)
    v_hat = v / (1.0 - ADAM_B2 ** ADAM_STEP)
    delta = -ADAM_LR * (m_hat / (_jnp.sqrt(v_hat) + ADAM_EPS) + ADAM_WD * w)
    return delta, m, v


def reference(x, meta_tokens, attn_norm_w, attn_w_in, attn_q_norm_w, attn_k_norm_w, attn_sinks, attn_w_out, dn_norm_w, dn_w_in, dn_conv_w, dn_a_log, dn_dt_bias, dn_o_norm_w, dn_w_out, loss_target, m_meta_tokens, m_attn_norm_w, m_attn_w_in, m_attn_q_norm_w, m_attn_k_norm_w, m_attn_sinks, m_attn_w_out, m_dn_norm_w, m_dn_w_in, m_dn_conv_w, m_dn_a_log, m_dn_dt_bias, m_dn_o_norm_w, m_dn_w_out, v_meta_tokens, v_attn_norm_w, v_attn_w_in, v_attn_q_norm_w, v_attn_k_norm_w, v_attn_sinks, v_attn_w_out, v_dn_norm_w, v_dn_w_in, v_dn_conv_w, v_dn_a_log, v_dn_dt_bias, v_dn_o_norm_w, v_dn_w_out):
    given = dict(x=x, meta_tokens=meta_tokens, attn_norm_w=attn_norm_w, attn_w_in=attn_w_in, attn_q_norm_w=attn_q_norm_w, attn_k_norm_w=attn_k_norm_w, attn_sinks=attn_sinks, attn_w_out=attn_w_out, dn_norm_w=dn_norm_w, dn_w_in=dn_w_in, dn_conv_w=dn_conv_w, dn_a_log=dn_a_log, dn_dt_bias=dn_dt_bias, dn_o_norm_w=dn_o_norm_w, dn_w_out=dn_w_out, loss_target=loss_target, m_meta_tokens=m_meta_tokens, m_attn_norm_w=m_attn_norm_w, m_attn_w_in=m_attn_w_in, m_attn_q_norm_w=m_attn_q_norm_w, m_attn_k_norm_w=m_attn_k_norm_w, m_attn_sinks=m_attn_sinks, m_attn_w_out=m_attn_w_out, m_dn_norm_w=m_dn_norm_w, m_dn_w_in=m_dn_w_in, m_dn_conv_w=m_dn_conv_w, m_dn_a_log=m_dn_a_log, m_dn_dt_bias=m_dn_dt_bias, m_dn_o_norm_w=m_dn_o_norm_w, m_dn_w_out=m_dn_w_out, v_meta_tokens=v_meta_tokens, v_attn_norm_w=v_attn_norm_w, v_attn_w_in=v_attn_w_in, v_attn_q_norm_w=v_attn_q_norm_w, v_attn_k_norm_w=v_attn_k_norm_w, v_attn_sinks=v_attn_sinks, v_attn_w_out=v_attn_w_out, v_dn_norm_w=v_dn_norm_w, v_dn_w_in=v_dn_w_in, v_dn_conv_w=v_dn_conv_w, v_dn_a_log=v_dn_a_log, v_dn_dt_bias=v_dn_dt_bias, v_dn_o_norm_w=v_dn_o_norm_w, v_dn_w_out=v_dn_w_out)
    weights = {n: given[n] for n in TWIN_WEIGHTS}
    shared = {n: given[n] for n in SHARED_INPUTS}
    per_example = {n: given[n] for n in ['x']}
    grad_fn = _jax.value_and_grad(_loss, argnums=(0, 1))

    def one_microbatch(ex, loss_target):
        ex = dict(ex)
        diff = ex.pop(TWIN_DIFF_INPUT)
        return grad_fn(weights, diff, {**shared, **ex}, loss_target)

    if N_MICROBATCH == 1:
        loss, (grad_w, grad_x) = one_microbatch(per_example, given["loss_target"])
    else:
        def body(carry, xs):
            loss_sum, grad_sum = carry
            l_k, (gw_k, gx_k) = one_microbatch(xs[0], xs[1])
            with _jax.named_scope("update"):
                return (loss_sum + l_k, _jax.tree.map(_jnp.add, grad_sum, gw_k)), gx_k

        init = (_jnp.zeros((), _jnp.float32), _jax.tree.map(_jnp.zeros_like, weights))
        (loss, grad_w), grad_x = _jax.lax.scan(body, init, (per_example, given["loss_target"]))
    with _jax.named_scope("update"):
        delta_w, new_m, new_v = {}, {}, {}
        for n in TWIN_WEIGHTS:
            delta_w[n], new_m[n], new_v[n] = _adamw(weights[n], grad_w[n], given["m_" + n], given["v_" + n])
    return (loss, grad_x, *[grad_w[n] for n in TWIN_WEIGHTS], *[delta_w[n] for n in TWIN_WEIGHTS],
            *[new_m[n] for n in TWIN_WEIGHTS], *[new_v[n] for n in TWIN_WEIGHTS])
```

```python
import functools

import jax
import jax.numpy as jnp
from jax import lax
from jax.experimental import pallas as pl
from jax.experimental.pallas import tpu as pltpu

F32 = jnp.float32
BF16 = jnp.bfloat16
SDS = jax.ShapeDtypeStruct
MESH = pl.DeviceIdType.MESH

D_MODEL = 1024
N_META = 16
EPS = 1e-6
BLK = 128
CH = 64
PAD = BLK - N_META
HEADS = 16
HD = 64
KVW = 256
DN_H = 16
DN_KH = 8
DK = 128
SLOPES = [2.0 ** (-8.0 * (h + 1) / HEADS) for h in range(HEADS)]
NEG = -1e30
NT = (((1,), (1,)), ((), ()))
TN = (((0,), (0,)), ((), ()))
HI = lax.Precision.HIGHEST

ADAM_LR, ADAM_B1, ADAM_B2, ADAM_EPS, ADAM_WD, ADAM_STEP = 0.001, 0.9, 0.999, 1e-08, 0.01, 10

VMEM_LIMIT = 56 * 1024 * 1024


def _cp(*sem):
    return pltpu.CompilerParams(dimension_semantics=sem, vmem_limit_bytes=VMEM_LIMIT)


def _row_tile(rows):
    for t in (384, 256, 128):
        if rows % t == 0:
            return t
    raise ValueError(rows)


def _dot(a, b, dims=None, precision=None):
    if dims is None:
        return jnp.dot(a, b, preferred_element_type=F32, precision=precision)
    return lax.dot_general(a, b, dims, preferred_element_type=F32, precision=precision)


def _silu(x):
    return x * jax.nn.sigmoid(x)


def _dsilu(x):
    s = jax.nn.sigmoid(x)
    return s * (1.0 + x * (1.0 - s))


def _rms(x):
    return lax.rsqrt(jnp.mean(x * x, axis=-1, keepdims=True) + EPS)


def norm_matmul(h, nw, w, tn, name):
    rows, k = h.shape
    n = w.shape[1]
    tm = _row_tile(rows)

    def body(h_ref, nw_ref, w_ref, o_ref, xn_ref, xs):
        @pl.when(pl.program_id(1) == 0)
        def _():
            x = h_ref[...]
            xn = (x * _rms(x) * nw_ref[...]).astype(BF16)
            xs[...] = xn
            xn_ref[...] = xn

        o_ref[...] = _dot(xs[...], w_ref[...])

    return pl.pallas_call(
        body, grid=(rows // tm, n // tn),
        in_specs=[pl.BlockSpec((tm, k), lambda i, j: (i, 0)), pl.BlockSpec((1, k), lambda i, j: (0, 0)),
                  pl.BlockSpec((k, tn), lambda i, j: (0, j))],
        out_specs=[pl.BlockSpec((tm, tn), lambda i, j: (i, j)), pl.BlockSpec((tm, k), lambda i, j: (i, 0))],
        out_shape=[SDS((rows, n), F32), SDS((rows, k), BF16)],
        scratch_shapes=[pltpu.VMEM((tm, k), BF16)],
        compiler_params=_cp("parallel", "arbitrary"), name=name)(h, nw, w)


def matmul_residual(a, w, res, name):
    rows, k = a.shape
    n = w.shape[1]
    tm = _row_tile(rows)

    def body(a_ref, w_ref, r_ref, o_ref):
        o_ref[...] = r_ref[...] + _dot(a_ref[...], w_ref[...])

    return pl.pallas_call(
        body, grid=(rows // tm,),
        in_specs=[pl.BlockSpec((tm, k), lambda i: (i, 0)), pl.BlockSpec((k, n), lambda i: (0, 0)),
                  pl.BlockSpec((tm, n), lambda i: (i, 0))],
        out_specs=pl.BlockSpec((tm, n), lambda i: (i, 0)),
        out_shape=SDS((rows, n), F32), compiler_params=_cp("parallel"), name=name)(a, w, res)


def wgrad(a, b, name):
    rows, k = a.shape
    n = b.shape[1]
    tm = _row_tile(rows)
    tn = min(n, 1024)

    def body(a_ref, b_ref, o_ref):
        @pl.when(pl.program_id(1) == 0)
        def _():
            o_ref[...] = jnp.zeros_like(o_ref)

        o_ref[...] += _dot(a_ref[...], b_ref[...].astype(BF16), TN)

    return pl.pallas_call(
        body, grid=(n // tn, rows // tm),
        in_specs=[pl.BlockSpec((tm, k), lambda j, i: (i, 0)), pl.BlockSpec((tm, tn), lambda j, i: (i, j))],
        out_specs=pl.BlockSpec((k, tn), lambda j, i: (0, j)),
        out_shape=SDS((k, n), F32), compiler_params=_cp("parallel", "arbitrary"), name=name)(a, b)


def in_proj_bwd(dus, ws, h, nw, dh_next, name):
    rows, k = h.shape
    tm = 128
    nd = len(dus)
    nt = rows // tm

    def body(*refs):
        du_refs, w_refs = refs[:nd], refs[nd:2 * nd]
        h_ref, nw_ref, dhn_ref, dh_ref, dnw_ref = refs[2 * nd:]
        dxn = _dot(du_refs[0][...].astype(BF16), w_refs[0][...], NT)
        for du_ref, w_ref in zip(du_refs[1:], w_refs[1:]):
            dxn += _dot(du_ref[...].astype(BF16), w_ref[...], NT)
        x = h_ref[...]
        r = _rms(x)
        y = x * r
        gy = dxn * nw_ref[...]
        dh_ref[...] = dhn_ref[...] + r * (gy - y * jnp.mean(y * gy, axis=-1, keepdims=True))
        dnw_ref[0] = jnp.sum(dxn * y, axis=0, keepdims=True)

    in_specs = [pl.BlockSpec((tm, du.shape[1]), lambda i: (i, 0)) for du in dus]
    in_specs += [pl.BlockSpec(w.shape, lambda i: (0, 0)) for w in ws]
    in_specs += [pl.BlockSpec((tm, k), lambda i: (i, 0)), pl.BlockSpec((1, k), lambda i: (0, 0)),
                 pl.BlockSpec((tm, k), lambda i: (i, 0))]
    return pl.pallas_call(
        body, grid=(nt,), in_specs=in_specs,
        out_specs=[pl.BlockSpec((tm, k), lambda i: (i, 0)), pl.BlockSpec((1, 1, k), lambda i: (i, 0, 0))],
        out_shape=[SDS((rows, k), F32), SDS((nt, 1, k), F32)],
        compiler_params=_cp("parallel"), name=name)(*dus, *ws, h, nw, dh_next)


def matmul_nt(a, w, name):
    rows, k = a.shape
    n = w.shape[0]
    tm = _row_tile(rows)

    def body(a_ref, w_ref, o_ref):
        o_ref[...] = _dot(a_ref[...].astype(BF16), w_ref[...], NT)

    return pl.pallas_call(
        body, grid=(rows // tm,),
        in_specs=[pl.BlockSpec((tm, k), lambda i: (i, 0)), pl.BlockSpec((n, k), lambda i: (0, 0))],
        out_specs=pl.BlockSpec((tm, n), lambda i: (i, 0)),
        out_shape=SDS((rows, n), F32), compiler_params=_cp("parallel"), name=name)(a, w)


def _attn_masks(n):
    qi = lax.broadcasted_iota(jnp.int32, (BLK, 2 * BLK), 0)
    kj = lax.broadcasted_iota(jnp.int32, (BLK, 2 * BLK), 1)
    dist = BLK + qi - kj
    valid_b = (dist >= 0) & (dist < BLK) & (kj >= 2 * BLK - BLK * n)
    qm = lax.broadcasted_iota(jnp.int32, (BLK, N_META), 0)
    mm = lax.broadcasted_iota(jnp.int32, (BLK, N_META), 1)
    dm = n * BLK + qm - PAD - mm
    return valid_b, dist.astype(F32), dm >= 0, jnp.minimum(dm, BLK).astype(F32)


def _attn_probs(qn16, knb, knm, sink, slope, masks):
    valid_b, dist_b, valid_m, dist_m = masks
    sb = _dot(qn16, knb, NT) * (HD ** -0.5)
    sm = _dot(qn16, knm, NT) * (HD ** -0.5)
    sb = jnp.where(valid_b, sb - slope * dist_b, NEG)
    sm = jnp.where(valid_m, sm - slope * dist_m, NEG)
    mx = jnp.maximum(jnp.maximum(jnp.max(sb, axis=-1, keepdims=True), jnp.max(sm, axis=-1, keepdims=True)), sink)
    eb = jnp.where(valid_b, jnp.exp(sb - mx), 0.0)
    em = jnp.where(valid_m, jnp.exp(sm - mx), 0.0)
    es = jnp.exp(sink - mx)
    inv = 1.0 / (jnp.sum(eb, axis=-1, keepdims=True) + jnp.sum(em, axis=-1, keepdims=True) + es)
    return eb * inv, em * inv, es * inv


def _kv_specs(nblk, clamp):
    cur = (lambda n: (jnp.minimum(n, nblk - 1), 8)) if clamp else (lambda n: (n, 8))
    return [pl.BlockSpec((BLK, KVW), cur),
            pl.BlockSpec((BLK, KVW), lambda n: (jnp.maximum(n - 1, 0), 8)),
            pl.BlockSpec((N_META, KVW), lambda n: (PAD // N_META, 8))]


def attn_fwd(u, qw, kw, sinks):
    rows = u.shape[0]
    nblk = rows // BLK

    def body(q_ref, g_ref, kvc_ref, kvp_ref, kvm_ref, qw_ref, kw_ref, sk_ref, og_ref):
        masks = _attn_masks(pl.program_id(0))
        kvb = jnp.concatenate([kvp_ref[...], kvc_ref[...]], axis=0)
        kvm = kvm_ref[...]
        qw_, kw_ = qw_ref[...], kw_ref[...]
        outs = []
        for kvh in range(2):
            kb, km = kvb[:, HD * kvh:HD * kvh + HD], kvm[:, HD * kvh:HD * kvh + HD]
            knb = (kb * _rms(kb) * kw_).astype(BF16)
            knm = (km * _rms(km) * kw_).astype(BF16)
            vb = kvb[:, BLK + HD * kvh:BLK + HD * kvh + HD].astype(BF16)
            vm = kvm[:, BLK + HD * kvh:BLK + HD * kvh + HD].astype(BF16)
            for g in range(8):
                h = kvh * 8 + g
                qh = q_ref[:, HD * h:HD * h + HD]
                qn16 = (qh * _rms(qh) * qw_).astype(BF16)
                pb, pm, _ = _attn_probs(qn16, knb, knm, sk_ref[:, h:h + 1], SLOPES[h], masks)
                outs.append(_dot(pb.astype(BF16), vb) + _dot(pm.astype(BF16), vm))
        og_ref[...] = (jnp.concatenate(outs, axis=1) * _silu(g_ref[...])).astype(BF16)

    small = lambda w: pl.BlockSpec((1, w), lambda n: (0, 0))
    return pl.pallas_call(
        body, grid=(nblk,),
        in_specs=[pl.BlockSpec((BLK, 1024), lambda n: (n, 0)), pl.BlockSpec((BLK, 1024), lambda n: (n, 1))]
        + _kv_specs(nblk, False) + [small(HD), small(HD), small(HEADS)],
        out_specs=pl.BlockSpec((BLK, 1024), lambda n: (n, 0)),
        out_shape=SDS((rows, 1024), BF16), compiler_params=_cp("parallel"), name="attn_fwd")(
            u, u, u, u, u, qw, kw, sinks)


def attn_bwd(u, qw, kw, sinks, dog):
    rows = u.shape[0]
    nblk = rows // BLK

    def knorm_bwd(k, dkn, kw_):
        r = _rms(k)
        y = k * r
        gy = dkn * kw_
        return r * (gy - y * jnp.mean(y * gy, axis=-1, keepdims=True)), jnp.sum(dkn * y, axis=0, keepdims=True)

    def body(q_ref, g_ref, dog_ref, kvc_ref, kvp_ref, kvm_ref, qw_ref, kw_ref, sk_ref,
             dq_ref, dg_ref, dkv_ref, dkvm_ref, dqw_ref, dkw_ref, dsk_ref, carry, prevp, curp, metap):
        n = pl.program_id(0)
        qw_, kw_ = qw_ref[...], kw_ref[...]

        @pl.when(n == 0)
        def _():
            carry[...] = jnp.zeros_like(carry)
            metap[...] = jnp.zeros_like(metap)
            dqw_ref[...] = jnp.zeros_like(dqw_ref)
            dkw_ref[...] = jnp.zeros_like(dkw_ref)
            dsk_ref[...] = jnp.zeros_like(dsk_ref)

        @pl.when(n == nblk)
        def _():
            prevp[...] = jnp.zeros_like(prevp)
            curp[...] = jnp.zeros_like(curp)

        @pl.when(n < nblk)
        def _():
            masks = _attn_masks(n)
            kvb = jnp.concatenate([kvp_ref[...], kvc_ref[...]], axis=0)
            kvm = kvm_ref[...]
            lane = lax.broadcasted_iota(jnp.int32, (1, HEADS), 1)
            dqs, dgs = [], []
            dqw = jnp.zeros((1, HD), F32)
            dsk = jnp.zeros((1, HEADS), F32)
            band_parts, meta_parts = [None] * 4, [None] * 4
            for kvh in range(2):
                kb, km = kvb[:, HD * kvh:HD * kvh + HD], kvm[:, HD * kvh:HD * kvh + HD]
                knb = (kb * _rms(kb) * kw_).astype(BF16)
                knm = (km * _rms(km) * kw_).astype(BF16)
                vb = kvb[:, BLK + HD * kvh:BLK + HD * kvh + HD].astype(BF16)
                vm = kvm[:, BLK + HD * kvh:BLK + HD * kvh + HD].astype(BF16)
                dkb = jnp.zeros((2 * BLK, HD), F32)
                dvb = jnp.zeros((2 * BLK, HD), F32)
                dkm = jnp.zeros((N_META, HD), F32)
                dvm = jnp.zeros((N_META, HD), F32)
                for g in range(8):
                    h = kvh * 8 + g
                    sl = slice(HD * h, HD * h + HD)
                    qh = q_ref[:, sl]
                    r = _rms(qh)
                    y = qh * r
                    qn16 = (y * qw_).astype(BF16)
                    pb, pm, ps = _attn_probs(qn16, knb, knm, sk_ref[:, h:h + 1], SLOPES[h], masks)
                    pb16, pm16 = pb.astype(BF16), pm.astype(BF16)
                    o = _dot(pb16, vb) + _dot(pm16, vm)
                    gate = g_ref[:, sl]
                    dogh = dog_ref[:, sl]
                    dgs.append(dogh * o * _dsilu(gate))
                    do16 = (dogh * _silu(gate)).astype(BF16)
                    dpb = _dot(do16, vb, NT)
                    dpm = _dot(do16, vm, NT)
                    delta = jnp.sum(pb * dpb, axis=-1, keepdims=True) + jnp.sum(pm * dpm, axis=-1, keepdims=True)
                    dsb = (pb * (dpb - delta)).astype(BF16)
                    dsm = (pm * (dpm - delta)).astype(BF16)
                    dsk += jnp.where(lane == h, jnp.sum(-ps * delta, axis=0, keepdims=True), 0.0)
                    dqn = (_dot(dsb, knb) + _dot(dsm, knm)) * (HD ** -0.5)
                    dkb += _dot(dsb, qn16, TN) * (HD ** -0.5)
                    dkm += _dot(dsm, qn16, TN) * (HD ** -0.5)
                    dvb += _dot(pb16, do16, TN)
                    dvm += _dot(pm16, do16, TN)
                    gy = dqn * qw_
                    dqs.append(r * (gy - y * jnp.mean(y * gy, axis=-1, keepdims=True)))
                    dqw += jnp.sum(dqn * y, axis=0, keepdims=True)
                band_parts[kvh], band_parts[2 + kvh] = dkb, dvb
                meta_parts[kvh], meta_parts[2 + kvh] = dkm, dvm
            dq_ref[...] = jnp.concatenate(dqs, axis=1)
            dg_ref[...] = jnp.concatenate(dgs, axis=1)
            band = jnp.concatenate(band_parts, axis=1)
            prevp[...] = band[:BLK]
            curp[...] = band[BLK:]
            metap[...] += jnp.concatenate(meta_parts, axis=1)
            dqw_ref[...] += dqw
            dsk_ref[...] += dsk

        tot = carry[...] + prevp[...]
        kprev = kvp_ref[...]
        dk0, w0 = knorm_bwd(kprev[:, 0:HD], tot[:, 0:HD], kw_)
        dk1, w1 = knorm_bwd(kprev[:, HD:2 * HD], tot[:, HD:2 * HD], kw_)
        dkv_ref[...] = jnp.concatenate([dk0, dk1, tot[:, 2 * HD:]], axis=1)
        dkw_ref[...] += w0 + w1
        carry[...] = curp[...]

        @pl.when(n == nblk)
        def _():
            mt = metap[...]
            km = kvm_ref[...]
            m0, v0 = knorm_bwd(km[:, 0:HD], mt[:, 0:HD], kw_)
            m1, v1 = knorm_bwd(km[:, HD:2 * HD], mt[:, HD:2 * HD], kw_)
            dkvm_ref[...] = jnp.concatenate([m0, m1, mt[:, 2 * HD:]], axis=1)
            dkw_ref[...] += v0 + v1

    small = lambda w: pl.BlockSpec((1, w), lambda n: (0, 0))
    cl = lambda n: jnp.minimum(n, nblk - 1)
    return pl.pallas_call(
        body, grid=(nblk + 1,),
        in_specs=[pl.BlockSpec((BLK, 1024), lambda n: (cl(n), 0)), pl.BlockSpec((BLK, 1024), lambda n: (cl(n), 1)),
                  pl.BlockSpec((BLK, 1024), lambda n: (cl(n), 0))]
        + _kv_specs(nblk, True) + [small(HD), small(HD), small(HEADS)],
        out_specs=[pl.BlockSpec((BLK, 1024), lambda n: (cl(n), 0)), pl.BlockSpec((BLK, 1024), lambda n: (cl(n), 0)),
                   pl.BlockSpec((BLK, KVW), lambda n: (jnp.maximum(n - 1, 0), 0)),
                   pl.BlockSpec((N_META, KVW), lambda n: (0, 0)), small(HD), small(HD), small(HEADS)],
        out_shape=[SDS((rows, 1024), F32), SDS((rows, 1024), F32), SDS((rows, KVW), F32), SDS((N_META, KVW), F32),
                   SDS((1, HD), F32), SDS((1, HD), F32), SDS((1, HEADS), F32)],
        scratch_shapes=[pltpu.VMEM((BLK, KVW), F32), pltpu.VMEM((BLK, KVW), F32), pltpu.VMEM((BLK, KVW), F32),
                        pltpu.VMEM((N_META, KVW), F32)],
        compiler_params=_cp("arbitrary"), name="attn_bwd")(u, u, dog, u, u, u, qw, kw, sinks)


def _tri_inv(m, ii, jj):
    eye = (ii == jj).astype(F32)
    mb = jnp.where((ii >> 3) == (jj >> 3), m, 0.0)
    m2 = _dot(mb, mb, precision=HI)
    m4 = _dot(m2, m2, precision=HI)
    x = _dot(_dot(eye - mb, eye + m2, precision=HI), eye + m4, precision=HI)
    for sh in (3, 4, 5):
        lb = jnp.where(((ii >> (sh + 1)) == (jj >> (sh + 1))) & ((ii >> sh) != (jj >> sh)), m, 0.0)
        x = x - _dot(_dot(x, lb, precision=HI), x, precision=HI)
    return x


def _conv_group(xc_ref, xp_ref, cw_ref, off, first):
    xp = jnp.where(first, 0.0, xp_ref[:, pl.ds(off, DK)])
    xx = jnp.concatenate([xp, xc_ref[:, pl.ds(off, DK)]], axis=0)
    y = cw_ref[0:1, pl.ds(off, DK)] * xx[5:5 + CH]
    for j in range(1, 4):
        y += cw_ref[j:j + 1, pl.ds(off, DK)] * xx[5 + j:5 + j + CH]
    return xx, y


def _gates(ba, al, dtb, c):
    row = c * CH + lax.broadcasted_iota(jnp.int32, (CH, DN_H), 0)
    real = row >= PAD
    xa = ba[:, DN_H:2 * DN_H] + dtb
    beta = jnp.where(real, jax.nn.sigmoid(ba[:, 0:DN_H]), 0.0)
    g = jnp.where(real, -jnp.exp(al) * jax.nn.softplus(xa), 0.0)
    return real, xa, beta, g


def _pick(x, sel):
    return jnp.sum(jnp.where(sel, x, 0.0), axis=1, keepdims=True)


def _chunk_specs(width_blocks):
    return [pl.BlockSpec((CH, 4096), lambda c: (c, 0)),
            pl.BlockSpec((8, 4096), lambda c: (jnp.maximum(8 * c - 1, 0), 0)),
            pl.BlockSpec((CH, DK), lambda c: (c, 48))]


def dn_prep(udn, conv_w, a_log, dt_bias):
    rows = udn.shape[0]
    nch = rows // CH

    def body(xc_ref, xp_ref, ba_ref, cw_ref, al_ref, dtb_ref,
             qn_ref, kn_ref, sv_ref, gc_ref, beta_ref, u_ref, w_ref, qe_ref, ks_ref, p_ref, a_ref, gct):
        c = pl.program_id(0)
        first = c == 0
        _, _, beta, g = _gates(ba_ref[...], al_ref[...], dtb_ref[...], c)
        ii = lax.broadcasted_iota(jnp.int32, (CH, CH), 0)
        jj = lax.broadcasted_iota(jnp.int32, (CH, CH), 1)
        gc = _dot((ii >= jj).astype(F32), g, precision=HI)
        gc_ref[...] = gc
        beta_ref[...] = beta
        gct[...] = gc.T

        def qk_body(kh, carry):
            off = pl.multiple_of(kh * DK, DK)
            _, yq = _conv_group(xc_ref, xp_ref, cw_ref, off, first)
            sq = _silu(yq)
            qn_ref[:, pl.ds(off, DK)] = sq * lax.rsqrt(jnp.sum(sq * sq, axis=-1, keepdims=True) + EPS) * (DK ** -0.5)
            _, yk = _conv_group(xc_ref, xp_ref, cw_ref, pl.multiple_of(1024 + kh * DK, DK), first)
            sk = _silu(yk)
            kn_ref[:, pl.ds(off, DK)] = sk * lax.rsqrt(jnp.sum(sk * sk, axis=-1, keepdims=True) + EPS)
            return carry

        lax.fori_loop(0, DN_KH, qk_body, 0)
        lane = lax.broadcasted_iota(jnp.int32, (CH, DN_H), 1)
        zpad = jnp.zeros((CH, DK - CH), F32)

        def v_body(hv, carry):
            off = pl.multiple_of(hv * DK, DK)
            koff = pl.multiple_of((hv // 2) * DK, DK)
            _, yv = _conv_group(xc_ref, xp_ref, cw_ref, pl.multiple_of(2048 + hv * DK, DK), first)
            v = _silu(yv)
            sv_ref[:, pl.ds(off, DK)] = v
            k = kn_ref[:, pl.ds(koff, DK)]
            q = qn_ref[:, pl.ds(koff, DK)]
            sel = lane == hv
            beta_c, gc_c = _pick(beta, sel), _pick(gc, sel)
            dec = jnp.exp(jnp.where(ii >= jj, gc_c - gct[pl.ds(hv, 1), :], NEG))
            eg = jnp.exp(gc_c)
            kb = k * beta_c
            k16 = k.astype(BF16)
            m = jnp.where(ii > jj, _dot(kb.astype(BF16), k16, NT) * dec, 0.0)
            a = _tri_inv(m, ii, jj)
            u_ref[:, pl.ds(off, DK)] = _dot(a, v * beta_c, precision=HI)
            w_ref[:, pl.ds(off, DK)] = _dot(a, kb * eg, precision=HI)
            p = _dot(q.astype(BF16), k16, NT) * dec
            qe_ref[:, pl.ds(off, DK)] = (q * eg).astype(BF16)
            ks_ref[:, pl.ds(off, DK)] = (k * jnp.exp(gc_c[CH - 1:CH, :] - gc_c)).astype(BF16)
            p_ref[:, pl.ds(off, DK)] = jnp.concatenate([p, zpad], axis=1).astype(BF16)
            a_ref[:, pl.ds(off, DK)] = jnp.concatenate([a, zpad], axis=1)
            return carry

        lax.fori_loop(0, DN_H, v_body, 0)

    full = lambda shape: pl.BlockSpec(shape, lambda c: (0, 0))
    blk = lambda w: pl.BlockSpec((CH, w), lambda c: (c, 0))
    return pl.pallas_call(
        body, grid=(nch,),
        in_specs=_chunk_specs(0) + [full((4, 4096)), full((1, DN_H)), full((1, DN_H))],
        out_specs=[blk(1024), blk(1024), blk(2048), blk(DN_H), blk(DN_H), blk(2048), blk(2048), blk(2048), blk(2048),
                   blk(2048), blk(2048)],
        out_shape=[SDS((rows, 1024), F32), SDS((rows, 1024), F32), SDS((rows, 2048), F32), SDS((rows, DN_H), F32),
                   SDS((rows, DN_H), F32), SDS((rows, 2048), F32), SDS((rows, 2048), F32), SDS((rows, 2048), BF16),
                   SDS((rows, 2048), BF16), SDS((rows, 2048), BF16), SDS((rows, 2048), F32)],
        scratch_shapes=[pltpu.VMEM((DN_H, CH), F32)],
        compiler_params=_cp("parallel"), name="dn_prep")(udn, udn, udn, conv_w, a_log, dt_bias)


def dn_scan(u, w, qe, ks, p, gc):
    rows = u.shape[0]
    nch = rows // CH

    def body(u_ref, w_ref, qe_ref, ks_ref, p_ref, gc_ref, o_ref, vn_ref, st_ref, s_scr):
        @pl.when(pl.program_id(0) == 0)
        def _():
            s_scr[...] = jnp.zeros_like(s_scr)

        gl_row = gc_ref[CH - 1:CH, :]
        lane = lax.broadcasted_iota(jnp.int32, (1, DN_H), 1)

        def head(hv, carry):
            sl = pl.ds(pl.multiple_of(hv * DK, DK), DK)
            s = s_scr[hv]
            st_ref[0, hv] = s
            s16 = s.astype(BF16)
            vn = u_ref[:, sl] - _dot(w_ref[:, sl].astype(BF16), s16)
            vn16 = vn.astype(BF16)
            vn_ref[:, sl] = vn16
            o_ref[:, sl] = _dot(qe_ref[:, sl], s16) + _dot(p_ref[:, sl][:, 0:CH], vn16)
            egl = jnp.exp(_pick(gl_row, lane == hv))
            s_scr[hv] = s * egl + _dot(ks_ref[:, sl], vn16, TN)
            return carry

        lax.fori_loop(0, DN_H, head, 0)

    blk = lambda wd: pl.BlockSpec((CH, wd), lambda c: (c, 0))
    return pl.pallas_call(
        body, grid=(nch,),
        in_specs=[blk(2048)] * 5 + [blk(DN_H)],
        out_specs=[blk(2048), blk(2048), pl.BlockSpec((1, DN_H, DK, DK), lambda c: (c, 0, 0, 0))],
        out_shape=[SDS((rows, 2048), F32), SDS((rows, 2048), BF16), SDS((nch, DN_H, DK, DK), F32)],
        scratch_shapes=[pltpu.VMEM((DN_H, DK, DK), F32)],
        compiler_params=_cp("arbitrary"), name="dn_scan")(u, w, qe, ks, p, gc)


def dn_out_fwd(o, udn, ow, wout, h1, tgt):
    rows = o.shape[0]
    tm = _row_tile(rows)
    nt = rows // tm

    def body(o_ref, z_ref, ow_ref, w_ref, h_ref, t_ref, dh_ref, on_ref, ls_ref):
        for hv in range(DN_H):
            sl = slice(hv * DK, hv * DK + DK)
            oh = o_ref[:, sl]
            on_ref[:, sl] = (oh * _rms(oh) * ow_ref[...] * _silu(z_ref[:, sl])).astype(BF16)
        h2 = h_ref[...] + _dot(on_ref[...], w_ref[...])
        row = pl.program_id(0) * tm + lax.broadcasted_iota(jnp.int32, (tm, 1), 0)
        err = jnp.where(row >= BLK, h2 - t_ref[...], 0.0)
        dh_ref[...] = err * (1.0 / D_MODEL)
        ls_ref[0] = jnp.sum(err * err, axis=0, keepdims=True)

    return pl.pallas_call(
        body, grid=(nt,),
        in_specs=[pl.BlockSpec((tm, 2048), lambda i: (i, 0)), pl.BlockSpec((tm, 2048), lambda i: (i, 2)),
                  pl.BlockSpec((1, DK), lambda i: (0, 0)), pl.BlockSpec((2048, D_MODEL), lambda i: (0, 0)),
                  pl.BlockSpec((tm, D_MODEL), lambda i: (i, 0)), pl.BlockSpec((tm, D_MODEL), lambda i: (i, 0))],
        out_specs=[pl.BlockSpec((tm, D_MODEL), lambda i: (i, 0)), pl.BlockSpec((tm, 2048), lambda i: (i, 0)),
                   pl.BlockSpec((1, 1, D_MODEL), lambda i: (i, 0, 0))],
        out_shape=[SDS((rows, D_MODEL), F32), SDS((rows, 2048), BF16), SDS((nt, 1, D_MODEL), F32)],
        compiler_params=_cp("parallel"), name="dn_out_fwd")(o, udn, ow, wout, h1, tgt)


def dn_out_bwd(dh2, wout, o, udn, ow):
    rows = o.shape[0]
    tm = _row_tile(rows)
    nt = rows // tm

    def body(dh_ref, w_ref, o_ref, z_ref, ow_ref, do_ref, dz_ref, dow_ref):
        don = _dot(dh_ref[...].astype(BF16), w_ref[...], NT)
        ow_ = ow_ref[...]
        dow = jnp.zeros((1, DK), F32)
        for hv in range(DN_H):
            sl = slice(hv * DK, hv * DK + DK)
            oh = o_ref[:, sl]
            r = _rms(oh)
            y = oh * r
            z = z_ref[:, sl]
            dn = don[:, sl] * _silu(z)
            dz_ref[:, sl] = don[:, sl] * (y * ow_) * _dsilu(z)
            dy = dn * ow_
            do_ref[:, sl] = r * (dy - y * jnp.mean(y * dy, axis=-1, keepdims=True))
            dow += jnp.sum(dn * y, axis=0, keepdims=True)
        dow_ref[0] = dow

    return pl.pallas_call(
        body, grid=(nt,),
        in_specs=[pl.BlockSpec((tm, D_MODEL), lambda i: (i, 0)), pl.BlockSpec((2048, D_MODEL), lambda i: (0, 0)),
                  pl.BlockSpec((tm, 2048), lambda i: (i, 0)), pl.BlockSpec((tm, 2048), lambda i: (i, 2)),
                  pl.BlockSpec((1, DK), lambda i: (0, 0))],
        out_specs=[pl.BlockSpec((tm, 2048), lambda i: (i, 0)), pl.BlockSpec((tm, 2048), lambda i: (i, 0)),
                   pl.BlockSpec((1, 1, DK), lambda i: (i, 0, 0))],
        out_shape=[SDS((rows, 2048), F32), SDS((rows, 2048), F32), SDS((nt, 1, DK), F32)],
        compiler_params=_cp("parallel"), name="dn_out_bwd")(dh2, wout, o, udn, ow)


def dn_scan_bwd(do, qn, kn, sv, gc, beta, a, u, w, vn, qe, ks, st):
    rows = do.shape[0]
    nch = rows // CH

    def body(do_ref, q_ref, k_ref, v_ref, gc_ref, beta_ref, a_ref, u_ref, w_ref, vn_ref, qe_ref, ks_ref, st_ref,
             dq_ref, dk_ref, dv_ref, dbeta_ref, dg_ref, ds_scr, gct):
        @pl.when(pl.program_id(0) == 0)
        def _():
            ds_scr[...] = jnp.zeros_like(ds_scr)

        gc, beta = gc_ref[...], beta_ref[...]
        gct[...] = gc.T
        ii = lax.broadcasted_iota(jnp.int32, (CH, CH), 0)
        jj = lax.broadcasted_iota(jnp.int32, (CH, CH), 1)
        lane = lax.broadcasted_iota(jnp.int32, (CH, DN_H), 1)
        last = lax.broadcasted_iota(jnp.int32, (CH, 1), 0) == CH - 1
        ones = jnp.ones((CH, DK), F32)

        def kh_body(kh, carry):
            dbeta_acc, dgc_acc = carry
            ksl = pl.ds(pl.multiple_of(kh * DK, DK), DK)
            k, q = k_ref[:, ksl], q_ref[:, ksl]
            k16, q16 = k.astype(BF16), q.astype(BF16)
            qk = _dot(q16, k16, NT)
            dq_tot = jnp.zeros((CH, DK), F32)
            dk_tot = jnp.zeros((CH, DK), F32)
            for sub in range(2):
                hv = 2 * kh + sub
                sl = pl.ds(pl.multiple_of(hv * DK, DK), DK)
                sel = lane == hv
                beta_c, gc_c = _pick(beta, sel), _pick(gc, sel)
                dec = jnp.exp(jnp.where(ii >= jj, gc_c - gct[pl.ds(hv, 1), :], NEG))
                eg = jnp.exp(gc_c)
                gl = gc_c[CH - 1:CH, :]
                e2 = jnp.exp(gl - gc_c)
                egl = jnp.exp(gl)
                v = v_ref[:, sl]
                do16 = do_ref[:, sl].astype(BF16)
                s = st_ref[0, hv]
                s16 = s.astype(BF16)
                dso = ds_scr[hv]
                dso16 = dso.astype(BF16)
                wf = w_ref[:, sl]
                w16 = wf.astype(BF16)
                uf = u_ref[:, sl]
                vn16 = vn_ref[:, sl]
                kb = k * beta_c
                kb16 = kb.astype(BF16)
                pm = qk * dec
                m = jnp.where(ii > jj, _dot(kb16, k16, NT) * dec, 0.0)
                dvn = _dot(pm.astype(BF16), do16, TN) + _dot(ks_ref[:, sl], dso16)
                dvn16 = dvn.astype(BF16)
                ds_scr[hv] = egl * dso + _dot(qe_ref[:, sl], do16, TN) - _dot(w16, dvn16, TN)
                dpm = jnp.where(ii >= jj, _dot(do16, vn16, NT), 0.0)
                dqk16 = (dpm * dec).astype(BF16)
                dqe = _dot(do16, s16, NT)
                dq_tot += eg * dqe + _dot(dqk16, k16)
                dk = _dot(dqk16, q16, TN)
                deg = jnp.sum(q * dqe, axis=-1, keepdims=True)
                dks = _dot(vn16, dso16, NT)
                dk += e2 * dks
                t = jnp.sum(dks * k, axis=-1, keepdims=True) * e2
                dgl = jnp.sum(t, axis=0, keepdims=True) + egl * jnp.sum(
                    jnp.sum(dso * s, axis=1, keepdims=True), axis=0, keepdims=True)
                dw = -_dot(dvn16, s16, NT)
                am = a_ref[:, sl][:, 0:CH]
                dbv = _dot(am, dvn, TN, precision=HI)
                dbk = _dot(am, dw, TN, precision=HI)
                dm = jnp.where(ii > jj, -(_dot(dbv, uf, NT, precision=HI) + _dot(dbk, wf, NT, precision=HI)), 0.0)
                g16 = (dm * dec).astype(BF16)
                dkb = _dot(g16, k16)
                dk += _dot(g16, kb16, TN)
                e = dpm * pm + dm * m
                dgc = jnp.sum(e, axis=1, keepdims=True) - _dot(e, ones, TN, precision=HI)[:, 0:1] - t
                dv_ref[:, sl] = beta_c * dbv
                dbeta = (jnp.sum(dbv * v, axis=-1, keepdims=True) + jnp.sum(dbk * k, axis=-1, keepdims=True) * eg
                         + jnp.sum(dkb * k, axis=-1, keepdims=True))
                dk += beta_c * (eg * dbk + dkb)
                deg += jnp.sum(dbk * kb, axis=-1, keepdims=True)
                dgc += deg * eg + jnp.where(last, dgl, 0.0)
                dk_tot += dk
                dbeta_acc += jnp.where(sel, dbeta, 0.0)
                dgc_acc += jnp.where(sel, dgc, 0.0)
            dq_ref[:, ksl] = dq_tot
            dk_ref[:, ksl] = dk_tot
            return dbeta_acc, dgc_acc

        zero = jnp.zeros((CH, DN_H), F32)
        dbeta_acc, dgc_acc = lax.fori_loop(0, DN_KH, kh_body, (zero, zero))
        dbeta_ref[...] = dbeta_acc
        dg_ref[...] = _dot((ii <= jj).astype(F32), dgc_acc, precision=HI)

    rev = lambda wd: pl.BlockSpec((CH, wd), lambda i: (nch - 1 - i, 0))
    return pl.pallas_call(
        body, grid=(nch,),
        in_specs=[rev(2048), rev(1024), rev(1024), rev(2048), rev(DN_H), rev(DN_H), rev(2048), rev(2048), rev(2048),
                  rev(2048), rev(2048), rev(2048), pl.BlockSpec((1, DN_H, DK, DK), lambda i: (nch - 1 - i, 0, 0, 0))],
        out_specs=[rev(1024), rev(1024), rev(2048), rev(DN_H), rev(DN_H)],
        out_shape=[SDS((rows, 1024), F32), SDS((rows, 1024), F32), SDS((rows, 2048), F32), SDS((rows, DN_H), F32),
                   SDS((rows, DN_H), F32)],
        scratch_shapes=[pltpu.VMEM((DN_H, DK, DK), F32), pltpu.VMEM((DN_H, CH), F32)],
        compiler_params=_cp("arbitrary"), name="dn_scan_bwd")(do, qn, kn, sv, gc, beta, a, u, w, vn, qe, ks, st)


def dn_prep_bwd(udn, conv_w, a_log, dt_bias, dqn, dkn, dv, dbeta, dg):
    rows = udn.shape[0]
    nch = rows // CH

    def body(xc_ref, xp_ref, ba_ref, cw_ref, al_ref, dtb_ref, dqn_ref, dkn_ref, dv_ref, dbeta_ref, dg_ref,
             dy_ref, dba_ref, dcw_ref, dal_ref, ddtb_ref):
        c = pl.program_id(0)
        first = c == 0

        @pl.when(first)
        def _():
            dcw_ref[...] = jnp.zeros_like(dcw_ref)
            dal_ref[...] = jnp.zeros_like(dal_ref)
            ddtb_ref[...] = jnp.zeros_like(ddtb_ref)

        real, xa, beta, g = _gates(ba_ref[...], al_ref[...], dtb_ref[...], c)
        dgm = jnp.where(real, dg_ref[...], 0.0)
        da = dgm * (-jnp.exp(al_ref[...])) * jax.nn.sigmoid(xa)
        dal_ref[...] += jnp.sum(dgm * g, axis=0, keepdims=True)
        ddtb_ref[...] += jnp.sum(da, axis=0, keepdims=True)
        dba_ref[...] = jnp.zeros_like(dba_ref)
        dba_ref[:, 0:DN_H] = jnp.where(real, dbeta_ref[...] * beta * (1.0 - beta), 0.0)
        dba_ref[:, DN_H:2 * DN_H] = da

        def through_conv(off, grad_fn):
            xx, y = _conv_group(xc_ref, xp_ref, cw_ref, off, first)
            dy = grad_fn(_silu(y)) * _dsilu(y)
            dy_ref[:, pl.ds(off, DK)] = dy
            for j in range(4):
                dcw_ref[j:j + 1, pl.ds(off, DK)] += jnp.sum(dy * xx[5 + j:5 + j + CH], axis=0, keepdims=True)

        def l2_bwd(gin):
            def f(s):
                r = lax.rsqrt(jnp.sum(s * s, axis=-1, keepdims=True) + EPS)
                nrm = s * r
                return r * (gin - nrm * jnp.sum(nrm * gin, axis=-1, keepdims=True))
            return f

        def qk_body(kh, carry):
            off = pl.multiple_of(kh * DK, DK)
            through_conv(off, l2_bwd(dqn_ref[:, pl.ds(off, DK)] * (DK ** -0.5)))
            through_conv(pl.multiple_of(1024 + kh * DK, DK), l2_bwd(dkn_ref[:, pl.ds(off, DK)]))
            return carry

        lax.fori_loop(0, DN_KH, qk_body, 0)

        def v_body(hv, carry):
            dvv = dv_ref[:, pl.ds(pl.multiple_of(hv * DK, DK), DK)]
            through_conv(pl.multiple_of(2048 + hv * DK, DK), lambda s: dvv)
            return carry

        lax.fori_loop(0, DN_H, v_body, 0)

    full = lambda shape: pl.BlockSpec(shape, lambda c: (0, 0))
    blk = lambda w: pl.BlockSpec((CH, w), lambda c: (c, 0))
    return pl.pallas_call(
        body, grid=(nch,),
        in_specs=_chunk_specs(0) + [full((4, 4096)), full((1, DN_H)), full((1, DN_H)),
                                    blk(1024), blk(1024), blk(2048), blk(DN_H), blk(DN_H)],
        out_specs=[blk(4096), blk(DK), full((8, 4096)), full((1, DN_H)), full((1, DN_H))],
        out_shape=[SDS((rows, 4096), F32), SDS((rows, DK), F32), SDS((8, 4096), F32), SDS((1, DN_H), F32),
                   SDS((1, DN_H), F32)],
        compiler_params=_cp("arbitrary"), name="dn_prep_bwd")(
            udn, udn, udn, conv_w, a_log, dt_bias, dqn, dkn, dv, dbeta, dg)


def conv_bwd(dy, conv_w):
    rows = dy.shape[0]
    nch = rows // CH

    def body(dc_ref, dn_ref, cw_ref, dx_ref):
        nxt = jnp.where(pl.program_id(0) == nch - 1, 0.0, dn_ref[...])
        xx = jnp.concatenate([dc_ref[...], nxt], axis=0)
        dx = cw_ref[0:1, :] * xx[3:3 + CH]
        for j in range(1, 4):
            dx += cw_ref[j:j + 1, :] * xx[3 - j:3 - j + CH]
        dx_ref[...] = dx

    return pl.pallas_call(
        body, grid=(nch, 4),
        in_specs=[pl.BlockSpec((CH, 1024), lambda c, j: (c, j)),
                  pl.BlockSpec((8, 1024), lambda c, j: (jnp.minimum(8 * c + 8, rows // 8 - 1), j)),
                  pl.BlockSpec((4, 1024), lambda c, j: (0, j))],
        out_specs=pl.BlockSpec((CH, 1024), lambda c, j: (c, j)),
        out_shape=SDS((rows, 4096), F32), compiler_params=_cp("parallel", "parallel"), name="conv_bwd")(dy, dy, conv_w)


def local_step(x, target, w):
    seq = x.shape[0]
    bf = lambda a: a.astype(BF16)
    h0 = jnp.concatenate([jnp.zeros((PAD, D_MODEL), F32), w["meta_tokens"], x], axis=0)
    tgt = jnp.concatenate([jnp.zeros((BLK, D_MODEL), F32), target], axis=0)
    win = w["attn_w_in"]
    wq, wkv, wg = win[:, :1024], win[:, 1024:1280], win[:, 1280:]
    wa_in = bf(jnp.concatenate([wq, wg, wkv], axis=1))
    wa_out = bf(w["attn_w_out"])
    wd_in = bf(jnp.concatenate([w["dn_w_in"], jnp.zeros((D_MODEL, 96), F32)], axis=1))
    wd_out = bf(w["dn_w_out"])
    qw, kw, sinks = w["attn_q_norm_w"], w["attn_k_norm_w"], w["attn_sinks"]
    cw, al, dtb, ow = w["dn_conv_w"], w["dn_a_log"], w["dn_dt_bias"], w["dn_o_norm_w"]

    ua, xn0 = norm_matmul(h0, w["attn_norm_w"], wa_in, 2304, "attn_in")
    og = attn_fwd(ua, qw, kw, sinks)
    h1 = matmul_residual(og, wa_out, h0, "attn_out")
    ud, xn1 = norm_matmul(h1, w["dn_norm_w"], wd_in, 896, "dn_in")
    qn, kn, sv, gc, beta, u, wy, qe, ks, p, a = dn_prep(ud, cw, al, dtb)
    o, vn, st = dn_scan(u, wy, qe, ks, p, gc)
    dh2, on, ls = dn_out_fwd(o, ud, ow, wd_out, h1, tgt)
    loss = (0.5 / D_MODEL) * jnp.sum(ls)

    do, dz, dow = dn_out_bwd(dh2, wd_out, o, ud, ow)
    g_dn_out = wgrad(on, dh2, "dn_out_wgrad")
    dqn, dkn, dv, dbeta, dg = dn_scan_bwd(do, qn, kn, sv, gc, beta, a, u, wy, vn, qe, ks, st)
    dy, dba, dcw, dal, ddtb = dn_prep_bwd(ud, cw, al, dtb, dqn, dkn, dv, dbeta, dg)
    dxc = conv_bwd(dy, cw)
    dh1, dnw1 = in_proj_bwd([dxc, dz, dba], [wd_in[:, :4096], wd_in[:, 4096:6144], wd_in[:, 6144:]],
                            h1, w["dn_norm_w"], dh2, "dn_in_bwd")
    g_dn_in = jnp.concatenate([wgrad(xn1, dxc, "dn_in_wgrad_qkv"), wgrad(xn1, dz, "dn_in_wgrad_z"),
                               wgrad(xn1, dba, "dn_in_wgrad_ba")[:, :2 * DN_H]], axis=1)

    dog = matmul_nt(dh1, wa_out, "attn_out_bwd")
    g_attn_out = wgrad(og, dh1, "attn_out_wgrad")
    dq, dgate, dkv, dkvm, dqw, dkw, dsk = attn_bwd(ua, qw, kw, sinks, dog)
    dkv = dkv.at[PAD:BLK].add(dkvm)
    dh0, dnw0 = in_proj_bwd([dq, dgate, dkv], [wa_in[:, :1024], wa_in[:, 1024:2048], wa_in[:, 2048:]],
                            h0, w["attn_norm_w"], dh1, "attn_in_bwd")
    g_attn_in = jnp.concatenate([wgrad(xn0, dq, "attn_in_wgrad_q"), wgrad(xn0, dkv, "attn_in_wgrad_kv"),
                                 wgrad(xn0, dgate, "attn_in_wgrad_g")], axis=1)
    grads = {
        "meta_tokens": dh0[PAD:BLK], "attn_norm_w": jnp.sum(dnw0, axis=0), "attn_w_in": g_attn_in,
        "attn_q_norm_w": dqw, "attn_k_norm_w": dkw, "attn_sinks": dsk, "attn_w_out": g_attn_out,
        "dn_norm_w": jnp.sum(dnw1, axis=0), "dn_w_in": g_dn_in, "dn_conv_w": dcw[:4], "dn_a_log": dal,
        "dn_dt_bias": ddtb, "dn_o_norm_w": jnp.sum(dow, axis=0), "dn_w_out": g_dn_out,
    }
    return loss, dh0[BLK:BLK + seq], grads


WEIGHTS = ["meta_tokens", "attn_norm_w", "attn_w_in", "attn_q_norm_w", "attn_k_norm_w", "attn_sinks", "attn_w_out",
           "dn_norm_w", "dn_w_in", "dn_conv_w", "dn_a_log", "dn_dt_bias", "dn_o_norm_w", "dn_w_out"]
SHARDED = {"attn_w_in": ((1024, 2304), 1), "attn_w_out": ((1024, 1024), 0), "dn_w_in": ((1024, 6176), 1),
           "dn_w_out": ((2048, 1024), 0), "dn_conv_w": ((4, 4096), 1), "meta_tokens": ((16, 1024), 1),
           "dn_norm_w": ((1, 1024), 1)}
REPLICATED = {"attn_norm_w": 1024, "attn_q_norm_w": 64, "attn_k_norm_w": 64, "attn_sinks": 16, "dn_a_log": 16,
              "dn_dt_bias": 16, "dn_o_norm_w": 128}
N_CHIPS = 4
PACK_ROWS = 2912
HALF_ROWS = PACK_ROWS // 2
SMALL_ROWS = 8


def _shard_shape(name):
    (r, c), axis = SHARDED[name]
    return (r // N_CHIPS, c) if axis == 0 else (r, c // N_CHIPS)


def _pack(parts, rows):
    flat = jnp.concatenate([p.reshape(-1) for p in parts])
    return jnp.pad(flat, (0, rows * 1024 - flat.shape[0])).reshape(rows, 1024)


def pack_shard(shards):
    return _pack([shards[n] for n in SHARDED], PACK_ROWS)


def unpack_shard(buf):
    flat, out, pos = buf.reshape(-1), {}, 0
    for n in SHARDED:
        shp = _shard_shape(n)
        size = shp[0] * shp[1]
        out[n] = flat[pos:pos + size].reshape(shp)
        pos += size
    return out


def pack_small(vals):
    return _pack([vals[n] for n in REPLICATED], SMALL_ROWS)


def unpack_small(buf):
    flat, out, pos = buf.reshape(-1), {}, 0
    for n, size in REPLICATED.items():
        out[n] = flat[pos:pos + size].reshape(1, size)
        pos += size
    return out


ANY = pl.BlockSpec(memory_space=pl.ANY)


def _place():
    return lax.axis_index("x"), lax.axis_index("y"), lax.axis_index("c")


def chips_exchange(src, gather):
    r = src.shape[-2]

    def body(s_ref, o_ref, send_sems, recv_sems, local_sem):
        x, y, c = _place()
        me = 2 * x + y
        peers = [(1 - x, y), (x, 1 - y), (1 - x, 1 - y)]
        mine = pltpu.make_async_copy(s_ref if gather else s_ref.at[me], o_ref.at[me], local_sem)
        mine.start()

        def copy(k, to_block, from_block):
            px, py = peers[k]
            return pltpu.make_async_remote_copy(
                src_ref=s_ref if gather else s_ref.at[to_block], dst_ref=o_ref.at[from_block],
                send_sem=send_sems.at[k], recv_sem=recv_sems.at[k], device_id=(px, py, c), device_id_type=MESH)

        sends = [copy(k, 2 * px + py, me) for k, (px, py) in enumerate(peers)]
        for cp in sends:
            cp.start()
        for k, (px, py) in enumerate(peers):
            copy(k, me, 2 * px + py).wait_recv()
        for cp in sends:
            cp.wait_send()
        mine.wait()

    return pl.pallas_call(
        body, in_specs=[ANY], out_specs=ANY, out_shape=SDS((N_CHIPS, r, 1024), F32),
        scratch_shapes=[pltpu.SemaphoreType.DMA((3,)), pltpu.SemaphoreType.DMA((3,)), pltpu.SemaphoreType.DMA],
        name="chips_gather" if gather else "chips_exchange")(src)


def sibling_exchange(src, name):
    def body(s_ref, o_ref, send_sem, recv_sem):
        x, y, c = _place()
        cp = pltpu.make_async_remote_copy(src_ref=s_ref, dst_ref=o_ref, send_sem=send_sem, recv_sem=recv_sem,
                                          device_id=(x, y, 1 - c), device_id_type=MESH)
        cp.start()
        cp.wait()

    return pl.pallas_call(
        body, in_specs=[ANY], out_specs=ANY, out_shape=SDS(src.shape, src.dtype),
        scratch_shapes=[pltpu.SemaphoreType.DMA, pltpu.SemaphoreType.DMA], name=name)(src)


def all_gather_small(src):
    def body(s_ref, o_ref, send_sems, recv_sems, local_sem):
        x, y, c = _place()
        flips = [(fx, fy, fc) for fx in (0, 1) for fy in (0, 1) for fc in (0, 1)][1:]
        idx = lambda px, py, pc: 4 * px + 2 * py + pc
        mine = pltpu.make_async_copy(s_ref, o_ref.at[idx(x, y, c)], local_sem)
        mine.start()

        def peer(k):
            fx, fy, fc = flips[k]
            return (1 - x if fx else x, 1 - y if fy else y, 1 - c if fc else c)

        def copy(k, block):
            return pltpu.make_async_remote_copy(
                src_ref=s_ref, dst_ref=o_ref.at[block], send_sem=send_sems.at[k], recv_sem=recv_sems.at[k],
                device_id=peer(k), device_id_type=MESH)

        sends = [copy(k, idx(x, y, c)) for k in range(7)]
        for cp in sends:
            cp.start()
        for k in range(7):
            copy(k, idx(*peer(k))).wait_recv()
        for cp in sends:
            cp.wait_send()
        mine.wait()

    return pl.pallas_call(
        body, in_specs=[ANY], out_specs=ANY, out_shape=SDS((8,) + src.shape, F32),
        scratch_shapes=[pltpu.SemaphoreType.DMA((7,)), pltpu.SemaphoreType.DMA((7,)), pltpu.SemaphoreType.DMA],
        name="all_gather_small")(src)


def sum_blocks(t, name):
    n, r, _ = t.shape
    tm = 208 if r % 208 == 0 else r

    def body(t_ref, o_ref):
        acc = t_ref[0]
        for i in range(1, n):
            acc = acc + t_ref[i]
        o_ref[...] = acc

    return pl.pallas_call(
        body, grid=(r // tm,), in_specs=[pl.BlockSpec((n, tm, 1024), lambda i: (0, i, 0))],
        out_specs=pl.BlockSpec((tm, 1024), lambda i: (i, 0)), out_shape=SDS((r, 1024), F32),
        compiler_params=_cp("parallel"), name=name)(t)


def adamw(w, g, m, v):
    rows = w.shape[0]
    tm = rows // 5 if rows % 40 == 0 else rows

    def body(w_ref, g_ref, m_ref, v_ref, d_ref, nm_ref, nv_ref):
        g_ = g_ref[...]
        m_ = ADAM_B1 * m_ref[...] + (1.0 - ADAM_B1) * g_
        v_ = ADAM_B2 * v_ref[...] + (1.0 - ADAM_B2) * (g_ * g_)
        m_hat = m_ / (1.0 - ADAM_B1 ** ADAM_STEP)
        v_hat = v_ / (1.0 - ADAM_B2 ** ADAM_STEP)
        d_ref[...] = -ADAM_LR * (m_hat / (jnp.sqrt(v_hat) + ADAM_EPS) + ADAM_WD * w_ref[...])
        nm_ref[...] = m_
        nv_ref[...] = v_

    spec = pl.BlockSpec((tm, 1024), lambda i: (i, 0))
    return pl.pallas_call(
        body, grid=(rows // tm,), in_specs=[spec] * 4, out_specs=[spec] * 3,
        out_shape=[SDS((rows, 1024), F32)] * 3, compiler_params=_cp("parallel"), name="adamw")(w, g, m, v)


LAYERED = ("attn_w_in", "attn_w_out", "dn_w_in", "dn_conv_w", "dn_w_out")


def _two_d(name, a):
    return a[0] if name in LAYERED else a


def kernel(x, meta_tokens, attn_norm_w, attn_w_in, attn_q_norm_w, attn_k_norm_w, attn_sinks, attn_w_out, dn_norm_w, dn_w_in, dn_conv_w, dn_a_log, dn_dt_bias, dn_o_norm_w, dn_w_out, loss_target, m_meta_tokens, m_attn_norm_w, m_attn_w_in, m_attn_q_norm_w, m_attn_k_norm_w, m_attn_sinks, m_attn_w_out, m_dn_norm_w, m_dn_w_in, m_dn_conv_w, m_dn_a_log, m_dn_dt_bias, m_dn_o_norm_w, m_dn_w_out, v_meta_tokens, v_attn_norm_w, v_attn_w_in, v_attn_q_norm_w, v_attn_k_norm_w, v_attn_sinks, v_attn_w_out, v_dn_norm_w, v_dn_w_in, v_dn_conv_w, v_dn_a_log, v_dn_dt_bias, v_dn_o_norm_w, v_dn_w_out):
    given = dict(zip(WEIGHTS, (meta_tokens, attn_norm_w, attn_w_in, attn_q_norm_w, attn_k_norm_w, attn_sinks,
                               attn_w_out, dn_norm_w, dn_w_in, dn_conv_w, dn_a_log, dn_dt_bias, dn_o_norm_w, dn_w_out)))
    mom1 = dict(zip(WEIGHTS, (m_meta_tokens, m_attn_norm_w, m_attn_w_in, m_attn_q_norm_w, m_attn_k_norm_w,
                              m_attn_sinks, m_attn_w_out, m_dn_norm_w, m_dn_w_in, m_dn_conv_w, m_dn_a_log,
                              m_dn_dt_bias, m_dn_o_norm_w, m_dn_w_out)))
    mom2 = dict(zip(WEIGHTS, (v_meta_tokens, v_attn_norm_w, v_attn_w_in, v_attn_q_norm_w, v_attn_k_norm_w,
                              v_attn_sinks, v_attn_w_out, v_dn_norm_w, v_dn_w_in, v_dn_conv_w, v_dn_a_log,
                              v_dn_dt_bias, v_dn_o_norm_w, v_dn_w_out)))
    two_d = lambda d: {n: _two_d(n, a) for n, a in d.items()}
    given, mom1, mom2 = two_d(given), two_d(mom1), two_d(mom2)
    c = lax.axis_index("c")

    w_shard = pack_shard(given)
    gathered = chips_exchange(w_shard, True)
    per_chip = [unpack_shard(gathered[j]) for j in range(N_CHIPS)]
    full = {n: jnp.concatenate([pc[n] for pc in per_chip], axis=SHARDED[n][1]) for n in SHARDED}
    full.update({n: given[n] for n in REPLICATED})

    loss, dx, grads = local_step(x[0], loss_target[0], full)

    split = lambda n: jnp.split(grads[n], N_CHIPS, axis=SHARDED[n][1])
    g_all = jnp.stack([pack_shard({n: split(n)[j] for n in SHARDED}) for j in range(N_CHIPS)])
    keep = lax.dynamic_slice_in_dim(g_all, c * HALF_ROWS, HALF_ROWS, axis=1)
    give = lax.dynamic_slice_in_dim(g_all, (1 - c) * HALF_ROWS, HALF_ROWS, axis=1)
    got = sibling_exchange(give, "pair_exchange")
    pair = sum_blocks(jnp.stack([keep, got]).reshape(2, N_CHIPS * HALF_ROWS, 1024), "pair_sum")
    from_chips = chips_exchange(pair.reshape(N_CHIPS, HALF_ROWS, 1024), False)
    half = sum_blocks(from_chips, "chip_sum")
    other = sibling_exchange(half, "half_exchange")
    g_shard = jnp.where(c == 0, jnp.concatenate([half, other]), jnp.concatenate([other, half]))

    g_small = sum_blocks(all_gather_small(pack_small(grads)), "small_sum")

    stack = lambda big, small: jnp.concatenate([big, small])
    delta, new_m, new_v = adamw(stack(w_shard, pack_small(given)), stack(g_shard, g_small),
                                stack(pack_shard(mom1), pack_small(mom1)), stack(pack_shard(mom2), pack_small(mom2)))

    def unpack(buf):
        out = unpack_shard(buf[:PACK_ROWS])
        out.update(unpack_small(buf[PACK_ROWS:]))
        return [out[n][None] if n in LAYERED else out[n] for n in WEIGHTS]

    loss = lax.psum(loss, ("x", "y", "c"))
    return (loss, dx[None], *unpack(stack(g_shard, g_small)), *unpack(delta), *unpack(new_m), *unpack(new_v))
```

```python
import functools

import jax
import jax.numpy as jnp
from jax import lax
from jax.experimental import pallas as pl
from jax.experimental.pallas import tpu as pltpu

F32 = jnp.float32
BF16 = jnp.bfloat16
SDS = jax.ShapeDtypeStruct
MESH = pl.DeviceIdType.MESH

D_MODEL = 1024
N_META = 16
EPS = 1e-6
BLK = 128
CH = 64
PAD = BLK - N_META
HEADS = 16
HD = 64
KVW = 256
DN_H = 16
DN_KH = 8
DK = 128
SLOPES = [2.0 ** (-8.0 * (h + 1) / HEADS) for h in range(HEADS)]
NEG = -1e30
NT = (((1,), (1,)), ((), ()))
TN = (((0,), (0,)), ((), ()))
HI = lax.Precision.HIGHEST

ADAM_LR, ADAM_B1, ADAM_B2, ADAM_EPS, ADAM_WD, ADAM_STEP = 0.001, 0.9, 0.999, 1e-08, 0.01, 10

VMEM_LIMIT = 56 * 1024 * 1024


def _cp(*sem):
    return pltpu.CompilerParams(dimension_semantics=sem, vmem_limit_bytes=VMEM_LIMIT)


def _row_tile(rows):
    for t in (384, 256, 128):
        if rows % t == 0:
            return t
    raise ValueError(rows)


def _dot(a, b, dims=None, precision=None):
    if dims is None:
        return jnp.dot(a, b, preferred_element_type=F32, precision=precision)
    return lax.dot_general(a, b, dims, preferred_element_type=F32, precision=precision)


def _silu(x):
    return x * jax.nn.sigmoid(x)


def _dsilu(x):
    s = jax.nn.sigmoid(x)
    return s * (1.0 + x * (1.0 - s))


def _rms(x):
    return lax.rsqrt(jnp.mean(x * x, axis=-1, keepdims=True) + EPS)


def norm_matmul(h, nw, w, tn, name):
    rows, k = h.shape
    n = w.shape[1]
    tm = _row_tile(rows)

    def body(h_ref, nw_ref, w_ref, o_ref, xn_ref, xs):
        @pl.when(pl.program_id(1) == 0)
        def _():
            x = h_ref[...]
            xn = (x * _rms(x) * nw_ref[...]).astype(BF16)
            xs[...] = xn
            xn_ref[...] = xn

        o_ref[...] = _dot(xs[...], w_ref[...])

    return pl.pallas_call(
        body, grid=(rows // tm, n // tn),
        in_specs=[pl.BlockSpec((tm, k), lambda i, j: (i, 0)), pl.BlockSpec((1, k), lambda i, j: (0, 0)),
                  pl.BlockSpec((k, tn), lambda i, j: (0, j))],
        out_specs=[pl.BlockSpec((tm, tn), lambda i, j: (i, j)), pl.BlockSpec((tm, k), lambda i, j: (i, 0))],
        out_shape=[SDS((rows, n), F32), SDS((rows, k), BF16)],
        scratch_shapes=[pltpu.VMEM((tm, k), BF16)],
        compiler_params=_cp("parallel", "arbitrary"), name=name)(h, nw, w)


def matmul_residual(a, w, res, name):
    rows, k = a.shape
    n = w.shape[1]
    tm = _row_tile(rows)

    def body(a_ref, w_ref, r_ref, o_ref):
        o_ref[...] = r_ref[...] + _dot(a_ref[...], w_ref[...])

    return pl.pallas_call(
        body, grid=(rows // tm,),
        in_specs=[pl.BlockSpec((tm, k), lambda i: (i, 0)), pl.BlockSpec((k, n), lambda i: (0, 0)),
                  pl.BlockSpec((tm, n), lambda i: (i, 0))],
        out_specs=pl.BlockSpec((tm, n), lambda i: (i, 0)),
        out_shape=SDS((rows, n), F32), compiler_params=_cp("parallel"), name=name)(a, w, res)


def wgrad(a, b, name):
    rows, k = a.shape
    n = b.shape[1]
    tm = _row_tile(rows)
    tn = min(n, 1024)

    def body(a_ref, b_ref, o_ref):
        @pl.when(pl.program_id(1) == 0)
        def _():
            o_ref[...] = jnp.zeros_like(o_ref)

        o_ref[...] += _dot(a_ref[...], b_ref[...].astype(BF16), TN)

    return pl.pallas_call(
        body, grid=(n // tn, rows // tm),
        in_specs=[pl.BlockSpec((tm, k), lambda j, i: (i, 0)), pl.BlockSpec((tm, tn), lambda j, i: (i, j))],
        out_specs=pl.BlockSpec((k, tn), lambda j, i: (0, j)),
        out_shape=SDS((k, n), F32), compiler_params=_cp("parallel", "arbitrary"), name=name)(a, b)


def in_proj_bwd(dus, ws, h, nw, dh_next, name):
    rows, k = h.shape
    tm = 128
    nd = len(dus)
    nt = rows // tm

    def body(*refs):
        du_refs, w_refs = refs[:nd], refs[nd:2 * nd]
        h_ref, nw_ref, dhn_ref, dh_ref, dnw_ref = refs[2 * nd:]
        dxn = _dot(du_refs[0][...].astype(BF16), w_refs[0][...], NT)
        for du_ref, w_ref in zip(du_refs[1:], w_refs[1:]):
            dxn += _dot(du_ref[...].astype(BF16), w_ref[...], NT)
        x = h_ref[...]
        r = _rms(x)
        y = x * r
        gy = dxn * nw_ref[...]
        dh_ref[...] = dhn_ref[...] + r * (gy - y * jnp.mean(y * gy, axis=-1, keepdims=True))
        dnw_ref[0] = jnp.sum(dxn * y, axis=0, keepdims=True)

    in_specs = [pl.BlockSpec((tm, du.shape[1]), lambda i: (i, 0)) for du in dus]
    in_specs += [pl.BlockSpec(w.shape, lambda i: (0, 0)) for w in ws]
    in_specs += [pl.BlockSpec((tm, k), lambda i: (i, 0)), pl.BlockSpec((1, k), lambda i: (0, 0)),
                 pl.BlockSpec((tm, k), lambda i: (i, 0))]
    return pl.pallas_call(
        body, grid=(nt,), in_specs=in_specs,
        out_specs=[pl.BlockSpec((tm, k), lambda i: (i, 0)), pl.BlockSpec((1, 1, k), lambda i: (i, 0, 0))],
        out_shape=[SDS((rows, k), F32), SDS((nt, 1, k), F32)],
        compiler_params=_cp("parallel"), name=name)(*dus, *ws, h, nw, dh_next)


def matmul_nt(a, w, name):
    rows, k = a.shape
    n = w.shape[0]
    tm = _row_tile(rows)

    def body(a_ref, w_ref, o_ref):
        o_ref[...] = _dot(a_ref[...].astype(BF16), w_ref[...], NT)

    return pl.pallas_call(
        body, grid=(rows // tm,),
        in_specs=[pl.BlockSpec((tm, k), lambda i: (i, 0)), pl.BlockSpec((n, k), lambda i: (0, 0))],
        out_specs=pl.BlockSpec((tm, n), lambda i: (i, 0)),
        out_shape=SDS((rows, n), F32), compiler_params=_cp("parallel"), name=name)(a, w)


def _attn_masks(n):
    qi = lax.broadcasted_iota(jnp.int32, (BLK, 2 * BLK), 0)
    kj = lax.broadcasted_iota(jnp.int32, (BLK, 2 * BLK), 1)
    dist = BLK + qi - kj
    valid_b = (dist >= 0) & (dist < BLK) & (kj >= 2 * BLK - BLK * n)
    qm = lax.broadcasted_iota(jnp.int32, (BLK, N_META), 0)
    mm = lax.broadcasted_iota(jnp.int32, (BLK, N_META), 1)
    dm = n * BLK + qm - PAD - mm
    return valid_b, dist.astype(F32), dm >= 0, jnp.minimum(dm, BLK).astype(F32)


def _attn_probs(qn16, knb, knm, sink, slope, masks):
    valid_b, dist_b, valid_m, dist_m = masks
    sb = _dot(qn16, knb, NT) * (HD ** -0.5)
    sm = _dot(qn16, knm, NT) * (HD ** -0.5)
    sb = jnp.where(valid_b, sb - slope * dist_b, NEG)
    sm = jnp.where(valid_m, sm - slope * dist_m, NEG)
    mx = jnp.maximum(jnp.maximum(jnp.max(sb, axis=-1, keepdims=True), jnp.max(sm, axis=-1, keepdims=True)), sink)
    eb = jnp.where(valid_b, jnp.exp(sb - mx), 0.0)
    em = jnp.where(valid_m, jnp.exp(sm - mx), 0.0)
    es = jnp.exp(sink - mx)
    inv = 1.0 / (jnp.sum(eb, axis=-1, keepdims=True) + jnp.sum(em, axis=-1, keepdims=True) + es)
    return eb * inv, em * inv, es * inv


def _kv_specs(nblk, clamp):
    cur = (lambda n: (jnp.minimum(n, nblk - 1), 8)) if clamp else (lambda n: (n, 8))
    return [pl.BlockSpec((BLK, KVW), cur),
            pl.BlockSpec((BLK, KVW), lambda n: (jnp.maximum(n - 1, 0), 8)),
            pl.BlockSpec((N_META, KVW), lambda n: (PAD // N_META, 8))]


def attn_fwd(u, qw, kw, sinks):
    rows = u.shape[0]
    nblk = rows // BLK

    def body(q_ref, g_ref, kvc_ref, kvp_ref, kvm_ref, qw_ref, kw_ref, sk_ref, og_ref):
        masks = _attn_masks(pl.program_id(0))
        kvb = jnp.concatenate([kvp_ref[...], kvc_ref[...]], axis=0)
        kvm = kvm_ref[...]
        qw_, kw_ = qw_ref[...], kw_ref[...]
        outs = []
        for kvh in range(2):
            kb, km = kvb[:, HD * kvh:HD * kvh + HD], kvm[:, HD * kvh:HD * kvh + HD]
            knb = (kb * _rms(kb) * kw_).astype(BF16)
            knm = (km * _rms(km) * kw_).astype(BF16)
            vb = kvb[:, BLK + HD * kvh:BLK + HD * kvh + HD].astype(BF16)
            vm = kvm[:, BLK + HD * kvh:BLK + HD * kvh + HD].astype(BF16)
            for g in range(8):
                h = kvh * 8 + g
                qh = q_ref[:, HD * h:HD * h + HD]
                qn16 = (qh * _rms(qh) * qw_).astype(BF16)
                pb, pm, _ = _attn_probs(qn16, knb, knm, sk_ref[:, h:h + 1], SLOPES[h], masks)
                outs.append(_dot(pb.astype(BF16), vb) + _dot(pm.astype(BF16), vm))
        og_ref[...] = (jnp.concatenate(outs, axis=1) * _silu(g_ref[...])).astype(BF16)

    small = lambda w: pl.BlockSpec((1, w), lambda n: (0, 0))
    return pl.pallas_call(
        body, grid=(nblk,),
        in_specs=[pl.BlockSpec((BLK, 1024), lambda n: (n, 0)), pl.BlockSpec((BLK, 1024), lambda n: (n, 1))]
        + _kv_specs(nblk, False) + [small(HD), small(HD), small(HEADS)],
        out_specs=pl.BlockSpec((BLK, 1024), lambda n: (n, 0)),
        out_shape=SDS((rows, 1024), BF16), compiler_params=_cp("parallel"), name="attn_fwd")(
            u, u, u, u, u, qw, kw, sinks)


def attn_bwd(u, qw, kw, sinks, dog):
    rows = u.shape[0]
    nblk = rows // BLK

    def knorm_bwd(k, dkn, kw_):
        r = _rms(k)
        y = k * r
        gy = dkn * kw_
        return r * (gy - y * jnp.mean(y * gy, axis=-1, keepdims=True)), jnp.sum(dkn * y, axis=0, keepdims=True)

    def body(q_ref, g_ref, dog_ref, kvc_ref, kvp_ref, kvm_ref, qw_ref, kw_ref, sk_ref,
             dq_ref, dg_ref, dkv_ref, dkvm_ref, dqw_ref, dkw_ref, dsk_ref, carry, prevp, curp, metap):
        n = pl.program_id(0)
        qw_, kw_ = qw_ref[...], kw_ref[...]

        @pl.when(n == 0)
        def _():
            carry[...] = jnp.zeros_like(carry)
            metap[...] = jnp.zeros_like(metap)
            dqw_ref[...] = jnp.zeros_like(dqw_ref)
            dkw_ref[...] = jnp.zeros_like(dkw_ref)
            dsk_ref[...] = jnp.zeros_like(dsk_ref)

        @pl.when(n == nblk)
        def _():
            prevp[...] = jnp.zeros_like(prevp)
            curp[...] = jnp.zeros_like(curp)

        @pl.when(n < nblk)
        def _():
            masks = _attn_masks(n)
            kvb = jnp.concatenate([kvp_ref[...], kvc_ref[...]], axis=0)
            kvm = kvm_ref[...]
            lane = lax.broadcasted_iota(jnp.int32, (1, HEADS), 1)
            dqs, dgs = [], []
            dqw = jnp.zeros((1, HD), F32)
            dsk = jnp.zeros((1, HEADS), F32)
            band_parts, meta_parts = [None] * 4, [None] * 4
            for kvh in range(2):
                kb, km = kvb[:, HD * kvh:HD * kvh + HD], kvm[:, HD * kvh:HD * kvh + HD]
                knb = (kb * _rms(kb) * kw_).astype(BF16)
                knm = (km * _rms(km) * kw_).astype(BF16)
                vb = kvb[:, BLK + HD * kvh:BLK + HD * kvh + HD].astype(BF16)
                vm = kvm[:, BLK + HD * kvh:BLK + HD * kvh + HD].astype(BF16)
                dkb = jnp.zeros((2 * BLK, HD), F32)
                dvb = jnp.zeros((2 * BLK, HD), F32)
                dkm = jnp.zeros((N_META, HD), F32)
                dvm = jnp.zeros((N_META, HD), F32)
                for g in range(8):
                    h = kvh * 8 + g
                    sl = slice(HD * h, HD * h + HD)
                    qh = q_ref[:, sl]
                    r = _rms(qh)
                    y = qh * r
                    qn16 = (y * qw_).astype(BF16)
                    pb, pm, ps = _attn_probs(qn16, knb, knm, sk_ref[:, h:h + 1], SLOPES[h], masks)
                    pb16, pm16 = pb.astype(BF16), pm.astype(BF16)
                    o = _dot(pb16, vb) + _dot(pm16, vm)
                    gate = g_ref[:, sl]
                    dogh = dog_ref[:, sl]
                    dgs.append(dogh * o * _dsilu(gate))
                    do16 = (dogh * _silu(gate)).astype(BF16)
                    dpb = _dot(do16, vb, NT)
                    dpm = _dot(do16, vm, NT)
                    delta = jnp.sum(pb * dpb, axis=-1, keepdims=True) + jnp.sum(pm * dpm, axis=-1, keepdims=True)
                    dsb = (pb * (dpb - delta)).astype(BF16)
                    dsm = (pm * (dpm - delta)).astype(BF16)
                    dsk += jnp.where(lane == h, jnp.sum(-ps * delta, axis=0, keepdims=True), 0.0)
                    dqn = (_dot(dsb, knb) + _dot(dsm, knm)) * (HD ** -0.5)
                    dkb += _dot(dsb, qn16, TN) * (HD ** -0.5)
                    dkm += _dot(dsm, qn16, TN) * (HD ** -0.5)
                    dvb += _dot(pb16, do16, TN)
                    dvm += _dot(pm16, do16, TN)
                    gy = dqn * qw_
                    dqs.append(r * (gy - y * jnp.mean(y * gy, axis=-1, keepdims=True)))
                    dqw += jnp.sum(dqn * y, axis=0, keepdims=True)
                band_parts[kvh], band_parts[2 + kvh] = dkb, dvb
                meta_parts[kvh], meta_parts[2 + kvh] = dkm, dvm
            dq_ref[...] = jnp.concatenate(dqs, axis=1)
            dg_ref[...] = jnp.concatenate(dgs, axis=1)
            band = jnp.concatenate(band_parts, axis=1)
            prevp[...] = band[:BLK]
            curp[...] = band[BLK:]
            metap[...] += jnp.concatenate(meta_parts, axis=1)
            dqw_ref[...] += dqw
            dsk_ref[...] += dsk

        tot = carry[...] + prevp[...]
        kprev = kvp_ref[...]
        dk0, w0 = knorm_bwd(kprev[:, 0:HD], tot[:, 0:HD], kw_)
        dk1, w1 = knorm_bwd(kprev[:, HD:2 * HD], tot[:, HD:2 * HD], kw_)
        dkv_ref[...] = jnp.concatenate([dk0, dk1, tot[:, 2 * HD:]], axis=1)
        dkw_ref[...] += w0 + w1
        carry[...] = curp[...]

        @pl.when(n == nblk)
        def _():
            mt = metap[...]
            km = kvm_ref[...]
            m0, v0 = knorm_bwd(km[:, 0:HD], mt[:, 0:HD], kw_)
            m1, v1 = knorm_bwd(km[:, HD:2 * HD], mt[:, HD:2 * HD], kw_)
            dkvm_ref[...] = jnp.concatenate([m0, m1, mt[:, 2 * HD:]], axis=1)
            dkw_ref[...] += v0 + v1

    small = lambda w: pl.BlockSpec((1, w), lambda n: (0, 0))
    cl = lambda n: jnp.minimum(n, nblk - 1)
    return pl.pallas_call(
        body, grid=(nblk + 1,),
        in_specs=[pl.BlockSpec((BLK, 1024), lambda n: (cl(n), 0)), pl.BlockSpec((BLK, 1024), lambda n: (cl(n), 1)),
                  pl.BlockSpec((BLK, 1024), lambda n: (cl(n), 0))]
        + _kv_specs(nblk, True) + [small(HD), small(HD), small(HEADS)],
        out_specs=[pl.BlockSpec((BLK, 1024), lambda n: (cl(n), 0)), pl.BlockSpec((BLK, 1024), lambda n: (cl(n), 0)),
                   pl.BlockSpec((BLK, KVW), lambda n: (jnp.maximum(n - 1, 0), 0)),
                   pl.BlockSpec((N_META, KVW), lambda n: (0, 0)), small(HD), small(HD), small(HEADS)],
        out_shape=[SDS((rows, 1024), F32), SDS((rows, 1024), F32), SDS((rows, KVW), F32), SDS((N_META, KVW), F32),
                   SDS((1, HD), F32), SDS((1, HD), F32), SDS((1, HEADS), F32)],
        scratch_shapes=[pltpu.VMEM((BLK, KVW), F32), pltpu.VMEM((BLK, KVW), F32), pltpu.VMEM((BLK, KVW), F32),
                        pltpu.VMEM((N_META, KVW), F32)],
        compiler_params=_cp("arbitrary"), name="attn_bwd")(u, u, dog, u, u, u, qw, kw, sinks)


def _tri_inv(m, ii, jj):
    eye = (ii == jj).astype(F32)
    mb = jnp.where((ii >> 3) == (jj >> 3), m, 0.0)
    m2 = _bdot(mb, mb, "nn", True)
    m4 = _bdot(m2, m2, "nn", True)
    x = _bdot(_bdot(eye - mb, eye + m2, "nn", True), eye + m4, "nn", True)
    for sh in (3, 4, 5):
        lb = jnp.where(((ii >> (sh + 1)) == (jj >> (sh + 1))) & ((ii >> sh) != (jj >> sh)), m, 0.0)
        x = x - _bdot(_bdot(x, lb, "nn", True), x, "nn", True)
    return x


HB = 8


def _bdot(a, b, kind, split=False):
    dims = {"nn": ((2,), (1,)), "nt": ((2,), (2,)), "tn": ((1,), (1,))}[kind]
    dg = lambda p, q: lax.dot_general(p, q, (dims, ((0,), (0,))), preferred_element_type=F32)
    if not split:
        return dg(a, b)
    ah, bh = a.astype(BF16), b.astype(BF16)
    al, bl = (a - ah.astype(F32)).astype(BF16), (b - bh.astype(F32)).astype(BF16)
    return (dg(ah, bl) + dg(al, bh)) + dg(ah, bh)


def _head_cols(hv, beta, gc, gct, lane):
    sel = lane == hv
    return _pick(beta, sel), _pick(gc, sel), gct[pl.ds(hv, 1), :]


def _conv_group(xc_ref, xp_ref, cw_ref, off, first):
    xp = jnp.where(first, 0.0, xp_ref[:, pl.ds(off, DK)])
    xx = jnp.concatenate([xp, xc_ref[:, pl.ds(off, DK)]], axis=0)
    y = cw_ref[0:1, pl.ds(off, DK)] * xx[5:5 + CH]
    for j in range(1, 4):
        y += cw_ref[j:j + 1, pl.ds(off, DK)] * xx[5 + j:5 + j + CH]
    return xx, y


def _gates(ba, al, dtb, c):
    row = c * CH + lax.broadcasted_iota(jnp.int32, (CH, DN_H), 0)
    real = row >= PAD
    xa = ba[:, DN_H:2 * DN_H] + dtb
    beta = jnp.where(real, jax.nn.sigmoid(ba[:, 0:DN_H]), 0.0)
    g = jnp.where(real, -jnp.exp(al) * jax.nn.softplus(xa), 0.0)
    return real, xa, beta, g


def _pick(x, sel):
    return jnp.sum(jnp.where(sel, x, 0.0), axis=1, keepdims=True)


def _chunk_specs(width_blocks):
    return [pl.BlockSpec((CH, 4096), lambda c: (c, 0)),
            pl.BlockSpec((8, 4096), lambda c: (jnp.maximum(8 * c - 1, 0), 0)),
            pl.BlockSpec((CH, DK), lambda c: (c, 48))]


def dn_prep(udn, conv_w, a_log, dt_bias):
    rows = udn.shape[0]
    nch = rows // CH

    def body(xc_ref, xp_ref, ba_ref, cw_ref, al_ref, dtb_ref,
             qn_ref, kn_ref, sv_ref, gc_ref, beta_ref, u_ref, w_ref, qe_ref, ks_ref, p_ref, a_ref, gct):
        c = pl.program_id(0)
        first = c == 0
        _, _, beta, g = _gates(ba_ref[...], al_ref[...], dtb_ref[...], c)
        ii = lax.broadcasted_iota(jnp.int32, (CH, CH), 0)
        jj = lax.broadcasted_iota(jnp.int32, (CH, CH), 1)
        gc = _dot((ii >= jj).astype(F32), g, precision=HI)
        gc_ref[...] = gc
        beta_ref[...] = beta
        gct[...] = gc.T

        def qk_body(kh, carry):
            off = pl.multiple_of(kh * DK, DK)
            _, yq = _conv_group(xc_ref, xp_ref, cw_ref, off, first)
            sq = _silu(yq)
            qn_ref[:, pl.ds(off, DK)] = sq * lax.rsqrt(jnp.sum(sq * sq, axis=-1, keepdims=True) + EPS) * (DK ** -0.5)
            _, yk = _conv_group(xc_ref, xp_ref, cw_ref, pl.multiple_of(1024 + kh * DK, DK), first)
            sk = _silu(yk)
            kn_ref[:, pl.ds(off, DK)] = sk * lax.rsqrt(jnp.sum(sk * sk, axis=-1, keepdims=True) + EPS)
            return carry

        lax.fori_loop(0, DN_KH, qk_body, 0)
        lane = lax.broadcasted_iota(jnp.int32, (CH, DN_H), 1)
        zpad = jnp.zeros((CH, DK - CH), F32)

        def v_group(grp, carry):
            offs, ks_, qs_, vs_, cols = [], [], [], [], []
            for i in range(HB):
                hv = grp * HB + i
                offs.append(pl.multiple_of(hv * DK, DK))
                koff = pl.multiple_of((grp * (HB // 2) + i // 2) * DK, DK)
                _, yv = _conv_group(xc_ref, xp_ref, cw_ref, pl.multiple_of(2048 + hv * DK, DK), first)
                vs_.append(_silu(yv))
                sv_ref[:, pl.ds(offs[i], DK)] = vs_[i]
                ks_.append(kn_ref[:, pl.ds(koff, DK)])
                qs_.append(qn_ref[:, pl.ds(koff, DK)])
                cols.append(_head_cols(hv, beta, gc, gct, lane))
            k, q, v = jnp.stack(ks_), jnp.stack(qs_), jnp.stack(vs_)
            beta_c, gc_c, gc_r = (jnp.stack([c_[j] for c_ in cols]) for j in range(3))
            dec = jnp.exp(jnp.where(ii >= jj, gc_c - gc_r, NEG))
            eg = jnp.exp(gc_c)
            kb = k * beta_c
            k16 = k.astype(BF16)
            m = jnp.where(ii > jj, _bdot(kb.astype(BF16), k16, "nt") * dec, 0.0)
            a = _tri_inv(m, ii, jj)
            uw = _bdot(a, jnp.concatenate([v * beta_c, kb * eg], axis=2), "nn", True)
            p = _bdot(q.astype(BF16), k16, "nt") * dec
            qe = (q * eg).astype(BF16)
            ksx = (k * jnp.exp(gc_c[:, CH - 1:CH, :] - gc_c)).astype(BF16)
            for i in range(HB):
                sl = pl.ds(offs[i], DK)
                u_ref[:, sl] = uw[i, :, :DK]
                w_ref[:, sl] = uw[i, :, DK:]
                qe_ref[:, sl] = qe[i]
                ks_ref[:, sl] = ksx[i]
                p_ref[:, sl] = jnp.concatenate([p[i], zpad], axis=1).astype(BF16)
                a_ref[:, sl] = jnp.concatenate([a[i], zpad], axis=1)
            return carry

        lax.fori_loop(0, DN_H // HB, v_group, 0)

    full = lambda shape: pl.BlockSpec(shape, lambda c: (0, 0))
    blk = lambda w: pl.BlockSpec((CH, w), lambda c: (c, 0))
    return pl.pallas_call(
        body, grid=(nch,),
        in_specs=_chunk_specs(0) + [full((4, 4096)), full((1, DN_H)), full((1, DN_H))],
        out_specs=[blk(1024), blk(1024), blk(2048), blk(DN_H), blk(DN_H), blk(2048), blk(2048), blk(2048), blk(2048),
                   blk(2048), blk(2048)],
        out_shape=[SDS((rows, 1024), F32), SDS((rows, 1024), F32), SDS((rows, 2048), F32), SDS((rows, DN_H), F32),
                   SDS((rows, DN_H), F32), SDS((rows, 2048), F32), SDS((rows, 2048), F32), SDS((rows, 2048), BF16),
                   SDS((rows, 2048), BF16), SDS((rows, 2048), BF16), SDS((rows, 2048), F32)],
        scratch_shapes=[pltpu.VMEM((DN_H, CH), F32)],
        compiler_params=_cp("parallel"), name="dn_prep")(udn, udn, udn, conv_w, a_log, dt_bias)


def dn_scan(u, w, qe, ks, p, gc):
    rows = u.shape[0]
    nch = rows // CH

    def body(u_ref, w_ref, qe_ref, ks_ref, p_ref, gc_ref, o_ref, vn_ref, st_ref, s_scr):
        @pl.when(pl.program_id(0) == 0)
        def _():
            s_scr[...] = jnp.zeros_like(s_scr)

        gl_row = gc_ref[CH - 1:CH, :]
        lane = lax.broadcasted_iota(jnp.int32, (1, DN_H), 1)

        def group(grp, carry):
            base = grp * HB
            sls = [pl.ds(pl.multiple_of((base + i) * DK, DK), DK) for i in range(HB)]
            heads = lambda ref: jnp.stack([ref[:, sl] for sl in sls])
            s = s_scr[pl.ds(base, HB)]
            st_ref[0, pl.ds(base, HB)] = s
            s16 = s.astype(BF16)
            vn = heads(u_ref) - _bdot(heads(w_ref).astype(BF16), s16, "nn")
            vn16 = vn.astype(BF16)
            o = _bdot(heads(qe_ref), s16, "nn") + _bdot(heads(p_ref)[:, :, 0:CH], vn16, "nn")
            egl = jnp.exp(jnp.stack([_pick(gl_row, lane == base + i) for i in range(HB)]))
            s_scr[pl.ds(base, HB)] = s * egl + _bdot(heads(ks_ref), vn16, "tn")
            for i in range(HB):
                vn_ref[:, sls[i]] = vn16[i]
                o_ref[:, sls[i]] = o[i]
            return carry

        lax.fori_loop(0, DN_H // HB, group, 0)

    blk = lambda wd: pl.BlockSpec((CH, wd), lambda c: (c, 0))
    return pl.pallas_call(
        body, grid=(nch,),
        in_specs=[blk(2048)] * 5 + [blk(DN_H)],
        out_specs=[blk(2048), blk(2048), pl.BlockSpec((1, DN_H, DK, DK), lambda c: (c, 0, 0, 0))],
        out_shape=[SDS((rows, 2048), F32), SDS((rows, 2048), BF16), SDS((nch, DN_H, DK, DK), F32)],
        scratch_shapes=[pltpu.VMEM((DN_H, DK, DK), F32)],
        compiler_params=_cp("arbitrary"), name="dn_scan")(u, w, qe, ks, p, gc)


def dn_out_fwd(o, udn, ow, wout, h1, tgt):
    rows = o.shape[0]
    tm = _row_tile(rows)
    nt = rows // tm

    def body(o_ref, z_ref, ow_ref, w_ref, h_ref, t_ref, dh_ref, on_ref, ls_ref):
        for hv in range(DN_H):
            sl = slice(hv * DK, hv * DK + DK)
            oh = o_ref[:, sl]
            on_ref[:, sl] = (oh * _rms(oh) * ow_ref[...] * _silu(z_ref[:, sl])).astype(BF16)
        h2 = h_ref[...] + _dot(on_ref[...], w_ref[...])
        row = pl.program_id(0) * tm + lax.broadcasted_iota(jnp.int32, (tm, 1), 0)
        err = jnp.where(row >= BLK, h2 - t_ref[...], 0.0)
        dh_ref[...] = err * (1.0 / D_MODEL)
        ls_ref[0] = jnp.sum(err * err, axis=0, keepdims=True)

    return pl.pallas_call(
        body, grid=(nt,),
        in_specs=[pl.BlockSpec((tm, 2048), lambda i: (i, 0)), pl.BlockSpec((tm, 2048), lambda i: (i, 2)),
                  pl.BlockSpec((1, DK), lambda i: (0, 0)), pl.BlockSpec((2048, D_MODEL), lambda i: (0, 0)),
                  pl.BlockSpec((tm, D_MODEL), lambda i: (i, 0)), pl.BlockSpec((tm, D_MODEL), lambda i: (i, 0))],
        out_specs=[pl.BlockSpec((tm, D_MODEL), lambda i: (i, 0)), pl.BlockSpec((tm, 2048), lambda i: (i, 0)),
                   pl.BlockSpec((1, 1, D_MODEL), lambda i: (i, 0, 0))],
        out_shape=[SDS((rows, D_MODEL), F32), SDS((rows, 2048), BF16), SDS((nt, 1, D_MODEL), F32)],
        compiler_params=_cp("parallel"), name="dn_out_fwd")(o, udn, ow, wout, h1, tgt)


def dn_out_bwd(dh2, wout, o, udn, ow):
    rows = o.shape[0]
    tm = _row_tile(rows)
    nt = rows // tm

    def body(dh_ref, w_ref, o_ref, z_ref, ow_ref, do_ref, dz_ref, dow_ref):
        don = _dot(dh_ref[...].astype(BF16), w_ref[...], NT)
        ow_ = ow_ref[...]
        dow = jnp.zeros((1, DK), F32)
        for hv in range(DN_H):
            sl = slice(hv * DK, hv * DK + DK)
            oh = o_ref[:, sl]
            r = _rms(oh)
            y = oh * r
            z = z_ref[:, sl]
            dn = don[:, sl] * _silu(z)
            dz_ref[:, sl] = don[:, sl] * (y * ow_) * _dsilu(z)
            dy = dn * ow_
            do_ref[:, sl] = r * (dy - y * jnp.mean(y * dy, axis=-1, keepdims=True))
            dow += jnp.sum(dn * y, axis=0, keepdims=True)
        dow_ref[0] = dow

    return pl.pallas_call(
        body, grid=(nt,),
        in_specs=[pl.BlockSpec((tm, D_MODEL), lambda i: (i, 0)), pl.BlockSpec((2048, D_MODEL), lambda i: (0, 0)),
                  pl.BlockSpec((tm, 2048), lambda i: (i, 0)), pl.BlockSpec((tm, 2048), lambda i: (i, 2)),
                  pl.BlockSpec((1, DK), lambda i: (0, 0))],
        out_specs=[pl.BlockSpec((tm, 2048), lambda i: (i, 0)), pl.BlockSpec((tm, 2048), lambda i: (i, 0)),
                   pl.BlockSpec((1, 1, DK), lambda i: (i, 0, 0))],
        out_shape=[SDS((rows, 2048), F32), SDS((rows, 2048), F32), SDS((nt, 1, DK), F32)],
        compiler_params=_cp("parallel"), name="dn_out_bwd")(dh2, wout, o, udn, ow)


def dn_scan_bwd(do, qn, kn, sv, gc, beta, a, u, w, vn, qe, ks, st):
    rows = do.shape[0]
    nch = rows // CH

    def body(do_ref, q_ref, k_ref, v_ref, gc_ref, beta_ref, a_ref, u_ref, w_ref, vn_ref, qe_ref, ks_ref, st_ref,
             dq_ref, dk_ref, dv_ref, dbeta_ref, dg_ref, ds_scr, gct):
        @pl.when(pl.program_id(0) == 0)
        def _():
            ds_scr[...] = jnp.zeros_like(ds_scr)

        gc, beta = gc_ref[...], beta_ref[...]
        gct[...] = gc.T
        ii = lax.broadcasted_iota(jnp.int32, (CH, CH), 0)
        jj = lax.broadcasted_iota(jnp.int32, (CH, CH), 1)
        lane = lax.broadcasted_iota(jnp.int32, (CH, DN_H), 1)
        last = lax.broadcasted_iota(jnp.int32, (CH, 1), 0) == CH - 1
        rsum = lambda x: jnp.sum(x, axis=2, keepdims=True)

        def group(grp, carry):
            dbeta_acc, dgc_acc = carry
            base = grp * HB
            sls = [pl.ds(pl.multiple_of((base + i) * DK, DK), DK) for i in range(HB)]
            ksls = [pl.ds(pl.multiple_of((grp * (HB // 2) + j) * DK, DK), DK) for j in range(HB // 2)]
            heads = lambda ref: jnp.stack([ref[:, sl] for sl in sls])
            kheads = lambda ref: jnp.stack([ref[:, ksls[i // 2]] for i in range(HB)])
            cols = [_head_cols(base + i, beta, gc, gct, lane) for i in range(HB)]
            beta_c, gc_c, gc_r = (jnp.stack([c_[j] for c_ in cols]) for j in range(3))
            k, q, v = kheads(k_ref), kheads(q_ref), heads(v_ref)
            dec = jnp.exp(jnp.where(ii >= jj, gc_c - gc_r, NEG))
            eg = jnp.exp(gc_c)
            gl = gc_c[:, CH - 1:CH, :]
            e2 = jnp.exp(gl - gc_c)
            egl = jnp.exp(gl)
            k16, q16 = k.astype(BF16), q.astype(BF16)
            do16 = heads(do_ref).astype(BF16)
            s = st_ref[0, pl.ds(base, HB)]
            s16 = s.astype(BF16)
            dso = ds_scr[pl.ds(base, HB)]
            dso16 = dso.astype(BF16)
            wf, uf, vn16 = heads(w_ref), heads(u_ref), heads(vn_ref)
            kb = k * beta_c
            kb16 = kb.astype(BF16)
            pm = _bdot(q16, k16, "nt") * dec
            m = jnp.where(ii > jj, _bdot(kb16, k16, "nt") * dec, 0.0)
            dvn = _bdot(pm.astype(BF16), do16, "tn") + _bdot(heads(ks_ref), dso16, "nn")
            dvn16 = dvn.astype(BF16)
            ds_scr[pl.ds(base, HB)] = (egl * dso + _bdot(heads(qe_ref), do16, "tn")
                                       - _bdot(wf.astype(BF16), dvn16, "tn"))
            dpm = jnp.where(ii >= jj, _bdot(do16, vn16, "nt"), 0.0)
            dqk16 = (dpm * dec).astype(BF16)
            dqe = _bdot(do16, s16, "nt")
            dq = eg * dqe + _bdot(dqk16, k16, "nn")
            dks = _bdot(vn16, dso16, "nt")
            t = rsum(dks * k) * e2
            dgl = jnp.sum(t, axis=1, keepdims=True) + egl * jnp.sum(rsum(dso * s), axis=1, keepdims=True)
            dw = -_bdot(dvn16, s16, "nt")
            am = heads(a_ref)[:, :, 0:CH]
            dbvk = _bdot(am, jnp.concatenate([dvn, dw], axis=2), "tn", True)
            dbv, dbk = dbvk[:, :, :DK], dbvk[:, :, DK:]
            dm = jnp.where(ii > jj, -_bdot(dbvk, jnp.concatenate([uf, wf], axis=2), "nt", True), 0.0)
            g16 = (dm * dec).astype(BF16)
            dkb = _bdot(g16, k16, "nn")
            dk = (_bdot(dqk16, q16, "tn") + e2 * dks + _bdot(g16, kb16, "tn") + beta_c * (eg * dbk + dkb))
            e = dpm * pm + dm * m
            deg = rsum(q * dqe) + rsum(dbk * kb)
            dgc = rsum(e) - t + deg * eg + jnp.where(last, dgl, 0.0)
            dgrow = -jnp.sum(e, axis=1, keepdims=True)
            dv = beta_c * dbv
            dbeta = rsum(dbv * v) + rsum(dbk * k) * eg + rsum(dkb * k)
            for i in range(HB):
                dv_ref[:, sls[i]] = dv[i]
                sel = lane == base + i
                dbeta_acc = jnp.where(sel, dbeta[i], dbeta_acc)
                dgc_acc = jnp.where(sel, dgc[i], dgc_acc)
                gct[pl.ds(base + i, 1), :] = dgrow[i]
            for j in range(HB // 2):
                dq_ref[:, ksls[j]] = dq[2 * j] + dq[2 * j + 1]
                dk_ref[:, ksls[j]] = dk[2 * j] + dk[2 * j + 1]
            return dbeta_acc, dgc_acc

        zero = jnp.zeros((CH, DN_H), F32)
        dbeta_acc, dgc_acc = lax.fori_loop(0, DN_H // HB, group, (zero, zero))
        dbeta_ref[...] = dbeta_acc
        dg_ref[...] = _dot((ii <= jj).astype(F32), dgc_acc + gct[...].T, precision=HI)

    rev = lambda wd: pl.BlockSpec((CH, wd), lambda i: (nch - 1 - i, 0))
    return pl.pallas_call(
        body, grid=(nch,),
        in_specs=[rev(2048), rev(1024), rev(1024), rev(2048), rev(DN_H), rev(DN_H), rev(2048), rev(2048), rev(2048),
                  rev(2048), rev(2048), rev(2048), pl.BlockSpec((1, DN_H, DK, DK), lambda i: (nch - 1 - i, 0, 0, 0))],
        out_specs=[rev(1024), rev(1024), rev(2048), rev(DN_H), rev(DN_H)],
        out_shape=[SDS((rows, 1024), F32), SDS((rows, 1024), F32), SDS((rows, 2048), F32), SDS((rows, DN_H), F32),
                   SDS((rows, DN_H), F32)],
        scratch_shapes=[pltpu.VMEM((DN_H, DK, DK), F32), pltpu.VMEM((DN_H, CH), F32)],
        compiler_params=_cp("arbitrary"), name="dn_scan_bwd")(do, qn, kn, sv, gc, beta, a, u, w, vn, qe, ks, st)


def dn_prep_bwd(udn, conv_w, a_log, dt_bias, dqn, dkn, dv, dbeta, dg):
    rows = udn.shape[0]
    nch = rows // CH

    def body(xc_ref, xp_ref, ba_ref, cw_ref, al_ref, dtb_ref, dqn_ref, dkn_ref, dv_ref, dbeta_ref, dg_ref,
             dy_ref, dba_ref, dcw_ref, dal_ref, ddtb_ref):
        c = pl.program_id(0)
        first = c == 0

        @pl.when(first)
        def _():
            dcw_ref[...] = jnp.zeros_like(dcw_ref)
            dal_ref[...] = jnp.zeros_like(dal_ref)
            ddtb_ref[...] = jnp.zeros_like(ddtb_ref)

        real, xa, beta, g = _gates(ba_ref[...], al_ref[...], dtb_ref[...], c)
        dgm = jnp.where(real, dg_ref[...], 0.0)
        da = dgm * (-jnp.exp(al_ref[...])) * jax.nn.sigmoid(xa)
        dal_ref[...] += jnp.sum(dgm * g, axis=0, keepdims=True)
        ddtb_ref[...] += jnp.sum(da, axis=0, keepdims=True)
        dba_ref[...] = jnp.zeros_like(dba_ref)
        dba_ref[:, 0:DN_H] = jnp.where(real, dbeta_ref[...] * beta * (1.0 - beta), 0.0)
        dba_ref[:, DN_H:2 * DN_H] = da

        def through_conv(off, grad_fn):
            xx, y = _conv_group(xc_ref, xp_ref, cw_ref, off, first)
            dy = grad_fn(_silu(y)) * _dsilu(y)
            dy_ref[:, pl.ds(off, DK)] = dy
            for j in range(4):
                dcw_ref[j:j + 1, pl.ds(off, DK)] += jnp.sum(dy * xx[5 + j:5 + j + CH], axis=0, keepdims=True)

        def l2_bwd(gin):
            def f(s):
                r = lax.rsqrt(jnp.sum(s * s, axis=-1, keepdims=True) + EPS)
                nrm = s * r
                return r * (gin - nrm * jnp.sum(nrm * gin, axis=-1, keepdims=True))
            return f

        def qk_body(kh, carry):
            off = pl.multiple_of(kh * DK, DK)
            through_conv(off, l2_bwd(dqn_ref[:, pl.ds(off, DK)] * (DK ** -0.5)))
            through_conv(pl.multiple_of(1024 + kh * DK, DK), l2_bwd(dkn_ref[:, pl.ds(off, DK)]))
            return carry

        lax.fori_loop(0, DN_KH, qk_body, 0)

        def v_body(hv, carry):
            dvv = dv_ref[:, pl.ds(pl.multiple_of(hv * DK, DK), DK)]
            through_conv(pl.multiple_of(2048 + hv * DK, DK), lambda s: dvv)
            return carry

        lax.fori_loop(0, DN_H, v_body, 0)

    full = lambda shape: pl.BlockSpec(shape, lambda c: (0, 0))
    blk = lambda w: pl.BlockSpec((CH, w), lambda c: (c, 0))
    return pl.pallas_call(
        body, grid=(nch,),
        in_specs=_chunk_specs(0) + [full((4, 4096)), full((1, DN_H)), full((1, DN_H)),
                                    blk(1024), blk(1024), blk(2048), blk(DN_H), blk(DN_H)],
        out_specs=[blk(4096), blk(DK), full((8, 4096)), full((1, DN_H)), full((1, DN_H))],
        out_shape=[SDS((rows, 4096), F32), SDS((rows, DK), F32), SDS((8, 4096), F32), SDS((1, DN_H), F32),
                   SDS((1, DN_H), F32)],
        compiler_params=_cp("arbitrary"), name="dn_prep_bwd")(
            udn, udn, udn, conv_w, a_log, dt_bias, dqn, dkn, dv, dbeta, dg)


def conv_bwd(dy, conv_w):
    rows = dy.shape[0]
    nch = rows // CH

    def body(dc_ref, dn_ref, cw_ref, dx_ref):
        nxt = jnp.where(pl.program_id(0) == nch - 1, 0.0, dn_ref[...])
        xx = jnp.concatenate([dc_ref[...], nxt], axis=0)
        dx = cw_ref[0:1, :] * xx[3:3 + CH]
        for j in range(1, 4):
            dx += cw_ref[j:j + 1, :] * xx[3 - j:3 - j + CH]
        dx_ref[...] = dx

    return pl.pallas_call(
        body, grid=(nch, 4),
        in_specs=[pl.BlockSpec((CH, 1024), lambda c, j: (c, j)),
                  pl.BlockSpec((8, 1024), lambda c, j: (jnp.minimum(8 * c + 8, rows // 8 - 1), j)),
                  pl.BlockSpec((4, 1024), lambda c, j: (0, j))],
        out_specs=pl.BlockSpec((CH, 1024), lambda c, j: (c, j)),
        out_shape=SDS((rows, 4096), F32), compiler_params=_cp("parallel", "parallel"), name="conv_bwd")(dy, dy, conv_w)


def local_step(x, target, w):
    seq = x.shape[0]
    bf = lambda a: a.astype(BF16)
    h0 = jnp.concatenate([jnp.zeros((PAD, D_MODEL), F32), w["meta_tokens"], x], axis=0)
    tgt = jnp.concatenate([jnp.zeros((BLK, D_MODEL), F32), target], axis=0)
    win = w["attn_w_in"]
    wq, wkv, wg = win[:, :1024], win[:, 1024:1280], win[:, 1280:]
    wa_in = bf(jnp.concatenate([wq, wg, wkv], axis=1))
    wa_out = bf(w["attn_w_out"])
    wd_in = bf(jnp.concatenate([w["dn_w_in"], jnp.zeros((D_MODEL, 96), F32)], axis=1))
    wd_out = bf(w["dn_w_out"])
    qw, kw, sinks = w["attn_q_norm_w"], w["attn_k_norm_w"], w["attn_sinks"]
    cw, al, dtb, ow = w["dn_conv_w"], w["dn_a_log"], w["dn_dt_bias"], w["dn_o_norm_w"]

    ua, xn0 = norm_matmul(h0, w["attn_norm_w"], wa_in, 2304, "attn_in")
    og = attn_fwd(ua, qw, kw, sinks)
    h1 = matmul_residual(og, wa_out, h0, "attn_out")
    ud, xn1 = norm_matmul(h1, w["dn_norm_w"], wd_in, 896, "dn_in")
    qn, kn, sv, gc, beta, u, wy, qe, ks, p, a = dn_prep(ud, cw, al, dtb)
    o, vn, st = dn_scan(u, wy, qe, ks, p, gc)
    dh2, on, ls = dn_out_fwd(o, ud, ow, wd_out, h1, tgt)
    loss = (0.5 / D_MODEL) * jnp.sum(ls)

    do, dz, dow = dn_out_bwd(dh2, wd_out, o, ud, ow)
    g_dn_out = wgrad(on, dh2, "dn_out_wgrad")
    dqn, dkn, dv, dbeta, dg = dn_scan_bwd(do, qn, kn, sv, gc, beta, a, u, wy, vn, qe, ks, st)
    dy, dba, dcw, dal, ddtb = dn_prep_bwd(ud, cw, al, dtb, dqn, dkn, dv, dbeta, dg)
    dxc = conv_bwd(dy, cw)
    dh1, dnw1 = in_proj_bwd([dxc, dz, dba], [wd_in[:, :4096], wd_in[:, 4096:6144], wd_in[:, 6144:]],
                            h1, w["dn_norm_w"], dh2, "dn_in_bwd")
    g_dn_in = jnp.concatenate([wgrad(xn1, dxc, "dn_in_wgrad_qkv"), wgrad(xn1, dz, "dn_in_wgrad_z"),
                               wgrad(xn1, dba, "dn_in_wgrad_ba")[:, :2 * DN_H]], axis=1)

    dog = matmul_nt(dh1, wa_out, "attn_out_bwd")
    g_attn_out = wgrad(og, dh1, "attn_out_wgrad")
    dq, dgate, dkv, dkvm, dqw, dkw, dsk = attn_bwd(ua, qw, kw, sinks, dog)
    dkv = dkv.at[PAD:BLK].add(dkvm)
    dh0, dnw0 = in_proj_bwd([dq, dgate, dkv], [wa_in[:, :1024], wa_in[:, 1024:2048], wa_in[:, 2048:]],
                            h0, w["attn_norm_w"], dh1, "attn_in_bwd")
    g_attn_in = jnp.concatenate([wgrad(xn0, dq, "attn_in_wgrad_q"), wgrad(xn0, dkv, "attn_in_wgrad_kv"),
                                 wgrad(xn0, dgate, "attn_in_wgrad_g")], axis=1)
    grads = {
        "meta_tokens": dh0[PAD:BLK], "attn_norm_w": jnp.sum(dnw0, axis=0), "attn_w_in": g_attn_in,
        "attn_q_norm_w": dqw, "attn_k_norm_w": dkw, "attn_sinks": dsk, "attn_w_out": g_attn_out,
        "dn_norm_w": jnp.sum(dnw1, axis=0), "dn_w_in": g_dn_in, "dn_conv_w": dcw[:4], "dn_a_log": dal,
        "dn_dt_bias": ddtb, "dn_o_norm_w": jnp.sum(dow, axis=0), "dn_w_out": g_dn_out,
    }
    return loss, dh0[BLK:BLK + seq], grads


WEIGHTS = ["meta_tokens", "attn_norm_w", "attn_w_in", "attn_q_norm_w", "attn_k_norm_w", "attn_sinks", "attn_w_out",
           "dn_norm_w", "dn_w_in", "dn_conv_w", "dn_a_log", "dn_dt_bias", "dn_o_norm_w", "dn_w_out"]
SHARDED = {"attn_w_in": ((1024, 2304), 1), "attn_w_out": ((1024, 1024), 0), "dn_w_in": ((1024, 6176), 1),
           "dn_w_out": ((2048, 1024), 0), "dn_conv_w": ((4, 4096), 1), "meta_tokens": ((16, 1024), 1),
           "dn_norm_w": ((1, 1024), 1)}
REPLICATED = {"attn_norm_w": 1024, "attn_q_norm_w": 64, "attn_k_norm_w": 64, "attn_sinks": 16, "dn_a_log": 16,
              "dn_dt_bias": 16, "dn_o_norm_w": 128}
N_CHIPS = 4
PACK_ROWS = 2912
HALF_ROWS = PACK_ROWS // 2
SMALL_ROWS = 8


def _shard_shape(name):
    (r, c), axis = SHARDED[name]
    return (r // N_CHIPS, c) if axis == 0 else (r, c // N_CHIPS)


def _pack(parts, rows):
    flat = jnp.concatenate([p.reshape(-1) for p in parts])
    return jnp.pad(flat, (0, rows * 1024 - flat.shape[0])).reshape(rows, 1024)


def pack_shard(shards):
    return _pack([shards[n] for n in SHARDED], PACK_ROWS)


def unpack_shard(buf):
    flat, out, pos = buf.reshape(-1), {}, 0
    for n in SHARDED:
        shp = _shard_shape(n)
        size = shp[0] * shp[1]
        out[n] = flat[pos:pos + size].reshape(shp)
        pos += size
    return out


def pack_small(vals):
    return _pack([vals[n] for n in REPLICATED], SMALL_ROWS)


def unpack_small(buf):
    flat, out, pos = buf.reshape(-1), {}, 0
    for n, size in REPLICATED.items():
        out[n] = flat[pos:pos + size].reshape(1, size)
        pos += size
    return out


ANY = pl.BlockSpec(memory_space=pl.ANY)


def _place():
    return lax.axis_index("x"), lax.axis_index("y"), lax.axis_index("c")


def chips_exchange(src, gather):
    r = src.shape[-2]

    def body(s_ref, o_ref, send_sems, recv_sems, local_sem):
        x, y, c = _place()
        me = 2 * x + y
        peers = [(1 - x, y), (x, 1 - y), (1 - x, 1 - y)]
        mine = pltpu.make_async_copy(s_ref if gather else s_ref.at[me], o_ref.at[me], local_sem)
        mine.start()

        def copy(k, to_block, from_block):
            px, py = peers[k]
            return pltpu.make_async_remote_copy(
                src_ref=s_ref if gather else s_ref.at[to_block], dst_ref=o_ref.at[from_block],
                send_sem=send_sems.at[k], recv_sem=recv_sems.at[k], device_id=(px, py, c), device_id_type=MESH)

        sends = [copy(k, 2 * px + py, me) for k, (px, py) in enumerate(peers)]
        for cp in sends:
            cp.start()
        for k, (px, py) in enumerate(peers):
            copy(k, me, 2 * px + py).wait_recv()
        for cp in sends:
            cp.wait_send()
        mine.wait()

    return pl.pallas_call(
        body, in_specs=[ANY], out_specs=ANY, out_shape=SDS((N_CHIPS, r, 1024), F32),
        scratch_shapes=[pltpu.SemaphoreType.DMA((3,)), pltpu.SemaphoreType.DMA((3,)), pltpu.SemaphoreType.DMA],
        name="chips_gather" if gather else "chips_exchange")(src)


def sibling_exchange(src, name):
    def body(s_ref, o_ref, send_sem, recv_sem):
        x, y, c = _place()
        cp = pltpu.make_async_remote_copy(src_ref=s_ref, dst_ref=o_ref, send_sem=send_sem, recv_sem=recv_sem,
                                          device_id=(x, y, 1 - c), device_id_type=MESH)
        cp.start()
        cp.wait()

    return pl.pallas_call(
        body, in_specs=[ANY], out_specs=ANY, out_shape=SDS(src.shape, src.dtype),
        scratch_shapes=[pltpu.SemaphoreType.DMA, pltpu.SemaphoreType.DMA], name=name)(src)


def all_gather_small(src):
    def body(s_ref, o_ref, send_sems, recv_sems, local_sem):
        x, y, c = _place()
        flips = [(fx, fy, fc) for fx in (0, 1) for fy in (0, 1) for fc in (0, 1)][1:]
        idx = lambda px, py, pc: 4 * px + 2 * py + pc
        mine = pltpu.make_async_copy(s_ref, o_ref.at[idx(x, y, c)], local_sem)
        mine.start()

        def peer(k):
            fx, fy, fc = flips[k]
            return (1 - x if fx else x, 1 - y if fy else y, 1 - c if fc else c)

        def copy(k, block):
            return pltpu.make_async_remote_copy(
                src_ref=s_ref, dst_ref=o_ref.at[block], send_sem=send_sems.at[k], recv_sem=recv_sems.at[k],
                device_id=peer(k), device_id_type=MESH)

        sends = [copy(k, idx(x, y, c)) for k in range(7)]
        for cp in sends:
            cp.start()
        for k in range(7):
            copy(k, idx(*peer(k))).wait_recv()
        for cp in sends:
            cp.wait_send()
        mine.wait()

    return pl.pallas_call(
        body, in_specs=[ANY], out_specs=ANY, out_shape=SDS((8,) + src.shape, F32),
        scratch_shapes=[pltpu.SemaphoreType.DMA((7,)), pltpu.SemaphoreType.DMA((7,)), pltpu.SemaphoreType.DMA],
        name="all_gather_small")(src)


def sum_blocks(t, name):
    n, r, _ = t.shape
    tm = 208 if r % 208 == 0 else r

    def body(t_ref, o_ref):
        acc = t_ref[0]
        for i in range(1, n):
            acc = acc + t_ref[i]
        o_ref[...] = acc

    return pl.pallas_call(
        body, grid=(r // tm,), in_specs=[pl.BlockSpec((n, tm, 1024), lambda i: (0, i, 0))],
        out_specs=pl.BlockSpec((tm, 1024), lambda i: (i, 0)), out_shape=SDS((r, 1024), F32),
        compiler_params=_cp("parallel"), name=name)(t)


def adamw(w, g, m, v):
    rows = w.shape[0]
    tm = rows // 5 if rows % 40 == 0 else rows

    def body(w_ref, g_ref, m_ref, v_ref, d_ref, nm_ref, nv_ref):
        g_ = g_ref[...]
        m_ = ADAM_B1 * m_ref[...] + (1.0 - ADAM_B1) * g_
        v_ = ADAM_B2 * v_ref[...] + (1.0 - ADAM_B2) * (g_ * g_)
        m_hat = m_ / (1.0 - ADAM_B1 ** ADAM_STEP)
        v_hat = v_ / (1.0 - ADAM_B2 ** ADAM_STEP)
        d_ref[...] = -ADAM_LR * (m_hat / (jnp.sqrt(v_hat) + ADAM_EPS) + ADAM_WD * w_ref[...])
        nm_ref[...] = m_
        nv_ref[...] = v_

    spec = pl.BlockSpec((tm, 1024), lambda i: (i, 0))
    return pl.pallas_call(
        body, grid=(rows // tm,), in_specs=[spec] * 4, out_specs=[spec] * 3,
        out_shape=[SDS((rows, 1024), F32)] * 3, compiler_params=_cp("parallel"), name="adamw")(w, g, m, v)


LAYERED = ("attn_w_in", "attn_w_out", "dn_w_in", "dn_conv_w", "dn_w_out")


def _two_d(name, a):
    return a[0] if name in LAYERED else a


def kernel(x, meta_tokens, attn_norm_w, attn_w_in, attn_q_norm_w, attn_k_norm_w, attn_sinks, attn_w_out, dn_norm_w, dn_w_in, dn_conv_w, dn_a_log, dn_dt_bias, dn_o_norm_w, dn_w_out, loss_target, m_meta_tokens, m_attn_norm_w, m_attn_w_in, m_attn_q_norm_w, m_attn_k_norm_w, m_attn_sinks, m_attn_w_out, m_dn_norm_w, m_dn_w_in, m_dn_conv_w, m_dn_a_log, m_dn_dt_bias, m_dn_o_norm_w, m_dn_w_out, v_meta_tokens, v_attn_norm_w, v_attn_w_in, v_attn_q_norm_w, v_attn_k_norm_w, v_attn_sinks, v_attn_w_out, v_dn_norm_w, v_dn_w_in, v_dn_conv_w, v_dn_a_log, v_dn_dt_bias, v_dn_o_norm_w, v_dn_w_out):
    given = dict(zip(WEIGHTS, (meta_tokens, attn_norm_w, attn_w_in, attn_q_norm_w, attn_k_norm_w, attn_sinks,
                               attn_w_out, dn_norm_w, dn_w_in, dn_conv_w, dn_a_log, dn_dt_bias, dn_o_norm_w, dn_w_out)))
    mom1 = dict(zip(WEIGHTS, (m_meta_tokens, m_attn_norm_w, m_attn_w_in, m_attn_q_norm_w, m_attn_k_norm_w,
                              m_attn_sinks, m_attn_w_out, m_dn_norm_w, m_dn_w_in, m_dn_conv_w, m_dn_a_log,
                              m_dn_dt_bias, m_dn_o_norm_w, m_dn_w_out)))
    mom2 = dict(zip(WEIGHTS, (v_meta_tokens, v_attn_norm_w, v_attn_w_in, v_attn_q_norm_w, v_attn_k_norm_w,
                              v_attn_sinks, v_attn_w_out, v_dn_norm_w, v_dn_w_in, v_dn_conv_w, v_dn_a_log,
                              v_dn_dt_bias, v_dn_o_norm_w, v_dn_w_out)))
    two_d = lambda d: {n: _two_d(n, a) for n, a in d.items()}
    given, mom1, mom2 = two_d(given), two_d(mom1), two_d(mom2)
    c = lax.axis_index("c")

    w_shard = pack_shard(given)
    gathered = chips_exchange(w_shard, True)
    per_chip = [unpack_shard(gathered[j]) for j in range(N_CHIPS)]
    full = {n: jnp.concatenate([pc[n] for pc in per_chip], axis=SHARDED[n][1]) for n in SHARDED}
    full.update({n: given[n] for n in REPLICATED})

    loss, dx, grads = local_step(x[0], loss_target[0], full)

    split = lambda n: jnp.split(grads[n], N_CHIPS, axis=SHARDED[n][1])
    g_all = jnp.stack([pack_shard({n: split(n)[j] for n in SHARDED}) for j in range(N_CHIPS)])
    keep = lax.dynamic_slice_in_dim(g_all, c * HALF_ROWS, HALF_ROWS, axis=1)
    give = lax.dynamic_slice_in_dim(g_all, (1 - c) * HALF_ROWS, HALF_ROWS, axis=1)
    got = sibling_exchange(give, "pair_exchange")
    pair = sum_blocks(jnp.stack([keep, got]).reshape(2, N_CHIPS * HALF_ROWS, 1024), "pair_sum")
    from_chips = chips_exchange(pair.reshape(N_CHIPS, HALF_ROWS, 1024), False)
    half = sum_blocks(from_chips, "chip_sum")
    other = sibling_exchange(half, "half_exchange")
    g_shard = jnp.where(c == 0, jnp.concatenate([half, other]), jnp.concatenate([other, half]))

    g_small = sum_blocks(all_gather_small(pack_small(grads)), "small_sum")

    stack = lambda big, small: jnp.concatenate([big, small])
    delta, new_m, new_v = adamw(stack(w_shard, pack_small(given)), stack(g_shard, g_small),
                                stack(pack_shard(mom1), pack_small(mom1)), stack(pack_shard(mom2), pack_small(mom2)))

    def unpack(buf):
        out = unpack_shard(buf[:PACK_ROWS])
        out.update(unpack_small(buf[PACK_ROWS:]))
        return [out[n][None] if n in LAYERED else out[n] for n in WEIGHTS]

    loss = lax.psum(loss, ("x", "y", "c"))
    return (loss, dx[None], *unpack(stack(g_shard, g_small)), *unpack(delta), *unpack(new_m), *unpack(new_v))
```

```python
import functools

import jax
import jax.numpy as jnp
from jax import lax
from jax.experimental import pallas as pl
from jax.experimental.pallas import tpu as pltpu

F32 = jnp.float32
BF16 = jnp.bfloat16
SDS = jax.ShapeDtypeStruct
MESH = pl.DeviceIdType.MESH

D_MODEL = 1024
N_META = 16
EPS = 1e-6
BLK = 128
CH = 64
PAD = BLK - N_META
HEADS = 16
HD = 64
KVW = 256
DN_H = 16
DN_KH = 8
DK = 128
SLOPES = [2.0 ** (-8.0 * (h + 1) / HEADS) for h in range(HEADS)]
NEG = -1e30
NT = (((1,), (1,)), ((), ()))
TN = (((0,), (0,)), ((), ()))
HI = lax.Precision.HIGHEST

ADAM_LR, ADAM_B1, ADAM_B2, ADAM_EPS, ADAM_WD, ADAM_STEP = 0.001, 0.9, 0.999, 1e-08, 0.01, 10

VMEM_LIMIT = 56 * 1024 * 1024


def _cp(*sem):
    return pltpu.CompilerParams(dimension_semantics=sem, vmem_limit_bytes=VMEM_LIMIT)


def _row_tile(rows):
    for t in (384, 256, 128):
        if rows % t == 0:
            return t
    raise ValueError(rows)


def _dot(a, b, dims=None, precision=None):
    if dims is None:
        return jnp.dot(a, b, preferred_element_type=F32, precision=precision)
    return lax.dot_general(a, b, dims, preferred_element_type=F32, precision=precision)


def _silu(x):
    return x * jax.nn.sigmoid(x)


def _dsilu(x):
    s = jax.nn.sigmoid(x)
    return s * (1.0 + x * (1.0 - s))


def _rms(x):
    return lax.rsqrt(jnp.mean(x * x, axis=-1, keepdims=True) + EPS)


def norm_matmul(h, nw, w, tn, name):
    rows, k = h.shape
    n = w.shape[1]
    tm = _row_tile(rows)

    def body(h_ref, nw_ref, w_ref, o_ref, xn_ref, xs):
        @pl.when(pl.program_id(1) == 0)
        def _():
            x = h_ref[...]
            xn = (x * _rms(x) * nw_ref[...]).astype(BF16)
            xs[...] = xn
            xn_ref[...] = xn

        o_ref[...] = _dot(xs[...], w_ref[...])

    return pl.pallas_call(
        body, grid=(rows // tm, n // tn),
        in_specs=[pl.BlockSpec((tm, k), lambda i, j: (i, 0)), pl.BlockSpec((1, k), lambda i, j: (0, 0)),
                  pl.BlockSpec((k, tn), lambda i, j: (0, j))],
        out_specs=[pl.BlockSpec((tm, tn), lambda i, j: (i, j)), pl.BlockSpec((tm, k), lambda i, j: (i, 0))],
        out_shape=[SDS((rows, n), F32), SDS((rows, k), BF16)],
        scratch_shapes=[pltpu.VMEM((tm, k), BF16)],
        compiler_params=_cp("parallel", "arbitrary"), name=name)(h, nw, w)


def matmul_residual(a, w, res, name):
    rows, k = a.shape
    n = w.shape[1]
    tm = _row_tile(rows)

    def body(a_ref, w_ref, r_ref, o_ref):
        o_ref[...] = r_ref[...] + _dot(a_ref[...], w_ref[...])

    return pl.pallas_call(
        body, grid=(rows // tm,),
        in_specs=[pl.BlockSpec((tm, k), lambda i: (i, 0)), pl.BlockSpec((k, n), lambda i: (0, 0)),
                  pl.BlockSpec((tm, n), lambda i: (i, 0))],
        out_specs=pl.BlockSpec((tm, n), lambda i: (i, 0)),
        out_shape=SDS((rows, n), F32), compiler_params=_cp("parallel"), name=name)(a, w, res)


def wgrad(a, b, name):
    rows, k = a.shape
    n = b.shape[1]
    tm = _row_tile(rows)
    tn = min(n, 1024)

    def body(a_ref, b_ref, o_ref):
        @pl.when(pl.program_id(1) == 0)
        def _():
            o_ref[...] = jnp.zeros_like(o_ref)

        o_ref[...] += _dot(a_ref[...], b_ref[...].astype(BF16), TN)

    return pl.pallas_call(
        body, grid=(n // tn, rows // tm),
        in_specs=[pl.BlockSpec((tm, k), lambda j, i: (i, 0)), pl.BlockSpec((tm, tn), lambda j, i: (i, j))],
        out_specs=pl.BlockSpec((k, tn), lambda j, i: (0, j)),
        out_shape=SDS((k, n), F32), compiler_params=_cp("parallel", "arbitrary"), name=name)(a, b)


def in_proj_bwd(dus, ws, h, nw, dh_next, name):
    rows, k = h.shape
    tm = 128
    nd = len(dus)
    nt = rows // tm

    def body(*refs):
        du_refs, w_refs = refs[:nd], refs[nd:2 * nd]
        h_ref, nw_ref, dhn_ref, dh_ref, dnw_ref = refs[2 * nd:]
        dxn = _dot(du_refs[0][...].astype(BF16), w_refs[0][...], NT)
        for du_ref, w_ref in zip(du_refs[1:], w_refs[1:]):
            dxn += _dot(du_ref[...].astype(BF16), w_ref[...], NT)
        x = h_ref[...]
        r = _rms(x)
        y = x * r
        gy = dxn * nw_ref[...]
        dh_ref[...] = dhn_ref[...] + r * (gy - y * jnp.mean(y * gy, axis=-1, keepdims=True))
        dnw_ref[0] = jnp.sum(dxn * y, axis=0, keepdims=True)

    in_specs = [pl.BlockSpec((tm, du.shape[1]), lambda i: (i, 0)) for du in dus]
    in_specs += [pl.BlockSpec(w.shape, lambda i: (0, 0)) for w in ws]
    in_specs += [pl.BlockSpec((tm, k), lambda i: (i, 0)), pl.BlockSpec((1, k), lambda i: (0, 0)),
                 pl.BlockSpec((tm, k), lambda i: (i, 0))]
    return pl.pallas_call(
        body, grid=(nt,), in_specs=in_specs,
        out_specs=[pl.BlockSpec((tm, k), lambda i: (i, 0)), pl.BlockSpec((1, 1, k), lambda i: (i, 0, 0))],
        out_shape=[SDS((rows, k), F32), SDS((nt, 1, k), F32)],
        compiler_params=_cp("parallel"), name=name)(*dus, *ws, h, nw, dh_next)


def matmul_nt(a, w, name):
    rows, k = a.shape
    n = w.shape[0]
    tm = _row_tile(rows)

    def body(a_ref, w_ref, o_ref):
        o_ref[...] = _dot(a_ref[...].astype(BF16), w_ref[...], NT)

    return pl.pallas_call(
        body, grid=(rows // tm,),
        in_specs=[pl.BlockSpec((tm, k), lambda i: (i, 0)), pl.BlockSpec((n, k), lambda i: (0, 0))],
        out_specs=pl.BlockSpec((tm, n), lambda i: (i, 0)),
        out_shape=SDS((rows, n), F32), compiler_params=_cp("parallel"), name=name)(a, w)


SUB = 64
GRP = 8
TR = GRP * SUB
NBAND = 192
TK = 256


def _tile_bias(n, sb):
    r = lax.broadcasted_iota(jnp.int32, (TR, TK), 0)
    c = lax.broadcasted_iota(jnp.int32, (TR, TK), 1)
    qi = r & (SUB - 1)
    d = BLK + qi - c
    dm = n * BLK + SUB * sb - PAD + NBAND + qi - c
    band = c < NBAND
    valid = (band & (d >= 0) & (d < BLK) & (c >= 2 * BLK - BLK * n - SUB * sb)) | (
        (c >= NBAND) & (c < NBAND + N_META) & (dm >= 0))
    return valid, jnp.where(band, d, jnp.minimum(dm, BLK)).astype(F32)


def _group_col(vals):
    g = lax.broadcasted_iota(jnp.int32, (TR, 1), 0) >> 6
    col = jnp.zeros((TR, 1), F32)
    for gi, v in enumerate(vals):
        col = jnp.where(g == gi, v, col)
    return col


def _stack_heads(ref, sb, kvh):
    return jnp.concatenate(
        [ref[SUB * sb:SUB * sb + SUB, HD * (GRP * kvh + g):HD * (GRP * kvh + g) + HD] for g in range(GRP)], axis=0)


def _unstack_heads(parts):
    return jnp.concatenate([parts[kvh][SUB * g:SUB * g + SUB] for kvh in range(2) for g in range(GRP)], axis=1)


def _tile_keys(band, meta, sb):
    return jnp.concatenate([band[SUB * sb:SUB * sb + NBAND], meta,
                            jnp.zeros((TK - NBAND - N_META, HD), band.dtype)], axis=0)


def _row_sums(x):
    ones = jnp.ones((x.shape[1], 128), BF16)
    hi = x.astype(BF16)
    lo = (x - hi.astype(F32)).astype(BF16)
    return _dot(hi, ones) + _dot(lo, ones)


def _rms_stack(q):
    return lax.rsqrt(_row_sums(q * q)[:, :HD] * (1.0 / HD) + EPS)


def _fill_bias(bias_scr, n):
    @pl.when(n <= 2)
    def _():
        for sb in range(2):
            valid, dist = _tile_bias(n, sb)
            for kvh in range(2):
                slope_col = _group_col([SLOPES[GRP * kvh + g] for g in range(GRP)])
                bias_scr[2 * sb + kvh] = jnp.where(valid, -slope_col * dist, NEG)


def _tile_probs(qn16, k16, bias, sink_col):
    s = _dot(qn16, k16, NT) * (HD ** -0.5) + bias
    mx = jnp.maximum(jnp.max(s.astype(BF16), axis=-1, keepdims=True).astype(F32), sink_col)
    e = jnp.exp(s - mx)
    es = jnp.exp(sink_col - mx)
    inv = 1.0 / (_row_sums(e) + es)
    return e * jnp.concatenate([inv, inv], axis=1), es * inv


def _kv_heads(kvb, kvm, kw_):
    out = []
    for kvh in range(2):
        kb, km = kvb[:, HD * kvh:HD * kvh + HD], kvm[:, HD * kvh:HD * kvh + HD]
        out.append(((kb * _rms(kb) * kw_).astype(BF16), (km * _rms(km) * kw_).astype(BF16),
                    kvb[:, BLK + HD * kvh:BLK + HD * kvh + HD].astype(BF16),
                    kvm[:, BLK + HD * kvh:BLK + HD * kvh + HD].astype(BF16)))
    return out


def _kv_specs(nblk, clamp):
    cur = (lambda n: (jnp.minimum(n, nblk - 1), 8)) if clamp else (lambda n: (n, 8))
    return [pl.BlockSpec((BLK, KVW), cur),
            pl.BlockSpec((BLK, KVW), lambda n: (jnp.maximum(n - 1, 0), 8)),
            pl.BlockSpec((N_META, KVW), lambda n: (PAD // N_META, 8))]


def attn_fwd(u, qw, kw, sinks):
    rows = u.shape[0]
    nblk = rows // BLK

    def body(q_ref, g_ref, kvc_ref, kvp_ref, kvm_ref, qw_ref, kw_ref, sk_ref, og_ref, bias_scr):
        _fill_bias(bias_scr, pl.program_id(0))
        qw_ = qw_ref[...]
        kv = _kv_heads(jnp.concatenate([kvp_ref[...], kvc_ref[...]], axis=0), kvm_ref[...], kw_ref[...])
        sink_cols = [_group_col([sk_ref[:, GRP * kvh + g:GRP * kvh + g + 1] for g in range(GRP)]) for kvh in range(2)]
        for sb in range(2):
            parts = []
            for kvh in range(2):
                knb, knm, vb, vm = kv[kvh]
                q = _stack_heads(q_ref, sb, kvh)
                qn16 = (q * _rms_stack(q) * qw_).astype(BF16)
                p, _ = _tile_probs(qn16, _tile_keys(knb, knm, sb), bias_scr[2 * sb + kvh], sink_cols[kvh])
                parts.append(_dot(p.astype(BF16), _tile_keys(vb, vm, sb)))
            rows = slice(SUB * sb, SUB * sb + SUB)
            og_ref[rows, :] = (_unstack_heads(parts) * _silu(g_ref[rows, :])).astype(BF16)

    small = lambda w: pl.BlockSpec((1, w), lambda n: (0, 0))
    return pl.pallas_call(
        body, grid=(nblk,),
        in_specs=[pl.BlockSpec((BLK, 1024), lambda n: (n, 0)), pl.BlockSpec((BLK, 1024), lambda n: (n, 1))]
        + _kv_specs(nblk, False) + [small(HD), small(HD), small(HEADS)],
        out_specs=pl.BlockSpec((BLK, 1024), lambda n: (n, 0)),
        out_shape=SDS((rows, 1024), BF16), scratch_shapes=[pltpu.VMEM((4, TR, TK), F32)],
        compiler_params=_cp("arbitrary"), name="attn_fwd")(u, u, u, u, u, qw, kw, sinks)


def attn_bwd(u, qw, kw, sinks, dog):
    rows = u.shape[0]
    nblk = rows // BLK

    def knorm_bwd(k, dkn, kw_):
        r = _rms(k)
        y = k * r
        gy = dkn * kw_
        return r * (gy - y * jnp.mean(y * gy, axis=-1, keepdims=True)), jnp.sum(dkn * y, axis=0, keepdims=True)

    def body(q_ref, g_ref, dog_ref, kvc_ref, kvp_ref, kvm_ref, qw_ref, kw_ref, sk_ref,
             dq_ref, dg_ref, dkv_ref, dkvm_ref, dqw_ref, dkw_ref, dsk_ref, carry, prevp, curp, metap, bias_scr):
        n = pl.program_id(0)
        qw_, kw_ = qw_ref[...], kw_ref[...]
        _fill_bias(bias_scr, n)

        @pl.when(n == 0)
        def _():
            carry[...] = jnp.zeros_like(carry)
            metap[...] = jnp.zeros_like(metap)
            dqw_ref[...] = jnp.zeros_like(dqw_ref)
            dkw_ref[...] = jnp.zeros_like(dkw_ref)
            dsk_ref[...] = jnp.zeros_like(dsk_ref)

        @pl.when(n == nblk)
        def _():
            prevp[...] = jnp.zeros_like(prevp)
            curp[...] = jnp.zeros_like(curp)

        @pl.when(n < nblk)
        def _():
            kv = _kv_heads(jnp.concatenate([kvp_ref[...], kvc_ref[...]], axis=0), kvm_ref[...], kw_)
            sink_cols = [_group_col([sk_ref[:, GRP * kvh + g:GRP * kvh + g + 1] for g in range(GRP)])
                         for kvh in range(2)]
            lane = lax.broadcasted_iota(jnp.int32, (1, HEADS), 1)
            dqw = jnp.zeros((1, HD), F32)
            dsk = jnp.zeros((1, HEADS), F32)
            band_parts = [jnp.zeros((2 * BLK, HD), F32) for _ in range(4)]
            meta_parts = [jnp.zeros((N_META, HD), F32) for _ in range(4)]

            def widen(x, sb):
                z = jnp.zeros((2 * BLK - NBAND, HD), F32)
                return jnp.concatenate([x, z] if sb == 0 else [z, x], axis=0)

            for sb in range(2):
                rows = slice(SUB * sb, SUB * sb + SUB)
                dq_parts, dg_parts = [], []
                for kvh in range(2):
                    knb, knm, vb, vm = kv[kvh]
                    k16, v16 = _tile_keys(knb, knm, sb), _tile_keys(vb, vm, sb)
                    q = _stack_heads(q_ref, sb, kvh)
                    r = _rms_stack(q)
                    y = q * r
                    qn16 = (y * qw_).astype(BF16)
                    p, ps = _tile_probs(qn16, k16, bias_scr[2 * sb + kvh], sink_cols[kvh])
                    p16 = p.astype(BF16)
                    o = _dot(p16, v16)
                    gate = _stack_heads(g_ref, sb, kvh)
                    dog_ = _stack_heads(dog_ref, sb, kvh)
                    dg_parts.append(dog_ * o * _dsilu(gate))
                    do_ = dog_ * _silu(gate)
                    do16 = do_.astype(BF16)
                    dp = _dot(do16, v16, NT)
                    delta = _row_sums(p * dp)
                    ds16 = (p * (dp - jnp.concatenate([delta, delta], axis=1))).astype(BF16)
                    dsink = -ps * delta
                    for g in range(GRP):
                        dsk += jnp.where(lane == GRP * kvh + g,
                                         jnp.sum(dsink[SUB * g:SUB * g + SUB, :HEADS], axis=0, keepdims=True), 0.0)
                    dqn = _dot(ds16, k16) * (HD ** -0.5)
                    dk = (_dot((y * qw_).T.astype(BF16), ds16) * (HD ** -0.5)).T
                    dv = _dot(do_.T.astype(BF16), p16).T
                    band_parts[kvh] += widen(dk[:NBAND], sb)
                    band_parts[2 + kvh] += widen(dv[:NBAND], sb)
                    meta_parts[kvh] += dk[NBAND:NBAND + N_META]
                    meta_parts[2 + kvh] += dv[NBAND:NBAND + N_META]
                    gy = dqn * qw_
                    dq_parts.append(r * (gy - y * (_row_sums(y * gy)[:, :HD] * (1.0 / HD))))
                    dqw += jnp.sum(dqn * y, axis=0, keepdims=True)
                dq_ref[rows, :] = _unstack_heads(dq_parts)
                dg_ref[rows, :] = _unstack_heads(dg_parts)
            band = jnp.concatenate(band_parts, axis=1)
            prevp[...] = band[:BLK]
            curp[...] = band[BLK:]
            metap[...] += jnp.concatenate(meta_parts, axis=1)
            dqw_ref[...] += dqw
            dsk_ref[...] += dsk

        tot = carry[...] + prevp[...]
        kprev = kvp_ref[...]
        dk0, w0 = knorm_bwd(kprev[:, 0:HD], tot[:, 0:HD], kw_)
        dk1, w1 = knorm_bwd(kprev[:, HD:2 * HD], tot[:, HD:2 * HD], kw_)
        dkv_ref[...] = jnp.concatenate([dk0, dk1, tot[:, 2 * HD:]], axis=1)
        dkw_ref[...] += w0 + w1
        carry[...] = curp[...]

        @pl.when(n == nblk)
        def _():
            mt = metap[...]
            km = kvm_ref[...]
            m0, v0 = knorm_bwd(km[:, 0:HD], mt[:, 0:HD], kw_)
            m1, v1 = knorm_bwd(km[:, HD:2 * HD], mt[:, HD:2 * HD], kw_)
            dkvm_ref[...] = jnp.concatenate([m0, m1, mt[:, 2 * HD:]], axis=1)
            dkw_ref[...] += v0 + v1

    small = lambda w: pl.BlockSpec((1, w), lambda n: (0, 0))
    cl = lambda n: jnp.minimum(n, nblk - 1)
    return pl.pallas_call(
        body, grid=(nblk + 1,),
        in_specs=[pl.BlockSpec((BLK, 1024), lambda n: (cl(n), 0)), pl.BlockSpec((BLK, 1024), lambda n: (cl(n), 1)),
                  pl.BlockSpec((BLK, 1024), lambda n: (cl(n), 0))]
        + _kv_specs(nblk, True) + [small(HD), small(HD), small(HEADS)],
        out_specs=[pl.BlockSpec((BLK, 1024), lambda n: (cl(n), 0)), pl.BlockSpec((BLK, 1024), lambda n: (cl(n), 0)),
                   pl.BlockSpec((BLK, KVW), lambda n: (jnp.maximum(n - 1, 0), 0)),
                   pl.BlockSpec((N_META, KVW), lambda n: (0, 0)), small(HD), small(HD), small(HEADS)],
        out_shape=[SDS((rows, 1024), F32), SDS((rows, 1024), F32), SDS((rows, KVW), F32), SDS((N_META, KVW), F32),
                   SDS((1, HD), F32), SDS((1, HD), F32), SDS((1, HEADS), F32)],
        scratch_shapes=[pltpu.VMEM((BLK, KVW), F32), pltpu.VMEM((BLK, KVW), F32), pltpu.VMEM((BLK, KVW), F32),
                        pltpu.VMEM((N_META, KVW), F32), pltpu.VMEM((4, TR, TK), F32)],
        compiler_params=_cp("arbitrary"), name="attn_bwd")(u, u, dog, u, u, u, qw, kw, sinks)


def _tri_inv(m, ii, jj):
    eye = (ii == jj).astype(F32)
    mb = jnp.where((ii >> 3) == (jj >> 3), m, 0.0)
    m2 = _bdot(mb, mb, "nn", True)
    m4 = _bdot(m2, m2, "nn", True)
    x = _bdot(_bdot(eye - mb, eye + m2, "nn", True), eye + m4, "nn", True)
    for sh in (3, 4, 5):
        lb = jnp.where(((ii >> (sh + 1)) == (jj >> (sh + 1))) & ((ii >> sh) != (jj >> sh)), m, 0.0)
        x = x - _bdot(_bdot(x, lb, "nn", True), x, "nn", True)
    return x


HB = 8


def _bdot(a, b, kind, split=False):
    dims = {"nn": ((2,), (1,)), "nt": ((2,), (2,)), "tn": ((1,), (1,))}[kind]
    dg = lambda p, q: lax.dot_general(p, q, (dims, ((0,), (0,))), preferred_element_type=F32)
    if not split:
        return dg(a, b)
    ah, bh = a.astype(BF16), b.astype(BF16)
    al, bl = (a - ah.astype(F32)).astype(BF16), (b - bh.astype(F32)).astype(BF16)
    return (dg(ah, bl) + dg(al, bh)) + dg(ah, bh)


def _head_cols(hv, beta, gc, gct, lane):
    sel = lane == hv
    return _pick(beta, sel), _pick(gc, sel), gct[pl.ds(hv, 1), :]


def _conv_group(xc_ref, xp_ref, cw_ref, off, first):
    xp = jnp.where(first, 0.0, xp_ref[:, pl.ds(off, DK)])
    xx = jnp.concatenate([xp, xc_ref[:, pl.ds(off, DK)]], axis=0)
    y = cw_ref[0:1, pl.ds(off, DK)] * xx[5:5 + CH]
    for j in range(1, 4):
        y += cw_ref[j:j + 1, pl.ds(off, DK)] * xx[5 + j:5 + j + CH]
    return xx, y


def _gates(ba, al, dtb, c):
    row = c * CH + lax.broadcasted_iota(jnp.int32, (CH, DN_H), 0)
    real = row >= PAD
    xa = ba[:, DN_H:2 * DN_H] + dtb
    beta = jnp.where(real, jax.nn.sigmoid(ba[:, 0:DN_H]), 0.0)
    g = jnp.where(real, -jnp.exp(al) * jax.nn.softplus(xa), 0.0)
    return real, xa, beta, g


def _pick(x, sel):
    return jnp.sum(jnp.where(sel, x, 0.0), axis=1, keepdims=True)


def _chunk_specs(width_blocks):
    return [pl.BlockSpec((CH, 4096), lambda c: (c, 0)),
            pl.BlockSpec((8, 4096), lambda c: (jnp.maximum(8 * c - 1, 0), 0)),
            pl.BlockSpec((CH, DK), lambda c: (c, 48))]


def dn_prep(udn, conv_w, a_log, dt_bias):
    rows = udn.shape[0]
    nch = rows // CH

    def body(xc_ref, xp_ref, ba_ref, cw_ref, al_ref, dtb_ref,
             qn_ref, kn_ref, sv_ref, gc_ref, beta_ref, u_ref, w_ref, qe_ref, ks_ref, p_ref, a_ref, gct):
        c = pl.program_id(0)
        first = c == 0
        _, _, beta, g = _gates(ba_ref[...], al_ref[...], dtb_ref[...], c)
        ii = lax.broadcasted_iota(jnp.int32, (CH, CH), 0)
        jj = lax.broadcasted_iota(jnp.int32, (CH, CH), 1)
        gc = _dot((ii >= jj).astype(F32), g, precision=HI)
        gc_ref[...] = gc
        beta_ref[...] = beta
        gct[...] = gc.T

        def qk_body(kh, carry):
            off = pl.multiple_of(kh * DK, DK)
            _, yq = _conv_group(xc_ref, xp_ref, cw_ref, off, first)
            sq = _silu(yq)
            qn_ref[:, pl.ds(off, DK)] = sq * lax.rsqrt(jnp.sum(sq * sq, axis=-1, keepdims=True) + EPS) * (DK ** -0.5)
            _, yk = _conv_group(xc_ref, xp_ref, cw_ref, pl.multiple_of(1024 + kh * DK, DK), first)
            sk = _silu(yk)
            kn_ref[:, pl.ds(off, DK)] = sk * lax.rsqrt(jnp.sum(sk * sk, axis=-1, keepdims=True) + EPS)
            return carry

        lax.fori_loop(0, DN_KH, qk_body, 0)
        lane = lax.broadcasted_iota(jnp.int32, (CH, DN_H), 1)
        zpad = jnp.zeros((CH, DK - CH), F32)

        def v_group(grp, carry):
            offs, ks_, qs_, vs_, cols = [], [], [], [], []
            for i in range(HB):
                hv = grp * HB + i
                offs.append(pl.multiple_of(hv * DK, DK))
                koff = pl.multiple_of((grp * (HB // 2) + i // 2) * DK, DK)
                _, yv = _conv_group(xc_ref, xp_ref, cw_ref, pl.multiple_of(2048 + hv * DK, DK), first)
                vs_.append(_silu(yv))
                sv_ref[:, pl.ds(offs[i], DK)] = vs_[i]
                ks_.append(kn_ref[:, pl.ds(koff, DK)])
                qs_.append(qn_ref[:, pl.ds(koff, DK)])
                cols.append(_head_cols(hv, beta, gc, gct, lane))
            k, q, v = jnp.stack(ks_), jnp.stack(qs_), jnp.stack(vs_)
            beta_c, gc_c, gc_r = (jnp.stack([c_[j] for c_ in cols]) for j in range(3))
            dec = jnp.exp(jnp.where(ii >= jj, gc_c - gc_r, NEG))
            eg = jnp.exp(gc_c)
            kb = k * beta_c
            k16 = k.astype(BF16)
            m = jnp.where(ii > jj, _bdot(kb.astype(BF16), k16, "nt") * dec, 0.0)
            a = _tri_inv(m, ii, jj)
            uw = _bdot(a, jnp.concatenate([v * beta_c, kb * eg], axis=2), "nn", True)
            p = _bdot(q.astype(BF16), k16, "nt") * dec
            qe = (q * eg).astype(BF16)
            ksx = (k * jnp.exp(gc_c[:, CH - 1:CH, :] - gc_c)).astype(BF16)
            for i in range(HB):
                sl = pl.ds(offs[i], DK)
                u_ref[:, sl] = uw[i, :, :DK]
                w_ref[:, sl] = uw[i, :, DK:]
                qe_ref[:, sl] = qe[i]
                ks_ref[:, sl] = ksx[i]
                p_ref[:, sl] = jnp.concatenate([p[i], zpad], axis=1).astype(BF16)
                a_ref[:, sl] = jnp.concatenate([a[i], zpad], axis=1)
            return carry

        lax.fori_loop(0, DN_H // HB, v_group, 0)

    full = lambda shape: pl.BlockSpec(shape, lambda c: (0, 0))
    blk = lambda w: pl.BlockSpec((CH, w), lambda c: (c, 0))
    return pl.pallas_call(
        body, grid=(nch,),
        in_specs=_chunk_specs(0) + [full((4, 4096)), full((1, DN_H)), full((1, DN_H))],
        out_specs=[blk(1024), blk(1024), blk(2048), blk(DN_H), blk(DN_H), blk(2048), blk(2048), blk(2048), blk(2048),
                   blk(2048), blk(2048)],
        out_shape=[SDS((rows, 1024), F32), SDS((rows, 1024), F32), SDS((rows, 2048), F32), SDS((rows, DN_H), F32),
                   SDS((rows, DN_H), F32), SDS((rows, 2048), F32), SDS((rows, 2048), F32), SDS((rows, 2048), BF16),
                   SDS((rows, 2048), BF16), SDS((rows, 2048), BF16), SDS((rows, 2048), F32)],
        scratch_shapes=[pltpu.VMEM((DN_H, CH), F32)],
        compiler_params=_cp("parallel"), name="dn_prep")(udn, udn, udn, conv_w, a_log, dt_bias)


def dn_scan(u, w, qe, ks, p, gc):
    rows = u.shape[0]
    nch = rows // CH

    def body(u_ref, w_ref, qe_ref, ks_ref, p_ref, gc_ref, o_ref, vn_ref, st_ref, s_scr):
        @pl.when(pl.program_id(0) == 0)
        def _():
            s_scr[...] = jnp.zeros_like(s_scr)

        gl_row = gc_ref[CH - 1:CH, :]
        lane = lax.broadcasted_iota(jnp.int32, (1, DN_H), 1)

        def group(grp, carry):
            base = grp * HB
            sls = [pl.ds(pl.multiple_of((base + i) * DK, DK), DK) for i in range(HB)]
            heads = lambda ref: jnp.stack([ref[:, sl] for sl in sls])
            s = s_scr[pl.ds(base, HB)]
            st_ref[0, pl.ds(base, HB)] = s
            s16 = s.astype(BF16)
            vn = heads(u_ref) - _bdot(heads(w_ref).astype(BF16), s16, "nn")
            vn16 = vn.astype(BF16)
            o = _bdot(heads(qe_ref), s16, "nn") + _bdot(heads(p_ref)[:, :, 0:CH], vn16, "nn")
            egl = jnp.exp(jnp.stack([_pick(gl_row, lane == base + i) for i in range(HB)]))
            s_scr[pl.ds(base, HB)] = s * egl + _bdot(heads(ks_ref), vn16, "tn")
            for i in range(HB):
                vn_ref[:, sls[i]] = vn16[i]
                o_ref[:, sls[i]] = o[i]
            return carry

        lax.fori_loop(0, DN_H // HB, group, 0)

    blk = lambda wd: pl.BlockSpec((CH, wd), lambda c: (c, 0))
    return pl.pallas_call(
        body, grid=(nch,),
        in_specs=[blk(2048)] * 5 + [blk(DN_H)],
        out_specs=[blk(2048), blk(2048), pl.BlockSpec((1, DN_H, DK, DK), lambda c: (c, 0, 0, 0))],
        out_shape=[SDS((rows, 2048), F32), SDS((rows, 2048), BF16), SDS((nch, DN_H, DK, DK), F32)],
        scratch_shapes=[pltpu.VMEM((DN_H, DK, DK), F32)],
        compiler_params=_cp("arbitrary"), name="dn_scan")(u, w, qe, ks, p, gc)


def dn_out_fwd(o, udn, ow, wout, h1, tgt):
    rows = o.shape[0]
    tm = _row_tile(rows)
    nt = rows // tm

    def body(o_ref, z_ref, ow_ref, w_ref, h_ref, t_ref, dh_ref, on_ref, ls_ref):
        for hv in range(DN_H):
            sl = slice(hv * DK, hv * DK + DK)
            oh = o_ref[:, sl]
            on_ref[:, sl] = (oh * _rms(oh) * ow_ref[...] * _silu(z_ref[:, sl])).astype(BF16)
        h2 = h_ref[...] + _dot(on_ref[...], w_ref[...])
        row = pl.program_id(0) * tm + lax.broadcasted_iota(jnp.int32, (tm, 1), 0)
        err = jnp.where(row >= BLK, h2 - t_ref[...], 0.0)
        dh_ref[...] = err * (1.0 / D_MODEL)
        ls_ref[0] = jnp.sum(err * err, axis=0, keepdims=True)

    return pl.pallas_call(
        body, grid=(nt,),
        in_specs=[pl.BlockSpec((tm, 2048), lambda i: (i, 0)), pl.BlockSpec((tm, 2048), lambda i: (i, 2)),
                  pl.BlockSpec((1, DK), lambda i: (0, 0)), pl.BlockSpec((2048, D_MODEL), lambda i: (0, 0)),
                  pl.BlockSpec((tm, D_MODEL), lambda i: (i, 0)), pl.BlockSpec((tm, D_MODEL), lambda i: (i, 0))],
        out_specs=[pl.BlockSpec((tm, D_MODEL), lambda i: (i, 0)), pl.BlockSpec((tm, 2048), lambda i: (i, 0)),
                   pl.BlockSpec((1, 1, D_MODEL), lambda i: (i, 0, 0))],
        out_shape=[SDS((rows, D_MODEL), F32), SDS((rows, 2048), BF16), SDS((nt, 1, D_MODEL), F32)],
        compiler_params=_cp("parallel"), name="dn_out_fwd")(o, udn, ow, wout, h1, tgt)


def dn_out_bwd(dh2, wout, o, udn, ow):
    rows = o.shape[0]
    tm = _row_tile(rows)
    nt = rows // tm

    def body(dh_ref, w_ref, o_ref, z_ref, ow_ref, do_ref, dz_ref, dow_ref):
        don = _dot(dh_ref[...].astype(BF16), w_ref[...], NT)
        ow_ = ow_ref[...]
        dow = jnp.zeros((1, DK), F32)
        for hv in range(DN_H):
            sl = slice(hv * DK, hv * DK + DK)
            oh = o_ref[:, sl]
            r = _rms(oh)
            y = oh * r
            z = z_ref[:, sl]
            dn = don[:, sl] * _silu(z)
            dz_ref[:, sl] = don[:, sl] * (y * ow_) * _dsilu(z)
            dy = dn * ow_
            do_ref[:, sl] = r * (dy - y * jnp.mean(y * dy, axis=-1, keepdims=True))
            dow += jnp.sum(dn * y, axis=0, keepdims=True)
        dow_ref[0] = dow

    return pl.pallas_call(
        body, grid=(nt,),
        in_specs=[pl.BlockSpec((tm, D_MODEL), lambda i: (i, 0)), pl.BlockSpec((2048, D_MODEL), lambda i: (0, 0)),
                  pl.BlockSpec((tm, 2048), lambda i: (i, 0)), pl.BlockSpec((tm, 2048), lambda i: (i, 2)),
                  pl.BlockSpec((1, DK), lambda i: (0, 0))],
        out_specs=[pl.BlockSpec((tm, 2048), lambda i: (i, 0)), pl.BlockSpec((tm, 2048), lambda i: (i, 0)),
                   pl.BlockSpec((1, 1, DK), lambda i: (i, 0, 0))],
        out_shape=[SDS((rows, 2048), F32), SDS((rows, 2048), F32), SDS((nt, 1, DK), F32)],
        compiler_params=_cp("parallel"), name="dn_out_bwd")(dh2, wout, o, udn, ow)


def dn_scan_bwd(do, qn, kn, sv, gc, beta, a, u, w, vn, qe, ks, st):
    rows = do.shape[0]
    nch = rows // CH

    def body(do_ref, q_ref, k_ref, v_ref, gc_ref, beta_ref, a_ref, u_ref, w_ref, vn_ref, qe_ref, ks_ref, st_ref,
             dq_ref, dk_ref, dv_ref, dbeta_ref, dg_ref, ds_scr, gct):
        @pl.when(pl.program_id(0) == 0)
        def _():
            ds_scr[...] = jnp.zeros_like(ds_scr)

        gc, beta = gc_ref[...], beta_ref[...]
        gct[...] = gc.T
        ii = lax.broadcasted_iota(jnp.int32, (CH, CH), 0)
        jj = lax.broadcasted_iota(jnp.int32, (CH, CH), 1)
        lane = lax.broadcasted_iota(jnp.int32, (CH, DN_H), 1)
        last = lax.broadcasted_iota(jnp.int32, (CH, 1), 0) == CH - 1
        rsum = lambda x: jnp.sum(x, axis=2, keepdims=True)

        def group(grp, carry):
            dbeta_acc, dgc_acc = carry
            base = grp * HB
            sls = [pl.ds(pl.multiple_of((base + i) * DK, DK), DK) for i in range(HB)]
            ksls = [pl.ds(pl.multiple_of((grp * (HB // 2) + j) * DK, DK), DK) for j in range(HB // 2)]
            heads = lambda ref: jnp.stack([ref[:, sl] for sl in sls])
            kheads = lambda ref: jnp.stack([ref[:, ksls[i // 2]] for i in range(HB)])
            cols = [_head_cols(base + i, beta, gc, gct, lane) for i in range(HB)]
            beta_c, gc_c, gc_r = (jnp.stack([c_[j] for c_ in cols]) for j in range(3))
            k, q, v = kheads(k_ref), kheads(q_ref), heads(v_ref)
            dec = jnp.exp(jnp.where(ii >= jj, gc_c - gc_r, NEG))
            eg = jnp.exp(gc_c)
            gl = gc_c[:, CH - 1:CH, :]
            e2 = jnp.exp(gl - gc_c)
            egl = jnp.exp(gl)
            k16, q16 = k.astype(BF16), q.astype(BF16)
            do16 = heads(do_ref).astype(BF16)
            s = st_ref[0, pl.ds(base, HB)]
            s16 = s.astype(BF16)
            dso = ds_scr[pl.ds(base, HB)]
            dso16 = dso.astype(BF16)
            wf, uf, vn16 = heads(w_ref), heads(u_ref), heads(vn_ref)
            kb = k * beta_c
            kb16 = kb.astype(BF16)
            pm = _bdot(q16, k16, "nt") * dec
            m = jnp.where(ii > jj, _bdot(kb16, k16, "nt") * dec, 0.0)
            dvn = _bdot(pm.astype(BF16), do16, "tn") + _bdot(heads(ks_ref), dso16, "nn")
            dvn16 = dvn.astype(BF16)
            ds_scr[pl.ds(base, HB)] = (egl * dso + _bdot(heads(qe_ref), do16, "tn")
                                       - _bdot(wf.astype(BF16), dvn16, "tn"))
            dpm = jnp.where(ii >= jj, _bdot(do16, vn16, "nt"), 0.0)
            dqk16 = (dpm * dec).astype(BF16)
            dqe = _bdot(do16, s16, "nt")
            dq = eg * dqe + _bdot(dqk16, k16, "nn")
            dks = _bdot(vn16, dso16, "nt")
            t = rsum(dks * k) * e2
            dgl = jnp.sum(t, axis=1, keepdims=True) + egl * jnp.sum(rsum(dso * s), axis=1, keepdims=True)
            dw = -_bdot(dvn16, s16, "nt")
            am = heads(a_ref)[:, :, 0:CH]
            dbvk = _bdot(am, jnp.concatenate([dvn, dw], axis=2), "tn", True)
            dbv, dbk = dbvk[:, :, :DK], dbvk[:, :, DK:]
            dm = jnp.where(ii > jj, -_bdot(dbvk, jnp.concatenate([uf, wf], axis=2), "nt", True), 0.0)
            g16 = (dm * dec).astype(BF16)
            dkb = _bdot(g16, k16, "nn")
            dk = (_bdot(dqk16, q16, "tn") + e2 * dks + _bdot(g16, kb16, "tn") + beta_c * (eg * dbk + dkb))
            e = dpm * pm + dm * m
            deg = rsum(q * dqe) + rsum(dbk * kb)
            dgc = rsum(e) - t + deg * eg + jnp.where(last, dgl, 0.0)
            dgrow = -jnp.sum(e, axis=1, keepdims=True)
            dv = beta_c * dbv
            dbeta = rsum(dbv * v) + rsum(dbk * k) * eg + rsum(dkb * k)
            for i in range(HB):
                dv_ref[:, sls[i]] = dv[i]
                sel = lane == base + i
                dbeta_acc = jnp.where(sel, dbeta[i], dbeta_acc)
                dgc_acc = jnp.where(sel, dgc[i], dgc_acc)
                gct[pl.ds(base + i, 1), :] = dgrow[i]
            for j in range(HB // 2):
                dq_ref[:, ksls[j]] = dq[2 * j] + dq[2 * j + 1]
                dk_ref[:, ksls[j]] = dk[2 * j] + dk[2 * j + 1]
            return dbeta_acc, dgc_acc

        zero = jnp.zeros((CH, DN_H), F32)
        dbeta_acc, dgc_acc = lax.fori_loop(0, DN_H // HB, group, (zero, zero))
        dbeta_ref[...] = dbeta_acc
        dg_ref[...] = _dot((ii <= jj).astype(F32), dgc_acc + gct[...].T, precision=HI)

    rev = lambda wd: pl.BlockSpec((CH, wd), lambda i: (nch - 1 - i, 0))
    return pl.pallas_call(
        body, grid=(nch,),
        in_specs=[rev(2048), rev(1024), rev(1024), rev(2048), rev(DN_H), rev(DN_H), rev(2048), rev(2048), rev(2048),
                  rev(2048), rev(2048), rev(2048), pl.BlockSpec((1, DN_H, DK, DK), lambda i: (nch - 1 - i, 0, 0, 0))],
        out_specs=[rev(1024), rev(1024), rev(2048), rev(DN_H), rev(DN_H)],
        out_shape=[SDS((rows, 1024), F32), SDS((rows, 1024), F32), SDS((rows, 2048), F32), SDS((rows, DN_H), F32),
                   SDS((rows, DN_H), F32)],
        scratch_shapes=[pltpu.VMEM((DN_H, DK, DK), F32), pltpu.VMEM((DN_H, CH), F32)],
        compiler_params=_cp("arbitrary"), name="dn_scan_bwd")(do, qn, kn, sv, gc, beta, a, u, w, vn, qe, ks, st)


def dn_prep_bwd(udn, conv_w, a_log, dt_bias, dqn, dkn, dv, dbeta, dg):
    rows = udn.shape[0]
    nch = rows // CH
    ext = CH + 8

    def body(xc_ref, xp_ref, ba_ref, xn_ref, dqn_n, dkn_n, dv_n, cw_ref, al_ref, dtb_ref, dqn_ref, dkn_ref, dv_ref,
             dbeta_ref, dg_ref, dx_ref, dba_ref, dcw_ref, dal_ref, ddtb_ref):
        c = pl.program_id(0)
        first = c == 0
        own = (lax.broadcasted_iota(jnp.int32, (ext, 1), 0) < CH) | (c < nch - 1)

        @pl.when(first)
        def _():
            dcw_ref[...] = jnp.zeros_like(dcw_ref)
            dal_ref[...] = jnp.zeros_like(dal_ref)
            ddtb_ref[...] = jnp.zeros_like(ddtb_ref)

        real, xa, beta, g = _gates(ba_ref[...], al_ref[...], dtb_ref[...], c)
        dgm = jnp.where(real, dg_ref[...], 0.0)
        da = dgm * (-jnp.exp(al_ref[...])) * jax.nn.sigmoid(xa)
        dal_ref[...] += jnp.sum(dgm * g, axis=0, keepdims=True)
        ddtb_ref[...] += jnp.sum(da, axis=0, keepdims=True)
        dba_ref[...] = jnp.zeros_like(dba_ref)
        dba_ref[:, 0:DN_H] = jnp.where(real, dbeta_ref[...] * beta * (1.0 - beta), 0.0)
        dba_ref[:, DN_H:2 * DN_H] = da

        def through_conv(off, g_cur, g_next, grad_fn):
            sl = pl.ds(off, DK)
            xx = jnp.concatenate([jnp.where(first, 0.0, xp_ref[:, sl]), xc_ref[:, sl], xn_ref[:, sl]], axis=0)
            y = cw_ref[0:1, sl] * xx[5:5 + ext]
            for j in range(1, 4):
                y += cw_ref[j:j + 1, sl] * xx[5 + j:5 + j + ext]
            dy = jnp.where(own, grad_fn(_silu(y), jnp.concatenate([g_cur, g_next], axis=0)) * _dsilu(y), 0.0)
            dx = cw_ref[0:1, sl] * dy[3:3 + CH]
            for j in range(1, 4):
                dx += cw_ref[j:j + 1, sl] * dy[3 - j:3 - j + CH]
            dx_ref[:, sl] = dx
            for j in range(4):
                dcw_ref[j:j + 1, sl] += jnp.sum(dy[:CH] * xx[5 + j:5 + j + CH], axis=0, keepdims=True)

        def l2_bwd(scale):
            def f(s, gin):
                r = lax.rsqrt(jnp.sum(s * s, axis=-1, keepdims=True) + EPS)
                nrm = s * r
                return (r * scale) * (gin - nrm * jnp.sum(nrm * gin, axis=-1, keepdims=True))
            return f

        def qk_body(kh, carry):
            sl = pl.ds(pl.multiple_of(kh * DK, DK), DK)
            through_conv(pl.multiple_of(kh * DK, DK), dqn_ref[:, sl], dqn_n[:, sl], l2_bwd(DK ** -0.5))
            through_conv(pl.multiple_of(1024 + kh * DK, DK), dkn_ref[:, sl], dkn_n[:, sl], l2_bwd(1.0))
            return carry

        lax.fori_loop(0, DN_KH, qk_body, 0)

        def v_body(hv, carry):
            sl = pl.ds(pl.multiple_of(hv * DK, DK), DK)
            through_conv(pl.multiple_of(2048 + hv * DK, DK), dv_ref[:, sl], dv_n[:, sl], lambda s, gin: gin)
            return carry

        lax.fori_loop(0, DN_H, v_body, 0)

    full = lambda shape: pl.BlockSpec(shape, lambda c: (0, 0))
    blk = lambda w: pl.BlockSpec((CH, w), lambda c: (c, 0))
    nxt = lambda w: pl.BlockSpec((8, w), lambda c: (jnp.minimum(8 * c + 8, rows // 8 - 1), 0))
    return pl.pallas_call(
        body, grid=(nch,),
        in_specs=_chunk_specs(0) + [nxt(4096), nxt(1024), nxt(1024), nxt(2048), full((4, 4096)), full((1, DN_H)),
                                    full((1, DN_H)), blk(1024), blk(1024), blk(2048), blk(DN_H), blk(DN_H)],
        out_specs=[blk(4096), blk(DK), full((8, 4096)), full((1, DN_H)), full((1, DN_H))],
        out_shape=[SDS((rows, 4096), F32), SDS((rows, DK), F32), SDS((8, 4096), F32), SDS((1, DN_H), F32),
                   SDS((1, DN_H), F32)],
        compiler_params=_cp("arbitrary"), name="dn_prep_bwd")(
            udn, udn, udn, udn, dqn, dkn, dv, conv_w, a_log, dt_bias, dqn, dkn, dv, dbeta, dg)


def local_step(x, target, w):
    seq = x.shape[0]
    bf = lambda a: a.astype(BF16)
    h0 = jnp.concatenate([jnp.zeros((PAD, D_MODEL), F32), w["meta_tokens"], x], axis=0)
    tgt = jnp.concatenate([jnp.zeros((BLK, D_MODEL), F32), target], axis=0)
    win = w["attn_w_in"]
    wq, wkv, wg = win[:, :1024], win[:, 1024:1280], win[:, 1280:]
    wa_in = bf(jnp.concatenate([wq, wg, wkv], axis=1))
    wa_out = bf(w["attn_w_out"])
    wd_in = jnp.concatenate([bf(w["dn_w_in"]), jnp.zeros((D_MODEL, 96), BF16)], axis=1)
    wd_out = bf(w["dn_w_out"])
    qw, kw, sinks = w["attn_q_norm_w"], w["attn_k_norm_w"], w["attn_sinks"]
    cw, al, dtb, ow = w["dn_conv_w"], w["dn_a_log"], w["dn_dt_bias"], w["dn_o_norm_w"]

    ua, xn0 = norm_matmul(h0, w["attn_norm_w"], wa_in, 2304, "attn_in")
    og = attn_fwd(ua, qw, kw, sinks)
    h1 = matmul_residual(og, wa_out, h0, "attn_out")
    ud, xn1 = norm_matmul(h1, w["dn_norm_w"], wd_in, 896, "dn_in")
    qn, kn, sv, gc, beta, u, wy, qe, ks, p, a = dn_prep(ud, cw, al, dtb)
    o, vn, st = dn_scan(u, wy, qe, ks, p, gc)
    dh2, on, ls = dn_out_fwd(o, ud, ow, wd_out, h1, tgt)
    loss = (0.5 / D_MODEL) * jnp.sum(ls)

    do, dz, dow = dn_out_bwd(dh2, wd_out, o, ud, ow)
    g_dn_out = wgrad(on, dh2, "dn_out_wgrad")
    dqn, dkn, dv, dbeta, dg = dn_scan_bwd(do, qn, kn, sv, gc, beta, a, u, wy, vn, qe, ks, st)
    dxc, dba, dcw, dal, ddtb = dn_prep_bwd(ud, cw, al, dtb, dqn, dkn, dv, dbeta, dg)
    dh1, dnw1 = in_proj_bwd([dxc, dz, dba], [wd_in[:, :4096], wd_in[:, 4096:6144], wd_in[:, 6144:]],
                            h1, w["dn_norm_w"], dh2, "dn_in_bwd")
    g_dn_in = jnp.concatenate([wgrad(xn1, dxc, "dn_in_wgrad_qkv"), wgrad(xn1, dz, "dn_in_wgrad_z"),
                               wgrad(xn1, dba, "dn_in_wgrad_ba")[:, :2 * DN_H]], axis=1)

    dog = matmul_nt(dh1, wa_out, "attn_out_bwd")
    g_attn_out = wgrad(og, dh1, "attn_out_wgrad")
    dq, dgate, dkv, dkvm, dqw, dkw, dsk = attn_bwd(ua, qw, kw, sinks, dog)
    dkv = dkv.at[PAD:BLK].add(dkvm)
    dh0, dnw0 = in_proj_bwd([dq, dgate, dkv], [wa_in[:, :1024], wa_in[:, 1024:2048], wa_in[:, 2048:]],
                            h0, w["attn_norm_w"], dh1, "attn_in_bwd")
    g_attn_in = jnp.concatenate([wgrad(xn0, dq, "attn_in_wgrad_q"), wgrad(xn0, dkv, "attn_in_wgrad_kv"),
                                 wgrad(xn0, dgate, "attn_in_wgrad_g")], axis=1)
    grads = {
        "meta_tokens": dh0[PAD:BLK], "attn_norm_w": jnp.sum(dnw0, axis=0), "attn_w_in": g_attn_in,
        "attn_q_norm_w": dqw, "attn_k_norm_w": dkw, "attn_sinks": dsk, "attn_w_out": g_attn_out,
        "dn_norm_w": jnp.sum(dnw1, axis=0), "dn_w_in": g_dn_in, "dn_conv_w": dcw[:4], "dn_a_log": dal,
        "dn_dt_bias": ddtb, "dn_o_norm_w": jnp.sum(dow, axis=0), "dn_w_out": g_dn_out,
    }
    return loss, dh0[BLK:BLK + seq], grads


WEIGHTS = ["meta_tokens", "attn_norm_w", "attn_w_in", "attn_q_norm_w", "attn_k_norm_w", "attn_sinks", "attn_w_out",
           "dn_norm_w", "dn_w_in", "dn_conv_w", "dn_a_log", "dn_dt_bias", "dn_o_norm_w", "dn_w_out"]
SHARDED = {"attn_w_in": ((1024, 2304), 1), "attn_w_out": ((1024, 1024), 0), "dn_w_in": ((1024, 6176), 1),
           "dn_w_out": ((2048, 1024), 0), "dn_conv_w": ((4, 4096), 1), "meta_tokens": ((16, 1024), 1),
           "dn_norm_w": ((1, 1024), 1)}
REPLICATED = {"attn_norm_w": 1024, "attn_q_norm_w": 64, "attn_k_norm_w": 64, "attn_sinks": 16, "dn_a_log": 16,
              "dn_dt_bias": 16, "dn_o_norm_w": 128}
N_CHIPS = 4
PACK_ROWS = 2912
HALF_ROWS = PACK_ROWS // 2
SMALL_ROWS = 8


def _shard_shape(name):
    (r, c), axis = SHARDED[name]
    return (r // N_CHIPS, c) if axis == 0 else (r, c // N_CHIPS)


def _pack(parts, rows):
    flat = jnp.concatenate([p.reshape(-1) for p in parts])
    return jnp.pad(flat, (0, rows * 1024 - flat.shape[0])).reshape(rows, 1024)


def pack_shard(shards):
    return _pack([shards[n] for n in SHARDED], PACK_ROWS)


def unpack_shard(buf):
    flat, out, pos = buf.reshape(-1), {}, 0
    for n in SHARDED:
        shp = _shard_shape(n)
        size = shp[0] * shp[1]
        out[n] = flat[pos:pos + size].reshape(shp)
        pos += size
    return out


MATRICES = ("attn_w_in", "attn_w_out", "dn_w_in", "dn_w_out")


def pack_gather(shards):
    big = [shards[n].astype(BF16).reshape(-1) for n in MATRICES]
    small = jnp.concatenate([shards[n].reshape(-1) for n in SHARDED if n not in MATRICES])
    flat = jnp.concatenate(big + [lax.bitcast_convert_type(small, BF16).reshape(-1)])
    return jnp.pad(flat, (0, PACK_ROWS * 1024 - flat.shape[0])).reshape(PACK_ROWS, 1024)


def unpack_gather(buf):
    PER_F32 = 4 // jnp.dtype(buf.dtype).itemsize
    flat, out, pos = buf.reshape(-1), {}, 0
    for n in MATRICES:
        shp = _shard_shape(n)
        out[n] = flat[pos:pos + shp[0] * shp[1]].reshape(shp)
        pos += shp[0] * shp[1]
    for n in SHARDED:
        if n not in MATRICES:
            shp = _shard_shape(n)
            raw = flat[pos:pos + shp[0] * shp[1] * PER_F32]
            out[n] = lax.bitcast_convert_type(raw.reshape(-1, PER_F32) if PER_F32 > 1 else raw, F32).reshape(shp)
            pos += shp[0] * shp[1] * PER_F32
    return out


def pack_small(vals):
    return _pack([vals[n] for n in REPLICATED], SMALL_ROWS)


def unpack_small(buf):
    flat, out, pos = buf.reshape(-1), {}, 0
    for n, size in REPLICATED.items():
        out[n] = flat[pos:pos + size].reshape(1, size)
        pos += size
    return out


ANY = pl.BlockSpec(memory_space=pl.ANY)


def _place():
    return lax.axis_index("x"), lax.axis_index("y"), lax.axis_index("c")


def chips_exchange(src, gather):
    r = src.shape[-2]

    def body(s_ref, o_ref, send_sems, recv_sems, local_sem):
        x, y, c = _place()
        me = 2 * x + y
        peers = [(1 - x, y), (x, 1 - y), (1 - x, 1 - y)]
        mine = pltpu.make_async_copy(s_ref if gather else s_ref.at[me], o_ref.at[me], local_sem)
        mine.start()

        def copy(k, to_block, from_block):
            px, py = peers[k]
            return pltpu.make_async_remote_copy(
                src_ref=s_ref if gather else s_ref.at[to_block], dst_ref=o_ref.at[from_block],
                send_sem=send_sems.at[k], recv_sem=recv_sems.at[k], device_id=(px, py, c), device_id_type=MESH)

        sends = [copy(k, 2 * px + py, me) for k, (px, py) in enumerate(peers)]
        for cp in sends:
            cp.start()
        for k, (px, py) in enumerate(peers):
            copy(k, me, 2 * px + py).wait_recv()
        for cp in sends:
            cp.wait_send()
        mine.wait()

    return pl.pallas_call(
        body, in_specs=[ANY], out_specs=ANY, out_shape=SDS((N_CHIPS, r, 1024), src.dtype),
        scratch_shapes=[pltpu.SemaphoreType.DMA((3,)), pltpu.SemaphoreType.DMA((3,)), pltpu.SemaphoreType.DMA],
        name="chips_gather" if gather else "chips_exchange")(src)


def sibling_exchange(src, name):
    def body(s_ref, o_ref, send_sem, recv_sem):
        x, y, c = _place()
        cp = pltpu.make_async_remote_copy(src_ref=s_ref, dst_ref=o_ref, send_sem=send_sem, recv_sem=recv_sem,
                                          device_id=(x, y, 1 - c), device_id_type=MESH)
        cp.start()
        cp.wait()

    return pl.pallas_call(
        body, in_specs=[ANY], out_specs=ANY, out_shape=SDS(src.shape, src.dtype),
        scratch_shapes=[pltpu.SemaphoreType.DMA, pltpu.SemaphoreType.DMA], name=name)(src)


def all_gather_small(src):
    def body(s_ref, o_ref, send_sems, recv_sems, local_sem):
        x, y, c = _place()
        flips = [(fx, fy, fc) for fx in (0, 1) for fy in (0, 1) for fc in (0, 1)][1:]
        idx = lambda px, py, pc: 4 * px + 2 * py + pc
        mine = pltpu.make_async_copy(s_ref, o_ref.at[idx(x, y, c)], local_sem)
        mine.start()

        def peer(k):
            fx, fy, fc = flips[k]
            return (1 - x if fx else x, 1 - y if fy else y, 1 - c if fc else c)

        def copy(k, block):
            return pltpu.make_async_remote_copy(
                src_ref=s_ref, dst_ref=o_ref.at[block], send_sem=send_sems.at[k], recv_sem=recv_sems.at[k],
                device_id=peer(k), device_id_type=MESH)

        sends = [copy(k, idx(x, y, c)) for k in range(7)]
        for cp in sends:
            cp.start()
        for k in range(7):
            copy(k, idx(*peer(k))).wait_recv()
        for cp in sends:
            cp.wait_send()
        mine.wait()

    return pl.pallas_call(
        body, in_specs=[ANY], out_specs=ANY, out_shape=SDS((8,) + src.shape, F32),
        scratch_shapes=[pltpu.SemaphoreType.DMA((7,)), pltpu.SemaphoreType.DMA((7,)), pltpu.SemaphoreType.DMA],
        name="all_gather_small")(src)


def sum_blocks(t, name):
    n, r, _ = t.shape
    tm = 208 if r % 208 == 0 else r

    def body(t_ref, o_ref):
        acc = t_ref[0]
        for i in range(1, n):
            acc = acc + t_ref[i]
        o_ref[...] = acc

    return pl.pallas_call(
        body, grid=(r // tm,), in_specs=[pl.BlockSpec((n, tm, 1024), lambda i: (0, i, 0))],
        out_specs=pl.BlockSpec((tm, 1024), lambda i: (i, 0)), out_shape=SDS((r, 1024), F32),
        compiler_params=_cp("parallel"), name=name)(t)


def adamw(w, g, m, v):
    rows = w.shape[0]
    tm = rows // 5 if rows % 40 == 0 else rows

    def body(w_ref, g_ref, m_ref, v_ref, d_ref, nm_ref, nv_ref):
        g_ = g_ref[...]
        m_ = ADAM_B1 * m_ref[...] + (1.0 - ADAM_B1) * g_
        v_ = ADAM_B2 * v_ref[...] + (1.0 - ADAM_B2) * (g_ * g_)
        m_hat = m_ / (1.0 - ADAM_B1 ** ADAM_STEP)
        v_hat = v_ / (1.0 - ADAM_B2 ** ADAM_STEP)
        d_ref[...] = -ADAM_LR * (m_hat / (jnp.sqrt(v_hat) + ADAM_EPS) + ADAM_WD * w_ref[...])
        nm_ref[...] = m_
        nv_ref[...] = v_

    spec = pl.BlockSpec((tm, 1024), lambda i: (i, 0))
    return pl.pallas_call(
        body, grid=(rows // tm,), in_specs=[spec] * 4, out_specs=[spec] * 3,
        out_shape=[SDS((rows, 1024), F32)] * 3, compiler_params=_cp("parallel"), name="adamw")(w, g, m, v)


LAYERED = ("attn_w_in", "attn_w_out", "dn_w_in", "dn_conv_w", "dn_w_out")


def _two_d(name, a):
    return a[0] if name in LAYERED else a


def kernel(x, meta_tokens, attn_norm_w, attn_w_in, attn_q_norm_w, attn_k_norm_w, attn_sinks, attn_w_out, dn_norm_w, dn_w_in, dn_conv_w, dn_a_log, dn_dt_bias, dn_o_norm_w, dn_w_out, loss_target, m_meta_tokens, m_attn_norm_w, m_attn_w_in, m_attn_q_norm_w, m_attn_k_norm_w, m_attn_sinks, m_attn_w_out, m_dn_norm_w, m_dn_w_in, m_dn_conv_w, m_dn_a_log, m_dn_dt_bias, m_dn_o_norm_w, m_dn_w_out, v_meta_tokens, v_attn_norm_w, v_attn_w_in, v_attn_q_norm_w, v_attn_k_norm_w, v_attn_sinks, v_attn_w_out, v_dn_norm_w, v_dn_w_in, v_dn_conv_w, v_dn_a_log, v_dn_dt_bias, v_dn_o_norm_w, v_dn_w_out):
    given = dict(zip(WEIGHTS, (meta_tokens, attn_norm_w, attn_w_in, attn_q_norm_w, attn_k_norm_w, attn_sinks,
                               attn_w_out, dn_norm_w, dn_w_in, dn_conv_w, dn_a_log, dn_dt_bias, dn_o_norm_w, dn_w_out)))
    mom1 = dict(zip(WEIGHTS, (m_meta_tokens, m_attn_norm_w, m_attn_w_in, m_attn_q_norm_w, m_attn_k_norm_w,
                              m_attn_sinks, m_attn_w_out, m_dn_norm_w, m_dn_w_in, m_dn_conv_w, m_dn_a_log,
                              m_dn_dt_bias, m_dn_o_norm_w, m_dn_w_out)))
    mom2 = dict(zip(WEIGHTS, (v_meta_tokens, v_attn_norm_w, v_attn_w_in, v_attn_q_norm_w, v_attn_k_norm_w,
                              v_attn_sinks, v_attn_w_out, v_dn_norm_w, v_dn_w_in, v_dn_conv_w, v_dn_a_log,
                              v_dn_dt_bias, v_dn_o_norm_w, v_dn_w_out)))
    two_d = lambda d: {n: _two_d(n, a) for n, a in d.items()}
    given, mom1, mom2 = two_d(given), two_d(mom1), two_d(mom2)
    c = lax.axis_index("c")

    w_shard = pack_shard(given)
    mine = chips_exchange(lax.dynamic_slice_in_dim(pack_gather(given), c * HALF_ROWS, HALF_ROWS, axis=0), True)
    theirs = sibling_exchange(mine, "gather_swap")
    gathered = jnp.where(c == 0, jnp.concatenate([mine, theirs], axis=1), jnp.concatenate([theirs, mine], axis=1))
    per_chip = [unpack_gather(gathered[j]) for j in range(N_CHIPS)]
    full = {n: jnp.concatenate([pc[n] for pc in per_chip], axis=SHARDED[n][1]) for n in SHARDED}
    full.update({n: given[n] for n in REPLICATED})

    loss, dx, grads = local_step(x[0], loss_target[0], full)

    split = lambda n: jnp.split(grads[n], N_CHIPS, axis=SHARDED[n][1])
    g_all = jnp.stack([pack_shard({n: split(n)[j] for n in SHARDED}) for j in range(N_CHIPS)])
    keep = lax.dynamic_slice_in_dim(g_all, c * HALF_ROWS, HALF_ROWS, axis=1)
    give = lax.dynamic_slice_in_dim(g_all, (1 - c) * HALF_ROWS, HALF_ROWS, axis=1)
    got = sibling_exchange(give, "pair_exchange")
    pair = sum_blocks(jnp.stack([keep, got]).reshape(2, N_CHIPS * HALF_ROWS, 1024), "pair_sum")
    from_chips = chips_exchange(pair.reshape(N_CHIPS, HALF_ROWS, 1024), False)
    half = sum_blocks(from_chips, "chip_sum")
    other = sibling_exchange(half, "half_exchange")
    g_shard = jnp.where(c == 0, jnp.concatenate([half, other]), jnp.concatenate([other, half]))

    g_small = sum_blocks(all_gather_small(pack_small(grads)), "small_sum")

    stack = lambda big, small: jnp.concatenate([big, small])
    delta, new_m, new_v = adamw(stack(w_shard, pack_small(given)), stack(g_shard, g_small),
                                stack(pack_shard(mom1), pack_small(mom1)), stack(pack_shard(mom2), pack_small(mom2)))

    def unpack(buf):
        out = unpack_shard(buf[:PACK_ROWS])
        out.update(unpack_small(buf[PACK_ROWS:]))
        return [out[n][None] if n in LAYERED else out[n] for n in WEIGHTS]

    loss = lax.psum(loss, ("x", "y", "c"))
    return (loss, dx[None], *unpack(stack(g_shard, g_small)), *unpack(delta), *unpack(new_m), *unpack(new_v))
```

```python
import functools

import jax
import jax.numpy as jnp
from jax import lax
from jax.experimental import pallas as pl
from jax.experimental.pallas import tpu as pltpu

F32 = jnp.float32
BF16 = jnp.bfloat16
SDS = jax.ShapeDtypeStruct
MESH = pl.DeviceIdType.MESH

D_MODEL = 1024
N_META = 16
EPS = 1e-6
BLK = 128
CH = 64
PAD = BLK - N_META
HEADS = 16
HD = 64
KVW = 256
DN_H = 16
DN_KH = 8
DK = 128
SLOPES = [2.0 ** (-8.0 * (h + 1) / HEADS) for h in range(HEADS)]
NEG = -1e30
NT = (((1,), (1,)), ((), ()))
TN = (((0,), (0,)), ((), ()))
HI = lax.Precision.HIGHEST

ADAM_LR, ADAM_B1, ADAM_B2, ADAM_EPS, ADAM_WD, ADAM_STEP = 0.001, 0.9, 0.999, 1e-08, 0.01, 10

VMEM_LIMIT = 56 * 1024 * 1024


def _cp(*sem):
    return pltpu.CompilerParams(dimension_semantics=sem, vmem_limit_bytes=VMEM_LIMIT)


def _row_tile(rows):
    for t in (384, 256, 128):
        if rows % t == 0:
            return t
    raise ValueError(rows)


def _dot(a, b, dims=None, precision=None):
    if dims is None:
        return jnp.dot(a, b, preferred_element_type=F32, precision=precision)
    return lax.dot_general(a, b, dims, preferred_element_type=F32, precision=precision)


def _silu(x):
    return x * jax.nn.sigmoid(x)


def _dsilu(x):
    s = jax.nn.sigmoid(x)
    return s * (1.0 + x * (1.0 - s))


def _rms(x):
    return lax.rsqrt(jnp.mean(x * x, axis=-1, keepdims=True) + EPS)


def norm_matmul(h, nw, w, tn, name):
    rows, k = h.shape
    n = w.shape[1]
    tm = _row_tile(rows)

    def norm_body(h_ref, nw_ref, xn_ref):
        x = h_ref[...]
        xn_ref[...] = (x * _rms(x) * nw_ref[...]).astype(BF16)

    xn = pl.pallas_call(
        norm_body, grid=(rows // tm,),
        in_specs=[pl.BlockSpec((tm, k), lambda i: (i, 0)), pl.BlockSpec((1, k), lambda i: (0, 0))],
        out_specs=pl.BlockSpec((tm, k), lambda i: (i, 0)), out_shape=SDS((rows, k), BF16),
        compiler_params=_cp("parallel"), name=name + "_norm")(h, nw)

    def body(a_ref, w_ref, o_ref):
        o_ref[...] = _dot(a_ref[...], w_ref[...])

    out = pl.pallas_call(
        body, grid=(n // tn, rows // tm),
        in_specs=[pl.BlockSpec((tm, k), lambda j, i: (i, 0)), pl.BlockSpec((k, tn), lambda j, i: (0, j))],
        out_specs=pl.BlockSpec((tm, tn), lambda j, i: (i, j)), out_shape=SDS((rows, n), F32),
        compiler_params=_cp("parallel", "parallel"), name=name)(xn, w)
    return out, xn


def matmul_residual(a, w, res, name):
    rows, k = a.shape
    n = w.shape[1]
    tm = _row_tile(rows)

    def body(a_ref, w_ref, r_ref, o_ref):
        o_ref[...] = r_ref[...] + _dot(a_ref[...], w_ref[...])

    return pl.pallas_call(
        body, grid=(rows // tm,),
        in_specs=[pl.BlockSpec((tm, k), lambda i: (i, 0)), pl.BlockSpec((k, n), lambda i: (0, 0)),
                  pl.BlockSpec((tm, n), lambda i: (i, 0))],
        out_specs=pl.BlockSpec((tm, n), lambda i: (i, 0)),
        out_shape=SDS((rows, n), F32), compiler_params=_cp("parallel"), name=name)(a, w, res)


def wgrad(a, b, name):
    rows, k = a.shape
    n = b.shape[1]
    tm = _row_tile(rows)
    tn = min(n, 1024)

    def body(a_ref, b_ref, o_ref):
        @pl.when(pl.program_id(1) == 0)
        def _():
            o_ref[...] = jnp.zeros_like(o_ref)

        o_ref[...] += _dot(a_ref[...], b_ref[...].astype(BF16), TN)

    return pl.pallas_call(
        body, grid=(n // tn, rows // tm),
        in_specs=[pl.BlockSpec((tm, k), lambda j, i: (i, 0)), pl.BlockSpec((tm, tn), lambda j, i: (i, j))],
        out_specs=pl.BlockSpec((k, tn), lambda j, i: (0, j)),
        out_shape=SDS((k, n), F32), compiler_params=_cp("parallel", "arbitrary"), name=name)(a, b)


def in_proj_bwd(dus, ws, h, nw, dh_next, name):
    rows, k = h.shape
    tm = _row_tile(rows)
    nd = len(dus)
    nt = rows // tm

    def body(*refs):
        du_refs, w_refs = refs[:nd], refs[nd:2 * nd]
        h_ref, nw_ref, dhn_ref, dh_ref, dnw_ref = refs[2 * nd:]
        dxn = _dot(du_refs[0][...].astype(BF16), w_refs[0][...], NT)
        for du_ref, w_ref in zip(du_refs[1:], w_refs[1:]):
            dxn += _dot(du_ref[...].astype(BF16), w_ref[...], NT)
        x = h_ref[...]
        r = _rms(x)
        y = x * r
        gy = dxn * nw_ref[...]
        dh_ref[...] = dhn_ref[...] + r * (gy - y * jnp.mean(y * gy, axis=-1, keepdims=True))
        dnw_ref[0] = jnp.sum(dxn * y, axis=0, keepdims=True)

    in_specs = [pl.BlockSpec((tm, du.shape[1]), lambda i: (i, 0)) for du in dus]
    in_specs += [pl.BlockSpec(w.shape, lambda i: (0, 0)) for w in ws]
    in_specs += [pl.BlockSpec((tm, k), lambda i: (i, 0)), pl.BlockSpec((1, k), lambda i: (0, 0)),
                 pl.BlockSpec((tm, k), lambda i: (i, 0))]
    return pl.pallas_call(
        body, grid=(nt,), in_specs=in_specs,
        out_specs=[pl.BlockSpec((tm, k), lambda i: (i, 0)), pl.BlockSpec((1, 1, k), lambda i: (i, 0, 0))],
        out_shape=[SDS((rows, k), F32), SDS((nt, 1, k), F32)],
        compiler_params=_cp("parallel"), name=name)(*dus, *ws, h, nw, dh_next)


def matmul_nt(a, w, name):
    rows, k = a.shape
    n = w.shape[0]
    tm = _row_tile(rows)

    def body(a_ref, w_ref, o_ref):
        o_ref[...] = _dot(a_ref[...].astype(BF16), w_ref[...], NT)

    return pl.pallas_call(
        body, grid=(rows // tm,),
        in_specs=[pl.BlockSpec((tm, k), lambda i: (i, 0)), pl.BlockSpec((n, k), lambda i: (0, 0))],
        out_specs=pl.BlockSpec((tm, n), lambda i: (i, 0)),
        out_shape=SDS((rows, n), F32), compiler_params=_cp("parallel"), name=name)(a, w)


SUB = 64
GRP = 8
TR = GRP * SUB
NBAND = 192
TK = 256


def _tile_bias(n, sb):
    r = lax.broadcasted_iota(jnp.int32, (TR, TK), 0)
    c = lax.broadcasted_iota(jnp.int32, (TR, TK), 1)
    qi = r & (SUB - 1)
    d = BLK + qi - c
    dm = n * BLK + SUB * sb - PAD + NBAND + qi - c
    band = c < NBAND
    valid = (band & (d >= 0) & (d < BLK) & (c >= 2 * BLK - BLK * n - SUB * sb)) | (
        (c >= NBAND) & (c < NBAND + N_META) & (dm >= 0))
    return valid, jnp.where(band, d, jnp.minimum(dm, BLK)).astype(F32)


def _group_col(vals):
    g = lax.broadcasted_iota(jnp.int32, (TR, 1), 0) >> 6
    col = jnp.zeros((TR, 1), F32)
    for gi, v in enumerate(vals):
        col = jnp.where(g == gi, v, col)
    return col


def _stack_heads(ref, sb, kvh):
    return jnp.concatenate(
        [ref[SUB * sb:SUB * sb + SUB, HD * (GRP * kvh + g):HD * (GRP * kvh + g) + HD] for g in range(GRP)], axis=0)


def _unstack_heads(parts):
    return jnp.concatenate([parts[kvh][SUB * g:SUB * g + SUB] for kvh in range(2) for g in range(GRP)], axis=1)


def _tile_keys(band, meta, sb):
    return jnp.concatenate([band[SUB * sb:SUB * sb + NBAND], meta,
                            jnp.zeros((TK - NBAND - N_META, HD), band.dtype)], axis=0)


def _row_sums(x):
    ones = jnp.ones((x.shape[1], 128), BF16)
    hi = x.astype(BF16)
    lo = (x - hi.astype(F32)).astype(BF16)
    return _dot(hi, ones) + _dot(lo, ones)


def _rms_stack(q):
    return lax.rsqrt(_row_sums(q * q)[:, :HD] * (1.0 / HD) + EPS)


def _fill_bias(bias_scr, n):
    @pl.when(n <= 2)
    def _():
        for sb in range(2):
            valid, dist = _tile_bias(n, sb)
            for kvh in range(2):
                slope_col = _group_col([SLOPES[GRP * kvh + g] for g in range(GRP)])
                bias_scr[2 * sb + kvh] = jnp.where(valid, -slope_col * dist, NEG)


def _tile_probs(qn16, k16, bias, sink_col):
    s = _dot(qn16, k16, NT) * (HD ** -0.5) + bias
    mx = jnp.maximum(jnp.max(s.astype(BF16), axis=-1, keepdims=True).astype(F32), sink_col)
    e = jnp.exp(s - mx)
    es = jnp.exp(sink_col - mx)
    inv = 1.0 / (_row_sums(e) + es)
    return e * jnp.concatenate([inv, inv], axis=1), es * inv


def _kv_heads(kvb, kvm, kw_):
    out = []
    for kvh in range(2):
        kb, km = kvb[:, HD * kvh:HD * kvh + HD], kvm[:, HD * kvh:HD * kvh + HD]
        out.append(((kb * _rms(kb) * kw_).astype(BF16), (km * _rms(km) * kw_).astype(BF16),
                    kvb[:, BLK + HD * kvh:BLK + HD * kvh + HD].astype(BF16),
                    kvm[:, BLK + HD * kvh:BLK + HD * kvh + HD].astype(BF16)))
    return out


def _kv_specs(nblk, clamp):
    cur = (lambda n: (jnp.minimum(n, nblk - 1), 8)) if clamp else (lambda n: (n, 8))
    return [pl.BlockSpec((BLK, KVW), cur),
            pl.BlockSpec((BLK, KVW), lambda n: (jnp.maximum(n - 1, 0), 8)),
            pl.BlockSpec((N_META, KVW), lambda n: (PAD // N_META, 8))]


def attn_fwd(u, qw, kw, sinks):
    rows = u.shape[0]
    nblk = rows // BLK

    def body(q_ref, g_ref, kvc_ref, kvp_ref, kvm_ref, qw_ref, kw_ref, sk_ref, og_ref, bias_scr):
        _fill_bias(bias_scr, pl.program_id(0))
        qw_ = qw_ref[...]
        kv = _kv_heads(jnp.concatenate([kvp_ref[...], kvc_ref[...]], axis=0), kvm_ref[...], kw_ref[...])
        sink_cols = [_group_col([sk_ref[:, GRP * kvh + g:GRP * kvh + g + 1] for g in range(GRP)]) for kvh in range(2)]
        for sb in range(2):
            parts = []
            for kvh in range(2):
                knb, knm, vb, vm = kv[kvh]
                q = _stack_heads(q_ref, sb, kvh)
                qn16 = (q * _rms_stack(q) * qw_).astype(BF16)
                p, _ = _tile_probs(qn16, _tile_keys(knb, knm, sb), bias_scr[2 * sb + kvh], sink_cols[kvh])
                parts.append(_dot(p.astype(BF16), _tile_keys(vb, vm, sb)))
            rows = slice(SUB * sb, SUB * sb + SUB)
            og_ref[rows, :] = (_unstack_heads(parts) * _silu(g_ref[rows, :])).astype(BF16)

    small = lambda w: pl.BlockSpec((1, w), lambda n: (0, 0))
    return pl.pallas_call(
        body, grid=(nblk,),
        in_specs=[pl.BlockSpec((BLK, 1024), lambda n: (n, 0)), pl.BlockSpec((BLK, 1024), lambda n: (n, 1))]
        + _kv_specs(nblk, False) + [small(HD), small(HD), small(HEADS)],
        out_specs=pl.BlockSpec((BLK, 1024), lambda n: (n, 0)),
        out_shape=SDS((rows, 1024), BF16), scratch_shapes=[pltpu.VMEM((4, TR, TK), F32)],
        compiler_params=_cp("arbitrary"), name="attn_fwd")(u, u, u, u, u, qw, kw, sinks)


def attn_bwd(u, qw, kw, sinks, dog):
    rows = u.shape[0]
    nblk = rows // BLK

    def knorm_bwd(k, dkn, kw_):
        r = _rms(k)
        y = k * r
        gy = dkn * kw_
        return r * (gy - y * jnp.mean(y * gy, axis=-1, keepdims=True)), jnp.sum(dkn * y, axis=0, keepdims=True)

    def body(q_ref, g_ref, dog_ref, kvc_ref, kvp_ref, kvm_ref, qw_ref, kw_ref, sk_ref,
             dq_ref, dg_ref, dkv_ref, dkvm_ref, dqw_ref, dkw_ref, dsk_ref, carry, prevp, curp, metap, bias_scr):
        n = pl.program_id(0)
        qw_, kw_ = qw_ref[...], kw_ref[...]
        _fill_bias(bias_scr, n)

        @pl.when(n == 0)
        def _():
            carry[...] = jnp.zeros_like(carry)
            metap[...] = jnp.zeros_like(metap)
            dqw_ref[...] = jnp.zeros_like(dqw_ref)
            dkw_ref[...] = jnp.zeros_like(dkw_ref)
            dsk_ref[...] = jnp.zeros_like(dsk_ref)

        @pl.when(n == nblk)
        def _():
            prevp[...] = jnp.zeros_like(prevp)
            curp[...] = jnp.zeros_like(curp)

        @pl.when(n < nblk)
        def _():
            kv = _kv_heads(jnp.concatenate([kvp_ref[...], kvc_ref[...]], axis=0), kvm_ref[...], kw_)
            sink_cols = [_group_col([sk_ref[:, GRP * kvh + g:GRP * kvh + g + 1] for g in range(GRP)])
                         for kvh in range(2)]
            lane = lax.broadcasted_iota(jnp.int32, (1, HEADS), 1)
            dqw = jnp.zeros((1, HD), F32)
            dsk = jnp.zeros((1, HEADS), F32)
            band_parts = [jnp.zeros((2 * BLK, HD), F32) for _ in range(4)]
            meta_parts = [jnp.zeros((N_META, HD), F32) for _ in range(4)]

            def widen(x, sb):
                z = jnp.zeros((2 * BLK - NBAND, HD), F32)
                return jnp.concatenate([x, z] if sb == 0 else [z, x], axis=0)

            for sb in range(2):
                rows = slice(SUB * sb, SUB * sb + SUB)
                dq_parts, dg_parts = [], []
                for kvh in range(2):
                    knb, knm, vb, vm = kv[kvh]
                    k16, v16 = _tile_keys(knb, knm, sb), _tile_keys(vb, vm, sb)
                    q = _stack_heads(q_ref, sb, kvh)
                    r = _rms_stack(q)
                    y = q * r
                    qn16 = (y * qw_).astype(BF16)
                    p, ps = _tile_probs(qn16, k16, bias_scr[2 * sb + kvh], sink_cols[kvh])
                    p16 = p.astype(BF16)
                    o = _dot(p16, v16)
                    gate = _stack_heads(g_ref, sb, kvh)
                    dog_ = _stack_heads(dog_ref, sb, kvh)
                    dg_parts.append(dog_ * o * _dsilu(gate))
                    do_ = dog_ * _silu(gate)
                    do16 = do_.astype(BF16)
                    dp = _dot(do16, v16, NT)
                    delta = _row_sums(p * dp)
                    ds16 = (p * (dp - jnp.concatenate([delta, delta], axis=1))).astype(BF16)
                    dsink = -ps * delta
                    for g in range(GRP):
                        dsk += jnp.where(lane == GRP * kvh + g,
                                         jnp.sum(dsink[SUB * g:SUB * g + SUB, :HEADS], axis=0, keepdims=True), 0.0)
                    dqn = _dot(ds16, k16) * (HD ** -0.5)
                    dk = (_dot((y * qw_).T.astype(BF16), ds16) * (HD ** -0.5)).T
                    dv = _dot(do_.T.astype(BF16), p16).T
                    band_parts[kvh] += widen(dk[:NBAND], sb)
                    band_parts[2 + kvh] += widen(dv[:NBAND], sb)
                    meta_parts[kvh] += dk[NBAND:NBAND + N_META]
                    meta_parts[2 + kvh] += dv[NBAND:NBAND + N_META]
                    gy = dqn * qw_
                    dq_parts.append(r * (gy - y * (_row_sums(y * gy)[:, :HD] * (1.0 / HD))))
                    dqw += jnp.sum(dqn * y, axis=0, keepdims=True)
                dq_ref[rows, :] = _unstack_heads(dq_parts).astype(BF16)
                dg_ref[rows, :] = _unstack_heads(dg_parts).astype(BF16)
            band = jnp.concatenate(band_parts, axis=1)
            prevp[...] = band[:BLK]
            curp[...] = band[BLK:]
            metap[...] += jnp.concatenate(meta_parts, axis=1)
            dqw_ref[...] += dqw
            dsk_ref[...] += dsk

        tot = carry[...] + prevp[...]
        kprev = kvp_ref[...]
        dk0, w0 = knorm_bwd(kprev[:, 0:HD], tot[:, 0:HD], kw_)
        dk1, w1 = knorm_bwd(kprev[:, HD:2 * HD], tot[:, HD:2 * HD], kw_)
        dkv_ref[...] = jnp.concatenate([dk0, dk1, tot[:, 2 * HD:]], axis=1)
        dkw_ref[...] += w0 + w1
        carry[...] = curp[...]

        @pl.when(n == nblk)
        def _():
            mt = metap[...]
            km = kvm_ref[...]
            m0, v0 = knorm_bwd(km[:, 0:HD], mt[:, 0:HD], kw_)
            m1, v1 = knorm_bwd(km[:, HD:2 * HD], mt[:, HD:2 * HD], kw_)
            dkvm_ref[...] = jnp.concatenate([m0, m1, mt[:, 2 * HD:]], axis=1)
            dkw_ref[...] += v0 + v1

    small = lambda w: pl.BlockSpec((1, w), lambda n: (0, 0))
    cl = lambda n: jnp.minimum(n, nblk - 1)
    return pl.pallas_call(
        body, grid=(nblk + 1,),
        in_specs=[pl.BlockSpec((BLK, 1024), lambda n: (cl(n), 0)), pl.BlockSpec((BLK, 1024), lambda n: (cl(n), 1)),
                  pl.BlockSpec((BLK, 1024), lambda n: (cl(n), 0))]
        + _kv_specs(nblk, True) + [small(HD), small(HD), small(HEADS)],
        out_specs=[pl.BlockSpec((BLK, 1024), lambda n: (cl(n), 0)), pl.BlockSpec((BLK, 1024), lambda n: (cl(n), 0)),
                   pl.BlockSpec((BLK, KVW), lambda n: (jnp.maximum(n - 1, 0), 0)),
                   pl.BlockSpec((N_META, KVW), lambda n: (0, 0)), small(HD), small(HD), small(HEADS)],
        out_shape=[SDS((rows, 1024), BF16), SDS((rows, 1024), BF16), SDS((rows, KVW), F32), SDS((N_META, KVW), F32),
                   SDS((1, HD), F32), SDS((1, HD), F32), SDS((1, HEADS), F32)],
        scratch_shapes=[pltpu.VMEM((BLK, KVW), F32), pltpu.VMEM((BLK, KVW), F32), pltpu.VMEM((BLK, KVW), F32),
                        pltpu.VMEM((N_META, KVW), F32), pltpu.VMEM((4, TR, TK), F32)],
        compiler_params=_cp("arbitrary"), name="attn_bwd")(u, u, dog, u, u, u, qw, kw, sinks)


def _tri_inv(m, ii, jj):
    eye = (ii == jj).astype(F32)
    mb = jnp.where((ii >> 3) == (jj >> 3), m, 0.0)
    m2 = _bdot(mb, mb, "nn", True)
    m4 = _bdot(m2, m2, "nn", True)
    x = _bdot(_bdot(eye - mb, eye + m2, "nn", True), eye + m4, "nn", True)
    for sh in (3, 4, 5):
        lb = jnp.where(((ii >> (sh + 1)) == (jj >> (sh + 1))) & ((ii >> sh) != (jj >> sh)), m, 0.0)
        x = x - _bdot(_bdot(x, lb, "nn", True), x, "nn", True)
    return x


HB = 16


def _bdot(a, b, kind, split=False):
    dims = {"nn": ((2,), (1,)), "nt": ((2,), (2,)), "tn": ((1,), (1,))}[kind]
    dg = lambda p, q: lax.dot_general(p, q, (dims, ((0,), (0,))), preferred_element_type=F32)
    if not split:
        return dg(a, b)
    ah, bh = a.astype(BF16), b.astype(BF16)
    al, bl = (a - ah.astype(F32)).astype(BF16), (b - bh.astype(F32)).astype(BF16)
    return (dg(ah, bl) + dg(al, bh)) + dg(ah, bh)


def _head_cols(hv, beta, gc, gct, lane):
    sel = lane == hv
    return _pick(beta, sel), _pick(gc, sel), gct[pl.ds(hv, 1), :]


def _conv_group(xc_ref, xp_ref, cw_ref, off, first):
    xp = jnp.where(first, 0.0, xp_ref[:, pl.ds(off, DK)])
    xx = jnp.concatenate([xp, xc_ref[:, pl.ds(off, DK)]], axis=0)
    y = cw_ref[0:1, pl.ds(off, DK)] * xx[5:5 + CH]
    for j in range(1, 4):
        y += cw_ref[j:j + 1, pl.ds(off, DK)] * xx[5 + j:5 + j + CH]
    return xx, y


def _gates(ba, al, dtb, c):
    row = c * CH + lax.broadcasted_iota(jnp.int32, (CH, DN_H), 0)
    real = row >= PAD
    xa = ba[:, DN_H:2 * DN_H] + dtb
    beta = jnp.where(real, jax.nn.sigmoid(ba[:, 0:DN_H]), 0.0)
    g = jnp.where(real, -jnp.exp(al) * jax.nn.softplus(xa), 0.0)
    return real, xa, beta, g


def _pick(x, sel):
    return jnp.sum(jnp.where(sel, x, 0.0), axis=1, keepdims=True)


def _chunk_specs(width_blocks):
    return [pl.BlockSpec((CH, 4096), lambda c: (c, 0)),
            pl.BlockSpec((8, 4096), lambda c: (jnp.maximum(8 * c - 1, 0), 0)),
            pl.BlockSpec((CH, DK), lambda c: (c, 48))]


def dn_prep(udn, conv_w, a_log, dt_bias):
    rows = udn.shape[0]
    nch = rows // CH

    def body(xc_ref, xp_ref, ba_ref, cw_ref, al_ref, dtb_ref,
             qn_ref, kn_ref, sv_ref, gc_ref, beta_ref, u_ref, w_ref, qe_ref, ks_ref, p_ref, at_ref, pt_ref,
             qet_ref, wt_ref, kst_ref, gct):
        c = pl.program_id(0)
        first = c == 0
        _, _, beta, g = _gates(ba_ref[...], al_ref[...], dtb_ref[...], c)
        ii = lax.broadcasted_iota(jnp.int32, (CH, CH), 0)
        jj = lax.broadcasted_iota(jnp.int32, (CH, CH), 1)
        gc = _dot((ii >= jj).astype(F32), g, precision=HI)
        gc_ref[...] = gc
        beta_ref[...] = beta
        gct[...] = gc.T

        def qk_body(kh, carry):
            off = pl.multiple_of(kh * DK, DK)
            _, yq = _conv_group(xc_ref, xp_ref, cw_ref, off, first)
            sq = _silu(yq)
            qn_ref[:, pl.ds(off, DK)] = sq * lax.rsqrt(jnp.sum(sq * sq, axis=-1, keepdims=True) + EPS) * (DK ** -0.5)
            _, yk = _conv_group(xc_ref, xp_ref, cw_ref, pl.multiple_of(1024 + kh * DK, DK), first)
            sk = _silu(yk)
            kn_ref[:, pl.ds(off, DK)] = sk * lax.rsqrt(jnp.sum(sk * sk, axis=-1, keepdims=True) + EPS)
            return carry

        lax.fori_loop(0, DN_KH, qk_body, 0)
        lane = lax.broadcasted_iota(jnp.int32, (CH, DN_H), 1)
        zpad = jnp.zeros((CH, DK - CH), F32)

        def v_group(grp, carry):
            offs, ks_, qs_, vs_, cols = [], [], [], [], []
            for i in range(HB):
                hv = grp * HB + i
                offs.append(pl.multiple_of(hv * DK, DK))
                koff = pl.multiple_of((grp * (HB // 2) + i // 2) * DK, DK)
                _, yv = _conv_group(xc_ref, xp_ref, cw_ref, pl.multiple_of(2048 + hv * DK, DK), first)
                vs_.append(_silu(yv))
                sv_ref[:, pl.ds(offs[i], DK)] = vs_[i]
                ks_.append(kn_ref[:, pl.ds(koff, DK)])
                qs_.append(qn_ref[:, pl.ds(koff, DK)])
                cols.append(_head_cols(hv, beta, gc, gct, lane))
            k, q, v = jnp.stack(ks_), jnp.stack(qs_), jnp.stack(vs_)
            beta_c, gc_c, gc_r = (jnp.stack([c_[j] for c_ in cols]) for j in range(3))
            dec = jnp.exp(jnp.where(ii >= jj, gc_c - gc_r, NEG))
            eg = jnp.exp(gc_c)
            kb = k * beta_c
            k16 = k.astype(BF16)
            m = jnp.where(ii > jj, _bdot(kb.astype(BF16), k16, "nt") * dec, 0.0)
            a = _tri_inv(m, ii, jj)
            uw = _bdot(a, jnp.concatenate([v * beta_c, kb * eg], axis=2), "nn", True)
            p = _bdot(q.astype(BF16), k16, "nt") * dec
            qe = q * eg
            ksx = k * jnp.exp(gc_c[:, CH - 1:CH, :] - gc_c)
            tslot = lambda x: jnp.concatenate([x.T, jnp.zeros((DK, DK - CH), F32)], axis=1).astype(BF16)
            for i in range(HB):
                sl = pl.ds(offs[i], DK)
                u_ref[:, sl] = uw[i, :, :DK]
                w_ref[:, sl] = uw[i, :, DK:]
                qe_ref[:, sl] = qe[i].astype(BF16)
                ks_ref[:, sl] = ksx[i].astype(BF16)
                p_ref[:, sl] = jnp.concatenate([p[i], zpad], axis=1).astype(BF16)
                at_ref[:, sl] = jnp.concatenate([a[i].T, zpad], axis=1)
                pt_ref[:, sl] = jnp.concatenate([p[i].T, zpad], axis=1).astype(BF16)
                qet_ref[:, sl] = tslot(qe[i])
                wt_ref[:, sl] = tslot(uw[i, :, DK:])
                kst_ref[:, sl] = tslot(ksx[i])
            return carry

        lax.fori_loop(0, DN_H // HB, v_group, 0)

    full = lambda shape: pl.BlockSpec(shape, lambda c: (0, 0))
    blk = lambda w: pl.BlockSpec((CH, w), lambda c: (c, 0))
    return pl.pallas_call(
        body, grid=(nch,),
        in_specs=_chunk_specs(0) + [full((4, 4096)), full((1, DN_H)), full((1, DN_H))],
        out_specs=[blk(1024), blk(1024), blk(2048), blk(DN_H), blk(DN_H), blk(2048), blk(2048), blk(2048), blk(2048),
                   blk(2048), blk(2048), blk(2048)] + [pl.BlockSpec((DK, 2048), lambda c: (c, 0))] * 3,
        out_shape=[SDS((rows, 1024), F32), SDS((rows, 1024), F32), SDS((rows, 2048), F32), SDS((rows, DN_H), F32),
                   SDS((rows, DN_H), F32), SDS((rows, 2048), F32), SDS((rows, 2048), F32), SDS((rows, 2048), BF16),
                   SDS((rows, 2048), BF16), SDS((rows, 2048), BF16), SDS((rows, 2048), F32),
                   SDS((rows, 2048), BF16)] + [SDS((2 * rows, 2048), BF16)] * 3,
        scratch_shapes=[pltpu.VMEM((DN_H, CH), F32)],
        compiler_params=_cp("parallel"), name="dn_prep")(udn, udn, udn, conv_w, a_log, dt_bias)


def dn_scan(u, w, qe, kst, p, gc):
    rows = u.shape[0]
    nch = rows // CH

    def body(u_ref, w_ref, qe_ref, kst_ref, p_ref, gc_ref, o_ref, vn_ref, st_ref, s_scr):
        @pl.when(pl.program_id(0) == 0)
        def _():
            s_scr[...] = jnp.zeros_like(s_scr)

        gl_row = gc_ref[CH - 1:CH, :]
        lane = lax.broadcasted_iota(jnp.int32, (1, DN_H), 1)

        def group(grp, carry):
            base = grp * HB
            sls = [pl.ds(pl.multiple_of((base + i) * DK, DK), DK) for i in range(HB)]
            heads = lambda ref: jnp.stack([ref[:, sl] for sl in sls])
            s = s_scr[pl.ds(base, HB)]
            st_ref[0, pl.ds(base, HB)] = s
            s16 = s.astype(BF16)
            vn = heads(u_ref) - _bdot(heads(w_ref).astype(BF16), s16, "nn")
            vn16 = vn.astype(BF16)
            o = _bdot(heads(qe_ref), s16, "nn") + _bdot(heads(p_ref)[:, :, 0:CH], vn16, "nn")
            egl = jnp.exp(jnp.stack([_pick(gl_row, lane == base + i) for i in range(HB)]))
            s_scr[pl.ds(base, HB)] = s * egl + _bdot(heads(kst_ref)[:, :, 0:CH], vn16, "nn")
            for i in range(HB):
                vn_ref[:, sls[i]] = vn16[i]
                o_ref[:, sls[i]] = o[i]
            return carry

        lax.fori_loop(0, DN_H // HB, group, 0)

    blk = lambda wd: pl.BlockSpec((CH, wd), lambda c: (c, 0))
    return pl.pallas_call(
        body, grid=(nch,),
        in_specs=[blk(2048)] * 3 + [pl.BlockSpec((DK, 2048), lambda c: (c, 0)), blk(2048), blk(DN_H)],
        out_specs=[blk(2048), blk(2048), pl.BlockSpec((1, DN_H, DK, DK), lambda c: (c, 0, 0, 0))],
        out_shape=[SDS((rows, 2048), F32), SDS((rows, 2048), BF16), SDS((nch, DN_H, DK, DK), F32)],
        scratch_shapes=[pltpu.VMEM((DN_H, DK, DK), F32)],
        compiler_params=_cp("arbitrary"), name="dn_scan")(u, w, qe, kst, p, gc)


def dn_out_fwd(o, udn, ow, wout, h1, tgt):
    rows = o.shape[0]
    tm = _row_tile(rows)
    nt = rows // tm

    def body(o_ref, z_ref, ow_ref, w_ref, h_ref, t_ref, dh_ref, on_ref, ls_ref):
        for hv in range(DN_H):
            sl = slice(hv * DK, hv * DK + DK)
            oh = o_ref[:, sl]
            on_ref[:, sl] = (oh * _rms(oh) * ow_ref[...] * _silu(z_ref[:, sl])).astype(BF16)
        h2 = h_ref[...] + _dot(on_ref[...], w_ref[...])
        row = pl.program_id(0) * tm + lax.broadcasted_iota(jnp.int32, (tm, 1), 0)
        err = jnp.where(row >= BLK, h2 - t_ref[...], 0.0)
        dh_ref[...] = err * (1.0 / D_MODEL)
        ls_ref[0] = jnp.sum(err * err, axis=0, keepdims=True)

    return pl.pallas_call(
        body, grid=(nt,),
        in_specs=[pl.BlockSpec((tm, 2048), lambda i: (i, 0)), pl.BlockSpec((tm, 2048), lambda i: (i, 2)),
                  pl.BlockSpec((1, DK), lambda i: (0, 0)), pl.BlockSpec((2048, D_MODEL), lambda i: (0, 0)),
                  pl.BlockSpec((tm, D_MODEL), lambda i: (i, 0)), pl.BlockSpec((tm, D_MODEL), lambda i: (i, 0))],
        out_specs=[pl.BlockSpec((tm, D_MODEL), lambda i: (i, 0)), pl.BlockSpec((tm, 2048), lambda i: (i, 0)),
                   pl.BlockSpec((1, 1, D_MODEL), lambda i: (i, 0, 0))],
        out_shape=[SDS((rows, D_MODEL), F32), SDS((rows, 2048), BF16), SDS((nt, 1, D_MODEL), F32)],
        compiler_params=_cp("parallel"), name="dn_out_fwd")(o, udn, ow, wout, h1, tgt)


def dn_out_bwd(dh2, wout, o, udn, ow):
    rows = o.shape[0]
    tm = _row_tile(rows)
    nt = rows // tm

    def body(dh_ref, w_ref, o_ref, z_ref, ow_ref, do_ref, dz_ref, dow_ref):
        don = _dot(dh_ref[...].astype(BF16), w_ref[...], NT)
        ow_ = ow_ref[...]
        dow = jnp.zeros((1, DK), F32)
        for hv in range(DN_H):
            sl = slice(hv * DK, hv * DK + DK)
            oh = o_ref[:, sl]
            r = _rms(oh)
            y = oh * r
            z = z_ref[:, sl]
            dn = don[:, sl] * _silu(z)
            dz_ref[:, sl] = (don[:, sl] * (y * ow_) * _dsilu(z)).astype(BF16)
            dy = dn * ow_
            do_ref[:, sl] = r * (dy - y * jnp.mean(y * dy, axis=-1, keepdims=True))
            dow += jnp.sum(dn * y, axis=0, keepdims=True)
        dow_ref[0] = dow

    return pl.pallas_call(
        body, grid=(nt,),
        in_specs=[pl.BlockSpec((tm, D_MODEL), lambda i: (i, 0)), pl.BlockSpec((2048, D_MODEL), lambda i: (0, 0)),
                  pl.BlockSpec((tm, 2048), lambda i: (i, 0)), pl.BlockSpec((tm, 2048), lambda i: (i, 2)),
                  pl.BlockSpec((1, DK), lambda i: (0, 0))],
        out_specs=[pl.BlockSpec((tm, 2048), lambda i: (i, 0)), pl.BlockSpec((tm, 2048), lambda i: (i, 0)),
                   pl.BlockSpec((1, 1, DK), lambda i: (i, 0, 0))],
        out_shape=[SDS((rows, 2048), F32), SDS((rows, 2048), BF16), SDS((nt, 1, DK), F32)],
        compiler_params=_cp("parallel"), name="dn_out_bwd")(dh2, wout, o, udn, ow)


def dn_scan_bwd(do, qn, kn, sv, gc, beta, at, pt, u, w, vn, qet, wt, ks, st):
    rows = do.shape[0]
    nch = rows // CH

    def body(do_ref, q_ref, k_ref, v_ref, gc_ref, beta_ref, at_ref, pt_ref, u_ref, w_ref, vn_ref, qet_ref, wt_ref,
             ks_ref, st_ref, dq_ref, dk_ref, dv_ref, dbeta_ref, dg_ref, ds_scr, gct):
        @pl.when(pl.program_id(0) == 0)
        def _():
            ds_scr[...] = jnp.zeros_like(ds_scr)

        gc, beta = gc_ref[...], beta_ref[...]
        gct[...] = gc.T
        ii = lax.broadcasted_iota(jnp.int32, (CH, CH), 0)
        jj = lax.broadcasted_iota(jnp.int32, (CH, CH), 1)
        lane = lax.broadcasted_iota(jnp.int32, (CH, DN_H), 1)
        last = lax.broadcasted_iota(jnp.int32, (CH, 1), 0) == CH - 1

        def group(grp, carry):
            dbeta_acc, dgc_acc = carry
            base = grp * HB
            sls = [pl.ds(pl.multiple_of((base + i) * DK, DK), DK) for i in range(HB)]
            ksls = [pl.ds(pl.multiple_of((grp * (HB // 2) + j) * DK, DK), DK) for j in range(HB // 2)]
            heads = lambda ref: jnp.stack([ref[:, sl] for sl in sls])
            kheads = lambda ref: jnp.stack([ref[:, ksls[i // 2]] for i in range(HB)])
            cols = [_head_cols(base + i, beta, gc, gct, lane) for i in range(HB)]
            beta_c, gc_c, gc_r = (jnp.stack([c_[j] for c_ in cols]) for j in range(3))
            k, q, v = kheads(k_ref), kheads(q_ref), heads(v_ref)
            dec = jnp.exp(jnp.where(ii >= jj, gc_c - gc_r, NEG))
            eg = jnp.exp(gc_c)
            gl = gc_c[:, CH - 1:CH, :]
            e2 = jnp.exp(gl - gc_c)
            egl = jnp.exp(gl)
            k16, q16 = k.astype(BF16), q.astype(BF16)
            do16 = heads(do_ref).astype(BF16)
            s = st_ref[0, pl.ds(base, HB)]
            s16 = s.astype(BF16)
            dso = ds_scr[pl.ds(base, HB)]
            dso16 = dso.astype(BF16)
            wf, uf, vn16 = heads(w_ref), heads(u_ref), heads(vn_ref)
            kb = k * beta_c
            kb16 = kb.astype(BF16)
            pm = _bdot(q16, k16, "nt") * dec
            m = jnp.where(ii > jj, _bdot(kb16, k16, "nt") * dec, 0.0)
            dvn = _bdot(heads(pt_ref)[:, :, 0:CH], do16, "nn") + _bdot(heads(ks_ref), dso16, "nn")
            dvn16 = dvn.astype(BF16)
            ds_scr[pl.ds(base, HB)] = (egl * dso + _bdot(heads(qet_ref)[:, :, 0:CH], do16, "nn")
                                       - _bdot(heads(wt_ref)[:, :, 0:CH], dvn16, "nn"))
            dpm = jnp.where(ii >= jj, _bdot(do16, vn16, "nt"), 0.0)
            dqk16 = (dpm * dec).astype(BF16)
            dqe = _bdot(do16, s16, "nt")
            dq = eg * dqe + _bdot(dqk16, k16, "nn")
            dks = _bdot(vn16, dso16, "nt")
            dw = -_bdot(dvn16, s16, "nt")
            dbvk = _bdot(heads(at_ref)[:, :, 0:CH], jnp.concatenate([dvn, dw], axis=2), "nn", True)
            dbv, dbk = dbvk[:, :, :DK], dbvk[:, :, DK:]
            dm = jnp.where(ii > jj, -_bdot(dbvk, jnp.concatenate([uf, wf], axis=2), "nt", True), 0.0)
            g16 = (dm * dec).astype(BF16)
            dkb = _bdot(g16, k16, "nn")
            dk = (_bdot(dqk16, q16, "tn") + e2 * dks + _bdot(g16, kb16, "tn") + beta_c * (eg * dbk + dkb))
            e = dpm * pm + dm * m
            rsum = lambda x: jnp.sum(x, axis=2, keepdims=True)
            r_bk, r_qe, r_beta, r_ks = rsum(dbk * k), rsum(q * dqe), rsum(dbv * v + dkb * k), rsum(dks * k)
            t = r_ks * e2
            dgl = jnp.sum(t, axis=1, keepdims=True) + egl * rsum(jnp.sum(dso * s, axis=1, keepdims=True))
            deg = r_qe + beta_c * r_bk
            dgc = rsum(e) - t + deg * eg + jnp.where(last, dgl, 0.0)
            dgrow = -jnp.sum(e, axis=1, keepdims=True)
            dv = beta_c * dbv
            dbeta = r_beta + eg * r_bk
            for i in range(HB):
                dv_ref[:, sls[i]] = dv[i]
                sel = lane == base + i
                dbeta_acc = jnp.where(sel, dbeta[i], dbeta_acc)
                dgc_acc = jnp.where(sel, dgc[i], dgc_acc)
                gct[pl.ds(base + i, 1), :] = dgrow[i]
            for j in range(HB // 2):
                dq_ref[:, ksls[j]] = dq[2 * j] + dq[2 * j + 1]
                dk_ref[:, ksls[j]] = dk[2 * j] + dk[2 * j + 1]
            return dbeta_acc, dgc_acc

        zero = jnp.zeros((CH, DN_H), F32)
        dbeta_acc, dgc_acc = lax.fori_loop(0, DN_H // HB, group, (zero, zero))
        dbeta_ref[...] = dbeta_acc
        dg_ref[...] = _dot((ii <= jj).astype(F32), dgc_acc + gct[...].T, precision=HI)

    rev = lambda wd: pl.BlockSpec((CH, wd), lambda i: (nch - 1 - i, 0))
    rev_t = pl.BlockSpec((DK, 2048), lambda i: (nch - 1 - i, 0))
    return pl.pallas_call(
        body, grid=(nch,),
        in_specs=[rev(2048), rev(1024), rev(1024), rev(2048), rev(DN_H), rev(DN_H), rev(2048), rev(2048), rev(2048),
                  rev(2048), rev(2048), rev_t, rev_t, rev(2048),
                  pl.BlockSpec((1, DN_H, DK, DK), lambda i: (nch - 1 - i, 0, 0, 0))],
        out_specs=[rev(1024), rev(1024), rev(2048), rev(DN_H), rev(DN_H)],
        out_shape=[SDS((rows, 1024), F32), SDS((rows, 1024), F32), SDS((rows, 2048), F32), SDS((rows, DN_H), F32),
                   SDS((rows, DN_H), F32)],
        scratch_shapes=[pltpu.VMEM((DN_H, DK, DK), F32), pltpu.VMEM((DN_H, CH), F32)],
        compiler_params=_cp("arbitrary"), name="dn_scan_bwd")(
            do, qn, kn, sv, gc, beta, at, pt, u, w, vn, qet, wt, ks, st)


def dn_prep_bwd(udn, conv_w, a_log, dt_bias, dqn, dkn, dv, dbeta, dg):
    rows = udn.shape[0]
    nch = rows // CH
    ext = CH + 8

    def body(xc_ref, xp_ref, ba_ref, xn_ref, dqn_n, dkn_n, dv_n, cw_ref, al_ref, dtb_ref, dqn_ref, dkn_ref, dv_ref,
             dbeta_ref, dg_ref, dx_ref, dba_ref, dcw_ref, dal_ref, ddtb_ref):
        c = pl.program_id(0)
        first = c == 0
        own = (lax.broadcasted_iota(jnp.int32, (ext, 1), 0) < CH) | (c < nch - 1)

        @pl.when(first)
        def _():
            dcw_ref[...] = jnp.zeros_like(dcw_ref)
            dal_ref[...] = jnp.zeros_like(dal_ref)
            ddtb_ref[...] = jnp.zeros_like(ddtb_ref)

        real, xa, beta, g = _gates(ba_ref[...], al_ref[...], dtb_ref[...], c)
        dgm = jnp.where(real, dg_ref[...], 0.0)
        da = dgm * (-jnp.exp(al_ref[...])) * jax.nn.sigmoid(xa)
        dal_ref[...] += jnp.sum(dgm * g, axis=0, keepdims=True)
        ddtb_ref[...] += jnp.sum(da, axis=0, keepdims=True)
        dba_ref[...] = jnp.zeros_like(dba_ref)
        dba_ref[:, 0:DN_H] = jnp.where(real, dbeta_ref[...] * beta * (1.0 - beta), 0.0)
        dba_ref[:, DN_H:2 * DN_H] = da

        def through_conv(off, g_cur, g_next, grad_fn):
            sl = pl.ds(off, DK)
            xx = jnp.concatenate([jnp.where(first, 0.0, xp_ref[:, sl]), xc_ref[:, sl], xn_ref[:, sl]], axis=0)
            y = cw_ref[0:1, sl] * xx[5:5 + ext]
            for j in range(1, 4):
                y += cw_ref[j:j + 1, sl] * xx[5 + j:5 + j + ext]
            dy = jnp.where(own, grad_fn(_silu(y), jnp.concatenate([g_cur, g_next], axis=0)) * _dsilu(y), 0.0)
            dx = cw_ref[0:1, sl] * dy[3:3 + CH]
            for j in range(1, 4):
                dx += cw_ref[j:j + 1, sl] * dy[3 - j:3 - j + CH]
            dx_ref[:, sl] = dx.astype(BF16)
            for j in range(4):
                dcw_ref[j:j + 1, sl] += jnp.sum(dy[:CH] * xx[5 + j:5 + j + CH], axis=0, keepdims=True)

        def l2_bwd(scale):
            def f(s, gin):
                r = lax.rsqrt(jnp.sum(s * s, axis=-1, keepdims=True) + EPS)
                nrm = s * r
                return (r * scale) * (gin - nrm * jnp.sum(nrm * gin, axis=-1, keepdims=True))
            return f

        def qk_body(kh, carry):
            sl = pl.ds(pl.multiple_of(kh * DK, DK), DK)
            through_conv(pl.multiple_of(kh * DK, DK), dqn_ref[:, sl], dqn_n[:, sl], l2_bwd(DK ** -0.5))
            through_conv(pl.multiple_of(1024 + kh * DK, DK), dkn_ref[:, sl], dkn_n[:, sl], l2_bwd(1.0))
            return carry

        lax.fori_loop(0, DN_KH, qk_body, 0)

        def v_body(hv, carry):
            sl = pl.ds(pl.multiple_of(hv * DK, DK), DK)
            through_conv(pl.multiple_of(2048 + hv * DK, DK), dv_ref[:, sl], dv_n[:, sl], lambda s, gin: gin)
            return carry

        lax.fori_loop(0, DN_H, v_body, 0)

    full = lambda shape: pl.BlockSpec(shape, lambda c: (0, 0))
    blk = lambda w: pl.BlockSpec((CH, w), lambda c: (c, 0))
    nxt = lambda w: pl.BlockSpec((8, w), lambda c: (jnp.minimum(8 * c + 8, rows // 8 - 1), 0))
    return pl.pallas_call(
        body, grid=(nch,),
        in_specs=_chunk_specs(0) + [nxt(4096), nxt(1024), nxt(1024), nxt(2048), full((4, 4096)), full((1, DN_H)),
                                    full((1, DN_H)), blk(1024), blk(1024), blk(2048), blk(DN_H), blk(DN_H)],
        out_specs=[blk(4096), blk(DK), full((8, 4096)), full((1, DN_H)), full((1, DN_H))],
        out_shape=[SDS((rows, 4096), BF16), SDS((rows, DK), F32), SDS((8, 4096), F32), SDS((1, DN_H), F32),
                   SDS((1, DN_H), F32)],
        compiler_params=_cp("arbitrary"), name="dn_prep_bwd")(
            udn, udn, udn, udn, dqn, dkn, dv, conv_w, a_log, dt_bias, dqn, dkn, dv, dbeta, dg)


def local_step(x, target, w):
    seq = x.shape[0]
    bf = lambda a: a.astype(BF16)
    h0 = jnp.concatenate([jnp.zeros((PAD, D_MODEL), F32), w["meta_tokens"], x], axis=0)
    tgt = jnp.concatenate([jnp.zeros((BLK, D_MODEL), F32), target], axis=0)
    win = w["attn_w_in"]
    wq, wkv, wg = win[:, :1024], win[:, 1024:1280], win[:, 1280:]
    wa_in = bf(jnp.concatenate([wq, wg, wkv], axis=1))
    wa_out = bf(w["attn_w_out"])
    wd_in = jnp.concatenate([bf(w["dn_w_in"]), jnp.zeros((D_MODEL, 96), BF16)], axis=1)
    wd_out = bf(w["dn_w_out"])
    qw, kw, sinks = w["attn_q_norm_w"], w["attn_k_norm_w"], w["attn_sinks"]
    cw, al, dtb, ow = w["dn_conv_w"], w["dn_a_log"], w["dn_dt_bias"], w["dn_o_norm_w"]

    ua, xn0 = norm_matmul(h0, w["attn_norm_w"], wa_in, 2304, "attn_in")
    og = attn_fwd(ua, qw, kw, sinks)
    h1 = matmul_residual(og, wa_out, h0, "attn_out")
    ud, xn1 = norm_matmul(h1, w["dn_norm_w"], wd_in, 896, "dn_in")
    qn, kn, sv, gc, beta, u, wy, qe, ks, p, at, pt, qet, wt, kst = dn_prep(ud, cw, al, dtb)
    o, vn, st = dn_scan(u, wy, qe, kst, p, gc)
    dh2, on, ls = dn_out_fwd(o, ud, ow, wd_out, h1, tgt)
    loss = (0.5 / D_MODEL) * jnp.sum(ls)

    do, dz, dow = dn_out_bwd(dh2, wd_out, o, ud, ow)
    g_dn_out = wgrad(on, dh2, "dn_out_wgrad")
    dqn, dkn, dv, dbeta, dg = dn_scan_bwd(do, qn, kn, sv, gc, beta, at, pt, u, wy, vn, qet, wt, ks, st)
    dxc, dba, dcw, dal, ddtb = dn_prep_bwd(ud, cw, al, dtb, dqn, dkn, dv, dbeta, dg)
    dh1, dnw1 = in_proj_bwd([dxc, dz, dba], [wd_in[:, :4096], wd_in[:, 4096:6144], wd_in[:, 6144:]],
                            h1, w["dn_norm_w"], dh2, "dn_in_bwd")
    g_dn_in = jnp.concatenate([wgrad(xn1, dxc, "dn_in_wgrad_qkv"), wgrad(xn1, dz, "dn_in_wgrad_z"),
                               wgrad(xn1, dba, "dn_in_wgrad_ba")[:, :2 * DN_H]], axis=1)

    dog = matmul_nt(dh1, wa_out, "attn_out_bwd")
    g_attn_out = wgrad(og, dh1, "attn_out_wgrad")
    dq, dgate, dkv, dkvm, dqw, dkw, dsk = attn_bwd(ua, qw, kw, sinks, dog)
    dkv = dkv.at[PAD:BLK].add(dkvm)
    dh0, dnw0 = in_proj_bwd([dq, dgate, dkv], [wa_in[:, :1024], wa_in[:, 1024:2048], wa_in[:, 2048:]],
                            h0, w["attn_norm_w"], dh1, "attn_in_bwd")
    g_attn_in = jnp.concatenate([wgrad(xn0, dq, "attn_in_wgrad_q"), wgrad(xn0, dkv, "attn_in_wgrad_kv"),
                                 wgrad(xn0, dgate, "attn_in_wgrad_g")], axis=1)
    grads = {
        "meta_tokens": dh0[PAD:BLK], "attn_norm_w": jnp.sum(dnw0, axis=0), "attn_w_in": g_attn_in,
        "attn_q_norm_w": dqw, "attn_k_norm_w": dkw, "attn_sinks": dsk, "attn_w_out": g_attn_out,
        "dn_norm_w": jnp.sum(dnw1, axis=0), "dn_w_in": g_dn_in, "dn_conv_w": dcw[:4], "dn_a_log": dal,
        "dn_dt_bias": ddtb, "dn_o_norm_w": jnp.sum(dow, axis=0), "dn_w_out": g_dn_out,
    }
    return loss, dh0[BLK:BLK + seq], grads


WEIGHTS = ["meta_tokens", "attn_norm_w", "attn_w_in", "attn_q_norm_w", "attn_k_norm_w", "attn_sinks", "attn_w_out",
           "dn_norm_w", "dn_w_in", "dn_conv_w", "dn_a_log", "dn_dt_bias", "dn_o_norm_w", "dn_w_out"]
SHARDED = {"attn_w_in": ((1024, 2304), 1), "attn_w_out": ((1024, 1024), 0), "dn_w_in": ((1024, 6176), 1),
           "dn_w_out": ((2048, 1024), 0), "dn_conv_w": ((4, 4096), 1), "meta_tokens": ((16, 1024), 1),
           "dn_norm_w": ((1, 1024), 1)}
REPLICATED = {"attn_norm_w": 1024, "attn_q_norm_w": 64, "attn_k_norm_w": 64, "attn_sinks": 16, "dn_a_log": 16,
              "dn_dt_bias": 16, "dn_o_norm_w": 128}
N_CHIPS = 4
PACK_ROWS = 2912
HALF_ROWS = PACK_ROWS // 2
SMALL_ROWS = 8


def _shard_shape(name):
    (r, c), axis = SHARDED[name]
    return (r // N_CHIPS, c) if axis == 0 else (r, c // N_CHIPS)


def _pack(parts, rows):
    flat = jnp.concatenate([p.reshape(-1) for p in parts])
    return jnp.pad(flat, (0, rows * 1024 - flat.shape[0])).reshape(rows, 1024)


def pack_shard(shards):
    return _pack([shards[n] for n in SHARDED], PACK_ROWS)


def unpack_shard(buf):
    flat, out, pos = buf.reshape(-1), {}, 0
    for n in SHARDED:
        shp = _shard_shape(n)
        size = shp[0] * shp[1]
        out[n] = flat[pos:pos + size].reshape(shp)
        pos += size
    return out


MATRICES = ("attn_w_in", "attn_w_out", "dn_w_in", "dn_w_out")


def pack_gather(shards):
    big = [shards[n].astype(BF16).reshape(-1) for n in MATRICES]
    small = jnp.concatenate([shards[n].reshape(-1) for n in SHARDED if n not in MATRICES])
    flat = jnp.concatenate(big + [lax.bitcast_convert_type(small, BF16).reshape(-1)])
    return jnp.pad(flat, (0, PACK_ROWS * 1024 - flat.shape[0])).reshape(PACK_ROWS, 1024)


def unpack_gather(buf):
    PER_F32 = 4 // jnp.dtype(buf.dtype).itemsize
    flat, out, pos = buf.reshape(-1), {}, 0
    for n in MATRICES:
        shp = _shard_shape(n)
        out[n] = flat[pos:pos + shp[0] * shp[1]].reshape(shp)
        pos += shp[0] * shp[1]
    for n in SHARDED:
        if n not in MATRICES:
            shp = _shard_shape(n)
            raw = flat[pos:pos + shp[0] * shp[1] * PER_F32]
            out[n] = lax.bitcast_convert_type(raw.reshape(-1, PER_F32) if PER_F32 > 1 else raw, F32).reshape(shp)
            pos += shp[0] * shp[1] * PER_F32
    return out


def pack_small(vals):
    return _pack([vals[n] for n in REPLICATED], SMALL_ROWS)


def unpack_small(buf):
    flat, out, pos = buf.reshape(-1), {}, 0
    for n, size in REPLICATED.items():
        out[n] = flat[pos:pos + size].reshape(1, size)
        pos += size
    return out


ANY = pl.BlockSpec(memory_space=pl.ANY)


def _place():
    return lax.axis_index("x"), lax.axis_index("y"), lax.axis_index("c")


def chips_exchange(src, gather):
    r = src.shape[-2]

    def body(s_ref, o_ref, send_sems, recv_sems, local_sem):
        x, y, c = _place()
        me = 2 * x + y
        peers = [(1 - x, y), (x, 1 - y), (1 - x, 1 - y)]
        mine = pltpu.make_async_copy(s_ref if gather else s_ref.at[me], o_ref.at[me], local_sem)
        mine.start()

        def copy(k, to_block, from_block):
            px, py = peers[k]
            return pltpu.make_async_remote_copy(
                src_ref=s_ref if gather else s_ref.at[to_block], dst_ref=o_ref.at[from_block],
                send_sem=send_sems.at[k], recv_sem=recv_sems.at[k], device_id=(px, py, c), device_id_type=MESH)

        sends = [copy(k, 2 * px + py, me) for k, (px, py) in enumerate(peers)]
        for cp in sends:
            cp.start()
        for k, (px, py) in enumerate(peers):
            copy(k, me, 2 * px + py).wait_recv()
        for cp in sends:
            cp.wait_send()
        mine.wait()

    return pl.pallas_call(
        body, in_specs=[ANY], out_specs=ANY, out_shape=SDS((N_CHIPS, r, 1024), src.dtype),
        scratch_shapes=[pltpu.SemaphoreType.DMA((3,)), pltpu.SemaphoreType.DMA((3,)), pltpu.SemaphoreType.DMA],
        name="chips_gather" if gather else "chips_exchange")(src)


def sibling_exchange(src, name):
    def body(s_ref, o_ref, send_sem, recv_sem):
        x, y, c = _place()
        cp = pltpu.make_async_remote_copy(src_ref=s_ref, dst_ref=o_ref, send_sem=send_sem, recv_sem=recv_sem,
                                          device_id=(x, y, 1 - c), device_id_type=MESH)
        cp.start()
        cp.wait()

    return pl.pallas_call(
        body, in_specs=[ANY], out_specs=ANY, out_shape=SDS(src.shape, src.dtype),
        scratch_shapes=[pltpu.SemaphoreType.DMA, pltpu.SemaphoreType.DMA], name=name)(src)


def all_gather_small(src):
    def body(s_ref, o_ref, send_sems, recv_sems, local_sem):
        x, y, c = _place()
        flips = [(fx, fy, fc) for fx in (0, 1) for fy in (0, 1) for fc in (0, 1)][1:]
        idx = lambda px, py, pc: 4 * px + 2 * py + pc
        mine = pltpu.make_async_copy(s_ref, o_ref.at[idx(x, y, c)], local_sem)
        mine.start()

        def peer(k):
            fx, fy, fc = flips[k]
            return (1 - x if fx else x, 1 - y if fy else y, 1 - c if fc else c)

        def copy(k, block):
            return pltpu.make_async_remote_copy(
                src_ref=s_ref, dst_ref=o_ref.at[block], send_sem=send_sems.at[k], recv_sem=recv_sems.at[k],
                device_id=peer(k), device_id_type=MESH)

        sends = [copy(k, idx(x, y, c)) for k in range(7)]
        for cp in sends:
            cp.start()
        for k in range(7):
            copy(k, idx(*peer(k))).wait_recv()
        for cp in sends:
            cp.wait_send()
        mine.wait()

    return pl.pallas_call(
        body, in_specs=[ANY], out_specs=ANY, out_shape=SDS((8,) + src.shape, F32),
        scratch_shapes=[pltpu.SemaphoreType.DMA((7,)), pltpu.SemaphoreType.DMA((7,)), pltpu.SemaphoreType.DMA],
        name="all_gather_small")(src)


def sum_blocks(t, name):
    n, r, _ = t.shape
    tm = 208 if r % 208 == 0 else r

    def body(t_ref, o_ref):
        acc = t_ref[0]
        for i in range(1, n):
            acc = acc + t_ref[i]
        o_ref[...] = acc

    return pl.pallas_call(
        body, grid=(r // tm,), in_specs=[pl.BlockSpec((n, tm, 1024), lambda i: (0, i, 0))],
        out_specs=pl.BlockSpec((tm, 1024), lambda i: (i, 0)), out_shape=SDS((r, 1024), F32),
        compiler_params=_cp("parallel"), name=name)(t)


def adamw(w, g, m, v):
    rows = w.shape[0]
    tm = rows // 5 if rows % 40 == 0 else rows

    def body(w_ref, g_ref, m_ref, v_ref, d_ref, nm_ref, nv_ref):
        g_ = g_ref[...]
        m_ = ADAM_B1 * m_ref[...] + (1.0 - ADAM_B1) * g_
        v_ = ADAM_B2 * v_ref[...] + (1.0 - ADAM_B2) * (g_ * g_)
        m_hat = m_ / (1.0 - ADAM_B1 ** ADAM_STEP)
        v_hat = v_ / (1.0 - ADAM_B2 ** ADAM_STEP)
        d_ref[...] = -ADAM_LR * (m_hat / (jnp.sqrt(v_hat) + ADAM_EPS) + ADAM_WD * w_ref[...])
        nm_ref[...] = m_
        nv_ref[...] = v_

    spec = pl.BlockSpec((tm, 1024), lambda i: (i, 0))
    return pl.pallas_call(
        body, grid=(rows // tm,), in_specs=[spec] * 4, out_specs=[spec] * 3,
        out_shape=[SDS((rows, 1024), F32)] * 3, compiler_params=_cp("parallel"), name="adamw")(w, g, m, v)


LAYERED = ("attn_w_in", "attn_w_out", "dn_w_in", "dn_conv_w", "dn_w_out")


def _two_d(name, a):
    return a[0] if name in LAYERED else a


def kernel(x, meta_tokens, attn_norm_w, attn_w_in, attn_q_norm_w, attn_k_norm_w, attn_sinks, attn_w_out, dn_norm_w, dn_w_in, dn_conv_w, dn_a_log, dn_dt_bias, dn_o_norm_w, dn_w_out, loss_target, m_meta_tokens, m_attn_norm_w, m_attn_w_in, m_attn_q_norm_w, m_attn_k_norm_w, m_attn_sinks, m_attn_w_out, m_dn_norm_w, m_dn_w_in, m_dn_conv_w, m_dn_a_log, m_dn_dt_bias, m_dn_o_norm_w, m_dn_w_out, v_meta_tokens, v_attn_norm_w, v_attn_w_in, v_attn_q_norm_w, v_attn_k_norm_w, v_attn_sinks, v_attn_w_out, v_dn_norm_w, v_dn_w_in, v_dn_conv_w, v_dn_a_log, v_dn_dt_bias, v_dn_o_norm_w, v_dn_w_out):
    given = dict(zip(WEIGHTS, (meta_tokens, attn_norm_w, attn_w_in, attn_q_norm_w, attn_k_norm_w, attn_sinks,
                               attn_w_out, dn_norm_w, dn_w_in, dn_conv_w, dn_a_log, dn_dt_bias, dn_o_norm_w, dn_w_out)))
    mom1 = dict(zip(WEIGHTS, (m_meta_tokens, m_attn_norm_w, m_attn_w_in, m_attn_q_norm_w, m_attn_k_norm_w,
                              m_attn_sinks, m_attn_w_out, m_dn_norm_w, m_dn_w_in, m_dn_conv_w, m_dn_a_log,
                              m_dn_dt_bias, m_dn_o_norm_w, m_dn_w_out)))
    mom2 = dict(zip(WEIGHTS, (v_meta_tokens, v_attn_norm_w, v_attn_w_in, v_attn_q_norm_w, v_attn_k_norm_w,
                              v_attn_sinks, v_attn_w_out, v_dn_norm_w, v_dn_w_in, v_dn_conv_w, v_dn_a_log,
                              v_dn_dt_bias, v_dn_o_norm_w, v_dn_w_out)))
    two_d = lambda d: {n: _two_d(n, a) for n, a in d.items()}
    given, mom1, mom2 = two_d(given), two_d(mom1), two_d(mom2)
    c = lax.axis_index("c")

    w_shard = pack_shard(given)
    mine = chips_exchange(lax.dynamic_slice_in_dim(pack_gather(given), c * HALF_ROWS, HALF_ROWS, axis=0), True)
    theirs = sibling_exchange(mine, "gather_swap")
    gathered = jnp.where(c == 0, jnp.concatenate([mine, theirs], axis=1), jnp.concatenate([theirs, mine], axis=1))
    per_chip = [unpack_gather(gathered[j]) for j in range(N_CHIPS)]
    full = {n: jnp.concatenate([pc[n] for pc in per_chip], axis=SHARDED[n][1]) for n in SHARDED}
    full.update({n: given[n] for n in REPLICATED})

    loss, dx, grads = local_step(x[0], loss_target[0], full)

    split = lambda n: jnp.split(grads[n], N_CHIPS, axis=SHARDED[n][1])
    g_all = jnp.stack([pack_shard({n: split(n)[j] for n in SHARDED}) for j in range(N_CHIPS)])
    keep = lax.dynamic_slice_in_dim(g_all, c * HALF_ROWS, HALF_ROWS, axis=1)
    give = lax.dynamic_slice_in_dim(g_all, (1 - c) * HALF_ROWS, HALF_ROWS, axis=1)
    got = sibling_exchange(give, "pair_exchange")
    pair = sum_blocks(jnp.stack([keep, got]).reshape(2, N_CHIPS * HALF_ROWS, 1024), "pair_sum")
    from_chips = chips_exchange(pair.reshape(N_CHIPS, HALF_ROWS, 1024), False)
    half = sum_blocks(from_chips, "chip_sum")
    other = sibling_exchange(half, "half_exchange")
    g_shard = jnp.where(c == 0, jnp.concatenate([half, other]), jnp.concatenate([other, half]))

    g_small = sum_blocks(all_gather_small(pack_small(grads)), "small_sum")

    stack = lambda big, small: jnp.concatenate([big, small])
    delta, new_m, new_v = adamw(stack(w_shard, pack_small(given)), stack(g_shard, g_small),
                                stack(pack_shard(mom1), pack_small(mom1)), stack(pack_shard(mom2), pack_small(mom2)))

    def unpack(buf):
        out = unpack_shard(buf[:PACK_ROWS])
        out.update(unpack_small(buf[PACK_ROWS:]))
        return [out[n][None] if n in LAYERED else out[n] for n in WEIGHTS]

    loss = lax.psum(loss, ("x", "y", "c"))
    return (loss, dx[None], *unpack(stack(g_shard, g_small)), *unpack(delta), *unpack(new_m), *unpack(new_v))
```

```python
import functools

import jax
import jax.numpy as jnp
from jax import lax
from jax.experimental import pallas as pl
from jax.experimental.pallas import tpu as pltpu

F32 = jnp.float32
BF16 = jnp.bfloat16
SDS = jax.ShapeDtypeStruct
MESH = pl.DeviceIdType.MESH

D_MODEL = 1024
N_META = 16
EPS = 1e-6
BLK = 128
CH = 64
PAD = BLK - N_META
HEADS = 16
HD = 64
KVW = 256
DN_H = 16
DN_KH = 8
DK = 128
SLOPES = [2.0 ** (-8.0 * (h + 1) / HEADS) for h in range(HEADS)]
NEG = -1e30
NT = (((1,), (1,)), ((), ()))
TN = (((0,), (0,)), ((), ()))
HI = lax.Precision.HIGHEST

ADAM_LR, ADAM_B1, ADAM_B2, ADAM_EPS, ADAM_WD, ADAM_STEP = 0.001, 0.9, 0.999, 1e-08, 0.01, 10

VMEM_LIMIT = 56 * 1024 * 1024


def _cp(*sem):
    return pltpu.CompilerParams(dimension_semantics=sem, vmem_limit_bytes=VMEM_LIMIT)


def _row_tile(rows):
    for t in (384, 256, 128):
        if rows % t == 0:
            return t
    raise ValueError(rows)


def _dot(a, b, dims=None, precision=None):
    if dims is None:
        return jnp.dot(a, b, preferred_element_type=F32, precision=precision)
    return lax.dot_general(a, b, dims, preferred_element_type=F32, precision=precision)


def _silu(x):
    return x * jax.nn.sigmoid(x)


def _dsilu(x):
    s = jax.nn.sigmoid(x)
    return s * (1.0 + x * (1.0 - s))


def _rms(x):
    return lax.rsqrt(jnp.mean(x * x, axis=-1, keepdims=True) + EPS)


def norm_matmul(h, nw, w, tn, name):
    rows, k = h.shape
    n = w.shape[1]
    tm = _row_tile(rows)

    def norm_body(h_ref, nw_ref, xn_ref):
        x = h_ref[...]
        xn_ref[...] = (x * _rms(x) * nw_ref[...]).astype(BF16)

    xn = pl.pallas_call(
        norm_body, grid=(rows // tm,),
        in_specs=[pl.BlockSpec((tm, k), lambda i: (i, 0)), pl.BlockSpec((1, k), lambda i: (0, 0))],
        out_specs=pl.BlockSpec((tm, k), lambda i: (i, 0)), out_shape=SDS((rows, k), BF16),
        compiler_params=_cp("parallel"), name=name + "_norm")(h, nw)

    def body(a_ref, w_ref, o_ref):
        o_ref[...] = _dot(a_ref[...], w_ref[...])

    out = pl.pallas_call(
        body, grid=(n // tn, rows // tm),
        in_specs=[pl.BlockSpec((tm, k), lambda j, i: (i, 0)), pl.BlockSpec((k, tn), lambda j, i: (0, j))],
        out_specs=pl.BlockSpec((tm, tn), lambda j, i: (i, j)), out_shape=SDS((rows, n), F32),
        compiler_params=_cp("parallel", "parallel"), name=name)(xn, w)
    return out, xn


def matmul_residual(a, w, res, name):
    rows, k = a.shape
    n = w.shape[1]
    tm = _row_tile(rows)

    def body(a_ref, w_ref, r_ref, o_ref):
        o_ref[...] = r_ref[...] + _dot(a_ref[...], w_ref[...])

    return pl.pallas_call(
        body, grid=(rows // tm,),
        in_specs=[pl.BlockSpec((tm, k), lambda i: (i, 0)), pl.BlockSpec((k, n), lambda i: (0, 0)),
                  pl.BlockSpec((tm, n), lambda i: (i, 0))],
        out_specs=pl.BlockSpec((tm, n), lambda i: (i, 0)),
        out_shape=SDS((rows, n), F32), compiler_params=_cp("parallel"), name=name)(a, w, res)


def wgrad(a, b, name):
    rows, k = a.shape
    n = b.shape[1]
    tm = _row_tile(rows)
    tn = min(n, 1024)

    def body(a_ref, b_ref, o_ref):
        @pl.when(pl.program_id(1) == 0)
        def _():
            o_ref[...] = jnp.zeros_like(o_ref)

        o_ref[...] += _dot(a_ref[...], b_ref[...].astype(BF16), TN)

    return pl.pallas_call(
        body, grid=(n // tn, rows // tm),
        in_specs=[pl.BlockSpec((tm, k), lambda j, i: (i, 0)), pl.BlockSpec((tm, tn), lambda j, i: (i, j))],
        out_specs=pl.BlockSpec((k, tn), lambda j, i: (0, j)),
        out_shape=SDS((k, n), F32), compiler_params=_cp("parallel", "arbitrary"), name=name)(a, b)


def in_proj_bwd(dus, ws, h, nw, dh_next, name):
    rows, k = h.shape
    tm = _row_tile(rows)
    nd = len(dus)
    nt = rows // tm

    def body(*refs):
        du_refs, w_refs = refs[:nd], refs[nd:2 * nd]
        h_ref, nw_ref, dhn_ref, dh_ref, dnw_ref = refs[2 * nd:]
        dxn = _dot(du_refs[0][...].astype(BF16), w_refs[0][...], NT)
        for du_ref, w_ref in zip(du_refs[1:], w_refs[1:]):
            dxn += _dot(du_ref[...].astype(BF16), w_ref[...], NT)
        x = h_ref[...]
        r = _rms(x)
        y = x * r
        gy = dxn * nw_ref[...]
        dh_ref[...] = dhn_ref[...] + r * (gy - y * jnp.mean(y * gy, axis=-1, keepdims=True))
        dnw_ref[0] = jnp.sum(dxn * y, axis=0, keepdims=True)

    in_specs = [pl.BlockSpec((tm, du.shape[1]), lambda i: (i, 0)) for du in dus]
    in_specs += [pl.BlockSpec(w.shape, lambda i: (0, 0)) for w in ws]
    in_specs += [pl.BlockSpec((tm, k), lambda i: (i, 0)), pl.BlockSpec((1, k), lambda i: (0, 0)),
                 pl.BlockSpec((tm, k), lambda i: (i, 0))]
    return pl.pallas_call(
        body, grid=(nt,), in_specs=in_specs,
        out_specs=[pl.BlockSpec((tm, k), lambda i: (i, 0)), pl.BlockSpec((1, 1, k), lambda i: (i, 0, 0))],
        out_shape=[SDS((rows, k), F32), SDS((nt, 1, k), F32)],
        compiler_params=_cp("parallel"), name=name)(*dus, *ws, h, nw, dh_next)


def matmul_nt(a, w, name):
    rows, k = a.shape
    n = w.shape[0]
    tm = _row_tile(rows)

    def body(a_ref, w_ref, o_ref):
        o_ref[...] = _dot(a_ref[...].astype(BF16), w_ref[...], NT)

    return pl.pallas_call(
        body, grid=(rows // tm,),
        in_specs=[pl.BlockSpec((tm, k), lambda i: (i, 0)), pl.BlockSpec((n, k), lambda i: (0, 0))],
        out_specs=pl.BlockSpec((tm, n), lambda i: (i, 0)),
        out_shape=SDS((rows, n), F32), compiler_params=_cp("parallel"), name=name)(a, w)


SUB = 64
GRP = 8
TR = GRP * SUB
NBAND = 192
TK = 256


def _tile_bias(n, sb):
    r = lax.broadcasted_iota(jnp.int32, (TR, TK), 0)
    c = lax.broadcasted_iota(jnp.int32, (TR, TK), 1)
    qi = r & (SUB - 1)
    d = BLK + qi - c
    dm = n * BLK + SUB * sb - PAD + NBAND + qi - c
    band = c < NBAND
    valid = (band & (d >= 0) & (d < BLK) & (c >= 2 * BLK - BLK * n - SUB * sb)) | (
        (c >= NBAND) & (c < NBAND + N_META) & (dm >= 0))
    return valid, jnp.where(band, d, jnp.minimum(dm, BLK)).astype(F32)


def _group_col(vals):
    g = lax.broadcasted_iota(jnp.int32, (TR, 1), 0) >> 6
    col = jnp.zeros((TR, 1), F32)
    for gi, v in enumerate(vals):
        col = jnp.where(g == gi, v, col)
    return col


def _stack_heads(ref, sb, kvh):
    return jnp.concatenate(
        [ref[SUB * sb:SUB * sb + SUB, HD * (GRP * kvh + g):HD * (GRP * kvh + g) + HD] for g in range(GRP)], axis=0)


def _unstack_heads(parts):
    return jnp.concatenate([parts[kvh][SUB * g:SUB * g + SUB] for kvh in range(2) for g in range(GRP)], axis=1)


def _tile_keys(band, meta, sb):
    return jnp.concatenate([band[SUB * sb:SUB * sb + NBAND], meta,
                            jnp.zeros((TK - NBAND - N_META, HD), band.dtype)], axis=0)


def _row_sums(x):
    ones = jnp.ones((x.shape[1], 128), BF16)
    hi = x.astype(BF16)
    lo = (x - hi.astype(F32)).astype(BF16)
    return _dot(hi, ones) + _dot(lo, ones)


def _rms_stack(q):
    return lax.rsqrt(_row_sums(q * q)[:, :HD] * (1.0 / HD) + EPS)


def _fill_bias(bias_scr, n):
    @pl.when(n <= 2)
    def _():
        for sb in range(2):
            valid, dist = _tile_bias(n, sb)
            for kvh in range(2):
                slope_col = _group_col([SLOPES[GRP * kvh + g] for g in range(GRP)])
                bias_scr[2 * sb + kvh] = jnp.where(valid, -slope_col * dist, NEG)


def _tile_vals(band, meta, sb):
    return jnp.concatenate([_tile_keys(band, meta, sb), jnp.ones((TK, 3 * HD), BF16)], axis=1)


def _tile_softmax(qn16, k16, vx16, bias, sink_col):
    s = _dot(qn16, k16, NT) * (HD ** -0.5) + bias
    mx = jnp.maximum(jnp.max(s.astype(BF16), axis=-1, keepdims=True).astype(F32), sink_col)
    e = jnp.exp(s - mx)
    es = jnp.exp(sink_col - mx)
    ox = _dot(e.astype(BF16), vx16)
    return e, 1.0 / (ox[:, 2 * HD:] + es), es, ox[:, :HD]


def _kv_heads(kvb, kvm, kw_):
    out = []
    for kvh in range(2):
        kb, km = kvb[:, HD * kvh:HD * kvh + HD], kvm[:, HD * kvh:HD * kvh + HD]
        out.append(((kb * _rms(kb) * kw_).astype(BF16), (km * _rms(km) * kw_).astype(BF16),
                    kvb[:, BLK + HD * kvh:BLK + HD * kvh + HD].astype(BF16),
                    kvm[:, BLK + HD * kvh:BLK + HD * kvh + HD].astype(BF16)))
    return out


def _kv_specs(nblk, clamp):
    cur = (lambda n: (jnp.minimum(n, nblk - 1), 8)) if clamp else (lambda n: (n, 8))
    return [pl.BlockSpec((BLK, KVW), cur),
            pl.BlockSpec((BLK, KVW), lambda n: (jnp.maximum(n - 1, 0), 8)),
            pl.BlockSpec((N_META, KVW), lambda n: (PAD // N_META, 8))]


def _sink_cols(sinks):
    return jnp.repeat(sinks.reshape(2, GRP), SUB, axis=1).reshape(2, TR, 1)


SINK_SPEC = pl.BlockSpec((2, TR, 1), lambda n: (0, 0, 0))


def attn_fwd(u, qw, kw, sinks):
    rows = u.shape[0]
    nblk = rows // BLK

    def body(q_ref, g_ref, kvc_ref, kvp_ref, kvm_ref, qw_ref, kw_ref, sc_ref, og_ref, bias_scr):
        _fill_bias(bias_scr, pl.program_id(0))
        qw_ = qw_ref[...]
        kv = _kv_heads(jnp.concatenate([kvp_ref[...], kvc_ref[...]], axis=0), kvm_ref[...], kw_ref[...])
        for sb in range(2):
            parts = []
            for kvh in range(2):
                knb, knm, vb, vm = kv[kvh]
                q = _stack_heads(q_ref, sb, kvh)
                qn16 = (q * _rms_stack(q) * qw_).astype(BF16)
                _, inv, _, o = _tile_softmax(qn16, _tile_keys(knb, knm, sb), _tile_vals(vb, vm, sb),
                                             bias_scr[2 * sb + kvh], sc_ref[kvh])
                parts.append(o * inv[:, :HD])
            rows = slice(SUB * sb, SUB * sb + SUB)
            og_ref[rows, :] = (_unstack_heads(parts) * _silu(g_ref[rows, :])).astype(BF16)

    small = lambda w: pl.BlockSpec((1, w), lambda n: (0, 0))
    return pl.pallas_call(
        body, grid=(nblk,),
        in_specs=[pl.BlockSpec((BLK, 1024), lambda n: (n, 0)), pl.BlockSpec((BLK, 1024), lambda n: (n, 1))]
        + _kv_specs(nblk, False) + [small(HD), small(HD), SINK_SPEC],
        out_specs=pl.BlockSpec((BLK, 1024), lambda n: (n, 0)),
        out_shape=SDS((rows, 1024), BF16), scratch_shapes=[pltpu.VMEM((4, TR, TK), F32)],
        compiler_params=_cp("arbitrary"), name="attn_fwd")(u, u, u, u, u, qw, kw, _sink_cols(sinks))


def attn_bwd(u, qw, kw, sinks, dog):
    rows = u.shape[0]
    nblk = rows // BLK

    def knorm_bwd(k, dkn, kw_):
        r = _rms(k)
        y = k * r
        gy = dkn * kw_
        return r * (gy - y * jnp.mean(y * gy, axis=-1, keepdims=True)), jnp.sum(dkn * y, axis=0, keepdims=True)

    def body(q_ref, g_ref, dog_ref, kvc_ref, kvp_ref, kvm_ref, qw_ref, kw_ref, sc_ref,
             dq_ref, dg_ref, dkv_ref, dkvm_ref, dqw_ref, dkw_ref, dsk_ref, carry, prevp, curp, metap, bias_scr):
        n = pl.program_id(0)
        qw_, kw_ = qw_ref[...], kw_ref[...]
        _fill_bias(bias_scr, n)

        @pl.when(n == 0)
        def _():
            carry[...] = jnp.zeros_like(carry)
            metap[...] = jnp.zeros_like(metap)
            dqw_ref[...] = jnp.zeros_like(dqw_ref)
            dkw_ref[...] = jnp.zeros_like(dkw_ref)
            dsk_ref[...] = jnp.zeros_like(dsk_ref)

        @pl.when(n == nblk)
        def _():
            prevp[...] = jnp.zeros_like(prevp)
            curp[...] = jnp.zeros_like(curp)

        @pl.when(n < nblk)
        def _():
            kv = _kv_heads(jnp.concatenate([kvp_ref[...], kvc_ref[...]], axis=0), kvm_ref[...], kw_)
            lane = lax.broadcasted_iota(jnp.int32, (1, HEADS), 1)
            dqw = jnp.zeros((1, HD), F32)
            dsk = jnp.zeros((1, HEADS), F32)
            band_parts = [jnp.zeros((2 * BLK, HD), F32) for _ in range(4)]
            meta_parts = [jnp.zeros((N_META, HD), F32) for _ in range(4)]

            def widen(x, sb):
                z = jnp.zeros((2 * BLK - NBAND, HD), F32)
                return jnp.concatenate([x, z] if sb == 0 else [z, x], axis=0)

            for sb in range(2):
                rows = slice(SUB * sb, SUB * sb + SUB)
                dq_parts, dg_parts = [], []
                for kvh in range(2):
                    knb, knm, vb, vm = kv[kvh]
                    k16, v16 = _tile_keys(knb, knm, sb), _tile_keys(vb, vm, sb)
                    q = _stack_heads(q_ref, sb, kvh)
                    r = _rms_stack(q)
                    y = q * r
                    qn16 = (y * qw_).astype(BF16)
                    e, inv, es, o = _tile_softmax(qn16, k16, _tile_vals(vb, vm, sb), bias_scr[2 * sb + kvh],
                                                  sc_ref[kvh])
                    p = e * jnp.concatenate([inv, inv], axis=1)
                    p16 = p.astype(BF16)
                    o = o * inv[:, :HD]
                    gate = _stack_heads(g_ref, sb, kvh)
                    dog_ = _stack_heads(dog_ref, sb, kvh)
                    dg_parts.append(dog_ * o * _dsilu(gate))
                    do_ = dog_ * _silu(gate)
                    do16 = do_.astype(BF16)
                    dp = _dot(do16, v16, NT)
                    delta = _row_sums(do_ * o)
                    ds16 = (p * (dp - jnp.concatenate([delta, delta], axis=1))).astype(BF16)
                    dsink = -(es * inv) * delta
                    for g in range(GRP):
                        dsk += jnp.where(lane == GRP * kvh + g,
                                         jnp.sum(dsink[SUB * g:SUB * g + SUB, :HEADS], axis=0, keepdims=True), 0.0)
                    dqn = _dot(ds16, k16) * (HD ** -0.5)
                    dk = (_dot((y * qw_).T.astype(BF16), ds16) * (HD ** -0.5)).T
                    dv = _dot(do_.T.astype(BF16), p16).T
                    band_parts[kvh] += widen(dk[:NBAND], sb)
                    band_parts[2 + kvh] += widen(dv[:NBAND], sb)
                    meta_parts[kvh] += dk[NBAND:NBAND + N_META]
                    meta_parts[2 + kvh] += dv[NBAND:NBAND + N_META]
                    gy = dqn * qw_
                    dq_parts.append(r * (gy - y * (_row_sums(y * gy)[:, :HD] * (1.0 / HD))))
                    dqw += jnp.sum(dqn * y, axis=0, keepdims=True)
                dq_ref[rows, :] = _unstack_heads(dq_parts).astype(BF16)
                dg_ref[rows, :] = _unstack_heads(dg_parts).astype(BF16)
            band = jnp.concatenate(band_parts, axis=1)
            prevp[...] = band[:BLK]
            curp[...] = band[BLK:]
            metap[...] += jnp.concatenate(meta_parts, axis=1)
            dqw_ref[...] += dqw
            dsk_ref[...] += dsk

        tot = carry[...] + prevp[...]
        kprev = kvp_ref[...]
        dk0, w0 = knorm_bwd(kprev[:, 0:HD], tot[:, 0:HD], kw_)
        dk1, w1 = knorm_bwd(kprev[:, HD:2 * HD], tot[:, HD:2 * HD], kw_)
        dkv_ref[...] = jnp.concatenate([dk0, dk1, tot[:, 2 * HD:]], axis=1)
        dkw_ref[...] += w0 + w1
        carry[...] = curp[...]

        @pl.when(n == nblk)
        def _():
            mt = metap[...]
            km = kvm_ref[...]
            m0, v0 = knorm_bwd(km[:, 0:HD], mt[:, 0:HD], kw_)
            m1, v1 = knorm_bwd(km[:, HD:2 * HD], mt[:, HD:2 * HD], kw_)
            dkvm_ref[...] = jnp.concatenate([m0, m1, mt[:, 2 * HD:]], axis=1)
            dkw_ref[...] += v0 + v1

    small = lambda w: pl.BlockSpec((1, w), lambda n: (0, 0))
    cl = lambda n: jnp.minimum(n, nblk - 1)
    return pl.pallas_call(
        body, grid=(nblk + 1,),
        in_specs=[pl.BlockSpec((BLK, 1024), lambda n: (cl(n), 0)), pl.BlockSpec((BLK, 1024), lambda n: (cl(n), 1)),
                  pl.BlockSpec((BLK, 1024), lambda n: (cl(n), 0))]
        + _kv_specs(nblk, True) + [small(HD), small(HD), SINK_SPEC],
        out_specs=[pl.BlockSpec((BLK, 1024), lambda n: (cl(n), 0)), pl.BlockSpec((BLK, 1024), lambda n: (cl(n), 0)),
                   pl.BlockSpec((BLK, KVW), lambda n: (jnp.maximum(n - 1, 0), 0)),
                   pl.BlockSpec((N_META, KVW), lambda n: (0, 0)), small(HD), small(HD), small(HEADS)],
        out_shape=[SDS((rows, 1024), BF16), SDS((rows, 1024), BF16), SDS((rows, KVW), F32), SDS((N_META, KVW), F32),
                   SDS((1, HD), F32), SDS((1, HD), F32), SDS((1, HEADS), F32)],
        scratch_shapes=[pltpu.VMEM((BLK, KVW), F32), pltpu.VMEM((BLK, KVW), F32), pltpu.VMEM((BLK, KVW), F32),
                        pltpu.VMEM((N_META, KVW), F32), pltpu.VMEM((4, TR, TK), F32)],
        compiler_params=_cp("arbitrary"), name="attn_bwd")(u, u, dog, u, u, u, qw, kw, _sink_cols(sinks))


HB = 16


def _bdot(a, b, kind, split=False):
    dims = {"nn": ((2,), (1,)), "nt": ((2,), (2,)), "tn": ((1,), (1,))}[kind]
    dg = lambda p, q: lax.dot_general(p, q, (dims, ((0,), (0,))), preferred_element_type=F32)
    if not split:
        return dg(a, b)
    ah, bh = a.astype(BF16), b.astype(BF16)
    al, bl = (a - ah.astype(F32)).astype(BF16), (b - bh.astype(F32)).astype(BF16)
    return (dg(ah, bl) + dg(al, bh)) + dg(ah, bh)


def _head_cols(hv, beta, gc, gct, lane):
    sel = lane == hv
    return _pick(beta, sel), _pick(gc, sel), gct[pl.ds(hv, 1), :]


def _conv_group(xc_ref, xp_ref, cw_ref, off, first):
    xp = jnp.where(first, 0.0, xp_ref[:, pl.ds(off, DK)])
    xx = jnp.concatenate([xp, xc_ref[:, pl.ds(off, DK)]], axis=0)
    y = cw_ref[0:1, pl.ds(off, DK)] * xx[5:5 + CH]
    for j in range(1, 4):
        y += cw_ref[j:j + 1, pl.ds(off, DK)] * xx[5 + j:5 + j + CH]
    return xx, y


def _gates(ba, al, dtb, c):
    row = c * CH + lax.broadcasted_iota(jnp.int32, (CH, DN_H), 0)
    real = row >= PAD
    xa = ba[:, DN_H:2 * DN_H] + dtb
    beta = jnp.where(real, jax.nn.sigmoid(ba[:, 0:DN_H]), 0.0)
    g = jnp.where(real, -jnp.exp(al) * jax.nn.softplus(xa), 0.0)
    return real, xa, beta, g


def _pick(x, sel):
    return jnp.sum(jnp.where(sel, x, 0.0), axis=1, keepdims=True)


def _chunk_specs(width_blocks):
    return [pl.BlockSpec((CH, 4096), lambda c: (c, 0)),
            pl.BlockSpec((8, 4096), lambda c: (jnp.maximum(8 * c - 1, 0), 0)),
            pl.BlockSpec((CH, DK), lambda c: (c, 48))]


def _tri_inv(m, ii, jj):
    eye = (ii == jj).astype(F32)
    mb = jnp.where((ii >> 3) == (jj >> 3), m, 0.0)
    m2 = _bdot(mb, mb, "nn", True)
    m4 = _bdot(m2, m2, "nn", True)
    x = _bdot(_bdot(eye - mb, eye + m2, "nn", True), eye + m4, "nn", True)
    for sh in (3, 4, 5):
        lb = jnp.where(((ii >> (sh + 1)) == (jj >> (sh + 1))) & ((ii >> sh) != (jj >> sh)), m, 0.0)
        x = x - _bdot(_bdot(x, lb, "nn", True), x, "nn", True)
    return x


def dn_prep(udn, conv_w, a_log, dt_bias):
    rows = udn.shape[0]
    nch = rows // CH

    def body(xc_ref, xp_ref, ba_ref, cw_ref, al_ref, dtb_ref,
             qn_ref, kn_ref, sv_ref, gc_ref, beta_ref, u_ref, w_ref, qe_ref, ks_ref, p_ref, at_ref, pt_ref,
             qet_ref, wt_ref, kst_ref, gct):
        c = pl.program_id(0)
        first = c == 0
        _, _, beta, g = _gates(ba_ref[...], al_ref[...], dtb_ref[...], c)
        ii = lax.broadcasted_iota(jnp.int32, (CH, CH), 0)
        jj = lax.broadcasted_iota(jnp.int32, (CH, CH), 1)
        gc = _dot((ii >= jj).astype(F32), g, precision=HI)
        gc_ref[...] = gc
        beta_ref[...] = beta
        gct[...] = gc.T

        def qk_body(kh, carry):
            off = pl.multiple_of(kh * DK, DK)
            _, yq = _conv_group(xc_ref, xp_ref, cw_ref, off, first)
            sq = _silu(yq)
            qn_ref[:, pl.ds(off, DK)] = sq * lax.rsqrt(jnp.sum(sq * sq, axis=-1, keepdims=True) + EPS) * (DK ** -0.5)
            _, yk = _conv_group(xc_ref, xp_ref, cw_ref, pl.multiple_of(1024 + kh * DK, DK), first)
            sk = _silu(yk)
            kn_ref[:, pl.ds(off, DK)] = sk * lax.rsqrt(jnp.sum(sk * sk, axis=-1, keepdims=True) + EPS)
            return carry

        lax.fori_loop(0, DN_KH, qk_body, 0)
        lane = lax.broadcasted_iota(jnp.int32, (CH, DN_H), 1)
        zpad = jnp.zeros((CH, DK - CH), F32)

        def v_group(grp, carry):
            offs, ks_, qs_, vs_, cols = [], [], [], [], []
            for i in range(HB):
                hv = grp * HB + i
                offs.append(pl.multiple_of(hv * DK, DK))
                koff = pl.multiple_of((grp * (HB // 2) + i // 2) * DK, DK)
                _, yv = _conv_group(xc_ref, xp_ref, cw_ref, pl.multiple_of(2048 + hv * DK, DK), first)
                vs_.append(_silu(yv))
                sv_ref[:, pl.ds(offs[i], DK)] = vs_[i]
                ks_.append(kn_ref[:, pl.ds(koff, DK)])
                qs_.append(qn_ref[:, pl.ds(koff, DK)])
                cols.append(_head_cols(hv, beta, gc, gct, lane))
            k, q, v = jnp.stack(ks_), jnp.stack(qs_), jnp.stack(vs_)
            beta_c, gc_c, gc_r = (jnp.stack([c_[j] for c_ in cols]) for j in range(3))
            dec = jnp.exp(jnp.where(ii >= jj, gc_c - gc_r, NEG))
            eg = jnp.exp(gc_c)
            kb = k * beta_c
            k16 = k.astype(BF16)
            m = jnp.where(ii > jj, _bdot(kb.astype(BF16), k16, "nt") * dec, 0.0)
            a = _tri_inv(m, ii, jj)
            uw = _bdot(a, jnp.concatenate([v * beta_c, kb * eg], axis=2), "nn", True)
            p = _bdot(q.astype(BF16), k16, "nt") * dec
            qe = q * eg
            ksx = k * jnp.exp(gc_c[:, CH - 1:CH, :] - gc_c)
            tslot = lambda x: jnp.concatenate([x.T, jnp.zeros((DK, DK - CH), F32)], axis=1).astype(BF16)
            for i in range(HB):
                sl = pl.ds(offs[i], DK)
                u_ref[:, sl] = uw[i, :, :DK]
                w_ref[:, sl] = uw[i, :, DK:]
                qe_ref[:, sl] = qe[i].astype(BF16)
                ks_ref[:, sl] = ksx[i].astype(BF16)
                p_ref[:, sl] = jnp.concatenate([p[i], zpad], axis=1).astype(BF16)
                at_ref[:, sl] = jnp.concatenate([a[i].T, zpad], axis=1)
                pt_ref[:, sl] = jnp.concatenate([p[i].T, zpad], axis=1).astype(BF16)
                qet_ref[:, sl] = tslot(qe[i])
                wt_ref[:, sl] = tslot(uw[i, :, DK:])
                kst_ref[:, sl] = tslot(ksx[i])
            return carry

        lax.fori_loop(0, DN_H // HB, v_group, 0)

    full = lambda shape: pl.BlockSpec(shape, lambda c: (0, 0))
    blk = lambda w: pl.BlockSpec((CH, w), lambda c: (c, 0))
    return pl.pallas_call(
        body, grid=(nch,),
        in_specs=_chunk_specs(0) + [full((4, 4096)), full((1, DN_H)), full((1, DN_H))],
        out_specs=[blk(1024), blk(1024), blk(2048), blk(DN_H), blk(DN_H), blk(2048), blk(2048), blk(2048), blk(2048),
                   blk(2048), blk(2048), blk(2048)] + [pl.BlockSpec((DK, 2048), lambda c: (c, 0))] * 3,
        out_shape=[SDS((rows, 1024), F32), SDS((rows, 1024), F32), SDS((rows, 2048), F32), SDS((rows, DN_H), F32),
                   SDS((rows, DN_H), F32), SDS((rows, 2048), F32), SDS((rows, 2048), F32), SDS((rows, 2048), BF16),
                   SDS((rows, 2048), BF16), SDS((rows, 2048), BF16), SDS((rows, 2048), F32),
                   SDS((rows, 2048), BF16)] + [SDS((2 * rows, 2048), BF16)] * 3,
        scratch_shapes=[pltpu.VMEM((DN_H, CH), F32)],
        compiler_params=_cp("parallel"), name="dn_prep")(udn, udn, udn, conv_w, a_log, dt_bias)


def dn_scan(u, w, qe, kst, p, gc):
    rows = u.shape[0]
    nch = rows // CH

    def body(u_ref, w_ref, qe_ref, kst_ref, p_ref, gc_ref, o_ref, vn_ref, st_ref, s_scr):
        @pl.when(pl.program_id(0) == 0)
        def _():
            s_scr[...] = jnp.zeros_like(s_scr)

        gl_row = gc_ref[CH - 1:CH, :]
        lane = lax.broadcasted_iota(jnp.int32, (1, DN_H), 1)

        def group(grp, carry):
            base = grp * HB
            sls = [pl.ds(pl.multiple_of((base + i) * DK, DK), DK) for i in range(HB)]
            heads = lambda ref: jnp.stack([ref[:, sl] for sl in sls])
            s = s_scr[pl.ds(base, HB)]
            st_ref[0, pl.ds(base, HB)] = s
            s16 = s.astype(BF16)
            vn = heads(u_ref) - _bdot(heads(w_ref).astype(BF16), s16, "nn")
            vn16 = vn.astype(BF16)
            o = _bdot(heads(qe_ref), s16, "nn") + _bdot(heads(p_ref)[:, :, 0:CH], vn16, "nn")
            egl = jnp.exp(jnp.stack([_pick(gl_row, lane == base + i) for i in range(HB)]))
            s_scr[pl.ds(base, HB)] = s * egl + _bdot(heads(kst_ref)[:, :, 0:CH], vn16, "nn")
            for i in range(HB):
                vn_ref[:, sls[i]] = vn16[i]
                o_ref[:, sls[i]] = o[i]
            return carry

        lax.fori_loop(0, DN_H // HB, group, 0)

    blk = lambda wd: pl.BlockSpec((CH, wd), lambda c: (c, 0))
    return pl.pallas_call(
        body, grid=(nch,),
        in_specs=[blk(2048)] * 3 + [pl.BlockSpec((DK, 2048), lambda c: (c, 0)), blk(2048), blk(DN_H)],
        out_specs=[blk(2048), blk(2048), pl.BlockSpec((1, DN_H, DK, DK), lambda c: (c, 0, 0, 0))],
        out_shape=[SDS((rows, 2048), F32), SDS((rows, 2048), BF16), SDS((nch, DN_H, DK, DK), F32)],
        scratch_shapes=[pltpu.VMEM((DN_H, DK, DK), F32)],
        compiler_params=_cp("arbitrary"), name="dn_scan")(u, w, qe, kst, p, gc)


def dn_out_fwd(o, udn, ow, wout, h1, tgt):
    rows = o.shape[0]
    tm = _row_tile(rows)
    nt = rows // tm

    def body(o_ref, z_ref, ow_ref, w_ref, h_ref, t_ref, dh_ref, on_ref, ls_ref):
        for hv in range(DN_H):
            sl = slice(hv * DK, hv * DK + DK)
            oh = o_ref[:, sl]
            on_ref[:, sl] = (oh * _rms(oh) * ow_ref[...] * _silu(z_ref[:, sl])).astype(BF16)
        h2 = h_ref[...] + _dot(on_ref[...], w_ref[...])
        row = pl.program_id(0) * tm + lax.broadcasted_iota(jnp.int32, (tm, 1), 0)
        err = jnp.where(row >= BLK, h2 - t_ref[...], 0.0)
        dh_ref[...] = err * (1.0 / D_MODEL)
        ls_ref[0] = jnp.sum(err * err, axis=0, keepdims=True)

    return pl.pallas_call(
        body, grid=(nt,),
        in_specs=[pl.BlockSpec((tm, 2048), lambda i: (i, 0)), pl.BlockSpec((tm, 2048), lambda i: (i, 2)),
                  pl.BlockSpec((1, DK), lambda i: (0, 0)), pl.BlockSpec((2048, D_MODEL), lambda i: (0, 0)),
                  pl.BlockSpec((tm, D_MODEL), lambda i: (i, 0)), pl.BlockSpec((tm, D_MODEL), lambda i: (i, 0))],
        out_specs=[pl.BlockSpec((tm, D_MODEL), lambda i: (i, 0)), pl.BlockSpec((tm, 2048), lambda i: (i, 0)),
                   pl.BlockSpec((1, 1, D_MODEL), lambda i: (i, 0, 0))],
        out_shape=[SDS((rows, D_MODEL), F32), SDS((rows, 2048), BF16), SDS((nt, 1, D_MODEL), F32)],
        compiler_params=_cp("parallel"), name="dn_out_fwd")(o, udn, ow, wout, h1, tgt)


def dn_out_bwd(dh2, wout, o, udn, ow):
    rows = o.shape[0]
    tm = _row_tile(rows)
    nt = rows // tm

    def body(dh_ref, w_ref, o_ref, z_ref, ow_ref, do_ref, dz_ref, dow_ref):
        don = _dot(dh_ref[...].astype(BF16), w_ref[...], NT)
        ow_ = ow_ref[...]
        dow = jnp.zeros((1, DK), F32)
        for hv in range(DN_H):
            sl = slice(hv * DK, hv * DK + DK)
            oh = o_ref[:, sl]
            r = _rms(oh)
            y = oh * r
            z = z_ref[:, sl]
            dn = don[:, sl] * _silu(z)
            dz_ref[:, sl] = (don[:, sl] * (y * ow_) * _dsilu(z)).astype(BF16)
            dy = dn * ow_
            do_ref[:, sl] = r * (dy - y * jnp.mean(y * dy, axis=-1, keepdims=True))
            dow += jnp.sum(dn * y, axis=0, keepdims=True)
        dow_ref[0] = dow

    return pl.pallas_call(
        body, grid=(nt,),
        in_specs=[pl.BlockSpec((tm, D_MODEL), lambda i: (i, 0)), pl.BlockSpec((2048, D_MODEL), lambda i: (0, 0)),
                  pl.BlockSpec((tm, 2048), lambda i: (i, 0)), pl.BlockSpec((tm, 2048), lambda i: (i, 2)),
                  pl.BlockSpec((1, DK), lambda i: (0, 0))],
        out_specs=[pl.BlockSpec((tm, 2048), lambda i: (i, 0)), pl.BlockSpec((tm, 2048), lambda i: (i, 0)),
                   pl.BlockSpec((1, 1, DK), lambda i: (i, 0, 0))],
        out_shape=[SDS((rows, 2048), F32), SDS((rows, 2048), BF16), SDS((nt, 1, DK), F32)],
        compiler_params=_cp("parallel"), name="dn_out_bwd")(dh2, wout, o, udn, ow)


def dn_scan_bwd(do, qn, kn, sv, gc, beta, at, pt, u, w, vn, qet, wt, ks, st):
    rows = do.shape[0]
    nch = rows // CH

    def body(do_ref, q_ref, k_ref, v_ref, gc_ref, beta_ref, at_ref, pt_ref, u_ref, w_ref, vn_ref, qet_ref, wt_ref,
             ks_ref, st_ref, dq_ref, dk_ref, dv_ref, dbeta_ref, dg_ref, ds_scr, gct):
        @pl.when(pl.program_id(0) == 0)
        def _():
            ds_scr[...] = jnp.zeros_like(ds_scr)

        gc, beta = gc_ref[...], beta_ref[...]
        gct[...] = gc.T
        ii = lax.broadcasted_iota(jnp.int32, (CH, CH), 0)
        jj = lax.broadcasted_iota(jnp.int32, (CH, CH), 1)
        lane = lax.broadcasted_iota(jnp.int32, (CH, DN_H), 1)
        last = lax.broadcasted_iota(jnp.int32, (CH, 1), 0) == CH - 1

        def group(grp, carry):
            dbeta_acc, dgc_acc = carry
            base = grp * HB
            sls = [pl.ds(pl.multiple_of((base + i) * DK, DK), DK) for i in range(HB)]
            ksls = [pl.ds(pl.multiple_of((grp * (HB // 2) + j) * DK, DK), DK) for j in range(HB // 2)]
            heads = lambda ref: jnp.stack([ref[:, sl] for sl in sls])
            kheads = lambda ref: jnp.stack([ref[:, ksls[i // 2]] for i in range(HB)])
            cols = [_head_cols(base + i, beta, gc, gct, lane) for i in range(HB)]
            beta_c, gc_c, gc_r = (jnp.stack([c_[j] for c_ in cols]) for j in range(3))
            k, q, v = kheads(k_ref), kheads(q_ref), heads(v_ref)
            dec = jnp.exp(jnp.where(ii >= jj, gc_c - gc_r, NEG))
            eg = jnp.exp(gc_c)
            gl = gc_c[:, CH - 1:CH, :]
            e2 = jnp.exp(gl - gc_c)
            egl = jnp.exp(gl)
            k16, q16 = k.astype(BF16), q.astype(BF16)
            do16 = heads(do_ref).astype(BF16)
            s = st_ref[0, pl.ds(base, HB)]
            s16 = s.astype(BF16)
            dso = ds_scr[pl.ds(base, HB)]
            dso16 = dso.astype(BF16)
            wf, uf, vn16 = heads(w_ref), heads(u_ref), heads(vn_ref)
            kb = k * beta_c
            kb16 = kb.astype(BF16)
            pm = _bdot(q16, k16, "nt") * dec
            m = jnp.where(ii > jj, _bdot(kb16, k16, "nt") * dec, 0.0)
            dvn = _bdot(heads(pt_ref)[:, :, 0:CH], do16, "nn") + _bdot(heads(ks_ref), dso16, "nn")
            dvn16 = dvn.astype(BF16)
            ds_scr[pl.ds(base, HB)] = (egl * dso + _bdot(heads(qet_ref)[:, :, 0:CH], do16, "nn")
                                       - _bdot(heads(wt_ref)[:, :, 0:CH], dvn16, "nn"))
            dpm = jnp.where(ii >= jj, _bdot(do16, vn16, "nt"), 0.0)
            dqk16 = (dpm * dec).astype(BF16)
            dqe = _bdot(do16, s16, "nt")
            dq = eg * dqe + _bdot(dqk16, k16, "nn")
            dks = _bdot(vn16, dso16, "nt")
            dw = -_bdot(dvn16, s16, "nt")
            dbvk = _bdot(heads(at_ref)[:, :, 0:CH], jnp.concatenate([dvn, dw], axis=2), "nn", True)
            dbv, dbk = dbvk[:, :, :DK], dbvk[:, :, DK:]
            dm = jnp.where(ii > jj, -_bdot(dbvk, jnp.concatenate([uf, wf], axis=2), "nt", True), 0.0)
            g16 = (dm * dec).astype(BF16)
            dkb = _bdot(g16, k16, "nn")
            dk = (_bdot(dqk16, q16, "tn") + e2 * dks + _bdot(g16, kb16, "tn") + beta_c * (eg * dbk + dkb))
            e = dpm * pm + dm * m
            rsum = lambda x: jnp.sum(x, axis=2, keepdims=True)
            r_bk, r_qe, r_beta, r_ks = rsum(dbk * k), rsum(q * dqe), rsum(dbv * v + dkb * k), rsum(dks * k)
            t = r_ks * e2
            dgl = jnp.sum(t, axis=1, keepdims=True) + egl * rsum(jnp.sum(dso * s, axis=1, keepdims=True))
            deg = r_qe + beta_c * r_bk
            dgc = rsum(e) - t + deg * eg + jnp.where(last, dgl, 0.0)
            dgrow = -jnp.sum(e, axis=1, keepdims=True)
            dv = beta_c * dbv
            dbeta = r_beta + eg * r_bk
            for i in range(HB):
                dv_ref[:, sls[i]] = dv[i]
                sel = lane == base + i
                dbeta_acc = jnp.where(sel, dbeta[i], dbeta_acc)
                dgc_acc = jnp.where(sel, dgc[i], dgc_acc)
                gct[pl.ds(base + i, 1), :] = dgrow[i]
            for j in range(HB // 2):
                dq_ref[:, ksls[j]] = dq[2 * j] + dq[2 * j + 1]
                dk_ref[:, ksls[j]] = dk[2 * j] + dk[2 * j + 1]
            return dbeta_acc, dgc_acc

        zero = jnp.zeros((CH, DN_H), F32)
        dbeta_acc, dgc_acc = lax.fori_loop(0, DN_H // HB, group, (zero, zero))
        dbeta_ref[...] = dbeta_acc
        dg_ref[...] = _dot((ii <= jj).astype(F32), dgc_acc + gct[...].T, precision=HI)

    rev = lambda wd: pl.BlockSpec((CH, wd), lambda i: (nch - 1 - i, 0))
    rev_t = pl.BlockSpec((DK, 2048), lambda i: (nch - 1 - i, 0))
    return pl.pallas_call(
        body, grid=(nch,),
        in_specs=[rev(2048), rev(1024), rev(1024), rev(2048), rev(DN_H), rev(DN_H), rev(2048), rev(2048), rev(2048),
                  rev(2048), rev(2048), rev_t, rev_t, rev(2048),
                  pl.BlockSpec((1, DN_H, DK, DK), lambda i: (nch - 1 - i, 0, 0, 0))],
        out_specs=[rev(1024), rev(1024), rev(2048), rev(DN_H), rev(DN_H)],
        out_shape=[SDS((rows, 1024), F32), SDS((rows, 1024), F32), SDS((rows, 2048), F32), SDS((rows, DN_H), F32),
                   SDS((rows, DN_H), F32)],
        scratch_shapes=[pltpu.VMEM((DN_H, DK, DK), F32), pltpu.VMEM((DN_H, CH), F32)],
        compiler_params=_cp("arbitrary"), name="dn_scan_bwd")(
            do, qn, kn, sv, gc, beta, at, pt, u, w, vn, qet, wt, ks, st)


def dn_prep_bwd(udn, conv_w, a_log, dt_bias, dqn, dkn, dv, dbeta, dg):
    rows = udn.shape[0]
    nch = rows // CH
    ext = CH + 8

    def body(xc_ref, xp_ref, ba_ref, xn_ref, dqn_n, dkn_n, dv_n, cw_ref, al_ref, dtb_ref, dqn_ref, dkn_ref, dv_ref,
             dbeta_ref, dg_ref, dx_ref, dba_ref, dcw_ref, dal_ref, ddtb_ref):
        c = pl.program_id(0)
        first = c == 0
        own = (lax.broadcasted_iota(jnp.int32, (ext, 1), 0) < CH) | (c < nch - 1)

        @pl.when(first)
        def _():
            dcw_ref[...] = jnp.zeros_like(dcw_ref)
            dal_ref[...] = jnp.zeros_like(dal_ref)
            ddtb_ref[...] = jnp.zeros_like(ddtb_ref)

        real, xa, beta, g = _gates(ba_ref[...], al_ref[...], dtb_ref[...], c)
        dgm = jnp.where(real, dg_ref[...], 0.0)
        da = dgm * (-jnp.exp(al_ref[...])) * jax.nn.sigmoid(xa)
        dal_ref[...] += jnp.sum(dgm * g, axis=0, keepdims=True)
        ddtb_ref[...] += jnp.sum(da, axis=0, keepdims=True)
        dba_ref[...] = jnp.zeros_like(dba_ref)
        dba_ref[:, 0:DN_H] = jnp.where(real, dbeta_ref[...] * beta * (1.0 - beta), 0.0)
        dba_ref[:, DN_H:2 * DN_H] = da

        def through_conv(off, g_cur, g_next, grad_fn):
            sl = pl.ds(off, DK)
            xx = jnp.concatenate([jnp.where(first, 0.0, xp_ref[:, sl]), xc_ref[:, sl], xn_ref[:, sl]], axis=0)
            y = cw_ref[0:1, sl] * xx[5:5 + ext]
            for j in range(1, 4):
                y += cw_ref[j:j + 1, sl] * xx[5 + j:5 + j + ext]
            dy = jnp.where(own, grad_fn(_silu(y), jnp.concatenate([g_cur, g_next], axis=0)) * _dsilu(y), 0.0)
            dx = cw_ref[0:1, sl] * dy[3:3 + CH]
            for j in range(1, 4):
                dx += cw_ref[j:j + 1, sl] * dy[3 - j:3 - j + CH]
            dx_ref[:, sl] = dx.astype(BF16)
            for j in range(4):
                dcw_ref[j:j + 1, sl] += jnp.sum(dy[:CH] * xx[5 + j:5 + j + CH], axis=0, keepdims=True)

        def l2_bwd(scale):
            def f(s, gin):
                r = lax.rsqrt(jnp.sum(s * s, axis=-1, keepdims=True) + EPS)
                nrm = s * r
                return (r * scale) * (gin - nrm * jnp.sum(nrm * gin, axis=-1, keepdims=True))
            return f

        def qk_body(kh, carry):
            sl = pl.ds(pl.multiple_of(kh * DK, DK), DK)
            through_conv(pl.multiple_of(kh * DK, DK), dqn_ref[:, sl], dqn_n[:, sl], l2_bwd(DK ** -0.5))
            through_conv(pl.multiple_of(1024 + kh * DK, DK), dkn_ref[:, sl], dkn_n[:, sl], l2_bwd(1.0))
            return carry

        lax.fori_loop(0, DN_KH, qk_body, 0)

        def v_body(hv, carry):
            sl = pl.ds(pl.multiple_of(hv * DK, DK), DK)
            through_conv(pl.multiple_of(2048 + hv * DK, DK), dv_ref[:, sl], dv_n[:, sl], lambda s, gin: gin)
            return carry

        lax.fori_loop(0, DN_H, v_body, 0)

    full = lambda shape: pl.BlockSpec(shape, lambda c: (0, 0))
    blk = lambda w: pl.BlockSpec((CH, w), lambda c: (c, 0))
    nxt = lambda w: pl.BlockSpec((8, w), lambda c: (jnp.minimum(8 * c + 8, rows // 8 - 1), 0))
    return pl.pallas_call(
        body, grid=(nch,),
        in_specs=_chunk_specs(0) + [nxt(4096), nxt(1024), nxt(1024), nxt(2048), full((4, 4096)), full((1, DN_H)),
                                    full((1, DN_H)), blk(1024), blk(1024), blk(2048), blk(DN_H), blk(DN_H)],
        out_specs=[blk(4096), blk(DK), full((8, 4096)), full((1, DN_H)), full((1, DN_H))],
        out_shape=[SDS((rows, 4096), BF16), SDS((rows, DK), F32), SDS((8, 4096), F32), SDS((1, DN_H), F32),
                   SDS((1, DN_H), F32)],
        compiler_params=_cp("arbitrary"), name="dn_prep_bwd")(
            udn, udn, udn, udn, dqn, dkn, dv, conv_w, a_log, dt_bias, dqn, dkn, dv, dbeta, dg)


def local_step(x, target, w):
    seq = x.shape[0]
    bf = lambda a: a.astype(BF16)
    h0 = jnp.concatenate([jnp.zeros((PAD, D_MODEL), F32), w["meta_tokens"], x], axis=0)
    tgt = jnp.concatenate([jnp.zeros((BLK, D_MODEL), F32), target], axis=0)
    win = w["attn_w_in"]
    wq, wkv, wg = win[:, :1024], win[:, 1024:1280], win[:, 1280:]
    wa_in = bf(jnp.concatenate([wq, wg, wkv], axis=1))
    wa_out = bf(w["attn_w_out"])
    wd_in = jnp.concatenate([bf(w["dn_w_in"]), jnp.zeros((D_MODEL, 96), BF16)], axis=1)
    wd_out = bf(w["dn_w_out"])
    qw, kw, sinks = w["attn_q_norm_w"], w["attn_k_norm_w"], w["attn_sinks"]
    cw, al, dtb, ow = w["dn_conv_w"], w["dn_a_log"], w["dn_dt_bias"], w["dn_o_norm_w"]

    ua, xn0 = norm_matmul(h0, w["attn_norm_w"], wa_in, 2304, "attn_in")
    og = attn_fwd(ua, qw, kw, sinks)
    h1 = matmul_residual(og, wa_out, h0, "attn_out")
    ud, xn1 = norm_matmul(h1, w["dn_norm_w"], wd_in, 896, "dn_in")
    qn, kn, sv, gc, beta, u, wy, qe, ks, p, at, pt, qet, wt, kst = dn_prep(ud, cw, al, dtb)
    o, vn, st = dn_scan(u, wy, qe, kst, p, gc)
    dh2, on, ls = dn_out_fwd(o, ud, ow, wd_out, h1, tgt)
    loss = (0.5 / D_MODEL) * jnp.sum(ls)

    do, dz, dow = dn_out_bwd(dh2, wd_out, o, ud, ow)
    g_dn_out = wgrad(on, dh2, "dn_out_wgrad")
    dqn, dkn, dv, dbeta, dg = dn_scan_bwd(do, qn, kn, sv, gc, beta, at, pt, u, wy, vn, qet, wt, ks, st)
    dxc, dba, dcw, dal, ddtb = dn_prep_bwd(ud, cw, al, dtb, dqn, dkn, dv, dbeta, dg)
    dh1, dnw1 = in_proj_bwd([dxc, dz, dba], [wd_in[:, :4096], wd_in[:, 4096:6144], wd_in[:, 6144:]],
                            h1, w["dn_norm_w"], dh2, "dn_in_bwd")
    g_dn_in = jnp.concatenate([wgrad(xn1, dxc, "dn_in_wgrad_qkv"), wgrad(xn1, dz, "dn_in_wgrad_z"),
                               wgrad(xn1, dba, "dn_in_wgrad_ba")[:, :2 * DN_H]], axis=1)

    dog = matmul_nt(dh1, wa_out, "attn_out_bwd")
    g_attn_out = wgrad(og, dh1, "attn_out_wgrad")
    dq, dgate, dkv, dkvm, dqw, dkw, dsk = attn_bwd(ua, qw, kw, sinks, dog)
    dkv = dkv.at[PAD:BLK].add(dkvm)
    dh0, dnw0 = in_proj_bwd([dq, dgate, dkv], [wa_in[:, :1024], wa_in[:, 1024:2048], wa_in[:, 2048:]],
                            h0, w["attn_norm_w"], dh1, "attn_in_bwd")
    g_attn_in = jnp.concatenate([wgrad(xn0, dq, "attn_in_wgrad_q"), wgrad(xn0, dkv, "attn_in_wgrad_kv"),
                                 wgrad(xn0, dgate, "attn_in_wgrad_g")], axis=1)
    grads = {
        "meta_tokens": dh0[PAD:BLK], "attn_norm_w": jnp.sum(dnw0, axis=0), "attn_w_in": g_attn_in,
        "attn_q_norm_w": dqw, "attn_k_norm_w": dkw, "attn_sinks": dsk, "attn_w_out": g_attn_out,
        "dn_norm_w": jnp.sum(dnw1, axis=0), "dn_w_in": g_dn_in, "dn_conv_w": dcw[:4], "dn_a_log": dal,
        "dn_dt_bias": ddtb, "dn_o_norm_w": jnp.sum(dow, axis=0), "dn_w_out": g_dn_out,
    }
    return loss, dh0[BLK:BLK + seq], grads


WEIGHTS = ["meta_tokens", "attn_norm_w", "attn_w_in", "attn_q_norm_w", "attn_k_norm_w", "attn_sinks", "attn_w_out",
           "dn_norm_w", "dn_w_in", "dn_conv_w", "dn_a_log", "dn_dt_bias", "dn_o_norm_w", "dn_w_out"]
SHARDED = {"attn_w_in": ((1024, 2304), 1), "attn_w_out": ((1024, 1024), 0), "dn_w_in": ((1024, 6176), 1),
           "dn_w_out": ((2048, 1024), 0), "dn_conv_w": ((4, 4096), 1), "meta_tokens": ((16, 1024), 1),
           "dn_norm_w": ((1, 1024), 1)}
REPLICATED = {"attn_norm_w": 1024, "attn_q_norm_w": 64, "attn_k_norm_w": 64, "attn_sinks": 16, "dn_a_log": 16,
              "dn_dt_bias": 16, "dn_o_norm_w": 128}
N_CHIPS = 4
PACK_ROWS = 2912
HALF_ROWS = PACK_ROWS // 2
SMALL_ROWS = 8


def _shard_shape(name):
    (r, c), axis = SHARDED[name]
    return (r // N_CHIPS, c) if axis == 0 else (r, c // N_CHIPS)


def _pack(parts, rows):
    flat = jnp.concatenate([p.reshape(-1) for p in parts])
    return jnp.pad(flat, (0, rows * 1024 - flat.shape[0])).reshape(rows, 1024)


def pack_shard(shards):
    return _pack([shards[n] for n in SHARDED], PACK_ROWS)


def unpack_shard(buf):
    flat, out, pos = buf.reshape(-1), {}, 0
    for n in SHARDED:
        shp = _shard_shape(n)
        size = shp[0] * shp[1]
        out[n] = flat[pos:pos + size].reshape(shp)
        pos += size
    return out


MATRICES = ("attn_w_in", "attn_w_out", "dn_w_in", "dn_w_out")


def pack_gather(shards):
    big = [shards[n].astype(BF16).reshape(-1) for n in MATRICES]
    small = jnp.concatenate([shards[n].reshape(-1) for n in SHARDED if n not in MATRICES])
    flat = jnp.concatenate(big + [lax.bitcast_convert_type(small, BF16).reshape(-1)])
    return jnp.pad(flat, (0, PACK_ROWS * 1024 - flat.shape[0])).reshape(PACK_ROWS, 1024)


def unpack_gather(buf):
    PER_F32 = 4 // jnp.dtype(buf.dtype).itemsize
    flat, out, pos = buf.reshape(-1), {}, 0
    for n in MATRICES:
        shp = _shard_shape(n)
        out[n] = flat[pos:pos + shp[0] * shp[1]].reshape(shp)
        pos += shp[0] * shp[1]
    for n in SHARDED:
        if n not in MATRICES:
            shp = _shard_shape(n)
            raw = flat[pos:pos + shp[0] * shp[1] * PER_F32]
            out[n] = lax.bitcast_convert_type(raw.reshape(-1, PER_F32) if PER_F32 > 1 else raw, F32).reshape(shp)
            pos += shp[0] * shp[1] * PER_F32
    return out


def pack_small(vals):
    return _pack([vals[n] for n in REPLICATED], SMALL_ROWS)


def unpack_small(buf):
    flat, out, pos = buf.reshape(-1), {}, 0
    for n, size in REPLICATED.items():
        out[n] = flat[pos:pos + size].reshape(1, size)
        pos += size
    return out


ANY = pl.BlockSpec(memory_space=pl.ANY)


def _place():
    return lax.axis_index("x"), lax.axis_index("y"), lax.axis_index("c")


def chips_exchange(src, gather):
    r = src.shape[-2]

    def body(s_ref, o_ref, send_sems, recv_sems, local_sem):
        x, y, c = _place()
        me = 2 * x + y
        peers = [(1 - x, y), (x, 1 - y), (1 - x, 1 - y)]
        mine = pltpu.make_async_copy(s_ref if gather else s_ref.at[me], o_ref.at[me], local_sem)
        mine.start()

        def copy(k, to_block, from_block):
            px, py = peers[k]
            return pltpu.make_async_remote_copy(
                src_ref=s_ref if gather else s_ref.at[to_block], dst_ref=o_ref.at[from_block],
                send_sem=send_sems.at[k], recv_sem=recv_sems.at[k], device_id=(px, py, c), device_id_type=MESH)

        sends = [copy(k, 2 * px + py, me) for k, (px, py) in enumerate(peers)]
        for cp in sends:
            cp.start()
        for k, (px, py) in enumerate(peers):
            copy(k, me, 2 * px + py).wait_recv()
        for cp in sends:
            cp.wait_send()
        mine.wait()

    return pl.pallas_call(
        body, in_specs=[ANY], out_specs=ANY, out_shape=SDS((N_CHIPS, r, 1024), src.dtype),
        scratch_shapes=[pltpu.SemaphoreType.DMA((3,)), pltpu.SemaphoreType.DMA((3,)), pltpu.SemaphoreType.DMA],
        name="chips_gather" if gather else "chips_exchange")(src)


def _rows_at(ref, start, size):
    return ref.at[:, pl.ds(start, size), :] if len(ref.shape) == 3 else ref.at[pl.ds(start, size), :]


def sibling_join(src, name):
    h = src.shape[-2]

    def body(s_ref, o_ref, send_sem, recv_sem, local_sem):
        x, y, c = _place()
        dst = _rows_at(o_ref, c * h, h)
        mine = pltpu.make_async_copy(s_ref, dst, local_sem)
        mine.start()
        cp = pltpu.make_async_remote_copy(src_ref=s_ref, dst_ref=dst, send_sem=send_sem, recv_sem=recv_sem,
                                          device_id=(x, y, 1 - c), device_id_type=MESH)
        cp.start()
        cp.wait()
        mine.wait()

    return pl.pallas_call(
        body, in_specs=[ANY], out_specs=ANY,
        out_shape=SDS(src.shape[:-2] + (2 * h, src.shape[-1]), src.dtype),
        scratch_shapes=[pltpu.SemaphoreType.DMA, pltpu.SemaphoreType.DMA, pltpu.SemaphoreType.DMA], name=name)(src)


def sibling_give(g_all):
    def body(s_ref, o_ref, send_sem, recv_sem):
        x, y, c = _place()
        cp = pltpu.make_async_remote_copy(
            src_ref=_rows_at(s_ref, (1 - c) * HALF_ROWS, HALF_ROWS), dst_ref=o_ref, send_sem=send_sem,
            recv_sem=recv_sem, device_id=(x, y, 1 - c), device_id_type=MESH)
        cp.start()
        cp.wait()

    return pl.pallas_call(
        body, in_specs=[ANY], out_specs=ANY, out_shape=SDS((N_CHIPS, HALF_ROWS, 1024), F32),
        scratch_shapes=[pltpu.SemaphoreType.DMA, pltpu.SemaphoreType.DMA], name="pair_exchange")(g_all)


def pair_sum(g_all, got, c):
    tm = 208
    per_half = HALF_ROWS // tm

    def body(c_ref, a_ref, b_ref, o_ref):
        o_ref[...] = a_ref[...] + b_ref[...]

    return pl.pallas_call(
        body,
        grid_spec=pltpu.PrefetchScalarGridSpec(
            num_scalar_prefetch=1, grid=(N_CHIPS, per_half),
            in_specs=[pl.BlockSpec((1, tm, 1024), lambda j, i, c_ref: (j, c_ref[0] * per_half + i, 0)),
                      pl.BlockSpec((1, tm, 1024), lambda j, i, c_ref: (j, i, 0))],
            out_specs=pl.BlockSpec((1, tm, 1024), lambda j, i, c_ref: (j, i, 0))),
        out_shape=SDS((N_CHIPS, HALF_ROWS, 1024), F32),
        compiler_params=_cp("parallel", "parallel"), name="pair_sum")(c.reshape(1).astype(jnp.int32), g_all, got)


def all_gather_small(src):
    def body(s_ref, o_ref, send_sems, recv_sems, local_sem):
        x, y, c = _place()
        flips = [(fx, fy, fc) for fx in (0, 1) for fy in (0, 1) for fc in (0, 1)][1:]
        idx = lambda px, py, pc: 4 * px + 2 * py + pc
        mine = pltpu.make_async_copy(s_ref, o_ref.at[idx(x, y, c)], local_sem)
        mine.start()

        def peer(k):
            fx, fy, fc = flips[k]
            return (1 - x if fx else x, 1 - y if fy else y, 1 - c if fc else c)

        def copy(k, block):
            return pltpu.make_async_remote_copy(
                src_ref=s_ref, dst_ref=o_ref.at[block], send_sem=send_sems.at[k], recv_sem=recv_sems.at[k],
                device_id=peer(k), device_id_type=MESH)

        sends = [copy(k, idx(x, y, c)) for k in range(7)]
        for cp in sends:
            cp.start()
        for k in range(7):
            copy(k, idx(*peer(k))).wait_recv()
        for cp in sends:
            cp.wait_send()
        mine.wait()

    return pl.pallas_call(
        body, in_specs=[ANY], out_specs=ANY, out_shape=SDS((8,) + src.shape, F32),
        scratch_shapes=[pltpu.SemaphoreType.DMA((7,)), pltpu.SemaphoreType.DMA((7,)), pltpu.SemaphoreType.DMA],
        name="all_gather_small")(src)


def sum_blocks(t, name):
    n, r, _ = t.shape
    tm = 208 if r % 208 == 0 else r

    def body(t_ref, o_ref):
        acc = t_ref[0]
        for i in range(1, n):
            acc = acc + t_ref[i]
        o_ref[...] = acc

    return pl.pallas_call(
        body, grid=(r // tm,), in_specs=[pl.BlockSpec((n, tm, 1024), lambda i: (0, i, 0))],
        out_specs=pl.BlockSpec((tm, 1024), lambda i: (i, 0)), out_shape=SDS((r, 1024), F32),
        compiler_params=_cp("parallel"), name=name)(t)


def adamw(w, g, m, v, name):
    rows = w.shape[0]
    tm = 416 if rows % 416 == 0 else rows

    def body(w_ref, g_ref, m_ref, v_ref, d_ref, nm_ref, nv_ref):
        g_ = g_ref[...]
        m_ = ADAM_B1 * m_ref[...] + (1.0 - ADAM_B1) * g_
        v_ = ADAM_B2 * v_ref[...] + (1.0 - ADAM_B2) * (g_ * g_)
        m_hat = m_ / (1.0 - ADAM_B1 ** ADAM_STEP)
        v_hat = v_ / (1.0 - ADAM_B2 ** ADAM_STEP)
        d_ref[...] = -ADAM_LR * (m_hat / (jnp.sqrt(v_hat) + ADAM_EPS) + ADAM_WD * w_ref[...])
        nm_ref[...] = m_
        nv_ref[...] = v_

    spec = pl.BlockSpec((tm, 1024), lambda i: (i, 0))
    return pl.pallas_call(
        body, grid=(rows // tm,), in_specs=[spec] * 4, out_specs=[spec] * 3,
        out_shape=[SDS((rows, 1024), F32)] * 3, compiler_params=_cp("parallel"), name=name)(w, g, m, v)


LAYERED = ("attn_w_in", "attn_w_out", "dn_w_in", "dn_conv_w", "dn_w_out")


def _two_d(name, a):
    return a[0] if name in LAYERED else a


def kernel(x, meta_tokens, attn_norm_w, attn_w_in, attn_q_norm_w, attn_k_norm_w, attn_sinks, attn_w_out, dn_norm_w, dn_w_in, dn_conv_w, dn_a_log, dn_dt_bias, dn_o_norm_w, dn_w_out, loss_target, m_meta_tokens, m_attn_norm_w, m_attn_w_in, m_attn_q_norm_w, m_attn_k_norm_w, m_attn_sinks, m_attn_w_out, m_dn_norm_w, m_dn_w_in, m_dn_conv_w, m_dn_a_log, m_dn_dt_bias, m_dn_o_norm_w, m_dn_w_out, v_meta_tokens, v_attn_norm_w, v_attn_w_in, v_attn_q_norm_w, v_attn_k_norm_w, v_attn_sinks, v_attn_w_out, v_dn_norm_w, v_dn_w_in, v_dn_conv_w, v_dn_a_log, v_dn_dt_bias, v_dn_o_norm_w, v_dn_w_out):
    given = dict(zip(WEIGHTS, (meta_tokens, attn_norm_w, attn_w_in, attn_q_norm_w, attn_k_norm_w, attn_sinks,
                               attn_w_out, dn_norm_w, dn_w_in, dn_conv_w, dn_a_log, dn_dt_bias, dn_o_norm_w, dn_w_out)))
    mom1 = dict(zip(WEIGHTS, (m_meta_tokens, m_attn_norm_w, m_attn_w_in, m_attn_q_norm_w, m_attn_k_norm_w,
                              m_attn_sinks, m_attn_w_out, m_dn_norm_w, m_dn_w_in, m_dn_conv_w, m_dn_a_log,
                              m_dn_dt_bias, m_dn_o_norm_w, m_dn_w_out)))
    mom2 = dict(zip(WEIGHTS, (v_meta_tokens, v_attn_norm_w, v_attn_w_in, v_attn_q_norm_w, v_attn_k_norm_w,
                              v_attn_sinks, v_attn_w_out, v_dn_norm_w, v_dn_w_in, v_dn_conv_w, v_dn_a_log,
                              v_dn_dt_bias, v_dn_o_norm_w, v_dn_w_out)))
    two_d = lambda d: {n: _two_d(n, a) for n, a in d.items()}
    given, mom1, mom2 = two_d(given), two_d(mom1), two_d(mom2)
    c = lax.axis_index("c")

    w_shard = pack_shard(given)
    mine = chips_exchange(lax.dynamic_slice_in_dim(pack_gather(given), c * HALF_ROWS, HALF_ROWS, axis=0), True)
    gathered = sibling_join(mine, "gather_swap")
    per_chip = [unpack_gather(gathered[j]) for j in range(N_CHIPS)]
    full = {n: jnp.concatenate([pc[n] for pc in per_chip], axis=SHARDED[n][1]) for n in SHARDED}
    full.update({n: given[n] for n in REPLICATED})

    loss, dx, grads = local_step(x[0], loss_target[0], full)

    split = lambda n: jnp.split(grads[n], N_CHIPS, axis=SHARDED[n][1])
    g_all = jnp.stack([pack_shard({n: split(n)[j] for n in SHARDED}) for j in range(N_CHIPS)])
    pair = pair_sum(g_all, sibling_give(g_all), c)
    half = sum_blocks(chips_exchange(pair, False), "chip_sum")
    g_shard = sibling_join(half, "half_exchange")

    g_small = sum_blocks(all_gather_small(pack_small(grads)), "small_sum")

    big = adamw(w_shard, g_shard, pack_shard(mom1), pack_shard(mom2), "adamw")
    small = adamw(pack_small(given), g_small, pack_small(mom1), pack_small(mom2), "adamw_small")

    def unpack(big_buf, small_buf):
        out = unpack_shard(big_buf)
        out.update(unpack_small(small_buf))
        return [out[n][None] if n in LAYERED else out[n] for n in WEIGHTS]

    loss = lax.psum(loss, ("x", "y", "c"))
    return (loss, dx[None], *unpack(g_shard, g_small), *[o for b, s in zip(big, small) for o in unpack(b, s)])
```

```python
import functools

import jax
import jax.numpy as jnp
from jax import lax
from jax.experimental import pallas as pl
from jax.experimental.pallas import tpu as pltpu

F32 = jnp.float32
BF16 = jnp.bfloat16
SDS = jax.ShapeDtypeStruct
MESH = pl.DeviceIdType.MESH

D_MODEL = 1024
N_META = 16
EPS = 1e-6
BLK = 128
CH = 64
PAD = BLK - N_META
HEADS = 16
HD = 64
KVW = 256
DN_H = 16
DN_KH = 8
DK = 128
SLOPES = [2.0 ** (-8.0 * (h + 1) / HEADS) for h in range(HEADS)]
NEG = -1e30
NT = (((1,), (1,)), ((), ()))
TN = (((0,), (0,)), ((), ()))
HI = lax.Precision.HIGHEST

ADAM_LR, ADAM_B1, ADAM_B2, ADAM_EPS, ADAM_WD, ADAM_STEP = 0.001, 0.9, 0.999, 1e-08, 0.01, 10

VMEM_LIMIT = 56 * 1024 * 1024


def _cp(*sem):
    return pltpu.CompilerParams(dimension_semantics=sem, vmem_limit_bytes=VMEM_LIMIT)


def _row_tile(rows):
    for t in (384, 256, 128):
        if rows % t == 0:
            return t
    raise ValueError(rows)


def _dot(a, b, dims=None, precision=None):
    if dims is None:
        return jnp.dot(a, b, preferred_element_type=F32, precision=precision)
    return lax.dot_general(a, b, dims, preferred_element_type=F32, precision=precision)


def _silu(x):
    return x * jax.nn.sigmoid(x)


def _dsilu(x):
    s = jax.nn.sigmoid(x)
    return s * (1.0 + x * (1.0 - s))


def _rms(x):
    return lax.rsqrt(jnp.mean(x * x, axis=-1, keepdims=True) + EPS)


def norm_matmul(h, nw, w, tn, name):
    rows, k = h.shape
    n = w.shape[1]
    tm = _row_tile(rows)

    def norm_body(h_ref, nw_ref, xn_ref):
        x = h_ref[...]
        xn_ref[...] = (x * _rms(x) * nw_ref[...]).astype(BF16)

    xn = pl.pallas_call(
        norm_body, grid=(rows // tm,),
        in_specs=[pl.BlockSpec((tm, k), lambda i: (i, 0)), pl.BlockSpec((1, k), lambda i: (0, 0))],
        out_specs=pl.BlockSpec((tm, k), lambda i: (i, 0)), out_shape=SDS((rows, k), BF16),
        compiler_params=_cp("parallel"), name=name + "_norm")(h, nw)

    def body(a_ref, w_ref, o_ref):
        o_ref[...] = _dot(a_ref[...], w_ref[...])

    out = pl.pallas_call(
        body, grid=(n // tn, rows // tm),
        in_specs=[pl.BlockSpec((tm, k), lambda j, i: (i, 0)), pl.BlockSpec((k, tn), lambda j, i: (0, j))],
        out_specs=pl.BlockSpec((tm, tn), lambda j, i: (i, j)), out_shape=SDS((rows, n), F32),
        compiler_params=_cp("parallel", "parallel"), name=name)(xn, w)
    return out, xn


def matmul_residual(a, w, res, name):
    rows, k = a.shape
    n = w.shape[1]
    tm = _row_tile(rows)

    def body(a_ref, w_ref, r_ref, o_ref):
        o_ref[...] = r_ref[...] + _dot(a_ref[...], w_ref[...])

    return pl.pallas_call(
        body, grid=(rows // tm,),
        in_specs=[pl.BlockSpec((tm, k), lambda i: (i, 0)), pl.BlockSpec((k, n), lambda i: (0, 0)),
                  pl.BlockSpec((tm, n), lambda i: (i, 0))],
        out_specs=pl.BlockSpec((tm, n), lambda i: (i, 0)),
        out_shape=SDS((rows, n), F32), compiler_params=_cp("parallel"), name=name)(a, w, res)


def wgrad(a, b, name):
    rows, k = a.shape
    n = b.shape[1]
    tm = _row_tile(rows)
    tn = min(n, 1024)

    def body(a_ref, b_ref, o_ref):
        @pl.when(pl.program_id(1) == 0)
        def _():
            o_ref[...] = jnp.zeros_like(o_ref)

        o_ref[...] += _dot(a_ref[...], b_ref[...].astype(BF16), TN)

    return pl.pallas_call(
        body, grid=(n // tn, rows // tm),
        in_specs=[pl.BlockSpec((tm, k), lambda j, i: (i, 0)), pl.BlockSpec((tm, tn), lambda j, i: (i, j))],
        out_specs=pl.BlockSpec((k, tn), lambda j, i: (0, j)),
        out_shape=SDS((k, n), F32), compiler_params=_cp("parallel", "arbitrary"), name=name)(a, b)


def in_proj_bwd(dus, ws, h, nw, dh_next, name):
    rows, k = h.shape
    tm = _row_tile(rows)
    nd = len(dus)
    nt = rows // tm

    def body(*refs):
        du_refs, w_refs = refs[:nd], refs[nd:2 * nd]
        h_ref, nw_ref, dhn_ref, dh_ref, dnw_ref = refs[2 * nd:]
        dxn = _dot(du_refs[0][...].astype(BF16), w_refs[0][...], NT)
        for du_ref, w_ref in zip(du_refs[1:], w_refs[1:]):
            dxn += _dot(du_ref[...].astype(BF16), w_ref[...], NT)
        x = h_ref[...]
        r = _rms(x)
        y = x * r
        gy = dxn * nw_ref[...]
        dh_ref[...] = dhn_ref[...] + r * (gy - y * jnp.mean(y * gy, axis=-1, keepdims=True))
        dnw_ref[0] = jnp.sum(dxn * y, axis=0, keepdims=True)

    in_specs = [pl.BlockSpec((tm, du.shape[1]), lambda i: (i, 0)) for du in dus]
    in_specs += [pl.BlockSpec(w.shape, lambda i: (0, 0)) for w in ws]
    in_specs += [pl.BlockSpec((tm, k), lambda i: (i, 0)), pl.BlockSpec((1, k), lambda i: (0, 0)),
                 pl.BlockSpec((tm, k), lambda i: (i, 0))]
    return pl.pallas_call(
        body, grid=(nt,), in_specs=in_specs,
        out_specs=[pl.BlockSpec((tm, k), lambda i: (i, 0)), pl.BlockSpec((1, 1, k), lambda i: (i, 0, 0))],
        out_shape=[SDS((rows, k), F32), SDS((nt, 1, k), F32)],
        compiler_params=_cp("parallel"), name=name)(*dus, *ws, h, nw, dh_next)


def matmul_nt(a, w, name):
    rows, k = a.shape
    n = w.shape[0]
    tm = _row_tile(rows)

    def body(a_ref, w_ref, o_ref):
        o_ref[...] = _dot(a_ref[...].astype(BF16), w_ref[...], NT)

    return pl.pallas_call(
        body, grid=(rows // tm,),
        in_specs=[pl.BlockSpec((tm, k), lambda i: (i, 0)), pl.BlockSpec((n, k), lambda i: (0, 0))],
        out_specs=pl.BlockSpec((tm, n), lambda i: (i, 0)),
        out_shape=SDS((rows, n), F32), compiler_params=_cp("parallel"), name=name)(a, w)


SUB = 64
GRP = 8
TR = GRP * SUB
NBAND = 192
TK = 256


def _tile_bias(n, sb):
    r = lax.broadcasted_iota(jnp.int32, (TR, TK), 0)
    c = lax.broadcasted_iota(jnp.int32, (TR, TK), 1)
    qi = r & (SUB - 1)
    d = BLK + qi - c
    dm = n * BLK + SUB * sb - PAD + NBAND + qi - c
    band = c < NBAND
    valid = (band & (d >= 0) & (d < BLK) & (c >= 2 * BLK - BLK * n - SUB * sb)) | (
        (c >= NBAND) & (c < NBAND + N_META) & (dm >= 0))
    return valid, jnp.where(band, d, jnp.minimum(dm, BLK)).astype(F32)


def _group_col(vals):
    g = lax.broadcasted_iota(jnp.int32, (TR, 1), 0) >> 6
    col = jnp.zeros((TR, 1), F32)
    for gi, v in enumerate(vals):
        col = jnp.where(g == gi, v, col)
    return col


def _stack_heads(ref, sb, kvh):
    return jnp.concatenate(
        [ref[SUB * sb:SUB * sb + SUB, HD * (GRP * kvh + g):HD * (GRP * kvh + g) + HD] for g in range(GRP)], axis=0)


def _unstack_heads(parts):
    return jnp.concatenate([parts[kvh][SUB * g:SUB * g + SUB] for kvh in range(2) for g in range(GRP)], axis=1)


def _tile_keys(band, meta, sb):
    return jnp.concatenate([band[SUB * sb:SUB * sb + NBAND], meta,
                            jnp.zeros((TK - NBAND - N_META, HD), band.dtype)], axis=0)


def _row_sums(x):
    ones = jnp.ones((x.shape[1], 128), BF16)
    hi = x.astype(BF16)
    lo = (x - hi.astype(F32)).astype(BF16)
    return _dot(hi, ones) + _dot(lo, ones)


def _rms_stack(q):
    return lax.rsqrt(_row_sums(q * q)[:, :HD] * (1.0 / HD) + EPS)


def _fill_bias(bias_scr, n):
    @pl.when(n <= 2)
    def _():
        for sb in range(2):
            valid, dist = _tile_bias(n, sb)
            for kvh in range(2):
                slope_col = _group_col([SLOPES[GRP * kvh + g] for g in range(GRP)])
                bias_scr[2 * sb + kvh] = jnp.where(valid, -slope_col * dist, NEG)


def _tile_vals(band, meta, sb):
    return jnp.concatenate([_tile_keys(band, meta, sb), jnp.ones((TK, 3 * HD), BF16)], axis=1)


def _tile_softmax(qn16, k16, vx16, bias, sink_col):
    s = _dot(qn16, k16, NT) * (HD ** -0.5) + bias
    mx = jnp.maximum(jnp.max(s.astype(BF16), axis=-1, keepdims=True).astype(F32), sink_col)
    e = jnp.exp(s - mx)
    es = jnp.exp(sink_col - mx)
    ox = _dot(e.astype(BF16), vx16)
    return e, 1.0 / (ox[:, 2 * HD:] + es), es, ox[:, :HD]


def _kv_heads(kvb, kvm, kw_):
    out = []
    for kvh in range(2):
        kb, km = kvb[:, HD * kvh:HD * kvh + HD], kvm[:, HD * kvh:HD * kvh + HD]
        out.append(((kb * _rms(kb) * kw_).astype(BF16), (km * _rms(km) * kw_).astype(BF16),
                    kvb[:, BLK + HD * kvh:BLK + HD * kvh + HD].astype(BF16),
                    kvm[:, BLK + HD * kvh:BLK + HD * kvh + HD].astype(BF16)))
    return out


def _kv_specs(nblk, clamp):
    cur = (lambda n: (jnp.minimum(n, nblk - 1), 8)) if clamp else (lambda n: (n, 8))
    return [pl.BlockSpec((BLK, KVW), cur),
            pl.BlockSpec((BLK, KVW), lambda n: (jnp.maximum(n - 1, 0), 8)),
            pl.BlockSpec((N_META, KVW), lambda n: (PAD // N_META, 8))]


def _sink_cols(sinks):
    return jnp.repeat(sinks.reshape(2, GRP), SUB, axis=1).reshape(2, TR, 1)


SINK_SPEC = pl.BlockSpec((2, TR, 1), lambda n: (0, 0, 0))


def attn_fwd(u, qw, kw, sinks):
    rows = u.shape[0]
    nblk = rows // BLK

    def body(q_ref, g_ref, kvc_ref, kvp_ref, kvm_ref, qw_ref, kw_ref, sc_ref, og_ref, bias_scr):
        _fill_bias(bias_scr, pl.program_id(0))
        qw_ = qw_ref[...]
        kv = _kv_heads(jnp.concatenate([kvp_ref[...], kvc_ref[...]], axis=0), kvm_ref[...], kw_ref[...])
        for sb in range(2):
            parts = []
            for kvh in range(2):
                knb, knm, vb, vm = kv[kvh]
                q = _stack_heads(q_ref, sb, kvh)
                qn16 = (q * _rms_stack(q) * qw_).astype(BF16)
                _, inv, _, o = _tile_softmax(qn16, _tile_keys(knb, knm, sb), _tile_vals(vb, vm, sb),
                                             bias_scr[2 * sb + kvh], sc_ref[kvh])
                parts.append(o * inv[:, :HD])
            rows = slice(SUB * sb, SUB * sb + SUB)
            og_ref[rows, :] = (_unstack_heads(parts) * _silu(g_ref[rows, :])).astype(BF16)

    small = lambda w: pl.BlockSpec((1, w), lambda n: (0, 0))
    return pl.pallas_call(
        body, grid=(nblk,),
        in_specs=[pl.BlockSpec((BLK, 1024), lambda n: (n, 0)), pl.BlockSpec((BLK, 1024), lambda n: (n, 1))]
        + _kv_specs(nblk, False) + [small(HD), small(HD), SINK_SPEC],
        out_specs=pl.BlockSpec((BLK, 1024), lambda n: (n, 0)),
        out_shape=SDS((rows, 1024), BF16), scratch_shapes=[pltpu.VMEM((4, TR, TK), F32)],
        compiler_params=_cp("arbitrary"), name="attn_fwd")(u, u, u, u, u, qw, kw, _sink_cols(sinks))


def attn_bwd(u, qw, kw, sinks, dog):
    rows = u.shape[0]
    nblk = rows // BLK

    def knorm_bwd(k, dkn, kw_):
        r = _rms(k)
        y = k * r
        gy = dkn * kw_
        return r * (gy - y * jnp.mean(y * gy, axis=-1, keepdims=True)), jnp.sum(dkn * y, axis=0, keepdims=True)

    def body(q_ref, g_ref, dog_ref, kvc_ref, kvp_ref, kvm_ref, qw_ref, kw_ref, sc_ref,
             dq_ref, dg_ref, dkv_ref, dkvm_ref, dqw_ref, dkw_ref, dsk_ref, carry, prevp, curp, metap, bias_scr):
        n = pl.program_id(0)
        qw_, kw_ = qw_ref[...], kw_ref[...]
        _fill_bias(bias_scr, n)

        @pl.when(n == 0)
        def _():
            carry[...] = jnp.zeros_like(carry)
            metap[...] = jnp.zeros_like(metap)
            dqw_ref[...] = jnp.zeros_like(dqw_ref)
            dkw_ref[...] = jnp.zeros_like(dkw_ref)
            dsk_ref[...] = jnp.zeros_like(dsk_ref)

        @pl.when(n == nblk)
        def _():
            prevp[...] = jnp.zeros_like(prevp)
            curp[...] = jnp.zeros_like(curp)

        @pl.when(n < nblk)
        def _():
            kv = _kv_heads(jnp.concatenate([kvp_ref[...], kvc_ref[...]], axis=0), kvm_ref[...], kw_)
            lane = lax.broadcasted_iota(jnp.int32, (1, HEADS), 1)
            dqw = jnp.zeros((1, HD), F32)
            dsk = jnp.zeros((1, HEADS), F32)
            band_parts = [jnp.zeros((2 * BLK, HD), F32) for _ in range(4)]
            meta_parts = [jnp.zeros((N_META, HD), F32) for _ in range(4)]

            def widen(x, sb):
                z = jnp.zeros((2 * BLK - NBAND, HD), F32)
                return jnp.concatenate([x, z] if sb == 0 else [z, x], axis=0)

            for sb in range(2):
                rows = slice(SUB * sb, SUB * sb + SUB)
                dq_parts, dg_parts = [], []
                for kvh in range(2):
                    knb, knm, vb, vm = kv[kvh]
                    k16, v16 = _tile_keys(knb, knm, sb), _tile_keys(vb, vm, sb)
                    q = _stack_heads(q_ref, sb, kvh)
                    r = _rms_stack(q)
                    y = q * r
                    qn16 = (y * qw_).astype(BF16)
                    e, inv, es, o = _tile_softmax(qn16, k16, _tile_vals(vb, vm, sb), bias_scr[2 * sb + kvh],
                                                  sc_ref[kvh])
                    p = e * jnp.concatenate([inv, inv], axis=1)
                    p16 = p.astype(BF16)
                    o = o * inv[:, :HD]
                    gate = _stack_heads(g_ref, sb, kvh)
                    dog_ = _stack_heads(dog_ref, sb, kvh)
                    dg_parts.append(dog_ * o * _dsilu(gate))
                    do_ = dog_ * _silu(gate)
                    do16 = do_.astype(BF16)
                    dp = _dot(do16, v16, NT)
                    delta = _row_sums(do_ * o)
                    ds16 = (p * (dp - jnp.concatenate([delta, delta], axis=1))).astype(BF16)
                    dsink = -(es * inv) * delta
                    for g in range(GRP):
                        dsk += jnp.where(lane == GRP * kvh + g,
                                         jnp.sum(dsink[SUB * g:SUB * g + SUB, :HEADS], axis=0, keepdims=True), 0.0)
                    dqn = _dot(ds16, k16) * (HD ** -0.5)
                    dk = (_dot((y * qw_).T.astype(BF16), ds16) * (HD ** -0.5)).T
                    dv = _dot(do_.T.astype(BF16), p16).T
                    band_parts[kvh] += widen(dk[:NBAND], sb)
                    band_parts[2 + kvh] += widen(dv[:NBAND], sb)
                    meta_parts[kvh] += dk[NBAND:NBAND + N_META]
                    meta_parts[2 + kvh] += dv[NBAND:NBAND + N_META]
                    gy = dqn * qw_
                    dq_parts.append(r * (gy - y * (_row_sums(y * gy)[:, :HD] * (1.0 / HD))))
                    dqw += jnp.sum(dqn * y, axis=0, keepdims=True)
                dq_ref[rows, :] = _unstack_heads(dq_parts).astype(BF16)
                dg_ref[rows, :] = _unstack_heads(dg_parts).astype(BF16)
            band = jnp.concatenate(band_parts, axis=1)
            prevp[...] = band[:BLK]
            curp[...] = band[BLK:]
            metap[...] += jnp.concatenate(meta_parts, axis=1)
            dqw_ref[...] += dqw
            dsk_ref[...] += dsk

        tot = carry[...] + prevp[...]
        kprev = kvp_ref[...]
        dk0, w0 = knorm_bwd(kprev[:, 0:HD], tot[:, 0:HD], kw_)
        dk1, w1 = knorm_bwd(kprev[:, HD:2 * HD], tot[:, HD:2 * HD], kw_)
        dkv_ref[...] = jnp.concatenate([dk0, dk1, tot[:, 2 * HD:]], axis=1)
        dkw_ref[...] += w0 + w1
        carry[...] = curp[...]

        @pl.when(n == nblk)
        def _():
            mt = metap[...]
            km = kvm_ref[...]
            m0, v0 = knorm_bwd(km[:, 0:HD], mt[:, 0:HD], kw_)
            m1, v1 = knorm_bwd(km[:, HD:2 * HD], mt[:, HD:2 * HD], kw_)
            dkvm_ref[...] = jnp.concatenate([m0, m1, mt[:, 2 * HD:]], axis=1)
            dkw_ref[...] += v0 + v1

    small = lambda w: pl.BlockSpec((1, w), lambda n: (0, 0))
    cl = lambda n: jnp.minimum(n, nblk - 1)
    return pl.pallas_call(
        body, grid=(nblk + 1,),
        in_specs=[pl.BlockSpec((BLK, 1024), lambda n: (cl(n), 0)), pl.BlockSpec((BLK, 1024), lambda n: (cl(n), 1)),
                  pl.BlockSpec((BLK, 1024), lambda n: (cl(n), 0))]
        + _kv_specs(nblk, True) + [small(HD), small(HD), SINK_SPEC],
        out_specs=[pl.BlockSpec((BLK, 1024), lambda n: (cl(n), 0)), pl.BlockSpec((BLK, 1024), lambda n: (cl(n), 0)),
                   pl.BlockSpec((BLK, KVW), lambda n: (jnp.maximum(n - 1, 0), 0)),
                   pl.BlockSpec((N_META, KVW), lambda n: (0, 0)), small(HD), small(HD), small(HEADS)],
        out_shape=[SDS((rows, 1024), BF16), SDS((rows, 1024), BF16), SDS((rows, KVW), F32), SDS((N_META, KVW), F32),
                   SDS((1, HD), F32), SDS((1, HD), F32), SDS((1, HEADS), F32)],
        scratch_shapes=[pltpu.VMEM((BLK, KVW), F32), pltpu.VMEM((BLK, KVW), F32), pltpu.VMEM((BLK, KVW), F32),
                        pltpu.VMEM((N_META, KVW), F32), pltpu.VMEM((4, TR, TK), F32)],
        compiler_params=_cp("arbitrary"), name="attn_bwd")(u, u, dog, u, u, u, qw, kw, _sink_cols(sinks))


HB = 16


def _bdot(a, b, kind, split=False):
    dims = {"nn": ((2,), (1,)), "nt": ((2,), (2,)), "tn": ((1,), (1,))}[kind]
    dg = lambda p, q: lax.dot_general(p, q, (dims, ((0,), (0,))), preferred_element_type=F32)
    if not split:
        return dg(a, b)
    ah, bh = a.astype(BF16), b.astype(BF16)
    al, bl = (a - ah.astype(F32)).astype(BF16), (b - bh.astype(F32)).astype(BF16)
    return (dg(ah, bl) + dg(al, bh)) + dg(ah, bh)


def _head_cols(hv, beta, gc, gct, lane):
    sel = lane == hv
    return _pick(beta, sel), _pick(gc, sel), gct[pl.ds(hv, 1), :]


def _conv_group(xc_ref, xp_ref, cw_ref, off, first):
    xp = jnp.where(first, 0.0, xp_ref[:, pl.ds(off, DK)])
    xx = jnp.concatenate([xp, xc_ref[:, pl.ds(off, DK)]], axis=0)
    y = cw_ref[0:1, pl.ds(off, DK)] * xx[5:5 + CH]
    for j in range(1, 4):
        y += cw_ref[j:j + 1, pl.ds(off, DK)] * xx[5 + j:5 + j + CH]
    return xx, y


def _gates(ba, al, dtb, c):
    row = c * CH + lax.broadcasted_iota(jnp.int32, (CH, DN_H), 0)
    real = row >= PAD
    xa = ba[:, DN_H:2 * DN_H] + dtb
    beta = jnp.where(real, jax.nn.sigmoid(ba[:, 0:DN_H]), 0.0)
    g = jnp.where(real, -jnp.exp(al) * jax.nn.softplus(xa), 0.0)
    return real, xa, beta, g


def _pick(x, sel):
    return jnp.sum(jnp.where(sel, x, 0.0), axis=1, keepdims=True)


def _chunk_specs(width_blocks):
    return [pl.BlockSpec((CH, 4096), lambda c: (c, 0)),
            pl.BlockSpec((8, 4096), lambda c: (jnp.maximum(8 * c - 1, 0), 0)),
            pl.BlockSpec((CH, DK), lambda c: (c, 48))]


def _tri_inv(m, ii, jj):
    eye = (ii == jj).astype(BF16)
    mh, ml = _split(m)
    blk8 = (ii >> 3) == (jj >> 3)
    mb = (jnp.where(blk8, mh, 0), jnp.where(blk8, ml, 0))
    m2 = _split(_dot3(mb, mb))
    m4 = _split(_dot3(m2, m2))
    x = _dot3(_split(_dot3((eye - mb[0], -mb[1]), (eye + m2[0], m2[1]))), (eye + m4[0], m4[1]))
    for sh in (3, 4, 5):
        off = ((ii >> (sh + 1)) == (jj >> (sh + 1))) & ((ii >> sh) != (jj >> sh))
        xs = _split(x)
        x = x - _dot3(_split(_dot3(xs, (jnp.where(off, mh, 0), jnp.where(off, ml, 0)))), xs)
    return x


def _split(x):
    hi = x.astype(BF16)
    return hi, (x - hi.astype(F32)).astype(BF16)


def _dot3(a, b):
    dg = lambda p, q: lax.dot_general(p, q, ((((2,), (1,))), ((0,), (0,))), preferred_element_type=F32)
    return (dg(a[0], b[1]) + dg(a[1], b[0])) + dg(a[0], b[0])


def dn_prep(udn, conv_w, a_log, dt_bias):
    rows = udn.shape[0]
    nch = rows // CH

    def body(xc_ref, xp_ref, ba_ref, cw_ref, al_ref, dtb_ref,
             qn_ref, kn_ref, sv_ref, gc_ref, beta_ref, u_ref, w_ref, qe_ref, ks_ref, p_ref, at_ref, pt_ref,
             qet_ref, wt_ref, kst_ref, gct):
        c = pl.program_id(0)
        first = c == 0
        _, _, beta, g = _gates(ba_ref[...], al_ref[...], dtb_ref[...], c)
        ii = lax.broadcasted_iota(jnp.int32, (CH, CH), 0)
        jj = lax.broadcasted_iota(jnp.int32, (CH, CH), 1)
        gc = _dot((ii >= jj).astype(F32), g, precision=HI)
        gc_ref[...] = gc
        beta_ref[...] = beta
        gct[...] = gc.T

        def qk_body(kh, carry):
            off = pl.multiple_of(kh * DK, DK)
            _, yq = _conv_group(xc_ref, xp_ref, cw_ref, off, first)
            sq = _silu(yq)
            qn_ref[:, pl.ds(off, DK)] = sq * lax.rsqrt(jnp.sum(sq * sq, axis=-1, keepdims=True) + EPS) * (DK ** -0.5)
            _, yk = _conv_group(xc_ref, xp_ref, cw_ref, pl.multiple_of(1024 + kh * DK, DK), first)
            sk = _silu(yk)
            kn_ref[:, pl.ds(off, DK)] = sk * lax.rsqrt(jnp.sum(sk * sk, axis=-1, keepdims=True) + EPS)
            return carry

        lax.fori_loop(0, DN_KH, qk_body, 0)
        lane = lax.broadcasted_iota(jnp.int32, (CH, DN_H), 1)
        zpad = jnp.zeros((CH, DK - CH), F32)

        def v_group(grp, carry):
            offs, ks_, qs_, vs_, cols = [], [], [], [], []
            for i in range(HB):
                hv = grp * HB + i
                offs.append(pl.multiple_of(hv * DK, DK))
                koff = pl.multiple_of((grp * (HB // 2) + i // 2) * DK, DK)
                _, yv = _conv_group(xc_ref, xp_ref, cw_ref, pl.multiple_of(2048 + hv * DK, DK), first)
                vs_.append(_silu(yv))
                sv_ref[:, pl.ds(offs[i], DK)] = vs_[i]
                ks_.append(kn_ref[:, pl.ds(koff, DK)])
                qs_.append(qn_ref[:, pl.ds(koff, DK)])
                cols.append(_head_cols(hv, beta, gc, gct, lane))
            k, q, v = jnp.stack(ks_), jnp.stack(qs_), jnp.stack(vs_)
            beta_c, gc_c, gc_r = (jnp.stack([c_[j] for c_ in cols]) for j in range(3))
            dec = jnp.exp(jnp.where(ii >= jj, gc_c - gc_r, NEG))
            eg = jnp.exp(gc_c)
            kb = k * beta_c
            k16 = k.astype(BF16)
            m = jnp.where(ii > jj, _bdot(kb.astype(BF16), k16, "nt") * dec, 0.0)
            a = _tri_inv(m, ii, jj)
            uw = _bdot(a, jnp.concatenate([v * beta_c, kb * eg], axis=2), "nn", True)
            p = _bdot(q.astype(BF16), k16, "nt") * dec
            qe = q * eg
            ksx = k * jnp.exp(gc_c[:, CH - 1:CH, :] - gc_c)
            tslot = lambda x: jnp.concatenate([x.T, jnp.zeros((DK, DK - CH), F32)], axis=1).astype(BF16)
            for i in range(HB):
                sl = pl.ds(offs[i], DK)
                u_ref[:, sl] = uw[i, :, :DK]
                w_ref[:, sl] = uw[i, :, DK:]
                qe_ref[:, sl] = qe[i].astype(BF16)
                ks_ref[:, sl] = ksx[i].astype(BF16)
                p_ref[:, sl] = jnp.concatenate([p[i], zpad], axis=1).astype(BF16)
                at_ref[:, sl] = jnp.concatenate([a[i].T, zpad], axis=1)
                pt_ref[:, sl] = jnp.concatenate([p[i].T, zpad], axis=1).astype(BF16)
                qet_ref[:, sl] = tslot(qe[i])
                wt_ref[:, sl] = tslot(uw[i, :, DK:])
                kst_ref[:, sl] = tslot(ksx[i])
            return carry

        lax.fori_loop(0, DN_H // HB, v_group, 0)

    full = lambda shape: pl.BlockSpec(shape, lambda c: (0, 0))
    blk = lambda w: pl.BlockSpec((CH, w), lambda c: (c, 0))
    return pl.pallas_call(
        body, grid=(nch,),
        in_specs=_chunk_specs(0) + [full((4, 4096)), full((1, DN_H)), full((1, DN_H))],
        out_specs=[blk(1024), blk(1024), blk(2048), blk(DN_H), blk(DN_H), blk(2048), blk(2048), blk(2048), blk(2048),
                   blk(2048), blk(2048), blk(2048)] + [pl.BlockSpec((DK, 2048), lambda c: (c, 0))] * 3,
        out_shape=[SDS((rows, 1024), F32), SDS((rows, 1024), F32), SDS((rows, 2048), F32), SDS((rows, DN_H), F32),
                   SDS((rows, DN_H), F32), SDS((rows, 2048), F32), SDS((rows, 2048), F32), SDS((rows, 2048), BF16),
                   SDS((rows, 2048), BF16), SDS((rows, 2048), BF16), SDS((rows, 2048), F32),
                   SDS((rows, 2048), BF16)] + [SDS((2 * rows, 2048), BF16)] * 3,
        scratch_shapes=[pltpu.VMEM((DN_H, CH), F32)],
        compiler_params=_cp("parallel"), name="dn_prep")(udn, udn, udn, conv_w, a_log, dt_bias)


def dn_scan(u, w, qe, kst, p, gc):
    rows = u.shape[0]
    nch = rows // CH

    def body(u_ref, w_ref, qe_ref, kst_ref, p_ref, gc_ref, o_ref, vn_ref, st_ref, s_scr):
        @pl.when(pl.program_id(0) == 0)
        def _():
            s_scr[...] = jnp.zeros_like(s_scr)

        gl_row = gc_ref[CH - 1:CH, :]
        lane = lax.broadcasted_iota(jnp.int32, (1, DN_H), 1)

        def group(grp, carry):
            base = grp * HB
            sls = [pl.ds(pl.multiple_of((base + i) * DK, DK), DK) for i in range(HB)]
            heads = lambda ref: jnp.stack([ref[:, sl] for sl in sls])
            s = s_scr[pl.ds(base, HB)]
            st_ref[0, pl.ds(base, HB)] = s
            s16 = s.astype(BF16)
            vn = heads(u_ref) - _bdot(heads(w_ref).astype(BF16), s16, "nn")
            vn16 = vn.astype(BF16)
            o = _bdot(heads(qe_ref), s16, "nn") + _bdot(heads(p_ref)[:, :, 0:CH], vn16, "nn")
            egl = jnp.exp(jnp.stack([_pick(gl_row, lane == base + i) for i in range(HB)]))
            s_scr[pl.ds(base, HB)] = s * egl + _bdot(heads(kst_ref)[:, :, 0:CH], vn16, "nn")
            for i in range(HB):
                vn_ref[:, sls[i]] = vn16[i]
                o_ref[:, sls[i]] = o[i]
            return carry

        lax.fori_loop(0, DN_H // HB, group, 0)

    blk = lambda wd: pl.BlockSpec((CH, wd), lambda c: (c, 0))
    return pl.pallas_call(
        body, grid=(nch,),
        in_specs=[blk(2048)] * 3 + [pl.BlockSpec((DK, 2048), lambda c: (c, 0)), blk(2048), blk(DN_H)],
        out_specs=[blk(2048), blk(2048), pl.BlockSpec((1, DN_H, DK, DK), lambda c: (c, 0, 0, 0))],
        out_shape=[SDS((rows, 2048), F32), SDS((rows, 2048), BF16), SDS((nch, DN_H, DK, DK), F32)],
        scratch_shapes=[pltpu.VMEM((DN_H, DK, DK), F32)],
        compiler_params=_cp("arbitrary"), name="dn_scan")(u, w, qe, kst, p, gc)


def dn_out_fwd(o, udn, ow, wout, h1, tgt):
    rows = o.shape[0]
    tm = _row_tile(rows)
    nt = rows // tm

    def body(o_ref, z_ref, ow_ref, w_ref, h_ref, t_ref, dh_ref, on_ref, ls_ref):
        for hv in range(DN_H):
            sl = slice(hv * DK, hv * DK + DK)
            oh = o_ref[:, sl]
            on_ref[:, sl] = (oh * _rms(oh) * ow_ref[...] * _silu(z_ref[:, sl])).astype(BF16)
        h2 = h_ref[...] + _dot(on_ref[...], w_ref[...])
        row = pl.program_id(0) * tm + lax.broadcasted_iota(jnp.int32, (tm, 1), 0)
        err = jnp.where(row >= BLK, h2 - t_ref[...], 0.0)
        dh_ref[...] = err * (1.0 / D_MODEL)
        ls_ref[0] = jnp.sum(err * err, axis=0, keepdims=True)

    return pl.pallas_call(
        body, grid=(nt,),
        in_specs=[pl.BlockSpec((tm, 2048), lambda i: (i, 0)), pl.BlockSpec((tm, 2048), lambda i: (i, 2)),
                  pl.BlockSpec((1, DK), lambda i: (0, 0)), pl.BlockSpec((2048, D_MODEL), lambda i: (0, 0)),
                  pl.BlockSpec((tm, D_MODEL), lambda i: (i, 0)), pl.BlockSpec((tm, D_MODEL), lambda i: (i, 0))],
        out_specs=[pl.BlockSpec((tm, D_MODEL), lambda i: (i, 0)), pl.BlockSpec((tm, 2048), lambda i: (i, 0)),
                   pl.BlockSpec((1, 1, D_MODEL), lambda i: (i, 0, 0))],
        out_shape=[SDS((rows, D_MODEL), F32), SDS((rows, 2048), BF16), SDS((nt, 1, D_MODEL), F32)],
        compiler_params=_cp("parallel"), name="dn_out_fwd")(o, udn, ow, wout, h1, tgt)


def dn_out_bwd(dh2, wout, o, udn, ow):
    rows = o.shape[0]
    tm = _row_tile(rows)
    nt = rows // tm

    def body(dh_ref, w_ref, o_ref, z_ref, ow_ref, do_ref, dz_ref, dow_ref):
        don = _dot(dh_ref[...].astype(BF16), w_ref[...], NT)
        ow_ = ow_ref[...]
        dow = jnp.zeros((1, DK), F32)
        for hv in range(DN_H):
            sl = slice(hv * DK, hv * DK + DK)
            oh = o_ref[:, sl]
            r = _rms(oh)
            y = oh * r
            z = z_ref[:, sl]
            dn = don[:, sl] * _silu(z)
            dz_ref[:, sl] = (don[:, sl] * (y * ow_) * _dsilu(z)).astype(BF16)
            dy = dn * ow_
            do_ref[:, sl] = r * (dy - y * jnp.mean(y * dy, axis=-1, keepdims=True))
            dow += jnp.sum(dn * y, axis=0, keepdims=True)
        dow_ref[0] = dow

    return pl.pallas_call(
        body, grid=(nt,),
        in_specs=[pl.BlockSpec((tm, D_MODEL), lambda i: (i, 0)), pl.BlockSpec((2048, D_MODEL), lambda i: (0, 0)),
                  pl.BlockSpec((tm, 2048), lambda i: (i, 0)), pl.BlockSpec((tm, 2048), lambda i: (i, 2)),
                  pl.BlockSpec((1, DK), lambda i: (0, 0))],
        out_specs=[pl.BlockSpec((tm, 2048), lambda i: (i, 0)), pl.BlockSpec((tm, 2048), lambda i: (i, 0)),
                   pl.BlockSpec((1, 1, DK), lambda i: (i, 0, 0))],
        out_shape=[SDS((rows, 2048), F32), SDS((rows, 2048), BF16), SDS((nt, 1, DK), F32)],
        compiler_params=_cp("parallel"), name="dn_out_bwd")(dh2, wout, o, udn, ow)


def dn_scan_bwd(do, qn, kn, sv, gc, beta, at, pt, u, w, vn, qet, wt, ks, st):
    rows = do.shape[0]
    nch = rows // CH

    def body(do_ref, q_ref, k_ref, v_ref, gc_ref, beta_ref, at_ref, pt_ref, u_ref, w_ref, vn_ref, qet_ref, wt_ref,
             ks_ref, st_ref, dq_ref, dk_ref, dv_ref, dbeta_ref, dg_ref, ds_scr, gct):
        @pl.when(pl.program_id(0) == 0)
        def _():
            ds_scr[...] = jnp.zeros_like(ds_scr)

        gc, beta = gc_ref[...], beta_ref[...]
        gct[...] = gc.T
        ii = lax.broadcasted_iota(jnp.int32, (CH, CH), 0)
        jj = lax.broadcasted_iota(jnp.int32, (CH, CH), 1)
        lane = lax.broadcasted_iota(jnp.int32, (CH, DN_H), 1)
        last = lax.broadcasted_iota(jnp.int32, (CH, 1), 0) == CH - 1

        def group(grp, carry):
            dbeta_acc, dgc_acc = carry
            base = grp * HB
            sls = [pl.ds(pl.multiple_of((base + i) * DK, DK), DK) for i in range(HB)]
            ksls = [pl.ds(pl.multiple_of((grp * (HB // 2) + j) * DK, DK), DK) for j in range(HB // 2)]
            heads = lambda ref: jnp.stack([ref[:, sl] for sl in sls])
            kheads = lambda ref: jnp.stack([ref[:, ksls[i // 2]] for i in range(HB)])
            cols = [_head_cols(base + i, beta, gc, gct, lane) for i in range(HB)]
            beta_c, gc_c, gc_r = (jnp.stack([c_[j] for c_ in cols]) for j in range(3))
            k, q, v = kheads(k_ref), kheads(q_ref), heads(v_ref)
            dec = jnp.exp(jnp.where(ii >= jj, gc_c - gc_r, NEG))
            eg = jnp.exp(gc_c)
            gl = gc_c[:, CH - 1:CH, :]
            e2 = jnp.exp(gl - gc_c)
            egl = jnp.exp(gl)
            k16, q16 = k.astype(BF16), q.astype(BF16)
            do16 = heads(do_ref).astype(BF16)
            s = st_ref[0, pl.ds(base, HB)]
            s16 = s.astype(BF16)
            dso = ds_scr[pl.ds(base, HB)]
            dso16 = dso.astype(BF16)
            wf, uf, vn16 = heads(w_ref), heads(u_ref), heads(vn_ref)
            kb = k * beta_c
            kb16 = kb.astype(BF16)
            pm = _bdot(q16, k16, "nt") * dec
            m = jnp.where(ii > jj, _bdot(kb16, k16, "nt") * dec, 0.0)
            dvn = _bdot(heads(pt_ref)[:, :, 0:CH], do16, "nn") + _bdot(heads(ks_ref), dso16, "nn")
            dvn16 = dvn.astype(BF16)
            ds_scr[pl.ds(base, HB)] = (egl * dso + _bdot(heads(qet_ref)[:, :, 0:CH], do16, "nn")
                                       - _bdot(heads(wt_ref)[:, :, 0:CH], dvn16, "nn"))
            dpm = jnp.where(ii >= jj, _bdot(do16, vn16, "nt"), 0.0)
            dqk16 = (dpm * dec).astype(BF16)
            dqe = _bdot(do16, s16, "nt")
            dq = eg * dqe + _bdot(dqk16, k16, "nn")
            dks = _bdot(vn16, dso16, "nt")
            dw = -_bdot(dvn16, s16, "nt")
            dbvk = _bdot(heads(at_ref)[:, :, 0:CH], jnp.concatenate([dvn, dw], axis=2), "nn", True)
            dbv, dbk = dbvk[:, :, :DK], dbvk[:, :, DK:]
            dm = jnp.where(ii > jj, -_bdot(dbvk, jnp.concatenate([uf, wf], axis=2), "nt", True), 0.0)
            g16 = (dm * dec).astype(BF16)
            dkb = _bdot(g16, k16, "nn")
            dk = (_bdot(dqk16, q16, "tn") + e2 * dks + _bdot(g16, kb16, "tn") + beta_c * (eg * dbk + dkb))
            e = dpm * pm + dm * m
            rsum = lambda x: jnp.sum(x, axis=2, keepdims=True)
            r_bk, r_qe, r_beta, r_ks = rsum(dbk * k), rsum(q * dqe), rsum(dbv * v + dkb * k), rsum(dks * k)
            t = r_ks * e2
            dgl = jnp.sum(t, axis=1, keepdims=True) + egl * rsum(jnp.sum(dso * s, axis=1, keepdims=True))
            deg = r_qe + beta_c * r_bk
            dgc = rsum(e) - t + deg * eg + jnp.where(last, dgl, 0.0)
            dgrow = -jnp.sum(e, axis=1, keepdims=True)
            dv = beta_c * dbv
            dbeta = r_beta + eg * r_bk
            for i in range(HB):
                dv_ref[:, sls[i]] = dv[i]
                sel = lane == base + i
                dbeta_acc = jnp.where(sel, dbeta[i], dbeta_acc)
                dgc_acc = jnp.where(sel, dgc[i], dgc_acc)
                gct[pl.ds(base + i, 1), :] = dgrow[i]
            for j in range(HB // 2):
                dq_ref[:, ksls[j]] = dq[2 * j] + dq[2 * j + 1]
                dk_ref[:, ksls[j]] = dk[2 * j] + dk[2 * j + 1]
            return dbeta_acc, dgc_acc

        zero = jnp.zeros((CH, DN_H), F32)
        dbeta_acc, dgc_acc = lax.fori_loop(0, DN_H // HB, group, (zero, zero))
        dbeta_ref[...] = dbeta_acc
        dg_ref[...] = _dot((ii <= jj).astype(F32), dgc_acc + gct[...].T, precision=HI)

    rev = lambda wd: pl.BlockSpec((CH, wd), lambda i: (nch - 1 - i, 0))
    rev_t = pl.BlockSpec((DK, 2048), lambda i: (nch - 1 - i, 0))
    return pl.pallas_call(
        body, grid=(nch,),
        in_specs=[rev(2048), rev(1024), rev(1024), rev(2048), rev(DN_H), rev(DN_H), rev(2048), rev(2048), rev(2048),
                  rev(2048), rev(2048), rev_t, rev_t, rev(2048),
                  pl.BlockSpec((1, DN_H, DK, DK), lambda i: (nch - 1 - i, 0, 0, 0))],
        out_specs=[rev(1024), rev(1024), rev(2048), rev(DN_H), rev(DN_H)],
        out_shape=[SDS((rows, 1024), F32), SDS((rows, 1024), F32), SDS((rows, 2048), F32), SDS((rows, DN_H), F32),
                   SDS((rows, DN_H), F32)],
        scratch_shapes=[pltpu.VMEM((DN_H, DK, DK), F32), pltpu.VMEM((DN_H, CH), F32)],
        compiler_params=_cp("arbitrary"), name="dn_scan_bwd")(
            do, qn, kn, sv, gc, beta, at, pt, u, w, vn, qet, wt, ks, st)


def dn_prep_bwd(udn, conv_w, a_log, dt_bias, dqn, dkn, dv, dbeta, dg):
    rows = udn.shape[0]
    nch = rows // CH
    ext = CH + 8

    def body(xc_ref, xp_ref, ba_ref, xn_ref, dqn_n, dkn_n, dv_n, cw_ref, al_ref, dtb_ref, dqn_ref, dkn_ref, dv_ref,
             dbeta_ref, dg_ref, dx_ref, dba_ref, dcw_ref, dal_ref, ddtb_ref):
        c = pl.program_id(0)
        first = c == 0
        own = (lax.broadcasted_iota(jnp.int32, (ext, 1), 0) < CH) | (c < nch - 1)

        @pl.when(first)
        def _():
            dcw_ref[...] = jnp.zeros_like(dcw_ref)
            dal_ref[...] = jnp.zeros_like(dal_ref)
            ddtb_ref[...] = jnp.zeros_like(ddtb_ref)

        real, xa, beta, g = _gates(ba_ref[...], al_ref[...], dtb_ref[...], c)
        dgm = jnp.where(real, dg_ref[...], 0.0)
        da = dgm * (-jnp.exp(al_ref[...])) * jax.nn.sigmoid(xa)
        dal_ref[...] += jnp.sum(dgm * g, axis=0, keepdims=True)
        ddtb_ref[...] += jnp.sum(da, axis=0, keepdims=True)
        dba_ref[...] = jnp.zeros_like(dba_ref)
        dba_ref[:, 0:DN_H] = jnp.where(real, dbeta_ref[...] * beta * (1.0 - beta), 0.0)
        dba_ref[:, DN_H:2 * DN_H] = da

        def through_conv(off, g_cur, g_next, grad_fn):
            sl = pl.ds(off, DK)
            xx = jnp.concatenate([jnp.where(first, 0.0, xp_ref[:, sl]), xc_ref[:, sl], xn_ref[:, sl]], axis=0)
            y = cw_ref[0:1, sl] * xx[5:5 + ext]
            for j in range(1, 4):
                y += cw_ref[j:j + 1, sl] * xx[5 + j:5 + j + ext]
            dy = jnp.where(own, grad_fn(_silu(y), jnp.concatenate([g_cur, g_next], axis=0)) * _dsilu(y), 0.0)
            dx = cw_ref[0:1, sl] * dy[3:3 + CH]
            for j in range(1, 4):
                dx += cw_ref[j:j + 1, sl] * dy[3 - j:3 - j + CH]
            dx_ref[:, sl] = dx.astype(BF16)
            for j in range(4):
                dcw_ref[j:j + 1, sl] += jnp.sum(dy[:CH] * xx[5 + j:5 + j + CH], axis=0, keepdims=True)

        def l2_bwd(scale):
            def f(s, gin):
                r = lax.rsqrt(jnp.sum(s * s, axis=-1, keepdims=True) + EPS)
                nrm = s * r
                return (r * scale) * (gin - nrm * jnp.sum(nrm * gin, axis=-1, keepdims=True))
            return f

        def qk_body(kh, carry):
            sl = pl.ds(pl.multiple_of(kh * DK, DK), DK)
            through_conv(pl.multiple_of(kh * DK, DK), dqn_ref[:, sl], dqn_n[:, sl], l2_bwd(DK ** -0.5))
            through_conv(pl.multiple_of(1024 + kh * DK, DK), dkn_ref[:, sl], dkn_n[:, sl], l2_bwd(1.0))
            return carry

        lax.fori_loop(0, DN_KH, qk_body, 0)

        def v_body(hv, carry):
            sl = pl.ds(pl.multiple_of(hv * DK, DK), DK)
            through_conv(pl.multiple_of(2048 + hv * DK, DK), dv_ref[:, sl], dv_n[:, sl], lambda s, gin: gin)
            return carry

        lax.fori_loop(0, DN_H, v_body, 0)

    full = lambda shape: pl.BlockSpec(shape, lambda c: (0, 0))
    blk = lambda w: pl.BlockSpec((CH, w), lambda c: (c, 0))
    nxt = lambda w: pl.BlockSpec((8, w), lambda c: (jnp.minimum(8 * c + 8, rows // 8 - 1), 0))
    return pl.pallas_call(
        body, grid=(nch,),
        in_specs=_chunk_specs(0) + [nxt(4096), nxt(1024), nxt(1024), nxt(2048), full((4, 4096)), full((1, DN_H)),
                                    full((1, DN_H)), blk(1024), blk(1024), blk(2048), blk(DN_H), blk(DN_H)],
        out_specs=[blk(4096), blk(DK), full((8, 4096)), full((1, DN_H)), full((1, DN_H))],
        out_shape=[SDS((rows, 4096), BF16), SDS((rows, DK), F32), SDS((8, 4096), F32), SDS((1, DN_H), F32),
                   SDS((1, DN_H), F32)],
        compiler_params=_cp("arbitrary"), name="dn_prep_bwd")(
            udn, udn, udn, udn, dqn, dkn, dv, conv_w, a_log, dt_bias, dqn, dkn, dv, dbeta, dg)


def local_step(x, target, w):
    seq = x.shape[0]
    bf = lambda a: a.astype(BF16)
    h0 = jnp.concatenate([jnp.zeros((PAD, D_MODEL), F32), w["meta_tokens"], x], axis=0)
    tgt = jnp.concatenate([jnp.zeros((BLK, D_MODEL), F32), target], axis=0)
    win = w["attn_w_in"]
    wq, wkv, wg = win[:, :1024], win[:, 1024:1280], win[:, 1280:]
    wa_in = bf(jnp.concatenate([wq, wg, wkv], axis=1))
    wa_out = bf(w["attn_w_out"])
    wd_in = jnp.concatenate([bf(w["dn_w_in"]), jnp.zeros((D_MODEL, 96), BF16)], axis=1)
    wd_out = bf(w["dn_w_out"])
    qw, kw, sinks = w["attn_q_norm_w"], w["attn_k_norm_w"], w["attn_sinks"]
    cw, al, dtb, ow = w["dn_conv_w"], w["dn_a_log"], w["dn_dt_bias"], w["dn_o_norm_w"]

    ua, xn0 = norm_matmul(h0, w["attn_norm_w"], wa_in, 2304, "attn_in")
    og = attn_fwd(ua, qw, kw, sinks)
    h1 = matmul_residual(og, wa_out, h0, "attn_out")
    ud, xn1 = norm_matmul(h1, w["dn_norm_w"], wd_in, 896, "dn_in")
    qn, kn, sv, gc, beta, u, wy, qe, ks, p, at, pt, qet, wt, kst = dn_prep(ud, cw, al, dtb)
    o, vn, st = dn_scan(u, wy, qe, kst, p, gc)
    dh2, on, ls = dn_out_fwd(o, ud, ow, wd_out, h1, tgt)
    loss = (0.5 / D_MODEL) * jnp.sum(ls)

    do, dz, dow = dn_out_bwd(dh2, wd_out, o, ud, ow)
    g_dn_out = wgrad(on, dh2, "dn_out_wgrad")
    dqn, dkn, dv, dbeta, dg = dn_scan_bwd(do, qn, kn, sv, gc, beta, at, pt, u, wy, vn, qet, wt, ks, st)
    dxc, dba, dcw, dal, ddtb = dn_prep_bwd(ud, cw, al, dtb, dqn, dkn, dv, dbeta, dg)
    dh1, dnw1 = in_proj_bwd([dxc, dz, dba], [wd_in[:, :4096], wd_in[:, 4096:6144], wd_in[:, 6144:]],
                            h1, w["dn_norm_w"], dh2, "dn_in_bwd")
    g_dn_in = jnp.concatenate([wgrad(xn1, dxc, "dn_in_wgrad_qkv"), wgrad(xn1, dz, "dn_in_wgrad_z"),
                               wgrad(xn1, dba, "dn_in_wgrad_ba")[:, :2 * DN_H]], axis=1)

    dog = matmul_nt(dh1, wa_out, "attn_out_bwd")
    g_attn_out = wgrad(og, dh1, "attn_out_wgrad")
    dq, dgate, dkv, dkvm, dqw, dkw, dsk = attn_bwd(ua, qw, kw, sinks, dog)
    dkv = dkv.at[PAD:BLK].add(dkvm)
    dh0, dnw0 = in_proj_bwd([dq, dgate, dkv], [wa_in[:, :1024], wa_in[:, 1024:2048], wa_in[:, 2048:]],
                            h0, w["attn_norm_w"], dh1, "attn_in_bwd")
    g_attn_in = jnp.concatenate([wgrad(xn0, dq, "attn_in_wgrad_q"), wgrad(xn0, dkv, "attn_in_wgrad_kv"),
                                 wgrad(xn0, dgate, "attn_in_wgrad_g")], axis=1)
    grads = {
        "meta_tokens": dh0[PAD:BLK], "attn_norm_w": jnp.sum(dnw0, axis=0), "attn_w_in": g_attn_in,
        "attn_q_norm_w": dqw, "attn_k_norm_w": dkw, "attn_sinks": dsk, "attn_w_out": g_attn_out,
        "dn_norm_w": jnp.sum(dnw1, axis=0), "dn_w_in": g_dn_in, "dn_conv_w": dcw[:4], "dn_a_log": dal,
        "dn_dt_bias": ddtb, "dn_o_norm_w": jnp.sum(dow, axis=0), "dn_w_out": g_dn_out,
    }
    return loss, dh0[BLK:BLK + seq], grads


WEIGHTS = ["meta_tokens", "attn_norm_w", "attn_w_in", "attn_q_norm_w", "attn_k_norm_w", "attn_sinks", "attn_w_out",
           "dn_norm_w", "dn_w_in", "dn_conv_w", "dn_a_log", "dn_dt_bias", "dn_o_norm_w", "dn_w_out"]
SHARDED = {"attn_w_in": ((1024, 2304), 1), "attn_w_out": ((1024, 1024), 0), "dn_w_in": ((1024, 6176), 1),
           "dn_w_out": ((2048, 1024), 0), "dn_conv_w": ((4, 4096), 1), "meta_tokens": ((16, 1024), 1),
           "dn_norm_w": ((1, 1024), 1)}
REPLICATED = {"attn_norm_w": 1024, "attn_q_norm_w": 64, "attn_k_norm_w": 64, "attn_sinks": 16, "dn_a_log": 16,
              "dn_dt_bias": 16, "dn_o_norm_w": 128}
N_CHIPS = 4
PACK_ROWS = 2912
HALF_ROWS = PACK_ROWS // 2
SMALL_ROWS = 8


def _shard_shape(name):
    (r, c), axis = SHARDED[name]
    return (r // N_CHIPS, c) if axis == 0 else (r, c // N_CHIPS)


def _pack(parts, rows):
    flat = jnp.concatenate([p.reshape(-1) for p in parts])
    return jnp.pad(flat, (0, rows * 1024 - flat.shape[0])).reshape(rows, 1024)


def pack_shard(shards):
    return _pack([shards[n] for n in SHARDED], PACK_ROWS)


def unpack_shard(buf):
    flat, out, pos = buf.reshape(-1), {}, 0
    for n in SHARDED:
        shp = _shard_shape(n)
        size = shp[0] * shp[1]
        out[n] = flat[pos:pos + size].reshape(shp)
        pos += size
    return out


MATRICES = ("attn_w_in", "attn_w_out", "dn_w_in", "dn_w_out")


def pack_gather(shards):
    big = [shards[n].astype(BF16).reshape(-1) for n in MATRICES]
    small = jnp.concatenate([shards[n].reshape(-1) for n in SHARDED if n not in MATRICES])
    flat = jnp.concatenate(big + [lax.bitcast_convert_type(small, BF16).reshape(-1)])
    return jnp.pad(flat, (0, PACK_ROWS * 1024 - flat.shape[0])).reshape(PACK_ROWS, 1024)


def unpack_gather(buf):
    PER_F32 = 4 // jnp.dtype(buf.dtype).itemsize
    flat, out, pos = buf.reshape(-1), {}, 0
    for n in MATRICES:
        shp = _shard_shape(n)
        out[n] = flat[pos:pos + shp[0] * shp[1]].reshape(shp)
        pos += shp[0] * shp[1]
    for n in SHARDED:
        if n not in MATRICES:
            shp = _shard_shape(n)
            raw = flat[pos:pos + shp[0] * shp[1] * PER_F32]
            out[n] = lax.bitcast_convert_type(raw.reshape(-1, PER_F32) if PER_F32 > 1 else raw, F32).reshape(shp)
            pos += shp[0] * shp[1] * PER_F32
    return out


def pack_small(vals):
    return _pack([vals[n] for n in REPLICATED], SMALL_ROWS)


def unpack_small(buf):
    flat, out, pos = buf.reshape(-1), {}, 0
    for n, size in REPLICATED.items():
        out[n] = flat[pos:pos + size].reshape(1, size)
        pos += size
    return out


ANY = pl.BlockSpec(memory_space=pl.ANY)


def _place():
    return lax.axis_index("x"), lax.axis_index("y"), lax.axis_index("c")


def chips_exchange(src, gather):
    r = src.shape[-2]

    def body(s_ref, o_ref, send_sems, recv_sems):
        x, y, c = _place()
        me = 2 * x + y
        peers = [(1 - x, y), (x, 1 - y), (1 - x, 1 - y)]

        def copy(k, to_block, from_block):
            px, py = peers[k]
            return pltpu.make_async_remote_copy(
                src_ref=s_ref if gather else s_ref.at[to_block], dst_ref=o_ref.at[from_block],
                send_sem=send_sems.at[k], recv_sem=recv_sems.at[k], device_id=(px, py, c), device_id_type=MESH)

        sends = [copy(k, 2 * px + py, me) for k, (px, py) in enumerate(peers)]
        for cp in sends:
            cp.start()
        for k, (px, py) in enumerate(peers):
            copy(k, me, 2 * px + py).wait_recv()
        for cp in sends:
            cp.wait_send()

    return pl.pallas_call(
        body, in_specs=[ANY], out_specs=ANY, out_shape=SDS((N_CHIPS, r, 1024), src.dtype),
        scratch_shapes=[pltpu.SemaphoreType.DMA((3,)), pltpu.SemaphoreType.DMA((3,))],
        name="chips_gather" if gather else "chips_exchange")(src)


def chip_sum(received, pair, me):
    tm = 208

    def body(me_ref, own_ref, r1_ref, r2_ref, r3_ref, o_ref):
        o_ref[...] = ((own_ref[0] + r1_ref[0]) + r2_ref[0]) + r3_ref[0]

    blk = lambda k: pl.BlockSpec((1, tm, 1024), lambda i, me_ref: ((me_ref[0] + k) % N_CHIPS, i, 0))
    return pl.pallas_call(
        body,
        grid_spec=pltpu.PrefetchScalarGridSpec(
            num_scalar_prefetch=1, grid=(HALF_ROWS // tm,), in_specs=[blk(0), blk(1), blk(2), blk(3)],
            out_specs=pl.BlockSpec((tm, 1024), lambda i, me_ref: (i, 0))),
        out_shape=SDS((HALF_ROWS, 1024), F32), compiler_params=_cp("parallel"), name="chip_sum")(
            me.reshape(1).astype(jnp.int32), pair, received, received, received)


def _rows_at(ref, start, size):
    return ref.at[:, pl.ds(start, size), :] if len(ref.shape) == 3 else ref.at[pl.ds(start, size), :]


def sibling_join(src, name):
    h = src.shape[-2]
    axis = len(src.shape) - 2

    def body(s_ref, o_ref, send_sem, recv_sem):
        x, y, c = _place()
        cp = pltpu.make_async_remote_copy(src_ref=s_ref, dst_ref=o_ref, send_sem=send_sem, recv_sem=recv_sem,
                                          device_id=(x, y, 1 - c), device_id_type=MESH)
        cp.start()
        cp.wait()

    theirs = pl.pallas_call(
        body, in_specs=[ANY], out_specs=ANY, out_shape=SDS(src.shape, src.dtype),
        scratch_shapes=[pltpu.SemaphoreType.DMA, pltpu.SemaphoreType.DMA], name=name)(src)
    c = lax.axis_index("c")
    both = jnp.zeros(src.shape[:-2] + (2 * h, src.shape[-1]), src.dtype)
    both = lax.dynamic_update_slice_in_dim(both, src, c * h, axis)
    return lax.dynamic_update_slice_in_dim(both, theirs, (1 - c) * h, axis)


def sibling_give(g_all):
    def body(s_ref, o_ref, send_sem, recv_sem):
        x, y, c = _place()
        cp = pltpu.make_async_remote_copy(
            src_ref=_rows_at(s_ref, (1 - c) * HALF_ROWS, HALF_ROWS), dst_ref=o_ref, send_sem=send_sem,
            recv_sem=recv_sem, device_id=(x, y, 1 - c), device_id_type=MESH)
        cp.start()
        cp.wait()

    return pl.pallas_call(
        body, in_specs=[ANY], out_specs=ANY, out_shape=SDS((N_CHIPS, HALF_ROWS, 1024), F32),
        scratch_shapes=[pltpu.SemaphoreType.DMA, pltpu.SemaphoreType.DMA], name="pair_exchange")(g_all)


def pair_sum(g_all, got, c):
    tm = 208
    per_half = HALF_ROWS // tm

    def body(c_ref, a_ref, b_ref, o_ref):
        o_ref[...] = a_ref[...] + b_ref[...]

    return pl.pallas_call(
        body,
        grid_spec=pltpu.PrefetchScalarGridSpec(
            num_scalar_prefetch=1, grid=(N_CHIPS, per_half),
            in_specs=[pl.BlockSpec((1, tm, 1024), lambda j, i, c_ref: (j, c_ref[0] * per_half + i, 0)),
                      pl.BlockSpec((1, tm, 1024), lambda j, i, c_ref: (j, i, 0))],
            out_specs=pl.BlockSpec((1, tm, 1024), lambda j, i, c_ref: (j, i, 0))),
        out_shape=SDS((N_CHIPS, HALF_ROWS, 1024), F32),
        compiler_params=_cp("parallel", "parallel"), name="pair_sum")(c.reshape(1).astype(jnp.int32), g_all, got)


def all_gather_small(src):
    def body(s_ref, o_ref, send_sems, recv_sems, local_sem):
        x, y, c = _place()
        flips = [(fx, fy, fc) for fx in (0, 1) for fy in (0, 1) for fc in (0, 1)][1:]
        idx = lambda px, py, pc: 4 * px + 2 * py + pc
        mine = pltpu.make_async_copy(s_ref, o_ref.at[idx(x, y, c)], local_sem)
        mine.start()

        def peer(k):
            fx, fy, fc = flips[k]
            return (1 - x if fx else x, 1 - y if fy else y, 1 - c if fc else c)

        def copy(k, block):
            return pltpu.make_async_remote_copy(
                src_ref=s_ref, dst_ref=o_ref.at[block], send_sem=send_sems.at[k], recv_sem=recv_sems.at[k],
                device_id=peer(k), device_id_type=MESH)

        sends = [copy(k, idx(x, y, c)) for k in range(7)]
        for cp in sends:
            cp.start()
        for k in range(7):
            copy(k, idx(*peer(k))).wait_recv()
        for cp in sends:
            cp.wait_send()
        mine.wait()

    return pl.pallas_call(
        body, in_specs=[ANY], out_specs=ANY, out_shape=SDS((8,) + src.shape, F32),
        scratch_shapes=[pltpu.SemaphoreType.DMA((7,)), pltpu.SemaphoreType.DMA((7,)), pltpu.SemaphoreType.DMA],
        name="all_gather_small")(src)


def sum_blocks(t, name):
    n, r, _ = t.shape
    tm = 208 if r % 208 == 0 else r

    def body(t_ref, o_ref):
        acc = t_ref[0]
        for i in range(1, n):
            acc = acc + t_ref[i]
        o_ref[...] = acc

    return pl.pallas_call(
        body, grid=(r // tm,), in_specs=[pl.BlockSpec((n, tm, 1024), lambda i: (0, i, 0))],
        out_specs=pl.BlockSpec((tm, 1024), lambda i: (i, 0)), out_shape=SDS((r, 1024), F32),
        compiler_params=_cp("parallel"), name=name)(t)


def adamw(w, g, m, v, name):
    rows = w.shape[0]
    tm = 416 if rows % 416 == 0 else rows

    def body(w_ref, g_ref, m_ref, v_ref, d_ref, nm_ref, nv_ref):
        g_ = g_ref[...]
        m_ = ADAM_B1 * m_ref[...] + (1.0 - ADAM_B1) * g_
        v_ = ADAM_B2 * v_ref[...] + (1.0 - ADAM_B2) * (g_ * g_)
        m_hat = m_ / (1.0 - ADAM_B1 ** ADAM_STEP)
        v_hat = v_ / (1.0 - ADAM_B2 ** ADAM_STEP)
        d_ref[...] = -ADAM_LR * (m_hat / (jnp.sqrt(v_hat) + ADAM_EPS) + ADAM_WD * w_ref[...])
        nm_ref[...] = m_
        nv_ref[...] = v_

    spec = pl.BlockSpec((tm, 1024), lambda i: (i, 0))
    return pl.pallas_call(
        body, grid=(rows // tm,), in_specs=[spec] * 4, out_specs=[spec] * 3,
        out_shape=[SDS((rows, 1024), F32)] * 3, compiler_params=_cp("parallel"), name=name)(w, g, m, v)


LAYERED = ("attn_w_in", "attn_w_out", "dn_w_in", "dn_conv_w", "dn_w_out")


def _two_d(name, a):
    return a[0] if name in LAYERED else a


def kernel(x, meta_tokens, attn_norm_w, attn_w_in, attn_q_norm_w, attn_k_norm_w, attn_sinks, attn_w_out, dn_norm_w, dn_w_in, dn_conv_w, dn_a_log, dn_dt_bias, dn_o_norm_w, dn_w_out, loss_target, m_meta_tokens, m_attn_norm_w, m_attn_w_in, m_attn_q_norm_w, m_attn_k_norm_w, m_attn_sinks, m_attn_w_out, m_dn_norm_w, m_dn_w_in, m_dn_conv_w, m_dn_a_log, m_dn_dt_bias, m_dn_o_norm_w, m_dn_w_out, v_meta_tokens, v_attn_norm_w, v_attn_w_in, v_attn_q_norm_w, v_attn_k_norm_w, v_attn_sinks, v_attn_w_out, v_dn_norm_w, v_dn_w_in, v_dn_conv_w, v_dn_a_log, v_dn_dt_bias, v_dn_o_norm_w, v_dn_w_out):
    given = dict(zip(WEIGHTS, (meta_tokens, attn_norm_w, attn_w_in, attn_q_norm_w, attn_k_norm_w, attn_sinks,
                               attn_w_out, dn_norm_w, dn_w_in, dn_conv_w, dn_a_log, dn_dt_bias, dn_o_norm_w, dn_w_out)))
    mom1 = dict(zip(WEIGHTS, (m_meta_tokens, m_attn_norm_w, m_attn_w_in, m_attn_q_norm_w, m_attn_k_norm_w,
                              m_attn_sinks, m_attn_w_out, m_dn_norm_w, m_dn_w_in, m_dn_conv_w, m_dn_a_log,
                              m_dn_dt_bias, m_dn_o_norm_w, m_dn_w_out)))
    mom2 = dict(zip(WEIGHTS, (v_meta_tokens, v_attn_norm_w, v_attn_w_in, v_attn_q_norm_w, v_attn_k_norm_w,
                              v_attn_sinks, v_attn_w_out, v_dn_norm_w, v_dn_w_in, v_dn_conv_w, v_dn_a_log,
                              v_dn_dt_bias, v_dn_o_norm_w, v_dn_w_out)))
    two_d = lambda d: {n: _two_d(n, a) for n, a in d.items()}
    given, mom1, mom2 = two_d(given), two_d(mom1), two_d(mom2)
    c = lax.axis_index("c")

    w_shard = pack_shard(given)
    me = 2 * lax.axis_index("x") + lax.axis_index("y")
    own_half = lax.dynamic_slice_in_dim(pack_gather(given), c * HALF_ROWS, HALF_ROWS, axis=0)
    mine = lax.dynamic_update_slice_in_dim(chips_exchange(own_half, True), own_half[None], me, 0)
    gathered = sibling_join(mine, "gather_swap")
    per_chip = [unpack_gather(gathered[j]) for j in range(N_CHIPS)]
    full = {n: jnp.concatenate([pc[n] for pc in per_chip], axis=SHARDED[n][1]) for n in SHARDED}
    full.update({n: given[n] for n in REPLICATED})

    loss, dx, grads = local_step(x[0], loss_target[0], full)

    split = lambda n: jnp.split(grads[n], N_CHIPS, axis=SHARDED[n][1])
    g_all = jnp.stack([pack_shard({n: split(n)[j] for n in SHARDED}) for j in range(N_CHIPS)])
    pair = pair_sum(g_all, sibling_give(g_all), c)
    half = chip_sum(chips_exchange(pair, False), pair, me)
    g_shard = sibling_join(half, "half_exchange")

    g_small = sum_blocks(all_gather_small(pack_small(grads)), "small_sum")

    big = adamw(w_shard, g_shard, pack_shard(mom1), pack_shard(mom2), "adamw")
    small = adamw(pack_small(given), g_small, pack_small(mom1), pack_small(mom2), "adamw_small")

    def unpack(big_buf, small_buf):
        out = unpack_shard(big_buf)
        out.update(unpack_small(small_buf))
        return [out[n][None] if n in LAYERED else out[n] for n in WEIGHTS]

    loss = lax.psum(loss, ("x", "y", "c"))
    return (loss, dx[None], *unpack(g_shard, g_small), *[o for b, s in zip(big, small) for o in unpack(b, s)])
```

```python
import functools

import jax
import jax.numpy as jnp
from jax import lax
from jax.experimental import pallas as pl
from jax.experimental.pallas import tpu as pltpu

F32 = jnp.float32
BF16 = jnp.bfloat16
SDS = jax.ShapeDtypeStruct
MESH = pl.DeviceIdType.MESH

D_MODEL = 1024
N_META = 16
EPS = 1e-6
BLK = 128
CH = 64
PAD = BLK - N_META
HEADS = 16
HD = 64
KVW = 256
DN_H = 16
DN_KH = 8
DK = 128
SLOPES = [2.0 ** (-8.0 * (h + 1) / HEADS) for h in range(HEADS)]
NEG = -1e30
NT = (((1,), (1,)), ((), ()))
TN = (((0,), (0,)), ((), ()))
HI = lax.Precision.HIGHEST

ADAM_LR, ADAM_B1, ADAM_B2, ADAM_EPS, ADAM_WD, ADAM_STEP = 0.001, 0.9, 0.999, 1e-08, 0.01, 10

VMEM_LIMIT = 56 * 1024 * 1024


def _cp(*sem):
    return pltpu.CompilerParams(dimension_semantics=sem, vmem_limit_bytes=VMEM_LIMIT)


def _row_tile(rows):
    for t in (384, 256, 128):
        if rows % t == 0:
            return t
    raise ValueError(rows)


def _dot(a, b, dims=None, precision=None):
    if dims is None:
        return jnp.dot(a, b, preferred_element_type=F32, precision=precision)
    return lax.dot_general(a, b, dims, preferred_element_type=F32, precision=precision)


def _silu(x):
    return x * jax.nn.sigmoid(x)


def _dsilu(x):
    s = jax.nn.sigmoid(x)
    return s * (1.0 + x * (1.0 - s))


def _rms(x):
    return lax.rsqrt(jnp.mean(x * x, axis=-1, keepdims=True) + EPS)


def norm_matmul(h, nw, w, tn, name):
    rows, k = h.shape
    n = w.shape[1]
    tm = _row_tile(rows)

    def norm_body(h_ref, nw_ref, xn_ref):
        x = h_ref[...]
        xn_ref[...] = (x * _rms(x) * nw_ref[...]).astype(BF16)

    xn = pl.pallas_call(
        norm_body, grid=(rows // tm,),
        in_specs=[pl.BlockSpec((tm, k), lambda i: (i, 0)), pl.BlockSpec((1, k), lambda i: (0, 0))],
        out_specs=pl.BlockSpec((tm, k), lambda i: (i, 0)), out_shape=SDS((rows, k), BF16),
        compiler_params=_cp("parallel"), name=name + "_norm")(h, nw)

    def body(a_ref, w_ref, o_ref):
        o_ref[...] = _dot(a_ref[...], w_ref[...])

    out = pl.pallas_call(
        body, grid=(n // tn, rows // tm),
        in_specs=[pl.BlockSpec((tm, k), lambda j, i: (i, 0)), pl.BlockSpec((k, tn), lambda j, i: (0, j))],
        out_specs=pl.BlockSpec((tm, tn), lambda j, i: (i, j)), out_shape=SDS((rows, n), F32),
        compiler_params=_cp("parallel", "parallel"), name=name)(xn, w)
    return out, xn


def matmul_residual(a, w, res, name):
    rows, k = a.shape
    n = w.shape[1]
    tm = _row_tile(rows)

    def body(a_ref, w_ref, r_ref, o_ref):
        o_ref[...] = r_ref[...] + _dot(a_ref[...], w_ref[...])

    return pl.pallas_call(
        body, grid=(rows // tm,),
        in_specs=[pl.BlockSpec((tm, k), lambda i: (i, 0)), pl.BlockSpec((k, n), lambda i: (0, 0)),
                  pl.BlockSpec((tm, n), lambda i: (i, 0))],
        out_specs=pl.BlockSpec((tm, n), lambda i: (i, 0)),
        out_shape=SDS((rows, n), F32), compiler_params=_cp("parallel"), name=name)(a, w, res)


def wgrad(a, b, name):
    rows, k = a.shape
    n = b.shape[1]
    tm = _row_tile(rows)
    tn = min(n, 1024)

    def body(a_ref, b_ref, o_ref):
        @pl.when(pl.program_id(1) == 0)
        def _():
            o_ref[...] = jnp.zeros_like(o_ref)

        o_ref[...] += _dot(a_ref[...], b_ref[...].astype(BF16), TN)

    return pl.pallas_call(
        body, grid=(n // tn, rows // tm),
        in_specs=[pl.BlockSpec((tm, k), lambda j, i: (i, 0)), pl.BlockSpec((tm, tn), lambda j, i: (i, j))],
        out_specs=pl.BlockSpec((k, tn), lambda j, i: (0, j)),
        out_shape=SDS((k, n), F32), compiler_params=_cp("parallel", "arbitrary"), name=name)(a, b)


def in_proj_bwd(dus, ws, h, nw, dh_next, name):
    rows, k = h.shape
    tm = _row_tile(rows)
    nd = len(dus)
    nt = rows // tm

    def body(*refs):
        du_refs, w_refs = refs[:nd], refs[nd:2 * nd]
        h_ref, nw_ref, dhn_ref, dh_ref, dnw_ref = refs[2 * nd:]
        dxn = _dot(du_refs[0][...].astype(BF16), w_refs[0][...], NT)
        for du_ref, w_ref in zip(du_refs[1:], w_refs[1:]):
            dxn += _dot(du_ref[...].astype(BF16), w_ref[...], NT)
        x = h_ref[...]
        r = _rms(x)
        y = x * r
        gy = dxn * nw_ref[...]
        dh_ref[...] = dhn_ref[...] + r * (gy - y * jnp.mean(y * gy, axis=-1, keepdims=True))
        dnw_ref[0] = jnp.sum(dxn * y, axis=0, keepdims=True)

    in_specs = [pl.BlockSpec((tm, du.shape[1]), lambda i: (i, 0)) for du in dus]
    in_specs += [pl.BlockSpec(w.shape, lambda i: (0, 0)) for w in ws]
    in_specs += [pl.BlockSpec((tm, k), lambda i: (i, 0)), pl.BlockSpec((1, k), lambda i: (0, 0)),
                 pl.BlockSpec((tm, k), lambda i: (i, 0))]
    return pl.pallas_call(
        body, grid=(nt,), in_specs=in_specs,
        out_specs=[pl.BlockSpec((tm, k), lambda i: (i, 0)), pl.BlockSpec((1, 1, k), lambda i: (i, 0, 0))],
        out_shape=[SDS((rows, k), F32), SDS((nt, 1, k), F32)],
        compiler_params=_cp("parallel"), name=name)(*dus, *ws, h, nw, dh_next)


def matmul_nt(a, w, name):
    rows, k = a.shape
    n = w.shape[0]
    tm = _row_tile(rows)

    def body(a_ref, w_ref, o_ref):
        o_ref[...] = _dot(a_ref[...].astype(BF16), w_ref[...], NT)

    return pl.pallas_call(
        body, grid=(rows // tm,),
        in_specs=[pl.BlockSpec((tm, k), lambda i: (i, 0)), pl.BlockSpec((n, k), lambda i: (0, 0))],
        out_specs=pl.BlockSpec((tm, n), lambda i: (i, 0)),
        out_shape=SDS((rows, n), F32), compiler_params=_cp("parallel"), name=name)(a, w)


SUB = 64
GRP = 8
TR = GRP * SUB
NBAND = 192
TK = 256


def _tile_bias(n, sb):
    r = lax.broadcasted_iota(jnp.int32, (TR, TK), 0)
    c = lax.broadcasted_iota(jnp.int32, (TR, TK), 1)
    qi = r & (SUB - 1)
    d = BLK + qi - c
    dm = n * BLK + SUB * sb - PAD + NBAND + qi - c
    band = c < NBAND
    valid = (band & (d >= 0) & (d < BLK) & (c >= 2 * BLK - BLK * n - SUB * sb)) | (
        (c >= NBAND) & (c < NBAND + N_META) & (dm >= 0))
    return valid, jnp.where(band, d, jnp.minimum(dm, BLK)).astype(F32)


def _group_col(vals):
    g = lax.broadcasted_iota(jnp.int32, (TR, 1), 0) >> 6
    col = jnp.zeros((TR, 1), F32)
    for gi, v in enumerate(vals):
        col = jnp.where(g == gi, v, col)
    return col


def _stack_heads(ref, sb, kvh):
    return jnp.concatenate(
        [ref[SUB * sb:SUB * sb + SUB, HD * (GRP * kvh + g):HD * (GRP * kvh + g) + HD] for g in range(GRP)], axis=0)


def _unstack_heads(parts):
    return jnp.concatenate([parts[kvh][SUB * g:SUB * g + SUB] for kvh in range(2) for g in range(GRP)], axis=1)


def _tile_keys(band, meta, sb):
    return jnp.concatenate([band[SUB * sb:SUB * sb + NBAND], meta,
                            jnp.zeros((TK - NBAND - N_META, HD), band.dtype)], axis=0)


def _row_sums(x):
    ones = jnp.ones((x.shape[1], 128), BF16)
    hi = x.astype(BF16)
    lo = (x - hi.astype(F32)).astype(BF16)
    return _dot(hi, ones) + _dot(lo, ones)


def _rms_stack(q):
    return lax.rsqrt(_row_sums(q * q)[:, :HD] * (1.0 / HD) + EPS)


def _fill_bias(bias_scr, n):
    @pl.when(n <= 2)
    def _():
        for sb in range(2):
            valid, dist = _tile_bias(n, sb)
            for kvh in range(2):
                slope_col = _group_col([SLOPES[GRP * kvh + g] for g in range(GRP)])
                bias_scr[2 * sb + kvh] = jnp.where(valid, -slope_col * dist, NEG)


def _tile_vals(band, meta, sb):
    return jnp.concatenate([_tile_keys(band, meta, sb), jnp.ones((TK, 3 * HD), BF16)], axis=1)


def _tile_softmax(qn16, k16, vx16, bias, sink_col):
    s = _dot(qn16, k16, NT) * (HD ** -0.5) + bias
    mx = jnp.maximum(jnp.max(s.astype(BF16), axis=-1, keepdims=True).astype(F32), sink_col)
    e = jnp.exp(s - mx)
    es = jnp.exp(sink_col - mx)
    ox = _dot(e.astype(BF16), vx16)
    return e, 1.0 / (ox[:, 2 * HD:] + es), es, ox[:, :HD]


def _kv_heads(kvb, kvm, kw_):
    out = []
    for kvh in range(2):
        kb, km = kvb[:, HD * kvh:HD * kvh + HD], kvm[:, HD * kvh:HD * kvh + HD]
        out.append(((kb * _rms(kb) * kw_).astype(BF16), (km * _rms(km) * kw_).astype(BF16),
                    kvb[:, BLK + HD * kvh:BLK + HD * kvh + HD].astype(BF16),
                    kvm[:, BLK + HD * kvh:BLK + HD * kvh + HD].astype(BF16)))
    return out


def _kv_specs(nblk, clamp):
    cur = (lambda n: (jnp.minimum(n, nblk - 1), 8)) if clamp else (lambda n: (n, 8))
    return [pl.BlockSpec((BLK, KVW), cur),
            pl.BlockSpec((BLK, KVW), lambda n: (jnp.maximum(n - 1, 0), 8)),
            pl.BlockSpec((N_META, KVW), lambda n: (PAD // N_META, 8))]


def _sink_cols(sinks):
    return jnp.repeat(sinks.reshape(2, GRP), SUB, axis=1).reshape(2, TR, 1)


SINK_SPEC = pl.BlockSpec((2, TR, 1), lambda n: (0, 0, 0))


def attn_fwd(u, qw, kw, sinks):
    rows = u.shape[0]
    nblk = rows // BLK

    def body(q_ref, g_ref, kvc_ref, kvp_ref, kvm_ref, qw_ref, kw_ref, sc_ref, og_ref, bias_scr):
        _fill_bias(bias_scr, pl.program_id(0))
        qw_ = qw_ref[...]
        kv = _kv_heads(jnp.concatenate([kvp_ref[...], kvc_ref[...]], axis=0), kvm_ref[...], kw_ref[...])
        for sb in range(2):
            parts = []
            for kvh in range(2):
                knb, knm, vb, vm = kv[kvh]
                q = _stack_heads(q_ref, sb, kvh)
                qn16 = (q * _rms_stack(q) * qw_).astype(BF16)
                _, inv, _, o = _tile_softmax(qn16, _tile_keys(knb, knm, sb), _tile_vals(vb, vm, sb),
                                             bias_scr[2 * sb + kvh], sc_ref[kvh])
                parts.append(o * inv[:, :HD])
            rows = slice(SUB * sb, SUB * sb + SUB)
            og_ref[rows, :] = (_unstack_heads(parts) * _silu(g_ref[rows, :])).astype(BF16)

    small = lambda w: pl.BlockSpec((1, w), lambda n: (0, 0))
    return pl.pallas_call(
        body, grid=(nblk,),
        in_specs=[pl.BlockSpec((BLK, 1024), lambda n: (n, 0)), pl.BlockSpec((BLK, 1024), lambda n: (n, 1))]
        + _kv_specs(nblk, False) + [small(HD), small(HD), SINK_SPEC],
        out_specs=pl.BlockSpec((BLK, 1024), lambda n: (n, 0)),
        out_shape=SDS((rows, 1024), BF16), scratch_shapes=[pltpu.VMEM((4, TR, TK), F32)],
        compiler_params=_cp("arbitrary"), name="attn_fwd")(u, u, u, u, u, qw, kw, _sink_cols(sinks))


def attn_bwd(u, qw, kw, sinks, dog):
    rows = u.shape[0]
    nblk = rows // BLK

    def knorm_bwd(k, dkn, kw_):
        r = _rms(k)
        y = k * r
        gy = dkn * kw_
        return r * (gy - y * jnp.mean(y * gy, axis=-1, keepdims=True)), jnp.sum(dkn * y, axis=0, keepdims=True)

    def body(q_ref, g_ref, dog_ref, kvc_ref, kvp_ref, kvm_ref, qw_ref, kw_ref, sc_ref,
             dq_ref, dg_ref, dkv_ref, dkvm_ref, dqw_ref, dkw_ref, dsk_ref, carry, prevp, curp, metap, bias_scr):
        n = pl.program_id(0)
        qw_, kw_ = qw_ref[...], kw_ref[...]
        _fill_bias(bias_scr, n)

        @pl.when(n == 0)
        def _():
            carry[...] = jnp.zeros_like(carry)
            metap[...] = jnp.zeros_like(metap)
            dqw_ref[...] = jnp.zeros_like(dqw_ref)
            dkw_ref[...] = jnp.zeros_like(dkw_ref)
            dsk_ref[...] = jnp.zeros_like(dsk_ref)

        @pl.when(n == nblk)
        def _():
            prevp[...] = jnp.zeros_like(prevp)
            curp[...] = jnp.zeros_like(curp)

        @pl.when(n < nblk)
        def _():
            kv = _kv_heads(jnp.concatenate([kvp_ref[...], kvc_ref[...]], axis=0), kvm_ref[...], kw_)
            lane = lax.broadcasted_iota(jnp.int32, (1, HEADS), 1)
            dqw = jnp.zeros((1, HD), F32)
            dsk = jnp.zeros((1, HEADS), F32)
            band_parts = [jnp.zeros((2 * BLK, HD), F32) for _ in range(4)]
            meta_parts = [jnp.zeros((N_META, HD), F32) for _ in range(4)]

            def widen(x, sb):
                z = jnp.zeros((2 * BLK - NBAND, HD), F32)
                return jnp.concatenate([x, z] if sb == 0 else [z, x], axis=0)

            for sb in range(2):
                rows = slice(SUB * sb, SUB * sb + SUB)
                dq_parts, dg_parts = [], []
                for kvh in range(2):
                    knb, knm, vb, vm = kv[kvh]
                    k16, v16 = _tile_keys(knb, knm, sb), _tile_keys(vb, vm, sb)
                    q = _stack_heads(q_ref, sb, kvh)
                    r = _rms_stack(q)
                    y = q * r
                    qn16 = (y * qw_).astype(BF16)
                    e, inv, es, o = _tile_softmax(qn16, k16, _tile_vals(vb, vm, sb), bias_scr[2 * sb + kvh],
                                                  sc_ref[kvh])
                    p = e * jnp.concatenate([inv, inv], axis=1)
                    p16 = p.astype(BF16)
                    o = o * inv[:, :HD]
                    gate = _stack_heads(g_ref, sb, kvh)
                    dog_ = _stack_heads(dog_ref, sb, kvh)
                    dg_parts.append(dog_ * o * _dsilu(gate))
                    do_ = dog_ * _silu(gate)
                    do16 = do_.astype(BF16)
                    dp = _dot(do16, v16, NT)
                    delta = _row_sums(do_ * o)
                    ds16 = (p * (dp - jnp.concatenate([delta, delta], axis=1))).astype(BF16)
                    dsink = -(es * inv) * delta
                    for g in range(GRP):
                        dsk += jnp.where(lane == GRP * kvh + g,
                                         jnp.sum(dsink[SUB * g:SUB * g + SUB, :HEADS], axis=0, keepdims=True), 0.0)
                    dqn = _dot(ds16, k16) * (HD ** -0.5)
                    dk = (_dot((y * qw_).T.astype(BF16), ds16) * (HD ** -0.5)).T
                    dv = _dot(do_.T.astype(BF16), p16).T
                    band_parts[kvh] += widen(dk[:NBAND], sb)
                    band_parts[2 + kvh] += widen(dv[:NBAND], sb)
                    meta_parts[kvh] += dk[NBAND:NBAND + N_META]
                    meta_parts[2 + kvh] += dv[NBAND:NBAND + N_META]
                    gy = dqn * qw_
                    dq_parts.append(r * (gy - y * (_row_sums(y * gy)[:, :HD] * (1.0 / HD))))
                    dqw += jnp.sum(dqn * y, axis=0, keepdims=True)
                dq_ref[rows, :] = _unstack_heads(dq_parts).astype(BF16)
                dg_ref[rows, :] = _unstack_heads(dg_parts).astype(BF16)
            band = jnp.concatenate(band_parts, axis=1)
            prevp[...] = band[:BLK]
            curp[...] = band[BLK:]
            metap[...] += jnp.concatenate(meta_parts, axis=1)
            dqw_ref[...] += dqw
            dsk_ref[...] += dsk

        tot = carry[...] + prevp[...]
        kprev = kvp_ref[...]
        dk0, w0 = knorm_bwd(kprev[:, 0:HD], tot[:, 0:HD], kw_)
        dk1, w1 = knorm_bwd(kprev[:, HD:2 * HD], tot[:, HD:2 * HD], kw_)
        dkv_ref[...] = jnp.concatenate([dk0, dk1, tot[:, 2 * HD:]], axis=1)
        dkw_ref[...] += w0 + w1
        carry[...] = curp[...]

        @pl.when(n == nblk)
        def _():
            mt = metap[...]
            km = kvm_ref[...]
            m0, v0 = knorm_bwd(km[:, 0:HD], mt[:, 0:HD], kw_)
            m1, v1 = knorm_bwd(km[:, HD:2 * HD], mt[:, HD:2 * HD], kw_)
            dkvm_ref[...] = jnp.concatenate([m0, m1, mt[:, 2 * HD:]], axis=1)
            dkw_ref[...] += v0 + v1

    small = lambda w: pl.BlockSpec((1, w), lambda n: (0, 0))
    cl = lambda n: jnp.minimum(n, nblk - 1)
    return pl.pallas_call(
        body, grid=(nblk + 1,),
        in_specs=[pl.BlockSpec((BLK, 1024), lambda n: (cl(n), 0)), pl.BlockSpec((BLK, 1024), lambda n: (cl(n), 1)),
                  pl.BlockSpec((BLK, 1024), lambda n: (cl(n), 0))]
        + _kv_specs(nblk, True) + [small(HD), small(HD), SINK_SPEC],
        out_specs=[pl.BlockSpec((BLK, 1024), lambda n: (cl(n), 0)), pl.BlockSpec((BLK, 1024), lambda n: (cl(n), 0)),
                   pl.BlockSpec((BLK, KVW), lambda n: (jnp.maximum(n - 1, 0), 0)),
                   pl.BlockSpec((N_META, KVW), lambda n: (0, 0)), small(HD), small(HD), small(HEADS)],
        out_shape=[SDS((rows, 1024), BF16), SDS((rows, 1024), BF16), SDS((rows, KVW), F32), SDS((N_META, KVW), F32),
                   SDS((1, HD), F32), SDS((1, HD), F32), SDS((1, HEADS), F32)],
        scratch_shapes=[pltpu.VMEM((BLK, KVW), F32), pltpu.VMEM((BLK, KVW), F32), pltpu.VMEM((BLK, KVW), F32),
                        pltpu.VMEM((N_META, KVW), F32), pltpu.VMEM((4, TR, TK), F32)],
        compiler_params=_cp("arbitrary"), name="attn_bwd")(u, u, dog, u, u, u, qw, kw, _sink_cols(sinks))


HB = 16


def _bdot(a, b, kind, split=False):
    dims = {"nn": ((2,), (1,)), "nt": ((2,), (2,)), "tn": ((1,), (1,))}[kind]
    dg = lambda p, q: lax.dot_general(p, q, (dims, ((0,), (0,))), preferred_element_type=F32)
    if not split:
        return dg(a, b)
    ah, bh = a.astype(BF16), b.astype(BF16)
    al, bl = (a - ah.astype(F32)).astype(BF16), (b - bh.astype(F32)).astype(BF16)
    return (dg(ah, bl) + dg(al, bh)) + dg(ah, bh)


def _head_cols(hv, beta, gc, gct, lane):
    sel = lane == hv
    return _pick(beta, sel), _pick(gc, sel), gct[pl.ds(hv, 1), :]


def _conv_group(xc_ref, xp_ref, cw_ref, off, first):
    xp = jnp.where(first, 0.0, xp_ref[:, pl.ds(off, DK)])
    xx = jnp.concatenate([xp, xc_ref[:, pl.ds(off, DK)]], axis=0)
    y = cw_ref[0:1, pl.ds(off, DK)] * xx[5:5 + CH]
    for j in range(1, 4):
        y += cw_ref[j:j + 1, pl.ds(off, DK)] * xx[5 + j:5 + j + CH]
    return xx, y


def _gates(ba, al, dtb, c):
    row = c * CH + lax.broadcasted_iota(jnp.int32, (CH, DN_H), 0)
    real = row >= PAD
    xa = ba[:, DN_H:2 * DN_H] + dtb
    beta = jnp.where(real, jax.nn.sigmoid(ba[:, 0:DN_H]), 0.0)
    g = jnp.where(real, -jnp.exp(al) * jax.nn.softplus(xa), 0.0)
    return real, xa, beta, g


def _pick(x, sel):
    return jnp.sum(jnp.where(sel, x, 0.0), axis=1, keepdims=True)


def _chunk_specs(width_blocks):
    return [pl.BlockSpec((CH, 4096), lambda c: (c, 0)),
            pl.BlockSpec((8, 4096), lambda c: (jnp.maximum(8 * c - 1, 0), 0)),
            pl.BlockSpec((CH, DK), lambda c: (c, 48))]


def _tri_inv(m, ii, jj):
    eye = (ii == jj).astype(BF16)
    mh, ml = _split(m)
    blk8 = (ii >> 3) == (jj >> 3)
    mb = (jnp.where(blk8, mh, 0), jnp.where(blk8, ml, 0))
    m2 = _split(_dot3(mb, mb))
    m4 = _split(_dot3(m2, m2))
    x = _dot3(_split(_dot3((eye - mb[0], -mb[1]), (eye + m2[0], m2[1]))), (eye + m4[0], m4[1]))
    for sh in (3, 4, 5):
        off = ((ii >> (sh + 1)) == (jj >> (sh + 1))) & ((ii >> sh) != (jj >> sh))
        xs = _split(x)
        x = x - _dot3(_split(_dot3(xs, (jnp.where(off, mh, 0), jnp.where(off, ml, 0)))), xs)
    return x


def _split(x):
    hi = x.astype(BF16)
    return hi, (x - hi.astype(F32)).astype(BF16)


def _dot3(a, b):
    dg = lambda p, q: lax.dot_general(p, q, ((((2,), (1,))), ((0,), (0,))), preferred_element_type=F32)
    return (dg(a[0], b[1]) + dg(a[1], b[0])) + dg(a[0], b[0])


def dn_prep(udn, conv_w, a_log, dt_bias):
    rows = udn.shape[0]
    nch = rows // CH

    def body(xc_ref, xp_ref, ba_ref, cw_ref, al_ref, dtb_ref,
             qn_ref, kn_ref, sv_ref, gc_ref, beta_ref, u_ref, w_ref, qe_ref, ks_ref, p_ref, at_ref, pt_ref,
             qet_ref, wt_ref, kst_ref, gct):
        c = pl.program_id(0)
        first = c == 0
        _, _, beta, g = _gates(ba_ref[...], al_ref[...], dtb_ref[...], c)
        ii = lax.broadcasted_iota(jnp.int32, (CH, CH), 0)
        jj = lax.broadcasted_iota(jnp.int32, (CH, CH), 1)
        gc = _dot((ii >= jj).astype(F32), g, precision=HI)
        gc_ref[...] = gc
        beta_ref[...] = beta
        gct[...] = gc.T

        def qk_body(kh, carry):
            off = pl.multiple_of(kh * DK, DK)
            _, yq = _conv_group(xc_ref, xp_ref, cw_ref, off, first)
            sq = _silu(yq)
            qn_ref[:, pl.ds(off, DK)] = sq * lax.rsqrt(jnp.sum(sq * sq, axis=-1, keepdims=True) + EPS) * (DK ** -0.5)
            _, yk = _conv_group(xc_ref, xp_ref, cw_ref, pl.multiple_of(1024 + kh * DK, DK), first)
            sk = _silu(yk)
            kn_ref[:, pl.ds(off, DK)] = sk * lax.rsqrt(jnp.sum(sk * sk, axis=-1, keepdims=True) + EPS)
            return carry

        lax.fori_loop(0, DN_KH, qk_body, 0)
        lane = lax.broadcasted_iota(jnp.int32, (CH, DN_H), 1)
        zpad = jnp.zeros((CH, DK - CH), F32)

        def v_group(grp, carry):
            offs, ks_, qs_, vs_, cols = [], [], [], [], []
            for i in range(HB):
                hv = grp * HB + i
                offs.append(pl.multiple_of(hv * DK, DK))
                koff = pl.multiple_of((grp * (HB // 2) + i // 2) * DK, DK)
                _, yv = _conv_group(xc_ref, xp_ref, cw_ref, pl.multiple_of(2048 + hv * DK, DK), first)
                vs_.append(_silu(yv))
                sv_ref[:, pl.ds(offs[i], DK)] = vs_[i]
                ks_.append(kn_ref[:, pl.ds(koff, DK)])
                qs_.append(qn_ref[:, pl.ds(koff, DK)])
                cols.append(_head_cols(hv, beta, gc, gct, lane))
            k, q, v = jnp.stack(ks_), jnp.stack(qs_), jnp.stack(vs_)
            beta_c, gc_c, gc_r = (jnp.stack([c_[j] for c_ in cols]) for j in range(3))
            dec = jnp.exp(jnp.where(ii >= jj, gc_c - gc_r, NEG))
            eg = jnp.exp(gc_c)
            kb = k * beta_c
            k16 = k.astype(BF16)
            m = jnp.where(ii > jj, _bdot(kb.astype(BF16), k16, "nt") * dec, 0.0)
            a = _tri_inv(m, ii, jj)
            uw = _bdot(a, jnp.concatenate([v * beta_c, kb * eg], axis=2), "nn", True)
            p = _bdot(q.astype(BF16), k16, "nt") * dec
            qe = q * eg
            ksx = k * jnp.exp(gc_c[:, CH - 1:CH, :] - gc_c)
            tslot = lambda x: jnp.concatenate([x.T, jnp.zeros((DK, DK - CH), F32)], axis=1).astype(BF16)
            for i in range(HB):
                sl = pl.ds(offs[i], DK)
                u_ref[:, sl] = uw[i, :, :DK]
                w_ref[:, sl] = uw[i, :, DK:]
                qe_ref[:, sl] = qe[i].astype(BF16)
                ks_ref[:, sl] = ksx[i].astype(BF16)
                p_ref[:, sl] = jnp.concatenate([p[i], zpad], axis=1).astype(BF16)
                at_ref[:, sl] = jnp.concatenate([a[i].T, zpad], axis=1)
                pt_ref[:, sl] = jnp.concatenate([p[i].T, zpad], axis=1).astype(BF16)
                qet_ref[:, sl] = tslot(qe[i])
                wt_ref[:, sl] = tslot(uw[i, :, DK:])
                kst_ref[:, sl] = tslot(ksx[i])
            return carry

        lax.fori_loop(0, DN_H // HB, v_group, 0)

    full = lambda shape: pl.BlockSpec(shape, lambda c: (0, 0))
    blk = lambda w: pl.BlockSpec((CH, w), lambda c: (c, 0))
    return pl.pallas_call(
        body, grid=(nch,),
        in_specs=_chunk_specs(0) + [full((4, 4096)), full((1, DN_H)), full((1, DN_H))],
        out_specs=[blk(1024), blk(1024), blk(2048), blk(DN_H), blk(DN_H), blk(2048), blk(2048), blk(2048), blk(2048),
                   blk(2048), blk(2048), blk(2048)] + [pl.BlockSpec((DK, 2048), lambda c: (c, 0))] * 3,
        out_shape=[SDS((rows, 1024), F32), SDS((rows, 1024), F32), SDS((rows, 2048), F32), SDS((rows, DN_H), F32),
                   SDS((rows, DN_H), F32), SDS((rows, 2048), F32), SDS((rows, 2048), F32), SDS((rows, 2048), BF16),
                   SDS((rows, 2048), BF16), SDS((rows, 2048), BF16), SDS((rows, 2048), F32),
                   SDS((rows, 2048), BF16)] + [SDS((2 * rows, 2048), BF16)] * 3,
        scratch_shapes=[pltpu.VMEM((DN_H, CH), F32)],
        compiler_params=_cp("parallel"), name="dn_prep")(udn, udn, udn, conv_w, a_log, dt_bias)


def dn_scan(u, w, qe, kst, p, gc):
    rows = u.shape[0]
    nch = rows // CH

    def body(u_ref, w_ref, qe_ref, kst_ref, p_ref, gc_ref, o_ref, vn_ref, st_ref, s_scr):
        @pl.when(pl.program_id(0) == 0)
        def _():
            s_scr[...] = jnp.zeros_like(s_scr)

        gl_row = gc_ref[CH - 1:CH, :]
        lane = lax.broadcasted_iota(jnp.int32, (1, DN_H), 1)

        def group(grp, carry):
            base = grp * HB
            sls = [pl.ds(pl.multiple_of((base + i) * DK, DK), DK) for i in range(HB)]
            heads = lambda ref: jnp.stack([ref[:, sl] for sl in sls])
            s = s_scr[pl.ds(base, HB)]
            st_ref[0, pl.ds(base, HB)] = s
            s16 = s.astype(BF16)
            vn = heads(u_ref) - _bdot(heads(w_ref).astype(BF16), s16, "nn")
            vn16 = vn.astype(BF16)
            o = _bdot(heads(qe_ref), s16, "nn") + _bdot(heads(p_ref)[:, :, 0:CH], vn16, "nn")
            egl = jnp.exp(jnp.stack([_pick(gl_row, lane == base + i) for i in range(HB)]))
            s_scr[pl.ds(base, HB)] = s * egl + _bdot(heads(kst_ref)[:, :, 0:CH], vn16, "nn")
            for i in range(HB):
                vn_ref[:, sls[i]] = vn16[i]
                o_ref[:, sls[i]] = o[i]
            return carry

        lax.fori_loop(0, DN_H // HB, group, 0)

    blk = lambda wd: pl.BlockSpec((CH, wd), lambda c: (c, 0))
    return pl.pallas_call(
        body, grid=(nch,),
        in_specs=[blk(2048)] * 3 + [pl.BlockSpec((DK, 2048), lambda c: (c, 0)), blk(2048), blk(DN_H)],
        out_specs=[blk(2048), blk(2048), pl.BlockSpec((1, DN_H, DK, DK), lambda c: (c, 0, 0, 0))],
        out_shape=[SDS((rows, 2048), F32), SDS((rows, 2048), BF16), SDS((nch, DN_H, DK, DK), F32)],
        scratch_shapes=[pltpu.VMEM((DN_H, DK, DK), F32)],
        compiler_params=_cp("arbitrary"), name="dn_scan")(u, w, qe, kst, p, gc)


def dn_out_fwd(o, udn, ow, wout, h1, tgt):
    rows = o.shape[0]
    tm = _row_tile(rows)
    nt = rows // tm

    def body(o_ref, z_ref, ow_ref, w_ref, h_ref, t_ref, dh_ref, on_ref, ls_ref):
        for hv in range(DN_H):
            sl = slice(hv * DK, hv * DK + DK)
            oh = o_ref[:, sl]
            on_ref[:, sl] = (oh * _rms(oh) * ow_ref[...] * _silu(z_ref[:, sl])).astype(BF16)
        h2 = h_ref[...] + _dot(on_ref[...], w_ref[...])
        row = pl.program_id(0) * tm + lax.broadcasted_iota(jnp.int32, (tm, 1), 0)
        err = jnp.where(row >= BLK, h2 - t_ref[...], 0.0)
        dh_ref[...] = err * (1.0 / D_MODEL)
        ls_ref[0] = jnp.sum(err * err, axis=0, keepdims=True)

    return pl.pallas_call(
        body, grid=(nt,),
        in_specs=[pl.BlockSpec((tm, 2048), lambda i: (i, 0)), pl.BlockSpec((tm, 2048), lambda i: (i, 2)),
                  pl.BlockSpec((1, DK), lambda i: (0, 0)), pl.BlockSpec((2048, D_MODEL), lambda i: (0, 0)),
                  pl.BlockSpec((tm, D_MODEL), lambda i: (i, 0)), pl.BlockSpec((tm, D_MODEL), lambda i: (i, 0))],
        out_specs=[pl.BlockSpec((tm, D_MODEL), lambda i: (i, 0)), pl.BlockSpec((tm, 2048), lambda i: (i, 0)),
                   pl.BlockSpec((1, 1, D_MODEL), lambda i: (i, 0, 0))],
        out_shape=[SDS((rows, D_MODEL), F32), SDS((rows, 2048), BF16), SDS((nt, 1, D_MODEL), F32)],
        compiler_params=_cp("parallel"), name="dn_out_fwd")(o, udn, ow, wout, h1, tgt)


def dn_out_bwd(dh2, wout, o, udn, ow):
    rows = o.shape[0]
    tm = _row_tile(rows)
    nt = rows // tm

    def body(dh_ref, w_ref, o_ref, z_ref, ow_ref, do_ref, dz_ref, dow_ref):
        don = _dot(dh_ref[...].astype(BF16), w_ref[...], NT)
        ow_ = ow_ref[...]
        dow = jnp.zeros((1, DK), F32)
        for hv in range(DN_H):
            sl = slice(hv * DK, hv * DK + DK)
            oh = o_ref[:, sl]
            r = _rms(oh)
            y = oh * r
            z = z_ref[:, sl]
            dn = don[:, sl] * _silu(z)
            dz_ref[:, sl] = (don[:, sl] * (y * ow_) * _dsilu(z)).astype(BF16)
            dy = dn * ow_
            do_ref[:, sl] = r * (dy - y * jnp.mean(y * dy, axis=-1, keepdims=True))
            dow += jnp.sum(dn * y, axis=0, keepdims=True)
        dow_ref[0] = dow

    return pl.pallas_call(
        body, grid=(nt,),
        in_specs=[pl.BlockSpec((tm, D_MODEL), lambda i: (i, 0)), pl.BlockSpec((2048, D_MODEL), lambda i: (0, 0)),
                  pl.BlockSpec((tm, 2048), lambda i: (i, 0)), pl.BlockSpec((tm, 2048), lambda i: (i, 2)),
                  pl.BlockSpec((1, DK), lambda i: (0, 0))],
        out_specs=[pl.BlockSpec((tm, 2048), lambda i: (i, 0)), pl.BlockSpec((tm, 2048), lambda i: (i, 0)),
                   pl.BlockSpec((1, 1, DK), lambda i: (i, 0, 0))],
        out_shape=[SDS((rows, 2048), F32), SDS((rows, 2048), BF16), SDS((nt, 1, DK), F32)],
        compiler_params=_cp("parallel"), name="dn_out_bwd")(dh2, wout, o, udn, ow)


def dn_scan_bwd(do, qn, kn, sv, gc, beta, at, pt, u, w, vn, qet, wt, ks, st):
    rows = do.shape[0]
    nch = rows // CH

    def body(do_ref, q_ref, k_ref, v_ref, gc_ref, beta_ref, at_ref, pt_ref, u_ref, w_ref, vn_ref, qet_ref, wt_ref,
             ks_ref, st_ref, dq_ref, dk_ref, dv_ref, dbeta_ref, dg_ref, ds_scr, gct):
        @pl.when(pl.program_id(0) == 0)
        def _():
            ds_scr[...] = jnp.zeros_like(ds_scr)

        gc, beta = gc_ref[...], beta_ref[...]
        gct[...] = gc.T
        ii = lax.broadcasted_iota(jnp.int32, (CH, CH), 0)
        jj = lax.broadcasted_iota(jnp.int32, (CH, CH), 1)
        lane = lax.broadcasted_iota(jnp.int32, (CH, DN_H), 1)
        last = lax.broadcasted_iota(jnp.int32, (CH, 1), 0) == CH - 1

        def group(grp, carry):
            dbeta_acc, dgc_acc = carry
            base = grp * HB
            sls = [pl.ds(pl.multiple_of((base + i) * DK, DK), DK) for i in range(HB)]
            ksls = [pl.ds(pl.multiple_of((grp * (HB // 2) + j) * DK, DK), DK) for j in range(HB // 2)]
            heads = lambda ref: jnp.stack([ref[:, sl] for sl in sls])
            kheads = lambda ref: jnp.stack([ref[:, ksls[i // 2]] for i in range(HB)])
            cols = [_head_cols(base + i, beta, gc, gct, lane) for i in range(HB)]
            beta_c, gc_c, gc_r = (jnp.stack([c_[j] for c_ in cols]) for j in range(3))
            k, q, v = kheads(k_ref), kheads(q_ref), heads(v_ref)
            dec = jnp.exp(jnp.where(ii >= jj, gc_c - gc_r, NEG))
            eg = jnp.exp(gc_c)
            gl = gc_c[:, CH - 1:CH, :]
            e2 = jnp.exp(gl - gc_c)
            egl = jnp.exp(gl)
            k16, q16 = k.astype(BF16), q.astype(BF16)
            do16 = heads(do_ref).astype(BF16)
            s = st_ref[0, pl.ds(base, HB)]
            s16 = s.astype(BF16)
            dso = ds_scr[pl.ds(base, HB)]
            dso16 = dso.astype(BF16)
            wf, uf, vn16 = heads(w_ref), heads(u_ref), heads(vn_ref)
            kb = k * beta_c
            kb16 = kb.astype(BF16)
            pm = _bdot(q16, k16, "nt") * dec
            m = jnp.where(ii > jj, _bdot(kb16, k16, "nt") * dec, 0.0)
            dvn = _bdot(heads(pt_ref)[:, :, 0:CH], do16, "nn") + _bdot(heads(ks_ref), dso16, "nn")
            dvn16 = dvn.astype(BF16)
            ds_scr[pl.ds(base, HB)] = (egl * dso + _bdot(heads(qet_ref)[:, :, 0:CH], do16, "nn")
                                       - _bdot(heads(wt_ref)[:, :, 0:CH], dvn16, "nn"))
            dpm = jnp.where(ii >= jj, _bdot(do16, vn16, "nt"), 0.0)
            dqk16 = (dpm * dec).astype(BF16)
            dqe = _bdot(do16, s16, "nt")
            dq = eg * dqe + _bdot(dqk16, k16, "nn")
            dks = _bdot(vn16, dso16, "nt")
            dw = -_bdot(dvn16, s16, "nt")
            dbvk = _bdot(heads(at_ref)[:, :, 0:CH], jnp.concatenate([dvn, dw], axis=2), "nn", True)
            dbv, dbk = dbvk[:, :, :DK], dbvk[:, :, DK:]
            dm = jnp.where(ii > jj, -_bdot(dbvk, jnp.concatenate([uf, wf], axis=2), "nt", True), 0.0)
            g16 = (dm * dec).astype(BF16)
            dkb = _bdot(g16, k16, "nn")
            dk = (_bdot(dqk16, q16, "tn") + e2 * dks + _bdot(g16, kb16, "tn") + beta_c * (eg * dbk + dkb))
            e = dpm * pm + dm * m
            rsum = lambda x: jnp.sum(x, axis=2, keepdims=True)
            r_bk, r_qe, r_beta, r_ks = rsum(dbk * k), rsum(q * dqe), rsum(dbv * v + dkb * k), rsum(dks * k)
            t = r_ks * e2
            dgl = jnp.sum(t, axis=1, keepdims=True) + egl * rsum(jnp.sum(dso * s, axis=1, keepdims=True))
            deg = r_qe + beta_c * r_bk
            dgc = rsum(e) - t + deg * eg + jnp.where(last, dgl, 0.0)
            dgrow = -jnp.sum(e, axis=1, keepdims=True)
            dv = beta_c * dbv
            dbeta = r_beta + eg * r_bk
            for i in range(HB):
                dv_ref[:, sls[i]] = dv[i]
                sel = lane == base + i
                dbeta_acc = jnp.where(sel, dbeta[i], dbeta_acc)
                dgc_acc = jnp.where(sel, dgc[i], dgc_acc)
                gct[pl.ds(base + i, 1), :] = dgrow[i]
            for j in range(HB // 2):
                dq_ref[:, ksls[j]] = dq[2 * j] + dq[2 * j + 1]
                dk_ref[:, ksls[j]] = dk[2 * j] + dk[2 * j + 1]
            return dbeta_acc, dgc_acc

        zero = jnp.zeros((CH, DN_H), F32)
        dbeta_acc, dgc_acc = lax.fori_loop(0, DN_H // HB, group, (zero, zero))
        dbeta_ref[...] = dbeta_acc
        dg_ref[...] = _dot((ii <= jj).astype(F32), dgc_acc + gct[...].T, precision=HI)

    rev = lambda wd: pl.BlockSpec((CH, wd), lambda i: (nch - 1 - i, 0))
    rev_t = pl.BlockSpec((DK, 2048), lambda i: (nch - 1 - i, 0))
    return pl.pallas_call(
        body, grid=(nch,),
        in_specs=[rev(2048), rev(1024), rev(1024), rev(2048), rev(DN_H), rev(DN_H), rev(2048), rev(2048), rev(2048),
                  rev(2048), rev(2048), rev_t, rev_t, rev(2048),
                  pl.BlockSpec((1, DN_H, DK, DK), lambda i: (nch - 1 - i, 0, 0, 0))],
        out_specs=[rev(1024), rev(1024), rev(2048), rev(DN_H), rev(DN_H)],
        out_shape=[SDS((rows, 1024), F32), SDS((rows, 1024), F32), SDS((rows, 2048), F32), SDS((rows, DN_H), F32),
                   SDS((rows, DN_H), F32)],
        scratch_shapes=[pltpu.VMEM((DN_H, DK, DK), F32), pltpu.VMEM((DN_H, CH), F32)],
        compiler_params=_cp("arbitrary"), name="dn_scan_bwd")(
            do, qn, kn, sv, gc, beta, at, pt, u, w, vn, qet, wt, ks, st)


def dn_prep_bwd(udn, conv_w, a_log, dt_bias, dqn, dkn, dv, dbeta, dg):
    rows = udn.shape[0]
    nch = rows // CH
    ext = CH + 8

    def body(xc_ref, xp_ref, ba_ref, xn_ref, dqn_n, dkn_n, dv_n, cw_ref, al_ref, dtb_ref, dqn_ref, dkn_ref, dv_ref,
             dbeta_ref, dg_ref, dx_ref, dba_ref, dcw_ref, dal_ref, ddtb_ref):
        c = pl.program_id(0)
        first = c == 0
        own = (lax.broadcasted_iota(jnp.int32, (ext, 1), 0) < CH) | (c < nch - 1)

        @pl.when(first)
        def _():
            dcw_ref[...] = jnp.zeros_like(dcw_ref)
            dal_ref[...] = jnp.zeros_like(dal_ref)
            ddtb_ref[...] = jnp.zeros_like(ddtb_ref)

        real, xa, beta, g = _gates(ba_ref[...], al_ref[...], dtb_ref[...], c)
        dgm = jnp.where(real, dg_ref[...], 0.0)
        da = dgm * (-jnp.exp(al_ref[...])) * jax.nn.sigmoid(xa)
        dal_ref[...] += jnp.sum(dgm * g, axis=0, keepdims=True)
        ddtb_ref[...] += jnp.sum(da, axis=0, keepdims=True)
        dba_ref[...] = jnp.zeros_like(dba_ref)
        dba_ref[:, 0:DN_H] = jnp.where(real, dbeta_ref[...] * beta * (1.0 - beta), 0.0)
        dba_ref[:, DN_H:2 * DN_H] = da

        def through_conv(off, g_cur, g_next, grad_fn):
            sl = pl.ds(off, DK)
            xx = jnp.concatenate([jnp.where(first, 0.0, xp_ref[:, sl]), xc_ref[:, sl], xn_ref[:, sl]], axis=0)
            taps = [xx[5 + j:5 + j + ext] for j in range(4)]
            y = cw_ref[0:1, sl] * taps[0]
            for j in range(1, 4):
                y += cw_ref[j:j + 1, sl] * taps[j]
            sg = jax.nn.sigmoid(y)
            dsilu = sg * (1.0 + y * (1.0 - sg))
            dy = jnp.where(own, grad_fn(y * sg, jnp.concatenate([g_cur, g_next], axis=0)) * dsilu, 0.0)
            dx = cw_ref[0:1, sl] * dy[3:3 + CH]
            for j in range(1, 4):
                dx += cw_ref[j:j + 1, sl] * dy[3 - j:3 - j + CH]
            dx_ref[:, sl] = dx.astype(BF16)
            for j in range(4):
                dcw_ref[j:j + 1, sl] += jnp.sum(dy[:CH] * taps[j][:CH], axis=0, keepdims=True)

        def l2_bwd(scale):
            def f(s, gin):
                r = lax.rsqrt(jnp.sum(s * s, axis=-1, keepdims=True) + EPS)
                nrm = s * r
                return (r * scale) * (gin - nrm * jnp.sum(nrm * gin, axis=-1, keepdims=True))
            return f

        def qk_body(kh, carry):
            sl = pl.ds(pl.multiple_of(kh * DK, DK), DK)
            through_conv(pl.multiple_of(kh * DK, DK), dqn_ref[:, sl], dqn_n[:, sl], l2_bwd(DK ** -0.5))
            through_conv(pl.multiple_of(1024 + kh * DK, DK), dkn_ref[:, sl], dkn_n[:, sl], l2_bwd(1.0))
            return carry

        lax.fori_loop(0, DN_KH, qk_body, 0)

        def v_body(hv, carry):
            sl = pl.ds(pl.multiple_of(hv * DK, DK), DK)
            through_conv(pl.multiple_of(2048 + hv * DK, DK), dv_ref[:, sl], dv_n[:, sl], lambda s, gin: gin)
            return carry

        lax.fori_loop(0, DN_H, v_body, 0)

    full = lambda shape: pl.BlockSpec(shape, lambda c: (0, 0))
    blk = lambda w: pl.BlockSpec((CH, w), lambda c: (c, 0))
    nxt = lambda w: pl.BlockSpec((8, w), lambda c: (jnp.minimum(8 * c + 8, rows // 8 - 1), 0))
    return pl.pallas_call(
        body, grid=(nch,),
        in_specs=_chunk_specs(0) + [nxt(4096), nxt(1024), nxt(1024), nxt(2048), full((4, 4096)), full((1, DN_H)),
                                    full((1, DN_H)), blk(1024), blk(1024), blk(2048), blk(DN_H), blk(DN_H)],
        out_specs=[blk(4096), blk(DK), full((8, 4096)), full((1, DN_H)), full((1, DN_H))],
        out_shape=[SDS((rows, 4096), BF16), SDS((rows, DK), F32), SDS((8, 4096), F32), SDS((1, DN_H), F32),
                   SDS((1, DN_H), F32)],
        compiler_params=_cp("arbitrary"), name="dn_prep_bwd")(
            udn, udn, udn, udn, dqn, dkn, dv, conv_w, a_log, dt_bias, dqn, dkn, dv, dbeta, dg)


def local_step(x, target, w):
    seq = x.shape[0]
    bf = lambda a: a.astype(BF16)
    h0 = jnp.concatenate([jnp.zeros((PAD, D_MODEL), F32), w["meta_tokens"], x], axis=0)
    tgt = jnp.concatenate([jnp.zeros((BLK, D_MODEL), F32), target], axis=0)
    win = w["attn_w_in"]
    wq, wkv, wg = win[:, :1024], win[:, 1024:1280], win[:, 1280:]
    wa_in = bf(jnp.concatenate([wq, wg, wkv], axis=1))
    wa_out = bf(w["attn_w_out"])
    wd_in = jnp.concatenate([bf(w["dn_w_in"]), jnp.zeros((D_MODEL, 96), BF16)], axis=1)
    wd_out = bf(w["dn_w_out"])
    qw, kw, sinks = w["attn_q_norm_w"], w["attn_k_norm_w"], w["attn_sinks"]
    cw, al, dtb, ow = w["dn_conv_w"], w["dn_a_log"], w["dn_dt_bias"], w["dn_o_norm_w"]

    ua, xn0 = norm_matmul(h0, w["attn_norm_w"], wa_in, 2304, "attn_in")
    og = attn_fwd(ua, qw, kw, sinks)
    h1 = matmul_residual(og, wa_out, h0, "attn_out")
    ud, xn1 = norm_matmul(h1, w["dn_norm_w"], wd_in, 896, "dn_in")
    qn, kn, sv, gc, beta, u, wy, qe, ks, p, at, pt, qet, wt, kst = dn_prep(ud, cw, al, dtb)
    o, vn, st = dn_scan(u, wy, qe, kst, p, gc)
    dh2, on, ls = dn_out_fwd(o, ud, ow, wd_out, h1, tgt)
    loss = (0.5 / D_MODEL) * jnp.sum(ls)

    do, dz, dow = dn_out_bwd(dh2, wd_out, o, ud, ow)
    g_dn_out = wgrad(on, dh2, "dn_out_wgrad")
    dqn, dkn, dv, dbeta, dg = dn_scan_bwd(do, qn, kn, sv, gc, beta, at, pt, u, wy, vn, qet, wt, ks, st)
    dxc, dba, dcw, dal, ddtb = dn_prep_bwd(ud, cw, al, dtb, dqn, dkn, dv, dbeta, dg)
    dh1, dnw1 = in_proj_bwd([dxc, dz, dba], [wd_in[:, :4096], wd_in[:, 4096:6144], wd_in[:, 6144:]],
                            h1, w["dn_norm_w"], dh2, "dn_in_bwd")
    g_dn_in = jnp.concatenate([wgrad(xn1, dxc, "dn_in_wgrad_qkv"), wgrad(xn1, dz, "dn_in_wgrad_z"),
                               wgrad(xn1, dba, "dn_in_wgrad_ba")[:, :2 * DN_H]], axis=1)

    dog = matmul_nt(dh1, wa_out, "attn_out_bwd")
    g_attn_out = wgrad(og, dh1, "attn_out_wgrad")
    dq, dgate, dkv, dkvm, dqw, dkw, dsk = attn_bwd(ua, qw, kw, sinks, dog)
    dkv = dkv.at[PAD:BLK].add(dkvm)
    dh0, dnw0 = in_proj_bwd([dq, dgate, dkv], [wa_in[:, :1024], wa_in[:, 1024:2048], wa_in[:, 2048:]],
                            h0, w["attn_norm_w"], dh1, "attn_in_bwd")
    g_attn_in = jnp.concatenate([wgrad(xn0, dq, "attn_in_wgrad_q"), wgrad(xn0, dkv, "attn_in_wgrad_kv"),
                                 wgrad(xn0, dgate, "attn_in_wgrad_g")], axis=1)
    grads = {
        "meta_tokens": dh0[PAD:BLK], "attn_norm_w": jnp.sum(dnw0, axis=0), "attn_w_in": g_attn_in,
        "attn_q_norm_w": dqw, "attn_k_norm_w": dkw, "attn_sinks": dsk, "attn_w_out": g_attn_out,
        "dn_norm_w": jnp.sum(dnw1, axis=0), "dn_w_in": g_dn_in, "dn_conv_w": dcw[:4], "dn_a_log": dal,
        "dn_dt_bias": ddtb, "dn_o_norm_w": jnp.sum(dow, axis=0), "dn_w_out": g_dn_out,
    }
    return loss, dh0[BLK:BLK + seq], grads


WEIGHTS = ["meta_tokens", "attn_norm_w", "attn_w_in", "attn_q_norm_w", "attn_k_norm_w", "attn_sinks", "attn_w_out",
           "dn_norm_w", "dn_w_in", "dn_conv_w", "dn_a_log", "dn_dt_bias", "dn_o_norm_w", "dn_w_out"]
SHARDED = {"attn_w_in": ((1024, 2304), 1), "attn_w_out": ((1024, 1024), 0), "dn_w_in": ((1024, 6176), 1),
           "dn_w_out": ((2048, 1024), 0), "dn_conv_w": ((4, 4096), 1), "meta_tokens": ((16, 1024), 1),
           "dn_norm_w": ((1, 1024), 1)}
REPLICATED = {"attn_norm_w": 1024, "attn_q_norm_w": 64, "attn_k_norm_w": 64, "attn_sinks": 16, "dn_a_log": 16,
              "dn_dt_bias": 16, "dn_o_norm_w": 128}
N_CHIPS = 4
PACK_ROWS = 2912
HALF_ROWS = PACK_ROWS // 2
SMALL_ROWS = 8


def _shard_shape(name):
    (r, c), axis = SHARDED[name]
    return (r // N_CHIPS, c) if axis == 0 else (r, c // N_CHIPS)


def _pack(parts, rows):
    flat = jnp.concatenate([p.reshape(-1) for p in parts])
    return jnp.pad(flat, (0, rows * 1024 - flat.shape[0])).reshape(rows, 1024)


def pack_shard(shards):
    return _pack([shards[n] for n in SHARDED], PACK_ROWS)


def unpack_shard(buf):
    flat, out, pos = buf.reshape(-1), {}, 0
    for n in SHARDED:
        shp = _shard_shape(n)
        size = shp[0] * shp[1]
        out[n] = flat[pos:pos + size].reshape(shp)
        pos += size
    return out


MATRICES = ("attn_w_in", "attn_w_out", "dn_w_in", "dn_w_out")


def pack_gather(shards):
    big = [shards[n].astype(BF16).reshape(-1) for n in MATRICES]
    small = jnp.concatenate([shards[n].reshape(-1) for n in SHARDED if n not in MATRICES])
    flat = jnp.concatenate(big + [lax.bitcast_convert_type(small, BF16).reshape(-1)])
    return jnp.pad(flat, (0, PACK_ROWS * 1024 - flat.shape[0])).reshape(PACK_ROWS, 1024)


def unpack_gather(buf):
    PER_F32 = 4 // jnp.dtype(buf.dtype).itemsize
    flat, out, pos = buf.reshape(-1), {}, 0
    for n in MATRICES:
        shp = _shard_shape(n)
        out[n] = flat[pos:pos + shp[0] * shp[1]].reshape(shp)
        pos += shp[0] * shp[1]
    for n in SHARDED:
        if n not in MATRICES:
            shp = _shard_shape(n)
            raw = flat[pos:pos + shp[0] * shp[1] * PER_F32]
            out[n] = lax.bitcast_convert_type(raw.reshape(-1, PER_F32) if PER_F32 > 1 else raw, F32).reshape(shp)
            pos += shp[0] * shp[1] * PER_F32
    return out


def pack_small(vals):
    return _pack([vals[n] for n in REPLICATED], SMALL_ROWS)


def unpack_small(buf):
    flat, out, pos = buf.reshape(-1), {}, 0
    for n, size in REPLICATED.items():
        out[n] = flat[pos:pos + size].reshape(1, size)
        pos += size
    return out


ANY = pl.BlockSpec(memory_space=pl.ANY)


def _place():
    return lax.axis_index("x"), lax.axis_index("y"), lax.axis_index("c")


def chips_exchange(src, gather):
    r = src.shape[-2]

    def body(s_ref, o_ref, send_sems, recv_sems):
        x, y, c = _place()
        me = 2 * x + y
        peers = [(1 - x, y), (x, 1 - y), (1 - x, 1 - y)]

        def copy(k, to_block, from_block):
            px, py = peers[k]
            return pltpu.make_async_remote_copy(
                src_ref=s_ref if gather else s_ref.at[to_block], dst_ref=o_ref.at[from_block],
                send_sem=send_sems.at[k], recv_sem=recv_sems.at[k], device_id=(px, py, c), device_id_type=MESH)

        sends = [copy(k, 2 * px + py, me) for k, (px, py) in enumerate(peers)]
        for cp in sends:
            cp.start()
        for k, (px, py) in enumerate(peers):
            copy(k, me, 2 * px + py).wait_recv()
        for cp in sends:
            cp.wait_send()

    return pl.pallas_call(
        body, in_specs=[ANY], out_specs=ANY, out_shape=SDS((N_CHIPS, r, 1024), src.dtype),
        scratch_shapes=[pltpu.SemaphoreType.DMA((3,)), pltpu.SemaphoreType.DMA((3,))],
        name="chips_gather" if gather else "chips_exchange")(src)


def chip_sum(received, pair, me):
    tm = 208

    def body(me_ref, own_ref, r1_ref, r2_ref, r3_ref, o_ref):
        o_ref[...] = ((own_ref[0] + r1_ref[0]) + r2_ref[0]) + r3_ref[0]

    blk = lambda k: pl.BlockSpec((1, tm, 1024), lambda i, me_ref: ((me_ref[0] + k) % N_CHIPS, i, 0))
    return pl.pallas_call(
        body,
        grid_spec=pltpu.PrefetchScalarGridSpec(
            num_scalar_prefetch=1, grid=(HALF_ROWS // tm,), in_specs=[blk(0), blk(1), blk(2), blk(3)],
            out_specs=pl.BlockSpec((tm, 1024), lambda i, me_ref: (i, 0))),
        out_shape=SDS((HALF_ROWS, 1024), F32), compiler_params=_cp("parallel"), name="chip_sum")(
            me.reshape(1).astype(jnp.int32), pair, received, received, received)


def _rows_at(ref, start, size):
    return ref.at[:, pl.ds(start, size), :] if len(ref.shape) == 3 else ref.at[pl.ds(start, size), :]


def sibling_join(src, name):
    axis = len(src.shape) - 2

    def body(s_ref, o_ref, send_sem, recv_sem):
        x, y, c = _place()
        cp = pltpu.make_async_remote_copy(src_ref=s_ref, dst_ref=o_ref, send_sem=send_sem, recv_sem=recv_sem,
                                          device_id=(x, y, 1 - c), device_id_type=MESH)
        cp.start()
        cp.wait()

    theirs = pl.pallas_call(
        body, in_specs=[ANY], out_specs=ANY, out_shape=SDS(src.shape, src.dtype),
        scratch_shapes=[pltpu.SemaphoreType.DMA, pltpu.SemaphoreType.DMA], name=name)(src)
    first = lax.axis_index("c") == 0
    return jnp.concatenate([jnp.where(first, src, theirs), jnp.where(first, theirs, src)], axis=axis)


def sibling_give(g_all):
    def body(s_ref, o_ref, send_sem, recv_sem):
        x, y, c = _place()
        cp = pltpu.make_async_remote_copy(
            src_ref=_rows_at(s_ref, (1 - c) * HALF_ROWS, HALF_ROWS), dst_ref=o_ref, send_sem=send_sem,
            recv_sem=recv_sem, device_id=(x, y, 1 - c), device_id_type=MESH)
        cp.start()
        cp.wait()

    return pl.pallas_call(
        body, in_specs=[ANY], out_specs=ANY, out_shape=SDS((N_CHIPS, HALF_ROWS, 1024), F32),
        scratch_shapes=[pltpu.SemaphoreType.DMA, pltpu.SemaphoreType.DMA], name="pair_exchange")(g_all)


def pair_sum(g_all, got, c):
    tm = 208
    per_half = HALF_ROWS // tm

    def body(c_ref, a_ref, b_ref, o_ref):
        o_ref[...] = a_ref[...] + b_ref[...]

    return pl.pallas_call(
        body,
        grid_spec=pltpu.PrefetchScalarGridSpec(
            num_scalar_prefetch=1, grid=(N_CHIPS, per_half),
            in_specs=[pl.BlockSpec((1, tm, 1024), lambda j, i, c_ref: (j, c_ref[0] * per_half + i, 0)),
                      pl.BlockSpec((1, tm, 1024), lambda j, i, c_ref: (j, i, 0))],
            out_specs=pl.BlockSpec((1, tm, 1024), lambda j, i, c_ref: (j, i, 0))),
        out_shape=SDS((N_CHIPS, HALF_ROWS, 1024), F32),
        compiler_params=_cp("parallel", "parallel"), name="pair_sum")(c.reshape(1).astype(jnp.int32), g_all, got)


def all_gather_small(src):
    def body(s_ref, o_ref, send_sems, recv_sems, local_sem):
        x, y, c = _place()
        flips = [(fx, fy, fc) for fx in (0, 1) for fy in (0, 1) for fc in (0, 1)][1:]
        idx = lambda px, py, pc: 4 * px + 2 * py + pc
        mine = pltpu.make_async_copy(s_ref, o_ref.at[idx(x, y, c)], local_sem)
        mine.start()

        def peer(k):
            fx, fy, fc = flips[k]
            return (1 - x if fx else x, 1 - y if fy else y, 1 - c if fc else c)

        def copy(k, block):
            return pltpu.make_async_remote_copy(
                src_ref=s_ref, dst_ref=o_ref.at[block], send_sem=send_sems.at[k], recv_sem=recv_sems.at[k],
                device_id=peer(k), device_id_type=MESH)

        sends = [copy(k, idx(x, y, c)) for k in range(7)]
        for cp in sends:
            cp.start()
        for k in range(7):
            copy(k, idx(*peer(k))).wait_recv()
        for cp in sends:
            cp.wait_send()
        mine.wait()

    return pl.pallas_call(
        body, in_specs=[ANY], out_specs=ANY, out_shape=SDS((8,) + src.shape, F32),
        scratch_shapes=[pltpu.SemaphoreType.DMA((7,)), pltpu.SemaphoreType.DMA((7,)), pltpu.SemaphoreType.DMA],
        name="all_gather_small")(src)


def sum_blocks(t, name):
    n, r, _ = t.shape
    tm = 208 if r % 208 == 0 else r

    def body(t_ref, o_ref):
        acc = t_ref[0]
        for i in range(1, n):
            acc = acc + t_ref[i]
        o_ref[...] = acc

    return pl.pallas_call(
        body, grid=(r // tm,), in_specs=[pl.BlockSpec((n, tm, 1024), lambda i: (0, i, 0))],
        out_specs=pl.BlockSpec((tm, 1024), lambda i: (i, 0)), out_shape=SDS((r, 1024), F32),
        compiler_params=_cp("parallel"), name=name)(t)


ADAM_BLOCK_BYTES = 1024 * 1024


def adamw(w, g, m, v, name):
    rows, cols = w.shape
    tm = rows
    while tm * cols * 4 > ADAM_BLOCK_BYTES and tm % 16 == 0:
        tm //= 2

    def body(w_ref, g_ref, m_ref, v_ref, d_ref, nm_ref, nv_ref):
        g_ = g_ref[...]
        m_ = ADAM_B1 * m_ref[...] + (1.0 - ADAM_B1) * g_
        v_ = ADAM_B2 * v_ref[...] + (1.0 - ADAM_B2) * (g_ * g_)
        m_hat = m_ / (1.0 - ADAM_B1 ** ADAM_STEP)
        v_hat = v_ / (1.0 - ADAM_B2 ** ADAM_STEP)
        d_ref[...] = -ADAM_LR * (m_hat / (jnp.sqrt(v_hat) + ADAM_EPS) + ADAM_WD * w_ref[...])
        nm_ref[...] = m_
        nv_ref[...] = v_

    spec = pl.BlockSpec((tm, cols), lambda i: (i, 0))
    return pl.pallas_call(
        body, grid=(rows // tm,), in_specs=[spec] * 4, out_specs=[spec] * 3,
        out_shape=[SDS((rows, cols), F32)] * 3, compiler_params=_cp("parallel"), name=name)(w, g, m, v)


LAYERED = ("attn_w_in", "attn_w_out", "dn_w_in", "dn_conv_w", "dn_w_out")


def _two_d(name, a):
    return a[0] if name in LAYERED else a


def kernel(x, meta_tokens, attn_norm_w, attn_w_in, attn_q_norm_w, attn_k_norm_w, attn_sinks, attn_w_out, dn_norm_w, dn_w_in, dn_conv_w, dn_a_log, dn_dt_bias, dn_o_norm_w, dn_w_out, loss_target, m_meta_tokens, m_attn_norm_w, m_attn_w_in, m_attn_q_norm_w, m_attn_k_norm_w, m_attn_sinks, m_attn_w_out, m_dn_norm_w, m_dn_w_in, m_dn_conv_w, m_dn_a_log, m_dn_dt_bias, m_dn_o_norm_w, m_dn_w_out, v_meta_tokens, v_attn_norm_w, v_attn_w_in, v_attn_q_norm_w, v_attn_k_norm_w, v_attn_sinks, v_attn_w_out, v_dn_norm_w, v_dn_w_in, v_dn_conv_w, v_dn_a_log, v_dn_dt_bias, v_dn_o_norm_w, v_dn_w_out):
    given = dict(zip(WEIGHTS, (meta_tokens, attn_norm_w, attn_w_in, attn_q_norm_w, attn_k_norm_w, attn_sinks,
                               attn_w_out, dn_norm_w, dn_w_in, dn_conv_w, dn_a_log, dn_dt_bias, dn_o_norm_w, dn_w_out)))
    mom1 = dict(zip(WEIGHTS, (m_meta_tokens, m_attn_norm_w, m_attn_w_in, m_attn_q_norm_w, m_attn_k_norm_w,
                              m_attn_sinks, m_attn_w_out, m_dn_norm_w, m_dn_w_in, m_dn_conv_w, m_dn_a_log,
                              m_dn_dt_bias, m_dn_o_norm_w, m_dn_w_out)))
    mom2 = dict(zip(WEIGHTS, (v_meta_tokens, v_attn_norm_w, v_attn_w_in, v_attn_q_norm_w, v_attn_k_norm_w,
                              v_attn_sinks, v_attn_w_out, v_dn_norm_w, v_dn_w_in, v_dn_conv_w, v_dn_a_log,
                              v_dn_dt_bias, v_dn_o_norm_w, v_dn_w_out)))
    two_d = lambda d: {n: _two_d(n, a) for n, a in d.items()}
    given, mom1, mom2 = two_d(given), two_d(mom1), two_d(mom2)
    c = lax.axis_index("c")

    me = 2 * lax.axis_index("x") + lax.axis_index("y")
    own_half = lax.dynamic_slice_in_dim(pack_gather(given), c * HALF_ROWS, HALF_ROWS, axis=0)
    mine = lax.dynamic_update_slice_in_dim(chips_exchange(own_half, True), own_half[None], me, 0)
    gathered = sibling_join(mine, "gather_swap")
    per_chip = [unpack_gather(gathered[j]) for j in range(N_CHIPS)]
    full = {n: jnp.concatenate([pc[n] for pc in per_chip], axis=SHARDED[n][1]) for n in SHARDED}
    full.update({n: given[n] for n in REPLICATED})

    loss, dx, grads = local_step(x[0], loss_target[0], full)

    split = lambda n: jnp.split(grads[n], N_CHIPS, axis=SHARDED[n][1])
    g_all = jnp.stack([pack_shard({n: split(n)[j] for n in SHARDED}) for j in range(N_CHIPS)])
    pair = pair_sum(g_all, sibling_give(g_all), c)
    half = chip_sum(chips_exchange(pair, False), pair, me)
    g_shard = sibling_join(half, "half_exchange")

    g_small = sum_blocks(all_gather_small(pack_small(grads)), "small_sum")

    g_local = unpack_shard(g_shard)
    g_local.update(unpack_small(g_small))
    steps = {n: adamw(given[n], g_local[n], mom1[n], mom2[n], "adamw_" + n) for n in WEIGHTS}
    shaped = lambda n, a: a[None] if n in LAYERED else a
    outs = [[shaped(n, g_local[n]) for n in WEIGHTS]]
    outs += [[shaped(n, steps[n][k]) for n in WEIGHTS] for k in range(3)]

    loss = lax.psum(loss, ("x", "y", "c"))
    return (loss, dx[None], *outs[0], *outs[1], *outs[2], *outs[3])
```

```python
import functools

import jax
import jax.numpy as jnp
from jax import lax
from jax.experimental import pallas as pl
from jax.experimental.pallas import tpu as pltpu

F32 = jnp.float32
BF16 = jnp.bfloat16
SDS = jax.ShapeDtypeStruct
MESH = pl.DeviceIdType.MESH

D_MODEL = 1024
N_META = 16
EPS = 1e-6
BLK = 128
CH = 64
PAD = BLK - N_META
HEADS = 16
HD = 64
KVW = 256
DN_H = 16
DN_KH = 8
DK = 128
SLOPES = [2.0 ** (-8.0 * (h + 1) / HEADS) for h in range(HEADS)]
NEG = -1e30
NT = (((1,), (1,)), ((), ()))
TN = (((0,), (0,)), ((), ()))
HI = lax.Precision.HIGHEST

ADAM_LR, ADAM_B1, ADAM_B2, ADAM_EPS, ADAM_WD, ADAM_STEP = 0.001, 0.9, 0.999, 1e-08, 0.01, 10

VMEM_LIMIT = 56 * 1024 * 1024


def _cp(*sem):
    return pltpu.CompilerParams(dimension_semantics=sem, vmem_limit_bytes=VMEM_LIMIT)


def _row_tile(rows):
    for t in (384, 256, 128):
        if rows % t == 0:
            return t
    raise ValueError(rows)


def _dot(a, b, dims=None, precision=None):
    if dims is None:
        return jnp.dot(a, b, preferred_element_type=F32, precision=precision)
    return lax.dot_general(a, b, dims, preferred_element_type=F32, precision=precision)


def _silu(x):
    return x * jax.nn.sigmoid(x)


def _dsilu(x):
    s = jax.nn.sigmoid(x)
    return s * (1.0 + x * (1.0 - s))


def _rms(x):
    return lax.rsqrt(jnp.mean(x * x, axis=-1, keepdims=True) + EPS)


def norm_matmul(h, nw, w, tn, name):
    rows, k = h.shape
    n = w.shape[1]
    tm = _row_tile(rows)

    def norm_body(h_ref, nw_ref, xn_ref):
        x = h_ref[...]
        xn_ref[...] = (x * _rms(x) * nw_ref[...]).astype(BF16)

    xn = pl.pallas_call(
        norm_body, grid=(rows // tm,),
        in_specs=[pl.BlockSpec((tm, k), lambda i: (i, 0)), pl.BlockSpec((1, k), lambda i: (0, 0))],
        out_specs=pl.BlockSpec((tm, k), lambda i: (i, 0)), out_shape=SDS((rows, k), BF16),
        compiler_params=_cp("parallel"), name=name + "_norm")(h, nw)

    def body(a_ref, w_ref, o_ref):
        o_ref[...] = _dot(a_ref[...], w_ref[...])

    out = pl.pallas_call(
        body, grid=(n // tn, rows // tm),
        in_specs=[pl.BlockSpec((tm, k), lambda j, i: (i, 0)), pl.BlockSpec((k, tn), lambda j, i: (0, j))],
        out_specs=pl.BlockSpec((tm, tn), lambda j, i: (i, j)), out_shape=SDS((rows, n), F32),
        compiler_params=_cp("parallel", "parallel"), name=name)(xn, w)
    return out, xn


def matmul_residual(a, w, res, name):
    rows, k = a.shape
    n = w.shape[1]
    tm = _row_tile(rows)

    def body(a_ref, w_ref, r_ref, o_ref):
        o_ref[...] = r_ref[...] + _dot(a_ref[...], w_ref[...])

    return pl.pallas_call(
        body, grid=(rows // tm,),
        in_specs=[pl.BlockSpec((tm, k), lambda i: (i, 0)), pl.BlockSpec((k, n), lambda i: (0, 0)),
                  pl.BlockSpec((tm, n), lambda i: (i, 0))],
        out_specs=pl.BlockSpec((tm, n), lambda i: (i, 0)),
        out_shape=SDS((rows, n), F32), compiler_params=_cp("parallel"), name=name)(a, w, res)


def wgrad(a, b, name):
    rows, k = a.shape
    n = b.shape[1]
    tm = _row_tile(rows)
    tn = min(n, 1024)

    def body(a_ref, b_ref, o_ref):
        @pl.when(pl.program_id(1) == 0)
        def _():
            o_ref[...] = jnp.zeros_like(o_ref)

        o_ref[...] += _dot(a_ref[...], b_ref[...].astype(BF16), TN)

    return pl.pallas_call(
        body, grid=(n // tn, rows // tm),
        in_specs=[pl.BlockSpec((tm, k), lambda j, i: (i, 0)), pl.BlockSpec((tm, tn), lambda j, i: (i, j))],
        out_specs=pl.BlockSpec((k, tn), lambda j, i: (0, j)),
        out_shape=SDS((k, n), F32), compiler_params=_cp("parallel", "arbitrary"), name=name)(a, b)


def in_proj_bwd(dus, ws, h, nw, dh_next, name):
    rows, k = h.shape
    tm = _row_tile(rows)
    nd = len(dus)
    nt = rows // tm

    def body(*refs):
        du_refs, w_refs = refs[:nd], refs[nd:2 * nd]
        h_ref, nw_ref, dhn_ref, dh_ref, dnw_ref = refs[2 * nd:]
        dxn = _dot(du_refs[0][...].astype(BF16), w_refs[0][...], NT)
        for du_ref, w_ref in zip(du_refs[1:], w_refs[1:]):
            dxn += _dot(du_ref[...].astype(BF16), w_ref[...], NT)
        x = h_ref[...]
        r = _rms(x)
        y = x * r
        gy = dxn * nw_ref[...]
        dh_ref[...] = dhn_ref[...] + r * (gy - y * jnp.mean(y * gy, axis=-1, keepdims=True))
        dnw_ref[0] = jnp.sum(dxn * y, axis=0, keepdims=True)

    in_specs = [pl.BlockSpec((tm, du.shape[1]), lambda i: (i, 0)) for du in dus]
    in_specs += [pl.BlockSpec(w.shape, lambda i: (0, 0)) for w in ws]
    in_specs += [pl.BlockSpec((tm, k), lambda i: (i, 0)), pl.BlockSpec((1, k), lambda i: (0, 0)),
                 pl.BlockSpec((tm, k), lambda i: (i, 0))]
    return pl.pallas_call(
        body, grid=(nt,), in_specs=in_specs,
        out_specs=[pl.BlockSpec((tm, k), lambda i: (i, 0)), pl.BlockSpec((1, 1, k), lambda i: (i, 0, 0))],
        out_shape=[SDS((rows, k), F32), SDS((nt, 1, k), F32)],
        compiler_params=_cp("parallel"), name=name)(*dus, *ws, h, nw, dh_next)


def matmul_nt(a, w, name):
    rows, k = a.shape
    n = w.shape[0]
    tm = _row_tile(rows)

    def body(a_ref, w_ref, o_ref):
        o_ref[...] = _dot(a_ref[...].astype(BF16), w_ref[...], NT)

    return pl.pallas_call(
        body, grid=(rows // tm,),
        in_specs=[pl.BlockSpec((tm, k), lambda i: (i, 0)), pl.BlockSpec((n, k), lambda i: (0, 0))],
        out_specs=pl.BlockSpec((tm, n), lambda i: (i, 0)),
        out_shape=SDS((rows, n), F32), compiler_params=_cp("parallel"), name=name)(a, w)


SUB = 64
GRP = 8
TR = GRP * SUB
NBAND = 192
TK = 256


def _tile_bias(n, sb):
    r = lax.broadcasted_iota(jnp.int32, (TR, TK), 0)
    c = lax.broadcasted_iota(jnp.int32, (TR, TK), 1)
    qi = r & (SUB - 1)
    d = BLK + qi - c
    dm = n * BLK + SUB * sb - PAD + NBAND + qi - c
    band = c < NBAND
    valid = (band & (d >= 0) & (d < BLK) & (c >= 2 * BLK - BLK * n - SUB * sb)) | (
        (c >= NBAND) & (c < NBAND + N_META) & (dm >= 0))
    return valid, jnp.where(band, d, jnp.minimum(dm, BLK)).astype(F32)


def _group_col(vals):
    g = lax.broadcasted_iota(jnp.int32, (TR, 1), 0) >> 6
    col = jnp.zeros((TR, 1), F32)
    for gi, v in enumerate(vals):
        col = jnp.where(g == gi, v, col)
    return col


def _stack_heads(ref, sb, kvh):
    return jnp.concatenate(
        [ref[SUB * sb:SUB * sb + SUB, HD * (GRP * kvh + g):HD * (GRP * kvh + g) + HD] for g in range(GRP)], axis=0)


def _unstack_heads(parts):
    return jnp.concatenate([parts[kvh][SUB * g:SUB * g + SUB] for kvh in range(2) for g in range(GRP)], axis=1)


def _tile_keys(band, meta, sb):
    return jnp.concatenate([band[SUB * sb:SUB * sb + NBAND], meta,
                            jnp.zeros((TK - NBAND - N_META, HD), band.dtype)], axis=0)


def _row_sums(x):
    ones = jnp.ones((x.shape[1], 128), BF16)
    hi = x.astype(BF16)
    lo = (x - hi.astype(F32)).astype(BF16)
    return _dot(hi, ones) + _dot(lo, ones)


def _rms_stack(q):
    return lax.rsqrt(_row_sums(q * q)[:, :HD] * (1.0 / HD) + EPS)


def _fill_bias(bias_scr, n):
    @pl.when(n <= 2)
    def _():
        for sb in range(2):
            valid, dist = _tile_bias(n, sb)
            for kvh in range(2):
                slope_col = _group_col([SLOPES[GRP * kvh + g] for g in range(GRP)])
                bias_scr[2 * sb + kvh] = jnp.where(valid, -slope_col * dist, NEG)


def _tile_vals(band, meta, sb):
    return jnp.concatenate([_tile_keys(band, meta, sb), jnp.ones((TK, 3 * HD), BF16)], axis=1)


def _tile_softmax(qn16, k16, vx16, bias, sink_col):
    s = _dot(qn16, k16, NT) * (HD ** -0.5) + bias
    mx = jnp.maximum(jnp.max(s.astype(BF16), axis=-1, keepdims=True).astype(F32), sink_col)
    e = jnp.exp(s - mx)
    es = jnp.exp(sink_col - mx)
    ox = _dot(e.astype(BF16), vx16)
    return e, 1.0 / (ox[:, 2 * HD:] + es), es, ox[:, :HD]


def _kv_heads(kvb, kvm, kw_):
    out = []
    for kvh in range(2):
        kb, km = kvb[:, HD * kvh:HD * kvh + HD], kvm[:, HD * kvh:HD * kvh + HD]
        out.append(((kb * _rms(kb) * kw_).astype(BF16), (km * _rms(km) * kw_).astype(BF16),
                    kvb[:, BLK + HD * kvh:BLK + HD * kvh + HD].astype(BF16),
                    kvm[:, BLK + HD * kvh:BLK + HD * kvh + HD].astype(BF16)))
    return out


def _kv_specs(nblk, clamp):
    cur = (lambda n: (jnp.minimum(n, nblk - 1), 8)) if clamp else (lambda n: (n, 8))
    return [pl.BlockSpec((BLK, KVW), cur),
            pl.BlockSpec((BLK, KVW), lambda n: (jnp.maximum(n - 1, 0), 8)),
            pl.BlockSpec((N_META, KVW), lambda n: (PAD // N_META, 8))]


def _sink_cols(sinks):
    return jnp.repeat(sinks.reshape(2, GRP), SUB, axis=1).reshape(2, TR, 1)


SINK_SPEC = pl.BlockSpec((2, TR, 1), lambda n: (0, 0, 0))


def attn_fwd(u, qw, kw, sinks):
    rows = u.shape[0]
    nblk = rows // BLK

    def body(q_ref, g_ref, kvc_ref, kvp_ref, kvm_ref, qw_ref, kw_ref, sc_ref, og_ref, bias_scr):
        _fill_bias(bias_scr, pl.program_id(0))
        qw_ = qw_ref[...]
        kv = _kv_heads(jnp.concatenate([kvp_ref[...], kvc_ref[...]], axis=0), kvm_ref[...], kw_ref[...])
        for sb in range(2):
            parts = []
            for kvh in range(2):
                knb, knm, vb, vm = kv[kvh]
                q = _stack_heads(q_ref, sb, kvh)
                qn16 = (q * _rms_stack(q) * qw_).astype(BF16)
                _, inv, _, o = _tile_softmax(qn16, _tile_keys(knb, knm, sb), _tile_vals(vb, vm, sb),
                                             bias_scr[2 * sb + kvh], sc_ref[kvh])
                parts.append(o * inv[:, :HD])
            rows = slice(SUB * sb, SUB * sb + SUB)
            og_ref[rows, :] = (_unstack_heads(parts) * _silu(g_ref[rows, :])).astype(BF16)

    small = lambda w: pl.BlockSpec((1, w), lambda n: (0, 0))
    return pl.pallas_call(
        body, grid=(nblk,),
        in_specs=[pl.BlockSpec((BLK, 1024), lambda n: (n, 0)), pl.BlockSpec((BLK, 1024), lambda n: (n, 1))]
        + _kv_specs(nblk, False) + [small(HD), small(HD), SINK_SPEC],
        out_specs=pl.BlockSpec((BLK, 1024), lambda n: (n, 0)),
        out_shape=SDS((rows, 1024), BF16), scratch_shapes=[pltpu.VMEM((4, TR, TK), F32)],
        compiler_params=_cp("arbitrary"), name="attn_fwd")(u, u, u, u, u, qw, kw, _sink_cols(sinks))


def attn_bwd(u, qw, kw, sinks, dog):
    rows = u.shape[0]
    nblk = rows // BLK

    def knorm_bwd(k, dkn, kw_):
        r = _rms(k)
        y = k * r
        gy = dkn * kw_
        return r * (gy - y * jnp.mean(y * gy, axis=-1, keepdims=True)), jnp.sum(dkn * y, axis=0, keepdims=True)

    def body(q_ref, g_ref, dog_ref, kvc_ref, kvp_ref, kvm_ref, qw_ref, kw_ref, sc_ref,
             dq_ref, dg_ref, dkv_ref, dkvm_ref, dqw_ref, dkw_ref, dsk_ref, carry, prevp, curp, metap, bias_scr):
        n = pl.program_id(0)
        qw_, kw_ = qw_ref[...], kw_ref[...]
        _fill_bias(bias_scr, n)

        @pl.when(n == 0)
        def _():
            carry[...] = jnp.zeros_like(carry)
            metap[...] = jnp.zeros_like(metap)
            dqw_ref[...] = jnp.zeros_like(dqw_ref)
            dkw_ref[...] = jnp.zeros_like(dkw_ref)
            dsk_ref[...] = jnp.zeros_like(dsk_ref)

        @pl.when(n == nblk)
        def _():
            prevp[...] = jnp.zeros_like(prevp)
            curp[...] = jnp.zeros_like(curp)

        @pl.when(n < nblk)
        def _():
            kv = _kv_heads(jnp.concatenate([kvp_ref[...], kvc_ref[...]], axis=0), kvm_ref[...], kw_)
            lane = lax.broadcasted_iota(jnp.int32, (1, HEADS), 1)
            dqw = jnp.zeros((1, HD), F32)
            dsk = jnp.zeros((1, HEADS), F32)
            band_parts = [jnp.zeros((2 * BLK, HD), F32) for _ in range(4)]
            meta_parts = [jnp.zeros((N_META, HD), F32) for _ in range(4)]

            def widen(x, sb):
                z = jnp.zeros((2 * BLK - NBAND, HD), F32)
                return jnp.concatenate([x, z] if sb == 0 else [z, x], axis=0)

            for sb in range(2):
                rows = slice(SUB * sb, SUB * sb + SUB)
                dq_parts, dg_parts = [], []
                for kvh in range(2):
                    knb, knm, vb, vm = kv[kvh]
                    k16, v16 = _tile_keys(knb, knm, sb), _tile_keys(vb, vm, sb)
                    q = _stack_heads(q_ref, sb, kvh)
                    r = _rms_stack(q)
                    y = q * r
                    qn16 = (y * qw_).astype(BF16)
                    e, inv, es, o = _tile_softmax(qn16, k16, _tile_vals(vb, vm, sb), bias_scr[2 * sb + kvh],
                                                  sc_ref[kvh])
                    p = e * jnp.concatenate([inv, inv], axis=1)
                    p16 = p.astype(BF16)
                    o = o * inv[:, :HD]
                    gate = _stack_heads(g_ref, sb, kvh)
                    dog_ = _stack_heads(dog_ref, sb, kvh)
                    dg_parts.append(dog_ * o * _dsilu(gate))
                    do_ = dog_ * _silu(gate)
                    do16 = do_.astype(BF16)
                    dp = _dot(do16, v16, NT)
                    delta = _row_sums(do_ * o)
                    ds16 = (p * (dp - jnp.concatenate([delta, delta], axis=1))).astype(BF16)
                    dsink = -(es * inv) * delta
                    for g in range(GRP):
                        dsk += jnp.where(lane == GRP * kvh + g,
                                         jnp.sum(dsink[SUB * g:SUB * g + SUB, :HEADS], axis=0, keepdims=True), 0.0)
                    dqn = _dot(ds16, k16) * (HD ** -0.5)
                    dk = (_dot((y * qw_).T.astype(BF16), ds16) * (HD ** -0.5)).T
                    dv = _dot(do_.T.astype(BF16), p16).T
                    band_parts[kvh] += widen(dk[:NBAND], sb)
                    band_parts[2 + kvh] += widen(dv[:NBAND], sb)
                    meta_parts[kvh] += dk[NBAND:NBAND + N_META]
                    meta_parts[2 + kvh] += dv[NBAND:NBAND + N_META]
                    gy = dqn * qw_
                    dq_parts.append(r * (gy - y * (_row_sums(y * gy)[:, :HD] * (1.0 / HD))))
                    dqw += jnp.sum(dqn * y, axis=0, keepdims=True)
                dq_ref[rows, :] = _unstack_heads(dq_parts).astype(BF16)
                dg_ref[rows, :] = _unstack_heads(dg_parts).astype(BF16)
            band = jnp.concatenate(band_parts, axis=1)
            prevp[...] = band[:BLK]
            curp[...] = band[BLK:]
            metap[...] += jnp.concatenate(meta_parts, axis=1)
            dqw_ref[...] += dqw
            dsk_ref[...] += dsk

        tot = carry[...] + prevp[...]
        kprev = kvp_ref[...]
        dk0, w0 = knorm_bwd(kprev[:, 0:HD], tot[:, 0:HD], kw_)
        dk1, w1 = knorm_bwd(kprev[:, HD:2 * HD], tot[:, HD:2 * HD], kw_)
        dkv_ref[...] = jnp.concatenate([dk0, dk1, tot[:, 2 * HD:]], axis=1)
        dkw_ref[...] += w0 + w1
        carry[...] = curp[...]

        @pl.when(n == nblk)
        def _():
            mt = metap[...]
            km = kvm_ref[...]
            m0, v0 = knorm_bwd(km[:, 0:HD], mt[:, 0:HD], kw_)
            m1, v1 = knorm_bwd(km[:, HD:2 * HD], mt[:, HD:2 * HD], kw_)
            dkvm_ref[...] = jnp.concatenate([m0, m1, mt[:, 2 * HD:]], axis=1)
            dkw_ref[...] += v0 + v1

    small = lambda w: pl.BlockSpec((1, w), lambda n: (0, 0))
    cl = lambda n: jnp.minimum(n, nblk - 1)
    return pl.pallas_call(
        body, grid=(nblk + 1,),
        in_specs=[pl.BlockSpec((BLK, 1024), lambda n: (cl(n), 0)), pl.BlockSpec((BLK, 1024), lambda n: (cl(n), 1)),
                  pl.BlockSpec((BLK, 1024), lambda n: (cl(n), 0))]
        + _kv_specs(nblk, True) + [small(HD), small(HD), SINK_SPEC],
        out_specs=[pl.BlockSpec((BLK, 1024), lambda n: (cl(n), 0)), pl.BlockSpec((BLK, 1024), lambda n: (cl(n), 0)),
                   pl.BlockSpec((BLK, KVW), lambda n: (jnp.maximum(n - 1, 0), 0)),
                   pl.BlockSpec((N_META, KVW), lambda n: (0, 0)), small(HD), small(HD), small(HEADS)],
        out_shape=[SDS((rows, 1024), BF16), SDS((rows, 1024), BF16), SDS((rows, KVW), F32), SDS((N_META, KVW), F32),
                   SDS((1, HD), F32), SDS((1, HD), F32), SDS((1, HEADS), F32)],
        scratch_shapes=[pltpu.VMEM((BLK, KVW), F32), pltpu.VMEM((BLK, KVW), F32), pltpu.VMEM((BLK, KVW), F32),
                        pltpu.VMEM((N_META, KVW), F32), pltpu.VMEM((4, TR, TK), F32)],
        compiler_params=_cp("arbitrary"), name="attn_bwd")(u, u, dog, u, u, u, qw, kw, _sink_cols(sinks))


HB = 16


def _bdot(a, b, kind, split=False):
    dims = {"nn": ((2,), (1,)), "nt": ((2,), (2,)), "tn": ((1,), (1,))}[kind]
    dg = lambda p, q: lax.dot_general(p, q, (dims, ((0,), (0,))), preferred_element_type=F32)
    if not split:
        return dg(a, b)
    ah, bh = a.astype(BF16), b.astype(BF16)
    al, bl = (a - ah.astype(F32)).astype(BF16), (b - bh.astype(F32)).astype(BF16)
    return (dg(ah, bl) + dg(al, bh)) + dg(ah, bh)


def _head_cols(hv, beta, gc, gct, lane):
    sel = lane == hv
    return _pick(beta, sel), _pick(gc, sel), gct[pl.ds(hv, 1), :]


def _conv_group(xc_ref, xp_ref, cw_ref, off, first):
    xp = jnp.where(first, 0.0, xp_ref[:, pl.ds(off, DK)])
    xx = jnp.concatenate([xp, xc_ref[:, pl.ds(off, DK)]], axis=0)
    y = cw_ref[0:1, pl.ds(off, DK)] * xx[5:5 + CH]
    for j in range(1, 4):
        y += cw_ref[j:j + 1, pl.ds(off, DK)] * xx[5 + j:5 + j + CH]
    return xx, y


def _gates(ba, al, dtb, c):
    row = c * CH + lax.broadcasted_iota(jnp.int32, (CH, DN_H), 0)
    real = row >= PAD
    xa = ba[:, DN_H:2 * DN_H] + dtb
    beta = jnp.where(real, jax.nn.sigmoid(ba[:, 0:DN_H]), 0.0)
    g = jnp.where(real, -jnp.exp(al) * jax.nn.softplus(xa), 0.0)
    return real, xa, beta, g


def _pick(x, sel):
    return jnp.sum(jnp.where(sel, x, 0.0), axis=1, keepdims=True)


def _chunk_specs(width_blocks):
    return [pl.BlockSpec((CH, 4096), lambda c: (c, 0)),
            pl.BlockSpec((8, 4096), lambda c: (jnp.maximum(8 * c - 1, 0), 0)),
            pl.BlockSpec((CH, DK), lambda c: (c, 48))]


def _tri_inv(m, ii, jj):
    eye = (ii == jj).astype(BF16)
    mh, ml = _split(m)
    blk8 = (ii >> 3) == (jj >> 3)
    mb = (jnp.where(blk8, mh, 0), jnp.where(blk8, ml, 0))
    m2 = _split(_dot3(mb, mb))
    m4 = _split(_dot3(m2, m2))
    x = _dot3(_split(_dot3((eye - mb[0], -mb[1]), (eye + m2[0], m2[1]))), (eye + m4[0], m4[1]))
    for sh in (3, 4, 5):
        off = ((ii >> (sh + 1)) == (jj >> (sh + 1))) & ((ii >> sh) != (jj >> sh))
        xs = _split(x)
        x = x - _dot3(_split(_dot3(xs, (jnp.where(off, mh, 0), jnp.where(off, ml, 0)))), xs)
    return x


def _split(x):
    hi = x.astype(BF16)
    return hi, (x - hi.astype(F32)).astype(BF16)


def _dot3(a, b):
    dg = lambda p, q: lax.dot_general(p, q, ((((2,), (1,))), ((0,), (0,))), preferred_element_type=F32)
    return (dg(a[0], b[1]) + dg(a[1], b[0])) + dg(a[0], b[0])


def dn_prep(udn, conv_w, a_log, dt_bias):
    rows = udn.shape[0]
    nch = rows // CH

    def body(xc_ref, xp_ref, ba_ref, cw_ref, al_ref, dtb_ref,
             qn_ref, kn_ref, sv_ref, gc_ref, beta_ref, u_ref, w_ref, qe_ref, ks_ref, p_ref, at_ref, pt_ref,
             qet_ref, wt_ref, kst_ref, y_ref, gct):
        c = pl.program_id(0)
        first = c == 0
        _, _, beta, g = _gates(ba_ref[...], al_ref[...], dtb_ref[...], c)
        ii = lax.broadcasted_iota(jnp.int32, (CH, CH), 0)
        jj = lax.broadcasted_iota(jnp.int32, (CH, CH), 1)
        gc = _dot((ii >= jj).astype(F32), g, precision=HI)
        gc_ref[...] = gc
        beta_ref[...] = beta
        gct[...] = gc.T

        def qk_body(kh, carry):
            off = pl.multiple_of(kh * DK, DK)
            _, yq = _conv_group(xc_ref, xp_ref, cw_ref, off, first)
            y_ref[:, pl.ds(off, DK)] = yq
            sq = _silu(yq)
            qn_ref[:, pl.ds(off, DK)] = sq * lax.rsqrt(jnp.sum(sq * sq, axis=-1, keepdims=True) + EPS) * (DK ** -0.5)
            _, yk = _conv_group(xc_ref, xp_ref, cw_ref, pl.multiple_of(1024 + kh * DK, DK), first)
            y_ref[:, pl.ds(pl.multiple_of(1024 + kh * DK, DK), DK)] = yk
            sk = _silu(yk)
            kn_ref[:, pl.ds(off, DK)] = sk * lax.rsqrt(jnp.sum(sk * sk, axis=-1, keepdims=True) + EPS)
            return carry

        lax.fori_loop(0, DN_KH, qk_body, 0)
        lane = lax.broadcasted_iota(jnp.int32, (CH, DN_H), 1)
        zpad = jnp.zeros((CH, DK - CH), F32)

        def v_group(grp, carry):
            offs, ks_, qs_, vs_, cols = [], [], [], [], []
            for i in range(HB):
                hv = grp * HB + i
                offs.append(pl.multiple_of(hv * DK, DK))
                koff = pl.multiple_of((grp * (HB // 2) + i // 2) * DK, DK)
                _, yv = _conv_group(xc_ref, xp_ref, cw_ref, pl.multiple_of(2048 + hv * DK, DK), first)
                y_ref[:, pl.ds(pl.multiple_of(2048 + hv * DK, DK), DK)] = yv
                vs_.append(_silu(yv))
                sv_ref[:, pl.ds(offs[i], DK)] = vs_[i]
                ks_.append(kn_ref[:, pl.ds(koff, DK)])
                qs_.append(qn_ref[:, pl.ds(koff, DK)])
                cols.append(_head_cols(hv, beta, gc, gct, lane))
            k, q, v = jnp.stack(ks_), jnp.stack(qs_), jnp.stack(vs_)
            beta_c, gc_c, gc_r = (jnp.stack([c_[j] for c_ in cols]) for j in range(3))
            dec = jnp.exp(jnp.where(ii >= jj, gc_c - gc_r, NEG))
            eg = jnp.exp(gc_c)
            kb = k * beta_c
            k16 = k.astype(BF16)
            m = jnp.where(ii > jj, _bdot(kb.astype(BF16), k16, "nt") * dec, 0.0)
            a = _tri_inv(m, ii, jj)
            uw = _bdot(a, jnp.concatenate([v * beta_c, kb * eg], axis=2), "nn", True)
            p = _bdot(q.astype(BF16), k16, "nt") * dec
            qe = q * eg
            ksx = k * jnp.exp(gc_c[:, CH - 1:CH, :] - gc_c)
            tslot = lambda x: jnp.concatenate([x.T, jnp.zeros((DK, DK - CH), F32)], axis=1).astype(BF16)
            for i in range(HB):
                sl = pl.ds(offs[i], DK)
                u_ref[:, sl] = uw[i, :, :DK]
                w_ref[:, sl] = uw[i, :, DK:]
                qe_ref[:, sl] = qe[i].astype(BF16)
                ks_ref[:, sl] = ksx[i].astype(BF16)
                p_ref[:, sl] = jnp.concatenate([p[i], zpad], axis=1).astype(BF16)
                at_ref[:, sl] = jnp.concatenate([a[i].T, zpad], axis=1)
                pt_ref[:, sl] = jnp.concatenate([p[i].T, zpad], axis=1).astype(BF16)
                qet_ref[:, sl] = tslot(qe[i])
                wt_ref[:, sl] = tslot(uw[i, :, DK:])
                kst_ref[:, sl] = tslot(ksx[i])
            return carry

        lax.fori_loop(0, DN_H // HB, v_group, 0)

    full = lambda shape: pl.BlockSpec(shape, lambda c: (0, 0))
    blk = lambda w: pl.BlockSpec((CH, w), lambda c: (c, 0))
    return pl.pallas_call(
        body, grid=(nch,),
        in_specs=_chunk_specs(0) + [full((4, 4096)), full((1, DN_H)), full((1, DN_H))],
        out_specs=[blk(1024), blk(1024), blk(2048), blk(DN_H), blk(DN_H), blk(2048), blk(2048), blk(2048), blk(2048),
                   blk(2048), blk(2048), blk(2048)] + [pl.BlockSpec((DK, 2048), lambda c: (c, 0))] * 3 + [blk(4096)],
        out_shape=[SDS((rows, 1024), F32), SDS((rows, 1024), F32), SDS((rows, 2048), F32), SDS((rows, DN_H), F32),
                   SDS((rows, DN_H), F32), SDS((rows, 2048), F32), SDS((rows, 2048), F32), SDS((rows, 2048), BF16),
                   SDS((rows, 2048), BF16), SDS((rows, 2048), BF16), SDS((rows, 2048), F32),
                   SDS((rows, 2048), BF16)] + [SDS((2 * rows, 2048), BF16)] * 3 + [SDS((rows, 4096), F32)],
        scratch_shapes=[pltpu.VMEM((DN_H, CH), F32)],
        compiler_params=_cp("parallel"), name="dn_prep")(udn, udn, udn, conv_w, a_log, dt_bias)


def dn_scan(u, w, qe, kst, p, gc):
    rows = u.shape[0]
    nch = rows // CH

    def body(u_ref, w_ref, qe_ref, kst_ref, p_ref, gc_ref, o_ref, vn_ref, st_ref, s_scr):
        @pl.when(pl.program_id(0) == 0)
        def _():
            s_scr[...] = jnp.zeros_like(s_scr)

        gl_row = gc_ref[CH - 1:CH, :]
        lane = lax.broadcasted_iota(jnp.int32, (1, DN_H), 1)

        def group(grp, carry):
            base = grp * HB
            sls = [pl.ds(pl.multiple_of((base + i) * DK, DK), DK) for i in range(HB)]
            heads = lambda ref: jnp.stack([ref[:, sl] for sl in sls])
            s = s_scr[pl.ds(base, HB)]
            st_ref[0, pl.ds(base, HB)] = s
            s16 = s.astype(BF16)
            vn = heads(u_ref) - _bdot(heads(w_ref).astype(BF16), s16, "nn")
            vn16 = vn.astype(BF16)
            o = _bdot(heads(qe_ref), s16, "nn") + _bdot(heads(p_ref)[:, :, 0:CH], vn16, "nn")
            egl = jnp.exp(jnp.stack([_pick(gl_row, lane == base + i) for i in range(HB)]))
            s_scr[pl.ds(base, HB)] = s * egl + _bdot(heads(kst_ref)[:, :, 0:CH], vn16, "nn")
            for i in range(HB):
                vn_ref[:, sls[i]] = vn16[i]
                o_ref[:, sls[i]] = o[i]
            return carry

        lax.fori_loop(0, DN_H // HB, group, 0)

    blk = lambda wd: pl.BlockSpec((CH, wd), lambda c: (c, 0))
    return pl.pallas_call(
        body, grid=(nch,),
        in_specs=[blk(2048)] * 3 + [pl.BlockSpec((DK, 2048), lambda c: (c, 0)), blk(2048), blk(DN_H)],
        out_specs=[blk(2048), blk(2048), pl.BlockSpec((1, DN_H, DK, DK), lambda c: (c, 0, 0, 0))],
        out_shape=[SDS((rows, 2048), F32), SDS((rows, 2048), BF16), SDS((nch, DN_H, DK, DK), F32)],
        scratch_shapes=[pltpu.VMEM((DN_H, DK, DK), F32)],
        compiler_params=_cp("arbitrary"), name="dn_scan")(u, w, qe, kst, p, gc)


def dn_out_fwd(o, udn, ow, wout, h1, tgt):
    rows = o.shape[0]
    tm = _row_tile(rows)
    nt = rows // tm

    def body(o_ref, z_ref, ow_ref, w_ref, h_ref, t_ref, dh_ref, on_ref, ls_ref):
        for hv in range(DN_H):
            sl = slice(hv * DK, hv * DK + DK)
            oh = o_ref[:, sl]
            on_ref[:, sl] = (oh * _rms(oh) * ow_ref[...] * _silu(z_ref[:, sl])).astype(BF16)
        h2 = h_ref[...] + _dot(on_ref[...], w_ref[...])
        row = pl.program_id(0) * tm + lax.broadcasted_iota(jnp.int32, (tm, 1), 0)
        err = jnp.where(row >= BLK, h2 - t_ref[...], 0.0)
        dh_ref[...] = err * (1.0 / D_MODEL)
        ls_ref[0] = jnp.sum(err * err, axis=0, keepdims=True)

    return pl.pallas_call(
        body, grid=(nt,),
        in_specs=[pl.BlockSpec((tm, 2048), lambda i: (i, 0)), pl.BlockSpec((tm, 2048), lambda i: (i, 2)),
                  pl.BlockSpec((1, DK), lambda i: (0, 0)), pl.BlockSpec((2048, D_MODEL), lambda i: (0, 0)),
                  pl.BlockSpec((tm, D_MODEL), lambda i: (i, 0)), pl.BlockSpec((tm, D_MODEL), lambda i: (i, 0))],
        out_specs=[pl.BlockSpec((tm, D_MODEL), lambda i: (i, 0)), pl.BlockSpec((tm, 2048), lambda i: (i, 0)),
                   pl.BlockSpec((1, 1, D_MODEL), lambda i: (i, 0, 0))],
        out_shape=[SDS((rows, D_MODEL), F32), SDS((rows, 2048), BF16), SDS((nt, 1, D_MODEL), F32)],
        compiler_params=_cp("parallel"), name="dn_out_fwd")(o, udn, ow, wout, h1, tgt)


def dn_out_bwd(dh2, wout, o, udn, ow):
    rows = o.shape[0]
    tm = _row_tile(rows)
    nt = rows // tm

    def body(dh_ref, w_ref, o_ref, z_ref, ow_ref, do_ref, dz_ref, dow_ref):
        don = _dot(dh_ref[...].astype(BF16), w_ref[...], NT)
        ow_ = ow_ref[...]
        dow = jnp.zeros((1, DK), F32)
        for hv in range(DN_H):
            sl = slice(hv * DK, hv * DK + DK)
            oh = o_ref[:, sl]
            r = _rms(oh)
            y = oh * r
            z = z_ref[:, sl]
            dn = don[:, sl] * _silu(z)
            dz_ref[:, sl] = (don[:, sl] * (y * ow_) * _dsilu(z)).astype(BF16)
            dy = dn * ow_
            do_ref[:, sl] = r * (dy - y * jnp.mean(y * dy, axis=-1, keepdims=True))
            dow += jnp.sum(dn * y, axis=0, keepdims=True)
        dow_ref[0] = dow

    return pl.pallas_call(
        body, grid=(nt,),
        in_specs=[pl.BlockSpec((tm, D_MODEL), lambda i: (i, 0)), pl.BlockSpec((2048, D_MODEL), lambda i: (0, 0)),
                  pl.BlockSpec((tm, 2048), lambda i: (i, 0)), pl.BlockSpec((tm, 2048), lambda i: (i, 2)),
                  pl.BlockSpec((1, DK), lambda i: (0, 0))],
        out_specs=[pl.BlockSpec((tm, 2048), lambda i: (i, 0)), pl.BlockSpec((tm, 2048), lambda i: (i, 0)),
                   pl.BlockSpec((1, 1, DK), lambda i: (i, 0, 0))],
        out_shape=[SDS((rows, 2048), F32), SDS((rows, 2048), BF16), SDS((nt, 1, DK), F32)],
        compiler_params=_cp("parallel"), name="dn_out_bwd")(dh2, wout, o, udn, ow)


def dn_scan_bwd(do, qn, kn, sv, gc, beta, at, pt, u, w, vn, qet, wt, ks, st):
    rows = do.shape[0]
    nch = rows // CH

    def body(do_ref, q_ref, k_ref, v_ref, gc_ref, beta_ref, at_ref, pt_ref, u_ref, w_ref, vn_ref, qet_ref, wt_ref,
             ks_ref, st_ref, dq_ref, dk_ref, dv_ref, dbeta_ref, dg_ref, ds_scr, gct):
        @pl.when(pl.program_id(0) == 0)
        def _():
            ds_scr[...] = jnp.zeros_like(ds_scr)

        gc, beta = gc_ref[...], beta_ref[...]
        gct[...] = gc.T
        ii = lax.broadcasted_iota(jnp.int32, (CH, CH), 0)
        jj = lax.broadcasted_iota(jnp.int32, (CH, CH), 1)
        lane = lax.broadcasted_iota(jnp.int32, (CH, DN_H), 1)
        last = lax.broadcasted_iota(jnp.int32, (CH, 1), 0) == CH - 1

        def group(grp, carry):
            dbeta_acc, dgc_acc = carry
            base = grp * HB
            sls = [pl.ds(pl.multiple_of((base + i) * DK, DK), DK) for i in range(HB)]
            ksls = [pl.ds(pl.multiple_of((grp * (HB // 2) + j) * DK, DK), DK) for j in range(HB // 2)]
            heads = lambda ref: jnp.stack([ref[:, sl] for sl in sls])
            kheads = lambda ref: jnp.stack([ref[:, ksls[i // 2]] for i in range(HB)])
            cols = [_head_cols(base + i, beta, gc, gct, lane) for i in range(HB)]
            beta_c, gc_c, gc_r = (jnp.stack([c_[j] for c_ in cols]) for j in range(3))
            k, q, v = kheads(k_ref), kheads(q_ref), heads(v_ref)
            dec = jnp.exp(jnp.where(ii >= jj, gc_c - gc_r, NEG))
            eg = jnp.exp(gc_c)
            gl = gc_c[:, CH - 1:CH, :]
            e2 = jnp.exp(gl - gc_c)
            egl = jnp.exp(gl)
            k16, q16 = k.astype(BF16), q.astype(BF16)
            do16 = heads(do_ref).astype(BF16)
            s = st_ref[0, pl.ds(base, HB)]
            s16 = s.astype(BF16)
            dso = ds_scr[pl.ds(base, HB)]
            dso16 = dso.astype(BF16)
            wf, uf, vn16 = heads(w_ref), heads(u_ref), heads(vn_ref)
            kb = k * beta_c
            kb16 = kb.astype(BF16)
            pm = _bdot(q16, k16, "nt") * dec
            m = jnp.where(ii > jj, _bdot(kb16, k16, "nt") * dec, 0.0)
            dvn = _bdot(heads(pt_ref)[:, :, 0:CH], do16, "nn") + _bdot(heads(ks_ref), dso16, "nn")
            dvn16 = dvn.astype(BF16)
            ds_scr[pl.ds(base, HB)] = (egl * dso + _bdot(heads(qet_ref)[:, :, 0:CH], do16, "nn")
                                       - _bdot(heads(wt_ref)[:, :, 0:CH], dvn16, "nn"))
            dpm = jnp.where(ii >= jj, _bdot(do16, vn16, "nt"), 0.0)
            dqk16 = (dpm * dec).astype(BF16)
            dqe = _bdot(do16, s16, "nt")
            dq = eg * dqe + _bdot(dqk16, k16, "nn")
            dks = _bdot(vn16, dso16, "nt")
            dw = -_bdot(dvn16, s16, "nt")
            dbvk = _bdot(heads(at_ref)[:, :, 0:CH], jnp.concatenate([dvn, dw], axis=2), "nn", True)
            dbv, dbk = dbvk[:, :, :DK], dbvk[:, :, DK:]
            dm = jnp.where(ii > jj, -_bdot(dbvk, jnp.concatenate([uf, wf], axis=2), "nt", True), 0.0)
            g16 = (dm * dec).astype(BF16)
            dkb = _bdot(g16, k16, "nn")
            dk = (_bdot(dqk16, q16, "tn") + e2 * dks + _bdot(g16, kb16, "tn") + beta_c * (eg * dbk + dkb))
            e = dpm * pm + dm * m
            rsum = lambda x: jnp.sum(x, axis=2, keepdims=True)
            r_bk, r_qe, r_beta, r_ks = rsum(dbk * k), rsum(q * dqe), rsum(dbv * v + dkb * k), rsum(dks * k)
            t = r_ks * e2
            dgl = jnp.sum(t, axis=1, keepdims=True) + egl * rsum(jnp.sum(dso * s, axis=1, keepdims=True))
            deg = r_qe + beta_c * r_bk
            dgc = rsum(e) - t + deg * eg + jnp.where(last, dgl, 0.0)
            dgrow = -jnp.sum(e, axis=1, keepdims=True)
            dv = beta_c * dbv
            dbeta = r_beta + eg * r_bk
            for i in range(HB):
                dv_ref[:, sls[i]] = dv[i]
                sel = lane == base + i
                dbeta_acc = jnp.where(sel, dbeta[i], dbeta_acc)
                dgc_acc = jnp.where(sel, dgc[i], dgc_acc)
                gct[pl.ds(base + i, 1), :] = dgrow[i]
            for j in range(HB // 2):
                dq_ref[:, ksls[j]] = dq[2 * j] + dq[2 * j + 1]
                dk_ref[:, ksls[j]] = dk[2 * j] + dk[2 * j + 1]
            return dbeta_acc, dgc_acc

        zero = jnp.zeros((CH, DN_H), F32)
        dbeta_acc, dgc_acc = lax.fori_loop(0, DN_H // HB, group, (zero, zero))
        dbeta_ref[...] = dbeta_acc
        dg_ref[...] = _dot((ii <= jj).astype(F32), dgc_acc + gct[...].T, precision=HI)

    rev = lambda wd: pl.BlockSpec((CH, wd), lambda i: (nch - 1 - i, 0))
    rev_t = pl.BlockSpec((DK, 2048), lambda i: (nch - 1 - i, 0))
    return pl.pallas_call(
        body, grid=(nch,),
        in_specs=[rev(2048), rev(1024), rev(1024), rev(2048), rev(DN_H), rev(DN_H), rev(2048), rev(2048), rev(2048),
                  rev(2048), rev(2048), rev_t, rev_t, rev(2048),
                  pl.BlockSpec((1, DN_H, DK, DK), lambda i: (nch - 1 - i, 0, 0, 0))],
        out_specs=[rev(1024), rev(1024), rev(2048), rev(DN_H), rev(DN_H)],
        out_shape=[SDS((rows, 1024), F32), SDS((rows, 1024), F32), SDS((rows, 2048), F32), SDS((rows, DN_H), F32),
                   SDS((rows, DN_H), F32)],
        scratch_shapes=[pltpu.VMEM((DN_H, DK, DK), F32), pltpu.VMEM((DN_H, CH), F32)],
        compiler_params=_cp("arbitrary"), name="dn_scan_bwd")(
            do, qn, kn, sv, gc, beta, at, pt, u, w, vn, qet, wt, ks, st)


def dn_prep_bwd(udn, yconv, conv_w, a_log, dt_bias, dqn, dkn, dv, dbeta, dg):
    rows = udn.shape[0]
    nch = rows // CH
    ext = CH + 8

    def body(xc_ref, ba_ref, yc_ref, yn_ref, dqn_n, dkn_n, dv_n, cw_ref, al_ref, dtb_ref, dqn_ref, dkn_ref, dv_ref,
             dbeta_ref, dg_ref, dx_ref, dba_ref, dcw_ref, dal_ref, ddtb_ref):
        c = pl.program_id(0)
        first = c == 0
        own = (lax.broadcasted_iota(jnp.int32, (ext, 1), 0) < CH) | (c < nch - 1)

        @pl.when(first)
        def _():
            dcw_ref[...] = jnp.zeros_like(dcw_ref)
            dal_ref[...] = jnp.zeros_like(dal_ref)
            ddtb_ref[...] = jnp.zeros_like(ddtb_ref)

        real, xa, beta, g = _gates(ba_ref[...], al_ref[...], dtb_ref[...], c)
        dgm = jnp.where(real, dg_ref[...], 0.0)
        da = dgm * (-jnp.exp(al_ref[...])) * jax.nn.sigmoid(xa)
        dal_ref[...] += jnp.sum(dgm * g, axis=0, keepdims=True)
        ddtb_ref[...] += jnp.sum(da, axis=0, keepdims=True)
        dba_ref[...] = jnp.zeros_like(dba_ref)
        dba_ref[:, 0:DN_H] = jnp.where(real, dbeta_ref[...] * beta * (1.0 - beta), 0.0)
        dba_ref[:, DN_H:2 * DN_H] = da

        def through_conv(off, g_cur, g_next, grad_fn):
            sl = pl.ds(off, DK)
            y = jnp.concatenate([yc_ref[:, sl], yn_ref[:, sl]], axis=0)
            sg = jax.nn.sigmoid(y)
            dsilu = sg * (1.0 + y * (1.0 - sg))
            dy = jnp.where(own, grad_fn(y * sg, jnp.concatenate([g_cur, g_next], axis=0)) * dsilu, 0.0)
            shifted = [dy[3 - j:3 - j + CH] for j in range(4)]
            x = xc_ref[:, sl]
            dx = cw_ref[0:1, sl] * shifted[0]
            for j in range(1, 4):
                dx += cw_ref[j:j + 1, sl] * shifted[j]
            dx_ref[:, sl] = dx.astype(BF16)
            for j in range(4):
                dcw_ref[j:j + 1, sl] += jnp.sum(shifted[j] * x, axis=0, keepdims=True)

        def l2_bwd(scale):
            def f(s, gin):
                r = lax.rsqrt(jnp.sum(s * s, axis=-1, keepdims=True) + EPS)
                nrm = s * r
                return (r * scale) * (gin - nrm * jnp.sum(nrm * gin, axis=-1, keepdims=True))
            return f

        def qk_body(kh, carry):
            sl = pl.ds(pl.multiple_of(kh * DK, DK), DK)
            through_conv(pl.multiple_of(kh * DK, DK), dqn_ref[:, sl], dqn_n[:, sl], l2_bwd(DK ** -0.5))
            through_conv(pl.multiple_of(1024 + kh * DK, DK), dkn_ref[:, sl], dkn_n[:, sl], l2_bwd(1.0))
            return carry

        lax.fori_loop(0, DN_KH, qk_body, 0)

        def v_body(hv, carry):
            sl = pl.ds(pl.multiple_of(hv * DK, DK), DK)
            through_conv(pl.multiple_of(2048 + hv * DK, DK), dv_ref[:, sl], dv_n[:, sl], lambda s, gin: gin)
            return carry

        lax.fori_loop(0, DN_H, v_body, 0)

    full = lambda shape: pl.BlockSpec(shape, lambda c: (0, 0))
    blk = lambda w: pl.BlockSpec((CH, w), lambda c: (c, 0))
    nxt = lambda w: pl.BlockSpec((8, w), lambda c: (jnp.minimum(8 * c + 8, rows // 8 - 1), 0))
    return pl.pallas_call(
        body, grid=(nch,),
        in_specs=[_chunk_specs(0)[0], _chunk_specs(0)[2], blk(4096), nxt(4096), nxt(1024), nxt(1024), nxt(2048),
                  full((4, 4096)), full((1, DN_H)), full((1, DN_H)), blk(1024), blk(1024), blk(2048), blk(DN_H),
                  blk(DN_H)],
        out_specs=[blk(4096), blk(DK), full((8, 4096)), full((1, DN_H)), full((1, DN_H))],
        out_shape=[SDS((rows, 4096), BF16), SDS((rows, DK), F32), SDS((8, 4096), F32), SDS((1, DN_H), F32),
                   SDS((1, DN_H), F32)],
        compiler_params=_cp("arbitrary"), name="dn_prep_bwd")(
            udn, udn, yconv, yconv, dqn, dkn, dv, conv_w, a_log, dt_bias, dqn, dkn, dv, dbeta, dg)


def local_step(x, target, w):
    seq = x.shape[0]
    bf = lambda a: a.astype(BF16)
    h0 = jnp.concatenate([jnp.zeros((PAD, D_MODEL), F32), w["meta_tokens"], x], axis=0)
    tgt = jnp.concatenate([jnp.zeros((BLK, D_MODEL), F32), target], axis=0)
    win = w["attn_w_in"]
    wq, wkv, wg = win[:, :1024], win[:, 1024:1280], win[:, 1280:]
    wa_in = bf(jnp.concatenate([wq, wg, wkv], axis=1))
    wa_out = bf(w["attn_w_out"])
    wd_in = jnp.concatenate([bf(w["dn_w_in"]), jnp.zeros((D_MODEL, 96), BF16)], axis=1)
    wd_out = bf(w["dn_w_out"])
    qw, kw, sinks = w["attn_q_norm_w"], w["attn_k_norm_w"], w["attn_sinks"]
    cw, al, dtb, ow = w["dn_conv_w"], w["dn_a_log"], w["dn_dt_bias"], w["dn_o_norm_w"]

    ua, xn0 = norm_matmul(h0, w["attn_norm_w"], wa_in, 2304, "attn_in")
    og = attn_fwd(ua, qw, kw, sinks)
    h1 = matmul_residual(og, wa_out, h0, "attn_out")
    ud, xn1 = norm_matmul(h1, w["dn_norm_w"], wd_in, 6272, "dn_in")
    qn, kn, sv, gc, beta, u, wy, qe, ks, p, at, pt, qet, wt, kst, yconv = dn_prep(ud, cw, al, dtb)
    o, vn, st = dn_scan(u, wy, qe, kst, p, gc)
    dh2, on, ls = dn_out_fwd(o, ud, ow, wd_out, h1, tgt)
    loss = (0.5 / D_MODEL) * jnp.sum(ls)

    do, dz, dow = dn_out_bwd(dh2, wd_out, o, ud, ow)
    g_dn_out = wgrad(on, dh2, "dn_out_wgrad")
    dqn, dkn, dv, dbeta, dg = dn_scan_bwd(do, qn, kn, sv, gc, beta, at, pt, u, wy, vn, qet, wt, ks, st)
    dxc, dba, dcw, dal, ddtb = dn_prep_bwd(ud, yconv, cw, al, dtb, dqn, dkn, dv, dbeta, dg)
    dh1, dnw1 = in_proj_bwd([dxc, dz, dba], [wd_in[:, :4096], wd_in[:, 4096:6144], wd_in[:, 6144:]],
                            h1, w["dn_norm_w"], dh2, "dn_in_bwd")
    g_dn_in = jnp.concatenate([wgrad(xn1, dxc, "dn_in_wgrad_qkv"), wgrad(xn1, dz, "dn_in_wgrad_z"),
                               wgrad(xn1, dba, "dn_in_wgrad_ba")[:, :2 * DN_H]], axis=1)

    dog = matmul_nt(dh1, wa_out, "attn_out_bwd")
    g_attn_out = wgrad(og, dh1, "attn_out_wgrad")
    dq, dgate, dkv, dkvm, dqw, dkw, dsk = attn_bwd(ua, qw, kw, sinks, dog)
    dkv = dkv.at[PAD:BLK].add(dkvm)
    dh0, dnw0 = in_proj_bwd([dq, dgate, dkv], [wa_in[:, :1024], wa_in[:, 1024:2048], wa_in[:, 2048:]],
                            h0, w["attn_norm_w"], dh1, "attn_in_bwd")
    g_attn_in = jnp.concatenate([wgrad(xn0, dq, "attn_in_wgrad_q"), wgrad(xn0, dkv, "attn_in_wgrad_kv"),
                                 wgrad(xn0, dgate, "attn_in_wgrad_g")], axis=1)
    grads = {
        "meta_tokens": dh0[PAD:BLK], "attn_norm_w": jnp.sum(dnw0, axis=0), "attn_w_in": g_attn_in,
        "attn_q_norm_w": dqw, "attn_k_norm_w": dkw, "attn_sinks": dsk, "attn_w_out": g_attn_out,
        "dn_norm_w": jnp.sum(dnw1, axis=0), "dn_w_in": g_dn_in, "dn_conv_w": dcw[:4], "dn_a_log": dal,
        "dn_dt_bias": ddtb, "dn_o_norm_w": jnp.sum(dow, axis=0), "dn_w_out": g_dn_out,
    }
    return loss, dh0[BLK:BLK + seq], grads


WEIGHTS = ["meta_tokens", "attn_norm_w", "attn_w_in", "attn_q_norm_w", "attn_k_norm_w", "attn_sinks", "attn_w_out",
           "dn_norm_w", "dn_w_in", "dn_conv_w", "dn_a_log", "dn_dt_bias", "dn_o_norm_w", "dn_w_out"]
SHARDED = {"attn_w_in": ((1024, 2304), 1), "attn_w_out": ((1024, 1024), 0), "dn_w_in": ((1024, 6176), 1),
           "dn_w_out": ((2048, 1024), 0), "dn_conv_w": ((4, 4096), 1), "meta_tokens": ((16, 1024), 1),
           "dn_norm_w": ((1, 1024), 1)}
REPLICATED = {"attn_norm_w": 1024, "attn_q_norm_w": 64, "attn_k_norm_w": 64, "attn_sinks": 16, "dn_a_log": 16,
              "dn_dt_bias": 16, "dn_o_norm_w": 128}
N_CHIPS = 4
PACK_ROWS = 2912
HALF_ROWS = PACK_ROWS // 2
SMALL_ROWS = 8


def _shard_shape(name):
    (r, c), axis = SHARDED[name]
    return (r // N_CHIPS, c) if axis == 0 else (r, c // N_CHIPS)


def _pack(parts, rows):
    flat = jnp.concatenate([p.reshape(-1) for p in parts])
    return jnp.pad(flat, (0, rows * 1024 - flat.shape[0])).reshape(rows, 1024)


def pack_shard(shards):
    return _pack([shards[n] for n in SHARDED], PACK_ROWS)


def unpack_shard(buf):
    flat, out, pos = buf.reshape(-1), {}, 0
    for n in SHARDED:
        shp = _shard_shape(n)
        size = shp[0] * shp[1]
        out[n] = flat[pos:pos + size].reshape(shp)
        pos += size
    return out


MATRICES = ("attn_w_in", "attn_w_out", "dn_w_in", "dn_w_out")


def pack_gather(shards):
    big = [shards[n].astype(BF16).reshape(-1) for n in MATRICES]
    small = jnp.concatenate([shards[n].reshape(-1) for n in SHARDED if n not in MATRICES])
    flat = jnp.concatenate(big + [lax.bitcast_convert_type(small, BF16).reshape(-1)])
    return jnp.pad(flat, (0, PACK_ROWS * 1024 - flat.shape[0])).reshape(PACK_ROWS, 1024)


def unpack_gather(buf):
    PER_F32 = 4 // jnp.dtype(buf.dtype).itemsize
    flat, out, pos = buf.reshape(-1), {}, 0
    for n in MATRICES:
        shp = _shard_shape(n)
        out[n] = flat[pos:pos + shp[0] * shp[1]].reshape(shp)
        pos += shp[0] * shp[1]
    for n in SHARDED:
        if n not in MATRICES:
            shp = _shard_shape(n)
            raw = flat[pos:pos + shp[0] * shp[1] * PER_F32]
            out[n] = lax.bitcast_convert_type(raw.reshape(-1, PER_F32) if PER_F32 > 1 else raw, F32).reshape(shp)
            pos += shp[0] * shp[1] * PER_F32
    return out


def pack_small(vals):
    return _pack([vals[n] for n in REPLICATED], SMALL_ROWS)


def unpack_small(buf):
    flat, out, pos = buf.reshape(-1), {}, 0
    for n, size in REPLICATED.items():
        out[n] = flat[pos:pos + size].reshape(1, size)
        pos += size
    return out


ANY = pl.BlockSpec(memory_space=pl.ANY)


def _place():
    return lax.axis_index("x"), lax.axis_index("y"), lax.axis_index("c")


def chips_exchange(src, gather):
    r = src.shape[-2]

    def body(s_ref, o_ref, send_sems, recv_sems):
        x, y, c = _place()
        me = 2 * x + y
        peers = [(1 - x, y), (x, 1 - y), (1 - x, 1 - y)]

        def copy(k, to_block, from_block):
            px, py = peers[k]
            return pltpu.make_async_remote_copy(
                src_ref=s_ref if gather else s_ref.at[to_block], dst_ref=o_ref.at[from_block],
                send_sem=send_sems.at[k], recv_sem=recv_sems.at[k], device_id=(px, py, c), device_id_type=MESH)

        sends = [copy(k, 2 * px + py, me) for k, (px, py) in enumerate(peers)]
        for cp in sends:
            cp.start()
        for k, (px, py) in enumerate(peers):
            copy(k, me, 2 * px + py).wait_recv()
        for cp in sends:
            cp.wait_send()

    return pl.pallas_call(
        body, in_specs=[ANY], out_specs=ANY, out_shape=SDS((N_CHIPS, r, 1024), src.dtype),
        scratch_shapes=[pltpu.SemaphoreType.DMA((3,)), pltpu.SemaphoreType.DMA((3,))],
        name="chips_gather" if gather else "chips_exchange")(src)


def chip_sum(received, pair, me):
    tm = 208

    def body(me_ref, own_ref, r1_ref, r2_ref, r3_ref, o_ref):
        o_ref[...] = ((own_ref[0] + r1_ref[0]) + r2_ref[0]) + r3_ref[0]

    blk = lambda k: pl.BlockSpec((1, tm, 1024), lambda i, me_ref: ((me_ref[0] + k) % N_CHIPS, i, 0))
    return pl.pallas_call(
        body,
        grid_spec=pltpu.PrefetchScalarGridSpec(
            num_scalar_prefetch=1, grid=(HALF_ROWS // tm,), in_specs=[blk(0), blk(1), blk(2), blk(3)],
            out_specs=pl.BlockSpec((tm, 1024), lambda i, me_ref: (i, 0))),
        out_shape=SDS((HALF_ROWS, 1024), F32), compiler_params=_cp("parallel"), name="chip_sum")(
            me.reshape(1).astype(jnp.int32), pair, received, received, received)


def _rows_at(ref, start, size):
    return ref.at[:, pl.ds(start, size), :] if len(ref.shape) == 3 else ref.at[pl.ds(start, size), :]


def sibling_join(src, name):
    axis = len(src.shape) - 2

    def body(s_ref, o_ref, send_sem, recv_sem):
        x, y, c = _place()
        cp = pltpu.make_async_remote_copy(src_ref=s_ref, dst_ref=o_ref, send_sem=send_sem, recv_sem=recv_sem,
                                          device_id=(x, y, 1 - c), device_id_type=MESH)
        cp.start()
        cp.wait()

    theirs = pl.pallas_call(
        body, in_specs=[ANY], out_specs=ANY, out_shape=SDS(src.shape, src.dtype),
        scratch_shapes=[pltpu.SemaphoreType.DMA, pltpu.SemaphoreType.DMA], name=name)(src)
    first = lax.axis_index("c") == 0
    return jnp.concatenate([jnp.where(first, src, theirs), jnp.where(first, theirs, src)], axis=axis)


def sibling_give(g_all):
    def body(s_ref, o_ref, send_sem, recv_sem):
        x, y, c = _place()
        cp = pltpu.make_async_remote_copy(
            src_ref=_rows_at(s_ref, (1 - c) * HALF_ROWS, HALF_ROWS), dst_ref=o_ref, send_sem=send_sem,
            recv_sem=recv_sem, device_id=(x, y, 1 - c), device_id_type=MESH)
        cp.start()
        cp.wait()

    return pl.pallas_call(
        body, in_specs=[ANY], out_specs=ANY, out_shape=SDS((N_CHIPS, HALF_ROWS, 1024), F32),
        scratch_shapes=[pltpu.SemaphoreType.DMA, pltpu.SemaphoreType.DMA], name="pair_exchange")(g_all)


def pair_sum(g_all, got, c):
    tm = 208
    per_half = HALF_ROWS // tm

    def body(c_ref, a_ref, b_ref, o_ref):
        o_ref[...] = a_ref[...] + b_ref[...]

    return pl.pallas_call(
        body,
        grid_spec=pltpu.PrefetchScalarGridSpec(
            num_scalar_prefetch=1, grid=(N_CHIPS, per_half),
            in_specs=[pl.BlockSpec((1, tm, 1024), lambda j, i, c_ref: (j, c_ref[0] * per_half + i, 0)),
                      pl.BlockSpec((1, tm, 1024), lambda j, i, c_ref: (j, i, 0))],
            out_specs=pl.BlockSpec((1, tm, 1024), lambda j, i, c_ref: (j, i, 0))),
        out_shape=SDS((N_CHIPS, HALF_ROWS, 1024), F32),
        compiler_params=_cp("parallel", "parallel"), name="pair_sum")(c.reshape(1).astype(jnp.int32), g_all, got)


def all_gather_small(src):
    def body(s_ref, o_ref, send_sems, recv_sems, local_sem):
        x, y, c = _place()
        flips = [(fx, fy, fc) for fx in (0, 1) for fy in (0, 1) for fc in (0, 1)][1:]
        idx = lambda px, py, pc: 4 * px + 2 * py + pc
        mine = pltpu.make_async_copy(s_ref, o_ref.at[idx(x, y, c)], local_sem)
        mine.start()

        def peer(k):
            fx, fy, fc = flips[k]
            return (1 - x if fx else x, 1 - y if fy else y, 1 - c if fc else c)

        def copy(k, block):
            return pltpu.make_async_remote_copy(
                src_ref=s_ref, dst_ref=o_ref.at[block], send_sem=send_sems.at[k], recv_sem=recv_sems.at[k],
                device_id=peer(k), device_id_type=MESH)

        sends = [copy(k, idx(x, y, c)) for k in range(7)]
        for cp in sends:
            cp.start()
        for k in range(7):
            copy(k, idx(*peer(k))).wait_recv()
        for cp in sends:
            cp.wait_send()
        mine.wait()

    return pl.pallas_call(
        body, in_specs=[ANY], out_specs=ANY, out_shape=SDS((8,) + src.shape, F32),
        scratch_shapes=[pltpu.SemaphoreType.DMA((7,)), pltpu.SemaphoreType.DMA((7,)), pltpu.SemaphoreType.DMA],
        name="all_gather_small")(src)


def sum_blocks(t, name):
    n, r, _ = t.shape
    tm = 208 if r % 208 == 0 else r

    def body(t_ref, o_ref):
        acc = t_ref[0]
        for i in range(1, n):
            acc = acc + t_ref[i]
        o_ref[...] = acc

    return pl.pallas_call(
        body, grid=(r // tm,), in_specs=[pl.BlockSpec((n, tm, 1024), lambda i: (0, i, 0))],
        out_specs=pl.BlockSpec((tm, 1024), lambda i: (i, 0)), out_shape=SDS((r, 1024), F32),
        compiler_params=_cp("parallel"), name=name)(t)


ADAM_BLOCK_BYTES = 1024 * 1024


def adamw(w, g, m, v, name):
    rows, cols = w.shape
    tm = rows
    while tm * cols * 4 > ADAM_BLOCK_BYTES and tm % 16 == 0:
        tm //= 2

    def body(w_ref, g_ref, m_ref, v_ref, d_ref, nm_ref, nv_ref):
        g_ = g_ref[...]
        m_ = ADAM_B1 * m_ref[...] + (1.0 - ADAM_B1) * g_
        v_ = ADAM_B2 * v_ref[...] + (1.0 - ADAM_B2) * (g_ * g_)
        m_hat = m_ / (1.0 - ADAM_B1 ** ADAM_STEP)
        v_hat = v_ / (1.0 - ADAM_B2 ** ADAM_STEP)
        d_ref[...] = -ADAM_LR * (m_hat / (jnp.sqrt(v_hat) + ADAM_EPS) + ADAM_WD * w_ref[...])
        nm_ref[...] = m_
        nv_ref[...] = v_

    spec = pl.BlockSpec((tm, cols), lambda i: (i, 0))
    return pl.pallas_call(
        body, grid=(rows // tm,), in_specs=[spec] * 4, out_specs=[spec] * 3,
        out_shape=[SDS((rows, cols), F32)] * 3, compiler_params=_cp("parallel"), name=name)(w, g, m, v)


LAYERED = ("attn_w_in", "attn_w_out", "dn_w_in", "dn_conv_w", "dn_w_out")


def _two_d(name, a):
    return a[0] if name in LAYERED else a


def kernel(x, meta_tokens, attn_norm_w, attn_w_in, attn_q_norm_w, attn_k_norm_w, attn_sinks, attn_w_out, dn_norm_w, dn_w_in, dn_conv_w, dn_a_log, dn_dt_bias, dn_o_norm_w, dn_w_out, loss_target, m_meta_tokens, m_attn_norm_w, m_attn_w_in, m_attn_q_norm_w, m_attn_k_norm_w, m_attn_sinks, m_attn_w_out, m_dn_norm_w, m_dn_w_in, m_dn_conv_w, m_dn_a_log, m_dn_dt_bias, m_dn_o_norm_w, m_dn_w_out, v_meta_tokens, v_attn_norm_w, v_attn_w_in, v_attn_q_norm_w, v_attn_k_norm_w, v_attn_sinks, v_attn_w_out, v_dn_norm_w, v_dn_w_in, v_dn_conv_w, v_dn_a_log, v_dn_dt_bias, v_dn_o_norm_w, v_dn_w_out):
    given = dict(zip(WEIGHTS, (meta_tokens, attn_norm_w, attn_w_in, attn_q_norm_w, attn_k_norm_w, attn_sinks,
                               attn_w_out, dn_norm_w, dn_w_in, dn_conv_w, dn_a_log, dn_dt_bias, dn_o_norm_w, dn_w_out)))
    mom1 = dict(zip(WEIGHTS, (m_meta_tokens, m_attn_norm_w, m_attn_w_in, m_attn_q_norm_w, m_attn_k_norm_w,
                              m_attn_sinks, m_attn_w_out, m_dn_norm_w, m_dn_w_in, m_dn_conv_w, m_dn_a_log,
                              m_dn_dt_bias, m_dn_o_norm_w, m_dn_w_out)))
    mom2 = dict(zip(WEIGHTS, (v_meta_tokens, v_attn_norm_w, v_attn_w_in, v_attn_q_norm_w, v_attn_k_norm_w,
                              v_attn_sinks, v_attn_w_out, v_dn_norm_w, v_dn_w_in, v_dn_conv_w, v_dn_a_log,
                              v_dn_dt_bias, v_dn_o_norm_w, v_dn_w_out)))
    two_d = lambda d: {n: _two_d(n, a) for n, a in d.items()}
    given, mom1, mom2 = two_d(given), two_d(mom1), two_d(mom2)
    c = lax.axis_index("c")

    me = 2 * lax.axis_index("x") + lax.axis_index("y")
    own_half = lax.dynamic_slice_in_dim(pack_gather(given), c * HALF_ROWS, HALF_ROWS, axis=0)
    mine = lax.dynamic_update_slice_in_dim(chips_exchange(own_half, True), own_half[None], me, 0)
    gathered = sibling_join(mine, "gather_swap")
    per_chip = [unpack_gather(gathered[j]) for j in range(N_CHIPS)]
    full = {n: jnp.concatenate([pc[n] for pc in per_chip], axis=SHARDED[n][1]) for n in SHARDED}
    full.update({n: given[n] for n in REPLICATED})

    loss, dx, grads = local_step(x[0], loss_target[0], full)

    split = lambda n: jnp.split(grads[n], N_CHIPS, axis=SHARDED[n][1])
    g_all = jnp.stack([pack_shard({n: split(n)[j] for n in SHARDED}) for j in range(N_CHIPS)])
    pair = pair_sum(g_all, sibling_give(g_all), c)
    half = chip_sum(chips_exchange(pair, False), pair, me)
    g_shard = sibling_join(half, "half_exchange")

    g_small = sum_blocks(all_gather_small(pack_small(grads)), "small_sum")

    g_local = unpack_shard(g_shard)
    g_local.update(unpack_small(g_small))
    steps = {n: adamw(given[n], g_local[n], mom1[n], mom2[n], "adamw_" + n) for n in WEIGHTS}
    shaped = lambda n, a: a[None] if n in LAYERED else a
    outs = [[shaped(n, g_local[n]) for n in WEIGHTS]]
    outs += [[shaped(n, steps[n][k]) for n in WEIGHTS] for k in range(3)]

    loss = lax.psum(loss, ("x", "y", "c"))
    return (loss, dx[None], *outs[0], *outs[1], *outs[2], *outs[3])
```

```python
import functools

import jax
import jax.numpy as jnp
from jax import lax
from jax.experimental import pallas as pl
from jax.experimental.pallas import tpu as pltpu

F32 = jnp.float32
BF16 = jnp.bfloat16
SDS = jax.ShapeDtypeStruct
MESH = pl.DeviceIdType.MESH

D_MODEL = 1024
N_META = 16
EPS = 1e-6
BLK = 128
CH = 64
PAD = BLK - N_META
HEADS = 16
HD = 64
KVW = 256
DN_H = 16
DN_KH = 8
DK = 128
SLOPES = [2.0 ** (-8.0 * (h + 1) / HEADS) for h in range(HEADS)]
NEG = -1e30
NT = (((1,), (1,)), ((), ()))
TN = (((0,), (0,)), ((), ()))
HI = lax.Precision.HIGHEST

ADAM_LR, ADAM_B1, ADAM_B2, ADAM_EPS, ADAM_WD, ADAM_STEP = 0.001, 0.9, 0.999, 1e-08, 0.01, 10

VMEM_LIMIT = 56 * 1024 * 1024
MXU_DEPTH = 256
WGRAD_BLOCK_ELEMS = 2 * 1024 * 1024


def _cp(*sem):
    return pltpu.CompilerParams(dimension_semantics=sem, vmem_limit_bytes=VMEM_LIMIT)


def _row_tile(rows):
    for t in (384, 256, 128):
        if rows % t == 0:
            return t
    raise ValueError(rows)


def _dot(a, b, dims=None, precision=None):
    if dims is None:
        return jnp.dot(a, b, preferred_element_type=F32, precision=precision)
    return lax.dot_general(a, b, dims, preferred_element_type=F32, precision=precision)


def _silu(x):
    return x * jax.nn.sigmoid(x)


def _dsilu(x):
    s = jax.nn.sigmoid(x)
    return s * (1.0 + x * (1.0 - s))


def _rms(x):
    return lax.rsqrt(jnp.mean(x * x, axis=-1, keepdims=True) + EPS)


def norm_matmul(h, nw, w, tn, name):
    rows, k = h.shape
    n = w.shape[1]
    tm = _row_tile(rows)

    def norm_body(h_ref, nw_ref, xn_ref):
        x = h_ref[...]
        xn_ref[...] = (x * _rms(x) * nw_ref[...]).astype(BF16)

    xn = pl.pallas_call(
        norm_body, grid=(rows // tm,),
        in_specs=[pl.BlockSpec((tm, k), lambda i: (i, 0)), pl.BlockSpec((1, k), lambda i: (0, 0))],
        out_specs=pl.BlockSpec((tm, k), lambda i: (i, 0)), out_shape=SDS((rows, k), BF16),
        compiler_params=_cp("parallel"), name=name + "_norm")(h, nw)

    def body(a_ref, w_ref, o_ref):
        o_ref[...] = _dot(a_ref[...], w_ref[...])

    out = pl.pallas_call(
        body, grid=(n // tn, rows // tm),
        in_specs=[pl.BlockSpec((tm, k), lambda j, i: (i, 0)), pl.BlockSpec((k, tn), lambda j, i: (0, j))],
        out_specs=pl.BlockSpec((tm, tn), lambda j, i: (i, j)), out_shape=SDS((rows, n), F32),
        compiler_params=_cp("parallel", "parallel"), name=name)(xn, w)
    return out, xn


def matmul_residual(a, w, res, name):
    rows, k = a.shape
    n = w.shape[1]
    tm = _row_tile(rows)

    def body(a_ref, w_ref, r_ref, o_ref):
        o_ref[...] = r_ref[...] + _dot(a_ref[...], w_ref[...])

    return pl.pallas_call(
        body, grid=(rows // tm,),
        in_specs=[pl.BlockSpec((tm, k), lambda i: (i, 0)), pl.BlockSpec((k, n), lambda i: (0, 0)),
                  pl.BlockSpec((tm, n), lambda i: (i, 0))],
        out_specs=pl.BlockSpec((tm, n), lambda i: (i, 0)),
        out_shape=SDS((rows, n), F32), compiler_params=_cp("parallel"), name=name)(a, w, res)


def wgrad(a, b, name):
    rows, k = a.shape
    n = b.shape[1]
    tm = _row_tile(rows)
    tn = min(n, WGRAD_BLOCK_ELEMS // k)

    def body(a_ref, b_ref, o_ref):
        @pl.when(pl.program_id(1) == 0)
        def _():
            o_ref[...] = jnp.zeros_like(o_ref)

        o_ref[...] += _dot(a_ref[...], b_ref[...].astype(BF16), TN)

    return pl.pallas_call(
        body, grid=(n // tn, rows // tm),
        in_specs=[pl.BlockSpec((tm, k), lambda j, i: (i, 0)), pl.BlockSpec((tm, tn), lambda j, i: (i, j))],
        out_specs=pl.BlockSpec((k, tn), lambda j, i: (0, j)),
        out_shape=SDS((k, n), F32), compiler_params=_cp("parallel", "arbitrary"), name=name)(a, b)


def in_proj_bwd(dus, ws, h, nw, dh_next, name):
    rows, k = h.shape
    tm = _row_tile(rows)
    nd = len(dus)
    nt = rows // tm

    def body(*refs):
        du_refs, w_refs = refs[:nd], refs[nd:2 * nd]
        h_ref, nw_ref, dhn_ref, dh_ref, dnw_ref = refs[2 * nd:]
        dxn = _dot(du_refs[0][...].astype(BF16), w_refs[0][...], NT)
        for du_ref, w_ref in zip(du_refs[1:], w_refs[1:]):
            dxn += _dot(du_ref[...].astype(BF16), w_ref[...], NT)
        x = h_ref[...]
        r = _rms(x)
        y = x * r
        gy = dxn * nw_ref[...]
        dh_ref[...] = dhn_ref[...] + r * (gy - y * jnp.mean(y * gy, axis=-1, keepdims=True))
        dnw_ref[0] = jnp.sum(dxn * y, axis=0, keepdims=True)

    in_specs = [pl.BlockSpec((tm, du.shape[1]), lambda i: (i, 0)) for du in dus]
    in_specs += [pl.BlockSpec(w.shape, lambda i: (0, 0)) for w in ws]
    in_specs += [pl.BlockSpec((tm, k), lambda i: (i, 0)), pl.BlockSpec((1, k), lambda i: (0, 0)),
                 pl.BlockSpec((tm, k), lambda i: (i, 0))]
    return pl.pallas_call(
        body, grid=(nt,), in_specs=in_specs,
        out_specs=[pl.BlockSpec((tm, k), lambda i: (i, 0)), pl.BlockSpec((1, 1, k), lambda i: (i, 0, 0))],
        out_shape=[SDS((rows, k), F32), SDS((nt, 1, k), F32)],
        compiler_params=_cp("parallel"), name=name)(*dus, *ws, h, nw, dh_next)


def matmul_nt(a, w, name):
    rows, k = a.shape
    n = w.shape[0]
    tm = _row_tile(rows)

    def body(a_ref, w_ref, o_ref):
        o_ref[...] = _dot(a_ref[...].astype(BF16), w_ref[...], NT)

    return pl.pallas_call(
        body, grid=(rows // tm,),
        in_specs=[pl.BlockSpec((tm, k), lambda i: (i, 0)), pl.BlockSpec((n, k), lambda i: (0, 0))],
        out_specs=pl.BlockSpec((tm, n), lambda i: (i, 0)),
        out_shape=SDS((rows, n), F32), compiler_params=_cp("parallel"), name=name)(a, w)


SUB = 64
GRP = 8
TR = GRP * SUB
NBAND = 192
TK = 256


def _tile_bias(n, sb):
    r = lax.broadcasted_iota(jnp.int32, (TR, TK), 0)
    c = lax.broadcasted_iota(jnp.int32, (TR, TK), 1)
    qi = r & (SUB - 1)
    d = BLK + qi - c
    dm = n * BLK + SUB * sb - PAD + NBAND + qi - c
    band = c < NBAND
    valid = (band & (d >= 0) & (d < BLK) & (c >= 2 * BLK - BLK * n - SUB * sb)) | (
        (c >= NBAND) & (c < NBAND + N_META) & (dm >= 0))
    return valid, jnp.where(band, d, jnp.minimum(dm, BLK)).astype(F32)


def _group_col(vals):
    g = lax.broadcasted_iota(jnp.int32, (TR, 1), 0) >> 6
    col = jnp.zeros((TR, 1), F32)
    for gi, v in enumerate(vals):
        col = jnp.where(g == gi, v, col)
    return col


def _stack_heads(ref, sb, kvh):
    return jnp.concatenate(
        [ref[SUB * sb:SUB * sb + SUB, HD * (GRP * kvh + g):HD * (GRP * kvh + g) + HD] for g in range(GRP)], axis=0)


def _unstack_heads(parts):
    return jnp.concatenate([parts[kvh][SUB * g:SUB * g + SUB] for kvh in range(2) for g in range(GRP)], axis=1)


def _tile_keys(band, meta, sb):
    return jnp.concatenate([band[SUB * sb:SUB * sb + NBAND], meta,
                            jnp.zeros((TK - NBAND - N_META, HD), band.dtype)], axis=0)


def _row_sums(x):
    ones = jnp.ones((x.shape[1], 128), BF16)
    hi = x.astype(BF16)
    lo = (x - hi.astype(F32)).astype(BF16)
    return _dot(hi, ones) + _dot(lo, ones)


def _rms_stack(q):
    return lax.rsqrt(_row_sums(q * q)[:, :HD] * (1.0 / HD) + EPS)


def _fill_bias(bias_scr, n):
    @pl.when(n <= 2)
    def _():
        for sb in range(2):
            valid, dist = _tile_bias(n, sb)
            for kvh in range(2):
                slope_col = _group_col([SLOPES[GRP * kvh + g] for g in range(GRP)])
                bias_scr[2 * sb + kvh] = jnp.where(valid, -slope_col * dist, NEG)


def _tile_vals(band, meta, sb):
    return jnp.concatenate([_tile_keys(band, meta, sb), jnp.ones((TK, 3 * HD), BF16)], axis=1)


def _tile_softmax(qn16, k16, vx16, bias, sink_col):
    s = _dot(qn16, k16, NT) * (HD ** -0.5) + bias
    mx = jnp.maximum(jnp.max(s.astype(BF16), axis=-1, keepdims=True).astype(F32), sink_col)
    e = jnp.exp(s - mx)
    es = jnp.exp(sink_col - mx)
    ox = _dot(e.astype(BF16), vx16)
    return e, 1.0 / (ox[:, 2 * HD:] + es), es, ox[:, :HD]


def _kv_heads(kvb, kvm, kw_):
    out = []
    for kvh in range(2):
        kb, km = kvb[:, HD * kvh:HD * kvh + HD], kvm[:, HD * kvh:HD * kvh + HD]
        out.append(((kb * _rms(kb) * kw_).astype(BF16), (km * _rms(km) * kw_).astype(BF16),
                    kvb[:, BLK + HD * kvh:BLK + HD * kvh + HD].astype(BF16),
                    kvm[:, BLK + HD * kvh:BLK + HD * kvh + HD].astype(BF16)))
    return out


def _kv_specs(nblk, clamp):
    cur = (lambda n: (jnp.minimum(n, nblk - 1), 8)) if clamp else (lambda n: (n, 8))
    return [pl.BlockSpec((BLK, KVW), cur),
            pl.BlockSpec((BLK, KVW), lambda n: (jnp.maximum(n - 1, 0), 8)),
            pl.BlockSpec((N_META, KVW), lambda n: (PAD // N_META, 8))]


def _sink_cols(sinks):
    return jnp.repeat(sinks.reshape(2, GRP), SUB, axis=1).reshape(2, TR, 1)


SINK_SPEC = pl.BlockSpec((2, TR, 1), lambda n: (0, 0, 0))


def attn_fwd(u, qw, kw, sinks):
    rows = u.shape[0]
    nblk = rows // BLK

    def body(q_ref, g_ref, kvc_ref, kvp_ref, kvm_ref, qw_ref, kw_ref, sc_ref, og_ref, bias_scr):
        _fill_bias(bias_scr, pl.program_id(0))
        qw_ = qw_ref[...]
        kv = _kv_heads(jnp.concatenate([kvp_ref[...], kvc_ref[...]], axis=0), kvm_ref[...], kw_ref[...])
        for sb in range(2):
            parts = []
            for kvh in range(2):
                knb, knm, vb, vm = kv[kvh]
                q = _stack_heads(q_ref, sb, kvh)
                qn16 = (q * _rms_stack(q) * qw_).astype(BF16)
                _, inv, _, o = _tile_softmax(qn16, _tile_keys(knb, knm, sb), _tile_vals(vb, vm, sb),
                                             bias_scr[2 * sb + kvh], sc_ref[kvh])
                parts.append(o * inv[:, :HD])
            rows = slice(SUB * sb, SUB * sb + SUB)
            og_ref[rows, :] = (_unstack_heads(parts) * _silu(g_ref[rows, :])).astype(BF16)

    small = lambda w: pl.BlockSpec((1, w), lambda n: (0, 0))
    return pl.pallas_call(
        body, grid=(nblk,),
        in_specs=[pl.BlockSpec((BLK, 1024), lambda n: (n, 0)), pl.BlockSpec((BLK, 1024), lambda n: (n, 1))]
        + _kv_specs(nblk, False) + [small(HD), small(HD), SINK_SPEC],
        out_specs=pl.BlockSpec((BLK, 1024), lambda n: (n, 0)),
        out_shape=SDS((rows, 1024), BF16), scratch_shapes=[pltpu.VMEM((4, TR, TK), F32)],
        compiler_params=_cp("arbitrary"), name="attn_fwd")(u, u, u, u, u, qw, kw, _sink_cols(sinks))


def attn_bwd(u, qw, kw, sinks, dog):
    rows = u.shape[0]
    nblk = rows // BLK

    def knorm_bwd(k, dkn, kw_):
        r = _rms(k)
        y = k * r
        gy = dkn * kw_
        return r * (gy - y * jnp.mean(y * gy, axis=-1, keepdims=True)), jnp.sum(dkn * y, axis=0, keepdims=True)

    def body(q_ref, g_ref, dog_ref, kvc_ref, kvp_ref, kvm_ref, qw_ref, kw_ref, sc_ref,
             dq_ref, dg_ref, dkv_ref, dkvm_ref, dqw_ref, dkw_ref, dsk_ref, carry, prevp, curp, metap, bias_scr):
        n = pl.program_id(0)
        qw_, kw_ = qw_ref[...], kw_ref[...]
        _fill_bias(bias_scr, n)

        @pl.when(n == 0)
        def _():
            carry[...] = jnp.zeros_like(carry)
            metap[...] = jnp.zeros_like(metap)
            dqw_ref[...] = jnp.zeros_like(dqw_ref)
            dkw_ref[...] = jnp.zeros_like(dkw_ref)
            dsk_ref[...] = jnp.zeros_like(dsk_ref)

        @pl.when(n == nblk)
        def _():
            prevp[...] = jnp.zeros_like(prevp)
            curp[...] = jnp.zeros_like(curp)

        @pl.when(n < nblk)
        def _():
            kv = _kv_heads(jnp.concatenate([kvp_ref[...], kvc_ref[...]], axis=0), kvm_ref[...], kw_)
            lane = lax.broadcasted_iota(jnp.int32, (1, HEADS), 1)
            dqw = jnp.zeros((1, HD), F32)
            dsk = jnp.zeros((1, HEADS), F32)
            band_parts = [jnp.zeros((2 * BLK, HD), F32) for _ in range(4)]
            meta_parts = [jnp.zeros((N_META, HD), F32) for _ in range(4)]

            def widen(x, sb):
                z = jnp.zeros((2 * BLK - NBAND, HD), F32)
                return jnp.concatenate([x, z] if sb == 0 else [z, x], axis=0)

            for sb in range(2):
                rows = slice(SUB * sb, SUB * sb + SUB)
                dq_parts, dg_parts = [], []
                for kvh in range(2):
                    knb, knm, vb, vm = kv[kvh]
                    k16, v16 = _tile_keys(knb, knm, sb), _tile_keys(vb, vm, sb)
                    q = _stack_heads(q_ref, sb, kvh)
                    r = _rms_stack(q)
                    y = q * r
                    qn16 = (y * qw_).astype(BF16)
                    e, inv, es, o = _tile_softmax(qn16, k16, _tile_vals(vb, vm, sb), bias_scr[2 * sb + kvh],
                                                  sc_ref[kvh])
                    p = e * jnp.concatenate([inv, inv], axis=1)
                    p16 = p.astype(BF16)
                    o = o * inv[:, :HD]
                    gate = _stack_heads(g_ref, sb, kvh)
                    dog_ = _stack_heads(dog_ref, sb, kvh)
                    dg_parts.append(dog_ * o * _dsilu(gate))
                    do_ = dog_ * _silu(gate)
                    do16 = do_.astype(BF16)
                    dp = _dot(do16, v16, NT)
                    delta = _row_sums(do_ * o)
                    ds16 = (p * (dp - jnp.concatenate([delta, delta], axis=1))).astype(BF16)
                    dsink = -(es * inv) * delta
                    for g in range(GRP):
                        dsk += jnp.where(lane == GRP * kvh + g,
                                         jnp.sum(dsink[SUB * g:SUB * g + SUB, :HEADS], axis=0, keepdims=True), 0.0)
                    dqn = _dot(ds16, k16) * (HD ** -0.5)
                    dk = (_dot((y * qw_).T.astype(BF16), ds16) * (HD ** -0.5)).T
                    dv = _dot(do_.T.astype(BF16), p16).T
                    band_parts[kvh] += widen(dk[:NBAND], sb)
                    band_parts[2 + kvh] += widen(dv[:NBAND], sb)
                    meta_parts[kvh] += dk[NBAND:NBAND + N_META]
                    meta_parts[2 + kvh] += dv[NBAND:NBAND + N_META]
                    gy = dqn * qw_
                    dq_parts.append(r * (gy - y * (_row_sums(y * gy)[:, :HD] * (1.0 / HD))))
                    dqw += jnp.sum(dqn * y, axis=0, keepdims=True)
                dq_ref[rows, :] = _unstack_heads(dq_parts).astype(BF16)
                dg_ref[rows, :] = _unstack_heads(dg_parts).astype(BF16)
            band = jnp.concatenate(band_parts, axis=1)
            prevp[...] = band[:BLK]
            curp[...] = band[BLK:]
            metap[...] += jnp.concatenate(meta_parts, axis=1)
            dqw_ref[...] += dqw
            dsk_ref[...] += dsk

        tot = carry[...] + prevp[...]
        kprev = kvp_ref[...]
        dk0, w0 = knorm_bwd(kprev[:, 0:HD], tot[:, 0:HD], kw_)
        dk1, w1 = knorm_bwd(kprev[:, HD:2 * HD], tot[:, HD:2 * HD], kw_)
        dkv_ref[...] = jnp.concatenate([dk0, dk1, tot[:, 2 * HD:]], axis=1)
        dkw_ref[...] += w0 + w1
        carry[...] = curp[...]

        @pl.when(n == nblk)
        def _():
            mt = metap[...]
            km = kvm_ref[...]
            m0, v0 = knorm_bwd(km[:, 0:HD], mt[:, 0:HD], kw_)
            m1, v1 = knorm_bwd(km[:, HD:2 * HD], mt[:, HD:2 * HD], kw_)
            dkvm_ref[...] = jnp.concatenate([m0, m1, mt[:, 2 * HD:]], axis=1)
            dkw_ref[...] += v0 + v1

    small = lambda w: pl.BlockSpec((1, w), lambda n: (0, 0))
    cl = lambda n: jnp.minimum(n, nblk - 1)
    return pl.pallas_call(
        body, grid=(nblk + 1,),
        in_specs=[pl.BlockSpec((BLK, 1024), lambda n: (cl(n), 0)), pl.BlockSpec((BLK, 1024), lambda n: (cl(n), 1)),
                  pl.BlockSpec((BLK, 1024), lambda n: (cl(n), 0))]
        + _kv_specs(nblk, True) + [small(HD), small(HD), SINK_SPEC],
        out_specs=[pl.BlockSpec((BLK, 1024), lambda n: (cl(n), 0)), pl.BlockSpec((BLK, 1024), lambda n: (cl(n), 0)),
                   pl.BlockSpec((BLK, KVW), lambda n: (jnp.maximum(n - 1, 0), 0)),
                   pl.BlockSpec((N_META, KVW), lambda n: (0, 0)), small(HD), small(HD), small(HEADS)],
        out_shape=[SDS((rows, 1024), BF16), SDS((rows, 1024), BF16), SDS((rows, KVW), F32), SDS((N_META, KVW), F32),
                   SDS((1, HD), F32), SDS((1, HD), F32), SDS((1, HEADS), F32)],
        scratch_shapes=[pltpu.VMEM((BLK, KVW), F32), pltpu.VMEM((BLK, KVW), F32), pltpu.VMEM((BLK, KVW), F32),
                        pltpu.VMEM((N_META, KVW), F32), pltpu.VMEM((4, TR, TK), F32)],
        compiler_params=_cp("arbitrary"), name="attn_bwd")(u, u, dog, u, u, u, qw, kw, _sink_cols(sinks))


HB = 16


def _bdot(a, b, kind, split=False, fused=False):
    dims = {"nn": ((2,), (1,)), "nt": ((2,), (2,)), "tn": ((1,), (1,))}[kind]
    dg = lambda p, q: lax.dot_general(p, q, (dims, ((0,), (0,))), preferred_element_type=F32)
    if not split:
        return dg(a, b)
    if fused:
        assert kind == "nn" and 3 * a.shape[2] <= MXU_DEPTH
        return _dot3(_split(a), _split(b))
    ah, bh = a.astype(BF16), b.astype(BF16)
    al, bl = (a - ah.astype(F32)).astype(BF16), (b - bh.astype(F32)).astype(BF16)
    return (dg(ah, bl) + dg(al, bh)) + dg(ah, bh)


def _head_cols(hv, beta, gc, gct, lane):
    sel = lane == hv
    return _pick(beta, sel), _pick(gc, sel), gct[pl.ds(hv, 1), :]


def _conv_group(xc_ref, xp_ref, cw_ref, off, first):
    xp = jnp.where(first, 0.0, xp_ref[:, pl.ds(off, DK)])
    xx = jnp.concatenate([xp, xc_ref[:, pl.ds(off, DK)]], axis=0)
    y = cw_ref[0:1, pl.ds(off, DK)] * xx[5:5 + CH]
    for j in range(1, 4):
        y += cw_ref[j:j + 1, pl.ds(off, DK)] * xx[5 + j:5 + j + CH]
    return xx, y


def _gates(ba, al, dtb, c):
    row = c * CH + lax.broadcasted_iota(jnp.int32, (CH, DN_H), 0)
    real = row >= PAD
    xa = ba[:, DN_H:2 * DN_H] + dtb
    beta = jnp.where(real, jax.nn.sigmoid(ba[:, 0:DN_H]), 0.0)
    g = jnp.where(real, -jnp.exp(al) * jax.nn.softplus(xa), 0.0)
    return real, xa, beta, g


def _pick(x, sel):
    return jnp.sum(jnp.where(sel, x, 0.0), axis=1, keepdims=True)


def _chunk_specs(width_blocks):
    return [pl.BlockSpec((CH, 4096), lambda c: (c, 0)),
            pl.BlockSpec((8, 4096), lambda c: (jnp.maximum(8 * c - 1, 0), 0)),
            pl.BlockSpec((CH, DK), lambda c: (c, 48))]


def _tri_inv(m, ii, jj):
    eye = (ii == jj).astype(BF16)
    mh, ml = _split(m)
    blk8 = (ii >> 3) == (jj >> 3)
    mb = (jnp.where(blk8, mh, 0), jnp.where(blk8, ml, 0))
    m2 = _split(_dot3(mb, mb))
    m4 = _split(_dot3(m2, m2))
    x = _dot3(_split(_dot3((eye - mb[0], -mb[1]), (eye + m2[0], m2[1]))), (eye + m4[0], m4[1]))
    for sh in (3, 4, 5):
        off = ((ii >> (sh + 1)) == (jj >> (sh + 1))) & ((ii >> sh) != (jj >> sh))
        xs = _split(x)
        x = x - _dot3(_split(_dot3(xs, (jnp.where(off, mh, 0), jnp.where(off, ml, 0)))), xs)
    return x


def _split(x):
    hi = x.astype(BF16)
    return hi, (x - hi.astype(F32)).astype(BF16)


def _dot3(a, b):
    lhs = jnp.concatenate([a[0], a[1], a[0]], axis=2)
    rhs = jnp.concatenate([b[0], b[0], b[1]], axis=1)
    return lax.dot_general(lhs, rhs, (((2,), (1,)), ((0,), (0,))), preferred_element_type=F32)


def dn_prep(udn, conv_w, a_log, dt_bias):
    rows = udn.shape[0]
    nch = rows // CH

    def body(xc_ref, xp_ref, ba_ref, cw_ref, al_ref, dtb_ref,
             qn_ref, kn_ref, sv_ref, gc_ref, beta_ref, u_ref, w_ref, qe_ref, ks_ref, p_ref, at_ref, pt_ref,
             qet_ref, wt_ref, kst_ref, y_ref, gct):
        c = pl.program_id(0)
        first = c == 0
        _, _, beta, g = _gates(ba_ref[...], al_ref[...], dtb_ref[...], c)
        ii = lax.broadcasted_iota(jnp.int32, (CH, CH), 0)
        jj = lax.broadcasted_iota(jnp.int32, (CH, CH), 1)
        gc = _dot((ii >= jj).astype(F32), g, precision=HI)
        gc_ref[...] = gc
        beta_ref[...] = beta
        gct[...] = gc.T

        def qk_body(kh, carry):
            off = pl.multiple_of(kh * DK, DK)
            _, yq = _conv_group(xc_ref, xp_ref, cw_ref, off, first)
            y_ref[:, pl.ds(off, DK)] = yq
            sq = _silu(yq)
            qn_ref[:, pl.ds(off, DK)] = sq * lax.rsqrt(jnp.sum(sq * sq, axis=-1, keepdims=True) + EPS) * (DK ** -0.5)
            _, yk = _conv_group(xc_ref, xp_ref, cw_ref, pl.multiple_of(1024 + kh * DK, DK), first)
            y_ref[:, pl.ds(pl.multiple_of(1024 + kh * DK, DK), DK)] = yk
            sk = _silu(yk)
            kn_ref[:, pl.ds(off, DK)] = sk * lax.rsqrt(jnp.sum(sk * sk, axis=-1, keepdims=True) + EPS)
            return carry

        lax.fori_loop(0, DN_KH, qk_body, 0)
        lane = lax.broadcasted_iota(jnp.int32, (CH, DN_H), 1)
        zpad = jnp.zeros((CH, DK - CH), F32)

        def v_group(grp, carry):
            offs, ks_, qs_, vs_, cols = [], [], [], [], []
            for i in range(HB):
                hv = grp * HB + i
                offs.append(pl.multiple_of(hv * DK, DK))
                koff = pl.multiple_of((grp * (HB // 2) + i // 2) * DK, DK)
                _, yv = _conv_group(xc_ref, xp_ref, cw_ref, pl.multiple_of(2048 + hv * DK, DK), first)
                y_ref[:, pl.ds(pl.multiple_of(2048 + hv * DK, DK), DK)] = yv
                vs_.append(_silu(yv))
                sv_ref[:, pl.ds(offs[i], DK)] = vs_[i]
                ks_.append(kn_ref[:, pl.ds(koff, DK)])
                qs_.append(qn_ref[:, pl.ds(koff, DK)])
                cols.append(_head_cols(hv, beta, gc, gct, lane))
            k, q, v = jnp.stack(ks_), jnp.stack(qs_), jnp.stack(vs_)
            beta_c, gc_c, gc_r = (jnp.stack([c_[j] for c_ in cols]) for j in range(3))
            dec = jnp.exp(jnp.where(ii >= jj, gc_c - gc_r, NEG))
            eg = jnp.exp(gc_c)
            kb = k * beta_c
            k16 = k.astype(BF16)
            m = jnp.where(ii > jj, _bdot(kb.astype(BF16), k16, "nt") * dec, 0.0)
            a = _tri_inv(m, ii, jj)
            uw = _bdot(a, jnp.concatenate([v * beta_c, kb * eg], axis=2), "nn", True, True)
            p = _bdot(q.astype(BF16), k16, "nt") * dec
            qe = q * eg
            ksx = k * jnp.exp(gc_c[:, CH - 1:CH, :] - gc_c)
            tslot = lambda x: jnp.concatenate([x.T, jnp.zeros((DK, DK - CH), F32)], axis=1).astype(BF16)
            for i in range(HB):
                sl = pl.ds(offs[i], DK)
                u_ref[:, sl] = uw[i, :, :DK]
                w_ref[:, sl] = uw[i, :, DK:]
                qe_ref[:, sl] = qe[i].astype(BF16)
                ks_ref[:, sl] = ksx[i].astype(BF16)
                p_ref[:, sl] = jnp.concatenate([p[i], zpad], axis=1).astype(BF16)
                at_ref[:, sl] = jnp.concatenate([a[i].T, zpad], axis=1)
                pt_ref[:, sl] = jnp.concatenate([p[i].T, zpad], axis=1).astype(BF16)
                qet_ref[:, sl] = tslot(qe[i])
                wt_ref[:, sl] = tslot(uw[i, :, DK:])
                kst_ref[:, sl] = tslot(ksx[i])
            return carry

        lax.fori_loop(0, DN_H // HB, v_group, 0)

    full = lambda shape: pl.BlockSpec(shape, lambda c: (0, 0))
    blk = lambda w: pl.BlockSpec((CH, w), lambda c: (c, 0))
    return pl.pallas_call(
        body, grid=(nch,),
        in_specs=_chunk_specs(0) + [full((4, 4096)), full((1, DN_H)), full((1, DN_H))],
        out_specs=[blk(1024), blk(1024), blk(2048), blk(DN_H), blk(DN_H), blk(2048), blk(2048), blk(2048), blk(2048),
                   blk(2048), blk(2048), blk(2048)] + [pl.BlockSpec((DK, 2048), lambda c: (c, 0))] * 3 + [blk(4096)],
        out_shape=[SDS((rows, 1024), F32), SDS((rows, 1024), F32), SDS((rows, 2048), F32), SDS((rows, DN_H), F32),
                   SDS((rows, DN_H), F32), SDS((rows, 2048), F32), SDS((rows, 2048), F32), SDS((rows, 2048), BF16),
                   SDS((rows, 2048), BF16), SDS((rows, 2048), BF16), SDS((rows, 2048), F32),
                   SDS((rows, 2048), BF16)] + [SDS((2 * rows, 2048), BF16)] * 3 + [SDS((rows, 4096), F32)],
        scratch_shapes=[pltpu.VMEM((DN_H, CH), F32)],
        compiler_params=_cp("parallel"), name="dn_prep")(udn, udn, udn, conv_w, a_log, dt_bias)


def dn_scan(u, w, qe, kst, p, gc):
    rows = u.shape[0]
    nch = rows // CH

    def body(u_ref, w_ref, qe_ref, kst_ref, p_ref, gc_ref, o_ref, vn_ref, st_ref, s_scr):
        @pl.when(pl.program_id(0) == 0)
        def _():
            s_scr[...] = jnp.zeros_like(s_scr)

        gl_row = gc_ref[CH - 1:CH, :]
        lane = lax.broadcasted_iota(jnp.int32, (1, DN_H), 1)

        def group(grp, carry):
            base = grp * HB
            sls = [pl.ds(pl.multiple_of((base + i) * DK, DK), DK) for i in range(HB)]
            heads = lambda ref: jnp.stack([ref[:, sl] for sl in sls])
            s = s_scr[pl.ds(base, HB)]
            st_ref[0, pl.ds(base, HB)] = s
            s16 = s.astype(BF16)
            vn = heads(u_ref) - _bdot(heads(w_ref).astype(BF16), s16, "nn")
            vn16 = vn.astype(BF16)
            o = _bdot(heads(qe_ref), s16, "nn") + _bdot(heads(p_ref)[:, :, 0:CH], vn16, "nn")
            egl = jnp.exp(jnp.stack([_pick(gl_row, lane == base + i) for i in range(HB)]))
            s_scr[pl.ds(base, HB)] = s * egl + _bdot(heads(kst_ref)[:, :, 0:CH], vn16, "nn")
            for i in range(HB):
                vn_ref[:, sls[i]] = vn16[i]
                o_ref[:, sls[i]] = o[i]
            return carry

        lax.fori_loop(0, DN_H // HB, group, 0)

    blk = lambda wd: pl.BlockSpec((CH, wd), lambda c: (c, 0))
    return pl.pallas_call(
        body, grid=(nch,),
        in_specs=[blk(2048)] * 3 + [pl.BlockSpec((DK, 2048), lambda c: (c, 0)), blk(2048), blk(DN_H)],
        out_specs=[blk(2048), blk(2048), pl.BlockSpec((1, DN_H, DK, DK), lambda c: (c, 0, 0, 0))],
        out_shape=[SDS((rows, 2048), F32), SDS((rows, 2048), BF16), SDS((nch, DN_H, DK, DK), F32)],
        scratch_shapes=[pltpu.VMEM((DN_H, DK, DK), F32)],
        compiler_params=_cp("arbitrary"), name="dn_scan")(u, w, qe, kst, p, gc)


def dn_out_fwd(o, udn, ow, wout, h1, tgt):
    rows = o.shape[0]
    tm = _row_tile(rows)
    nt = rows // tm

    def body(o_ref, z_ref, ow_ref, w_ref, h_ref, t_ref, dh_ref, on_ref, ls_ref):
        for hv in range(DN_H):
            sl = slice(hv * DK, hv * DK + DK)
            oh = o_ref[:, sl]
            on_ref[:, sl] = (oh * _rms(oh) * ow_ref[...] * _silu(z_ref[:, sl])).astype(BF16)
        h2 = h_ref[...] + _dot(on_ref[...], w_ref[...])
        row = pl.program_id(0) * tm + lax.broadcasted_iota(jnp.int32, (tm, 1), 0)
        err = jnp.where(row >= BLK, h2 - t_ref[...], 0.0)
        dh_ref[...] = err * (1.0 / D_MODEL)
        ls_ref[0] = jnp.sum(err * err, axis=0, keepdims=True)

    return pl.pallas_call(
        body, grid=(nt,),
        in_specs=[pl.BlockSpec((tm, 2048), lambda i: (i, 0)), pl.BlockSpec((tm, 2048), lambda i: (i, 2)),
                  pl.BlockSpec((1, DK), lambda i: (0, 0)), pl.BlockSpec((2048, D_MODEL), lambda i: (0, 0)),
                  pl.BlockSpec((tm, D_MODEL), lambda i: (i, 0)), pl.BlockSpec((tm, D_MODEL), lambda i: (i, 0))],
        out_specs=[pl.BlockSpec((tm, D_MODEL), lambda i: (i, 0)), pl.BlockSpec((tm, 2048), lambda i: (i, 0)),
                   pl.BlockSpec((1, 1, D_MODEL), lambda i: (i, 0, 0))],
        out_shape=[SDS((rows, D_MODEL), F32), SDS((rows, 2048), BF16), SDS((nt, 1, D_MODEL), F32)],
        compiler_params=_cp("parallel"), name="dn_out_fwd")(o, udn, ow, wout, h1, tgt)


def dn_out_bwd(dh2, wout, o, udn, ow):
    rows = o.shape[0]
    tm = _row_tile(rows)
    nt = rows // tm

    def body(dh_ref, w_ref, o_ref, z_ref, ow_ref, do_ref, dz_ref, dow_ref):
        don = _dot(dh_ref[...].astype(BF16), w_ref[...], NT)
        ow_ = ow_ref[...]
        dow = jnp.zeros((1, DK), F32)
        for hv in range(DN_H):
            sl = slice(hv * DK, hv * DK + DK)
            oh = o_ref[:, sl]
            r = _rms(oh)
            y = oh * r
            z = z_ref[:, sl]
            dn = don[:, sl] * _silu(z)
            dz_ref[:, sl] = (don[:, sl] * (y * ow_) * _dsilu(z)).astype(BF16)
            dy = dn * ow_
            do_ref[:, sl] = r * (dy - y * jnp.mean(y * dy, axis=-1, keepdims=True))
            dow += jnp.sum(dn * y, axis=0, keepdims=True)
        dow_ref[0] = dow

    return pl.pallas_call(
        body, grid=(nt,),
        in_specs=[pl.BlockSpec((tm, D_MODEL), lambda i: (i, 0)), pl.BlockSpec((2048, D_MODEL), lambda i: (0, 0)),
                  pl.BlockSpec((tm, 2048), lambda i: (i, 0)), pl.BlockSpec((tm, 2048), lambda i: (i, 2)),
                  pl.BlockSpec((1, DK), lambda i: (0, 0))],
        out_specs=[pl.BlockSpec((tm, 2048), lambda i: (i, 0)), pl.BlockSpec((tm, 2048), lambda i: (i, 0)),
                   pl.BlockSpec((1, 1, DK), lambda i: (i, 0, 0))],
        out_shape=[SDS((rows, 2048), F32), SDS((rows, 2048), BF16), SDS((nt, 1, DK), F32)],
        compiler_params=_cp("parallel"), name="dn_out_bwd")(dh2, wout, o, udn, ow)


def dn_scan_bwd(do, qn, kn, sv, gc, beta, at, pt, u, w, vn, qet, wt, ks, st):
    rows = do.shape[0]
    nch = rows // CH

    def body(do_ref, q_ref, k_ref, v_ref, gc_ref, beta_ref, at_ref, pt_ref, u_ref, w_ref, vn_ref, qet_ref, wt_ref,
             ks_ref, st_ref, dq_ref, dk_ref, dv_ref, dbeta_ref, dg_ref, ds_scr, gct):
        @pl.when(pl.program_id(0) == 0)
        def _():
            ds_scr[...] = jnp.zeros_like(ds_scr)

        gc, beta = gc_ref[...], beta_ref[...]
        gct[...] = gc.T
        ii = lax.broadcasted_iota(jnp.int32, (CH, CH), 0)
        jj = lax.broadcasted_iota(jnp.int32, (CH, CH), 1)
        lane = lax.broadcasted_iota(jnp.int32, (CH, DN_H), 1)
        last = lax.broadcasted_iota(jnp.int32, (CH, 1), 0) == CH - 1

        def group(grp, carry):
            dbeta_acc, dgc_acc = carry
            base = grp * HB
            sls = [pl.ds(pl.multiple_of((base + i) * DK, DK), DK) for i in range(HB)]
            ksls = [pl.ds(pl.multiple_of((grp * (HB // 2) + j) * DK, DK), DK) for j in range(HB // 2)]
            heads = lambda ref: jnp.stack([ref[:, sl] for sl in sls])
            kheads = lambda ref: jnp.stack([ref[:, ksls[i // 2]] for i in range(HB)])
            cols = [_head_cols(base + i, beta, gc, gct, lane) for i in range(HB)]
            beta_c, gc_c, gc_r = (jnp.stack([c_[j] for c_ in cols]) for j in range(3))
            k, q, v = kheads(k_ref), kheads(q_ref), heads(v_ref)
            dec = jnp.exp(jnp.where(ii >= jj, gc_c - gc_r, NEG))
            eg = jnp.exp(gc_c)
            gl = gc_c[:, CH - 1:CH, :]
            e2 = jnp.exp(gl - gc_c)
            egl = jnp.exp(gl)
            k16, q16 = k.astype(BF16), q.astype(BF16)
            do16 = heads(do_ref).astype(BF16)
            s = st_ref[0, pl.ds(base, HB)]
            s16 = s.astype(BF16)
            dso = ds_scr[pl.ds(base, HB)]
            dso16 = dso.astype(BF16)
            wf, uf, vn16 = heads(w_ref), heads(u_ref), heads(vn_ref)
            kb = k * beta_c
            kb16 = kb.astype(BF16)
            pm = _bdot(q16, k16, "nt") * dec
            m = jnp.where(ii > jj, _bdot(kb16, k16, "nt") * dec, 0.0)
            dvn = _bdot(heads(pt_ref)[:, :, 0:CH], do16, "nn") + _bdot(heads(ks_ref), dso16, "nn")
            dvn16 = dvn.astype(BF16)
            ds_scr[pl.ds(base, HB)] = (egl * dso + _bdot(heads(qet_ref)[:, :, 0:CH], do16, "nn")
                                       - _bdot(heads(wt_ref)[:, :, 0:CH], dvn16, "nn"))
            dpm = jnp.where(ii >= jj, _bdot(do16, vn16, "nt"), 0.0)
            dqk16 = (dpm * dec).astype(BF16)
            dqe = _bdot(do16, s16, "nt")
            dq = eg * dqe + _bdot(dqk16, k16, "nn")
            dks = _bdot(vn16, dso16, "nt")
            dw = -_bdot(dvn16, s16, "nt")
            dbvk = _bdot(heads(at_ref)[:, :, 0:CH], jnp.concatenate([dvn, dw], axis=2), "nn", True)
            dbv, dbk = dbvk[:, :, :DK], dbvk[:, :, DK:]
            dm = jnp.where(ii > jj, -_bdot(dbvk, jnp.concatenate([uf, wf], axis=2), "nt", True), 0.0)
            g16 = (dm * dec).astype(BF16)
            dkb = _bdot(g16, k16, "nn")
            dk = (_bdot(dqk16, q16, "tn") + e2 * dks + _bdot(g16, kb16, "tn") + beta_c * (eg * dbk + dkb))
            e = dpm * pm + dm * m
            rsum = lambda x: jnp.sum(x, axis=2, keepdims=True)
            r_bk, r_qe, r_beta, r_ks = rsum(dbk * k), rsum(q * dqe), rsum(dbv * v + dkb * k), rsum(dks * k)
            t = r_ks * e2
            dgl = jnp.sum(t, axis=1, keepdims=True) + egl * rsum(jnp.sum(dso * s, axis=1, keepdims=True))
            deg = r_qe + beta_c * r_bk
            dgc = rsum(e) - t + deg * eg + jnp.where(last, dgl, 0.0)
            dgrow = -jnp.sum(e, axis=1, keepdims=True)
            dv = beta_c * dbv
            dbeta = r_beta + eg * r_bk
            for i in range(HB):
                dv_ref[:, sls[i]] = dv[i]
                sel = lane == base + i
                dbeta_acc = jnp.where(sel, dbeta[i], dbeta_acc)
                dgc_acc = jnp.where(sel, dgc[i], dgc_acc)
                gct[pl.ds(base + i, 1), :] = dgrow[i]
            for j in range(HB // 2):
                dq_ref[:, ksls[j]] = dq[2 * j] + dq[2 * j + 1]
                dk_ref[:, ksls[j]] = dk[2 * j] + dk[2 * j + 1]
            return dbeta_acc, dgc_acc

        zero = jnp.zeros((CH, DN_H), F32)
        dbeta_acc, dgc_acc = lax.fori_loop(0, DN_H // HB, group, (zero, zero))
        dbeta_ref[...] = dbeta_acc
        dg_ref[...] = _dot((ii <= jj).astype(F32), dgc_acc + gct[...].T, precision=HI)

    rev = lambda wd: pl.BlockSpec((CH, wd), lambda i: (nch - 1 - i, 0))
    rev_t = pl.BlockSpec((DK, 2048), lambda i: (nch - 1 - i, 0))
    return pl.pallas_call(
        body, grid=(nch,),
        in_specs=[rev(2048), rev(1024), rev(1024), rev(2048), rev(DN_H), rev(DN_H), rev(2048), rev(2048), rev(2048),
                  rev(2048), rev(2048), rev_t, rev_t, rev(2048),
                  pl.BlockSpec((1, DN_H, DK, DK), lambda i: (nch - 1 - i, 0, 0, 0))],
        out_specs=[rev(1024), rev(1024), rev(2048), rev(DN_H), rev(DN_H)],
        out_shape=[SDS((rows, 1024), F32), SDS((rows, 1024), F32), SDS((rows, 2048), F32), SDS((rows, DN_H), F32),
                   SDS((rows, DN_H), F32)],
        scratch_shapes=[pltpu.VMEM((DN_H, DK, DK), F32), pltpu.VMEM((DN_H, CH), F32)],
        compiler_params=_cp("arbitrary"), name="dn_scan_bwd")(
            do, qn, kn, sv, gc, beta, at, pt, u, w, vn, qet, wt, ks, st)


def dn_prep_bwd(udn, yconv, conv_w, a_log, dt_bias, dqn, dkn, dv, dbeta, dg):
    rows = udn.shape[0]
    nch = rows // CH
    ext = CH + 8

    def body(xc_ref, ba_ref, yc_ref, yn_ref, dqn_n, dkn_n, dv_n, cw_ref, al_ref, dtb_ref, dqn_ref, dkn_ref, dv_ref,
             dbeta_ref, dg_ref, dx_ref, dba_ref, dcw_ref, dal_ref, ddtb_ref):
        c = pl.program_id(0)
        first = c == 0
        own = (lax.broadcasted_iota(jnp.int32, (ext, 1), 0) < CH) | (c < nch - 1)

        @pl.when(first)
        def _():
            dcw_ref[...] = jnp.zeros_like(dcw_ref)
            dal_ref[...] = jnp.zeros_like(dal_ref)
            ddtb_ref[...] = jnp.zeros_like(ddtb_ref)

        real, xa, beta, g = _gates(ba_ref[...], al_ref[...], dtb_ref[...], c)
        dgm = jnp.where(real, dg_ref[...], 0.0)
        da = dgm * (-jnp.exp(al_ref[...])) * jax.nn.sigmoid(xa)
        dal_ref[...] += jnp.sum(dgm * g, axis=0, keepdims=True)
        ddtb_ref[...] += jnp.sum(da, axis=0, keepdims=True)
        dba_ref[...] = jnp.zeros_like(dba_ref)
        dba_ref[:, 0:DN_H] = jnp.where(real, dbeta_ref[...] * beta * (1.0 - beta), 0.0)
        dba_ref[:, DN_H:2 * DN_H] = da

        def through_conv(off, g_cur, g_next, grad_fn):
            sl = pl.ds(off, DK)
            y = jnp.concatenate([yc_ref[:, sl], yn_ref[:, sl]], axis=0)
            sg = jax.nn.sigmoid(y)
            dsilu = sg * (1.0 + y * (1.0 - sg))
            dy = jnp.where(own, grad_fn(y * sg, jnp.concatenate([g_cur, g_next], axis=0)) * dsilu, 0.0)
            shifted = [dy[3 - j:3 - j + CH] for j in range(4)]
            x = xc_ref[:, sl]
            dx = cw_ref[0:1, sl] * shifted[0]
            for j in range(1, 4):
                dx += cw_ref[j:j + 1, sl] * shifted[j]
            dx_ref[:, sl] = dx.astype(BF16)
            for j in range(4):
                dcw_ref[j:j + 1, sl] += jnp.sum(shifted[j] * x, axis=0, keepdims=True)

        def l2_bwd(scale):
            def f(s, gin):
                r = lax.rsqrt(jnp.sum(s * s, axis=-1, keepdims=True) + EPS)
                nrm = s * r
                return (r * scale) * (gin - nrm * jnp.sum(nrm * gin, axis=-1, keepdims=True))
            return f

        def qk_body(kh, carry):
            sl = pl.ds(pl.multiple_of(kh * DK, DK), DK)
            through_conv(pl.multiple_of(kh * DK, DK), dqn_ref[:, sl], dqn_n[:, sl], l2_bwd(DK ** -0.5))
            through_conv(pl.multiple_of(1024 + kh * DK, DK), dkn_ref[:, sl], dkn_n[:, sl], l2_bwd(1.0))
            return carry

        lax.fori_loop(0, DN_KH, qk_body, 0)

        def v_body(hv, carry):
            sl = pl.ds(pl.multiple_of(hv * DK, DK), DK)
            through_conv(pl.multiple_of(2048 + hv * DK, DK), dv_ref[:, sl], dv_n[:, sl], lambda s, gin: gin)
            return carry

        lax.fori_loop(0, DN_H, v_body, 0)

    full = lambda shape: pl.BlockSpec(shape, lambda c: (0, 0))
    blk = lambda w: pl.BlockSpec((CH, w), lambda c: (c, 0))
    nxt = lambda w: pl.BlockSpec((8, w), lambda c: (jnp.minimum(8 * c + 8, rows // 8 - 1), 0))
    return pl.pallas_call(
        body, grid=(nch,),
        in_specs=[_chunk_specs(0)[0], _chunk_specs(0)[2], blk(4096), nxt(4096), nxt(1024), nxt(1024), nxt(2048),
                  full((4, 4096)), full((1, DN_H)), full((1, DN_H)), blk(1024), blk(1024), blk(2048), blk(DN_H),
                  blk(DN_H)],
        out_specs=[blk(4096), blk(DK), full((8, 4096)), full((1, DN_H)), full((1, DN_H))],
        out_shape=[SDS((rows, 4096), BF16), SDS((rows, DK), F32), SDS((8, 4096), F32), SDS((1, DN_H), F32),
                   SDS((1, DN_H), F32)],
        compiler_params=_cp("arbitrary"), name="dn_prep_bwd")(
            udn, udn, yconv, yconv, dqn, dkn, dv, conv_w, a_log, dt_bias, dqn, dkn, dv, dbeta, dg)


def local_step(x, target, w):
    seq = x.shape[0]
    bf = lambda a: a.astype(BF16)
    h0 = jnp.concatenate([jnp.zeros((PAD, D_MODEL), F32), w["meta_tokens"], x], axis=0)
    tgt = jnp.concatenate([jnp.zeros((BLK, D_MODEL), F32), target], axis=0)
    win = w["attn_w_in"]
    wq, wkv, wg = win[:, :1024], win[:, 1024:1280], win[:, 1280:]
    wa_in = bf(jnp.concatenate([wq, wg, wkv], axis=1))
    wa_out = bf(w["attn_w_out"])
    wd_in = jnp.concatenate([bf(w["dn_w_in"]), jnp.zeros((D_MODEL, 96), BF16)], axis=1)
    wd_out = bf(w["dn_w_out"])
    qw, kw, sinks = w["attn_q_norm_w"], w["attn_k_norm_w"], w["attn_sinks"]
    cw, al, dtb, ow = w["dn_conv_w"], w["dn_a_log"], w["dn_dt_bias"], w["dn_o_norm_w"]

    ua, xn0 = norm_matmul(h0, w["attn_norm_w"], wa_in, 2304, "attn_in")
    og = attn_fwd(ua, qw, kw, sinks)
    h1 = matmul_residual(og, wa_out, h0, "attn_out")
    ud, xn1 = norm_matmul(h1, w["dn_norm_w"], wd_in, 6272, "dn_in")
    qn, kn, sv, gc, beta, u, wy, qe, ks, p, at, pt, qet, wt, kst, yconv = dn_prep(ud, cw, al, dtb)
    o, vn, st = dn_scan(u, wy, qe, kst, p, gc)
    dh2, on, ls = dn_out_fwd(o, ud, ow, wd_out, h1, tgt)
    loss = (0.5 / D_MODEL) * jnp.sum(ls)

    do, dz, dow = dn_out_bwd(dh2, wd_out, o, ud, ow)
    g_dn_out = wgrad(on, dh2, "dn_out_wgrad")
    dqn, dkn, dv, dbeta, dg = dn_scan_bwd(do, qn, kn, sv, gc, beta, at, pt, u, wy, vn, qet, wt, ks, st)
    dxc, dba, dcw, dal, ddtb = dn_prep_bwd(ud, yconv, cw, al, dtb, dqn, dkn, dv, dbeta, dg)
    dh1, dnw1 = in_proj_bwd([dxc, dz, dba], [wd_in[:, :4096], wd_in[:, 4096:6144], wd_in[:, 6144:]],
                            h1, w["dn_norm_w"], dh2, "dn_in_bwd")
    g_dn_in = jnp.concatenate([wgrad(xn1, dxc, "dn_in_wgrad_qkv"), wgrad(xn1, dz, "dn_in_wgrad_z"),
                               wgrad(xn1, dba, "dn_in_wgrad_ba")[:, :2 * DN_H]], axis=1)

    dog = matmul_nt(dh1, wa_out, "attn_out_bwd")
    g_attn_out = wgrad(og, dh1, "attn_out_wgrad")
    dq, dgate, dkv, dkvm, dqw, dkw, dsk = attn_bwd(ua, qw, kw, sinks, dog)
    dkv = dkv.at[PAD:BLK].add(dkvm)
    dh0, dnw0 = in_proj_bwd([dq, dgate, dkv], [wa_in[:, :1024], wa_in[:, 1024:2048], wa_in[:, 2048:]],
                            h0, w["attn_norm_w"], dh1, "attn_in_bwd")
    g_attn_in = jnp.concatenate([wgrad(xn0, dq, "attn_in_wgrad_q"), wgrad(xn0, dkv, "attn_in_wgrad_kv"),
                                 wgrad(xn0, dgate, "attn_in_wgrad_g")], axis=1)
    grads = {
        "meta_tokens": dh0[PAD:BLK], "attn_norm_w": jnp.sum(dnw0, axis=0), "attn_w_in": g_attn_in,
        "attn_q_norm_w": dqw, "attn_k_norm_w": dkw, "attn_sinks": dsk, "attn_w_out": g_attn_out,
        "dn_norm_w": jnp.sum(dnw1, axis=0), "dn_w_in": g_dn_in, "dn_conv_w": dcw[:4], "dn_a_log": dal,
        "dn_dt_bias": ddtb, "dn_o_norm_w": jnp.sum(dow, axis=0), "dn_w_out": g_dn_out,
    }
    return loss, dh0[BLK:BLK + seq], grads


WEIGHTS = ["meta_tokens", "attn_norm_w", "attn_w_in", "attn_q_norm_w", "attn_k_norm_w", "attn_sinks", "attn_w_out",
           "dn_norm_w", "dn_w_in", "dn_conv_w", "dn_a_log", "dn_dt_bias", "dn_o_norm_w", "dn_w_out"]
SHARDED = {"attn_w_in": ((1024, 2304), 1), "attn_w_out": ((1024, 1024), 0), "dn_w_in": ((1024, 6176), 1),
           "dn_w_out": ((2048, 1024), 0), "dn_conv_w": ((4, 4096), 1), "meta_tokens": ((16, 1024), 1),
           "dn_norm_w": ((1, 1024), 1)}
REPLICATED = {"attn_norm_w": 1024, "attn_q_norm_w": 64, "attn_k_norm_w": 64, "attn_sinks": 16, "dn_a_log": 16,
              "dn_dt_bias": 16, "dn_o_norm_w": 128}
N_CHIPS = 4
PACK_ROWS = 2912
HALF_ROWS = PACK_ROWS // 2
SMALL_ROWS = 8


def _shard_shape(name):
    (r, c), axis = SHARDED[name]
    return (r // N_CHIPS, c) if axis == 0 else (r, c // N_CHIPS)


def _pack(parts, rows):
    flat = jnp.concatenate([p.reshape(-1) for p in parts])
    return jnp.pad(flat, (0, rows * 1024 - flat.shape[0])).reshape(rows, 1024)


def pack_shard(shards):
    return _pack([shards[n] for n in SHARDED], PACK_ROWS)


def unpack_shard(buf):
    flat, out, pos = buf.reshape(-1), {}, 0
    for n in SHARDED:
        shp = _shard_shape(n)
        size = shp[0] * shp[1]
        out[n] = flat[pos:pos + size].reshape(shp)
        pos += size
    return out


MATRICES = ("attn_w_in", "attn_w_out", "dn_w_in", "dn_w_out")


def pack_gather(shards):
    big = [shards[n].astype(BF16).reshape(-1) for n in MATRICES]
    small = jnp.concatenate([shards[n].reshape(-1) for n in SHARDED if n not in MATRICES])
    flat = jnp.concatenate(big + [lax.bitcast_convert_type(small, BF16).reshape(-1)])
    return jnp.pad(flat, (0, PACK_ROWS * 1024 - flat.shape[0])).reshape(PACK_ROWS, 1024)


def unpack_gather(buf):
    PER_F32 = 4 // jnp.dtype(buf.dtype).itemsize
    flat, out, pos = buf.reshape(-1), {}, 0
    for n in MATRICES:
        shp = _shard_shape(n)
        out[n] = flat[pos:pos + shp[0] * shp[1]].reshape(shp)
        pos += shp[0] * shp[1]
    for n in SHARDED:
        if n not in MATRICES:
            shp = _shard_shape(n)
            raw = flat[pos:pos + shp[0] * shp[1] * PER_F32]
            out[n] = lax.bitcast_convert_type(raw.reshape(-1, PER_F32) if PER_F32 > 1 else raw, F32).reshape(shp)
            pos += shp[0] * shp[1] * PER_F32
    return out


def pack_small(vals):
    return _pack([vals[n] for n in REPLICATED], SMALL_ROWS)


def unpack_small(buf):
    flat, out, pos = buf.reshape(-1), {}, 0
    for n, size in REPLICATED.items():
        out[n] = flat[pos:pos + size].reshape(1, size)
        pos += size
    return out


ANY = pl.BlockSpec(memory_space=pl.ANY)


def _place():
    return lax.axis_index("x"), lax.axis_index("y"), lax.axis_index("c")


def chips_exchange(src, gather):
    r = src.shape[-2]

    def body(s_ref, o_ref, send_sems, recv_sems):
        x, y, c = _place()
        me = 2 * x + y
        peers = [(1 - x, y), (x, 1 - y), (1 - x, 1 - y)]

        def copy(k, to_block, from_block):
            px, py = peers[k]
            return pltpu.make_async_remote_copy(
                src_ref=s_ref if gather else s_ref.at[to_block], dst_ref=o_ref.at[from_block],
                send_sem=send_sems.at[k], recv_sem=recv_sems.at[k], device_id=(px, py, c), device_id_type=MESH)

        sends = [copy(k, 2 * px + py, me) for k, (px, py) in enumerate(peers)]
        for cp in sends:
            cp.start()
        for k, (px, py) in enumerate(peers):
            copy(k, me, 2 * px + py).wait_recv()
        for cp in sends:
            cp.wait_send()

    return pl.pallas_call(
        body, in_specs=[ANY], out_specs=ANY, out_shape=SDS((N_CHIPS, r, 1024), src.dtype),
        scratch_shapes=[pltpu.SemaphoreType.DMA((3,)), pltpu.SemaphoreType.DMA((3,))],
        name="chips_gather" if gather else "chips_exchange")(src)


def chip_sum(received, pair, me):
    tm = 208

    def body(me_ref, own_ref, r1_ref, r2_ref, r3_ref, o_ref):
        o_ref[...] = ((own_ref[0] + r1_ref[0]) + r2_ref[0]) + r3_ref[0]

    blk = lambda k: pl.BlockSpec((1, tm, 1024), lambda i, me_ref: ((me_ref[0] + k) % N_CHIPS, i, 0))
    return pl.pallas_call(
        body,
        grid_spec=pltpu.PrefetchScalarGridSpec(
            num_scalar_prefetch=1, grid=(HALF_ROWS // tm,), in_specs=[blk(0), blk(1), blk(2), blk(3)],
            out_specs=pl.BlockSpec((tm, 1024), lambda i, me_ref: (i, 0))),
        out_shape=SDS((HALF_ROWS, 1024), F32), compiler_params=_cp("parallel"), name="chip_sum")(
            me.reshape(1).astype(jnp.int32), pair, received, received, received)


def _rows_at(ref, start, size):
    return ref.at[:, pl.ds(start, size), :] if len(ref.shape) == 3 else ref.at[pl.ds(start, size), :]


def sibling_join(src, name):
    axis = len(src.shape) - 2

    def body(s_ref, o_ref, send_sem, recv_sem):
        x, y, c = _place()
        cp = pltpu.make_async_remote_copy(src_ref=s_ref, dst_ref=o_ref, send_sem=send_sem, recv_sem=recv_sem,
                                          device_id=(x, y, 1 - c), device_id_type=MESH)
        cp.start()
        cp.wait()

    theirs = pl.pallas_call(
        body, in_specs=[ANY], out_specs=ANY, out_shape=SDS(src.shape, src.dtype),
        scratch_shapes=[pltpu.SemaphoreType.DMA, pltpu.SemaphoreType.DMA], name=name)(src)
    first = lax.axis_index("c") == 0
    return jnp.concatenate([jnp.where(first, src, theirs), jnp.where(first, theirs, src)], axis=axis)


def sibling_give(g_all):
    def body(s_ref, o_ref, send_sem, recv_sem):
        x, y, c = _place()
        cp = pltpu.make_async_remote_copy(
            src_ref=_rows_at(s_ref, (1 - c) * HALF_ROWS, HALF_ROWS), dst_ref=o_ref, send_sem=send_sem,
            recv_sem=recv_sem, device_id=(x, y, 1 - c), device_id_type=MESH)
        cp.start()
        cp.wait()

    return pl.pallas_call(
        body, in_specs=[ANY], out_specs=ANY, out_shape=SDS((N_CHIPS, HALF_ROWS, 1024), F32),
        scratch_shapes=[pltpu.SemaphoreType.DMA, pltpu.SemaphoreType.DMA], name="pair_exchange")(g_all)


def pair_sum(g_all, got, c):
    tm = 208
    per_half = HALF_ROWS // tm

    def body(c_ref, a_ref, b_ref, o_ref):
        o_ref[...] = a_ref[...] + b_ref[...]

    return pl.pallas_call(
        body,
        grid_spec=pltpu.PrefetchScalarGridSpec(
            num_scalar_prefetch=1, grid=(N_CHIPS, per_half),
            in_specs=[pl.BlockSpec((1, tm, 1024), lambda j, i, c_ref: (j, c_ref[0] * per_half + i, 0)),
                      pl.BlockSpec((1, tm, 1024), lambda j, i, c_ref: (j, i, 0))],
            out_specs=pl.BlockSpec((1, tm, 1024), lambda j, i, c_ref: (j, i, 0))),
        out_shape=SDS((N_CHIPS, HALF_ROWS, 1024), F32),
        compiler_params=_cp("parallel", "parallel"), name="pair_sum")(c.reshape(1).astype(jnp.int32), g_all, got)


def all_gather_small(src):
    def body(s_ref, o_ref, send_sems, recv_sems, local_sem):
        x, y, c = _place()
        flips = [(fx, fy, fc) for fx in (0, 1) for fy in (0, 1) for fc in (0, 1)][1:]
        idx = lambda px, py, pc: 4 * px + 2 * py + pc
        mine = pltpu.make_async_copy(s_ref, o_ref.at[idx(x, y, c)], local_sem)
        mine.start()

        def peer(k):
            fx, fy, fc = flips[k]
            return (1 - x if fx else x, 1 - y if fy else y, 1 - c if fc else c)

        def copy(k, block):
            return pltpu.make_async_remote_copy(
                src_ref=s_ref, dst_ref=o_ref.at[block], send_sem=send_sems.at[k], recv_sem=recv_sems.at[k],
                device_id=peer(k), device_id_type=MESH)

        sends = [copy(k, idx(x, y, c)) for k in range(7)]
        for cp in sends:
            cp.start()
        for k in range(7):
            copy(k, idx(*peer(k))).wait_recv()
        for cp in sends:
            cp.wait_send()
        mine.wait()

    return pl.pallas_call(
        body, in_specs=[ANY], out_specs=ANY, out_shape=SDS((8,) + src.shape, F32),
        scratch_shapes=[pltpu.SemaphoreType.DMA((7,)), pltpu.SemaphoreType.DMA((7,)), pltpu.SemaphoreType.DMA],
        name="all_gather_small")(src)


def sum_blocks(t, name):
    n, r, _ = t.shape
    tm = 208 if r % 208 == 0 else r

    def body(t_ref, o_ref):
        acc = t_ref[0]
        for i in range(1, n):
            acc = acc + t_ref[i]
        o_ref[...] = acc

    return pl.pallas_call(
        body, grid=(r // tm,), in_specs=[pl.BlockSpec((n, tm, 1024), lambda i: (0, i, 0))],
        out_specs=pl.BlockSpec((tm, 1024), lambda i: (i, 0)), out_shape=SDS((r, 1024), F32),
        compiler_params=_cp("parallel"), name=name)(t)


ADAM_BLOCK_BYTES = 1024 * 1024


def adamw(w, g, m, v, name):
    rows, cols = w.shape
    tm = rows
    while tm * cols * 4 > ADAM_BLOCK_BYTES and tm % 16 == 0:
        tm //= 2

    def body(w_ref, g_ref, m_ref, v_ref, d_ref, nm_ref, nv_ref):
        g_ = g_ref[...]
        m_ = ADAM_B1 * m_ref[...] + (1.0 - ADAM_B1) * g_
        v_ = ADAM_B2 * v_ref[...] + (1.0 - ADAM_B2) * (g_ * g_)
        m_hat = m_ / (1.0 - ADAM_B1 ** ADAM_STEP)
        v_hat = v_ / (1.0 - ADAM_B2 ** ADAM_STEP)
        d_ref[...] = -ADAM_LR * (m_hat / (jnp.sqrt(v_hat) + ADAM_EPS) + ADAM_WD * w_ref[...])
        nm_ref[...] = m_
        nv_ref[...] = v_

    spec = pl.BlockSpec((tm, cols), lambda i: (i, 0))
    return pl.pallas_call(
        body, grid=(rows // tm,), in_specs=[spec] * 4, out_specs=[spec] * 3,
        out_shape=[SDS((rows, cols), F32)] * 3, compiler_params=_cp("parallel"), name=name)(w, g, m, v)


LAYERED = ("attn_w_in", "attn_w_out", "dn_w_in", "dn_conv_w", "dn_w_out")


def _two_d(name, a):
    return a[0] if name in LAYERED else a


def kernel(x, meta_tokens, attn_norm_w, attn_w_in, attn_q_norm_w, attn_k_norm_w, attn_sinks, attn_w_out, dn_norm_w, dn_w_in, dn_conv_w, dn_a_log, dn_dt_bias, dn_o_norm_w, dn_w_out, loss_target, m_meta_tokens, m_attn_norm_w, m_attn_w_in, m_attn_q_norm_w, m_attn_k_norm_w, m_attn_sinks, m_attn_w_out, m_dn_norm_w, m_dn_w_in, m_dn_conv_w, m_dn_a_log, m_dn_dt_bias, m_dn_o_norm_w, m_dn_w_out, v_meta_tokens, v_attn_norm_w, v_attn_w_in, v_attn_q_norm_w, v_attn_k_norm_w, v_attn_sinks, v_attn_w_out, v_dn_norm_w, v_dn_w_in, v_dn_conv_w, v_dn_a_log, v_dn_dt_bias, v_dn_o_norm_w, v_dn_w_out):
    given = dict(zip(WEIGHTS, (meta_tokens, attn_norm_w, attn_w_in, attn_q_norm_w, attn_k_norm_w, attn_sinks,
                               attn_w_out, dn_norm_w, dn_w_in, dn_conv_w, dn_a_log, dn_dt_bias, dn_o_norm_w, dn_w_out)))
    mom1 = dict(zip(WEIGHTS, (m_meta_tokens, m_attn_norm_w, m_attn_w_in, m_attn_q_norm_w, m_attn_k_norm_w,
                              m_attn_sinks, m_attn_w_out, m_dn_norm_w, m_dn_w_in, m_dn_conv_w, m_dn_a_log,
                              m_dn_dt_bias, m_dn_o_norm_w, m_dn_w_out)))
    mom2 = dict(zip(WEIGHTS, (v_meta_tokens, v_attn_norm_w, v_attn_w_in, v_attn_q_norm_w, v_attn_k_norm_w,
                              v_attn_sinks, v_attn_w_out, v_dn_norm_w, v_dn_w_in, v_dn_conv_w, v_dn_a_log,
                              v_dn_dt_bias, v_dn_o_norm_w, v_dn_w_out)))
    two_d = lambda d: {n: _two_d(n, a) for n, a in d.items()}
    given, mom1, mom2 = two_d(given), two_d(mom1), two_d(mom2)
    c = lax.axis_index("c")

    me = 2 * lax.axis_index("x") + lax.axis_index("y")
    own_half = lax.dynamic_slice_in_dim(pack_gather(given), c * HALF_ROWS, HALF_ROWS, axis=0)
    mine = lax.dynamic_update_slice_in_dim(chips_exchange(own_half, True), own_half[None], me, 0)
    gathered = sibling_join(mine, "gather_swap")
    per_chip = [unpack_gather(gathered[j]) for j in range(N_CHIPS)]
    full = {n: jnp.concatenate([pc[n] for pc in per_chip], axis=SHARDED[n][1]) for n in SHARDED}
    full.update({n: given[n] for n in REPLICATED})

    loss, dx, grads = local_step(x[0], loss_target[0], full)

    split = lambda n: jnp.split(grads[n], N_CHIPS, axis=SHARDED[n][1])
    g_all = jnp.stack([pack_shard({n: split(n)[j] for n in SHARDED}) for j in range(N_CHIPS)])
    pair = pair_sum(g_all, sibling_give(g_all), c)
    half = chip_sum(chips_exchange(pair, False), pair, me)
    g_shard = sibling_join(half, "half_exchange")

    g_small = sum_blocks(all_gather_small(pack_small(grads)), "small_sum")

    g_local = unpack_shard(g_shard)
    g_local.update(unpack_small(g_small))
    steps = {n: adamw(given[n], g_local[n], mom1[n], mom2[n], "adamw_" + n) for n in WEIGHTS}
    shaped = lambda n, a: a[None] if n in LAYERED else a
    outs = [[shaped(n, g_local[n]) for n in WEIGHTS]]
    outs += [[shaped(n, steps[n][k]) for n in WEIGHTS] for k in range(3)]

    loss = lax.psum(loss, ("x", "y", "c"))
    return (loss, dx[None], *outs[0], *outs[1], *outs[2], *outs[3])
```

```python
import functools

import jax
import jax.numpy as jnp
from jax import lax
from jax.experimental import pallas as pl
from jax.experimental.pallas import tpu as pltpu

F32 = jnp.float32
BF16 = jnp.bfloat16
SDS = jax.ShapeDtypeStruct
MESH = pl.DeviceIdType.MESH

D_MODEL = 1024
N_META = 16
EPS = 1e-6
BLK = 128
CH = 64
PAD = BLK - N_META
HEADS = 16
HD = 64
KVW = 256
DN_H = 16
DN_KH = 8
DK = 128
SLOPES = [2.0 ** (-8.0 * (h + 1) / HEADS) for h in range(HEADS)]
NEG = -1e30
NT = (((1,), (1,)), ((), ()))
TN = (((0,), (0,)), ((), ()))
HI = lax.Precision.HIGHEST

ADAM_LR, ADAM_B1, ADAM_B2, ADAM_EPS, ADAM_WD, ADAM_STEP = 0.001, 0.9, 0.999, 1e-08, 0.01, 10

VMEM_LIMIT = 56 * 1024 * 1024
MXU_DEPTH = 256
WGRAD_BLOCK_ELEMS = 2 * 1024 * 1024


def _cp(*sem):
    return pltpu.CompilerParams(dimension_semantics=sem, vmem_limit_bytes=VMEM_LIMIT)


def _row_tile(rows):
    for t in (384, 256, 128):
        if rows % t == 0:
            return t
    raise ValueError(rows)


def _dot(a, b, dims=None, precision=None):
    if dims is None:
        return jnp.dot(a, b, preferred_element_type=F32, precision=precision)
    return lax.dot_general(a, b, dims, preferred_element_type=F32, precision=precision)


def _silu(x):
    return x * jax.nn.sigmoid(x)


def _dsilu(x):
    s = jax.nn.sigmoid(x)
    return s * (1.0 + x * (1.0 - s))


def _rms(x):
    return lax.rsqrt(jnp.mean(x * x, axis=-1, keepdims=True) + EPS)


def norm_matmul(h, nw, w, tn, name):
    rows, k = h.shape
    n = w.shape[1]
    tm = _row_tile(rows)

    def norm_body(h_ref, nw_ref, xn_ref):
        x = h_ref[...]
        xn_ref[...] = (x * _rms(x) * nw_ref[...]).astype(BF16)

    xn = pl.pallas_call(
        norm_body, grid=(rows // tm,),
        in_specs=[pl.BlockSpec((tm, k), lambda i: (i, 0)), pl.BlockSpec((1, k), lambda i: (0, 0))],
        out_specs=pl.BlockSpec((tm, k), lambda i: (i, 0)), out_shape=SDS((rows, k), BF16),
        compiler_params=_cp("parallel"), name=name + "_norm")(h, nw)

    def body(a_ref, w_ref, o_ref):
        o_ref[...] = _dot(a_ref[...], w_ref[...])

    out = pl.pallas_call(
        body, grid=(n // tn, rows // tm),
        in_specs=[pl.BlockSpec((tm, k), lambda j, i: (i, 0)), pl.BlockSpec((k, tn), lambda j, i: (0, j))],
        out_specs=pl.BlockSpec((tm, tn), lambda j, i: (i, j)), out_shape=SDS((rows, n), F32),
        compiler_params=_cp("parallel", "parallel"), name=name)(xn, w)
    return out, xn


def matmul_residual(a, w, res, name):
    rows, k = a.shape
    n = w.shape[1]
    tm = _row_tile(rows)

    def body(a_ref, w_ref, r_ref, o_ref):
        o_ref[...] = r_ref[...] + _dot(a_ref[...], w_ref[...])

    return pl.pallas_call(
        body, grid=(rows // tm,),
        in_specs=[pl.BlockSpec((tm, k), lambda i: (i, 0)), pl.BlockSpec((k, n), lambda i: (0, 0)),
                  pl.BlockSpec((tm, n), lambda i: (i, 0))],
        out_specs=pl.BlockSpec((tm, n), lambda i: (i, 0)),
        out_shape=SDS((rows, n), F32), compiler_params=_cp("parallel"), name=name)(a, w, res)


def wgrad(a, b, name):
    rows, k = a.shape
    n = b.shape[1]
    tm = _row_tile(rows)
    tn = min(n, WGRAD_BLOCK_ELEMS // k)

    def body(a_ref, b_ref, o_ref):
        @pl.when(pl.program_id(1) == 0)
        def _():
            o_ref[...] = jnp.zeros_like(o_ref)

        o_ref[...] += _dot(a_ref[...], b_ref[...].astype(BF16), TN)

    return pl.pallas_call(
        body, grid=(n // tn, rows // tm),
        in_specs=[pl.BlockSpec((tm, k), lambda j, i: (i, 0)), pl.BlockSpec((tm, tn), lambda j, i: (i, j))],
        out_specs=pl.BlockSpec((k, tn), lambda j, i: (0, j)),
        out_shape=SDS((k, n), F32), compiler_params=_cp("parallel", "arbitrary"), name=name)(a, b)


def in_proj_bwd(dus, ws, h, nw, dh_next, name):
    rows, k = h.shape
    tm = _row_tile(rows)
    nd = len(dus)
    nt = rows // tm

    def body(*refs):
        du_refs, w_refs = refs[:nd], refs[nd:2 * nd]
        h_ref, nw_ref, dhn_ref, dh_ref, dnw_ref = refs[2 * nd:]
        dxn = _dot(du_refs[0][...].astype(BF16), w_refs[0][...], NT)
        for du_ref, w_ref in zip(du_refs[1:], w_refs[1:]):
            dxn += _dot(du_ref[...].astype(BF16), w_ref[...], NT)
        x = h_ref[...]
        r = _rms(x)
        y = x * r
        gy = dxn * nw_ref[...]
        dh_ref[...] = dhn_ref[...] + r * (gy - y * jnp.mean(y * gy, axis=-1, keepdims=True))
        dnw_ref[0] = jnp.sum(dxn * y, axis=0, keepdims=True)

    in_specs = [pl.BlockSpec((tm, du.shape[1]), lambda i: (i, 0)) for du in dus]
    in_specs += [pl.BlockSpec(w.shape, lambda i: (0, 0)) for w in ws]
    in_specs += [pl.BlockSpec((tm, k), lambda i: (i, 0)), pl.BlockSpec((1, k), lambda i: (0, 0)),
                 pl.BlockSpec((tm, k), lambda i: (i, 0))]
    return pl.pallas_call(
        body, grid=(nt,), in_specs=in_specs,
        out_specs=[pl.BlockSpec((tm, k), lambda i: (i, 0)), pl.BlockSpec((1, 1, k), lambda i: (i, 0, 0))],
        out_shape=[SDS((rows, k), F32), SDS((nt, 1, k), F32)],
        compiler_params=_cp("parallel"), name=name)(*dus, *ws, h, nw, dh_next)


def matmul_nt(a, w, name):
    rows, k = a.shape
    n = w.shape[0]
    tm = _row_tile(rows)

    def body(a_ref, w_ref, o_ref):
        o_ref[...] = _dot(a_ref[...].astype(BF16), w_ref[...], NT)

    return pl.pallas_call(
        body, grid=(rows // tm,),
        in_specs=[pl.BlockSpec((tm, k), lambda i: (i, 0)), pl.BlockSpec((n, k), lambda i: (0, 0))],
        out_specs=pl.BlockSpec((tm, n), lambda i: (i, 0)),
        out_shape=SDS((rows, n), F32), compiler_params=_cp("parallel"), name=name)(a, w)


SUB = 64
GRP = 8
TR = GRP * SUB
NBAND = 192
TK = 256


def _tile_bias(n, sb):
    r = lax.broadcasted_iota(jnp.int32, (TR, TK), 0)
    c = lax.broadcasted_iota(jnp.int32, (TR, TK), 1)
    qi = r & (SUB - 1)
    d = BLK + qi - c
    dm = n * BLK + SUB * sb - PAD + NBAND + qi - c
    band = c < NBAND
    valid = (band & (d >= 0) & (d < BLK) & (c >= 2 * BLK - BLK * n - SUB * sb)) | (
        (c >= NBAND) & (c < NBAND + N_META) & (dm >= 0))
    return valid, jnp.where(band, d, jnp.minimum(dm, BLK)).astype(F32)


def _group_col(vals):
    g = lax.broadcasted_iota(jnp.int32, (TR, 1), 0) >> 6
    col = jnp.zeros((TR, 1), F32)
    for gi, v in enumerate(vals):
        col = jnp.where(g == gi, v, col)
    return col


def _stack_heads(ref, sb, kvh):
    return jnp.concatenate(
        [ref[SUB * sb:SUB * sb + SUB, HD * (GRP * kvh + g):HD * (GRP * kvh + g) + HD] for g in range(GRP)], axis=0)


def _unstack_heads(parts):
    return jnp.concatenate([parts[kvh][SUB * g:SUB * g + SUB] for kvh in range(2) for g in range(GRP)], axis=1)


def _tile_keys(band, meta, sb):
    return jnp.concatenate([band[SUB * sb:SUB * sb + NBAND], meta,
                            jnp.zeros((TK - NBAND - N_META, HD), band.dtype)], axis=0)


def _row_sums(x):
    ones = jnp.ones((x.shape[1], 128), BF16)
    hi = x.astype(BF16)
    lo = (x - hi.astype(F32)).astype(BF16)
    return _dot(hi, ones) + _dot(lo, ones)


def _rms_stack(q):
    return lax.rsqrt(_row_sums(q * q)[:, :HD] * (1.0 / HD) + EPS)


def _fill_bias(bias_scr, n):
    @pl.when(n <= 2)
    def _():
        for sb in range(2):
            valid, dist = _tile_bias(n, sb)
            for kvh in range(2):
                slope_col = _group_col([SLOPES[GRP * kvh + g] for g in range(GRP)])
                bias_scr[2 * sb + kvh] = jnp.where(valid, -slope_col * dist, NEG)


def _tile_vals(band, meta, sb):
    return jnp.concatenate([_tile_keys(band, meta, sb), jnp.ones((TK, 3 * HD), BF16)], axis=1)


def _tile_softmax(qn16, k16, vx16, bias, sink_col):
    s = _dot(qn16, k16, NT) * (HD ** -0.5) + bias
    mx = jnp.maximum(jnp.max(s.astype(BF16), axis=-1, keepdims=True).astype(F32), sink_col)
    e = jnp.exp(s - mx)
    es = jnp.exp(sink_col - mx)
    ox = _dot(e.astype(BF16), vx16)
    return e, 1.0 / (ox[:, 2 * HD:] + es), es, ox[:, :HD]


def _kv_heads(kvb, kvm, kw_):
    out = []
    for kvh in range(2):
        kb, km = kvb[:, HD * kvh:HD * kvh + HD], kvm[:, HD * kvh:HD * kvh + HD]
        out.append(((kb * _rms(kb) * kw_).astype(BF16), (km * _rms(km) * kw_).astype(BF16),
                    kvb[:, BLK + HD * kvh:BLK + HD * kvh + HD].astype(BF16),
                    kvm[:, BLK + HD * kvh:BLK + HD * kvh + HD].astype(BF16)))
    return out


def _kv_specs(nblk, clamp):
    cur = (lambda n: (jnp.minimum(n, nblk - 1), 8)) if clamp else (lambda n: (n, 8))
    return [pl.BlockSpec((BLK, KVW), cur),
            pl.BlockSpec((BLK, KVW), lambda n: (jnp.maximum(n - 1, 0), 8)),
            pl.BlockSpec((N_META, KVW), lambda n: (PAD // N_META, 8))]


def _sink_cols(sinks):
    return jnp.repeat(sinks.reshape(2, GRP), SUB, axis=1).reshape(2, TR, 1)


SINK_SPEC = pl.BlockSpec((2, TR, 1), lambda n: (0, 0, 0))


def attn_fwd(u, qw, kw, sinks):
    rows = u.shape[0]
    nblk = rows // BLK

    def body(q_ref, g_ref, kvc_ref, kvp_ref, kvm_ref, qw_ref, kw_ref, sc_ref, og_ref, bias_scr):
        _fill_bias(bias_scr, pl.program_id(0))
        qw_ = qw_ref[...]
        kv = _kv_heads(jnp.concatenate([kvp_ref[...], kvc_ref[...]], axis=0), kvm_ref[...], kw_ref[...])
        for sb in range(2):
            parts = []
            for kvh in range(2):
                knb, knm, vb, vm = kv[kvh]
                q = _stack_heads(q_ref, sb, kvh)
                qn16 = (q * _rms_stack(q) * qw_).astype(BF16)
                _, inv, _, o = _tile_softmax(qn16, _tile_keys(knb, knm, sb), _tile_vals(vb, vm, sb),
                                             bias_scr[2 * sb + kvh], sc_ref[kvh])
                parts.append(o * inv[:, :HD])
            rows = slice(SUB * sb, SUB * sb + SUB)
            og_ref[rows, :] = (_unstack_heads(parts) * _silu(g_ref[rows, :])).astype(BF16)

    small = lambda w: pl.BlockSpec((1, w), lambda n: (0, 0))
    return pl.pallas_call(
        body, grid=(nblk,),
        in_specs=[pl.BlockSpec((BLK, 1024), lambda n: (n, 0)), pl.BlockSpec((BLK, 1024), lambda n: (n, 1))]
        + _kv_specs(nblk, False) + [small(HD), small(HD), SINK_SPEC],
        out_specs=pl.BlockSpec((BLK, 1024), lambda n: (n, 0)),
        out_shape=SDS((rows, 1024), BF16), scratch_shapes=[pltpu.VMEM((4, TR, TK), F32)],
        compiler_params=_cp("arbitrary"), name="attn_fwd")(u, u, u, u, u, qw, kw, _sink_cols(sinks))


def attn_bwd(u, qw, kw, sinks, dog):
    rows = u.shape[0]
    nblk = rows // BLK

    def knorm_bwd(k, dkn, kw_):
        r = _rms(k)
        y = k * r
        gy = dkn * kw_
        return r * (gy - y * jnp.mean(y * gy, axis=-1, keepdims=True)), jnp.sum(dkn * y, axis=0, keepdims=True)

    def body(q_ref, g_ref, dog_ref, kvc_ref, kvp_ref, kvm_ref, qw_ref, kw_ref, sc_ref,
             dq_ref, dg_ref, dkv_ref, dkvm_ref, dqw_ref, dkw_ref, dsk_ref, carry, prevp, curp, metap, bias_scr):
        n = pl.program_id(0)
        qw_, kw_ = qw_ref[...], kw_ref[...]
        _fill_bias(bias_scr, n)

        @pl.when(n == 0)
        def _():
            carry[...] = jnp.zeros_like(carry)
            metap[...] = jnp.zeros_like(metap)
            dqw_ref[...] = jnp.zeros_like(dqw_ref)
            dkw_ref[...] = jnp.zeros_like(dkw_ref)
            dsk_ref[...] = jnp.zeros_like(dsk_ref)

        @pl.when(n == nblk)
        def _():
            prevp[...] = jnp.zeros_like(prevp)
            curp[...] = jnp.zeros_like(curp)

        @pl.when(n < nblk)
        def _():
            kv = _kv_heads(jnp.concatenate([kvp_ref[...], kvc_ref[...]], axis=0), kvm_ref[...], kw_)
            lane = lax.broadcasted_iota(jnp.int32, (1, HEADS), 1)
            dqw = jnp.zeros((1, HD), F32)
            dsk = jnp.zeros((1, HEADS), F32)
            band_parts = [jnp.zeros((2 * BLK, HD), F32) for _ in range(4)]
            meta_parts = [jnp.zeros((N_META, HD), F32) for _ in range(4)]

            def widen(x, sb):
                z = jnp.zeros((2 * BLK - NBAND, HD), F32)
                return jnp.concatenate([x, z] if sb == 0 else [z, x], axis=0)

            for sb in range(2):
                rows = slice(SUB * sb, SUB * sb + SUB)
                dq_parts, dg_parts = [], []
                for kvh in range(2):
                    knb, knm, vb, vm = kv[kvh]
                    k16, v16 = _tile_keys(knb, knm, sb), _tile_keys(vb, vm, sb)
                    q = _stack_heads(q_ref, sb, kvh)
                    r = _rms_stack(q)
                    y = q * r
                    qn16 = (y * qw_).astype(BF16)
                    e, inv, es, o = _tile_softmax(qn16, k16, _tile_vals(vb, vm, sb), bias_scr[2 * sb + kvh],
                                                  sc_ref[kvh])
                    p = e * jnp.concatenate([inv, inv], axis=1)
                    p16 = p.astype(BF16)
                    o = o * inv[:, :HD]
                    gate = _stack_heads(g_ref, sb, kvh)
                    dog_ = _stack_heads(dog_ref, sb, kvh)
                    dg_parts.append(dog_ * o * _dsilu(gate))
                    do_ = dog_ * _silu(gate)
                    do16 = do_.astype(BF16)
                    dp = _dot(do16, v16, NT)
                    delta = _row_sums(do_ * o)
                    ds16 = (p * (dp - jnp.concatenate([delta, delta], axis=1))).astype(BF16)
                    dsink = -(es * inv) * delta
                    for g in range(GRP):
                        dsk += jnp.where(lane == GRP * kvh + g,
                                         jnp.sum(dsink[SUB * g:SUB * g + SUB, :HEADS], axis=0, keepdims=True), 0.0)
                    dqn = _dot(ds16, k16) * (HD ** -0.5)
                    dk = (_dot((y * qw_).T.astype(BF16), ds16) * (HD ** -0.5)).T
                    dv = _dot(do_.T.astype(BF16), p16).T
                    band_parts[kvh] += widen(dk[:NBAND], sb)
                    band_parts[2 + kvh] += widen(dv[:NBAND], sb)
                    meta_parts[kvh] += dk[NBAND:NBAND + N_META]
                    meta_parts[2 + kvh] += dv[NBAND:NBAND + N_META]
                    gy = dqn * qw_
                    dq_parts.append(r * (gy - y * (_row_sums(y * gy)[:, :HD] * (1.0 / HD))))
                    dqw += jnp.sum(dqn * y, axis=0, keepdims=True)
                dq_ref[rows, :] = _unstack_heads(dq_parts).astype(BF16)
                dg_ref[rows, :] = _unstack_heads(dg_parts).astype(BF16)
            band = jnp.concatenate(band_parts, axis=1)
            prevp[...] = band[:BLK]
            curp[...] = band[BLK:]
            metap[...] += jnp.concatenate(meta_parts, axis=1)
            dqw_ref[...] += dqw
            dsk_ref[...] += dsk

        tot = carry[...] + prevp[...]
        kprev = kvp_ref[...]
        dk0, w0 = knorm_bwd(kprev[:, 0:HD], tot[:, 0:HD], kw_)
        dk1, w1 = knorm_bwd(kprev[:, HD:2 * HD], tot[:, HD:2 * HD], kw_)
        dkv_ref[...] = jnp.concatenate([dk0, dk1, tot[:, 2 * HD:]], axis=1)
        dkw_ref[...] += w0 + w1
        carry[...] = curp[...]

        @pl.when(n == nblk)
        def _():
            mt = metap[...]
            km = kvm_ref[...]
            m0, v0 = knorm_bwd(km[:, 0:HD], mt[:, 0:HD], kw_)
            m1, v1 = knorm_bwd(km[:, HD:2 * HD], mt[:, HD:2 * HD], kw_)
            dkvm_ref[...] = jnp.concatenate([m0, m1, mt[:, 2 * HD:]], axis=1)
            dkw_ref[...] += v0 + v1

    small = lambda w: pl.BlockSpec((1, w), lambda n: (0, 0))
    cl = lambda n: jnp.minimum(n, nblk - 1)
    return pl.pallas_call(
        body, grid=(nblk + 1,),
        in_specs=[pl.BlockSpec((BLK, 1024), lambda n: (cl(n), 0)), pl.BlockSpec((BLK, 1024), lambda n: (cl(n), 1)),
                  pl.BlockSpec((BLK, 1024), lambda n: (cl(n), 0))]
        + _kv_specs(nblk, True) + [small(HD), small(HD), SINK_SPEC],
        out_specs=[pl.BlockSpec((BLK, 1024), lambda n: (cl(n), 0)), pl.BlockSpec((BLK, 1024), lambda n: (cl(n), 0)),
                   pl.BlockSpec((BLK, KVW), lambda n: (jnp.maximum(n - 1, 0), 0)),
                   pl.BlockSpec((N_META, KVW), lambda n: (0, 0)), small(HD), small(HD), small(HEADS)],
        out_shape=[SDS((rows, 1024), BF16), SDS((rows, 1024), BF16), SDS((rows, KVW), F32), SDS((N_META, KVW), F32),
                   SDS((1, HD), F32), SDS((1, HD), F32), SDS((1, HEADS), F32)],
        scratch_shapes=[pltpu.VMEM((BLK, KVW), F32), pltpu.VMEM((BLK, KVW), F32), pltpu.VMEM((BLK, KVW), F32),
                        pltpu.VMEM((N_META, KVW), F32), pltpu.VMEM((4, TR, TK), F32)],
        compiler_params=_cp("arbitrary"), name="attn_bwd")(u, u, dog, u, u, u, qw, kw, _sink_cols(sinks))


GROUP_UNROLL = 4
HB = 16


def _bdot(a, b, kind, split=False, fused=False):
    dims = {"nn": ((2,), (1,)), "nt": ((2,), (2,)), "tn": ((1,), (1,))}[kind]
    dg = lambda p, q: lax.dot_general(p, q, (dims, ((0,), (0,))), preferred_element_type=F32)
    if not split:
        return dg(a, b)
    if fused:
        assert kind == "nn" and 3 * a.shape[2] <= MXU_DEPTH
        return _dot3(_split(a), _split(b))
    ah, bh = a.astype(BF16), b.astype(BF16)
    al, bl = (a - ah.astype(F32)).astype(BF16), (b - bh.astype(F32)).astype(BF16)
    return (dg(ah, bl) + dg(al, bh)) + dg(ah, bh)


def _head_cols(hv, beta, gc, gct, lane):
    sel = lane == hv
    return _pick(beta, sel), _pick(gc, sel), gct[pl.ds(hv, 1), :]


def _conv_group(xc_ref, xp_ref, cw_ref, off, first):
    xp = jnp.where(first, 0.0, xp_ref[:, pl.ds(off, DK)])
    xx = jnp.concatenate([xp, xc_ref[:, pl.ds(off, DK)]], axis=0)
    y = cw_ref[0:1, pl.ds(off, DK)] * xx[5:5 + CH]
    for j in range(1, 4):
        y += cw_ref[j:j + 1, pl.ds(off, DK)] * xx[5 + j:5 + j + CH]
    return xx, y


def _gates(ba, al, dtb, c):
    row = c * CH + lax.broadcasted_iota(jnp.int32, (CH, DN_H), 0)
    real = row >= PAD
    xa = ba[:, DN_H:2 * DN_H] + dtb
    beta = jnp.where(real, jax.nn.sigmoid(ba[:, 0:DN_H]), 0.0)
    g = jnp.where(real, -jnp.exp(al) * jax.nn.softplus(xa), 0.0)
    return real, xa, beta, g


def _pick(x, sel):
    return jnp.sum(jnp.where(sel, x, 0.0), axis=1, keepdims=True)


def _chunk_specs(width_blocks):
    return [pl.BlockSpec((CH, 4096), lambda c: (c, 0)),
            pl.BlockSpec((8, 4096), lambda c: (jnp.maximum(8 * c - 1, 0), 0)),
            pl.BlockSpec((CH, DK), lambda c: (c, 48))]


def _tri_inv(m, ii, jj):
    eye = (ii == jj).astype(BF16)
    mh, ml = _split(m)
    blk8 = (ii >> 3) == (jj >> 3)
    mb = (jnp.where(blk8, mh, 0), jnp.where(blk8, ml, 0))
    m2 = _split(_dot3(mb, mb))
    m4 = _split(_dot3(m2, m2))
    x = _dot3(_split(_dot3((eye - mb[0], -mb[1]), (eye + m2[0], m2[1]))), (eye + m4[0], m4[1]))
    for sh in (3, 4, 5):
        off = ((ii >> (sh + 1)) == (jj >> (sh + 1))) & ((ii >> sh) != (jj >> sh))
        xs = _split(x)
        x = x - _dot3(_split(_dot3(xs, (jnp.where(off, mh, 0), jnp.where(off, ml, 0)))), xs)
    return x


def _split(x):
    hi = x.astype(BF16)
    return hi, (x - hi.astype(F32)).astype(BF16)


def _dot3(a, b):
    lhs = jnp.concatenate([a[0], a[1], a[0]], axis=2)
    rhs = jnp.concatenate([b[0], b[0], b[1]], axis=1)
    return lax.dot_general(lhs, rhs, (((2,), (1,)), ((0,), (0,))), preferred_element_type=F32)


def dn_prep(udn, conv_w, a_log, dt_bias):
    rows = udn.shape[0]
    nch = rows // CH

    def body(xc_ref, xp_ref, ba_ref, cw_ref, al_ref, dtb_ref,
             qn_ref, kn_ref, sv_ref, gc_ref, beta_ref, u_ref, w_ref, qe_ref, ks_ref, p_ref, at_ref, pt_ref,
             qet_ref, wt_ref, kst_ref, y_ref, gct):
        c = pl.program_id(0)
        first = c == 0
        _, _, beta, g = _gates(ba_ref[...], al_ref[...], dtb_ref[...], c)
        ii = lax.broadcasted_iota(jnp.int32, (CH, CH), 0)
        jj = lax.broadcasted_iota(jnp.int32, (CH, CH), 1)
        gc = _dot((ii >= jj).astype(F32), g, precision=HI)
        gc_ref[...] = gc
        beta_ref[...] = beta
        gct[...] = gc.T

        def qk_body(kh, carry):
            off = pl.multiple_of(kh * DK, DK)
            _, yq = _conv_group(xc_ref, xp_ref, cw_ref, off, first)
            y_ref[:, pl.ds(off, DK)] = yq
            sq = _silu(yq)
            qn_ref[:, pl.ds(off, DK)] = sq * lax.rsqrt(jnp.sum(sq * sq, axis=-1, keepdims=True) + EPS) * (DK ** -0.5)
            _, yk = _conv_group(xc_ref, xp_ref, cw_ref, pl.multiple_of(1024 + kh * DK, DK), first)
            y_ref[:, pl.ds(pl.multiple_of(1024 + kh * DK, DK), DK)] = yk
            sk = _silu(yk)
            kn_ref[:, pl.ds(off, DK)] = sk * lax.rsqrt(jnp.sum(sk * sk, axis=-1, keepdims=True) + EPS)
            return carry

        lax.fori_loop(0, DN_KH, qk_body, 0, unroll=GROUP_UNROLL)
        lane = lax.broadcasted_iota(jnp.int32, (CH, DN_H), 1)
        zpad = jnp.zeros((CH, DK - CH), F32)

        def v_group(grp, carry):
            offs, ks_, qs_, vs_, cols = [], [], [], [], []
            for i in range(HB):
                hv = grp * HB + i
                offs.append(pl.multiple_of(hv * DK, DK))
                koff = pl.multiple_of((grp * (HB // 2) + i // 2) * DK, DK)
                _, yv = _conv_group(xc_ref, xp_ref, cw_ref, pl.multiple_of(2048 + hv * DK, DK), first)
                y_ref[:, pl.ds(pl.multiple_of(2048 + hv * DK, DK), DK)] = yv
                vs_.append(_silu(yv))
                sv_ref[:, pl.ds(offs[i], DK)] = vs_[i]
                ks_.append(kn_ref[:, pl.ds(koff, DK)])
                qs_.append(qn_ref[:, pl.ds(koff, DK)])
                cols.append(_head_cols(hv, beta, gc, gct, lane))
            k, q, v = jnp.stack(ks_), jnp.stack(qs_), jnp.stack(vs_)
            beta_c, gc_c, gc_r = (jnp.stack([c_[j] for c_ in cols]) for j in range(3))
            dec = jnp.exp(jnp.where(ii >= jj, gc_c - gc_r, NEG))
            eg = jnp.exp(gc_c)
            kb = k * beta_c
            k16 = k.astype(BF16)
            m = jnp.where(ii > jj, _bdot(kb.astype(BF16), k16, "nt") * dec, 0.0)
            a = _tri_inv(m, ii, jj)
            uw = _bdot(a, jnp.concatenate([v * beta_c, kb * eg], axis=2), "nn", True, True)
            p = _bdot(q.astype(BF16), k16, "nt") * dec
            qe = q * eg
            ksx = k * jnp.exp(gc_c[:, CH - 1:CH, :] - gc_c)
            tslot = lambda x: jnp.concatenate([x.T, jnp.zeros((DK, DK - CH), F32)], axis=1).astype(BF16)
            for i in range(HB):
                sl = pl.ds(offs[i], DK)
                u_ref[:, sl] = uw[i, :, :DK]
                w_ref[:, sl] = uw[i, :, DK:]
                qe_ref[:, sl] = qe[i].astype(BF16)
                ks_ref[:, sl] = ksx[i].astype(BF16)
                p_ref[:, sl] = jnp.concatenate([p[i], zpad], axis=1).astype(BF16)
                at_ref[:, sl] = jnp.concatenate([a[i].T, zpad], axis=1)
                pt_ref[:, sl] = jnp.concatenate([p[i].T, zpad], axis=1).astype(BF16)
                qet_ref[:, sl] = tslot(qe[i])
                wt_ref[:, sl] = tslot(uw[i, :, DK:])
                kst_ref[:, sl] = tslot(ksx[i])
            return carry

        lax.fori_loop(0, DN_H // HB, v_group, 0)

    full = lambda shape: pl.BlockSpec(shape, lambda c: (0, 0))
    blk = lambda w: pl.BlockSpec((CH, w), lambda c: (c, 0))
    return pl.pallas_call(
        body, grid=(nch,),
        in_specs=_chunk_specs(0) + [full((4, 4096)), full((1, DN_H)), full((1, DN_H))],
        out_specs=[blk(1024), blk(1024), blk(2048), blk(DN_H), blk(DN_H), blk(2048), blk(2048), blk(2048), blk(2048),
                   blk(2048), blk(2048), blk(2048)] + [pl.BlockSpec((DK, 2048), lambda c: (c, 0))] * 3 + [blk(4096)],
        out_shape=[SDS((rows, 1024), F32), SDS((rows, 1024), F32), SDS((rows, 2048), F32), SDS((rows, DN_H), F32),
                   SDS((rows, DN_H), F32), SDS((rows, 2048), F32), SDS((rows, 2048), F32), SDS((rows, 2048), BF16),
                   SDS((rows, 2048), BF16), SDS((rows, 2048), BF16), SDS((rows, 2048), F32),
                   SDS((rows, 2048), BF16)] + [SDS((2 * rows, 2048), BF16)] * 3 + [SDS((rows, 4096), F32)],
        scratch_shapes=[pltpu.VMEM((DN_H, CH), F32)],
        compiler_params=_cp("parallel"), name="dn_prep")(udn, udn, udn, conv_w, a_log, dt_bias)


def dn_scan(u, w, qe, kst, p, gc):
    rows = u.shape[0]
    nch = rows // CH

    def body(u_ref, w_ref, qe_ref, kst_ref, p_ref, gc_ref, o_ref, vn_ref, st_ref, s_scr):
        @pl.when(pl.program_id(0) == 0)
        def _():
            s_scr[...] = jnp.zeros_like(s_scr)

        gl_row = gc_ref[CH - 1:CH, :]
        lane = lax.broadcasted_iota(jnp.int32, (1, DN_H), 1)

        def group(grp, carry):
            base = grp * HB
            sls = [pl.ds(pl.multiple_of((base + i) * DK, DK), DK) for i in range(HB)]
            heads = lambda ref: jnp.stack([ref[:, sl] for sl in sls])
            s = s_scr[pl.ds(base, HB)]
            st_ref[0, pl.ds(base, HB)] = s
            s16 = s.astype(BF16)
            vn = heads(u_ref) - _bdot(heads(w_ref).astype(BF16), s16, "nn")
            vn16 = vn.astype(BF16)
            o = _bdot(heads(qe_ref), s16, "nn") + _bdot(heads(p_ref)[:, :, 0:CH], vn16, "nn")
            egl = jnp.exp(jnp.stack([_pick(gl_row, lane == base + i) for i in range(HB)]))
            s_scr[pl.ds(base, HB)] = s * egl + _bdot(heads(kst_ref)[:, :, 0:CH], vn16, "nn")
            for i in range(HB):
                vn_ref[:, sls[i]] = vn16[i]
                o_ref[:, sls[i]] = o[i]
            return carry

        lax.fori_loop(0, DN_H // HB, group, 0)

    blk = lambda wd: pl.BlockSpec((CH, wd), lambda c: (c, 0))
    return pl.pallas_call(
        body, grid=(nch,),
        in_specs=[blk(2048)] * 3 + [pl.BlockSpec((DK, 2048), lambda c: (c, 0)), blk(2048), blk(DN_H)],
        out_specs=[blk(2048), blk(2048), pl.BlockSpec((1, DN_H, DK, DK), lambda c: (c, 0, 0, 0))],
        out_shape=[SDS((rows, 2048), F32), SDS((rows, 2048), BF16), SDS((nch, DN_H, DK, DK), F32)],
        scratch_shapes=[pltpu.VMEM((DN_H, DK, DK), F32)],
        compiler_params=_cp("arbitrary"), name="dn_scan")(u, w, qe, kst, p, gc)


def dn_out_fwd(o, udn, ow, wout, h1, tgt):
    rows = o.shape[0]
    tm = _row_tile(rows)
    nt = rows // tm

    def body(o_ref, z_ref, ow_ref, w_ref, h_ref, t_ref, dh_ref, on_ref, ls_ref):
        for hv in range(DN_H):
            sl = slice(hv * DK, hv * DK + DK)
            oh = o_ref[:, sl]
            on_ref[:, sl] = (oh * _rms(oh) * ow_ref[...] * _silu(z_ref[:, sl])).astype(BF16)
        h2 = h_ref[...] + _dot(on_ref[...], w_ref[...])
        row = pl.program_id(0) * tm + lax.broadcasted_iota(jnp.int32, (tm, 1), 0)
        err = jnp.where(row >= BLK, h2 - t_ref[...], 0.0)
        dh_ref[...] = err * (1.0 / D_MODEL)
        ls_ref[0] = jnp.sum(err * err, axis=0, keepdims=True)

    return pl.pallas_call(
        body, grid=(nt,),
        in_specs=[pl.BlockSpec((tm, 2048), lambda i: (i, 0)), pl.BlockSpec((tm, 2048), lambda i: (i, 2)),
                  pl.BlockSpec((1, DK), lambda i: (0, 0)), pl.BlockSpec((2048, D_MODEL), lambda i: (0, 0)),
                  pl.BlockSpec((tm, D_MODEL), lambda i: (i, 0)), pl.BlockSpec((tm, D_MODEL), lambda i: (i, 0))],
        out_specs=[pl.BlockSpec((tm, D_MODEL), lambda i: (i, 0)), pl.BlockSpec((tm, 2048), lambda i: (i, 0)),
                   pl.BlockSpec((1, 1, D_MODEL), lambda i: (i, 0, 0))],
        out_shape=[SDS((rows, D_MODEL), F32), SDS((rows, 2048), BF16), SDS((nt, 1, D_MODEL), F32)],
        compiler_params=_cp("parallel"), name="dn_out_fwd")(o, udn, ow, wout, h1, tgt)


def dn_out_bwd(dh2, wout, o, udn, ow):
    rows = o.shape[0]
    tm = _row_tile(rows)
    nt = rows // tm

    def body(dh_ref, w_ref, o_ref, z_ref, ow_ref, do_ref, dz_ref, dow_ref):
        don = _dot(dh_ref[...].astype(BF16), w_ref[...], NT)
        ow_ = ow_ref[...]
        dow = jnp.zeros((1, DK), F32)
        for hv in range(DN_H):
            sl = slice(hv * DK, hv * DK + DK)
            oh = o_ref[:, sl]
            r = _rms(oh)
            y = oh * r
            z = z_ref[:, sl]
            dn = don[:, sl] * _silu(z)
            dz_ref[:, sl] = (don[:, sl] * (y * ow_) * _dsilu(z)).astype(BF16)
            dy = dn * ow_
            do_ref[:, sl] = r * (dy - y * jnp.mean(y * dy, axis=-1, keepdims=True))
            dow += jnp.sum(dn * y, axis=0, keepdims=True)
        dow_ref[0] = dow

    return pl.pallas_call(
        body, grid=(nt,),
        in_specs=[pl.BlockSpec((tm, D_MODEL), lambda i: (i, 0)), pl.BlockSpec((2048, D_MODEL), lambda i: (0, 0)),
                  pl.BlockSpec((tm, 2048), lambda i: (i, 0)), pl.BlockSpec((tm, 2048), lambda i: (i, 2)),
                  pl.BlockSpec((1, DK), lambda i: (0, 0))],
        out_specs=[pl.BlockSpec((tm, 2048), lambda i: (i, 0)), pl.BlockSpec((tm, 2048), lambda i: (i, 0)),
                   pl.BlockSpec((1, 1, DK), lambda i: (i, 0, 0))],
        out_shape=[SDS((rows, 2048), F32), SDS((rows, 2048), BF16), SDS((nt, 1, DK), F32)],
        compiler_params=_cp("parallel"), name="dn_out_bwd")(dh2, wout, o, udn, ow)


def dn_scan_bwd(do, qn, kn, sv, gc, beta, at, pt, u, w, vn, qet, wt, ks, st):
    rows = do.shape[0]
    nch = rows // CH

    def body(do_ref, q_ref, k_ref, v_ref, gc_ref, beta_ref, at_ref, pt_ref, u_ref, w_ref, vn_ref, qet_ref, wt_ref,
             ks_ref, st_ref, dq_ref, dk_ref, dv_ref, dbeta_ref, dg_ref, ds_scr, gct):
        @pl.when(pl.program_id(0) == 0)
        def _():
            ds_scr[...] = jnp.zeros_like(ds_scr)

        gc, beta = gc_ref[...], beta_ref[...]
        gct[...] = gc.T
        ii = lax.broadcasted_iota(jnp.int32, (CH, CH), 0)
        jj = lax.broadcasted_iota(jnp.int32, (CH, CH), 1)
        lane = lax.broadcasted_iota(jnp.int32, (CH, DN_H), 1)
        last = lax.broadcasted_iota(jnp.int32, (CH, 1), 0) == CH - 1

        def group(grp, carry):
            dbeta_acc, dgc_acc = carry
            base = grp * HB
            sls = [pl.ds(pl.multiple_of((base + i) * DK, DK), DK) for i in range(HB)]
            ksls = [pl.ds(pl.multiple_of((grp * (HB // 2) + j) * DK, DK), DK) for j in range(HB // 2)]
            heads = lambda ref: jnp.stack([ref[:, sl] for sl in sls])
            kheads = lambda ref: jnp.stack([ref[:, ksls[i // 2]] for i in range(HB)])
            cols = [_head_cols(base + i, beta, gc, gct, lane) for i in range(HB)]
            beta_c, gc_c, gc_r = (jnp.stack([c_[j] for c_ in cols]) for j in range(3))
            k, q, v = kheads(k_ref), kheads(q_ref), heads(v_ref)
            dec = jnp.exp(jnp.where(ii >= jj, gc_c - gc_r, NEG))
            eg = jnp.exp(gc_c)
            gl = gc_c[:, CH - 1:CH, :]
            e2 = jnp.exp(gl - gc_c)
            egl = jnp.exp(gl)
            k16, q16 = k.astype(BF16), q.astype(BF16)
            do16 = heads(do_ref).astype(BF16)
            s = st_ref[0, pl.ds(base, HB)]
            s16 = s.astype(BF16)
            dso = ds_scr[pl.ds(base, HB)]
            dso16 = dso.astype(BF16)
            wf, uf, vn16 = heads(w_ref), heads(u_ref), heads(vn_ref)
            kb = k * beta_c
            kb16 = kb.astype(BF16)
            pm = _bdot(q16, k16, "nt") * dec
            m = jnp.where(ii > jj, _bdot(kb16, k16, "nt") * dec, 0.0)
            dvn = _bdot(heads(pt_ref)[:, :, 0:CH], do16, "nn") + _bdot(heads(ks_ref), dso16, "nn")
            dvn16 = dvn.astype(BF16)
            ds_scr[pl.ds(base, HB)] = (egl * dso + _bdot(heads(qet_ref)[:, :, 0:CH], do16, "nn")
                                       - _bdot(heads(wt_ref)[:, :, 0:CH], dvn16, "nn"))
            dpm = jnp.where(ii >= jj, _bdot(do16, vn16, "nt"), 0.0)
            dqk16 = (dpm * dec).astype(BF16)
            dqe = _bdot(do16, s16, "nt")
            dq = eg * dqe + _bdot(dqk16, k16, "nn")
            dks = _bdot(vn16, dso16, "nt")
            dw = -_bdot(dvn16, s16, "nt")
            dbvk = _bdot(heads(at_ref)[:, :, 0:CH], jnp.concatenate([dvn, dw], axis=2), "nn", True)
            dbv, dbk = dbvk[:, :, :DK], dbvk[:, :, DK:]
            dm = jnp.where(ii > jj, -_bdot(dbvk, jnp.concatenate([uf, wf], axis=2), "nt", True), 0.0)
            g16 = (dm * dec).astype(BF16)
            dkb = _bdot(g16, k16, "nn")
            dk = (_bdot(dqk16, q16, "tn") + e2 * dks + _bdot(g16, kb16, "tn") + beta_c * (eg * dbk + dkb))
            e = dpm * pm + dm * m
            rsum = lambda x: jnp.sum(x, axis=2, keepdims=True)
            r_bk, r_qe, r_beta, r_ks = rsum(dbk * k), rsum(q * dqe), rsum(dbv * v + dkb * k), rsum(dks * k)
            t = r_ks * e2
            dgl = jnp.sum(t, axis=1, keepdims=True) + egl * rsum(jnp.sum(dso * s, axis=1, keepdims=True))
            deg = r_qe + beta_c * r_bk
            dgc = rsum(e) - t + deg * eg + jnp.where(last, dgl, 0.0)
            dgrow = -jnp.sum(e, axis=1, keepdims=True)
            dv = beta_c * dbv
            dbeta = r_beta + eg * r_bk
            for i in range(HB):
                dv_ref[:, sls[i]] = dv[i]
                sel = lane == base + i
                dbeta_acc = jnp.where(sel, dbeta[i], dbeta_acc)
                dgc_acc = jnp.where(sel, dgc[i], dgc_acc)
                gct[pl.ds(base + i, 1), :] = dgrow[i]
            for j in range(HB // 2):
                dq_ref[:, ksls[j]] = dq[2 * j] + dq[2 * j + 1]
                dk_ref[:, ksls[j]] = dk[2 * j] + dk[2 * j + 1]
            return dbeta_acc, dgc_acc

        zero = jnp.zeros((CH, DN_H), F32)
        dbeta_acc, dgc_acc = lax.fori_loop(0, DN_H // HB, group, (zero, zero))
        dbeta_ref[...] = dbeta_acc
        dg_ref[...] = _dot((ii <= jj).astype(F32), dgc_acc + gct[...].T, precision=HI)

    rev = lambda wd: pl.BlockSpec((CH, wd), lambda i: (nch - 1 - i, 0))
    rev_t = pl.BlockSpec((DK, 2048), lambda i: (nch - 1 - i, 0))
    return pl.pallas_call(
        body, grid=(nch,),
        in_specs=[rev(2048), rev(1024), rev(1024), rev(2048), rev(DN_H), rev(DN_H), rev(2048), rev(2048), rev(2048),
                  rev(2048), rev(2048), rev_t, rev_t, rev(2048),
                  pl.BlockSpec((1, DN_H, DK, DK), lambda i: (nch - 1 - i, 0, 0, 0))],
        out_specs=[rev(1024), rev(1024), rev(2048), rev(DN_H), rev(DN_H)],
        out_shape=[SDS((rows, 1024), F32), SDS((rows, 1024), F32), SDS((rows, 2048), F32), SDS((rows, DN_H), F32),
                   SDS((rows, DN_H), F32)],
        scratch_shapes=[pltpu.VMEM((DN_H, DK, DK), F32), pltpu.VMEM((DN_H, CH), F32)],
        compiler_params=_cp("arbitrary"), name="dn_scan_bwd")(
            do, qn, kn, sv, gc, beta, at, pt, u, w, vn, qet, wt, ks, st)


def dn_prep_bwd(udn, yconv, conv_w, a_log, dt_bias, dqn, dkn, dv, dbeta, dg):
    rows = udn.shape[0]
    nch = rows // CH
    ext = CH + 8

    def body(xc_ref, ba_ref, yc_ref, yn_ref, dqn_n, dkn_n, dv_n, cw_ref, al_ref, dtb_ref, dqn_ref, dkn_ref, dv_ref,
             dbeta_ref, dg_ref, dx_ref, dba_ref, dcw_ref, dal_ref, ddtb_ref):
        c = pl.program_id(0)
        first = c == 0
        own = (lax.broadcasted_iota(jnp.int32, (ext, 1), 0) < CH) | (c < nch - 1)

        @pl.when(first)
        def _():
            dcw_ref[...] = jnp.zeros_like(dcw_ref)
            dal_ref[...] = jnp.zeros_like(dal_ref)
            ddtb_ref[...] = jnp.zeros_like(ddtb_ref)

        real, xa, beta, g = _gates(ba_ref[...], al_ref[...], dtb_ref[...], c)
        dgm = jnp.where(real, dg_ref[...], 0.0)
        da = dgm * (-jnp.exp(al_ref[...])) * jax.nn.sigmoid(xa)
        dal_ref[...] += jnp.sum(dgm * g, axis=0, keepdims=True)
        ddtb_ref[...] += jnp.sum(da, axis=0, keepdims=True)
        dba_ref[...] = jnp.zeros_like(dba_ref)
        dba_ref[:, 0:DN_H] = jnp.where(real, dbeta_ref[...] * beta * (1.0 - beta), 0.0)
        dba_ref[:, DN_H:2 * DN_H] = da

        def through_conv(off, g_cur, g_next, grad_fn):
            sl = pl.ds(off, DK)
            y = jnp.concatenate([yc_ref[:, sl], yn_ref[:, sl]], axis=0)
            sg = jax.nn.sigmoid(y)
            dsilu = sg * (1.0 + y * (1.0 - sg))
            dy = jnp.where(own, grad_fn(y * sg, jnp.concatenate([g_cur, g_next], axis=0)) * dsilu, 0.0)
            shifted = [dy[3 - j:3 - j + CH] for j in range(4)]
            x = xc_ref[:, sl]
            dx = cw_ref[0:1, sl] * shifted[0]
            for j in range(1, 4):
                dx += cw_ref[j:j + 1, sl] * shifted[j]
            dx_ref[:, sl] = dx.astype(BF16)
            for j in range(4):
                dcw_ref[j:j + 1, sl] += jnp.sum(shifted[j] * x, axis=0, keepdims=True)

        def l2_bwd(scale):
            def f(s, gin):
                r = lax.rsqrt(jnp.sum(s * s, axis=-1, keepdims=True) + EPS)
                nrm = s * r
                return (r * scale) * (gin - nrm * jnp.sum(nrm * gin, axis=-1, keepdims=True))
            return f

        def qk_body(kh, carry):
            sl = pl.ds(pl.multiple_of(kh * DK, DK), DK)
            through_conv(pl.multiple_of(kh * DK, DK), dqn_ref[:, sl], dqn_n[:, sl], l2_bwd(DK ** -0.5))
            through_conv(pl.multiple_of(1024 + kh * DK, DK), dkn_ref[:, sl], dkn_n[:, sl], l2_bwd(1.0))
            return carry

        lax.fori_loop(0, DN_KH, qk_body, 0, unroll=GROUP_UNROLL)

        def v_body(hv, carry):
            sl = pl.ds(pl.multiple_of(hv * DK, DK), DK)
            through_conv(pl.multiple_of(2048 + hv * DK, DK), dv_ref[:, sl], dv_n[:, sl], lambda s, gin: gin)
            return carry

        lax.fori_loop(0, DN_H, v_body, 0, unroll=GROUP_UNROLL)

    full = lambda shape: pl.BlockSpec(shape, lambda c: (0, 0))
    blk = lambda w: pl.BlockSpec((CH, w), lambda c: (c, 0))
    nxt = lambda w: pl.BlockSpec((8, w), lambda c: (jnp.minimum(8 * c + 8, rows // 8 - 1), 0))
    return pl.pallas_call(
        body, grid=(nch,),
        in_specs=[_chunk_specs(0)[0], _chunk_specs(0)[2], blk(4096), nxt(4096), nxt(1024), nxt(1024), nxt(2048),
                  full((4, 4096)), full((1, DN_H)), full((1, DN_H)), blk(1024), blk(1024), blk(2048), blk(DN_H),
                  blk(DN_H)],
        out_specs=[blk(4096), blk(DK), full((8, 4096)), full((1, DN_H)), full((1, DN_H))],
        out_shape=[SDS((rows, 4096), BF16), SDS((rows, DK), F32), SDS((8, 4096), F32), SDS((1, DN_H), F32),
                   SDS((1, DN_H), F32)],
        compiler_params=_cp("arbitrary"), name="dn_prep_bwd")(
            udn, udn, yconv, yconv, dqn, dkn, dv, conv_w, a_log, dt_bias, dqn, dkn, dv, dbeta, dg)


def local_step(x, target, w):
    seq = x.shape[0]
    bf = lambda a: a.astype(BF16)
    h0 = jnp.concatenate([jnp.zeros((PAD, D_MODEL), F32), w["meta_tokens"], x], axis=0)
    tgt = jnp.concatenate([jnp.zeros((BLK, D_MODEL), F32), target], axis=0)
    win = w["attn_w_in"]
    wq, wkv, wg = win[:, :1024], win[:, 1024:1280], win[:, 1280:]
    wa_in = bf(jnp.concatenate([wq, wg, wkv], axis=1))
    wa_out = bf(w["attn_w_out"])
    wd_in = jnp.concatenate([bf(w["dn_w_in"]), jnp.zeros((D_MODEL, 96), BF16)], axis=1)
    wd_out = bf(w["dn_w_out"])
    qw, kw, sinks = w["attn_q_norm_w"], w["attn_k_norm_w"], w["attn_sinks"]
    cw, al, dtb, ow = w["dn_conv_w"], w["dn_a_log"], w["dn_dt_bias"], w["dn_o_norm_w"]

    ua, xn0 = norm_matmul(h0, w["attn_norm_w"], wa_in, 2304, "attn_in")
    og = attn_fwd(ua, qw, kw, sinks)
    h1 = matmul_residual(og, wa_out, h0, "attn_out")
    ud, xn1 = norm_matmul(h1, w["dn_norm_w"], wd_in, 6272, "dn_in")
    qn, kn, sv, gc, beta, u, wy, qe, ks, p, at, pt, qet, wt, kst, yconv = dn_prep(ud, cw, al, dtb)
    o, vn, st = dn_scan(u, wy, qe, kst, p, gc)
    dh2, on, ls = dn_out_fwd(o, ud, ow, wd_out, h1, tgt)
    loss = (0.5 / D_MODEL) * jnp.sum(ls)

    do, dz, dow = dn_out_bwd(dh2, wd_out, o, ud, ow)
    g_dn_out = wgrad(on, dh2, "dn_out_wgrad")
    dqn, dkn, dv, dbeta, dg = dn_scan_bwd(do, qn, kn, sv, gc, beta, at, pt, u, wy, vn, qet, wt, ks, st)
    dxc, dba, dcw, dal, ddtb = dn_prep_bwd(ud, yconv, cw, al, dtb, dqn, dkn, dv, dbeta, dg)
    dh1, dnw1 = in_proj_bwd([dxc, dz, dba], [wd_in[:, :4096], wd_in[:, 4096:6144], wd_in[:, 6144:]],
                            h1, w["dn_norm_w"], dh2, "dn_in_bwd")
    g_dn_in = jnp.concatenate([wgrad(xn1, dxc, "dn_in_wgrad_qkv"), wgrad(xn1, dz, "dn_in_wgrad_z"),
                               wgrad(xn1, dba, "dn_in_wgrad_ba")[:, :2 * DN_H]], axis=1)

    dog = matmul_nt(dh1, wa_out, "attn_out_bwd")
    g_attn_out = wgrad(og, dh1, "attn_out_wgrad")
    dq, dgate, dkv, dkvm, dqw, dkw, dsk = attn_bwd(ua, qw, kw, sinks, dog)
    dkv = dkv.at[PAD:BLK].add(dkvm)
    dh0, dnw0 = in_proj_bwd([dq, dgate, dkv], [wa_in[:, :1024], wa_in[:, 1024:2048], wa_in[:, 2048:]],
                            h0, w["attn_norm_w"], dh1, "attn_in_bwd")
    g_attn_in = jnp.concatenate([wgrad(xn0, dq, "attn_in_wgrad_q"), wgrad(xn0, dkv, "attn_in_wgrad_kv"),
                                 wgrad(xn0, dgate, "attn_in_wgrad_g")], axis=1)
    grads = {
        "meta_tokens": dh0[PAD:BLK], "attn_norm_w": jnp.sum(dnw0, axis=0), "attn_w_in": g_attn_in,
        "attn_q_norm_w": dqw, "attn_k_norm_w": dkw, "attn_sinks": dsk, "attn_w_out": g_attn_out,
        "dn_norm_w": jnp.sum(dnw1, axis=0), "dn_w_in": g_dn_in, "dn_conv_w": dcw[:4], "dn_a_log": dal,
        "dn_dt_bias": ddtb, "dn_o_norm_w": jnp.sum(dow, axis=0), "dn_w_out": g_dn_out,
    }
    return loss, dh0[BLK:BLK + seq], grads


WEIGHTS = ["meta_tokens", "attn_norm_w", "attn_w_in", "attn_q_norm_w", "attn_k_norm_w", "attn_sinks", "attn_w_out",
           "dn_norm_w", "dn_w_in", "dn_conv_w", "dn_a_log", "dn_dt_bias", "dn_o_norm_w", "dn_w_out"]
SHARDED = {"attn_w_in": ((1024, 2304), 1), "attn_w_out": ((1024, 1024), 0), "dn_w_in": ((1024, 6176), 1),
           "dn_w_out": ((2048, 1024), 0), "dn_conv_w": ((4, 4096), 1), "meta_tokens": ((16, 1024), 1),
           "dn_norm_w": ((1, 1024), 1)}
REPLICATED = {"attn_norm_w": 1024, "attn_q_norm_w": 64, "attn_k_norm_w": 64, "attn_sinks": 16, "dn_a_log": 16,
              "dn_dt_bias": 16, "dn_o_norm_w": 128}
N_CHIPS = 4
PACK_ROWS = 2912
HALF_ROWS = PACK_ROWS // 2
SMALL_ROWS = 8


def _shard_shape(name):
    (r, c), axis = SHARDED[name]
    return (r // N_CHIPS, c) if axis == 0 else (r, c // N_CHIPS)


def _pack(parts, rows):
    flat = jnp.concatenate([p.reshape(-1) for p in parts])
    return jnp.pad(flat, (0, rows * 1024 - flat.shape[0])).reshape(rows, 1024)


def pack_shard(shards):
    return _pack([shards[n] for n in SHARDED], PACK_ROWS)


def unpack_shard(buf):
    flat, out, pos = buf.reshape(-1), {}, 0
    for n in SHARDED:
        shp = _shard_shape(n)
        size = shp[0] * shp[1]
        out[n] = flat[pos:pos + size].reshape(shp)
        pos += size
    return out


MATRICES = ("attn_w_in", "attn_w_out", "dn_w_in", "dn_w_out")


def pack_gather(shards):
    big = [shards[n].astype(BF16).reshape(-1) for n in MATRICES]
    small = jnp.concatenate([shards[n].reshape(-1) for n in SHARDED if n not in MATRICES])
    flat = jnp.concatenate(big + [lax.bitcast_convert_type(small, BF16).reshape(-1)])
    return jnp.pad(flat, (0, PACK_ROWS * 1024 - flat.shape[0])).reshape(PACK_ROWS, 1024)


def unpack_gather(buf):
    PER_F32 = 4 // jnp.dtype(buf.dtype).itemsize
    flat, out, pos = buf.reshape(-1), {}, 0
    for n in MATRICES:
        shp = _shard_shape(n)
        out[n] = flat[pos:pos + shp[0] * shp[1]].reshape(shp)
        pos += shp[0] * shp[1]
    for n in SHARDED:
        if n not in MATRICES:
            shp = _shard_shape(n)
            raw = flat[pos:pos + shp[0] * shp[1] * PER_F32]
            out[n] = lax.bitcast_convert_type(raw.reshape(-1, PER_F32) if PER_F32 > 1 else raw, F32).reshape(shp)
            pos += shp[0] * shp[1] * PER_F32
    return out


def pack_small(vals):
    return _pack([vals[n] for n in REPLICATED], SMALL_ROWS)


def unpack_small(buf):
    flat, out, pos = buf.reshape(-1), {}, 0
    for n, size in REPLICATED.items():
        out[n] = flat[pos:pos + size].reshape(1, size)
        pos += size
    return out


ANY = pl.BlockSpec(memory_space=pl.ANY)


def _place():
    return lax.axis_index("x"), lax.axis_index("y"), lax.axis_index("c")


def chips_exchange(src, gather):
    r = src.shape[-2]

    def body(s_ref, o_ref, send_sems, recv_sems):
        x, y, c = _place()
        me = 2 * x + y
        peers = [(1 - x, y), (x, 1 - y), (1 - x, 1 - y)]

        def copy(k, to_block, from_block):
            px, py = peers[k]
            return pltpu.make_async_remote_copy(
                src_ref=s_ref if gather else s_ref.at[to_block], dst_ref=o_ref.at[from_block],
                send_sem=send_sems.at[k], recv_sem=recv_sems.at[k], device_id=(px, py, c), device_id_type=MESH)

        sends = [copy(k, 2 * px + py, me) for k, (px, py) in enumerate(peers)]
        for cp in sends:
            cp.start()
        for k, (px, py) in enumerate(peers):
            copy(k, me, 2 * px + py).wait_recv()
        for cp in sends:
            cp.wait_send()

    return pl.pallas_call(
        body, in_specs=[ANY], out_specs=ANY, out_shape=SDS((N_CHIPS, r, 1024), src.dtype),
        scratch_shapes=[pltpu.SemaphoreType.DMA((3,)), pltpu.SemaphoreType.DMA((3,))],
        name="chips_gather" if gather else "chips_exchange")(src)


def chip_sum(received, pair, me):
    tm = 208

    def body(me_ref, own_ref, r1_ref, r2_ref, r3_ref, o_ref):
        o_ref[...] = ((own_ref[0] + r1_ref[0]) + r2_ref[0]) + r3_ref[0]

    blk = lambda k: pl.BlockSpec((1, tm, 1024), lambda i, me_ref: ((me_ref[0] + k) % N_CHIPS, i, 0))
    return pl.pallas_call(
        body,
        grid_spec=pltpu.PrefetchScalarGridSpec(
            num_scalar_prefetch=1, grid=(HALF_ROWS // tm,), in_specs=[blk(0), blk(1), blk(2), blk(3)],
            out_specs=pl.BlockSpec((tm, 1024), lambda i, me_ref: (i, 0))),
        out_shape=SDS((HALF_ROWS, 1024), F32), compiler_params=_cp("parallel"), name="chip_sum")(
            me.reshape(1).astype(jnp.int32), pair, received, received, received)


def _rows_at(ref, start, size):
    return ref.at[:, pl.ds(start, size), :] if len(ref.shape) == 3 else ref.at[pl.ds(start, size), :]


def sibling_join(src, name):
    axis = len(src.shape) - 2

    def body(s_ref, o_ref, send_sem, recv_sem):
        x, y, c = _place()
        cp = pltpu.make_async_remote_copy(src_ref=s_ref, dst_ref=o_ref, send_sem=send_sem, recv_sem=recv_sem,
                                          device_id=(x, y, 1 - c), device_id_type=MESH)
        cp.start()
        cp.wait()

    theirs = pl.pallas_call(
        body, in_specs=[ANY], out_specs=ANY, out_shape=SDS(src.shape, src.dtype),
        scratch_shapes=[pltpu.SemaphoreType.DMA, pltpu.SemaphoreType.DMA], name=name)(src)
    first = lax.axis_index("c") == 0
    return jnp.concatenate([jnp.where(first, src, theirs), jnp.where(first, theirs, src)], axis=axis)


def sibling_give(g_all):
    def body(s_ref, o_ref, send_sem, recv_sem):
        x, y, c = _place()
        cp = pltpu.make_async_remote_copy(
            src_ref=_rows_at(s_ref, (1 - c) * HALF_ROWS, HALF_ROWS), dst_ref=o_ref, send_sem=send_sem,
            recv_sem=recv_sem, device_id=(x, y, 1 - c), device_id_type=MESH)
        cp.start()
        cp.wait()

    return pl.pallas_call(
        body, in_specs=[ANY], out_specs=ANY, out_shape=SDS((N_CHIPS, HALF_ROWS, 1024), F32),
        scratch_shapes=[pltpu.SemaphoreType.DMA, pltpu.SemaphoreType.DMA], name="pair_exchange")(g_all)


def pair_sum(g_all, got, c):
    tm = 208
    per_half = HALF_ROWS // tm

    def body(c_ref, a_ref, b_ref, o_ref):
        o_ref[...] = a_ref[...] + b_ref[...]

    return pl.pallas_call(
        body,
        grid_spec=pltpu.PrefetchScalarGridSpec(
            num_scalar_prefetch=1, grid=(N_CHIPS, per_half),
            in_specs=[pl.BlockSpec((1, tm, 1024), lambda j, i, c_ref: (j, c_ref[0] * per_half + i, 0)),
                      pl.BlockSpec((1, tm, 1024), lambda j, i, c_ref: (j, i, 0))],
            out_specs=pl.BlockSpec((1, tm, 1024), lambda j, i, c_ref: (j, i, 0))),
        out_shape=SDS((N_CHIPS, HALF_ROWS, 1024), F32),
        compiler_params=_cp("parallel", "parallel"), name="pair_sum")(c.reshape(1).astype(jnp.int32), g_all, got)


def all_gather_small(src):
    def body(s_ref, o_ref, send_sems, recv_sems, local_sem):
        x, y, c = _place()
        flips = [(fx, fy, fc) for fx in (0, 1) for fy in (0, 1) for fc in (0, 1)][1:]
        idx = lambda px, py, pc: 4 * px + 2 * py + pc
        mine = pltpu.make_async_copy(s_ref, o_ref.at[idx(x, y, c)], local_sem)
        mine.start()

        def peer(k):
            fx, fy, fc = flips[k]
            return (1 - x if fx else x, 1 - y if fy else y, 1 - c if fc else c)

        def copy(k, block):
            return pltpu.make_async_remote_copy(
                src_ref=s_ref, dst_ref=o_ref.at[block], send_sem=send_sems.at[k], recv_sem=recv_sems.at[k],
                device_id=peer(k), device_id_type=MESH)

        sends = [copy(k, idx(x, y, c)) for k in range(7)]
        for cp in sends:
            cp.start()
        for k in range(7):
            copy(k, idx(*peer(k))).wait_recv()
        for cp in sends:
            cp.wait_send()
        mine.wait()

    return pl.pallas_call(
        body, in_specs=[ANY], out_specs=ANY, out_shape=SDS((8,) + src.shape, F32),
        scratch_shapes=[pltpu.SemaphoreType.DMA((7,)), pltpu.SemaphoreType.DMA((7,)), pltpu.SemaphoreType.DMA],
        name="all_gather_small")(src)


def sum_blocks(t, name):
    n, r, _ = t.shape
    tm = 208 if r % 208 == 0 else r

    def body(t_ref, o_ref):
        acc = t_ref[0]
        for i in range(1, n):
            acc = acc + t_ref[i]
        o_ref[...] = acc

    return pl.pallas_call(
        body, grid=(r // tm,), in_specs=[pl.BlockSpec((n, tm, 1024), lambda i: (0, i, 0))],
        out_specs=pl.BlockSpec((tm, 1024), lambda i: (i, 0)), out_shape=SDS((r, 1024), F32),
        compiler_params=_cp("parallel"), name=name)(t)


ADAM_BLOCK_BYTES = 1024 * 1024


def adamw(w, g, m, v, name):
    rows, cols = w.shape
    tm = rows
    while tm * cols * 4 > ADAM_BLOCK_BYTES and tm % 16 == 0:
        tm //= 2

    def body(w_ref, g_ref, m_ref, v_ref, d_ref, nm_ref, nv_ref):
        g_ = g_ref[...]
        m_ = ADAM_B1 * m_ref[...] + (1.0 - ADAM_B1) * g_
        v_ = ADAM_B2 * v_ref[...] + (1.0 - ADAM_B2) * (g_ * g_)
        m_hat = m_ / (1.0 - ADAM_B1 ** ADAM_STEP)
        v_hat = v_ / (1.0 - ADAM_B2 ** ADAM_STEP)
        d_ref[...] = -ADAM_LR * (m_hat / (jnp.sqrt(v_hat) + ADAM_EPS) + ADAM_WD * w_ref[...])
        nm_ref[...] = m_
        nv_ref[...] = v_

    spec = pl.BlockSpec((tm, cols), lambda i: (i, 0))
    return pl.pallas_call(
        body, grid=(rows // tm,), in_specs=[spec] * 4, out_specs=[spec] * 3,
        out_shape=[SDS((rows, cols), F32)] * 3, compiler_params=_cp("parallel"), name=name)(w, g, m, v)


LAYERED = ("attn_w_in", "attn_w_out", "dn_w_in", "dn_conv_w", "dn_w_out")


def _two_d(name, a):
    return a[0] if name in LAYERED else a


def kernel(x, meta_tokens, attn_norm_w, attn_w_in, attn_q_norm_w, attn_k_norm_w, attn_sinks, attn_w_out, dn_norm_w, dn_w_in, dn_conv_w, dn_a_log, dn_dt_bias, dn_o_norm_w, dn_w_out, loss_target, m_meta_tokens, m_attn_norm_w, m_attn_w_in, m_attn_q_norm_w, m_attn_k_norm_w, m_attn_sinks, m_attn_w_out, m_dn_norm_w, m_dn_w_in, m_dn_conv_w, m_dn_a_log, m_dn_dt_bias, m_dn_o_norm_w, m_dn_w_out, v_meta_tokens, v_attn_norm_w, v_attn_w_in, v_attn_q_norm_w, v_attn_k_norm_w, v_attn_sinks, v_attn_w_out, v_dn_norm_w, v_dn_w_in, v_dn_conv_w, v_dn_a_log, v_dn_dt_bias, v_dn_o_norm_w, v_dn_w_out):
    given = dict(zip(WEIGHTS, (meta_tokens, attn_norm_w, attn_w_in, attn_q_norm_w, attn_k_norm_w, attn_sinks,
                               attn_w_out, dn_norm_w, dn_w_in, dn_conv_w, dn_a_log, dn_dt_bias, dn_o_norm_w, dn_w_out)))
    mom1 = dict(zip(WEIGHTS, (m_meta_tokens, m_attn_norm_w, m_attn_w_in, m_attn_q_norm_w, m_attn_k_norm_w,
                              m_attn_sinks, m_attn_w_out, m_dn_norm_w, m_dn_w_in, m_dn_conv_w, m_dn_a_log,
                              m_dn_dt_bias, m_dn_o_norm_w, m_dn_w_out)))
    mom2 = dict(zip(WEIGHTS, (v_meta_tokens, v_attn_norm_w, v_attn_w_in, v_attn_q_norm_w, v_attn_k_norm_w,
                              v_attn_sinks, v_attn_w_out, v_dn_norm_w, v_dn_w_in, v_dn_conv_w, v_dn_a_log,
                              v_dn_dt_bias, v_dn_o_norm_w, v_dn_w_out)))
    two_d = lambda d: {n: _two_d(n, a) for n, a in d.items()}
    given, mom1, mom2 = two_d(given), two_d(mom1), two_d(mom2)
    c = lax.axis_index("c")

    me = 2 * lax.axis_index("x") + lax.axis_index("y")
    own_half = lax.dynamic_slice_in_dim(pack_gather(given), c * HALF_ROWS, HALF_ROWS, axis=0)
    mine = lax.dynamic_update_slice_in_dim(chips_exchange(own_half, True), own_half[None], me, 0)
    gathered = sibling_join(mine, "gather_swap")
    per_chip = [unpack_gather(gathered[j]) for j in range(N_CHIPS)]
    full = {n: jnp.concatenate([pc[n] for pc in per_chip], axis=SHARDED[n][1]) for n in SHARDED}
    full.update({n: given[n] for n in REPLICATED})

    loss, dx, grads = local_step(x[0], loss_target[0], full)

    split = lambda n: jnp.split(grads[n], N_CHIPS, axis=SHARDED[n][1])
    g_all = jnp.stack([pack_shard({n: split(n)[j] for n in SHARDED}) for j in range(N_CHIPS)])
    pair = pair_sum(g_all, sibling_give(g_all), c)
    half = chip_sum(chips_exchange(pair, False), pair, me)
    g_shard = sibling_join(half, "half_exchange")

    g_small = sum_blocks(all_gather_small(pack_small(grads)), "small_sum")

    g_local = unpack_shard(g_shard)
    g_local.update(unpack_small(g_small))
    steps = {n: adamw(given[n], g_local[n], mom1[n], mom2[n], "adamw_" + n) for n in WEIGHTS}
    shaped = lambda n, a: a[None] if n in LAYERED else a
    outs = [[shaped(n, g_local[n]) for n in WEIGHTS]]
    outs += [[shaped(n, steps[n][k]) for n in WEIGHTS] for k in range(3)]

    loss = lax.psum(loss, ("x", "y", "c"))
    return (loss, dx[None], *outs[0], *outs[1], *outs[2], *outs[3])
```

```python
import functools

import jax
import jax.numpy as jnp
from jax import lax
from jax.experimental import pallas as pl
from jax.experimental.pallas import tpu as pltpu

F32 = jnp.float32
BF16 = jnp.bfloat16
SDS = jax.ShapeDtypeStruct
MESH = pl.DeviceIdType.MESH

D_MODEL = 1024
N_META = 16
EPS = 1e-6
BLK = 128
CH = 64
PAD = BLK - N_META
HEADS = 16
HD = 64
KVW = 256
DN_H = 16
DN_KH = 8
DK = 128
SLOPES = [2.0 ** (-8.0 * (h + 1) / HEADS) for h in range(HEADS)]
NEG = -1e30
NT = (((1,), (1,)), ((), ()))
TN = (((0,), (0,)), ((), ()))
HI = lax.Precision.HIGHEST

ADAM_LR, ADAM_B1, ADAM_B2, ADAM_EPS, ADAM_WD, ADAM_STEP = 0.001, 0.9, 0.999, 1e-08, 0.01, 10

VMEM_LIMIT = 56 * 1024 * 1024
MXU_DEPTH = 256
WGRAD_BLOCK_ELEMS = 2 * 1024 * 1024


def _cp(*sem):
    return pltpu.CompilerParams(dimension_semantics=sem, vmem_limit_bytes=VMEM_LIMIT)


def _row_tile(rows):
    for t in (384, 256, 128):
        if rows % t == 0:
            return t
    raise ValueError(rows)


def _dot(a, b, dims=None, precision=None):
    if dims is None:
        return jnp.dot(a, b, preferred_element_type=F32, precision=precision)
    return lax.dot_general(a, b, dims, preferred_element_type=F32, precision=precision)


def _silu(x):
    return x * jax.nn.sigmoid(x)


def _dsilu(x):
    s = jax.nn.sigmoid(x)
    return s * (1.0 + x * (1.0 - s))


def _rms(x):
    return lax.rsqrt(jnp.mean(x * x, axis=-1, keepdims=True) + EPS)


def norm_matmul(h, nw, w, tn, name):
    rows, k = h.shape
    n = w.shape[1]
    tm = _row_tile(rows)

    def norm_body(h_ref, nw_ref, xn_ref):
        x = h_ref[...]
        xn_ref[...] = (x * _rms(x) * nw_ref[...]).astype(BF16)

    xn = pl.pallas_call(
        norm_body, grid=(rows // tm,),
        in_specs=[pl.BlockSpec((tm, k), lambda i: (i, 0)), pl.BlockSpec((1, k), lambda i: (0, 0))],
        out_specs=pl.BlockSpec((tm, k), lambda i: (i, 0)), out_shape=SDS((rows, k), BF16),
        compiler_params=_cp("parallel"), name=name + "_norm")(h, nw)

    def body(a_ref, w_ref, o_ref):
        o_ref[...] = _dot(a_ref[...], w_ref[...])

    out = pl.pallas_call(
        body, grid=(n // tn, rows // tm),
        in_specs=[pl.BlockSpec((tm, k), lambda j, i: (i, 0)), pl.BlockSpec((k, tn), lambda j, i: (0, j))],
        out_specs=pl.BlockSpec((tm, tn), lambda j, i: (i, j)), out_shape=SDS((rows, n), F32),
        compiler_params=_cp("parallel", "parallel"), name=name)(xn, w)
    return out, xn


def matmul_residual(a, w, res, name):
    rows, k = a.shape
    n = w.shape[1]
    tm = _row_tile(rows)

    def body(a_ref, w_ref, r_ref, o_ref):
        o_ref[...] = r_ref[...] + _dot(a_ref[...], w_ref[...])

    return pl.pallas_call(
        body, grid=(rows // tm,),
        in_specs=[pl.BlockSpec((tm, k), lambda i: (i, 0)), pl.BlockSpec((k, n), lambda i: (0, 0)),
                  pl.BlockSpec((tm, n), lambda i: (i, 0))],
        out_specs=pl.BlockSpec((tm, n), lambda i: (i, 0)),
        out_shape=SDS((rows, n), F32), compiler_params=_cp("parallel"), name=name)(a, w, res)


def wgrad(a, b, name):
    rows, k = a.shape
    n = b.shape[1]
    tm = _row_tile(rows)
    tn = min(n, WGRAD_BLOCK_ELEMS // k)

    def body(a_ref, b_ref, o_ref):
        @pl.when(pl.program_id(1) == 0)
        def _():
            o_ref[...] = jnp.zeros_like(o_ref)

        o_ref[...] += _dot(a_ref[...], b_ref[...].astype(BF16), TN)

    return pl.pallas_call(
        body, grid=(n // tn, rows // tm),
        in_specs=[pl.BlockSpec((tm, k), lambda j, i: (i, 0)), pl.BlockSpec((tm, tn), lambda j, i: (i, j))],
        out_specs=pl.BlockSpec((k, tn), lambda j, i: (0, j)),
        out_shape=SDS((k, n), F32), compiler_params=_cp("parallel", "arbitrary"), name=name)(a, b)


def in_proj_bwd(dus, ws, h, nw, dh_next, name):
    rows, k = h.shape
    tm = _row_tile(rows)
    nd = len(dus)
    nt = rows // tm

    def body(*refs):
        du_refs, w_refs = refs[:nd], refs[nd:2 * nd]
        h_ref, nw_ref, dhn_ref, dh_ref, dnw_ref = refs[2 * nd:]
        dxn = _dot(du_refs[0][...].astype(BF16), w_refs[0][...], NT)
        for du_ref, w_ref in zip(du_refs[1:], w_refs[1:]):
            dxn += _dot(du_ref[...].astype(BF16), w_ref[...], NT)
        x = h_ref[...]
        r = _rms(x)
        y = x * r
        gy = dxn * nw_ref[...]
        dh_ref[...] = dhn_ref[...] + r * (gy - y * jnp.mean(y * gy, axis=-1, keepdims=True))
        dnw_ref[0] = jnp.sum(dxn * y, axis=0, keepdims=True)

    in_specs = [pl.BlockSpec((tm, du.shape[1]), lambda i: (i, 0)) for du in dus]
    in_specs += [pl.BlockSpec(w.shape, lambda i: (0, 0)) for w in ws]
    in_specs += [pl.BlockSpec((tm, k), lambda i: (i, 0)), pl.BlockSpec((1, k), lambda i: (0, 0)),
                 pl.BlockSpec((tm, k), lambda i: (i, 0))]
    return pl.pallas_call(
        body, grid=(nt,), in_specs=in_specs,
        out_specs=[pl.BlockSpec((tm, k), lambda i: (i, 0)), pl.BlockSpec((1, 1, k), lambda i: (i, 0, 0))],
        out_shape=[SDS((rows, k), F32), SDS((nt, 1, k), F32)],
        compiler_params=_cp("parallel"), name=name)(*dus, *ws, h, nw, dh_next)


def matmul_nt(a, w, name):
    rows, k = a.shape
    n = w.shape[0]
    tm = _row_tile(rows)

    def body(a_ref, w_ref, o_ref):
        o_ref[...] = _dot(a_ref[...].astype(BF16), w_ref[...], NT)

    return pl.pallas_call(
        body, grid=(rows // tm,),
        in_specs=[pl.BlockSpec((tm, k), lambda i: (i, 0)), pl.BlockSpec((n, k), lambda i: (0, 0))],
        out_specs=pl.BlockSpec((tm, n), lambda i: (i, 0)),
        out_shape=SDS((rows, n), F32), compiler_params=_cp("parallel"), name=name)(a, w)


SUB = 64
GRP = 8
TR = GRP * SUB
NBAND = 192
TK = 256


def _tile_bias(n, sb):
    r = lax.broadcasted_iota(jnp.int32, (TR, TK), 0)
    c = lax.broadcasted_iota(jnp.int32, (TR, TK), 1)
    qi = r & (SUB - 1)
    d = BLK + qi - c
    dm = n * BLK + SUB * sb - PAD + NBAND + qi - c
    band = c < NBAND
    valid = (band & (d >= 0) & (d < BLK) & (c >= 2 * BLK - BLK * n - SUB * sb)) | (
        (c >= NBAND) & (c < NBAND + N_META) & (dm >= 0))
    return valid, jnp.where(band, d, jnp.minimum(dm, BLK)).astype(F32)


def _group_col(vals):
    g = lax.broadcasted_iota(jnp.int32, (TR, 1), 0) >> 6
    col = jnp.zeros((TR, 1), F32)
    for gi, v in enumerate(vals):
        col = jnp.where(g == gi, v, col)
    return col


def _stack_heads(ref, sb, kvh):
    return jnp.concatenate(
        [ref[SUB * sb:SUB * sb + SUB, HD * (GRP * kvh + g):HD * (GRP * kvh + g) + HD] for g in range(GRP)], axis=0)


def _unstack_heads(parts):
    return jnp.concatenate([parts[kvh][SUB * g:SUB * g + SUB] for kvh in range(2) for g in range(GRP)], axis=1)


def _tile_keys(band, meta, sb):
    return jnp.concatenate([band[SUB * sb:SUB * sb + NBAND], meta,
                            jnp.zeros((TK - NBAND - N_META, HD), band.dtype)], axis=0)


def _row_sums(x):
    ones = jnp.ones((x.shape[1], 128), BF16)
    hi = x.astype(BF16)
    lo = (x - hi.astype(F32)).astype(BF16)
    return _dot(hi, ones) + _dot(lo, ones)


def _rms_stack(q):
    return lax.rsqrt(_row_sums(q * q)[:, :HD] * (1.0 / HD) + EPS)


def _fill_bias(bias_scr, n):
    @pl.when(n <= 2)
    def _():
        for sb in range(2):
            valid, dist = _tile_bias(n, sb)
            for kvh in range(2):
                slope_col = _group_col([SLOPES[GRP * kvh + g] for g in range(GRP)])
                bias_scr[2 * sb + kvh] = jnp.where(valid, -slope_col * dist, NEG)


def _tile_vals(band, meta, sb):
    return jnp.concatenate([_tile_keys(band, meta, sb), jnp.ones((TK, 3 * HD), BF16)], axis=1)


def _tile_softmax(qn16, k16, vx16, bias, sink_col):
    s = _dot(qn16, k16, NT) * (HD ** -0.5) + bias
    mx = jnp.maximum(jnp.max(s.astype(BF16), axis=-1, keepdims=True).astype(F32), sink_col)
    e = jnp.exp(s - mx)
    es = jnp.exp(sink_col - mx)
    ox = _dot(e.astype(BF16), vx16)
    return e, 1.0 / (ox[:, 2 * HD:] + es), es, ox[:, :HD]


def _kv_heads(kvb, kvm, kw_):
    out = []
    for kvh in range(2):
        kb, km = kvb[:, HD * kvh:HD * kvh + HD], kvm[:, HD * kvh:HD * kvh + HD]
        out.append(((kb * _rms(kb) * kw_).astype(BF16), (km * _rms(km) * kw_).astype(BF16),
                    kvb[:, BLK + HD * kvh:BLK + HD * kvh + HD].astype(BF16),
                    kvm[:, BLK + HD * kvh:BLK + HD * kvh + HD].astype(BF16)))
    return out


def _kv_specs(nblk, clamp):
    cur = (lambda n: (jnp.minimum(n, nblk - 1), 8)) if clamp else (lambda n: (n, 8))
    return [pl.BlockSpec((BLK, KVW), cur),
            pl.BlockSpec((BLK, KVW), lambda n: (jnp.maximum(n - 1, 0), 8)),
            pl.BlockSpec((N_META, KVW), lambda n: (PAD // N_META, 8))]


def _sink_cols(sinks):
    return jnp.repeat(sinks.reshape(2, GRP), SUB, axis=1).reshape(2, TR, 1)


SINK_SPEC = pl.BlockSpec((2, TR, 1), lambda n: (0, 0, 0))


def attn_fwd(u, qw, kw, sinks):
    rows = u.shape[0]
    nblk = rows // BLK

    def body(q_ref, g_ref, kvc_ref, kvp_ref, kvm_ref, qw_ref, kw_ref, sc_ref, og_ref, bias_scr):
        _fill_bias(bias_scr, pl.program_id(0))
        qw_ = qw_ref[...]
        kv = _kv_heads(jnp.concatenate([kvp_ref[...], kvc_ref[...]], axis=0), kvm_ref[...], kw_ref[...])
        for sb in range(2):
            parts = []
            for kvh in range(2):
                knb, knm, vb, vm = kv[kvh]
                q = _stack_heads(q_ref, sb, kvh)
                qn16 = (q * _rms_stack(q) * qw_).astype(BF16)
                _, inv, _, o = _tile_softmax(qn16, _tile_keys(knb, knm, sb), _tile_vals(vb, vm, sb),
                                             bias_scr[2 * sb + kvh], sc_ref[kvh])
                parts.append(o * inv[:, :HD])
            rows = slice(SUB * sb, SUB * sb + SUB)
            og_ref[rows, :] = (_unstack_heads(parts) * _silu(g_ref[rows, :])).astype(BF16)

    small = lambda w: pl.BlockSpec((1, w), lambda n: (0, 0))
    return pl.pallas_call(
        body, grid=(nblk,),
        in_specs=[pl.BlockSpec((BLK, 1024), lambda n: (n, 0)), pl.BlockSpec((BLK, 1024), lambda n: (n, 1))]
        + _kv_specs(nblk, False) + [small(HD), small(HD), SINK_SPEC],
        out_specs=pl.BlockSpec((BLK, 1024), lambda n: (n, 0)),
        out_shape=SDS((rows, 1024), BF16), scratch_shapes=[pltpu.VMEM((4, TR, TK), F32)],
        compiler_params=_cp("arbitrary"), name="attn_fwd")(u, u, u, u, u, qw, kw, _sink_cols(sinks))


def attn_bwd(u, qw, kw, sinks, dog):
    rows = u.shape[0]
    nblk = rows // BLK

    def knorm_bwd(k, dkn, kw_):
        r = _rms(k)
        y = k * r
        gy = dkn * kw_
        return r * (gy - y * jnp.mean(y * gy, axis=-1, keepdims=True)), jnp.sum(dkn * y, axis=0, keepdims=True)

    def body(q_ref, g_ref, dog_ref, kvc_ref, kvp_ref, kvm_ref, qw_ref, kw_ref, sc_ref,
             dq_ref, dg_ref, dkv_ref, dkvm_ref, dqw_ref, dkw_ref, dsk_ref, carry, prevp, curp, metap, bias_scr):
        n = pl.program_id(0)
        qw_, kw_ = qw_ref[...], kw_ref[...]
        _fill_bias(bias_scr, n)

        @pl.when(n == 0)
        def _():
            carry[...] = jnp.zeros_like(carry)
            metap[...] = jnp.zeros_like(metap)
            dqw_ref[...] = jnp.zeros_like(dqw_ref)
            dkw_ref[...] = jnp.zeros_like(dkw_ref)
            dsk_ref[...] = jnp.zeros_like(dsk_ref)

        @pl.when(n == nblk)
        def _():
            prevp[...] = jnp.zeros_like(prevp)
            curp[...] = jnp.zeros_like(curp)

        @pl.when(n < nblk)
        def _():
            kv = _kv_heads(jnp.concatenate([kvp_ref[...], kvc_ref[...]], axis=0), kvm_ref[...], kw_)
            lane = lax.broadcasted_iota(jnp.int32, (1, HEADS), 1)
            dqw = jnp.zeros((1, HD), F32)
            dsk = jnp.zeros((1, HEADS), F32)
            band_parts = [jnp.zeros((2 * BLK, HD), F32) for _ in range(4)]
            meta_parts = [jnp.zeros((N_META, HD), F32) for _ in range(4)]

            def widen(x, sb):
                z = jnp.zeros((2 * BLK - NBAND, HD), F32)
                return jnp.concatenate([x, z] if sb == 0 else [z, x], axis=0)

            for sb in range(2):
                rows = slice(SUB * sb, SUB * sb + SUB)
                dq_parts, dg_parts = [], []
                for kvh in range(2):
                    knb, knm, vb, vm = kv[kvh]
                    k16, v16 = _tile_keys(knb, knm, sb), _tile_keys(vb, vm, sb)
                    q = _stack_heads(q_ref, sb, kvh)
                    r = _rms_stack(q)
                    y = q * r
                    qn16 = (y * qw_).astype(BF16)
                    e, inv, es, o = _tile_softmax(qn16, k16, _tile_vals(vb, vm, sb), bias_scr[2 * sb + kvh],
                                                  sc_ref[kvh])
                    p = e * jnp.concatenate([inv, inv], axis=1)
                    p16 = p.astype(BF16)
                    o = o * inv[:, :HD]
                    gate = _stack_heads(g_ref, sb, kvh)
                    dog_ = _stack_heads(dog_ref, sb, kvh)
                    dg_parts.append(dog_ * o * _dsilu(gate))
                    do_ = dog_ * _silu(gate)
                    do16 = do_.astype(BF16)
                    dp = _dot(do16, v16, NT)
                    delta = _row_sums(do_ * o)
                    ds16 = (p * (dp - jnp.concatenate([delta, delta], axis=1))).astype(BF16)
                    dsink = -(es * inv) * delta
                    for g in range(GRP):
                        dsk += jnp.where(lane == GRP * kvh + g,
                                         jnp.sum(dsink[SUB * g:SUB * g + SUB, :HEADS], axis=0, keepdims=True), 0.0)
                    dqn = _dot(ds16, k16) * (HD ** -0.5)
                    dk = (_dot((y * qw_).T.astype(BF16), ds16) * (HD ** -0.5)).T
                    dv = _dot(do_.T.astype(BF16), p16).T
                    band_parts[kvh] += widen(dk[:NBAND], sb)
                    band_parts[2 + kvh] += widen(dv[:NBAND], sb)
                    meta_parts[kvh] += dk[NBAND:NBAND + N_META]
                    meta_parts[2 + kvh] += dv[NBAND:NBAND + N_META]
                    gy = dqn * qw_
                    dq_parts.append(r * (gy - y * (_row_sums(y * gy)[:, :HD] * (1.0 / HD))))
                    dqw += jnp.sum(dqn * y, axis=0, keepdims=True)
                dq_ref[rows, :] = _unstack_heads(dq_parts).astype(BF16)
                dg_ref[rows, :] = _unstack_heads(dg_parts).astype(BF16)
            band = jnp.concatenate(band_parts, axis=1)
            prevp[...] = band[:BLK]
            curp[...] = band[BLK:]
            metap[...] += jnp.concatenate(meta_parts, axis=1)
            dqw_ref[...] += dqw
            dsk_ref[...] += dsk

        tot = carry[...] + prevp[...]
        kprev = kvp_ref[...]
        dk0, w0 = knorm_bwd(kprev[:, 0:HD], tot[:, 0:HD], kw_)
        dk1, w1 = knorm_bwd(kprev[:, HD:2 * HD], tot[:, HD:2 * HD], kw_)
        dkv_ref[...] = jnp.concatenate([dk0, dk1, tot[:, 2 * HD:]], axis=1)
        dkw_ref[...] += w0 + w1
        carry[...] = curp[...]

        @pl.when(n == nblk)
        def _():
            mt = metap[...]
            km = kvm_ref[...]
            m0, v0 = knorm_bwd(km[:, 0:HD], mt[:, 0:HD], kw_)
            m1, v1 = knorm_bwd(km[:, HD:2 * HD], mt[:, HD:2 * HD], kw_)
            dkvm_ref[...] = jnp.concatenate([m0, m1, mt[:, 2 * HD:]], axis=1)
            dkw_ref[...] += v0 + v1

    small = lambda w: pl.BlockSpec((1, w), lambda n: (0, 0))
    cl = lambda n: jnp.minimum(n, nblk - 1)
    return pl.pallas_call(
        body, grid=(nblk + 1,),
        in_specs=[pl.BlockSpec((BLK, 1024), lambda n: (cl(n), 0)), pl.BlockSpec((BLK, 1024), lambda n: (cl(n), 1)),
                  pl.BlockSpec((BLK, 1024), lambda n: (cl(n), 0))]
        + _kv_specs(nblk, True) + [small(HD), small(HD), SINK_SPEC],
        out_specs=[pl.BlockSpec((BLK, 1024), lambda n: (cl(n), 0)), pl.BlockSpec((BLK, 1024), lambda n: (cl(n), 0)),
                   pl.BlockSpec((BLK, KVW), lambda n: (jnp.maximum(n - 1, 0), 0)),
                   pl.BlockSpec((N_META, KVW), lambda n: (0, 0)), small(HD), small(HD), small(HEADS)],
        out_shape=[SDS((rows, 1024), BF16), SDS((rows, 1024), BF16), SDS((rows, KVW), F32), SDS((N_META, KVW), F32),
                   SDS((1, HD), F32), SDS((1, HD), F32), SDS((1, HEADS), F32)],
        scratch_shapes=[pltpu.VMEM((BLK, KVW), F32), pltpu.VMEM((BLK, KVW), F32), pltpu.VMEM((BLK, KVW), F32),
                        pltpu.VMEM((N_META, KVW), F32), pltpu.VMEM((4, TR, TK), F32)],
        compiler_params=_cp("arbitrary"), name="attn_bwd")(u, u, dog, u, u, u, qw, kw, _sink_cols(sinks))


GROUP_UNROLL = 4
HB = 16


def _bdot(a, b, kind, split=False, fused=False):
    dims = {"nn": ((2,), (1,)), "nt": ((2,), (2,)), "tn": ((1,), (1,))}[kind]
    dg = lambda p, q: lax.dot_general(p, q, (dims, ((0,), (0,))), preferred_element_type=F32)
    if not split:
        return dg(a, b)
    if fused:
        assert kind == "nn" and 3 * a.shape[2] <= MXU_DEPTH
        return _dot3(_split(a), _split(b))
    ah, bh = a.astype(BF16), b.astype(BF16)
    al, bl = (a - ah.astype(F32)).astype(BF16), (b - bh.astype(F32)).astype(BF16)
    return (dg(ah, bl) + dg(al, bh)) + dg(ah, bh)


def _head_cols(hv, beta, gc, gct, lane):
    sel = lane == hv
    return _pick(beta, sel), _pick(gc, sel), gct[pl.ds(hv, 1), :]


def _conv_group(xc_ref, xp_ref, cw_ref, off, first):
    xp = jnp.where(first, 0.0, xp_ref[:, pl.ds(off, DK)])
    xx = jnp.concatenate([xp, xc_ref[:, pl.ds(off, DK)]], axis=0)
    y = cw_ref[0:1, pl.ds(off, DK)] * xx[5:5 + CH]
    for j in range(1, 4):
        y += cw_ref[j:j + 1, pl.ds(off, DK)] * xx[5 + j:5 + j + CH]
    return xx, y


def _gates(ba, al, dtb, c):
    row = c * CH + lax.broadcasted_iota(jnp.int32, (CH, DN_H), 0)
    real = row >= PAD
    xa = ba[:, DN_H:2 * DN_H] + dtb
    beta = jnp.where(real, jax.nn.sigmoid(ba[:, 0:DN_H]), 0.0)
    g = jnp.where(real, -jnp.exp(al) * jax.nn.softplus(xa), 0.0)
    return real, xa, beta, g


def _pick(x, sel):
    return jnp.sum(jnp.where(sel, x, 0.0), axis=1, keepdims=True)


def _chunk_specs(width_blocks):
    return [pl.BlockSpec((CH, 4096), lambda c: (c, 0)),
            pl.BlockSpec((8, 4096), lambda c: (jnp.maximum(8 * c - 1, 0), 0)),
            pl.BlockSpec((CH, DK), lambda c: (c, 48))]


def _tri_inv(m, ii, jj):
    eye = (ii == jj).astype(BF16)
    mh, ml = _split(m)
    blk8 = (ii >> 3) == (jj >> 3)
    mb = (jnp.where(blk8, mh, 0), jnp.where(blk8, ml, 0))
    m2 = _split(_dot3(mb, mb))
    m4 = _split(_dot3(m2, m2))
    x = _dot3(_split(_dot3((eye - mb[0], -mb[1]), (eye + m2[0], m2[1]))), (eye + m4[0], m4[1]))
    for sh in (3, 4, 5):
        off = ((ii >> (sh + 1)) == (jj >> (sh + 1))) & ((ii >> sh) != (jj >> sh))
        xs = _split(x)
        x = x - _dot3(_split(_dot3(xs, (jnp.where(off, mh, 0), jnp.where(off, ml, 0)))), xs)
    return x


def _split(x):
    hi = x.astype(BF16)
    return hi, (x - hi.astype(F32)).astype(BF16)


def _dot3(a, b):
    lhs = jnp.concatenate([a[0], a[1], a[0]], axis=2)
    rhs = jnp.concatenate([b[0], b[0], b[1]], axis=1)
    return lax.dot_general(lhs, rhs, (((2,), (1,)), ((0,), (0,))), preferred_element_type=F32)


def dn_prep(udn, conv_w, a_log, dt_bias):
    rows = udn.shape[0]
    nch = rows // CH

    def body(xc_ref, xp_ref, ba_ref, cw_ref, al_ref, dtb_ref,
             qn_ref, kn_ref, sv_ref, gc_ref, beta_ref, u_ref, w_ref, qe_ref, ks_ref, p_ref, at_ref, pt_ref,
             qet_ref, wt_ref, kst_ref, y_ref, gct):
        c = pl.program_id(0)
        first = c == 0
        _, _, beta, g = _gates(ba_ref[...], al_ref[...], dtb_ref[...], c)
        ii = lax.broadcasted_iota(jnp.int32, (CH, CH), 0)
        jj = lax.broadcasted_iota(jnp.int32, (CH, CH), 1)
        gc = _dot((ii >= jj).astype(F32), g, precision=HI)
        gc_ref[...] = gc
        beta_ref[...] = beta
        gct[...] = gc.T

        def qk_body(kh, carry):
            off = pl.multiple_of(kh * DK, DK)
            _, yq = _conv_group(xc_ref, xp_ref, cw_ref, off, first)
            y_ref[:, pl.ds(off, DK)] = yq
            sq = _silu(yq)
            qn_ref[:, pl.ds(off, DK)] = sq * lax.rsqrt(jnp.sum(sq * sq, axis=-1, keepdims=True) + EPS) * (DK ** -0.5)
            _, yk = _conv_group(xc_ref, xp_ref, cw_ref, pl.multiple_of(1024 + kh * DK, DK), first)
            y_ref[:, pl.ds(pl.multiple_of(1024 + kh * DK, DK), DK)] = yk
            sk = _silu(yk)
            kn_ref[:, pl.ds(off, DK)] = sk * lax.rsqrt(jnp.sum(sk * sk, axis=-1, keepdims=True) + EPS)
            return carry

        lax.fori_loop(0, DN_KH, qk_body, 0, unroll=GROUP_UNROLL)
        lane = lax.broadcasted_iota(jnp.int32, (CH, DN_H), 1)
        zpad = jnp.zeros((CH, DK - CH), F32)

        def v_group(grp, carry):
            offs, ks_, qs_, vs_, cols = [], [], [], [], []
            for i in range(HB):
                hv = grp * HB + i
                offs.append(pl.multiple_of(hv * DK, DK))
                koff = pl.multiple_of((grp * (HB // 2) + i // 2) * DK, DK)
                _, yv = _conv_group(xc_ref, xp_ref, cw_ref, pl.multiple_of(2048 + hv * DK, DK), first)
                y_ref[:, pl.ds(pl.multiple_of(2048 + hv * DK, DK), DK)] = yv
                vs_.append(_silu(yv))
                sv_ref[:, pl.ds(offs[i], DK)] = vs_[i]
                ks_.append(kn_ref[:, pl.ds(koff, DK)])
                qs_.append(qn_ref[:, pl.ds(koff, DK)])
                cols.append(_head_cols(hv, beta, gc, gct, lane))
            k, q, v = jnp.stack(ks_), jnp.stack(qs_), jnp.stack(vs_)
            beta_c, gc_c, gc_r = (jnp.stack([c_[j] for c_ in cols]) for j in range(3))
            dec = jnp.exp(jnp.where(ii >= jj, gc_c - gc_r, NEG))
            eg = jnp.exp(gc_c)
            kb = k * beta_c
            k16 = k.astype(BF16)
            m = jnp.where(ii > jj, _bdot(kb.astype(BF16), k16, "nt") * dec, 0.0)
            a = _tri_inv(m, ii, jj)
            uw = _bdot(a, jnp.concatenate([v * beta_c, kb * eg], axis=2), "nn", True, True)
            p = _bdot(q.astype(BF16), k16, "nt") * dec
            qe = q * eg
            ksx = k * jnp.exp(gc_c[:, CH - 1:CH, :] - gc_c)
            tslot = lambda x: jnp.concatenate([x.T, jnp.zeros((DK, DK - CH), F32)], axis=1).astype(BF16)
            for i in range(HB):
                sl = pl.ds(offs[i], DK)
                u_ref[:, sl] = uw[i, :, :DK]
                w_ref[:, sl] = uw[i, :, DK:]
                qe_ref[:, sl] = qe[i].astype(BF16)
                ks_ref[:, sl] = ksx[i].astype(BF16)
                p_ref[:, sl] = jnp.concatenate([p[i], zpad], axis=1).astype(BF16)
                at_ref[:, sl] = jnp.concatenate([a[i].T, zpad], axis=1)
                pt_ref[:, sl] = jnp.concatenate([p[i].T, zpad], axis=1).astype(BF16)
                qet_ref[:, sl] = tslot(qe[i])
                wt_ref[:, sl] = tslot(uw[i, :, DK:])
                kst_ref[:, sl] = tslot(ksx[i])
            return carry

        lax.fori_loop(0, DN_H // HB, v_group, 0)

    full = lambda shape: pl.BlockSpec(shape, lambda c: (0, 0))
    blk = lambda w: pl.BlockSpec((CH, w), lambda c: (c, 0))
    return pl.pallas_call(
        body, grid=(nch,),
        in_specs=_chunk_specs(0) + [full((4, 4096)), full((1, DN_H)), full((1, DN_H))],
        out_specs=[blk(1024), blk(1024), blk(2048), blk(DN_H), blk(DN_H), blk(2048), blk(2048), blk(2048), blk(2048),
                   blk(2048), blk(2048), blk(2048)] + [pl.BlockSpec((DK, 2048), lambda c: (c, 0))] * 3 + [blk(4096)],
        out_shape=[SDS((rows, 1024), F32), SDS((rows, 1024), F32), SDS((rows, 2048), F32), SDS((rows, DN_H), F32),
                   SDS((rows, DN_H), F32), SDS((rows, 2048), F32), SDS((rows, 2048), F32), SDS((rows, 2048), BF16),
                   SDS((rows, 2048), BF16), SDS((rows, 2048), BF16), SDS((rows, 2048), F32),
                   SDS((rows, 2048), BF16)] + [SDS((2 * rows, 2048), BF16)] * 3 + [SDS((rows, 4096), F32)],
        scratch_shapes=[pltpu.VMEM((DN_H, CH), F32)],
        compiler_params=_cp("parallel"), name="dn_prep")(udn, udn, udn, conv_w, a_log, dt_bias)


def dn_scan(u, w, qe, kst, p, gc):
    rows = u.shape[0]
    nch = rows // CH

    def body(u_ref, w_ref, qe_ref, kst_ref, p_ref, gc_ref, o_ref, vn_ref, st_ref, s_scr):
        @pl.when(pl.program_id(0) == 0)
        def _():
            s_scr[...] = jnp.zeros_like(s_scr)

        gl_row = gc_ref[CH - 1:CH, :]
        lane = lax.broadcasted_iota(jnp.int32, (1, DN_H), 1)

        def group(grp, carry):
            base = grp * HB
            sls = [pl.ds(pl.multiple_of((base + i) * DK, DK), DK) for i in range(HB)]
            heads = lambda ref: jnp.stack([ref[:, sl] for sl in sls])
            s = s_scr[pl.ds(base, HB)]
            st_ref[0, pl.ds(base, HB)] = s
            s16 = s.astype(BF16)
            vn = heads(u_ref) - _bdot(heads(w_ref).astype(BF16), s16, "nn")
            vn16 = vn.astype(BF16)
            o = _bdot(heads(qe_ref), s16, "nn") + _bdot(heads(p_ref)[:, :, 0:CH], vn16, "nn")
            egl = jnp.exp(jnp.stack([_pick(gl_row, lane == base + i) for i in range(HB)]))
            s_scr[pl.ds(base, HB)] = s * egl + _bdot(heads(kst_ref)[:, :, 0:CH], vn16, "nn")
            for i in range(HB):
                vn_ref[:, sls[i]] = vn16[i]
                o_ref[:, sls[i]] = o[i]
            return carry

        lax.fori_loop(0, DN_H // HB, group, 0)

    blk = lambda wd: pl.BlockSpec((CH, wd), lambda c: (c, 0))
    return pl.pallas_call(
        body, grid=(nch,),
        in_specs=[blk(2048)] * 3 + [pl.BlockSpec((DK, 2048), lambda c: (c, 0)), blk(2048), blk(DN_H)],
        out_specs=[blk(2048), blk(2048), pl.BlockSpec((1, DN_H, DK, DK), lambda c: (c, 0, 0, 0))],
        out_shape=[SDS((rows, 2048), F32), SDS((rows, 2048), BF16), SDS((nch, DN_H, DK, DK), F32)],
        scratch_shapes=[pltpu.VMEM((DN_H, DK, DK), F32)],
        compiler_params=_cp("arbitrary"), name="dn_scan")(u, w, qe, kst, p, gc)


def dn_out_fwd(o, udn, ow, wout, h1, tgt):
    rows = o.shape[0]
    tm = _row_tile(rows)
    nt = rows // tm

    def body(o_ref, z_ref, ow_ref, w_ref, h_ref, t_ref, dh_ref, on_ref, ls_ref):
        for hv in range(DN_H):
            sl = slice(hv * DK, hv * DK + DK)
            oh = o_ref[:, sl]
            on_ref[:, sl] = (oh * _rms(oh) * ow_ref[...] * _silu(z_ref[:, sl])).astype(BF16)
        h2 = h_ref[...] + _dot(on_ref[...], w_ref[...])
        row = pl.program_id(0) * tm + lax.broadcasted_iota(jnp.int32, (tm, 1), 0)
        err = jnp.where(row >= BLK, h2 - t_ref[...], 0.0)
        dh_ref[...] = err * (1.0 / D_MODEL)
        ls_ref[0] = jnp.sum(err * err, axis=0, keepdims=True)

    return pl.pallas_call(
        body, grid=(nt,),
        in_specs=[pl.BlockSpec((tm, 2048), lambda i: (i, 0)), pl.BlockSpec((tm, 2048), lambda i: (i, 2)),
                  pl.BlockSpec((1, DK), lambda i: (0, 0)), pl.BlockSpec((2048, D_MODEL), lambda i: (0, 0)),
                  pl.BlockSpec((tm, D_MODEL), lambda i: (i, 0)), pl.BlockSpec((tm, D_MODEL), lambda i: (i, 0))],
        out_specs=[pl.BlockSpec((tm, D_MODEL), lambda i: (i, 0)), pl.BlockSpec((tm, 2048), lambda i: (i, 0)),
                   pl.BlockSpec((1, 1, D_MODEL), lambda i: (i, 0, 0))],
        out_shape=[SDS((rows, D_MODEL), F32), SDS((rows, 2048), BF16), SDS((nt, 1, D_MODEL), F32)],
        compiler_params=_cp("parallel"), name="dn_out_fwd")(o, udn, ow, wout, h1, tgt)


def dn_out_bwd(dh2, wout, o, udn, ow):
    rows = o.shape[0]
    tm = _row_tile(rows)
    nt = rows // tm

    def body(dh_ref, w_ref, o_ref, z_ref, ow_ref, do_ref, dz_ref, dow_ref):
        don = _dot(dh_ref[...].astype(BF16), w_ref[...], NT)
        ow_ = ow_ref[...]
        dow = jnp.zeros((1, DK), F32)
        for hv in range(DN_H):
            sl = slice(hv * DK, hv * DK + DK)
            oh = o_ref[:, sl]
            r = _rms(oh)
            y = oh * r
            z = z_ref[:, sl]
            dn = don[:, sl] * _silu(z)
            dz_ref[:, sl] = (don[:, sl] * (y * ow_) * _dsilu(z)).astype(BF16)
            dy = dn * ow_
            do_ref[:, sl] = r * (dy - y * jnp.mean(y * dy, axis=-1, keepdims=True))
            dow += jnp.sum(dn * y, axis=0, keepdims=True)
        dow_ref[0] = dow

    return pl.pallas_call(
        body, grid=(nt,),
        in_specs=[pl.BlockSpec((tm, D_MODEL), lambda i: (i, 0)), pl.BlockSpec((2048, D_MODEL), lambda i: (0, 0)),
                  pl.BlockSpec((tm, 2048), lambda i: (i, 0)), pl.BlockSpec((tm, 2048), lambda i: (i, 2)),
                  pl.BlockSpec((1, DK), lambda i: (0, 0))],
        out_specs=[pl.BlockSpec((tm, 2048), lambda i: (i, 0)), pl.BlockSpec((tm, 2048), lambda i: (i, 0)),
                   pl.BlockSpec((1, 1, DK), lambda i: (i, 0, 0))],
        out_shape=[SDS((rows, 2048), F32), SDS((rows, 2048), BF16), SDS((nt, 1, DK), F32)],
        compiler_params=_cp("parallel"), name="dn_out_bwd")(dh2, wout, o, udn, ow)


def dn_scan_bwd(do, qn, kn, sv, gc, beta, at, pt, u, w, vn, qet, wt, ks, st):
    rows = do.shape[0]
    nch = rows // CH

    def body(do_ref, q_ref, k_ref, v_ref, gc_ref, beta_ref, at_ref, pt_ref, u_ref, w_ref, vn_ref, qet_ref, wt_ref,
             ks_ref, st_ref, dq_ref, dk_ref, dv_ref, dbeta_ref, dg_ref, ds_scr, gct):
        @pl.when(pl.program_id(0) == 0)
        def _():
            ds_scr[...] = jnp.zeros_like(ds_scr)

        gc, beta = gc_ref[...], beta_ref[...]
        gct[...] = gc.T
        ii = lax.broadcasted_iota(jnp.int32, (CH, CH), 0)
        jj = lax.broadcasted_iota(jnp.int32, (CH, CH), 1)
        lane = lax.broadcasted_iota(jnp.int32, (CH, DN_H), 1)
        last = lax.broadcasted_iota(jnp.int32, (CH, 1), 0) == CH - 1

        def group(grp, carry):
            dbeta_acc, dgc_acc = carry
            base = grp * HB
            sls = [pl.ds(pl.multiple_of((base + i) * DK, DK), DK) for i in range(HB)]
            ksls = [pl.ds(pl.multiple_of((grp * (HB // 2) + j) * DK, DK), DK) for j in range(HB // 2)]
            heads = lambda ref: jnp.stack([ref[:, sl] for sl in sls])
            kheads = lambda ref: jnp.stack([ref[:, ksls[i // 2]] for i in range(HB)])
            cols = [_head_cols(base + i, beta, gc, gct, lane) for i in range(HB)]
            beta_c, gc_c, gc_r = (jnp.stack([c_[j] for c_ in cols]) for j in range(3))
            k, q, v = kheads(k_ref), kheads(q_ref), heads(v_ref)
            dec = jnp.exp(jnp.where(ii >= jj, gc_c - gc_r, NEG))
            eg = jnp.exp(gc_c)
            gl = gc_c[:, CH - 1:CH, :]
            e2 = jnp.exp(gl - gc_c)
            egl = jnp.exp(gl)
            k16, q16 = k.astype(BF16), q.astype(BF16)
            do16 = heads(do_ref).astype(BF16)
            s = st_ref[0, pl.ds(base, HB)]
            s16 = s.astype(BF16)
            dso = ds_scr[pl.ds(base, HB)]
            dso16 = dso.astype(BF16)
            wf, uf, vn16 = heads(w_ref), heads(u_ref), heads(vn_ref)
            kb = k * beta_c
            kb16 = kb.astype(BF16)
            pm = _bdot(q16, k16, "nt") * dec
            m = jnp.where(ii > jj, _bdot(kb16, k16, "nt") * dec, 0.0)
            dvn = _bdot(heads(pt_ref)[:, :, 0:CH], do16, "nn") + _bdot(heads(ks_ref), dso16, "nn")
            dvn16 = dvn.astype(BF16)
            ds_scr[pl.ds(base, HB)] = (egl * dso + _bdot(heads(qet_ref)[:, :, 0:CH], do16, "nn")
                                       - _bdot(heads(wt_ref)[:, :, 0:CH], dvn16, "nn"))
            dpm = jnp.where(ii >= jj, _bdot(do16, vn16, "nt"), 0.0)
            dqk16 = (dpm * dec).astype(BF16)
            dqe = _bdot(do16, s16, "nt")
            dq = eg * dqe + _bdot(dqk16, k16, "nn")
            dks = _bdot(vn16, dso16, "nt")
            dw = -_bdot(dvn16, s16, "nt")
            dbvk = _bdot(heads(at_ref)[:, :, 0:CH], jnp.concatenate([dvn, dw], axis=2), "nn", True)
            dbv, dbk = dbvk[:, :, :DK], dbvk[:, :, DK:]
            dm = jnp.where(ii > jj, -_bdot(dbvk, jnp.concatenate([uf, wf], axis=2), "nt", True), 0.0)
            g16 = (dm * dec).astype(BF16)
            dkb = _bdot(g16, k16, "nn")
            dk = (_bdot(dqk16, q16, "tn") + e2 * dks + _bdot(g16, kb16, "tn") + beta_c * (eg * dbk + dkb))
            e = dpm * pm + dm * m
            rsum = lambda x: jnp.sum(x, axis=2, keepdims=True)
            r_bk, r_qe, r_beta, r_ks = rsum(dbk * k), rsum(q * dqe), rsum(dbv * v + dkb * k), rsum(dks * k)
            t = r_ks * e2
            dgl = jnp.sum(t, axis=1, keepdims=True) + egl * rsum(jnp.sum(dso * s, axis=1, keepdims=True))
            deg = r_qe + beta_c * r_bk
            dgc = rsum(e) - t + deg * eg + jnp.where(last, dgl, 0.0)
            dgrow = -jnp.sum(e, axis=1, keepdims=True)
            dv = beta_c * dbv
            dbeta = r_beta + eg * r_bk
            for i in range(HB):
                dv_ref[:, sls[i]] = dv[i]
                sel = lane == base + i
                dbeta_acc = jnp.where(sel, dbeta[i], dbeta_acc)
                dgc_acc = jnp.where(sel, dgc[i], dgc_acc)
                gct[pl.ds(base + i, 1), :] = dgrow[i]
            for j in range(HB // 2):
                dq_ref[:, ksls[j]] = dq[2 * j] + dq[2 * j + 1]
                dk_ref[:, ksls[j]] = dk[2 * j] + dk[2 * j + 1]
            return dbeta_acc, dgc_acc

        zero = jnp.zeros((CH, DN_H), F32)
        dbeta_acc, dgc_acc = lax.fori_loop(0, DN_H // HB, group, (zero, zero))
        dbeta_ref[...] = dbeta_acc
        dg_ref[...] = _dot((ii <= jj).astype(F32), dgc_acc + gct[...].T, precision=HI)

    rev = lambda wd: pl.BlockSpec((CH, wd), lambda i: (nch - 1 - i, 0))
    rev_t = pl.BlockSpec((DK, 2048), lambda i: (nch - 1 - i, 0))
    return pl.pallas_call(
        body, grid=(nch,),
        in_specs=[rev(2048), rev(1024), rev(1024), rev(2048), rev(DN_H), rev(DN_H), rev(2048), rev(2048), rev(2048),
                  rev(2048), rev(2048), rev_t, rev_t, rev(2048),
                  pl.BlockSpec((1, DN_H, DK, DK), lambda i: (nch - 1 - i, 0, 0, 0))],
        out_specs=[rev(1024), rev(1024), rev(2048), rev(DN_H), rev(DN_H)],
        out_shape=[SDS((rows, 1024), F32), SDS((rows, 1024), F32), SDS((rows, 2048), F32), SDS((rows, DN_H), F32),
                   SDS((rows, DN_H), F32)],
        scratch_shapes=[pltpu.VMEM((DN_H, DK, DK), F32), pltpu.VMEM((DN_H, CH), F32)],
        compiler_params=_cp("arbitrary"), name="dn_scan_bwd")(
            do, qn, kn, sv, gc, beta, at, pt, u, w, vn, qet, wt, ks, st)


def dn_prep_bwd(udn, yconv, conv_w, a_log, dt_bias, dqn, dkn, dv, dbeta, dg):
    rows = udn.shape[0]
    nch = rows // CH
    ext = CH + 8

    def body(xc_ref, ba_ref, yc_ref, yn_ref, dqn_n, dkn_n, dv_n, cw_ref, al_ref, dtb_ref, dqn_ref, dkn_ref, dv_ref,
             dbeta_ref, dg_ref, dx_ref, dba_ref, dcw_ref, dal_ref, ddtb_ref):
        c = pl.program_id(0)
        first = c == 0
        own = (lax.broadcasted_iota(jnp.int32, (ext, 1), 0) < CH) | (c < nch - 1)

        @pl.when(first)
        def _():
            dcw_ref[...] = jnp.zeros_like(dcw_ref)
            dal_ref[...] = jnp.zeros_like(dal_ref)
            ddtb_ref[...] = jnp.zeros_like(ddtb_ref)

        real, xa, beta, g = _gates(ba_ref[...], al_ref[...], dtb_ref[...], c)
        dgm = jnp.where(real, dg_ref[...], 0.0)
        da = dgm * (-jnp.exp(al_ref[...])) * jax.nn.sigmoid(xa)
        dal_ref[...] += jnp.sum(dgm * g, axis=0, keepdims=True)
        ddtb_ref[...] += jnp.sum(da, axis=0, keepdims=True)
        dba_ref[...] = jnp.zeros_like(dba_ref)
        dba_ref[:, 0:DN_H] = jnp.where(real, dbeta_ref[...] * beta * (1.0 - beta), 0.0)
        dba_ref[:, DN_H:2 * DN_H] = da

        def through_conv(off, g_cur, g_next, grad_fn):
            sl = pl.ds(off, DK)
            y = jnp.concatenate([yc_ref[:, sl], yn_ref[:, sl]], axis=0)
            sg = jax.nn.sigmoid(y)
            dsilu = sg * (1.0 + y * (1.0 - sg))
            dy = jnp.where(own, grad_fn(y * sg, jnp.concatenate([g_cur, g_next], axis=0)) * dsilu, 0.0)
            shifted = [dy[3 - j:3 - j + CH] for j in range(4)]
            x = xc_ref[:, sl]
            dx = cw_ref[0:1, sl] * shifted[0]
            for j in range(1, 4):
                dx += cw_ref[j:j + 1, sl] * shifted[j]
            dx_ref[:, sl] = dx.astype(BF16)
            for j in range(4):
                dcw_ref[j:j + 1, sl] += jnp.sum(shifted[j] * x, axis=0, keepdims=True)

        def l2_bwd(scale):
            def f(s, gin):
                r = lax.rsqrt(jnp.sum(s * s, axis=-1, keepdims=True) + EPS)
                nrm = s * r
                return (r * scale) * (gin - nrm * jnp.sum(nrm * gin, axis=-1, keepdims=True))
            return f

        def qk_body(kh, carry):
            sl = pl.ds(pl.multiple_of(kh * DK, DK), DK)
            through_conv(pl.multiple_of(kh * DK, DK), dqn_ref[:, sl], dqn_n[:, sl], l2_bwd(DK ** -0.5))
            through_conv(pl.multiple_of(1024 + kh * DK, DK), dkn_ref[:, sl], dkn_n[:, sl], l2_bwd(1.0))
            return carry

        lax.fori_loop(0, DN_KH, qk_body, 0, unroll=GROUP_UNROLL)

        def v_body(hv, carry):
            sl = pl.ds(pl.multiple_of(hv * DK, DK), DK)
            through_conv(pl.multiple_of(2048 + hv * DK, DK), dv_ref[:, sl], dv_n[:, sl], lambda s, gin: gin)
            return carry

        lax.fori_loop(0, DN_H, v_body, 0, unroll=GROUP_UNROLL)

    full = lambda shape: pl.BlockSpec(shape, lambda c: (0, 0))
    blk = lambda w: pl.BlockSpec((CH, w), lambda c: (c, 0))
    nxt = lambda w: pl.BlockSpec((8, w), lambda c: (jnp.minimum(8 * c + 8, rows // 8 - 1), 0))
    return pl.pallas_call(
        body, grid=(nch,),
        in_specs=[_chunk_specs(0)[0], _chunk_specs(0)[2], blk(4096), nxt(4096), nxt(1024), nxt(1024), nxt(2048),
                  full((4, 4096)), full((1, DN_H)), full((1, DN_H)), blk(1024), blk(1024), blk(2048), blk(DN_H),
                  blk(DN_H)],
        out_specs=[blk(4096), blk(DK), full((8, 4096)), full((1, DN_H)), full((1, DN_H))],
        out_shape=[SDS((rows, 4096), BF16), SDS((rows, DK), F32), SDS((8, 4096), F32), SDS((1, DN_H), F32),
                   SDS((1, DN_H), F32)],
        compiler_params=_cp("arbitrary"), name="dn_prep_bwd")(
            udn, udn, yconv, yconv, dqn, dkn, dv, conv_w, a_log, dt_bias, dqn, dkn, dv, dbeta, dg)


def local_step(x, target, w):
    seq = x.shape[0]
    bf = lambda a: a.astype(BF16)
    h0 = jnp.concatenate([jnp.zeros((PAD, D_MODEL), F32), w["meta_tokens"], x], axis=0)
    tgt = jnp.concatenate([jnp.zeros((BLK, D_MODEL), F32), target], axis=0)
    win = w["attn_w_in"]
    wq, wkv, wg = win[:, :1024], win[:, 1024:1280], win[:, 1280:]
    wa_in = bf(jnp.concatenate([wq, wg, wkv], axis=1))
    wa_out = bf(w["attn_w_out"])
    wd_in = jnp.concatenate([bf(w["dn_w_in"]), jnp.zeros((D_MODEL, 96), BF16)], axis=1)
    wd_out = bf(w["dn_w_out"])
    qw, kw, sinks = w["attn_q_norm_w"], w["attn_k_norm_w"], w["attn_sinks"]
    cw, al, dtb, ow = w["dn_conv_w"], w["dn_a_log"], w["dn_dt_bias"], w["dn_o_norm_w"]

    ua, xn0 = norm_matmul(h0, w["attn_norm_w"], wa_in, 2304, "attn_in")
    og = attn_fwd(ua, qw, kw, sinks)
    h1 = matmul_residual(og, wa_out, h0, "attn_out")
    ud, xn1 = norm_matmul(h1, w["dn_norm_w"], wd_in, 6272, "dn_in")
    qn, kn, sv, gc, beta, u, wy, qe, ks, p, at, pt, qet, wt, kst, yconv = dn_prep(ud, cw, al, dtb)
    o, vn, st = dn_scan(u, wy, qe, kst, p, gc)
    dh2, on, ls = dn_out_fwd(o, ud, ow, wd_out, h1, tgt)
    loss = (0.5 / D_MODEL) * jnp.sum(ls)

    do, dz, dow = dn_out_bwd(dh2, wd_out, o, ud, ow)
    g_dn_out = wgrad(on, dh2, "dn_out_wgrad")
    dqn, dkn, dv, dbeta, dg = dn_scan_bwd(do, qn, kn, sv, gc, beta, at, pt, u, wy, vn, qet, wt, ks, st)
    dxc, dba, dcw, dal, ddtb = dn_prep_bwd(ud, yconv, cw, al, dtb, dqn, dkn, dv, dbeta, dg)
    dh1, dnw1 = in_proj_bwd([dxc, dz, dba], [wd_in[:, :4096], wd_in[:, 4096:6144], wd_in[:, 6144:]],
                            h1, w["dn_norm_w"], dh2, "dn_in_bwd")
    g_dn_in = jnp.concatenate([wgrad(xn1, dxc, "dn_in_wgrad_qkv"), wgrad(xn1, dz, "dn_in_wgrad_z"),
                               wgrad(xn1, dba, "dn_in_wgrad_ba")[:, :2 * DN_H]], axis=1)

    dog = matmul_nt(dh1, wa_out, "attn_out_bwd")
    g_attn_out = wgrad(og, dh1, "attn_out_wgrad")
    dq, dgate, dkv, dkvm, dqw, dkw, dsk = attn_bwd(ua, qw, kw, sinks, dog)
    dkv = dkv.at[PAD:BLK].add(dkvm)
    dh0, dnw0 = in_proj_bwd([dq, dgate, dkv], [wa_in[:, :1024], wa_in[:, 1024:2048], wa_in[:, 2048:]],
                            h0, w["attn_norm_w"], dh1, "attn_in_bwd")
    g_attn_in = jnp.concatenate([wgrad(xn0, dq, "attn_in_wgrad_q"), wgrad(xn0, dkv, "attn_in_wgrad_kv"),
                                 wgrad(xn0, dgate, "attn_in_wgrad_g")], axis=1)
    grads = {
        "meta_tokens": dh0[PAD:BLK], "attn_norm_w": jnp.sum(dnw0, axis=0), "attn_w_in": g_attn_in,
        "attn_q_norm_w": dqw, "attn_k_norm_w": dkw, "attn_sinks": dsk, "attn_w_out": g_attn_out,
        "dn_norm_w": jnp.sum(dnw1, axis=0), "dn_w_in": g_dn_in, "dn_conv_w": dcw[:4], "dn_a_log": dal,
        "dn_dt_bias": ddtb, "dn_o_norm_w": jnp.sum(dow, axis=0), "dn_w_out": g_dn_out,
    }
    return loss, dh0[BLK:BLK + seq], grads


WEIGHTS = ["meta_tokens", "attn_norm_w", "attn_w_in", "attn_q_norm_w", "attn_k_norm_w", "attn_sinks", "attn_w_out",
           "dn_norm_w", "dn_w_in", "dn_conv_w", "dn_a_log", "dn_dt_bias", "dn_o_norm_w", "dn_w_out"]
SHARDED = {"attn_w_in": ((1024, 2304), 1), "attn_w_out": ((1024, 1024), 0), "dn_w_in": ((1024, 6176), 1),
           "dn_w_out": ((2048, 1024), 0), "dn_conv_w": ((4, 4096), 1), "meta_tokens": ((16, 1024), 1),
           "dn_norm_w": ((1, 1024), 1)}
REPLICATED = {"attn_norm_w": 1024, "attn_q_norm_w": 64, "attn_k_norm_w": 64, "attn_sinks": 16, "dn_a_log": 16,
              "dn_dt_bias": 16, "dn_o_norm_w": 128}
N_CHIPS = 4
PACK_ROWS = 2912
HALF_ROWS = PACK_ROWS // 2
SMALL_ROWS = 8


def _shard_shape(name):
    (r, c), axis = SHARDED[name]
    return (r // N_CHIPS, c) if axis == 0 else (r, c // N_CHIPS)


def _pack(parts, rows):
    flat = jnp.concatenate([p.reshape(-1) for p in parts])
    return jnp.pad(flat, (0, rows * 1024 - flat.shape[0])).reshape(rows, 1024)


def pack_shard(shards):
    return _pack([shards[n] for n in SHARDED], PACK_ROWS)


def unpack_shard(buf):
    flat, out, pos = buf.reshape(-1), {}, 0
    for n in SHARDED:
        shp = _shard_shape(n)
        size = shp[0] * shp[1]
        out[n] = flat[pos:pos + size].reshape(shp)
        pos += size
    return out


MATRICES = ("attn_w_in", "attn_w_out", "dn_w_in", "dn_w_out")


def pack_gather(shards):
    big = [shards[n].astype(BF16).reshape(-1) for n in MATRICES]
    small = jnp.concatenate([shards[n].reshape(-1) for n in SHARDED if n not in MATRICES])
    flat = jnp.concatenate(big + [lax.bitcast_convert_type(small, BF16).reshape(-1)])
    return jnp.pad(flat, (0, PACK_ROWS * 1024 - flat.shape[0])).reshape(PACK_ROWS, 1024)


def unpack_gather(buf):
    PER_F32 = 4 // jnp.dtype(buf.dtype).itemsize
    flat, out, pos = buf.reshape(-1), {}, 0
    for n in MATRICES:
        shp = _shard_shape(n)
        out[n] = flat[pos:pos + shp[0] * shp[1]].reshape(shp)
        pos += shp[0] * shp[1]
    for n in SHARDED:
        if n not in MATRICES:
            shp = _shard_shape(n)
            raw = flat[pos:pos + shp[0] * shp[1] * PER_F32]
            out[n] = lax.bitcast_convert_type(raw.reshape(-1, PER_F32) if PER_F32 > 1 else raw, F32).reshape(shp)
            pos += shp[0] * shp[1] * PER_F32
    return out


def pack_small(vals):
    return _pack([vals[n] for n in REPLICATED], SMALL_ROWS)


def unpack_small(buf):
    flat, out, pos = buf.reshape(-1), {}, 0
    for n, size in REPLICATED.items():
        out[n] = flat[pos:pos + size].reshape(1, size)
        pos += size
    return out


ANY = pl.BlockSpec(memory_space=pl.ANY)


def _place():
    return lax.axis_index("x"), lax.axis_index("y"), lax.axis_index("c")


def chips_exchange(src, gather):
    r = src.shape[-2]

    def body(s_ref, o_ref, send_sems, recv_sems):
        x, y, c = _place()
        me = 2 * x + y
        peers = [(1 - x, y), (x, 1 - y), (1 - x, 1 - y)]

        def copy(k, to_block, from_block):
            px, py = peers[k]
            return pltpu.make_async_remote_copy(
                src_ref=s_ref if gather else s_ref.at[to_block], dst_ref=o_ref.at[from_block],
                send_sem=send_sems.at[k], recv_sem=recv_sems.at[k], device_id=(px, py, c), device_id_type=MESH)

        sends = [copy(k, 2 * px + py, me) for k, (px, py) in enumerate(peers)]
        for cp in sends:
            cp.start()
        for k, (px, py) in enumerate(peers):
            copy(k, me, 2 * px + py).wait_recv()
        for cp in sends:
            cp.wait_send()

    return pl.pallas_call(
        body, in_specs=[ANY], out_specs=ANY, out_shape=SDS((N_CHIPS, r, 1024), src.dtype),
        scratch_shapes=[pltpu.SemaphoreType.DMA((3,)), pltpu.SemaphoreType.DMA((3,))],
        name="chips_gather" if gather else "chips_exchange")(src)


def chip_sum(received, pair, me):
    tm = 208

    def body(me_ref, own_ref, r1_ref, r2_ref, r3_ref, o_ref):
        o_ref[...] = ((own_ref[0] + r1_ref[0].astype(F32)) + r2_ref[0].astype(F32)) + r3_ref[0].astype(F32)

    blk = lambda k: pl.BlockSpec((1, tm, 1024), lambda i, me_ref: ((me_ref[0] + k) % N_CHIPS, i, 0))
    return pl.pallas_call(
        body,
        grid_spec=pltpu.PrefetchScalarGridSpec(
            num_scalar_prefetch=1, grid=(HALF_ROWS // tm,), in_specs=[blk(0), blk(1), blk(2), blk(3)],
            out_specs=pl.BlockSpec((tm, 1024), lambda i, me_ref: (i, 0))),
        out_shape=SDS((HALF_ROWS, 1024), F32), compiler_params=_cp("parallel"), name="chip_sum")(
            me.reshape(1).astype(jnp.int32), pair, received, received, received)


def _rows_at(ref, start, size):
    return ref.at[:, pl.ds(start, size), :] if len(ref.shape) == 3 else ref.at[pl.ds(start, size), :]


def sibling_join(src, name):
    axis = len(src.shape) - 2

    def body(s_ref, o_ref, send_sem, recv_sem):
        x, y, c = _place()
        cp = pltpu.make_async_remote_copy(src_ref=s_ref, dst_ref=o_ref, send_sem=send_sem, recv_sem=recv_sem,
                                          device_id=(x, y, 1 - c), device_id_type=MESH)
        cp.start()
        cp.wait()

    theirs = pl.pallas_call(
        body, in_specs=[ANY], out_specs=ANY, out_shape=SDS(src.shape, src.dtype),
        scratch_shapes=[pltpu.SemaphoreType.DMA, pltpu.SemaphoreType.DMA], name=name)(src)
    first = lax.axis_index("c") == 0
    return jnp.concatenate([jnp.where(first, src, theirs), jnp.where(first, theirs, src)], axis=axis)


def sibling_give(g_all):
    def body(s_ref, o_ref, send_sem, recv_sem):
        x, y, c = _place()
        cp = pltpu.make_async_remote_copy(
            src_ref=_rows_at(s_ref, (1 - c) * HALF_ROWS, HALF_ROWS), dst_ref=o_ref, send_sem=send_sem,
            recv_sem=recv_sem, device_id=(x, y, 1 - c), device_id_type=MESH)
        cp.start()
        cp.wait()

    return pl.pallas_call(
        body, in_specs=[ANY], out_specs=ANY, out_shape=SDS((N_CHIPS, HALF_ROWS, 1024), F32),
        scratch_shapes=[pltpu.SemaphoreType.DMA, pltpu.SemaphoreType.DMA], name="pair_exchange")(g_all)


def pair_sum(g_all, got, c):
    tm = 208
    per_half = HALF_ROWS // tm

    def body(c_ref, a_ref, b_ref, o_ref, o16_ref):
        s = a_ref[...] + b_ref[...]
        o_ref[...] = s
        o16_ref[...] = s.astype(BF16)

    out = pl.BlockSpec((1, tm, 1024), lambda j, i, c_ref: (j, i, 0))
    return pl.pallas_call(
        body,
        grid_spec=pltpu.PrefetchScalarGridSpec(
            num_scalar_prefetch=1, grid=(N_CHIPS, per_half),
            in_specs=[pl.BlockSpec((1, tm, 1024), lambda j, i, c_ref: (j, c_ref[0] * per_half + i, 0)), out],
            out_specs=[out, out]),
        out_shape=[SDS((N_CHIPS, HALF_ROWS, 1024), F32), SDS((N_CHIPS, HALF_ROWS, 1024), BF16)],
        compiler_params=_cp("parallel", "parallel"), name="pair_sum")(c.reshape(1).astype(jnp.int32), g_all, got)


def all_gather_small(src):
    def body(s_ref, o_ref, send_sems, recv_sems, local_sem):
        x, y, c = _place()
        flips = [(fx, fy, fc) for fx in (0, 1) for fy in (0, 1) for fc in (0, 1)][1:]
        idx = lambda px, py, pc: 4 * px + 2 * py + pc
        mine = pltpu.make_async_copy(s_ref, o_ref.at[idx(x, y, c)], local_sem)
        mine.start()

        def peer(k):
            fx, fy, fc = flips[k]
            return (1 - x if fx else x, 1 - y if fy else y, 1 - c if fc else c)

        def copy(k, block):
            return pltpu.make_async_remote_copy(
                src_ref=s_ref, dst_ref=o_ref.at[block], send_sem=send_sems.at[k], recv_sem=recv_sems.at[k],
                device_id=peer(k), device_id_type=MESH)

        sends = [copy(k, idx(x, y, c)) for k in range(7)]
        for cp in sends:
            cp.start()
        for k in range(7):
            copy(k, idx(*peer(k))).wait_recv()
        for cp in sends:
            cp.wait_send()
        mine.wait()

    return pl.pallas_call(
        body, in_specs=[ANY], out_specs=ANY, out_shape=SDS((8,) + src.shape, F32),
        scratch_shapes=[pltpu.SemaphoreType.DMA((7,)), pltpu.SemaphoreType.DMA((7,)), pltpu.SemaphoreType.DMA],
        name="all_gather_small")(src)


def sum_blocks(t, name):
    n, r, _ = t.shape
    tm = 208 if r % 208 == 0 else r

    def body(t_ref, o_ref):
        acc = t_ref[0]
        for i in range(1, n):
            acc = acc + t_ref[i]
        o_ref[...] = acc

    return pl.pallas_call(
        body, grid=(r // tm,), in_specs=[pl.BlockSpec((n, tm, 1024), lambda i: (0, i, 0))],
        out_specs=pl.BlockSpec((tm, 1024), lambda i: (i, 0)), out_shape=SDS((r, 1024), F32),
        compiler_params=_cp("parallel"), name=name)(t)


ADAM_BLOCK_BYTES = 1024 * 1024


def adamw(w, g, m, v, name):
    rows, cols = w.shape
    tm = rows
    while tm * cols * 4 > ADAM_BLOCK_BYTES and tm % 16 == 0:
        tm //= 2

    def body(w_ref, g_ref, m_ref, v_ref, d_ref, nm_ref, nv_ref):
        g_ = g_ref[...]
        m_ = ADAM_B1 * m_ref[...] + (1.0 - ADAM_B1) * g_
        v_ = ADAM_B2 * v_ref[...] + (1.0 - ADAM_B2) * (g_ * g_)
        m_hat = m_ / (1.0 - ADAM_B1 ** ADAM_STEP)
        v_hat = v_ / (1.0 - ADAM_B2 ** ADAM_STEP)
        d_ref[...] = -ADAM_LR * (m_hat / (jnp.sqrt(v_hat) + ADAM_EPS) + ADAM_WD * w_ref[...])
        nm_ref[...] = m_
        nv_ref[...] = v_

    spec = pl.BlockSpec((tm, cols), lambda i: (i, 0))
    return pl.pallas_call(
        body, grid=(rows // tm,), in_specs=[spec] * 4, out_specs=[spec] * 3,
        out_shape=[SDS((rows, cols), F32)] * 3, compiler_params=_cp("parallel"), name=name)(w, g, m, v)


LAYERED = ("attn_w_in", "attn_w_out", "dn_w_in", "dn_conv_w", "dn_w_out")


def _two_d(name, a):
    return a[0] if name in LAYERED else a


def kernel(x, meta_tokens, attn_norm_w, attn_w_in, attn_q_norm_w, attn_k_norm_w, attn_sinks, attn_w_out, dn_norm_w, dn_w_in, dn_conv_w, dn_a_log, dn_dt_bias, dn_o_norm_w, dn_w_out, loss_target, m_meta_tokens, m_attn_norm_w, m_attn_w_in, m_attn_q_norm_w, m_attn_k_norm_w, m_attn_sinks, m_attn_w_out, m_dn_norm_w, m_dn_w_in, m_dn_conv_w, m_dn_a_log, m_dn_dt_bias, m_dn_o_norm_w, m_dn_w_out, v_meta_tokens, v_attn_norm_w, v_attn_w_in, v_attn_q_norm_w, v_attn_k_norm_w, v_attn_sinks, v_attn_w_out, v_dn_norm_w, v_dn_w_in, v_dn_conv_w, v_dn_a_log, v_dn_dt_bias, v_dn_o_norm_w, v_dn_w_out):
    given = dict(zip(WEIGHTS, (meta_tokens, attn_norm_w, attn_w_in, attn_q_norm_w, attn_k_norm_w, attn_sinks,
                               attn_w_out, dn_norm_w, dn_w_in, dn_conv_w, dn_a_log, dn_dt_bias, dn_o_norm_w, dn_w_out)))
    mom1 = dict(zip(WEIGHTS, (m_meta_tokens, m_attn_norm_w, m_attn_w_in, m_attn_q_norm_w, m_attn_k_norm_w,
                              m_attn_sinks, m_attn_w_out, m_dn_norm_w, m_dn_w_in, m_dn_conv_w, m_dn_a_log,
                              m_dn_dt_bias, m_dn_o_norm_w, m_dn_w_out)))
    mom2 = dict(zip(WEIGHTS, (v_meta_tokens, v_attn_norm_w, v_attn_w_in, v_attn_q_norm_w, v_attn_k_norm_w,
                              v_attn_sinks, v_attn_w_out, v_dn_norm_w, v_dn_w_in, v_dn_conv_w, v_dn_a_log,
                              v_dn_dt_bias, v_dn_o_norm_w, v_dn_w_out)))
    two_d = lambda d: {n: _two_d(n, a) for n, a in d.items()}
    given, mom1, mom2 = two_d(given), two_d(mom1), two_d(mom2)
    c = lax.axis_index("c")

    me = 2 * lax.axis_index("x") + lax.axis_index("y")
    own_half = lax.dynamic_slice_in_dim(pack_gather(given), c * HALF_ROWS, HALF_ROWS, axis=0)
    mine = lax.dynamic_update_slice_in_dim(chips_exchange(own_half, True), own_half[None], me, 0)
    gathered = sibling_join(mine, "gather_swap")
    per_chip = [unpack_gather(gathered[j]) for j in range(N_CHIPS)]
    full = {n: jnp.concatenate([pc[n] for pc in per_chip], axis=SHARDED[n][1]) for n in SHARDED}
    full.update({n: given[n] for n in REPLICATED})

    loss, dx, grads = local_step(x[0], loss_target[0], full)

    split = lambda n: jnp.split(grads[n], N_CHIPS, axis=SHARDED[n][1])
    g_all = jnp.stack([pack_shard({n: split(n)[j] for n in SHARDED}) for j in range(N_CHIPS)])
    pair, pair16 = pair_sum(g_all, sibling_give(g_all), c)
    half = chip_sum(chips_exchange(pair16, False), pair, me)
    g_shard = sibling_join(half, "half_exchange")

    g_small = sum_blocks(all_gather_small(pack_small(grads)), "small_sum")

    g_local = unpack_shard(g_shard)
    g_local.update(unpack_small(g_small))
    steps = {n: adamw(given[n], g_local[n], mom1[n], mom2[n], "adamw_" + n) for n in WEIGHTS}
    shaped = lambda n, a: a[None] if n in LAYERED else a
    outs = [[shaped(n, g_local[n]) for n in WEIGHTS]]
    outs += [[shaped(n, steps[n][k]) for n in WEIGHTS] for k in range(3)]

    loss = lax.psum(loss, ("x", "y", "c"))
    return (loss, dx[None], *outs[0], *outs[1], *outs[2], *outs[3])
```

```python
import functools

import jax
import jax.numpy as jnp
from jax import lax
from jax.experimental import pallas as pl
from jax.experimental.pallas import tpu as pltpu

F32 = jnp.float32
BF16 = jnp.bfloat16
SDS = jax.ShapeDtypeStruct
MESH = pl.DeviceIdType.MESH

D_MODEL = 1024
N_META = 16
EPS = 1e-6
BLK = 128
CH = 64
PAD = BLK - N_META
HEADS = 16
HD = 64
KVW = 256
DN_H = 16
DN_KH = 8
DK = 128
SLOPES = [2.0 ** (-8.0 * (h + 1) / HEADS) for h in range(HEADS)]
NEG = -1e30
NT = (((1,), (1,)), ((), ()))
TN = (((0,), (0,)), ((), ()))
HI = lax.Precision.HIGHEST

ADAM_LR, ADAM_B1, ADAM_B2, ADAM_EPS, ADAM_WD, ADAM_STEP = 0.001, 0.9, 0.999, 1e-08, 0.01, 10

VMEM_LIMIT = 56 * 1024 * 1024
MXU_DEPTH = 256
WGRAD_BLOCK_ELEMS = 2 * 1024 * 1024


def _cp(*sem):
    return pltpu.CompilerParams(dimension_semantics=sem, vmem_limit_bytes=VMEM_LIMIT)


def _row_tile(rows):
    for t in (384, 256, 128):
        if rows % t == 0:
            return t
    raise ValueError(rows)


def _dot(a, b, dims=None, precision=None):
    if dims is None:
        return jnp.dot(a, b, preferred_element_type=F32, precision=precision)
    return lax.dot_general(a, b, dims, preferred_element_type=F32, precision=precision)


def _silu(x):
    return x * jax.nn.sigmoid(x)


def _dsilu(x):
    s = jax.nn.sigmoid(x)
    return s * (1.0 + x * (1.0 - s))


def _rms(x):
    return lax.rsqrt(jnp.mean(x * x, axis=-1, keepdims=True) + EPS)


def norm_matmul(h, nw, w, tn, name):
    rows, k = h.shape
    n = w.shape[1]
    tm = _row_tile(rows)

    def norm_body(h_ref, nw_ref, xn_ref):
        x = h_ref[...]
        xn_ref[...] = (x * _rms(x) * nw_ref[...]).astype(BF16)

    xn = pl.pallas_call(
        norm_body, grid=(rows // tm,),
        in_specs=[pl.BlockSpec((tm, k), lambda i: (i, 0)), pl.BlockSpec((1, k), lambda i: (0, 0))],
        out_specs=pl.BlockSpec((tm, k), lambda i: (i, 0)), out_shape=SDS((rows, k), BF16),
        compiler_params=_cp("parallel"), name=name + "_norm")(h, nw)

    def body(a_ref, w_ref, o_ref):
        o_ref[...] = _dot(a_ref[...], w_ref[...])

    out = pl.pallas_call(
        body, grid=(n // tn, rows // tm),
        in_specs=[pl.BlockSpec((tm, k), lambda j, i: (i, 0)), pl.BlockSpec((k, tn), lambda j, i: (0, j))],
        out_specs=pl.BlockSpec((tm, tn), lambda j, i: (i, j)), out_shape=SDS((rows, n), F32),
        compiler_params=_cp("parallel", "parallel"), name=name)(xn, w)
    return out, xn


def matmul_residual(a, w, res, name):
    rows, k = a.shape
    n = w.shape[1]
    tm = _row_tile(rows)

    def body(a_ref, w_ref, r_ref, o_ref):
        o_ref[...] = r_ref[...] + _dot(a_ref[...], w_ref[...])

    return pl.pallas_call(
        body, grid=(rows // tm,),
        in_specs=[pl.BlockSpec((tm, k), lambda i: (i, 0)), pl.BlockSpec((k, n), lambda i: (0, 0)),
                  pl.BlockSpec((tm, n), lambda i: (i, 0))],
        out_specs=pl.BlockSpec((tm, n), lambda i: (i, 0)),
        out_shape=SDS((rows, n), F32), compiler_params=_cp("parallel"), name=name)(a, w, res)


def wgrad(a, b, name):
    rows, k = a.shape
    n = b.shape[1]
    tm = _row_tile(rows)
    tn = min(n, WGRAD_BLOCK_ELEMS // k)

    def body(a_ref, b_ref, o_ref):
        @pl.when(pl.program_id(1) == 0)
        def _():
            o_ref[...] = jnp.zeros_like(o_ref)

        o_ref[...] += _dot(a_ref[...], b_ref[...].astype(BF16), TN)

    return pl.pallas_call(
        body, grid=(n // tn, rows // tm),
        in_specs=[pl.BlockSpec((tm, k), lambda j, i: (i, 0)), pl.BlockSpec((tm, tn), lambda j, i: (i, j))],
        out_specs=pl.BlockSpec((k, tn), lambda j, i: (0, j)),
        out_shape=SDS((k, n), F32), compiler_params=_cp("parallel", "arbitrary"), name=name)(a, b)


def in_proj_bwd(dus, ws, h, nw, dh_next, name):
    rows, k = h.shape
    tm = _row_tile(rows)
    nd = len(dus)
    nt = rows // tm

    def body(*refs):
        du_refs, w_refs = refs[:nd], refs[nd:2 * nd]
        h_ref, nw_ref, dhn_ref, dh_ref, dnw_ref = refs[2 * nd:]
        dxn = _dot(du_refs[0][...].astype(BF16), w_refs[0][...], NT)
        for du_ref, w_ref in zip(du_refs[1:], w_refs[1:]):
            dxn += _dot(du_ref[...].astype(BF16), w_ref[...], NT)
        x = h_ref[...]
        r = _rms(x)
        y = x * r
        gy = dxn * nw_ref[...]
        dh_ref[...] = dhn_ref[...] + r * (gy - y * jnp.mean(y * gy, axis=-1, keepdims=True))
        dnw_ref[0] = jnp.sum(dxn * y, axis=0, keepdims=True)

    in_specs = [pl.BlockSpec((tm, du.shape[1]), lambda i: (i, 0)) for du in dus]
    in_specs += [pl.BlockSpec(w.shape, lambda i: (0, 0)) for w in ws]
    in_specs += [pl.BlockSpec((tm, k), lambda i: (i, 0)), pl.BlockSpec((1, k), lambda i: (0, 0)),
                 pl.BlockSpec((tm, k), lambda i: (i, 0))]
    return pl.pallas_call(
        body, grid=(nt,), in_specs=in_specs,
        out_specs=[pl.BlockSpec((tm, k), lambda i: (i, 0)), pl.BlockSpec((1, 1, k), lambda i: (i, 0, 0))],
        out_shape=[SDS((rows, k), F32), SDS((nt, 1, k), F32)],
        compiler_params=_cp("parallel"), name=name)(*dus, *ws, h, nw, dh_next)


def matmul_nt(a, w, name):
    rows, k = a.shape
    n = w.shape[0]
    tm = _row_tile(rows)

    def body(a_ref, w_ref, o_ref):
        o_ref[...] = _dot(a_ref[...].astype(BF16), w_ref[...], NT)

    return pl.pallas_call(
        body, grid=(rows // tm,),
        in_specs=[pl.BlockSpec((tm, k), lambda i: (i, 0)), pl.BlockSpec((n, k), lambda i: (0, 0))],
        out_specs=pl.BlockSpec((tm, n), lambda i: (i, 0)),
        out_shape=SDS((rows, n), F32), compiler_params=_cp("parallel"), name=name)(a, w)


SUB = 64
GRP = 8
TR = GRP * SUB
NBAND = 192
TK = 256


def _tile_bias(n, sb):
    r = lax.broadcasted_iota(jnp.int32, (TR, TK), 0)
    c = lax.broadcasted_iota(jnp.int32, (TR, TK), 1)
    qi = r & (SUB - 1)
    d = BLK + qi - c
    dm = n * BLK + SUB * sb - PAD + NBAND + qi - c
    band = c < NBAND
    valid = (band & (d >= 0) & (d < BLK) & (c >= 2 * BLK - BLK * n - SUB * sb)) | (
        (c >= NBAND) & (c < NBAND + N_META) & (dm >= 0))
    return valid, jnp.where(band, d, jnp.minimum(dm, BLK)).astype(F32)


def _group_col(vals):
    g = lax.broadcasted_iota(jnp.int32, (TR, 1), 0) >> 6
    col = jnp.zeros((TR, 1), F32)
    for gi, v in enumerate(vals):
        col = jnp.where(g == gi, v, col)
    return col


def _stack_heads(ref, sb, kvh):
    return jnp.concatenate(
        [ref[SUB * sb:SUB * sb + SUB, HD * (GRP * kvh + g):HD * (GRP * kvh + g) + HD] for g in range(GRP)], axis=0)


def _unstack_heads(parts):
    return jnp.concatenate([parts[kvh][SUB * g:SUB * g + SUB] for kvh in range(2) for g in range(GRP)], axis=1)


def _tile_keys(band, meta, sb):
    return jnp.concatenate([band[SUB * sb:SUB * sb + NBAND], meta,
                            jnp.zeros((TK - NBAND - N_META, HD), band.dtype)], axis=0)


def _row_sums(x):
    ones = jnp.ones((x.shape[1], 128), BF16)
    hi = x.astype(BF16)
    lo = (x - hi.astype(F32)).astype(BF16)
    return _dot(hi, ones) + _dot(lo, ones)


def _rms_stack(q):
    return lax.rsqrt(_row_sums(q * q)[:, :HD] * (1.0 / HD) + EPS)


def _fill_bias(bias_scr, n):
    @pl.when(n <= 2)
    def _():
        for sb in range(2):
            valid, dist = _tile_bias(n, sb)
            for kvh in range(2):
                slope_col = _group_col([SLOPES[GRP * kvh + g] for g in range(GRP)])
                bias_scr[2 * sb + kvh] = jnp.where(valid, -slope_col * dist, NEG)


def _tile_vals(band, meta, sb):
    return jnp.concatenate([_tile_keys(band, meta, sb), jnp.ones((TK, 3 * HD), BF16)], axis=1)


def _tile_softmax(qn16, k16, vx16, bias, sink_col):
    s = _dot(qn16, k16, NT) * (HD ** -0.5) + bias
    mx = jnp.maximum(jnp.max(s.astype(BF16), axis=-1, keepdims=True).astype(F32), sink_col)
    e = jnp.exp(s - mx)
    es = jnp.exp(sink_col - mx)
    ox = _dot(e.astype(BF16), vx16)
    return e, 1.0 / (ox[:, 2 * HD:] + es), es, ox[:, :HD]


def _kv_heads(kvb, kvm, kw_):
    out = []
    for kvh in range(2):
        kb, km = kvb[:, HD * kvh:HD * kvh + HD], kvm[:, HD * kvh:HD * kvh + HD]
        out.append(((kb * _rms(kb) * kw_).astype(BF16), (km * _rms(km) * kw_).astype(BF16),
                    kvb[:, BLK + HD * kvh:BLK + HD * kvh + HD].astype(BF16),
                    kvm[:, BLK + HD * kvh:BLK + HD * kvh + HD].astype(BF16)))
    return out


def _kv_specs(nblk, clamp):
    cur = (lambda n: (jnp.minimum(n, nblk - 1), 8)) if clamp else (lambda n: (n, 8))
    return [pl.BlockSpec((BLK, KVW), cur),
            pl.BlockSpec((BLK, KVW), lambda n: (jnp.maximum(n - 1, 0), 8)),
            pl.BlockSpec((N_META, KVW), lambda n: (PAD // N_META, 8))]


def _sink_cols(sinks):
    return jnp.repeat(sinks.reshape(2, GRP), SUB, axis=1).reshape(2, TR, 1)


SINK_SPEC = pl.BlockSpec((2, TR, 1), lambda n: (0, 0, 0))


def attn_fwd(u, qw, kw, sinks):
    rows = u.shape[0]
    nblk = rows // BLK

    def body(q_ref, g_ref, kvc_ref, kvp_ref, kvm_ref, qw_ref, kw_ref, sc_ref, og_ref, bias_scr):
        _fill_bias(bias_scr, pl.program_id(0))
        qw_ = qw_ref[...]
        kv = _kv_heads(jnp.concatenate([kvp_ref[...], kvc_ref[...]], axis=0), kvm_ref[...], kw_ref[...])
        for sb in range(2):
            parts = []
            for kvh in range(2):
                knb, knm, vb, vm = kv[kvh]
                q = _stack_heads(q_ref, sb, kvh)
                qn16 = (q * _rms_stack(q) * qw_).astype(BF16)
                _, inv, _, o = _tile_softmax(qn16, _tile_keys(knb, knm, sb), _tile_vals(vb, vm, sb),
                                             bias_scr[2 * sb + kvh], sc_ref[kvh])
                parts.append(o * inv[:, :HD])
            rows = slice(SUB * sb, SUB * sb + SUB)
            og_ref[rows, :] = (_unstack_heads(parts) * _silu(g_ref[rows, :])).astype(BF16)

    small = lambda w: pl.BlockSpec((1, w), lambda n: (0, 0))
    return pl.pallas_call(
        body, grid=(nblk,),
        in_specs=[pl.BlockSpec((BLK, 1024), lambda n: (n, 0)), pl.BlockSpec((BLK, 1024), lambda n: (n, 1))]
        + _kv_specs(nblk, False) + [small(HD), small(HD), SINK_SPEC],
        out_specs=pl.BlockSpec((BLK, 1024), lambda n: (n, 0)),
        out_shape=SDS((rows, 1024), BF16), scratch_shapes=[pltpu.VMEM((4, TR, TK), F32)],
        compiler_params=_cp("arbitrary"), name="attn_fwd")(u, u, u, u, u, qw, kw, _sink_cols(sinks))


def attn_bwd(u, qw, kw, sinks, dog):
    rows = u.shape[0]
    nblk = rows // BLK

    def knorm_bwd(k, dkn, kw_):
        r = _rms(k)
        y = k * r
        gy = dkn * kw_
        return r * (gy - y * jnp.mean(y * gy, axis=-1, keepdims=True)), jnp.sum(dkn * y, axis=0, keepdims=True)

    def body(q_ref, g_ref, dog_ref, kvc_ref, kvp_ref, kvm_ref, qw_ref, kw_ref, sc_ref,
             dq_ref, dg_ref, dkv_ref, dkvm_ref, dqw_ref, dkw_ref, dsk_ref, carry, prevp, curp, metap, bias_scr):
        n = pl.program_id(0)
        qw_, kw_ = qw_ref[...], kw_ref[...]
        _fill_bias(bias_scr, n)

        @pl.when(n == 0)
        def _():
            carry[...] = jnp.zeros_like(carry)
            metap[...] = jnp.zeros_like(metap)
            dqw_ref[...] = jnp.zeros_like(dqw_ref)
            dkw_ref[...] = jnp.zeros_like(dkw_ref)
            dsk_ref[...] = jnp.zeros_like(dsk_ref)

        @pl.when(n == nblk)
        def _():
            prevp[...] = jnp.zeros_like(prevp)
            curp[...] = jnp.zeros_like(curp)

        @pl.when(n < nblk)
        def _():
            kv = _kv_heads(jnp.concatenate([kvp_ref[...], kvc_ref[...]], axis=0), kvm_ref[...], kw_)
            lane = lax.broadcasted_iota(jnp.int32, (1, HEADS), 1)
            dqw = jnp.zeros((1, HD), F32)
            dsk = jnp.zeros((1, HEADS), F32)
            band_parts = [jnp.zeros((2 * BLK, HD), F32) for _ in range(4)]
            meta_parts = [jnp.zeros((N_META, HD), F32) for _ in range(4)]

            def widen(x, sb):
                z = jnp.zeros((2 * BLK - NBAND, HD), F32)
                return jnp.concatenate([x, z] if sb == 0 else [z, x], axis=0)

            for sb in range(2):
                rows = slice(SUB * sb, SUB * sb + SUB)
                dq_parts, dg_parts = [], []
                for kvh in range(2):
                    knb, knm, vb, vm = kv[kvh]
                    k16, v16 = _tile_keys(knb, knm, sb), _tile_keys(vb, vm, sb)
                    q = _stack_heads(q_ref, sb, kvh)
                    r = _rms_stack(q)
                    y = q * r
                    qn16 = (y * qw_).astype(BF16)
                    e, inv, es, o = _tile_softmax(qn16, k16, _tile_vals(vb, vm, sb), bias_scr[2 * sb + kvh],
                                                  sc_ref[kvh])
                    p = e * jnp.concatenate([inv, inv], axis=1)
                    p16 = p.astype(BF16)
                    o = o * inv[:, :HD]
                    gate = _stack_heads(g_ref, sb, kvh)
                    dog_ = _stack_heads(dog_ref, sb, kvh)
                    dg_parts.append(dog_ * o * _dsilu(gate))
                    do_ = dog_ * _silu(gate)
                    do16 = do_.astype(BF16)
                    dp = _dot(do16, v16, NT)
                    delta = _row_sums(do_ * o)
                    ds16 = (p * (dp - jnp.concatenate([delta, delta], axis=1))).astype(BF16)
                    dsink = -(es * inv) * delta
                    for g in range(GRP):
                        dsk += jnp.where(lane == GRP * kvh + g,
                                         jnp.sum(dsink[SUB * g:SUB * g + SUB, :HEADS], axis=0, keepdims=True), 0.0)
                    dqn = _dot(ds16, k16) * (HD ** -0.5)
                    dk = (_dot((y * qw_).T.astype(BF16), ds16) * (HD ** -0.5)).T
                    dv = _dot(do_.T.astype(BF16), p16).T
                    band_parts[kvh] += widen(dk[:NBAND], sb)
                    band_parts[2 + kvh] += widen(dv[:NBAND], sb)
                    meta_parts[kvh] += dk[NBAND:NBAND + N_META]
                    meta_parts[2 + kvh] += dv[NBAND:NBAND + N_META]
                    gy = dqn * qw_
                    dq_parts.append(r * (gy - y * (_row_sums(y * gy)[:, :HD] * (1.0 / HD))))
                    dqw += jnp.sum(dqn * y, axis=0, keepdims=True)
                dq_ref[rows, :] = _unstack_heads(dq_parts).astype(BF16)
                dg_ref[rows, :] = _unstack_heads(dg_parts).astype(BF16)
            band = jnp.concatenate(band_parts, axis=1)
            prevp[...] = band[:BLK]
            curp[...] = band[BLK:]
            metap[...] += jnp.concatenate(meta_parts, axis=1)
            dqw_ref[...] += dqw
            dsk_ref[...] += dsk

        tot = carry[...] + prevp[...]
        kprev = kvp_ref[...]
        dk0, w0 = knorm_bwd(kprev[:, 0:HD], tot[:, 0:HD], kw_)
        dk1, w1 = knorm_bwd(kprev[:, HD:2 * HD], tot[:, HD:2 * HD], kw_)
        dkv_ref[...] = jnp.concatenate([dk0, dk1, tot[:, 2 * HD:]], axis=1)
        dkw_ref[...] += w0 + w1
        carry[...] = curp[...]

        @pl.when(n == nblk)
        def _():
            mt = metap[...]
            km = kvm_ref[...]
            m0, v0 = knorm_bwd(km[:, 0:HD], mt[:, 0:HD], kw_)
            m1, v1 = knorm_bwd(km[:, HD:2 * HD], mt[:, HD:2 * HD], kw_)
            dkvm_ref[...] = jnp.concatenate([m0, m1, mt[:, 2 * HD:]], axis=1)
            dkw_ref[...] += v0 + v1

    small = lambda w: pl.BlockSpec((1, w), lambda n: (0, 0))
    cl = lambda n: jnp.minimum(n, nblk - 1)
    return pl.pallas_call(
        body, grid=(nblk + 1,),
        in_specs=[pl.BlockSpec((BLK, 1024), lambda n: (cl(n), 0)), pl.BlockSpec((BLK, 1024), lambda n: (cl(n), 1)),
                  pl.BlockSpec((BLK, 1024), lambda n: (cl(n), 0))]
        + _kv_specs(nblk, True) + [small(HD), small(HD), SINK_SPEC],
        out_specs=[pl.BlockSpec((BLK, 1024), lambda n: (cl(n), 0)), pl.BlockSpec((BLK, 1024), lambda n: (cl(n), 0)),
                   pl.BlockSpec((BLK, KVW), lambda n: (jnp.maximum(n - 1, 0), 0)),
                   pl.BlockSpec((N_META, KVW), lambda n: (0, 0)), small(HD), small(HD), small(HEADS)],
        out_shape=[SDS((rows, 1024), BF16), SDS((rows, 1024), BF16), SDS((rows, KVW), F32), SDS((N_META, KVW), F32),
                   SDS((1, HD), F32), SDS((1, HD), F32), SDS((1, HEADS), F32)],
        scratch_shapes=[pltpu.VMEM((BLK, KVW), F32), pltpu.VMEM((BLK, KVW), F32), pltpu.VMEM((BLK, KVW), F32),
                        pltpu.VMEM((N_META, KVW), F32), pltpu.VMEM((4, TR, TK), F32)],
        compiler_params=_cp("arbitrary"), name="attn_bwd")(u, u, dog, u, u, u, qw, kw, _sink_cols(sinks))


GROUP_UNROLL = 8
HB = 16


def _bdot(a, b, kind, split=False, fused=False):
    dims = {"nn": ((2,), (1,)), "nt": ((2,), (2,)), "tn": ((1,), (1,))}[kind]
    dg = lambda p, q: lax.dot_general(p, q, (dims, ((0,), (0,))), preferred_element_type=F32)
    if not split:
        return dg(a, b)
    if fused:
        assert kind == "nn" and 3 * a.shape[2] <= MXU_DEPTH
        return _dot3(_split(a), _split(b))
    ah, bh = a.astype(BF16), b.astype(BF16)
    al, bl = (a - ah.astype(F32)).astype(BF16), (b - bh.astype(F32)).astype(BF16)
    return (dg(ah, bl) + dg(al, bh)) + dg(ah, bh)


def _head_cols(hv, beta, gc, gct, lane):
    sel = lane == hv
    return _pick(beta, sel), _pick(gc, sel), gct[pl.ds(hv, 1), :]


def _conv_group(xc_ref, xp_ref, cw_ref, off, first):
    xp = jnp.where(first, 0.0, xp_ref[:, pl.ds(off, DK)])
    xx = jnp.concatenate([xp, xc_ref[:, pl.ds(off, DK)]], axis=0)
    y = cw_ref[0:1, pl.ds(off, DK)] * xx[5:5 + CH]
    for j in range(1, 4):
        y += cw_ref[j:j + 1, pl.ds(off, DK)] * xx[5 + j:5 + j + CH]
    return xx, y


def _gates(ba, al, dtb, c):
    row = c * CH + lax.broadcasted_iota(jnp.int32, (CH, DN_H), 0)
    real = row >= PAD
    xa = ba[:, DN_H:2 * DN_H] + dtb
    beta = jnp.where(real, jax.nn.sigmoid(ba[:, 0:DN_H]), 0.0)
    g = jnp.where(real, -jnp.exp(al) * jax.nn.softplus(xa), 0.0)
    return real, xa, beta, g


def _pick(x, sel):
    return jnp.sum(jnp.where(sel, x, 0.0), axis=1, keepdims=True)


def _chunk_specs(width_blocks):
    return [pl.BlockSpec((CH, 4096), lambda c: (c, 0)),
            pl.BlockSpec((8, 4096), lambda c: (jnp.maximum(8 * c - 1, 0), 0)),
            pl.BlockSpec((CH, DK), lambda c: (c, 48))]


def _tri_inv(m, ii, jj):
    eye = (ii == jj).astype(BF16)
    mh, ml = _split(m)
    blk8 = (ii >> 3) == (jj >> 3)
    mb = (jnp.where(blk8, mh, 0), jnp.where(blk8, ml, 0))
    m2 = _split(_dot3(mb, mb))
    m4 = _split(_dot3(m2, m2))
    x = _dot3(_split(_dot3((eye - mb[0], -mb[1]), (eye + m2[0], m2[1]))), (eye + m4[0], m4[1]))
    for sh in (3, 4, 5):
        off = ((ii >> (sh + 1)) == (jj >> (sh + 1))) & ((ii >> sh) != (jj >> sh))
        xs = _split(x)
        x = x - _dot3(_split(_dot3(xs, (jnp.where(off, mh, 0), jnp.where(off, ml, 0)))), xs)
    return x


def _split(x):
    hi = x.astype(BF16)
    return hi, (x - hi.astype(F32)).astype(BF16)


def _dot3(a, b):
    lhs = jnp.concatenate([a[0], a[1], a[0]], axis=2)
    rhs = jnp.concatenate([b[0], b[0], b[1]], axis=1)
    return lax.dot_general(lhs, rhs, (((2,), (1,)), ((0,), (0,))), preferred_element_type=F32)


def dn_prep(udn, conv_w, a_log, dt_bias):
    rows = udn.shape[0]
    nch = rows // CH

    def body(xc_ref, xp_ref, ba_ref, cw_ref, al_ref, dtb_ref,
             qn_ref, kn_ref, sv_ref, gc_ref, beta_ref, u_ref, w_ref, qe_ref, ks_ref, p_ref, at_ref, pt_ref,
             qet_ref, wt_ref, kst_ref, y_ref, gct):
        c = pl.program_id(0)
        first = c == 0
        _, _, beta, g = _gates(ba_ref[...], al_ref[...], dtb_ref[...], c)
        ii = lax.broadcasted_iota(jnp.int32, (CH, CH), 0)
        jj = lax.broadcasted_iota(jnp.int32, (CH, CH), 1)
        gc = _dot((ii >= jj).astype(F32), g, precision=HI)
        gc_ref[...] = gc
        beta_ref[...] = beta
        gct[...] = gc.T

        def qk_body(kh, carry):
            off = pl.multiple_of(kh * DK, DK)
            _, yq = _conv_group(xc_ref, xp_ref, cw_ref, off, first)
            y_ref[:, pl.ds(off, DK)] = yq
            sq = _silu(yq)
            qn_ref[:, pl.ds(off, DK)] = sq * lax.rsqrt(jnp.sum(sq * sq, axis=-1, keepdims=True) + EPS) * (DK ** -0.5)
            _, yk = _conv_group(xc_ref, xp_ref, cw_ref, pl.multiple_of(1024 + kh * DK, DK), first)
            y_ref[:, pl.ds(pl.multiple_of(1024 + kh * DK, DK), DK)] = yk
            sk = _silu(yk)
            kn_ref[:, pl.ds(off, DK)] = sk * lax.rsqrt(jnp.sum(sk * sk, axis=-1, keepdims=True) + EPS)
            return carry

        lax.fori_loop(0, DN_KH, qk_body, 0, unroll=GROUP_UNROLL)
        lane = lax.broadcasted_iota(jnp.int32, (CH, DN_H), 1)
        zpad = jnp.zeros((CH, DK - CH), F32)

        def v_group(grp, carry):
            offs, ks_, qs_, vs_, cols = [], [], [], [], []
            for i in range(HB):
                hv = grp * HB + i
                offs.append(pl.multiple_of(hv * DK, DK))
                koff = pl.multiple_of((grp * (HB // 2) + i // 2) * DK, DK)
                _, yv = _conv_group(xc_ref, xp_ref, cw_ref, pl.multiple_of(2048 + hv * DK, DK), first)
                y_ref[:, pl.ds(pl.multiple_of(2048 + hv * DK, DK), DK)] = yv
                vs_.append(_silu(yv))
                sv_ref[:, pl.ds(offs[i], DK)] = vs_[i]
                ks_.append(kn_ref[:, pl.ds(koff, DK)])
                qs_.append(qn_ref[:, pl.ds(koff, DK)])
                cols.append(_head_cols(hv, beta, gc, gct, lane))
            k, q, v = jnp.stack(ks_), jnp.stack(qs_), jnp.stack(vs_)
            beta_c, gc_c, gc_r = (jnp.stack([c_[j] for c_ in cols]) for j in range(3))
            dec = jnp.exp(jnp.where(ii >= jj, gc_c - gc_r, NEG))
            eg = jnp.exp(gc_c)
            kb = k * beta_c
            k16 = k.astype(BF16)
            m = jnp.where(ii > jj, _bdot(kb.astype(BF16), k16, "nt") * dec, 0.0)
            a = _tri_inv(m, ii, jj)
            uw = _bdot(a, jnp.concatenate([v * beta_c, kb * eg], axis=2), "nn", True, True)
            p = _bdot(q.astype(BF16), k16, "nt") * dec
            qe = q * eg
            ksx = k * jnp.exp(gc_c[:, CH - 1:CH, :] - gc_c)
            tslot = lambda x: jnp.concatenate([x.T, jnp.zeros((DK, DK - CH), F32)], axis=1).astype(BF16)
            for i in range(HB):
                sl = pl.ds(offs[i], DK)
                u_ref[:, sl] = uw[i, :, :DK]
                w_ref[:, sl] = uw[i, :, DK:]
                qe_ref[:, sl] = qe[i].astype(BF16)
                ks_ref[:, sl] = ksx[i].astype(BF16)
                p_ref[:, sl] = jnp.concatenate([p[i], zpad], axis=1).astype(BF16)
                at_ref[:, sl] = jnp.concatenate([a[i].T, zpad], axis=1)
                pt_ref[:, sl] = jnp.concatenate([p[i].T, zpad], axis=1).astype(BF16)
                qet_ref[:, sl] = tslot(qe[i])
                wt_ref[:, sl] = tslot(uw[i, :, DK:])
                kst_ref[:, sl] = tslot(ksx[i])
            return carry

        lax.fori_loop(0, DN_H // HB, v_group, 0)

    full = lambda shape: pl.BlockSpec(shape, lambda c: (0, 0))
    blk = lambda w: pl.BlockSpec((CH, w), lambda c: (c, 0))
    return pl.pallas_call(
        body, grid=(nch,),
        in_specs=_chunk_specs(0) + [full((4, 4096)), full((1, DN_H)), full((1, DN_H))],
        out_specs=[blk(1024), blk(1024), blk(2048), blk(DN_H), blk(DN_H), blk(2048), blk(2048), blk(2048), blk(2048),
                   blk(2048), blk(2048), blk(2048)] + [pl.BlockSpec((DK, 2048), lambda c: (c, 0))] * 3 + [blk(4096)],
        out_shape=[SDS((rows, 1024), F32), SDS((rows, 1024), F32), SDS((rows, 2048), F32), SDS((rows, DN_H), F32),
                   SDS((rows, DN_H), F32), SDS((rows, 2048), F32), SDS((rows, 2048), F32), SDS((rows, 2048), BF16),
                   SDS((rows, 2048), BF16), SDS((rows, 2048), BF16), SDS((rows, 2048), F32),
                   SDS((rows, 2048), BF16)] + [SDS((2 * rows, 2048), BF16)] * 3 + [SDS((rows, 4096), F32)],
        scratch_shapes=[pltpu.VMEM((DN_H, CH), F32)],
        compiler_params=_cp("parallel"), name="dn_prep")(udn, udn, udn, conv_w, a_log, dt_bias)


def dn_scan(u, w, qe, kst, p, gc):
    rows = u.shape[0]
    nch = rows // CH

    def body(u_ref, w_ref, qe_ref, kst_ref, p_ref, gc_ref, o_ref, vn_ref, st_ref, s_scr):
        @pl.when(pl.program_id(0) == 0)
        def _():
            s_scr[...] = jnp.zeros_like(s_scr)

        gl_row = gc_ref[CH - 1:CH, :]
        lane = lax.broadcasted_iota(jnp.int32, (1, DN_H), 1)

        def group(grp, carry):
            base = grp * HB
            sls = [pl.ds(pl.multiple_of((base + i) * DK, DK), DK) for i in range(HB)]
            heads = lambda ref: jnp.stack([ref[:, sl] for sl in sls])
            s = s_scr[pl.ds(base, HB)]
            st_ref[0, pl.ds(base, HB)] = s
            s16 = s.astype(BF16)
            vn = heads(u_ref) - _bdot(heads(w_ref).astype(BF16), s16, "nn")
            vn16 = vn.astype(BF16)
            o = _bdot(heads(qe_ref), s16, "nn") + _bdot(heads(p_ref)[:, :, 0:CH], vn16, "nn")
            egl = jnp.exp(jnp.stack([_pick(gl_row, lane == base + i) for i in range(HB)]))
            s_scr[pl.ds(base, HB)] = s * egl + _bdot(heads(kst_ref)[:, :, 0:CH], vn16, "nn")
            for i in range(HB):
                vn_ref[:, sls[i]] = vn16[i]
                o_ref[:, sls[i]] = o[i]
            return carry

        lax.fori_loop(0, DN_H // HB, group, 0)

    blk = lambda wd: pl.BlockSpec((CH, wd), lambda c: (c, 0))
    return pl.pallas_call(
        body, grid=(nch,),
        in_specs=[blk(2048)] * 3 + [pl.BlockSpec((DK, 2048), lambda c: (c, 0)), blk(2048), blk(DN_H)],
        out_specs=[blk(2048), blk(2048), pl.BlockSpec((1, DN_H, DK, DK), lambda c: (c, 0, 0, 0))],
        out_shape=[SDS((rows, 2048), F32), SDS((rows, 2048), BF16), SDS((nch, DN_H, DK, DK), F32)],
        scratch_shapes=[pltpu.VMEM((DN_H, DK, DK), F32)],
        compiler_params=_cp("arbitrary"), name="dn_scan")(u, w, qe, kst, p, gc)


def dn_out_fwd(o, udn, ow, wout, h1, tgt):
    rows = o.shape[0]
    tm = _row_tile(rows)
    nt = rows // tm

    def body(o_ref, z_ref, ow_ref, w_ref, h_ref, t_ref, dh_ref, on_ref, ls_ref):
        for hv in range(DN_H):
            sl = slice(hv * DK, hv * DK + DK)
            oh = o_ref[:, sl]
            on_ref[:, sl] = (oh * _rms(oh) * ow_ref[...] * _silu(z_ref[:, sl])).astype(BF16)
        h2 = h_ref[...] + _dot(on_ref[...], w_ref[...])
        row = pl.program_id(0) * tm + lax.broadcasted_iota(jnp.int32, (tm, 1), 0)
        err = jnp.where(row >= BLK, h2 - t_ref[...], 0.0)
        dh_ref[...] = err * (1.0 / D_MODEL)
        ls_ref[0] = jnp.sum(err * err, axis=0, keepdims=True)

    return pl.pallas_call(
        body, grid=(nt,),
        in_specs=[pl.BlockSpec((tm, 2048), lambda i: (i, 0)), pl.BlockSpec((tm, 2048), lambda i: (i, 2)),
                  pl.BlockSpec((1, DK), lambda i: (0, 0)), pl.BlockSpec((2048, D_MODEL), lambda i: (0, 0)),
                  pl.BlockSpec((tm, D_MODEL), lambda i: (i, 0)), pl.BlockSpec((tm, D_MODEL), lambda i: (i, 0))],
        out_specs=[pl.BlockSpec((tm, D_MODEL), lambda i: (i, 0)), pl.BlockSpec((tm, 2048), lambda i: (i, 0)),
                   pl.BlockSpec((1, 1, D_MODEL), lambda i: (i, 0, 0))],
        out_shape=[SDS((rows, D_MODEL), F32), SDS((rows, 2048), BF16), SDS((nt, 1, D_MODEL), F32)],
        compiler_params=_cp("parallel"), name="dn_out_fwd")(o, udn, ow, wout, h1, tgt)


def dn_out_bwd(dh2, wout, o, udn, ow):
    rows = o.shape[0]
    tm = _row_tile(rows)
    nt = rows // tm

    def body(dh_ref, w_ref, o_ref, z_ref, ow_ref, do_ref, dz_ref, dow_ref):
        don = _dot(dh_ref[...].astype(BF16), w_ref[...], NT)
        ow_ = ow_ref[...]
        dow = jnp.zeros((1, DK), F32)
        for hv in range(DN_H):
            sl = slice(hv * DK, hv * DK + DK)
            oh = o_ref[:, sl]
            r = _rms(oh)
            y = oh * r
            z = z_ref[:, sl]
            dn = don[:, sl] * _silu(z)
            dz_ref[:, sl] = (don[:, sl] * (y * ow_) * _dsilu(z)).astype(BF16)
            dy = dn * ow_
            do_ref[:, sl] = r * (dy - y * jnp.mean(y * dy, axis=-1, keepdims=True))
            dow += jnp.sum(dn * y, axis=0, keepdims=True)
        dow_ref[0] = dow

    return pl.pallas_call(
        body, grid=(nt,),
        in_specs=[pl.BlockSpec((tm, D_MODEL), lambda i: (i, 0)), pl.BlockSpec((2048, D_MODEL), lambda i: (0, 0)),
                  pl.BlockSpec((tm, 2048), lambda i: (i, 0)), pl.BlockSpec((tm, 2048), lambda i: (i, 2)),
                  pl.BlockSpec((1, DK), lambda i: (0, 0))],
        out_specs=[pl.BlockSpec((tm, 2048), lambda i: (i, 0)), pl.BlockSpec((tm, 2048), lambda i: (i, 0)),
                   pl.BlockSpec((1, 1, DK), lambda i: (i, 0, 0))],
        out_shape=[SDS((rows, 2048), F32), SDS((rows, 2048), BF16), SDS((nt, 1, DK), F32)],
        compiler_params=_cp("parallel"), name="dn_out_bwd")(dh2, wout, o, udn, ow)


def dn_scan_bwd(do, qn, kn, sv, gc, beta, at, pt, u, w, vn, qet, wt, ks, st):
    rows = do.shape[0]
    nch = rows // CH

    def body(do_ref, q_ref, k_ref, v_ref, gc_ref, beta_ref, at_ref, pt_ref, u_ref, w_ref, vn_ref, qet_ref, wt_ref,
             ks_ref, st_ref, dq_ref, dk_ref, dv_ref, dbeta_ref, dg_ref, ds_scr, gct):
        @pl.when(pl.program_id(0) == 0)
        def _():
            ds_scr[...] = jnp.zeros_like(ds_scr)

        gc, beta = gc_ref[...], beta_ref[...]
        gct[...] = gc.T
        ii = lax.broadcasted_iota(jnp.int32, (CH, CH), 0)
        jj = lax.broadcasted_iota(jnp.int32, (CH, CH), 1)
        lane = lax.broadcasted_iota(jnp.int32, (CH, DN_H), 1)
        last = lax.broadcasted_iota(jnp.int32, (CH, 1), 0) == CH - 1

        def group(grp, carry):
            dbeta_acc, dgc_acc = carry
            base = grp * HB
            sls = [pl.ds(pl.multiple_of((base + i) * DK, DK), DK) for i in range(HB)]
            ksls = [pl.ds(pl.multiple_of((grp * (HB // 2) + j) * DK, DK), DK) for j in range(HB // 2)]
            heads = lambda ref: jnp.stack([ref[:, sl] for sl in sls])
            kheads = lambda ref: jnp.stack([ref[:, ksls[i // 2]] for i in range(HB)])
            cols = [_head_cols(base + i, beta, gc, gct, lane) for i in range(HB)]
            beta_c, gc_c, gc_r = (jnp.stack([c_[j] for c_ in cols]) for j in range(3))
            k, q, v = kheads(k_ref), kheads(q_ref), heads(v_ref)
            dec = jnp.exp(jnp.where(ii >= jj, gc_c - gc_r, NEG))
            eg = jnp.exp(gc_c)
            gl = gc_c[:, CH - 1:CH, :]
            e2 = jnp.exp(gl - gc_c)
            egl = jnp.exp(gl)
            k16, q16 = k.astype(BF16), q.astype(BF16)
            do16 = heads(do_ref).astype(BF16)
            s = st_ref[0, pl.ds(base, HB)]
            s16 = s.astype(BF16)
            dso = ds_scr[pl.ds(base, HB)]
            dso16 = dso.astype(BF16)
            wf, uf, vn16 = heads(w_ref), heads(u_ref), heads(vn_ref)
            kb = k * beta_c
            kb16 = kb.astype(BF16)
            pm = _bdot(q16, k16, "nt") * dec
            m = jnp.where(ii > jj, _bdot(kb16, k16, "nt") * dec, 0.0)
            dvn = _bdot(heads(pt_ref)[:, :, 0:CH], do16, "nn") + _bdot(heads(ks_ref), dso16, "nn")
            dvn16 = dvn.astype(BF16)
            ds_scr[pl.ds(base, HB)] = (egl * dso + _bdot(heads(qet_ref)[:, :, 0:CH], do16, "nn")
                                       - _bdot(heads(wt_ref)[:, :, 0:CH], dvn16, "nn"))
            dpm = jnp.where(ii >= jj, _bdot(do16, vn16, "nt"), 0.0)
            dqk16 = (dpm * dec).astype(BF16)
            dqe = _bdot(do16, s16, "nt")
            dq = eg * dqe + _bdot(dqk16, k16, "nn")
            dks = _bdot(vn16, dso16, "nt")
            dw = -_bdot(dvn16, s16, "nt")
            dbvk = _bdot(heads(at_ref)[:, :, 0:CH], jnp.concatenate([dvn, dw], axis=2), "nn", True)
            dbv, dbk = dbvk[:, :, :DK], dbvk[:, :, DK:]
            dm = jnp.where(ii > jj, -_bdot(dbvk, jnp.concatenate([uf, wf], axis=2), "nt", True), 0.0)
            g16 = (dm * dec).astype(BF16)
            dkb = _bdot(g16, k16, "nn")
            dk = (_bdot(dqk16, q16, "tn") + e2 * dks + _bdot(g16, kb16, "tn") + beta_c * (eg * dbk + dkb))
            e = dpm * pm + dm * m
            rsum = lambda x: jnp.sum(x, axis=2, keepdims=True)
            r_bk, r_qe, r_beta, r_ks = rsum(dbk * k), rsum(q * dqe), rsum(dbv * v + dkb * k), rsum(dks * k)
            t = r_ks * e2
            dgl = jnp.sum(t, axis=1, keepdims=True) + egl * rsum(jnp.sum(dso * s, axis=1, keepdims=True))
            deg = r_qe + beta_c * r_bk
            dgc = rsum(e) - t + deg * eg + jnp.where(last, dgl, 0.0)
            dgrow = -jnp.sum(e, axis=1, keepdims=True)
            dv = beta_c * dbv
            dbeta = r_beta + eg * r_bk
            for i in range(HB):
                dv_ref[:, sls[i]] = dv[i]
                sel = lane == base + i
                dbeta_acc = jnp.where(sel, dbeta[i], dbeta_acc)
                dgc_acc = jnp.where(sel, dgc[i], dgc_acc)
                gct[pl.ds(base + i, 1), :] = dgrow[i]
            for j in range(HB // 2):
                dq_ref[:, ksls[j]] = dq[2 * j] + dq[2 * j + 1]
                dk_ref[:, ksls[j]] = dk[2 * j] + dk[2 * j + 1]
            return dbeta_acc, dgc_acc

        zero = jnp.zeros((CH, DN_H), F32)
        dbeta_acc, dgc_acc = lax.fori_loop(0, DN_H // HB, group, (zero, zero))
        dbeta_ref[...] = dbeta_acc
        dg_ref[...] = _dot((ii <= jj).astype(F32), dgc_acc + gct[...].T, precision=HI)

    rev = lambda wd: pl.BlockSpec((CH, wd), lambda i: (nch - 1 - i, 0))
    rev_t = pl.BlockSpec((DK, 2048), lambda i: (nch - 1 - i, 0))
    return pl.pallas_call(
        body, grid=(nch,),
        in_specs=[rev(2048), rev(1024), rev(1024), rev(2048), rev(DN_H), rev(DN_H), rev(2048), rev(2048), rev(2048),
                  rev(2048), rev(2048), rev_t, rev_t, rev(2048),
                  pl.BlockSpec((1, DN_H, DK, DK), lambda i: (nch - 1 - i, 0, 0, 0))],
        out_specs=[rev(1024), rev(1024), rev(2048), rev(DN_H), rev(DN_H)],
        out_shape=[SDS((rows, 1024), F32), SDS((rows, 1024), F32), SDS((rows, 2048), F32), SDS((rows, DN_H), F32),
                   SDS((rows, DN_H), F32)],
        scratch_shapes=[pltpu.VMEM((DN_H, DK, DK), F32), pltpu.VMEM((DN_H, CH), F32)],
        compiler_params=_cp("arbitrary"), name="dn_scan_bwd")(
            do, qn, kn, sv, gc, beta, at, pt, u, w, vn, qet, wt, ks, st)


def dn_prep_bwd(udn, yconv, conv_w, a_log, dt_bias, dqn, dkn, dv, dbeta, dg):
    rows = udn.shape[0]
    nch = rows // CH
    ext = CH + 8

    def body(xc_ref, ba_ref, yc_ref, yn_ref, dqn_n, dkn_n, dv_n, cw_ref, al_ref, dtb_ref, dqn_ref, dkn_ref, dv_ref,
             dbeta_ref, dg_ref, dx_ref, dba_ref, dcw_ref, dal_ref, ddtb_ref):
        c = pl.program_id(0)
        first = c == 0
        own = (lax.broadcasted_iota(jnp.int32, (ext, 1), 0) < CH) | (c < nch - 1)

        @pl.when(first)
        def _():
            dcw_ref[...] = jnp.zeros_like(dcw_ref)
            dal_ref[...] = jnp.zeros_like(dal_ref)
            ddtb_ref[...] = jnp.zeros_like(ddtb_ref)

        real, xa, beta, g = _gates(ba_ref[...], al_ref[...], dtb_ref[...], c)
        dgm = jnp.where(real, dg_ref[...], 0.0)
        da = dgm * (-jnp.exp(al_ref[...])) * jax.nn.sigmoid(xa)
        dal_ref[...] += jnp.sum(dgm * g, axis=0, keepdims=True)
        ddtb_ref[...] += jnp.sum(da, axis=0, keepdims=True)
        dba_ref[...] = jnp.zeros_like(dba_ref)
        dba_ref[:, 0:DN_H] = jnp.where(real, dbeta_ref[...] * beta * (1.0 - beta), 0.0)
        dba_ref[:, DN_H:2 * DN_H] = da

        def through_conv(off, g_cur, g_next, grad_fn):
            sl = pl.ds(off, DK)
            y = jnp.concatenate([yc_ref[:, sl], yn_ref[:, sl]], axis=0)
            sg = jax.nn.sigmoid(y)
            dsilu = sg * (1.0 + y * (1.0 - sg))
            dy = jnp.where(own, grad_fn(y * sg, jnp.concatenate([g_cur, g_next], axis=0)) * dsilu, 0.0)
            shifted = [dy[3 - j:3 - j + CH] for j in range(4)]
            x = xc_ref[:, sl]
            dx = cw_ref[0:1, sl] * shifted[0]
            for j in range(1, 4):
                dx += cw_ref[j:j + 1, sl] * shifted[j]
            dx_ref[:, sl] = dx.astype(BF16)
            for j in range(4):
                dcw_ref[j:j + 1, sl] += jnp.sum(shifted[j] * x, axis=0, keepdims=True)

        def l2_bwd(scale):
            def f(s, gin):
                r = lax.rsqrt(jnp.sum(s * s, axis=-1, keepdims=True) + EPS)
                nrm = s * r
                return (r * scale) * (gin - nrm * jnp.sum(nrm * gin, axis=-1, keepdims=True))
            return f

        def qk_body(kh, carry):
            sl = pl.ds(pl.multiple_of(kh * DK, DK), DK)
            through_conv(pl.multiple_of(kh * DK, DK), dqn_ref[:, sl], dqn_n[:, sl], l2_bwd(DK ** -0.5))
            through_conv(pl.multiple_of(1024 + kh * DK, DK), dkn_ref[:, sl], dkn_n[:, sl], l2_bwd(1.0))
            return carry

        lax.fori_loop(0, DN_KH, qk_body, 0, unroll=GROUP_UNROLL)

        def v_body(hv, carry):
            sl = pl.ds(pl.multiple_of(hv * DK, DK), DK)
            through_conv(pl.multiple_of(2048 + hv * DK, DK), dv_ref[:, sl], dv_n[:, sl], lambda s, gin: gin)
            return carry

        lax.fori_loop(0, DN_H, v_body, 0, unroll=GROUP_UNROLL)

    full = lambda shape: pl.BlockSpec(shape, lambda c: (0, 0))
    blk = lambda w: pl.BlockSpec((CH, w), lambda c: (c, 0))
    nxt = lambda w: pl.BlockSpec((8, w), lambda c: (jnp.minimum(8 * c + 8, rows // 8 - 1), 0))
    return pl.pallas_call(
        body, grid=(nch,),
        in_specs=[_chunk_specs(0)[0], _chunk_specs(0)[2], blk(4096), nxt(4096), nxt(1024), nxt(1024), nxt(2048),
                  full((4, 4096)), full((1, DN_H)), full((1, DN_H)), blk(1024), blk(1024), blk(2048), blk(DN_H),
                  blk(DN_H)],
        out_specs=[blk(4096), blk(DK), full((8, 4096)), full((1, DN_H)), full((1, DN_H))],
        out_shape=[SDS((rows, 4096), BF16), SDS((rows, DK), F32), SDS((8, 4096), F32), SDS((1, DN_H), F32),
                   SDS((1, DN_H), F32)],
        compiler_params=_cp("arbitrary"), name="dn_prep_bwd")(
            udn, udn, yconv, yconv, dqn, dkn, dv, conv_w, a_log, dt_bias, dqn, dkn, dv, dbeta, dg)


def local_step(x, target, w):
    seq = x.shape[0]
    bf = lambda a: a.astype(BF16)
    h0 = jnp.concatenate([jnp.zeros((PAD, D_MODEL), F32), w["meta_tokens"], x], axis=0)
    tgt = jnp.concatenate([jnp.zeros((BLK, D_MODEL), F32), target], axis=0)
    win = w["attn_w_in"]
    wq, wkv, wg = win[:, :1024], win[:, 1024:1280], win[:, 1280:]
    wa_in = bf(jnp.concatenate([wq, wg, wkv], axis=1))
    wa_out = bf(w["attn_w_out"])
    wd_in = jnp.concatenate([bf(w["dn_w_in"]), jnp.zeros((D_MODEL, 96), BF16)], axis=1)
    wd_out = bf(w["dn_w_out"])
    qw, kw, sinks = w["attn_q_norm_w"], w["attn_k_norm_w"], w["attn_sinks"]
    cw, al, dtb, ow = w["dn_conv_w"], w["dn_a_log"], w["dn_dt_bias"], w["dn_o_norm_w"]

    ua, xn0 = norm_matmul(h0, w["attn_norm_w"], wa_in, 2304, "attn_in")
    og = attn_fwd(ua, qw, kw, sinks)
    h1 = matmul_residual(og, wa_out, h0, "attn_out")
    ud, xn1 = norm_matmul(h1, w["dn_norm_w"], wd_in, 6272, "dn_in")
    qn, kn, sv, gc, beta, u, wy, qe, ks, p, at, pt, qet, wt, kst, yconv = dn_prep(ud, cw, al, dtb)
    o, vn, st = dn_scan(u, wy, qe, kst, p, gc)
    dh2, on, ls = dn_out_fwd(o, ud, ow, wd_out, h1, tgt)
    loss = (0.5 / D_MODEL) * jnp.sum(ls)

    do, dz, dow = dn_out_bwd(dh2, wd_out, o, ud, ow)
    g_dn_out = wgrad(on, dh2, "dn_out_wgrad")
    dqn, dkn, dv, dbeta, dg = dn_scan_bwd(do, qn, kn, sv, gc, beta, at, pt, u, wy, vn, qet, wt, ks, st)
    dxc, dba, dcw, dal, ddtb = dn_prep_bwd(ud, yconv, cw, al, dtb, dqn, dkn, dv, dbeta, dg)
    dh1, dnw1 = in_proj_bwd([dxc, dz, dba], [wd_in[:, :4096], wd_in[:, 4096:6144], wd_in[:, 6144:]],
                            h1, w["dn_norm_w"], dh2, "dn_in_bwd")
    g_dn_in = jnp.concatenate([wgrad(xn1, dxc, "dn_in_wgrad_qkv"), wgrad(xn1, dz, "dn_in_wgrad_z"),
                               wgrad(xn1, dba, "dn_in_wgrad_ba")[:, :2 * DN_H]], axis=1)

    dog = matmul_nt(dh1, wa_out, "attn_out_bwd")
    g_attn_out = wgrad(og, dh1, "attn_out_wgrad")
    dq, dgate, dkv, dkvm, dqw, dkw, dsk = attn_bwd(ua, qw, kw, sinks, dog)
    dkv = dkv.at[PAD:BLK].add(dkvm)
    dh0, dnw0 = in_proj_bwd([dq, dgate, dkv], [wa_in[:, :1024], wa_in[:, 1024:2048], wa_in[:, 2048:]],
                            h0, w["attn_norm_w"], dh1, "attn_in_bwd")
    g_attn_in = jnp.concatenate([wgrad(xn0, dq, "attn_in_wgrad_q"), wgrad(xn0, dkv, "attn_in_wgrad_kv"),
                                 wgrad(xn0, dgate, "attn_in_wgrad_g")], axis=1)
    grads = {
        "meta_tokens": dh0[PAD:BLK], "attn_norm_w": jnp.sum(dnw0, axis=0), "attn_w_in": g_attn_in,
        "attn_q_norm_w": dqw, "attn_k_norm_w": dkw, "attn_sinks": dsk, "attn_w_out": g_attn_out,
        "dn_norm_w": jnp.sum(dnw1, axis=0), "dn_w_in": g_dn_in, "dn_conv_w": dcw[:4], "dn_a_log": dal,
        "dn_dt_bias": ddtb, "dn_o_norm_w": jnp.sum(dow, axis=0), "dn_w_out": g_dn_out,
    }
    return loss, dh0[BLK:BLK + seq], grads


WEIGHTS = ["meta_tokens", "attn_norm_w", "attn_w_in", "attn_q_norm_w", "attn_k_norm_w", "attn_sinks", "attn_w_out",
           "dn_norm_w", "dn_w_in", "dn_conv_w", "dn_a_log", "dn_dt_bias", "dn_o_norm_w", "dn_w_out"]
SHARDED = {"attn_w_in": ((1024, 2304), 1), "attn_w_out": ((1024, 1024), 0), "dn_w_in": ((1024, 6176), 1),
           "dn_w_out": ((2048, 1024), 0), "dn_conv_w": ((4, 4096), 1), "meta_tokens": ((16, 1024), 1),
           "dn_norm_w": ((1, 1024), 1)}
REPLICATED = {"attn_norm_w": 1024, "attn_q_norm_w": 64, "attn_k_norm_w": 64, "attn_sinks": 16, "dn_a_log": 16,
              "dn_dt_bias": 16, "dn_o_norm_w": 128}
N_CHIPS = 4
PACK_ROWS = 2912
HALF_ROWS = PACK_ROWS // 2
SMALL_ROWS = 8


def _shard_shape(name):
    (r, c), axis = SHARDED[name]
    return (r // N_CHIPS, c) if axis == 0 else (r, c // N_CHIPS)


def _pack(parts, rows):
    flat = jnp.concatenate([p.reshape(-1) for p in parts])
    return jnp.pad(flat, (0, rows * 1024 - flat.shape[0])).reshape(rows, 1024)


def pack_shard(shards):
    return _pack([shards[n] for n in SHARDED], PACK_ROWS)


def unpack_shard(buf):
    flat, out, pos = buf.reshape(-1), {}, 0
    for n in SHARDED:
        shp = _shard_shape(n)
        size = shp[0] * shp[1]
        out[n] = flat[pos:pos + size].reshape(shp)
        pos += size
    return out


MATRICES = ("attn_w_in", "attn_w_out", "dn_w_in", "dn_w_out")


def pack_gather(shards):
    big = [shards[n].astype(BF16).reshape(-1) for n in MATRICES]
    small = jnp.concatenate([shards[n].reshape(-1) for n in SHARDED if n not in MATRICES])
    flat = jnp.concatenate(big + [lax.bitcast_convert_type(small, BF16).reshape(-1)])
    return jnp.pad(flat, (0, PACK_ROWS * 1024 - flat.shape[0])).reshape(PACK_ROWS, 1024)


def unpack_gather(buf):
    PER_F32 = 4 // jnp.dtype(buf.dtype).itemsize
    flat, out, pos = buf.reshape(-1), {}, 0
    for n in MATRICES:
        shp = _shard_shape(n)
        out[n] = flat[pos:pos + shp[0] * shp[1]].reshape(shp)
        pos += shp[0] * shp[1]
    for n in SHARDED:
        if n not in MATRICES:
            shp = _shard_shape(n)
            raw = flat[pos:pos + shp[0] * shp[1] * PER_F32]
            out[n] = lax.bitcast_convert_type(raw.reshape(-1, PER_F32) if PER_F32 > 1 else raw, F32).reshape(shp)
            pos += shp[0] * shp[1] * PER_F32
    return out


def pack_small(vals):
    return _pack([vals[n] for n in REPLICATED], SMALL_ROWS)


def unpack_small(buf):
    flat, out, pos = buf.reshape(-1), {}, 0
    for n, size in REPLICATED.items():
        out[n] = flat[pos:pos + size].reshape(1, size)
        pos += size
    return out


ANY = pl.BlockSpec(memory_space=pl.ANY)


def _place():
    return lax.axis_index("x"), lax.axis_index("y"), lax.axis_index("c")


def chips_exchange(src, gather):
    r = src.shape[-2]

    def body(s_ref, o_ref, send_sems, recv_sems):
        x, y, c = _place()
        me = 2 * x + y
        peers = [(1 - x, y), (x, 1 - y), (1 - x, 1 - y)]

        def copy(k, to_block, from_block):
            px, py = peers[k]
            return pltpu.make_async_remote_copy(
                src_ref=s_ref if gather else s_ref.at[to_block], dst_ref=o_ref.at[from_block],
                send_sem=send_sems.at[k], recv_sem=recv_sems.at[k], device_id=(px, py, c), device_id_type=MESH)

        sends = [copy(k, 2 * px + py, me) for k, (px, py) in enumerate(peers)]
        for cp in sends:
            cp.start()
        for k, (px, py) in enumerate(peers):
            copy(k, me, 2 * px + py).wait_recv()
        for cp in sends:
            cp.wait_send()

    return pl.pallas_call(
        body, in_specs=[ANY], out_specs=ANY, out_shape=SDS((N_CHIPS, r, 1024), src.dtype),
        scratch_shapes=[pltpu.SemaphoreType.DMA((3,)), pltpu.SemaphoreType.DMA((3,))],
        name="chips_gather" if gather else "chips_exchange")(src)


def chip_sum(received, pair, me):
    tm = 208

    def body(me_ref, own_ref, r1_ref, r2_ref, r3_ref, o_ref):
        o_ref[...] = ((own_ref[0] + r1_ref[0].astype(F32)) + r2_ref[0].astype(F32)) + r3_ref[0].astype(F32)

    blk = lambda k: pl.BlockSpec((1, tm, 1024), lambda i, me_ref: ((me_ref[0] + k) % N_CHIPS, i, 0))
    return pl.pallas_call(
        body,
        grid_spec=pltpu.PrefetchScalarGridSpec(
            num_scalar_prefetch=1, grid=(HALF_ROWS // tm,), in_specs=[blk(0), blk(1), blk(2), blk(3)],
            out_specs=pl.BlockSpec((tm, 1024), lambda i, me_ref: (i, 0))),
        out_shape=SDS((HALF_ROWS, 1024), F32), compiler_params=_cp("parallel"), name="chip_sum")(
            me.reshape(1).astype(jnp.int32), pair, received, received, received)


def _rows_at(ref, start, size):
    return ref.at[:, pl.ds(start, size), :] if len(ref.shape) == 3 else ref.at[pl.ds(start, size), :]


def sibling_join(src, name):
    axis = len(src.shape) - 2

    def body(s_ref, o_ref, send_sem, recv_sem):
        x, y, c = _place()
        cp = pltpu.make_async_remote_copy(src_ref=s_ref, dst_ref=o_ref, send_sem=send_sem, recv_sem=recv_sem,
                                          device_id=(x, y, 1 - c), device_id_type=MESH)
        cp.start()
        cp.wait()

    theirs = pl.pallas_call(
        body, in_specs=[ANY], out_specs=ANY, out_shape=SDS(src.shape, src.dtype),
        scratch_shapes=[pltpu.SemaphoreType.DMA, pltpu.SemaphoreType.DMA], name=name)(src)
    first = lax.axis_index("c") == 0
    return jnp.concatenate([jnp.where(first, src, theirs), jnp.where(first, theirs, src)], axis=axis)


def sibling_give(g_all):
    def body(s_ref, o_ref, send_sem, recv_sem):
        x, y, c = _place()
        cp = pltpu.make_async_remote_copy(
            src_ref=_rows_at(s_ref, (1 - c) * HALF_ROWS, HALF_ROWS), dst_ref=o_ref, send_sem=send_sem,
            recv_sem=recv_sem, device_id=(x, y, 1 - c), device_id_type=MESH)
        cp.start()
        cp.wait()

    return pl.pallas_call(
        body, in_specs=[ANY], out_specs=ANY, out_shape=SDS((N_CHIPS, HALF_ROWS, 1024), F32),
        scratch_shapes=[pltpu.SemaphoreType.DMA, pltpu.SemaphoreType.DMA], name="pair_exchange")(g_all)


def pair_sum(g_all, got, c):
    tm = 208
    per_half = HALF_ROWS // tm

    def body(c_ref, a_ref, b_ref, o_ref, o16_ref):
        s = a_ref[...] + b_ref[...]
        o_ref[...] = s
        o16_ref[...] = s.astype(BF16)

    out = pl.BlockSpec((1, tm, 1024), lambda j, i, c_ref: (j, i, 0))
    return pl.pallas_call(
        body,
        grid_spec=pltpu.PrefetchScalarGridSpec(
            num_scalar_prefetch=1, grid=(N_CHIPS, per_half),
            in_specs=[pl.BlockSpec((1, tm, 1024), lambda j, i, c_ref: (j, c_ref[0] * per_half + i, 0)), out],
            out_specs=[out, out]),
        out_shape=[SDS((N_CHIPS, HALF_ROWS, 1024), F32), SDS((N_CHIPS, HALF_ROWS, 1024), BF16)],
        compiler_params=_cp("parallel", "parallel"), name="pair_sum")(c.reshape(1).astype(jnp.int32), g_all, got)


def all_gather_small(src):
    def body(s_ref, o_ref, send_sems, recv_sems, local_sem):
        x, y, c = _place()
        flips = [(fx, fy, fc) for fx in (0, 1) for fy in (0, 1) for fc in (0, 1)][1:]
        idx = lambda px, py, pc: 4 * px + 2 * py + pc
        mine = pltpu.make_async_copy(s_ref, o_ref.at[idx(x, y, c)], local_sem)
        mine.start()

        def peer(k):
            fx, fy, fc = flips[k]
            return (1 - x if fx else x, 1 - y if fy else y, 1 - c if fc else c)

        def copy(k, block):
            return pltpu.make_async_remote_copy(
                src_ref=s_ref, dst_ref=o_ref.at[block], send_sem=send_sems.at[k], recv_sem=recv_sems.at[k],
                device_id=peer(k), device_id_type=MESH)

        sends = [copy(k, idx(x, y, c)) for k in range(7)]
        for cp in sends:
            cp.start()
        for k in range(7):
            copy(k, idx(*peer(k))).wait_recv()
        for cp in sends:
            cp.wait_send()
        mine.wait()

    return pl.pallas_call(
        body, in_specs=[ANY], out_specs=ANY, out_shape=SDS((8,) + src.shape, F32),
        scratch_shapes=[pltpu.SemaphoreType.DMA((7,)), pltpu.SemaphoreType.DMA((7,)), pltpu.SemaphoreType.DMA],
        name="all_gather_small")(src)


def sum_blocks(t, name):
    n, r, _ = t.shape
    tm = 208 if r % 208 == 0 else r

    def body(t_ref, o_ref):
        acc = t_ref[0]
        for i in range(1, n):
            acc = acc + t_ref[i]
        o_ref[...] = acc

    return pl.pallas_call(
        body, grid=(r // tm,), in_specs=[pl.BlockSpec((n, tm, 1024), lambda i: (0, i, 0))],
        out_specs=pl.BlockSpec((tm, 1024), lambda i: (i, 0)), out_shape=SDS((r, 1024), F32),
        compiler_params=_cp("parallel"), name=name)(t)


ADAM_BLOCK_BYTES = 1024 * 1024


def adamw(w, g, m, v, name):
    rows, cols = w.shape
    tm = rows
    while tm * cols * 4 > ADAM_BLOCK_BYTES and tm % 16 == 0:
        tm //= 2

    def body(w_ref, g_ref, m_ref, v_ref, d_ref, nm_ref, nv_ref):
        g_ = g_ref[...]
        m_ = ADAM_B1 * m_ref[...] + (1.0 - ADAM_B1) * g_
        v_ = ADAM_B2 * v_ref[...] + (1.0 - ADAM_B2) * (g_ * g_)
        m_hat = m_ / (1.0 - ADAM_B1 ** ADAM_STEP)
        v_hat = v_ / (1.0 - ADAM_B2 ** ADAM_STEP)
        d_ref[...] = -ADAM_LR * (m_hat / (jnp.sqrt(v_hat) + ADAM_EPS) + ADAM_WD * w_ref[...])
        nm_ref[...] = m_
        nv_ref[...] = v_

    spec = pl.BlockSpec((tm, cols), lambda i: (i, 0))
    return pl.pallas_call(
        body, grid=(rows // tm,), in_specs=[spec] * 4, out_specs=[spec] * 3,
        out_shape=[SDS((rows, cols), F32)] * 3, compiler_params=_cp("parallel"), name=name)(w, g, m, v)


LAYERED = ("attn_w_in", "attn_w_out", "dn_w_in", "dn_conv_w", "dn_w_out")


def _two_d(name, a):
    return a[0] if name in LAYERED else a


def kernel(x, meta_tokens, attn_norm_w, attn_w_in, attn_q_norm_w, attn_k_norm_w, attn_sinks, attn_w_out, dn_norm_w, dn_w_in, dn_conv_w, dn_a_log, dn_dt_bias, dn_o_norm_w, dn_w_out, loss_target, m_meta_tokens, m_attn_norm_w, m_attn_w_in, m_attn_q_norm_w, m_attn_k_norm_w, m_attn_sinks, m_attn_w_out, m_dn_norm_w, m_dn_w_in, m_dn_conv_w, m_dn_a_log, m_dn_dt_bias, m_dn_o_norm_w, m_dn_w_out, v_meta_tokens, v_attn_norm_w, v_attn_w_in, v_attn_q_norm_w, v_attn_k_norm_w, v_attn_sinks, v_attn_w_out, v_dn_norm_w, v_dn_w_in, v_dn_conv_w, v_dn_a_log, v_dn_dt_bias, v_dn_o_norm_w, v_dn_w_out):
    given = dict(zip(WEIGHTS, (meta_tokens, attn_norm_w, attn_w_in, attn_q_norm_w, attn_k_norm_w, attn_sinks,
                               attn_w_out, dn_norm_w, dn_w_in, dn_conv_w, dn_a_log, dn_dt_bias, dn_o_norm_w, dn_w_out)))
    mom1 = dict(zip(WEIGHTS, (m_meta_tokens, m_attn_norm_w, m_attn_w_in, m_attn_q_norm_w, m_attn_k_norm_w,
                              m_attn_sinks, m_attn_w_out, m_dn_norm_w, m_dn_w_in, m_dn_conv_w, m_dn_a_log,
                              m_dn_dt_bias, m_dn_o_norm_w, m_dn_w_out)))
    mom2 = dict(zip(WEIGHTS, (v_meta_tokens, v_attn_norm_w, v_attn_w_in, v_attn_q_norm_w, v_attn_k_norm_w,
                              v_attn_sinks, v_attn_w_out, v_dn_norm_w, v_dn_w_in, v_dn_conv_w, v_dn_a_log,
                              v_dn_dt_bias, v_dn_o_norm_w, v_dn_w_out)))
    two_d = lambda d: {n: _two_d(n, a) for n, a in d.items()}
    given, mom1, mom2 = two_d(given), two_d(mom1), two_d(mom2)
    c = lax.axis_index("c")

    me = 2 * lax.axis_index("x") + lax.axis_index("y")
    own_half = lax.dynamic_slice_in_dim(pack_gather(given), c * HALF_ROWS, HALF_ROWS, axis=0)
    mine = lax.dynamic_update_slice_in_dim(chips_exchange(own_half, True), own_half[None], me, 0)
    gathered = sibling_join(mine, "gather_swap")
    per_chip = [unpack_gather(gathered[j]) for j in range(N_CHIPS)]
    full = {n: jnp.concatenate([pc[n] for pc in per_chip], axis=SHARDED[n][1]) for n in SHARDED}
    full.update({n: given[n] for n in REPLICATED})

    loss, dx, grads = local_step(x[0], loss_target[0], full)

    split = lambda n: jnp.split(grads[n], N_CHIPS, axis=SHARDED[n][1])
    g_all = jnp.stack([pack_shard({n: split(n)[j] for n in SHARDED}) for j in range(N_CHIPS)])
    pair, pair16 = pair_sum(g_all, sibling_give(g_all), c)
    half = chip_sum(chips_exchange(pair16, False), pair, me)
    g_shard = sibling_join(half, "half_exchange")

    g_small = sum_blocks(all_gather_small(pack_small(grads)), "small_sum")

    g_local = unpack_shard(g_shard)
    g_local.update(unpack_small(g_small))
    steps = {n: adamw(given[n], g_local[n], mom1[n], mom2[n], "adamw_" + n) for n in WEIGHTS}
    shaped = lambda n, a: a[None] if n in LAYERED else a
    outs = [[shaped(n, g_local[n]) for n in WEIGHTS]]
    outs += [[shaped(n, steps[n][k]) for n in WEIGHTS] for k in range(3)]

    loss = lax.psum(loss, ("x", "y", "c"))
    return (loss, dx[None], *outs[0], *outs[1], *outs[2], *outs[3])
```

```python
import functools

import jax
import jax.numpy as jnp
from jax import lax
from jax.experimental import pallas as pl
from jax.experimental.pallas import tpu as pltpu

F32 = jnp.float32
BF16 = jnp.bfloat16
SDS = jax.ShapeDtypeStruct
MESH = pl.DeviceIdType.MESH

D_MODEL = 1024
N_META = 16
EPS = 1e-6
BLK = 128
CH = 64
PAD = BLK - N_META
HEADS = 16
HD = 64
KVW = 256
DN_H = 16
DN_KH = 8
DK = 128
SLOPES = [2.0 ** (-8.0 * (h + 1) / HEADS) for h in range(HEADS)]
NEG = -1e30
NT = (((1,), (1,)), ((), ()))
TN = (((0,), (0,)), ((), ()))
HI = lax.Precision.HIGHEST

ADAM_LR, ADAM_B1, ADAM_B2, ADAM_EPS, ADAM_WD, ADAM_STEP = 0.001, 0.9, 0.999, 1e-08, 0.01, 10

VMEM_LIMIT = 56 * 1024 * 1024
MXU_DEPTH = 256
WGRAD_BLOCK_ELEMS = 2 * 1024 * 1024


def _cp(*sem):
    return pltpu.CompilerParams(dimension_semantics=sem, vmem_limit_bytes=VMEM_LIMIT)


def _row_tile(rows):
    for t in (384, 256, 128):
        if rows % t == 0:
            return t
    raise ValueError(rows)


def _dot(a, b, dims=None, precision=None):
    if dims is None:
        return jnp.dot(a, b, preferred_element_type=F32, precision=precision)
    return lax.dot_general(a, b, dims, preferred_element_type=F32, precision=precision)


def _silu(x):
    return x * jax.nn.sigmoid(x)


def _dsilu(x):
    s = jax.nn.sigmoid(x)
    return s * (1.0 + x * (1.0 - s))


def _rms(x):
    return lax.rsqrt(jnp.mean(x * x, axis=-1, keepdims=True) + EPS)


def norm_matmul(h, nw, w, tn, name):
    rows, k = h.shape
    n = w.shape[1]
    tm = _row_tile(rows)

    def norm_body(h_ref, nw_ref, xn_ref):
        x = h_ref[...]
        xn_ref[...] = (x * _rms(x) * nw_ref[...]).astype(BF16)

    xn = pl.pallas_call(
        norm_body, grid=(rows // tm,),
        in_specs=[pl.BlockSpec((tm, k), lambda i: (i, 0)), pl.BlockSpec((1, k), lambda i: (0, 0))],
        out_specs=pl.BlockSpec((tm, k), lambda i: (i, 0)), out_shape=SDS((rows, k), BF16),
        compiler_params=_cp("parallel"), name=name + "_norm")(h, nw)

    def body(a_ref, w_ref, o_ref):
        o_ref[...] = _dot(a_ref[...], w_ref[...])

    out = pl.pallas_call(
        body, grid=(n // tn, rows // tm),
        in_specs=[pl.BlockSpec((tm, k), lambda j, i: (i, 0)), pl.BlockSpec((k, tn), lambda j, i: (0, j))],
        out_specs=pl.BlockSpec((tm, tn), lambda j, i: (i, j)), out_shape=SDS((rows, n), F32),
        compiler_params=_cp("parallel", "parallel"), name=name)(xn, w)
    return out, xn


def matmul_residual(a, w, res, name):
    rows, k = a.shape
    n = w.shape[1]
    tm = _row_tile(rows)

    def body(a_ref, w_ref, r_ref, o_ref):
        o_ref[...] = r_ref[...] + _dot(a_ref[...], w_ref[...])

    return pl.pallas_call(
        body, grid=(rows // tm,),
        in_specs=[pl.BlockSpec((tm, k), lambda i: (i, 0)), pl.BlockSpec((k, n), lambda i: (0, 0)),
                  pl.BlockSpec((tm, n), lambda i: (i, 0))],
        out_specs=pl.BlockSpec((tm, n), lambda i: (i, 0)),
        out_shape=SDS((rows, n), F32), compiler_params=_cp("parallel"), name=name)(a, w, res)


def wgrad(a, b, name):
    rows, k = a.shape
    n = b.shape[1]
    tm = _row_tile(rows)
    tn = min(n, WGRAD_BLOCK_ELEMS // k)

    def body(a_ref, b_ref, o_ref):
        @pl.when(pl.program_id(1) == 0)
        def _():
            o_ref[...] = jnp.zeros_like(o_ref)

        o_ref[...] += _dot(a_ref[...], b_ref[...].astype(BF16), TN)

    return pl.pallas_call(
        body, grid=(n // tn, rows // tm),
        in_specs=[pl.BlockSpec((tm, k), lambda j, i: (i, 0)), pl.BlockSpec((tm, tn), lambda j, i: (i, j))],
        out_specs=pl.BlockSpec((k, tn), lambda j, i: (0, j)),
        out_shape=SDS((k, n), F32), compiler_params=_cp("parallel", "arbitrary"), name=name)(a, b)


def in_proj_bwd(dus, ws, h, nw, dh_next, name):
    rows, k = h.shape
    tm = _row_tile(rows)
    nd = len(dus)
    nt = rows // tm

    def body(*refs):
        du_refs, w_refs = refs[:nd], refs[nd:2 * nd]
        h_ref, nw_ref, dhn_ref, dh_ref, dnw_ref = refs[2 * nd:]
        dxn = _dot(du_refs[0][...].astype(BF16), w_refs[0][...], NT)
        for du_ref, w_ref in zip(du_refs[1:], w_refs[1:]):
            dxn += _dot(du_ref[...].astype(BF16), w_ref[...], NT)
        x = h_ref[...]
        r = _rms(x)
        y = x * r
        gy = dxn * nw_ref[...]
        dh_ref[...] = dhn_ref[...] + r * (gy - y * jnp.mean(y * gy, axis=-1, keepdims=True))
        dnw_ref[0] = jnp.sum(dxn * y, axis=0, keepdims=True)

    in_specs = [pl.BlockSpec((tm, du.shape[1]), lambda i: (i, 0)) for du in dus]
    in_specs += [pl.BlockSpec(w.shape, lambda i: (0, 0)) for w in ws]
    in_specs += [pl.BlockSpec((tm, k), lambda i: (i, 0)), pl.BlockSpec((1, k), lambda i: (0, 0)),
                 pl.BlockSpec((tm, k), lambda i: (i, 0))]
    return pl.pallas_call(
        body, grid=(nt,), in_specs=in_specs,
        out_specs=[pl.BlockSpec((tm, k), lambda i: (i, 0)), pl.BlockSpec((1, 1, k), lambda i: (i, 0, 0))],
        out_shape=[SDS((rows, k), F32), SDS((nt, 1, k), F32)],
        compiler_params=_cp("parallel"), name=name)(*dus, *ws, h, nw, dh_next)


def matmul_nt(a, w, name):
    rows, k = a.shape
    n = w.shape[0]
    tm = _row_tile(rows)

    def body(a_ref, w_ref, o_ref):
        o_ref[...] = _dot(a_ref[...].astype(BF16), w_ref[...], NT)

    return pl.pallas_call(
        body, grid=(rows // tm,),
        in_specs=[pl.BlockSpec((tm, k), lambda i: (i, 0)), pl.BlockSpec((n, k), lambda i: (0, 0))],
        out_specs=pl.BlockSpec((tm, n), lambda i: (i, 0)),
        out_shape=SDS((rows, n), F32), compiler_params=_cp("parallel"), name=name)(a, w)


SUB = 64
GRP = 8
TR = GRP * SUB
NBAND = 192
TK = 256


def _tile_bias(n, sb):
    r = lax.broadcasted_iota(jnp.int32, (TR, TK), 0)
    c = lax.broadcasted_iota(jnp.int32, (TR, TK), 1)
    qi = r & (SUB - 1)
    d = BLK + qi - c
    dm = n * BLK + SUB * sb - PAD + NBAND + qi - c
    band = c < NBAND
    valid = (band & (d >= 0) & (d < BLK) & (c >= 2 * BLK - BLK * n - SUB * sb)) | (
        (c >= NBAND) & (c < NBAND + N_META) & (dm >= 0))
    return valid, jnp.where(band, d, jnp.minimum(dm, BLK)).astype(F32)


def _group_col(vals):
    g = lax.broadcasted_iota(jnp.int32, (TR, 1), 0) >> 6
    col = jnp.zeros((TR, 1), F32)
    for gi, v in enumerate(vals):
        col = jnp.where(g == gi, v, col)
    return col


def _stack_heads(ref, sb, kvh):
    return jnp.concatenate(
        [ref[SUB * sb:SUB * sb + SUB, HD * (GRP * kvh + g):HD * (GRP * kvh + g) + HD] for g in range(GRP)], axis=0)


def _unstack_heads(parts):
    return jnp.concatenate([parts[kvh][SUB * g:SUB * g + SUB] for kvh in range(2) for g in range(GRP)], axis=1)


def _tile_keys(band, meta, sb):
    return jnp.concatenate([band[SUB * sb:SUB * sb + NBAND], meta,
                            jnp.zeros((TK - NBAND - N_META, HD), band.dtype)], axis=0)


def _row_sums(x):
    ones = jnp.ones((x.shape[1], 128), BF16)
    hi = x.astype(BF16)
    lo = (x - hi.astype(F32)).astype(BF16)
    return _dot(hi, ones) + _dot(lo, ones)


def _rms_stack(q):
    return lax.rsqrt(_row_sums(q * q)[:, :HD] * (1.0 / HD) + EPS)


def _fill_bias(bias_scr, n):
    @pl.when(n <= 2)
    def _():
        for sb in range(2):
            valid, dist = _tile_bias(n, sb)
            for kvh in range(2):
                slope_col = _group_col([SLOPES[GRP * kvh + g] for g in range(GRP)])
                bias_scr[2 * sb + kvh] = jnp.where(valid, -slope_col * dist, NEG)


def _tile_vals(band, meta, sb):
    return jnp.concatenate([_tile_keys(band, meta, sb), jnp.ones((TK, 3 * HD), BF16)], axis=1)


def _tile_softmax(qn16, k16, vx16, bias, sink_col):
    s = _dot(qn16, k16, NT) * (HD ** -0.5) + bias
    mx = jnp.maximum(jnp.max(s.astype(BF16), axis=-1, keepdims=True).astype(F32), sink_col)
    e = jnp.exp(s - mx)
    es = jnp.exp(sink_col - mx)
    ox = _dot(e.astype(BF16), vx16)
    return e, 1.0 / (ox[:, 2 * HD:] + es), es, ox[:, :HD]


def _kv_heads(kvb, kvm, kw_):
    out = []
    for kvh in range(2):
        kb, km = kvb[:, HD * kvh:HD * kvh + HD], kvm[:, HD * kvh:HD * kvh + HD]
        out.append(((kb * _rms(kb) * kw_).astype(BF16), (km * _rms(km) * kw_).astype(BF16),
                    kvb[:, BLK + HD * kvh:BLK + HD * kvh + HD].astype(BF16),
                    kvm[:, BLK + HD * kvh:BLK + HD * kvh + HD].astype(BF16)))
    return out


def _kv_specs(nblk, clamp):
    cur = (lambda n: (jnp.minimum(n, nblk - 1), 8)) if clamp else (lambda n: (n, 8))
    return [pl.BlockSpec((BLK, KVW), cur),
            pl.BlockSpec((BLK, KVW), lambda n: (jnp.maximum(n - 1, 0), 8)),
            pl.BlockSpec((N_META, KVW), lambda n: (PAD // N_META, 8))]


def _sink_cols(sinks):
    return jnp.repeat(sinks.reshape(2, GRP), SUB, axis=1).reshape(2, TR, 1)


SINK_SPEC = pl.BlockSpec((2, TR, 1), lambda n: (0, 0, 0))


def attn_fwd(u, qw, kw, sinks):
    rows = u.shape[0]
    nblk = rows // BLK

    def body(q_ref, g_ref, kvc_ref, kvp_ref, kvm_ref, qw_ref, kw_ref, sc_ref, og_ref, bias_scr):
        _fill_bias(bias_scr, pl.program_id(0))
        qw_ = qw_ref[...]
        kv = _kv_heads(jnp.concatenate([kvp_ref[...], kvc_ref[...]], axis=0), kvm_ref[...], kw_ref[...])
        for sb in range(2):
            parts = []
            for kvh in range(2):
                knb, knm, vb, vm = kv[kvh]
                q = _stack_heads(q_ref, sb, kvh)
                qn16 = (q * _rms_stack(q) * qw_).astype(BF16)
                _, inv, _, o = _tile_softmax(qn16, _tile_keys(knb, knm, sb), _tile_vals(vb, vm, sb),
                                             bias_scr[2 * sb + kvh], sc_ref[kvh])
                parts.append(o * inv[:, :HD])
            rows = slice(SUB * sb, SUB * sb + SUB)
            og_ref[rows, :] = (_unstack_heads(parts) * _silu(g_ref[rows, :])).astype(BF16)

    small = lambda w: pl.BlockSpec((1, w), lambda n: (0, 0))
    return pl.pallas_call(
        body, grid=(nblk,),
        in_specs=[pl.BlockSpec((BLK, 1024), lambda n: (n, 0)), pl.BlockSpec((BLK, 1024), lambda n: (n, 1))]
        + _kv_specs(nblk, False) + [small(HD), small(HD), SINK_SPEC],
        out_specs=pl.BlockSpec((BLK, 1024), lambda n: (n, 0)),
        out_shape=SDS((rows, 1024), BF16), scratch_shapes=[pltpu.VMEM((4, TR, TK), F32)],
        compiler_params=_cp("arbitrary"), name="attn_fwd")(u, u, u, u, u, qw, kw, _sink_cols(sinks))


def attn_bwd(u, qw, kw, sinks, dog):
    rows = u.shape[0]
    nblk = rows // BLK

    def knorm_bwd(k, dkn, kw_):
        r = _rms(k)
        y = k * r
        gy = dkn * kw_
        return r * (gy - y * jnp.mean(y * gy, axis=-1, keepdims=True)), jnp.sum(dkn * y, axis=0, keepdims=True)

    def body(q_ref, g_ref, dog_ref, kvc_ref, kvp_ref, kvm_ref, qw_ref, kw_ref, sc_ref,
             dq_ref, dg_ref, dkv_ref, dkvm_ref, dqw_ref, dkw_ref, dsk_ref, carry, prevp, curp, metap, bias_scr):
        n = pl.program_id(0)
        qw_, kw_ = qw_ref[...], kw_ref[...]
        _fill_bias(bias_scr, n)

        @pl.when(n == 0)
        def _():
            carry[...] = jnp.zeros_like(carry)
            metap[...] = jnp.zeros_like(metap)
            dqw_ref[...] = jnp.zeros_like(dqw_ref)
            dkw_ref[...] = jnp.zeros_like(dkw_ref)
            dsk_ref[...] = jnp.zeros_like(dsk_ref)

        @pl.when(n == nblk)
        def _():
            prevp[...] = jnp.zeros_like(prevp)
            curp[...] = jnp.zeros_like(curp)

        @pl.when(n < nblk)
        def _():
            kv = _kv_heads(jnp.concatenate([kvp_ref[...], kvc_ref[...]], axis=0), kvm_ref[...], kw_)
            lane = lax.broadcasted_iota(jnp.int32, (1, HEADS), 1)
            dqw = jnp.zeros((1, HD), F32)
            dsk = jnp.zeros((1, HEADS), F32)
            band_parts = [jnp.zeros((2 * BLK, HD), F32) for _ in range(4)]
            meta_parts = [jnp.zeros((N_META, HD), F32) for _ in range(4)]

            def widen(x, sb):
                z = jnp.zeros((2 * BLK - NBAND, HD), F32)
                return jnp.concatenate([x, z] if sb == 0 else [z, x], axis=0)

            for sb in range(2):
                rows = slice(SUB * sb, SUB * sb + SUB)
                dq_parts, dg_parts = [], []
                for kvh in range(2):
                    knb, knm, vb, vm = kv[kvh]
                    k16, v16 = _tile_keys(knb, knm, sb), _tile_keys(vb, vm, sb)
                    q = _stack_heads(q_ref, sb, kvh)
                    r = _rms_stack(q)
                    y = q * r
                    qn16 = (y * qw_).astype(BF16)
                    e, inv, es, o = _tile_softmax(qn16, k16, _tile_vals(vb, vm, sb), bias_scr[2 * sb + kvh],
                                                  sc_ref[kvh])
                    p = e * jnp.concatenate([inv, inv], axis=1)
                    p16 = p.astype(BF16)
                    o = o * inv[:, :HD]
                    gate = _stack_heads(g_ref, sb, kvh)
                    dog_ = _stack_heads(dog_ref, sb, kvh)
                    dg_parts.append(dog_ * o * _dsilu(gate))
                    do_ = dog_ * _silu(gate)
                    do16 = do_.astype(BF16)
                    dp = _dot(do16, v16, NT)
                    delta = _row_sums(do_ * o)
                    ds16 = (p * (dp - jnp.concatenate([delta, delta], axis=1))).astype(BF16)
                    dsink = -(es * inv) * delta
                    for g in range(GRP):
                        dsk += jnp.where(lane == GRP * kvh + g,
                                         jnp.sum(dsink[SUB * g:SUB * g + SUB, :HEADS], axis=0, keepdims=True), 0.0)
                    dqn = _dot(ds16, k16) * (HD ** -0.5)
                    dk = (_dot((y * qw_).T.astype(BF16), ds16) * (HD ** -0.5)).T
                    dv = _dot(do_.T.astype(BF16), p16).T
                    band_parts[kvh] += widen(dk[:NBAND], sb)
                    band_parts[2 + kvh] += widen(dv[:NBAND], sb)
                    meta_parts[kvh] += dk[NBAND:NBAND + N_META]
                    meta_parts[2 + kvh] += dv[NBAND:NBAND + N_META]
                    gy = dqn * qw_
                    dq_parts.append(r * (gy - y * (_row_sums(y * gy)[:, :HD] * (1.0 / HD))))
                    dqw += jnp.sum(dqn * y, axis=0, keepdims=True)
                dq_ref[rows, :] = _unstack_heads(dq_parts).astype(BF16)
                dg_ref[rows, :] = _unstack_heads(dg_parts).astype(BF16)
            band = jnp.concatenate(band_parts, axis=1)
            prevp[...] = band[:BLK]
            curp[...] = band[BLK:]
            metap[...] += jnp.concatenate(meta_parts, axis=1)
            dqw_ref[...] += dqw
            dsk_ref[...] += dsk

        tot = carry[...] + prevp[...]
        kprev = kvp_ref[...]
        dk0, w0 = knorm_bwd(kprev[:, 0:HD], tot[:, 0:HD], kw_)
        dk1, w1 = knorm_bwd(kprev[:, HD:2 * HD], tot[:, HD:2 * HD], kw_)
        dkv_ref[...] = jnp.concatenate([dk0, dk1, tot[:, 2 * HD:]], axis=1)
        dkw_ref[...] += w0 + w1
        carry[...] = curp[...]

        @pl.when(n == nblk)
        def _():
            mt = metap[...]
            km = kvm_ref[...]
            m0, v0 = knorm_bwd(km[:, 0:HD], mt[:, 0:HD], kw_)
            m1, v1 = knorm_bwd(km[:, HD:2 * HD], mt[:, HD:2 * HD], kw_)
            dkvm_ref[...] = jnp.concatenate([m0, m1, mt[:, 2 * HD:]], axis=1)
            dkw_ref[...] += v0 + v1

    small = lambda w: pl.BlockSpec((1, w), lambda n: (0, 0))
    cl = lambda n: jnp.minimum(n, nblk - 1)
    return pl.pallas_call(
        body, grid=(nblk + 1,),
        in_specs=[pl.BlockSpec((BLK, 1024), lambda n: (cl(n), 0)), pl.BlockSpec((BLK, 1024), lambda n: (cl(n), 1)),
                  pl.BlockSpec((BLK, 1024), lambda n: (cl(n), 0))]
        + _kv_specs(nblk, True) + [small(HD), small(HD), SINK_SPEC],
        out_specs=[pl.BlockSpec((BLK, 1024), lambda n: (cl(n), 0)), pl.BlockSpec((BLK, 1024), lambda n: (cl(n), 0)),
                   pl.BlockSpec((BLK, KVW), lambda n: (jnp.maximum(n - 1, 0), 0)),
                   pl.BlockSpec((N_META, KVW), lambda n: (0, 0)), small(HD), small(HD), small(HEADS)],
        out_shape=[SDS((rows, 1024), BF16), SDS((rows, 1024), BF16), SDS((rows, KVW), F32), SDS((N_META, KVW), F32),
                   SDS((1, HD), F32), SDS((1, HD), F32), SDS((1, HEADS), F32)],
        scratch_shapes=[pltpu.VMEM((BLK, KVW), F32), pltpu.VMEM((BLK, KVW), F32), pltpu.VMEM((BLK, KVW), F32),
                        pltpu.VMEM((N_META, KVW), F32), pltpu.VMEM((4, TR, TK), F32)],
        compiler_params=_cp("arbitrary"), name="attn_bwd")(u, u, dog, u, u, u, qw, kw, _sink_cols(sinks))


GROUP_UNROLL = 8
HB = 16


def _bdot(a, b, kind, split=False, fused=False):
    dims = {"nn": ((2,), (1,)), "nt": ((2,), (2,)), "tn": ((1,), (1,))}[kind]
    dg = lambda p, q: lax.dot_general(p, q, (dims, ((0,), (0,))), preferred_element_type=F32)
    if not split:
        return dg(a, b)
    if fused:
        assert kind == "nn" and 3 * a.shape[2] <= MXU_DEPTH
        return _dot3(_split(a), _split(b))
    ah, bh = a.astype(BF16), b.astype(BF16)
    al, bl = (a - ah.astype(F32)).astype(BF16), (b - bh.astype(F32)).astype(BF16)
    return (dg(ah, bl) + dg(al, bh)) + dg(ah, bh)


def _head_cols(hv, beta, gc, gct, lane):
    sel = lane == hv
    return _pick(beta, sel), _pick(gc, sel), gct[pl.ds(hv, 1), :]


def _conv_group(xc_ref, xp_ref, cw_ref, off, first):
    xp = jnp.where(first, 0.0, xp_ref[:, pl.ds(off, DK)])
    xx = jnp.concatenate([xp, xc_ref[:, pl.ds(off, DK)]], axis=0)
    y = cw_ref[0:1, pl.ds(off, DK)] * xx[5:5 + CH]
    for j in range(1, 4):
        y += cw_ref[j:j + 1, pl.ds(off, DK)] * xx[5 + j:5 + j + CH]
    return xx, y


def _gates(ba, al, dtb, c):
    row = c * CH + lax.broadcasted_iota(jnp.int32, (CH, DN_H), 0)
    real = row >= PAD
    xa = ba[:, DN_H:2 * DN_H] + dtb
    beta = jnp.where(real, jax.nn.sigmoid(ba[:, 0:DN_H]), 0.0)
    g = jnp.where(real, -jnp.exp(al) * jax.nn.softplus(xa), 0.0)
    return real, xa, beta, g


def _pick(x, sel):
    return jnp.sum(jnp.where(sel, x, 0.0), axis=1, keepdims=True)


def _chunk_specs(width_blocks):
    return [pl.BlockSpec((CH, 4096), lambda c: (c, 0)),
            pl.BlockSpec((8, 4096), lambda c: (jnp.maximum(8 * c - 1, 0), 0)),
            pl.BlockSpec((CH, DK), lambda c: (c, 48))]


def _tri_inv(m, ii, jj):
    eye = (ii == jj).astype(BF16)
    mh, ml = _split(m)
    blk8 = (ii >> 3) == (jj >> 3)
    mb = (jnp.where(blk8, mh, 0), jnp.where(blk8, ml, 0))
    m2 = _split(_dot3(mb, mb))
    m4 = _split(_dot3(m2, m2))
    x = _dot3(_split(_dot3((eye - mb[0], -mb[1]), (eye + m2[0], m2[1]))), (eye + m4[0], m4[1]))
    for sh in (3, 4, 5):
        off = ((ii >> (sh + 1)) == (jj >> (sh + 1))) & ((ii >> sh) != (jj >> sh))
        xs = _split(x)
        x = x - _dot3(_split(_dot3(xs, (jnp.where(off, mh, 0), jnp.where(off, ml, 0)))), xs)
    return x


def _split(x):
    hi = x.astype(BF16)
    return hi, (x - hi.astype(F32)).astype(BF16)


def _dot3(a, b):
    lhs = jnp.concatenate([a[0], a[1], a[0]], axis=2)
    rhs = jnp.concatenate([b[0], b[0], b[1]], axis=1)
    return lax.dot_general(lhs, rhs, (((2,), (1,)), ((0,), (0,))), preferred_element_type=F32)


def dn_prep(udn, conv_w, a_log, dt_bias):
    rows = udn.shape[0]
    nch = rows // CH

    def body(xc_ref, xp_ref, ba_ref, cw_ref, al_ref, dtb_ref,
             qn_ref, kn_ref, sv_ref, gc_ref, beta_ref, u_ref, w_ref, qe_ref, ks_ref, p_ref, at_ref, pt_ref,
             qet_ref, wt_ref, kst_ref, y_ref, gct):
        c = pl.program_id(0)
        first = c == 0
        _, _, beta, g = _gates(ba_ref[...], al_ref[...], dtb_ref[...], c)
        ii = lax.broadcasted_iota(jnp.int32, (CH, CH), 0)
        jj = lax.broadcasted_iota(jnp.int32, (CH, CH), 1)
        gc = _dot((ii >= jj).astype(F32), g, precision=HI)
        gc_ref[...] = gc
        beta_ref[...] = beta
        gct[...] = gc.T

        def qk_body(kh, carry):
            off = pl.multiple_of(kh * DK, DK)
            _, yq = _conv_group(xc_ref, xp_ref, cw_ref, off, first)
            y_ref[:, pl.ds(off, DK)] = yq
            sq = _silu(yq)
            qn_ref[:, pl.ds(off, DK)] = sq * lax.rsqrt(jnp.sum(sq * sq, axis=-1, keepdims=True) + EPS) * (DK ** -0.5)
            _, yk = _conv_group(xc_ref, xp_ref, cw_ref, pl.multiple_of(1024 + kh * DK, DK), first)
            y_ref[:, pl.ds(pl.multiple_of(1024 + kh * DK, DK), DK)] = yk
            sk = _silu(yk)
            kn_ref[:, pl.ds(off, DK)] = sk * lax.rsqrt(jnp.sum(sk * sk, axis=-1, keepdims=True) + EPS)
            return carry

        lax.fori_loop(0, DN_KH, qk_body, 0, unroll=GROUP_UNROLL)
        lane = lax.broadcasted_iota(jnp.int32, (CH, DN_H), 1)
        zpad = jnp.zeros((CH, DK - CH), F32)

        def v_group(grp, carry):
            offs, ks_, qs_, vs_, cols = [], [], [], [], []
            for i in range(HB):
                hv = grp * HB + i
                offs.append(pl.multiple_of(hv * DK, DK))
                koff = pl.multiple_of((grp * (HB // 2) + i // 2) * DK, DK)
                _, yv = _conv_group(xc_ref, xp_ref, cw_ref, pl.multiple_of(2048 + hv * DK, DK), first)
                y_ref[:, pl.ds(pl.multiple_of(2048 + hv * DK, DK), DK)] = yv
                vs_.append(_silu(yv))
                sv_ref[:, pl.ds(offs[i], DK)] = vs_[i]
                ks_.append(kn_ref[:, pl.ds(koff, DK)])
                qs_.append(qn_ref[:, pl.ds(koff, DK)])
                cols.append(_head_cols(hv, beta, gc, gct, lane))
            k, q, v = jnp.stack(ks_), jnp.stack(qs_), jnp.stack(vs_)
            beta_c, gc_c, gc_r = (jnp.stack([c_[j] for c_ in cols]) for j in range(3))
            dec = jnp.exp(jnp.where(ii >= jj, gc_c - gc_r, NEG))
            eg = jnp.exp(gc_c)
            kb = k * beta_c
            k16 = k.astype(BF16)
            m = jnp.where(ii > jj, _bdot(kb.astype(BF16), k16, "nt") * dec, 0.0)
            a = _tri_inv(m, ii, jj)
            uw = _bdot(a, jnp.concatenate([v * beta_c, kb * eg], axis=2), "nn", True, True)
            p = _bdot(q.astype(BF16), k16, "nt") * dec
            qe = q * eg
            ksx = k * jnp.exp(gc_c[:, CH - 1:CH, :] - gc_c)
            tslot = lambda x: jnp.concatenate([x.T, jnp.zeros((DK, DK - CH), F32)], axis=1).astype(BF16)
            for i in range(HB):
                sl = pl.ds(offs[i], DK)
                u_ref[:, sl] = uw[i, :, :DK]
                w_ref[:, sl] = uw[i, :, DK:]
                qe_ref[:, sl] = qe[i].astype(BF16)
                ks_ref[:, sl] = ksx[i].astype(BF16)
                p_ref[:, sl] = jnp.concatenate([p[i], zpad], axis=1).astype(BF16)
                at_ref[:, sl] = jnp.concatenate([a[i].T, zpad], axis=1)
                pt_ref[:, sl] = jnp.concatenate([p[i].T, zpad], axis=1).astype(BF16)
                qet_ref[:, sl] = tslot(qe[i])
                wt_ref[:, sl] = tslot(uw[i, :, DK:])
                kst_ref[:, sl] = tslot(ksx[i])
            return carry

        lax.fori_loop(0, DN_H // HB, v_group, 0)

    full = lambda shape: pl.BlockSpec(shape, lambda c: (0, 0))
    blk = lambda w: pl.BlockSpec((CH, w), lambda c: (c, 0))
    return pl.pallas_call(
        body, grid=(nch,),
        in_specs=_chunk_specs(0) + [full((4, 4096)), full((1, DN_H)), full((1, DN_H))],
        out_specs=[blk(1024), blk(1024), blk(2048), blk(DN_H), blk(DN_H), blk(2048), blk(2048), blk(2048), blk(2048),
                   blk(2048), blk(2048), blk(2048)] + [pl.BlockSpec((DK, 2048), lambda c: (c, 0))] * 3 + [blk(4096)],
        out_shape=[SDS((rows, 1024), F32), SDS((rows, 1024), F32), SDS((rows, 2048), F32), SDS((rows, DN_H), F32),
                   SDS((rows, DN_H), F32), SDS((rows, 2048), F32), SDS((rows, 2048), F32), SDS((rows, 2048), BF16),
                   SDS((rows, 2048), BF16), SDS((rows, 2048), BF16), SDS((rows, 2048), F32),
                   SDS((rows, 2048), BF16)] + [SDS((2 * rows, 2048), BF16)] * 3 + [SDS((rows, 4096), F32)],
        scratch_shapes=[pltpu.VMEM((DN_H, CH), F32)],
        compiler_params=_cp("parallel"), name="dn_prep")(udn, udn, udn, conv_w, a_log, dt_bias)


SCAN_BUFS = 3


def dn_scan(u, w, qe, kst, p, gc):
    rows = u.shape[0]
    nch = rows // CH

    def body(u_hbm, w_hbm, qe_hbm, kst_hbm, p_hbm, gc_ref, o_ref, vn_ref, st_ref, s_scr,
             u_buf, w_buf, qe_buf, kst_buf, p_buf, sems):
        c = pl.program_id(0)
        streams = [(u_hbm, u_buf, CH), (w_hbm, w_buf, CH), (qe_hbm, qe_buf, CH), (kst_hbm, kst_buf, DK),
                   (p_hbm, p_buf, CH)]

        def fetch(k, chunk, slot):
            hbm, buf, r = streams[k]
            return pltpu.make_async_copy(hbm.at[pl.ds(pl.multiple_of(chunk * r, r), r), :], buf.at[slot],
                                         sems.at[k, slot])

        @pl.when(c == 0)
        def _():
            s_scr[...] = jnp.zeros_like(s_scr)
            for k in range(len(streams)):
                for ahead in range(SCAN_BUFS - 1):
                    fetch(k, ahead, ahead).start()

        @pl.when(c + SCAN_BUFS - 1 < nch)
        def _():
            for k in range(len(streams)):
                fetch(k, c + SCAN_BUFS - 1, (c + SCAN_BUFS - 1) % SCAN_BUFS).start()

        slot = c % SCAN_BUFS
        for k in range(len(streams)):
            fetch(k, c, slot).wait()
        u_ref, w_ref, qe_ref, kst_ref, p_ref = (buf.at[slot] for _, buf, _ in streams)
        gl_row = gc_ref[CH - 1:CH, :]
        lane = lax.broadcasted_iota(jnp.int32, (1, DN_H), 1)

        def group(grp, carry):
            base = grp * HB
            sls = [pl.ds(pl.multiple_of((base + i) * DK, DK), DK) for i in range(HB)]
            heads = lambda ref: jnp.stack([ref[:, sl] for sl in sls])
            s = s_scr[pl.ds(base, HB)]
            st_ref[0, pl.ds(base, HB)] = s
            s16 = s.astype(BF16)
            vn = heads(u_ref) - _bdot(heads(w_ref).astype(BF16), s16, "nn")
            vn16 = vn.astype(BF16)
            o = _bdot(heads(qe_ref), s16, "nn") + _bdot(heads(p_ref)[:, :, 0:CH], vn16, "nn")
            egl = jnp.exp(jnp.stack([_pick(gl_row, lane == base + i) for i in range(HB)]))
            s_scr[pl.ds(base, HB)] = s * egl + _bdot(heads(kst_ref)[:, :, 0:CH], vn16, "nn")
            for i in range(HB):
                vn_ref[:, sls[i]] = vn16[i]
                o_ref[:, sls[i]] = o[i]
            return carry

        lax.fori_loop(0, DN_H // HB, group, 0)

    blk = lambda wd: pl.BlockSpec((CH, wd), lambda c: (c, 0))
    assert nch >= SCAN_BUFS
    return pl.pallas_call(
        body, grid=(nch,),
        in_specs=[ANY] * 5 + [blk(DN_H)],
        out_specs=[blk(2048), blk(2048), pl.BlockSpec((1, DN_H, DK, DK), lambda c: (c, 0, 0, 0))],
        out_shape=[SDS((rows, 2048), F32), SDS((rows, 2048), BF16), SDS((nch, DN_H, DK, DK), F32)],
        scratch_shapes=[pltpu.VMEM((DN_H, DK, DK), F32), pltpu.VMEM((SCAN_BUFS, CH, 2048), F32),
                        pltpu.VMEM((SCAN_BUFS, CH, 2048), F32), pltpu.VMEM((SCAN_BUFS, CH, 2048), BF16),
                        pltpu.VMEM((SCAN_BUFS, DK, 2048), BF16), pltpu.VMEM((SCAN_BUFS, CH, 2048), BF16),
                        pltpu.SemaphoreType.DMA((5, SCAN_BUFS))],
        compiler_params=_cp("arbitrary"), name="dn_scan")(u, w, qe, kst, p, gc)


def dn_out_fwd(o, udn, ow, wout, h1, tgt):
    rows = o.shape[0]
    tm = _row_tile(rows)
    nt = rows // tm

    def body(o_ref, z_ref, ow_ref, w_ref, h_ref, t_ref, dh_ref, on_ref, ls_ref):
        for hv in range(DN_H):
            sl = slice(hv * DK, hv * DK + DK)
            oh = o_ref[:, sl]
            on_ref[:, sl] = (oh * _rms(oh) * ow_ref[...] * _silu(z_ref[:, sl])).astype(BF16)
        h2 = h_ref[...] + _dot(on_ref[...], w_ref[...])
        row = pl.program_id(0) * tm + lax.broadcasted_iota(jnp.int32, (tm, 1), 0)
        err = jnp.where(row >= BLK, h2 - t_ref[...], 0.0)
        dh_ref[...] = err * (1.0 / D_MODEL)
        ls_ref[0] = jnp.sum(err * err, axis=0, keepdims=True)

    return pl.pallas_call(
        body, grid=(nt,),
        in_specs=[pl.BlockSpec((tm, 2048), lambda i: (i, 0)), pl.BlockSpec((tm, 2048), lambda i: (i, 2)),
                  pl.BlockSpec((1, DK), lambda i: (0, 0)), pl.BlockSpec((2048, D_MODEL), lambda i: (0, 0)),
                  pl.BlockSpec((tm, D_MODEL), lambda i: (i, 0)), pl.BlockSpec((tm, D_MODEL), lambda i: (i, 0))],
        out_specs=[pl.BlockSpec((tm, D_MODEL), lambda i: (i, 0)), pl.BlockSpec((tm, 2048), lambda i: (i, 0)),
                   pl.BlockSpec((1, 1, D_MODEL), lambda i: (i, 0, 0))],
        out_shape=[SDS((rows, D_MODEL), F32), SDS((rows, 2048), BF16), SDS((nt, 1, D_MODEL), F32)],
        compiler_params=_cp("parallel"), name="dn_out_fwd")(o, udn, ow, wout, h1, tgt)


def dn_out_bwd(dh2, wout, o, udn, ow):
    rows = o.shape[0]
    tm = _row_tile(rows)
    nt = rows // tm

    def body(dh_ref, w_ref, o_ref, z_ref, ow_ref, do_ref, dz_ref, dow_ref):
        don = _dot(dh_ref[...].astype(BF16), w_ref[...], NT)
        ow_ = ow_ref[...]
        dow = jnp.zeros((1, DK), F32)
        for hv in range(DN_H):
            sl = slice(hv * DK, hv * DK + DK)
            oh = o_ref[:, sl]
            r = _rms(oh)
            y = oh * r
            z = z_ref[:, sl]
            dn = don[:, sl] * _silu(z)
            dz_ref[:, sl] = (don[:, sl] * (y * ow_) * _dsilu(z)).astype(BF16)
            dy = dn * ow_
            do_ref[:, sl] = r * (dy - y * jnp.mean(y * dy, axis=-1, keepdims=True))
            dow += jnp.sum(dn * y, axis=0, keepdims=True)
        dow_ref[0] = dow

    return pl.pallas_call(
        body, grid=(nt,),
        in_specs=[pl.BlockSpec((tm, D_MODEL), lambda i: (i, 0)), pl.BlockSpec((2048, D_MODEL), lambda i: (0, 0)),
                  pl.BlockSpec((tm, 2048), lambda i: (i, 0)), pl.BlockSpec((tm, 2048), lambda i: (i, 2)),
                  pl.BlockSpec((1, DK), lambda i: (0, 0))],
        out_specs=[pl.BlockSpec((tm, 2048), lambda i: (i, 0)), pl.BlockSpec((tm, 2048), lambda i: (i, 0)),
                   pl.BlockSpec((1, 1, DK), lambda i: (i, 0, 0))],
        out_shape=[SDS((rows, 2048), F32), SDS((rows, 2048), BF16), SDS((nt, 1, DK), F32)],
        compiler_params=_cp("parallel"), name="dn_out_bwd")(dh2, wout, o, udn, ow)


def dn_scan_bwd(do, qn, kn, sv, gc, beta, at, pt, u, w, vn, qet, wt, ks, st):
    rows = do.shape[0]
    nch = rows // CH

    def body(do_ref, q_ref, k_ref, v_ref, gc_ref, beta_ref, at_ref, pt_ref, u_ref, w_ref, vn_ref, qet_ref, wt_ref,
             ks_ref, st_ref, dq_ref, dk_ref, dv_ref, dbeta_ref, dg_ref, ds_scr, gct):
        @pl.when(pl.program_id(0) == 0)
        def _():
            ds_scr[...] = jnp.zeros_like(ds_scr)

        gc, beta = gc_ref[...], beta_ref[...]
        gct[...] = gc.T
        ii = lax.broadcasted_iota(jnp.int32, (CH, CH), 0)
        jj = lax.broadcasted_iota(jnp.int32, (CH, CH), 1)
        lane = lax.broadcasted_iota(jnp.int32, (CH, DN_H), 1)
        last = lax.broadcasted_iota(jnp.int32, (CH, 1), 0) == CH - 1

        def group(grp, carry):
            dbeta_acc, dgc_acc = carry
            base = grp * HB
            sls = [pl.ds(pl.multiple_of((base + i) * DK, DK), DK) for i in range(HB)]
            ksls = [pl.ds(pl.multiple_of((grp * (HB // 2) + j) * DK, DK), DK) for j in range(HB // 2)]
            heads = lambda ref: jnp.stack([ref[:, sl] for sl in sls])
            kheads = lambda ref: jnp.stack([ref[:, ksls[i // 2]] for i in range(HB)])
            cols = [_head_cols(base + i, beta, gc, gct, lane) for i in range(HB)]
            beta_c, gc_c, gc_r = (jnp.stack([c_[j] for c_ in cols]) for j in range(3))
            k, q, v = kheads(k_ref), kheads(q_ref), heads(v_ref)
            dec = jnp.exp(jnp.where(ii >= jj, gc_c - gc_r, NEG))
            eg = jnp.exp(gc_c)
            gl = gc_c[:, CH - 1:CH, :]
            e2 = jnp.exp(gl - gc_c)
            egl = jnp.exp(gl)
            k16, q16 = k.astype(BF16), q.astype(BF16)
            do16 = heads(do_ref).astype(BF16)
            s = st_ref[0, pl.ds(base, HB)]
            s16 = s.astype(BF16)
            dso = ds_scr[pl.ds(base, HB)]
            dso16 = dso.astype(BF16)
            wf, uf, vn16 = heads(w_ref), heads(u_ref), heads(vn_ref)
            kb = k * beta_c
            kb16 = kb.astype(BF16)
            pm = _bdot(q16, k16, "nt") * dec
            m = jnp.where(ii > jj, _bdot(kb16, k16, "nt") * dec, 0.0)
            dvn = _bdot(heads(pt_ref)[:, :, 0:CH], do16, "nn") + _bdot(heads(ks_ref), dso16, "nn")
            dvn16 = dvn.astype(BF16)
            ds_scr[pl.ds(base, HB)] = (egl * dso + _bdot(heads(qet_ref)[:, :, 0:CH], do16, "nn")
                                       - _bdot(heads(wt_ref)[:, :, 0:CH], dvn16, "nn"))
            dpm = jnp.where(ii >= jj, _bdot(do16, vn16, "nt"), 0.0)
            dqk16 = (dpm * dec).astype(BF16)
            dqe = _bdot(do16, s16, "nt")
            dq = eg * dqe + _bdot(dqk16, k16, "nn")
            dks = _bdot(vn16, dso16, "nt")
            dw = -_bdot(dvn16, s16, "nt")
            dbvk = _bdot(heads(at_ref)[:, :, 0:CH], jnp.concatenate([dvn, dw], axis=2), "nn", True)
            dbv, dbk = dbvk[:, :, :DK], dbvk[:, :, DK:]
            dm = jnp.where(ii > jj, -_bdot(dbvk, jnp.concatenate([uf, wf], axis=2), "nt", True), 0.0)
            g16 = (dm * dec).astype(BF16)
            dkb = _bdot(g16, k16, "nn")
            dk = (_bdot(dqk16, q16, "tn") + e2 * dks + _bdot(g16, kb16, "tn") + beta_c * (eg * dbk + dkb))
            e = dpm * pm + dm * m
            rsum = lambda x: jnp.sum(x, axis=2, keepdims=True)
            r_bk, r_qe, r_beta, r_ks = rsum(dbk * k), rsum(q * dqe), rsum(dbv * v + dkb * k), rsum(dks * k)
            t = r_ks * e2
            dgl = jnp.sum(t, axis=1, keepdims=True) + egl * rsum(jnp.sum(dso * s, axis=1, keepdims=True))
            deg = r_qe + beta_c * r_bk
            dgc = rsum(e) - t + deg * eg + jnp.where(last, dgl, 0.0)
            dgrow = -jnp.sum(e, axis=1, keepdims=True)
            dv = beta_c * dbv
            dbeta = r_beta + eg * r_bk
            for i in range(HB):
                dv_ref[:, sls[i]] = dv[i]
                sel = lane == base + i
                dbeta_acc = jnp.where(sel, dbeta[i], dbeta_acc)
                dgc_acc = jnp.where(sel, dgc[i], dgc_acc)
                gct[pl.ds(base + i, 1), :] = dgrow[i]
            for j in range(HB // 2):
                dq_ref[:, ksls[j]] = dq[2 * j] + dq[2 * j + 1]
                dk_ref[:, ksls[j]] = dk[2 * j] + dk[2 * j + 1]
            return dbeta_acc, dgc_acc

        zero = jnp.zeros((CH, DN_H), F32)
        dbeta_acc, dgc_acc = lax.fori_loop(0, DN_H // HB, group, (zero, zero))
        dbeta_ref[...] = dbeta_acc
        dg_ref[...] = _dot((ii <= jj).astype(F32), dgc_acc + gct[...].T, precision=HI)

    rev = lambda wd: pl.BlockSpec((CH, wd), lambda i: (nch - 1 - i, 0))
    rev_t = pl.BlockSpec((DK, 2048), lambda i: (nch - 1 - i, 0))
    return pl.pallas_call(
        body, grid=(nch,),
        in_specs=[rev(2048), rev(1024), rev(1024), rev(2048), rev(DN_H), rev(DN_H), rev(2048), rev(2048), rev(2048),
                  rev(2048), rev(2048), rev_t, rev_t, rev(2048),
                  pl.BlockSpec((1, DN_H, DK, DK), lambda i: (nch - 1 - i, 0, 0, 0))],
        out_specs=[rev(1024), rev(1024), rev(2048), rev(DN_H), rev(DN_H)],
        out_shape=[SDS((rows, 1024), F32), SDS((rows, 1024), F32), SDS((rows, 2048), F32), SDS((rows, DN_H), F32),
                   SDS((rows, DN_H), F32)],
        scratch_shapes=[pltpu.VMEM((DN_H, DK, DK), F32), pltpu.VMEM((DN_H, CH), F32)],
        compiler_params=_cp("arbitrary"), name="dn_scan_bwd")(
            do, qn, kn, sv, gc, beta, at, pt, u, w, vn, qet, wt, ks, st)


def dn_prep_bwd(udn, yconv, conv_w, a_log, dt_bias, dqn, dkn, dv, dbeta, dg):
    rows = udn.shape[0]
    nch = rows // CH
    ext = CH + 8

    def body(xc_ref, ba_ref, yc_ref, yn_ref, dqn_n, dkn_n, dv_n, cw_ref, al_ref, dtb_ref, dqn_ref, dkn_ref, dv_ref,
             dbeta_ref, dg_ref, dx_ref, dba_ref, dcw_ref, dal_ref, ddtb_ref):
        c = pl.program_id(0)
        first = c == 0
        own = (lax.broadcasted_iota(jnp.int32, (ext, 1), 0) < CH) | (c < nch - 1)

        @pl.when(first)
        def _():
            dcw_ref[...] = jnp.zeros_like(dcw_ref)
            dal_ref[...] = jnp.zeros_like(dal_ref)
            ddtb_ref[...] = jnp.zeros_like(ddtb_ref)

        real, xa, beta, g = _gates(ba_ref[...], al_ref[...], dtb_ref[...], c)
        dgm = jnp.where(real, dg_ref[...], 0.0)
        da = dgm * (-jnp.exp(al_ref[...])) * jax.nn.sigmoid(xa)
        dal_ref[...] += jnp.sum(dgm * g, axis=0, keepdims=True)
        ddtb_ref[...] += jnp.sum(da, axis=0, keepdims=True)
        dba_ref[...] = jnp.zeros_like(dba_ref)
        dba_ref[:, 0:DN_H] = jnp.where(real, dbeta_ref[...] * beta * (1.0 - beta), 0.0)
        dba_ref[:, DN_H:2 * DN_H] = da

        def through_conv(off, g_cur, g_next, grad_fn):
            sl = pl.ds(off, DK)
            y = jnp.concatenate([yc_ref[:, sl], yn_ref[:, sl]], axis=0)
            sg = jax.nn.sigmoid(y)
            dsilu = sg * (1.0 + y * (1.0 - sg))
            dy = jnp.where(own, grad_fn(y * sg, jnp.concatenate([g_cur, g_next], axis=0)) * dsilu, 0.0)
            shifted = [dy[3 - j:3 - j + CH] for j in range(4)]
            x = xc_ref[:, sl]
            dx = cw_ref[0:1, sl] * shifted[0]
            for j in range(1, 4):
                dx += cw_ref[j:j + 1, sl] * shifted[j]
            dx_ref[:, sl] = dx.astype(BF16)
            for j in range(4):
                dcw_ref[j:j + 1, sl] += jnp.sum(shifted[j] * x, axis=0, keepdims=True)

        def l2_bwd(scale):
            def f(s, gin):
                r = lax.rsqrt(jnp.sum(s * s, axis=-1, keepdims=True) + EPS)
                nrm = s * r
                return (r * scale) * (gin - nrm * jnp.sum(nrm * gin, axis=-1, keepdims=True))
            return f

        def qk_body(kh, carry):
            sl = pl.ds(pl.multiple_of(kh * DK, DK), DK)
            through_conv(pl.multiple_of(kh * DK, DK), dqn_ref[:, sl], dqn_n[:, sl], l2_bwd(DK ** -0.5))
            through_conv(pl.multiple_of(1024 + kh * DK, DK), dkn_ref[:, sl], dkn_n[:, sl], l2_bwd(1.0))
            return carry

        lax.fori_loop(0, DN_KH, qk_body, 0, unroll=GROUP_UNROLL)

        def v_body(hv, carry):
            sl = pl.ds(pl.multiple_of(hv * DK, DK), DK)
            through_conv(pl.multiple_of(2048 + hv * DK, DK), dv_ref[:, sl], dv_n[:, sl], lambda s, gin: gin)
            return carry

        lax.fori_loop(0, DN_H, v_body, 0, unroll=GROUP_UNROLL)

    full = lambda shape: pl.BlockSpec(shape, lambda c: (0, 0))
    blk = lambda w: pl.BlockSpec((CH, w), lambda c: (c, 0))
    nxt = lambda w: pl.BlockSpec((8, w), lambda c: (jnp.minimum(8 * c + 8, rows // 8 - 1), 0))
    return pl.pallas_call(
        body, grid=(nch,),
        in_specs=[_chunk_specs(0)[0], _chunk_specs(0)[2], blk(4096), nxt(4096), nxt(1024), nxt(1024), nxt(2048),
                  full((4, 4096)), full((1, DN_H)), full((1, DN_H)), blk(1024), blk(1024), blk(2048), blk(DN_H),
                  blk(DN_H)],
        out_specs=[blk(4096), blk(DK), full((8, 4096)), full((1, DN_H)), full((1, DN_H))],
        out_shape=[SDS((rows, 4096), BF16), SDS((rows, DK), F32), SDS((8, 4096), F32), SDS((1, DN_H), F32),
                   SDS((1, DN_H), F32)],
        compiler_params=_cp("arbitrary"), name="dn_prep_bwd")(
            udn, udn, yconv, yconv, dqn, dkn, dv, conv_w, a_log, dt_bias, dqn, dkn, dv, dbeta, dg)


def local_step(x, target, w):
    seq = x.shape[0]
    bf = lambda a: a.astype(BF16)
    h0 = jnp.concatenate([jnp.zeros((PAD, D_MODEL), F32), w["meta_tokens"], x], axis=0)
    tgt = jnp.concatenate([jnp.zeros((BLK, D_MODEL), F32), target], axis=0)
    win = w["attn_w_in"]
    wq, wkv, wg = win[:, :1024], win[:, 1024:1280], win[:, 1280:]
    wa_in = bf(jnp.concatenate([wq, wg, wkv], axis=1))
    wa_out = bf(w["attn_w_out"])
    wd_in = jnp.concatenate([bf(w["dn_w_in"]), jnp.zeros((D_MODEL, 96), BF16)], axis=1)
    wd_out = bf(w["dn_w_out"])
    qw, kw, sinks = w["attn_q_norm_w"], w["attn_k_norm_w"], w["attn_sinks"]
    cw, al, dtb, ow = w["dn_conv_w"], w["dn_a_log"], w["dn_dt_bias"], w["dn_o_norm_w"]

    ua, xn0 = norm_matmul(h0, w["attn_norm_w"], wa_in, 2304, "attn_in")
    og = attn_fwd(ua, qw, kw, sinks)
    h1 = matmul_residual(og, wa_out, h0, "attn_out")
    ud, xn1 = norm_matmul(h1, w["dn_norm_w"], wd_in, 6272, "dn_in")
    qn, kn, sv, gc, beta, u, wy, qe, ks, p, at, pt, qet, wt, kst, yconv = dn_prep(ud, cw, al, dtb)
    o, vn, st = dn_scan(u, wy, qe, kst, p, gc)
    dh2, on, ls = dn_out_fwd(o, ud, ow, wd_out, h1, tgt)
    loss = (0.5 / D_MODEL) * jnp.sum(ls)

    do, dz, dow = dn_out_bwd(dh2, wd_out, o, ud, ow)
    g_dn_out = wgrad(on, dh2, "dn_out_wgrad")
    dqn, dkn, dv, dbeta, dg = dn_scan_bwd(do, qn, kn, sv, gc, beta, at, pt, u, wy, vn, qet, wt, ks, st)
    dxc, dba, dcw, dal, ddtb = dn_prep_bwd(ud, yconv, cw, al, dtb, dqn, dkn, dv, dbeta, dg)
    dh1, dnw1 = in_proj_bwd([dxc, dz, dba], [wd_in[:, :4096], wd_in[:, 4096:6144], wd_in[:, 6144:]],
                            h1, w["dn_norm_w"], dh2, "dn_in_bwd")
    g_dn_in = jnp.concatenate([wgrad(xn1, dxc, "dn_in_wgrad_qkv"), wgrad(xn1, dz, "dn_in_wgrad_z"),
                               wgrad(xn1, dba, "dn_in_wgrad_ba")[:, :2 * DN_H]], axis=1)

    dog = matmul_nt(dh1, wa_out, "attn_out_bwd")
    g_attn_out = wgrad(og, dh1, "attn_out_wgrad")
    dq, dgate, dkv, dkvm, dqw, dkw, dsk = attn_bwd(ua, qw, kw, sinks, dog)
    dkv = dkv.at[PAD:BLK].add(dkvm)
    dh0, dnw0 = in_proj_bwd([dq, dgate, dkv], [wa_in[:, :1024], wa_in[:, 1024:2048], wa_in[:, 2048:]],
                            h0, w["attn_norm_w"], dh1, "attn_in_bwd")
    g_attn_in = jnp.concatenate([wgrad(xn0, dq, "attn_in_wgrad_q"), wgrad(xn0, dkv, "attn_in_wgrad_kv"),
                                 wgrad(xn0, dgate, "attn_in_wgrad_g")], axis=1)
    grads = {
        "meta_tokens": dh0[PAD:BLK], "attn_norm_w": jnp.sum(dnw0, axis=0), "attn_w_in": g_attn_in,
        "attn_q_norm_w": dqw, "attn_k_norm_w": dkw, "attn_sinks": dsk, "attn_w_out": g_attn_out,
        "dn_norm_w": jnp.sum(dnw1, axis=0), "dn_w_in": g_dn_in, "dn_conv_w": dcw[:4], "dn_a_log": dal,
        "dn_dt_bias": ddtb, "dn_o_norm_w": jnp.sum(dow, axis=0), "dn_w_out": g_dn_out,
    }
    return loss, dh0[BLK:BLK + seq], grads


WEIGHTS = ["meta_tokens", "attn_norm_w", "attn_w_in", "attn_q_norm_w", "attn_k_norm_w", "attn_sinks", "attn_w_out",
           "dn_norm_w", "dn_w_in", "dn_conv_w", "dn_a_log", "dn_dt_bias", "dn_o_norm_w", "dn_w_out"]
SHARDED = {"attn_w_in": ((1024, 2304), 1), "attn_w_out": ((1024, 1024), 0), "dn_w_in": ((1024, 6176), 1),
           "dn_w_out": ((2048, 1024), 0), "dn_conv_w": ((4, 4096), 1), "meta_tokens": ((16, 1024), 1),
           "dn_norm_w": ((1, 1024), 1)}
REPLICATED = {"attn_norm_w": 1024, "attn_q_norm_w": 64, "attn_k_norm_w": 64, "attn_sinks": 16, "dn_a_log": 16,
              "dn_dt_bias": 16, "dn_o_norm_w": 128}
N_CHIPS = 4
PACK_ROWS = 2912
HALF_ROWS = PACK_ROWS // 2
SMALL_ROWS = 8


def _shard_shape(name):
    (r, c), axis = SHARDED[name]
    return (r // N_CHIPS, c) if axis == 0 else (r, c // N_CHIPS)


def _pack(parts, rows):
    flat = jnp.concatenate([p.reshape(-1) for p in parts])
    return jnp.pad(flat, (0, rows * 1024 - flat.shape[0])).reshape(rows, 1024)


def pack_shard(shards):
    return _pack([shards[n] for n in SHARDED], PACK_ROWS)


def unpack_shard(buf):
    flat, out, pos = buf.reshape(-1), {}, 0
    for n in SHARDED:
        shp = _shard_shape(n)
        size = shp[0] * shp[1]
        out[n] = flat[pos:pos + size].reshape(shp)
        pos += size
    return out


MATRICES = ("attn_w_in", "attn_w_out", "dn_w_in", "dn_w_out")


def pack_gather(shards):
    big = [shards[n].astype(BF16).reshape(-1) for n in MATRICES]
    small = jnp.concatenate([shards[n].reshape(-1) for n in SHARDED if n not in MATRICES])
    flat = jnp.concatenate(big + [lax.bitcast_convert_type(small, BF16).reshape(-1)])
    return jnp.pad(flat, (0, PACK_ROWS * 1024 - flat.shape[0])).reshape(PACK_ROWS, 1024)


def unpack_gather(buf):
    PER_F32 = 4 // jnp.dtype(buf.dtype).itemsize
    flat, out, pos = buf.reshape(-1), {}, 0
    for n in MATRICES:
        shp = _shard_shape(n)
        out[n] = flat[pos:pos + shp[0] * shp[1]].reshape(shp)
        pos += shp[0] * shp[1]
    for n in SHARDED:
        if n not in MATRICES:
            shp = _shard_shape(n)
            raw = flat[pos:pos + shp[0] * shp[1] * PER_F32]
            out[n] = lax.bitcast_convert_type(raw.reshape(-1, PER_F32) if PER_F32 > 1 else raw, F32).reshape(shp)
            pos += shp[0] * shp[1] * PER_F32
    return out


def pack_small(vals):
    return _pack([vals[n] for n in REPLICATED], SMALL_ROWS)


def unpack_small(buf):
    flat, out, pos = buf.reshape(-1), {}, 0
    for n, size in REPLICATED.items():
        out[n] = flat[pos:pos + size].reshape(1, size)
        pos += size
    return out


ANY = pl.BlockSpec(memory_space=pl.ANY)


def _place():
    return lax.axis_index("x"), lax.axis_index("y"), lax.axis_index("c")


def chips_exchange(src, gather):
    r = src.shape[-2]

    def body(s_ref, o_ref, send_sems, recv_sems):
        x, y, c = _place()
        me = 2 * x + y
        peers = [(1 - x, y), (x, 1 - y), (1 - x, 1 - y)]

        def copy(k, to_block, from_block):
            px, py = peers[k]
            return pltpu.make_async_remote_copy(
                src_ref=s_ref if gather else s_ref.at[to_block], dst_ref=o_ref.at[from_block],
                send_sem=send_sems.at[k], recv_sem=recv_sems.at[k], device_id=(px, py, c), device_id_type=MESH)

        sends = [copy(k, 2 * px + py, me) for k, (px, py) in enumerate(peers)]
        for cp in sends:
            cp.start()
        for k, (px, py) in enumerate(peers):
            copy(k, me, 2 * px + py).wait_recv()
        for cp in sends:
            cp.wait_send()

    return pl.pallas_call(
        body, in_specs=[ANY], out_specs=ANY, out_shape=SDS((N_CHIPS, r, 1024), src.dtype),
        scratch_shapes=[pltpu.SemaphoreType.DMA((3,)), pltpu.SemaphoreType.DMA((3,))],
        name="chips_gather" if gather else "chips_exchange")(src)


def chip_sum(received, pair, me):
    tm = 208

    def body(me_ref, own_ref, r1_ref, r2_ref, r3_ref, o_ref):
        o_ref[...] = ((own_ref[0] + r1_ref[0].astype(F32)) + r2_ref[0].astype(F32)) + r3_ref[0].astype(F32)

    blk = lambda k: pl.BlockSpec((1, tm, 1024), lambda i, me_ref: ((me_ref[0] + k) % N_CHIPS, i, 0))
    return pl.pallas_call(
        body,
        grid_spec=pltpu.PrefetchScalarGridSpec(
            num_scalar_prefetch=1, grid=(HALF_ROWS // tm,), in_specs=[blk(0), blk(1), blk(2), blk(3)],
            out_specs=pl.BlockSpec((tm, 1024), lambda i, me_ref: (i, 0))),
        out_shape=SDS((HALF_ROWS, 1024), F32), compiler_params=_cp("parallel"), name="chip_sum")(
            me.reshape(1).astype(jnp.int32), pair, received, received, received)


def _rows_at(ref, start, size):
    return ref.at[:, pl.ds(start, size), :] if len(ref.shape) == 3 else ref.at[pl.ds(start, size), :]


def sibling_join(src, name):
    axis = len(src.shape) - 2

    def body(s_ref, o_ref, send_sem, recv_sem):
        x, y, c = _place()
        cp = pltpu.make_async_remote_copy(src_ref=s_ref, dst_ref=o_ref, send_sem=send_sem, recv_sem=recv_sem,
                                          device_id=(x, y, 1 - c), device_id_type=MESH)
        cp.start()
        cp.wait()

    theirs = pl.pallas_call(
        body, in_specs=[ANY], out_specs=ANY, out_shape=SDS(src.shape, src.dtype),
        scratch_shapes=[pltpu.SemaphoreType.DMA, pltpu.SemaphoreType.DMA], name=name)(src)
    first = lax.axis_index("c") == 0
    return jnp.concatenate([jnp.where(first, src, theirs), jnp.where(first, theirs, src)], axis=axis)


def sibling_give(g_all):
    def body(s_ref, o_ref, send_sem, recv_sem):
        x, y, c = _place()
        cp = pltpu.make_async_remote_copy(
            src_ref=_rows_at(s_ref, (1 - c) * HALF_ROWS, HALF_ROWS), dst_ref=o_ref, send_sem=send_sem,
            recv_sem=recv_sem, device_id=(x, y, 1 - c), device_id_type=MESH)
        cp.start()
        cp.wait()

    return pl.pallas_call(
        body, in_specs=[ANY], out_specs=ANY, out_shape=SDS((N_CHIPS, HALF_ROWS, 1024), F32),
        scratch_shapes=[pltpu.SemaphoreType.DMA, pltpu.SemaphoreType.DMA], name="pair_exchange")(g_all)


def pair_sum(g_all, got, c):
    tm = 208
    per_half = HALF_ROWS // tm

    def body(c_ref, a_ref, b_ref, o_ref, o16_ref):
        s = a_ref[...] + b_ref[...]
        o_ref[...] = s
        o16_ref[...] = s.astype(BF16)

    out = pl.BlockSpec((1, tm, 1024), lambda j, i, c_ref: (j, i, 0))
    return pl.pallas_call(
        body,
        grid_spec=pltpu.PrefetchScalarGridSpec(
            num_scalar_prefetch=1, grid=(N_CHIPS, per_half),
            in_specs=[pl.BlockSpec((1, tm, 1024), lambda j, i, c_ref: (j, c_ref[0] * per_half + i, 0)), out],
            out_specs=[out, out]),
        out_shape=[SDS((N_CHIPS, HALF_ROWS, 1024), F32), SDS((N_CHIPS, HALF_ROWS, 1024), BF16)],
        compiler_params=_cp("parallel", "parallel"), name="pair_sum")(c.reshape(1).astype(jnp.int32), g_all, got)


def all_gather_small(src):
    def body(s_ref, o_ref, send_sems, recv_sems, local_sem):
        x, y, c = _place()
        flips = [(fx, fy, fc) for fx in (0, 1) for fy in (0, 1) for fc in (0, 1)][1:]
        idx = lambda px, py, pc: 4 * px + 2 * py + pc
        mine = pltpu.make_async_copy(s_ref, o_ref.at[idx(x, y, c)], local_sem)
        mine.start()

        def peer(k):
            fx, fy, fc = flips[k]
            return (1 - x if fx else x, 1 - y if fy else y, 1 - c if fc else c)

        def copy(k, block):
            return pltpu.make_async_remote_copy(
                src_ref=s_ref, dst_ref=o_ref.at[block], send_sem=send_sems.at[k], recv_sem=recv_sems.at[k],
                device_id=peer(k), device_id_type=MESH)

        sends = [copy(k, idx(x, y, c)) for k in range(7)]
        for cp in sends:
            cp.start()
        for k in range(7):
            copy(k, idx(*peer(k))).wait_recv()
        for cp in sends:
            cp.wait_send()
        mine.wait()

    return pl.pallas_call(
        body, in_specs=[ANY], out_specs=ANY, out_shape=SDS((8,) + src.shape, F32),
        scratch_shapes=[pltpu.SemaphoreType.DMA((7,)), pltpu.SemaphoreType.DMA((7,)), pltpu.SemaphoreType.DMA],
        name="all_gather_small")(src)


def sum_blocks(t, name):
    n, r, _ = t.shape
    tm = 208 if r % 208 == 0 else r

    def body(t_ref, o_ref):
        acc = t_ref[0]
        for i in range(1, n):
            acc = acc + t_ref[i]
        o_ref[...] = acc

    return pl.pallas_call(
        body, grid=(r // tm,), in_specs=[pl.BlockSpec((n, tm, 1024), lambda i: (0, i, 0))],
        out_specs=pl.BlockSpec((tm, 1024), lambda i: (i, 0)), out_shape=SDS((r, 1024), F32),
        compiler_params=_cp("parallel"), name=name)(t)


ADAM_BLOCK_BYTES = 1024 * 1024


def adamw(w, g, m, v, name):
    rows, cols = w.shape
    tm = rows
    while tm * cols * 4 > ADAM_BLOCK_BYTES and tm % 16 == 0:
        tm //= 2

    def body(w_ref, g_ref, m_ref, v_ref, d_ref, nm_ref, nv_ref):
        g_ = g_ref[...]
        m_ = ADAM_B1 * m_ref[...] + (1.0 - ADAM_B1) * g_
        v_ = ADAM_B2 * v_ref[...] + (1.0 - ADAM_B2) * (g_ * g_)
        m_hat = m_ / (1.0 - ADAM_B1 ** ADAM_STEP)
        v_hat = v_ / (1.0 - ADAM_B2 ** ADAM_STEP)
        d_ref[...] = -ADAM_LR * (m_hat / (jnp.sqrt(v_hat) + ADAM_EPS) + ADAM_WD * w_ref[...])
        nm_ref[...] = m_
        nv_ref[...] = v_

    spec = pl.BlockSpec((tm, cols), lambda i: (i, 0))
    return pl.pallas_call(
        body, grid=(rows // tm,), in_specs=[spec] * 4, out_specs=[spec] * 3,
        out_shape=[SDS((rows, cols), F32)] * 3, compiler_params=_cp("parallel"), name=name)(w, g, m, v)


LAYERED = ("attn_w_in", "attn_w_out", "dn_w_in", "dn_conv_w", "dn_w_out")


def _two_d(name, a):
    return a[0] if name in LAYERED else a


def kernel(x, meta_tokens, attn_norm_w, attn_w_in, attn_q_norm_w, attn_k_norm_w, attn_sinks, attn_w_out, dn_norm_w, dn_w_in, dn_conv_w, dn_a_log, dn_dt_bias, dn_o_norm_w, dn_w_out, loss_target, m_meta_tokens, m_attn_norm_w, m_attn_w_in, m_attn_q_norm_w, m_attn_k_norm_w, m_attn_sinks, m_attn_w_out, m_dn_norm_w, m_dn_w_in, m_dn_conv_w, m_dn_a_log, m_dn_dt_bias, m_dn_o_norm_w, m_dn_w_out, v_meta_tokens, v_attn_norm_w, v_attn_w_in, v_attn_q_norm_w, v_attn_k_norm_w, v_attn_sinks, v_attn_w_out, v_dn_norm_w, v_dn_w_in, v_dn_conv_w, v_dn_a_log, v_dn_dt_bias, v_dn_o_norm_w, v_dn_w_out):
    given = dict(zip(WEIGHTS, (meta_tokens, attn_norm_w, attn_w_in, attn_q_norm_w, attn_k_norm_w, attn_sinks,
                               attn_w_out, dn_norm_w, dn_w_in, dn_conv_w, dn_a_log, dn_dt_bias, dn_o_norm_w, dn_w_out)))
    mom1 = dict(zip(WEIGHTS, (m_meta_tokens, m_attn_norm_w, m_attn_w_in, m_attn_q_norm_w, m_attn_k_norm_w,
                              m_attn_sinks, m_attn_w_out, m_dn_norm_w, m_dn_w_in, m_dn_conv_w, m_dn_a_log,
                              m_dn_dt_bias, m_dn_o_norm_w, m_dn_w_out)))
    mom2 = dict(zip(WEIGHTS, (v_meta_tokens, v_attn_norm_w, v_attn_w_in, v_attn_q_norm_w, v_attn_k_norm_w,
                              v_attn_sinks, v_attn_w_out, v_dn_norm_w, v_dn_w_in, v_dn_conv_w, v_dn_a_log,
                              v_dn_dt_bias, v_dn_o_norm_w, v_dn_w_out)))
    two_d = lambda d: {n: _two_d(n, a) for n, a in d.items()}
    given, mom1, mom2 = two_d(given), two_d(mom1), two_d(mom2)
    c = lax.axis_index("c")

    me = 2 * lax.axis_index("x") + lax.axis_index("y")
    own_half = lax.dynamic_slice_in_dim(pack_gather(given), c * HALF_ROWS, HALF_ROWS, axis=0)
    mine = lax.dynamic_update_slice_in_dim(chips_exchange(own_half, True), own_half[None], me, 0)
    gathered = sibling_join(mine, "gather_swap")
    per_chip = [unpack_gather(gathered[j]) for j in range(N_CHIPS)]
    full = {n: jnp.concatenate([pc[n] for pc in per_chip], axis=SHARDED[n][1]) for n in SHARDED}
    full.update({n: given[n] for n in REPLICATED})

    loss, dx, grads = local_step(x[0], loss_target[0], full)

    split = lambda n: jnp.split(grads[n], N_CHIPS, axis=SHARDED[n][1])
    g_all = jnp.stack([pack_shard({n: split(n)[j] for n in SHARDED}) for j in range(N_CHIPS)])
    pair, pair16 = pair_sum(g_all, sibling_give(g_all), c)
    half = chip_sum(chips_exchange(pair16, False), pair, me)
    g_shard = sibling_join(half, "half_exchange")

    g_small = sum_blocks(all_gather_small(pack_small(grads)), "small_sum")

    g_local = unpack_shard(g_shard)
    g_local.update(unpack_small(g_small))
    steps = {n: adamw(given[n], g_local[n], mom1[n], mom2[n], "adamw_" + n) for n in WEIGHTS}
    shaped = lambda n, a: a[None] if n in LAYERED else a
    outs = [[shaped(n, g_local[n]) for n in WEIGHTS]]
    outs += [[shaped(n, steps[n][k]) for n in WEIGHTS] for k in range(3)]

    loss = lax.psum(loss, ("x", "y", "c"))
    return (loss, dx[None], *outs[0], *outs[1], *outs[2], *outs[3])
```

```python
import functools

import jax
import jax.numpy as jnp
from jax import lax
from jax.experimental import pallas as pl
from jax.experimental.pallas import tpu as pltpu

F32 = jnp.float32
BF16 = jnp.bfloat16
SDS = jax.ShapeDtypeStruct
MESH = pl.DeviceIdType.MESH

D_MODEL = 1024
N_META = 16
EPS = 1e-6
BLK = 128
CH = 64
PAD = BLK - N_META
HEADS = 16
HD = 64
KVW = 256
DN_H = 16
DN_KH = 8
DK = 128
SLOPES = [2.0 ** (-8.0 * (h + 1) / HEADS) for h in range(HEADS)]
NEG = -1e30
NT = (((1,), (1,)), ((), ()))
TN = (((0,), (0,)), ((), ()))
HI = lax.Precision.HIGHEST

ADAM_LR, ADAM_B1, ADAM_B2, ADAM_EPS, ADAM_WD, ADAM_STEP = 0.001, 0.9, 0.999, 1e-08, 0.01, 10

VMEM_LIMIT = 56 * 1024 * 1024
MXU_DEPTH = 256
WGRAD_BLOCK_ELEMS = 2 * 1024 * 1024


def _cp(*sem):
    return pltpu.CompilerParams(dimension_semantics=sem, vmem_limit_bytes=VMEM_LIMIT)


def _row_tile(rows):
    for t in (384, 256, 128):
        if rows % t == 0:
            return t
    raise ValueError(rows)


def _dot(a, b, dims=None, precision=None):
    if dims is None:
        return jnp.dot(a, b, preferred_element_type=F32, precision=precision)
    return lax.dot_general(a, b, dims, preferred_element_type=F32, precision=precision)


def _silu(x):
    return x * jax.nn.sigmoid(x)


def _dsilu(x):
    s = jax.nn.sigmoid(x)
    return s * (1.0 + x * (1.0 - s))


def _rms(x):
    return lax.rsqrt(jnp.mean(x * x, axis=-1, keepdims=True) + EPS)


def norm_matmul(h, nw, w, tn, name):
    rows, k = h.shape
    n = w.shape[1]
    tm = _row_tile(rows)

    def norm_body(h_ref, nw_ref, xn_ref):
        x = h_ref[...]
        xn_ref[...] = (x * _rms(x) * nw_ref[...]).astype(BF16)

    xn = pl.pallas_call(
        norm_body, grid=(rows // tm,),
        in_specs=[pl.BlockSpec((tm, k), lambda i: (i, 0)), pl.BlockSpec((1, k), lambda i: (0, 0))],
        out_specs=pl.BlockSpec((tm, k), lambda i: (i, 0)), out_shape=SDS((rows, k), BF16),
        compiler_params=_cp("parallel"), name=name + "_norm")(h, nw)

    def body(a_ref, w_ref, o_ref):
        o_ref[...] = _dot(a_ref[...], w_ref[...])

    out = pl.pallas_call(
        body, grid=(n // tn, rows // tm),
        in_specs=[pl.BlockSpec((tm, k), lambda j, i: (i, 0)), pl.BlockSpec((k, tn), lambda j, i: (0, j))],
        out_specs=pl.BlockSpec((tm, tn), lambda j, i: (i, j)), out_shape=SDS((rows, n), F32),
        compiler_params=_cp("parallel", "parallel"), name=name)(xn, w)
    return out, xn


def matmul_residual(a, w, res, name):
    rows, k = a.shape
    n = w.shape[1]
    tm = _row_tile(rows)

    def body(a_ref, w_ref, r_ref, o_ref):
        o_ref[...] = r_ref[...] + _dot(a_ref[...], w_ref[...])

    return pl.pallas_call(
        body, grid=(rows // tm,),
        in_specs=[pl.BlockSpec((tm, k), lambda i: (i, 0)), pl.BlockSpec((k, n), lambda i: (0, 0)),
                  pl.BlockSpec((tm, n), lambda i: (i, 0))],
        out_specs=pl.BlockSpec((tm, n), lambda i: (i, 0)),
        out_shape=SDS((rows, n), F32), compiler_params=_cp("parallel"), name=name)(a, w, res)


def wgrad(a, b, name):
    rows, k = a.shape
    n = b.shape[1]
    tm = _row_tile(rows)
    tn = min(n, WGRAD_BLOCK_ELEMS // k)

    def body(a_ref, b_ref, o_ref):
        @pl.when(pl.program_id(1) == 0)
        def _():
            o_ref[...] = jnp.zeros_like(o_ref)

        o_ref[...] += _dot(a_ref[...], b_ref[...].astype(BF16), TN)

    return pl.pallas_call(
        body, grid=(n // tn, rows // tm),
        in_specs=[pl.BlockSpec((tm, k), lambda j, i: (i, 0)), pl.BlockSpec((tm, tn), lambda j, i: (i, j))],
        out_specs=pl.BlockSpec((k, tn), lambda j, i: (0, j)),
        out_shape=SDS((k, n), F32), compiler_params=_cp("parallel", "arbitrary"), name=name)(a, b)


def in_proj_bwd(dus, ws, h, nw, dh_next, name):
    rows, k = h.shape
    tm = _row_tile(rows)
    nd = len(dus)
    nt = rows // tm

    def body(*refs):
        du_refs, w_refs = refs[:nd], refs[nd:2 * nd]
        h_ref, nw_ref, dhn_ref, dh_ref, dnw_ref = refs[2 * nd:]
        dxn = _dot(du_refs[0][...].astype(BF16), w_refs[0][...], NT)
        for du_ref, w_ref in zip(du_refs[1:], w_refs[1:]):
            dxn += _dot(du_ref[...].astype(BF16), w_ref[...], NT)
        x = h_ref[...]
        r = _rms(x)
        y = x * r
        gy = dxn * nw_ref[...]
        dh_ref[...] = dhn_ref[...] + r * (gy - y * jnp.mean(y * gy, axis=-1, keepdims=True))
        dnw_ref[0] = jnp.sum(dxn * y, axis=0, keepdims=True)

    in_specs = [pl.BlockSpec((tm, du.shape[1]), lambda i: (i, 0)) for du in dus]
    in_specs += [pl.BlockSpec(w.shape, lambda i: (0, 0)) for w in ws]
    in_specs += [pl.BlockSpec((tm, k), lambda i: (i, 0)), pl.BlockSpec((1, k), lambda i: (0, 0)),
                 pl.BlockSpec((tm, k), lambda i: (i, 0))]
    return pl.pallas_call(
        body, grid=(nt,), in_specs=in_specs,
        out_specs=[pl.BlockSpec((tm, k), lambda i: (i, 0)), pl.BlockSpec((1, 1, k), lambda i: (i, 0, 0))],
        out_shape=[SDS((rows, k), F32), SDS((nt, 1, k), F32)],
        compiler_params=_cp("parallel"), name=name)(*dus, *ws, h, nw, dh_next)


def matmul_nt(a, w, name):
    rows, k = a.shape
    n = w.shape[0]
    tm = _row_tile(rows)

    def body(a_ref, w_ref, o_ref):
        o_ref[...] = _dot(a_ref[...].astype(BF16), w_ref[...], NT)

    return pl.pallas_call(
        body, grid=(rows // tm,),
        in_specs=[pl.BlockSpec((tm, k), lambda i: (i, 0)), pl.BlockSpec((n, k), lambda i: (0, 0))],
        out_specs=pl.BlockSpec((tm, n), lambda i: (i, 0)),
        out_shape=SDS((rows, n), F32), compiler_params=_cp("parallel"), name=name)(a, w)


SUB = 64
GRP = 8
TR = GRP * SUB
NBAND = 192
TK = 256


def _tile_bias(n, sb):
    r = lax.broadcasted_iota(jnp.int32, (TR, TK), 0)
    c = lax.broadcasted_iota(jnp.int32, (TR, TK), 1)
    qi = r & (SUB - 1)
    d = BLK + qi - c
    dm = n * BLK + SUB * sb - PAD + NBAND + qi - c
    band = c < NBAND
    valid = (band & (d >= 0) & (d < BLK) & (c >= 2 * BLK - BLK * n - SUB * sb)) | (
        (c >= NBAND) & (c < NBAND + N_META) & (dm >= 0))
    return valid, jnp.where(band, d, jnp.minimum(dm, BLK)).astype(F32)


def _group_col(vals):
    g = lax.broadcasted_iota(jnp.int32, (TR, 1), 0) >> 6
    col = jnp.zeros((TR, 1), F32)
    for gi, v in enumerate(vals):
        col = jnp.where(g == gi, v, col)
    return col


def _stack_heads(ref, sb, kvh):
    return jnp.concatenate(
        [ref[SUB * sb:SUB * sb + SUB, HD * (GRP * kvh + g):HD * (GRP * kvh + g) + HD] for g in range(GRP)], axis=0)


def _unstack_heads(parts):
    return jnp.concatenate([parts[kvh][SUB * g:SUB * g + SUB] for kvh in range(2) for g in range(GRP)], axis=1)


def _tile_keys(band, meta, sb):
    return jnp.concatenate([band[SUB * sb:SUB * sb + NBAND], meta,
                            jnp.zeros((TK - NBAND - N_META, HD), band.dtype)], axis=0)


def _row_sums(x):
    ones = jnp.ones((x.shape[1], 128), BF16)
    hi = x.astype(BF16)
    lo = (x - hi.astype(F32)).astype(BF16)
    return _dot(hi, ones) + _dot(lo, ones)


def _rms_stack(q):
    return lax.rsqrt(_row_sums(q * q)[:, :HD] * (1.0 / HD) + EPS)


def _fill_bias(bias_scr, n):
    @pl.when(n <= 2)
    def _():
        for sb in range(2):
            valid, dist = _tile_bias(n, sb)
            for kvh in range(2):
                slope_col = _group_col([SLOPES[GRP * kvh + g] for g in range(GRP)])
                bias_scr[2 * sb + kvh] = jnp.where(valid, -slope_col * dist, NEG)


def _tile_vals(band, meta, sb):
    return jnp.concatenate([_tile_keys(band, meta, sb), jnp.ones((TK, 3 * HD), BF16)], axis=1)


def _tile_softmax(qn16, k16, vx16, bias, sink_col):
    s = _dot(qn16, k16, NT) * (HD ** -0.5) + bias
    mx = jnp.maximum(jnp.max(s.astype(BF16), axis=-1, keepdims=True).astype(F32), sink_col)
    e = jnp.exp(s - mx)
    es = jnp.exp(sink_col - mx)
    ox = _dot(e.astype(BF16), vx16)
    return e, 1.0 / (ox[:, 2 * HD:] + es), es, ox[:, :HD]


def _kv_heads(kvb, kvm, kw_):
    out = []
    for kvh in range(2):
        kb, km = kvb[:, HD * kvh:HD * kvh + HD], kvm[:, HD * kvh:HD * kvh + HD]
        out.append(((kb * _rms(kb) * kw_).astype(BF16), (km * _rms(km) * kw_).astype(BF16),
                    kvb[:, BLK + HD * kvh:BLK + HD * kvh + HD].astype(BF16),
                    kvm[:, BLK + HD * kvh:BLK + HD * kvh + HD].astype(BF16)))
    return out


def _kv_specs(nblk, clamp):
    cur = (lambda n: (jnp.minimum(n, nblk - 1), 8)) if clamp else (lambda n: (n, 8))
    return [pl.BlockSpec((BLK, KVW), cur),
            pl.BlockSpec((BLK, KVW), lambda n: (jnp.maximum(n - 1, 0), 8)),
            pl.BlockSpec((N_META, KVW), lambda n: (PAD // N_META, 8))]


def _sink_cols(sinks):
    return jnp.repeat(sinks.reshape(2, GRP), SUB, axis=1).reshape(2, TR, 1)


SINK_SPEC = pl.BlockSpec((2, TR, 1), lambda n: (0, 0, 0))


def attn_fwd(u, qw, kw, sinks):
    rows = u.shape[0]
    nblk = rows // BLK

    def body(q_ref, g_ref, kvc_ref, kvp_ref, kvm_ref, qw_ref, kw_ref, sc_ref, og_ref, bias_scr):
        _fill_bias(bias_scr, pl.program_id(0))
        qw_ = qw_ref[...]
        kv = _kv_heads(jnp.concatenate([kvp_ref[...], kvc_ref[...]], axis=0), kvm_ref[...], kw_ref[...])
        for sb in range(2):
            parts = []
            for kvh in range(2):
                knb, knm, vb, vm = kv[kvh]
                q = _stack_heads(q_ref, sb, kvh)
                qn16 = (q * _rms_stack(q) * qw_).astype(BF16)
                _, inv, _, o = _tile_softmax(qn16, _tile_keys(knb, knm, sb), _tile_vals(vb, vm, sb),
                                             bias_scr[2 * sb + kvh], sc_ref[kvh])
                parts.append(o * inv[:, :HD])
            rows = slice(SUB * sb, SUB * sb + SUB)
            og_ref[rows, :] = (_unstack_heads(parts) * _silu(g_ref[rows, :])).astype(BF16)

    small = lambda w: pl.BlockSpec((1, w), lambda n: (0, 0))
    return pl.pallas_call(
        body, grid=(nblk,),
        in_specs=[pl.BlockSpec((BLK, 1024), lambda n: (n, 0)), pl.BlockSpec((BLK, 1024), lambda n: (n, 1))]
        + _kv_specs(nblk, False) + [small(HD), small(HD), SINK_SPEC],
        out_specs=pl.BlockSpec((BLK, 1024), lambda n: (n, 0)),
        out_shape=SDS((rows, 1024), BF16), scratch_shapes=[pltpu.VMEM((4, TR, TK), F32)],
        compiler_params=_cp("arbitrary"), name="attn_fwd")(u, u, u, u, u, qw, kw, _sink_cols(sinks))


def attn_bwd(u, qw, kw, sinks, dog):
    rows = u.shape[0]
    nblk = rows // BLK

    def knorm_bwd(k, dkn, kw_):
        r = _rms(k)
        y = k * r
        gy = dkn * kw_
        return r * (gy - y * jnp.mean(y * gy, axis=-1, keepdims=True)), jnp.sum(dkn * y, axis=0, keepdims=True)

    def body(q_ref, g_ref, dog_ref, kvc_ref, kvp_ref, kvm_ref, qw_ref, kw_ref, sc_ref,
             dq_ref, dg_ref, dkv_ref, dkvm_ref, dqw_ref, dkw_ref, dsk_ref, carry, prevp, curp, metap, bias_scr):
        n = pl.program_id(0)
        qw_, kw_ = qw_ref[...], kw_ref[...]
        _fill_bias(bias_scr, n)

        @pl.when(n == 0)
        def _():
            carry[...] = jnp.zeros_like(carry)
            metap[...] = jnp.zeros_like(metap)
            dqw_ref[...] = jnp.zeros_like(dqw_ref)
            dkw_ref[...] = jnp.zeros_like(dkw_ref)
            dsk_ref[...] = jnp.zeros_like(dsk_ref)

        @pl.when(n == nblk)
        def _():
            prevp[...] = jnp.zeros_like(prevp)
            curp[...] = jnp.zeros_like(curp)

        @pl.when(n < nblk)
        def _():
            kv = _kv_heads(jnp.concatenate([kvp_ref[...], kvc_ref[...]], axis=0), kvm_ref[...], kw_)
            lane = lax.broadcasted_iota(jnp.int32, (1, HEADS), 1)
            dqw = jnp.zeros((1, HD), F32)
            dsk = jnp.zeros((1, HEADS), F32)
            band_parts = [jnp.zeros((2 * BLK, HD), F32) for _ in range(4)]
            meta_parts = [jnp.zeros((N_META, HD), F32) for _ in range(4)]

            def widen(x, sb):
                z = jnp.zeros((2 * BLK - NBAND, HD), F32)
                return jnp.concatenate([x, z] if sb == 0 else [z, x], axis=0)

            for sb in range(2):
                rows = slice(SUB * sb, SUB * sb + SUB)
                dq_parts, dg_parts = [], []
                for kvh in range(2):
                    knb, knm, vb, vm = kv[kvh]
                    k16, v16 = _tile_keys(knb, knm, sb), _tile_keys(vb, vm, sb)
                    q = _stack_heads(q_ref, sb, kvh)
                    r = _rms_stack(q)
                    y = q * r
                    qn16 = (y * qw_).astype(BF16)
                    e, inv, es, o = _tile_softmax(qn16, k16, _tile_vals(vb, vm, sb), bias_scr[2 * sb + kvh],
                                                  sc_ref[kvh])
                    p = e * jnp.concatenate([inv, inv], axis=1)
                    p16 = p.astype(BF16)
                    o = o * inv[:, :HD]
                    gate = _stack_heads(g_ref, sb, kvh)
                    dog_ = _stack_heads(dog_ref, sb, kvh)
                    dg_parts.append(dog_ * o * _dsilu(gate))
                    do_ = dog_ * _silu(gate)
                    do16 = do_.astype(BF16)
                    dp = _dot(do16, v16, NT)
                    delta = _row_sums(do_ * o)
                    ds16 = (p * (dp - jnp.concatenate([delta, delta], axis=1))).astype(BF16)
                    dsink = -(es * inv) * delta
                    for g in range(GRP):
                        dsk += jnp.where(lane == GRP * kvh + g,
                                         jnp.sum(dsink[SUB * g:SUB * g + SUB, :HEADS], axis=0, keepdims=True), 0.0)
                    dqn = _dot(ds16, k16) * (HD ** -0.5)
                    dk = (_dot((y * qw_).T.astype(BF16), ds16) * (HD ** -0.5)).T
                    dv = _dot(do_.T.astype(BF16), p16).T
                    band_parts[kvh] += widen(dk[:NBAND], sb)
                    band_parts[2 + kvh] += widen(dv[:NBAND], sb)
                    meta_parts[kvh] += dk[NBAND:NBAND + N_META]
                    meta_parts[2 + kvh] += dv[NBAND:NBAND + N_META]
                    gy = dqn * qw_
                    dq_parts.append(r * (gy - y * (_row_sums(y * gy)[:, :HD] * (1.0 / HD))))
                    dqw += jnp.sum(dqn * y, axis=0, keepdims=True)
                dq_ref[rows, :] = _unstack_heads(dq_parts).astype(BF16)
                dg_ref[rows, :] = _unstack_heads(dg_parts).astype(BF16)
            band = jnp.concatenate(band_parts, axis=1)
            prevp[...] = band[:BLK]
            curp[...] = band[BLK:]
            metap[...] += jnp.concatenate(meta_parts, axis=1)
            dqw_ref[...] += dqw
            dsk_ref[...] += dsk

        tot = carry[...] + prevp[...]
        kprev = kvp_ref[...]
        dk0, w0 = knorm_bwd(kprev[:, 0:HD], tot[:, 0:HD], kw_)
        dk1, w1 = knorm_bwd(kprev[:, HD:2 * HD], tot[:, HD:2 * HD], kw_)
        dkv_ref[...] = jnp.concatenate([dk0, dk1, tot[:, 2 * HD:]], axis=1)
        dkw_ref[...] += w0 + w1
        carry[...] = curp[...]

        @pl.when(n == nblk)
        def _():
            mt = metap[...]
            km = kvm_ref[...]
            m0, v0 = knorm_bwd(km[:, 0:HD], mt[:, 0:HD], kw_)
            m1, v1 = knorm_bwd(km[:, HD:2 * HD], mt[:, HD:2 * HD], kw_)
            dkvm_ref[...] = jnp.concatenate([m0, m1, mt[:, 2 * HD:]], axis=1)
            dkw_ref[...] += v0 + v1

    small = lambda w: pl.BlockSpec((1, w), lambda n: (0, 0))
    cl = lambda n: jnp.minimum(n, nblk - 1)
    return pl.pallas_call(
        body, grid=(nblk + 1,),
        in_specs=[pl.BlockSpec((BLK, 1024), lambda n: (cl(n), 0)), pl.BlockSpec((BLK, 1024), lambda n: (cl(n), 1)),
                  pl.BlockSpec((BLK, 1024), lambda n: (cl(n), 0))]
        + _kv_specs(nblk, True) + [small(HD), small(HD), SINK_SPEC],
        out_specs=[pl.BlockSpec((BLK, 1024), lambda n: (cl(n), 0)), pl.BlockSpec((BLK, 1024), lambda n: (cl(n), 0)),
                   pl.BlockSpec((BLK, KVW), lambda n: (jnp.maximum(n - 1, 0), 0)),
                   pl.BlockSpec((N_META, KVW), lambda n: (0, 0)), small(HD), small(HD), small(HEADS)],
        out_shape=[SDS((rows, 1024), BF16), SDS((rows, 1024), BF16), SDS((rows, KVW), F32), SDS((N_META, KVW), F32),
                   SDS((1, HD), F32), SDS((1, HD), F32), SDS((1, HEADS), F32)],
        scratch_shapes=[pltpu.VMEM((BLK, KVW), F32), pltpu.VMEM((BLK, KVW), F32), pltpu.VMEM((BLK, KVW), F32),
                        pltpu.VMEM((N_META, KVW), F32), pltpu.VMEM((4, TR, TK), F32)],
        compiler_params=_cp("arbitrary"), name="attn_bwd")(u, u, dog, u, u, u, qw, kw, _sink_cols(sinks))


GROUP_UNROLL = 8
HB = 16


def _bdot(a, b, kind, split=False, fused=False):
    dims = {"nn": ((2,), (1,)), "nt": ((2,), (2,)), "tn": ((1,), (1,))}[kind]
    dg = lambda p, q: lax.dot_general(p, q, (dims, ((0,), (0,))), preferred_element_type=F32)
    if not split:
        return dg(a, b)
    if fused:
        assert kind == "nn" and 3 * a.shape[2] <= MXU_DEPTH
        return _dot3(_split(a), _split(b))
    ah, bh = a.astype(BF16), b.astype(BF16)
    al, bl = (a - ah.astype(F32)).astype(BF16), (b - bh.astype(F32)).astype(BF16)
    return (dg(ah, bl) + dg(al, bh)) + dg(ah, bh)


def _head_cols(hv, beta, gc, gct, lane):
    sel = lane == hv
    return _pick(beta, sel), _pick(gc, sel), gct[pl.ds(hv, 1), :]


def _conv_group(xc_ref, xp_ref, cw_ref, off, first):
    xp = jnp.where(first, 0.0, xp_ref[:, pl.ds(off, DK)])
    xx = jnp.concatenate([xp, xc_ref[:, pl.ds(off, DK)]], axis=0)
    y = cw_ref[0:1, pl.ds(off, DK)] * xx[5:5 + CH]
    for j in range(1, 4):
        y += cw_ref[j:j + 1, pl.ds(off, DK)] * xx[5 + j:5 + j + CH]
    return xx, y


def _gates(ba, al, dtb, c):
    row = c * CH + lax.broadcasted_iota(jnp.int32, (CH, DN_H), 0)
    real = row >= PAD
    xa = ba[:, DN_H:2 * DN_H] + dtb
    beta = jnp.where(real, jax.nn.sigmoid(ba[:, 0:DN_H]), 0.0)
    g = jnp.where(real, -jnp.exp(al) * jax.nn.softplus(xa), 0.0)
    return real, xa, beta, g


def _pick(x, sel):
    return jnp.sum(jnp.where(sel, x, 0.0), axis=1, keepdims=True)


def _chunk_specs(width_blocks):
    return [pl.BlockSpec((CH, 4096), lambda c: (c, 0)),
            pl.BlockSpec((8, 4096), lambda c: (jnp.maximum(8 * c - 1, 0), 0)),
            pl.BlockSpec((CH, DK), lambda c: (c, 48))]


def _tri_inv(m, ii, jj):
    eye = (ii == jj).astype(BF16)
    mh, ml = _split(m)
    blk8 = (ii >> 3) == (jj >> 3)
    mb = (jnp.where(blk8, mh, 0), jnp.where(blk8, ml, 0))
    m2 = _split(_dot3(mb, mb))
    m4 = _split(_dot3(m2, m2))
    x = _dot3(_split(_dot3((eye - mb[0], -mb[1]), (eye + m2[0], m2[1]))), (eye + m4[0], m4[1]))
    for sh in (3, 4, 5):
        off = ((ii >> (sh + 1)) == (jj >> (sh + 1))) & ((ii >> sh) != (jj >> sh))
        xs = _split(x)
        x = x - _dot3(_split(_dot3(xs, (jnp.where(off, mh, 0), jnp.where(off, ml, 0)))), xs)
    return x


def _split(x):
    hi = x.astype(BF16)
    return hi, (x - hi.astype(F32)).astype(BF16)


def _dot3(a, b):
    lhs = jnp.concatenate([a[0], a[1], a[0]], axis=2)
    rhs = jnp.concatenate([b[0], b[0], b[1]], axis=1)
    return lax.dot_general(lhs, rhs, (((2,), (1,)), ((0,), (0,))), preferred_element_type=F32)


def dn_prep(udn, conv_w, a_log, dt_bias):
    rows = udn.shape[0]
    nch = rows // CH

    def body(xc_ref, xp_ref, ba_ref, cw_ref, al_ref, dtb_ref,
             qn_ref, kn_ref, sv_ref, gc_ref, beta_ref, u_ref, w_ref, qe_ref, ks_ref, p_ref, at_ref, pt_ref,
             qet_ref, wt_ref, kst_ref, y_ref, gct):
        c = pl.program_id(0)
        first = c == 0
        _, _, beta, g = _gates(ba_ref[...], al_ref[...], dtb_ref[...], c)
        ii = lax.broadcasted_iota(jnp.int32, (CH, CH), 0)
        jj = lax.broadcasted_iota(jnp.int32, (CH, CH), 1)
        gc = _dot((ii >= jj).astype(F32), g, precision=HI)
        gc_ref[...] = gc
        beta_ref[...] = beta
        gct[...] = gc.T

        def qk_body(kh, carry):
            off = pl.multiple_of(kh * DK, DK)
            _, yq = _conv_group(xc_ref, xp_ref, cw_ref, off, first)
            y_ref[:, pl.ds(off, DK)] = yq
            sq = _silu(yq)
            qn_ref[:, pl.ds(off, DK)] = sq * lax.rsqrt(jnp.sum(sq * sq, axis=-1, keepdims=True) + EPS) * (DK ** -0.5)
            _, yk = _conv_group(xc_ref, xp_ref, cw_ref, pl.multiple_of(1024 + kh * DK, DK), first)
            y_ref[:, pl.ds(pl.multiple_of(1024 + kh * DK, DK), DK)] = yk
            sk = _silu(yk)
            kn_ref[:, pl.ds(off, DK)] = sk * lax.rsqrt(jnp.sum(sk * sk, axis=-1, keepdims=True) + EPS)
            return carry

        lax.fori_loop(0, DN_KH, qk_body, 0, unroll=GROUP_UNROLL)
        lane = lax.broadcasted_iota(jnp.int32, (CH, DN_H), 1)
        zpad = jnp.zeros((CH, DK - CH), F32)

        def v_group(grp, carry):
            offs, ks_, qs_, vs_, cols = [], [], [], [], []
            for i in range(HB):
                hv = grp * HB + i
                offs.append(pl.multiple_of(hv * DK, DK))
                koff = pl.multiple_of((grp * (HB // 2) + i // 2) * DK, DK)
                _, yv = _conv_group(xc_ref, xp_ref, cw_ref, pl.multiple_of(2048 + hv * DK, DK), first)
                y_ref[:, pl.ds(pl.multiple_of(2048 + hv * DK, DK), DK)] = yv
                vs_.append(_silu(yv))
                sv_ref[:, pl.ds(offs[i], DK)] = vs_[i]
                ks_.append(kn_ref[:, pl.ds(koff, DK)])
                qs_.append(qn_ref[:, pl.ds(koff, DK)])
                cols.append(_head_cols(hv, beta, gc, gct, lane))
            k, q, v = jnp.stack(ks_), jnp.stack(qs_), jnp.stack(vs_)
            beta_c, gc_c, gc_r = (jnp.stack([c_[j] for c_ in cols]) for j in range(3))
            dec = jnp.exp(jnp.where(ii >= jj, gc_c - gc_r, NEG))
            eg = jnp.exp(gc_c)
            kb = k * beta_c
            k16 = k.astype(BF16)
            m = jnp.where(ii > jj, _bdot(kb.astype(BF16), k16, "nt") * dec, 0.0)
            a = _tri_inv(m, ii, jj)
            uw = _bdot(a, jnp.concatenate([v * beta_c, kb * eg], axis=2), "nn", True, True)
            p = _bdot(q.astype(BF16), k16, "nt") * dec
            qe = q * eg
            ksx = k * jnp.exp(gc_c[:, CH - 1:CH, :] - gc_c)
            tslot = lambda x: jnp.concatenate([x.T, jnp.zeros((DK, DK - CH), F32)], axis=1).astype(BF16)
            for i in range(HB):
                sl = pl.ds(offs[i], DK)
                u_ref[:, sl] = uw[i, :, :DK]
                w_ref[:, sl] = uw[i, :, DK:]
                qe_ref[:, sl] = qe[i].astype(BF16)
                ks_ref[:, sl] = ksx[i].astype(BF16)
                p_ref[:, sl] = jnp.concatenate([p[i], zpad], axis=1).astype(BF16)
                at_ref[:, sl] = jnp.concatenate([a[i].T, zpad], axis=1)
                pt_ref[:, sl] = jnp.concatenate([p[i].T, zpad], axis=1).astype(BF16)
                qet_ref[:, sl] = tslot(qe[i])
                wt_ref[:, sl] = tslot(uw[i, :, DK:])
                kst_ref[:, sl] = tslot(ksx[i])
            return carry

        lax.fori_loop(0, DN_H // HB, v_group, 0)

    full = lambda shape: pl.BlockSpec(shape, lambda c: (0, 0))
    blk = lambda w: pl.BlockSpec((CH, w), lambda c: (c, 0))
    return pl.pallas_call(
        body, grid=(nch,),
        in_specs=_chunk_specs(0) + [full((4, 4096)), full((1, DN_H)), full((1, DN_H))],
        out_specs=[blk(1024), blk(1024), blk(2048), blk(DN_H), blk(DN_H), blk(2048), blk(2048), blk(2048), blk(2048),
                   blk(2048), blk(2048), blk(2048)] + [pl.BlockSpec((DK, 2048), lambda c: (c, 0))] * 3 + [blk(4096)],
        out_shape=[SDS((rows, 1024), F32), SDS((rows, 1024), F32), SDS((rows, 2048), F32), SDS((rows, DN_H), F32),
                   SDS((rows, DN_H), F32), SDS((rows, 2048), F32), SDS((rows, 2048), F32), SDS((rows, 2048), BF16),
                   SDS((rows, 2048), BF16), SDS((rows, 2048), BF16), SDS((rows, 2048), F32),
                   SDS((rows, 2048), BF16)] + [SDS((2 * rows, 2048), BF16)] * 3 + [SDS((rows, 4096), F32)],
        scratch_shapes=[pltpu.VMEM((DN_H, CH), F32)],
        compiler_params=_cp("parallel"), name="dn_prep")(udn, udn, udn, conv_w, a_log, dt_bias)


SCAN_BUFS = 3


def dn_scan(u, w, qe, kst, p, gc):
    rows = u.shape[0]
    nch = rows // CH

    def body(u_hbm, w_hbm, qe_hbm, kst_hbm, p_hbm, gc_ref, o_ref, vn_ref, st_hbm, s_scr,
             u_buf, w_buf, qe_buf, kst_buf, p_buf, sems, st_buf, st_sems):
        c = pl.program_id(0)
        streams = [(u_hbm, u_buf, CH), (w_hbm, w_buf, CH), (qe_hbm, qe_buf, CH), (kst_hbm, kst_buf, DK),
                   (p_hbm, p_buf, CH)]

        def fetch(k, chunk, slot):
            hbm, buf, r = streams[k]
            return pltpu.make_async_copy(hbm.at[pl.ds(pl.multiple_of(chunk * r, r), r), :], buf.at[slot],
                                         sems.at[k, slot])

        @pl.when(c == 0)
        def _():
            s_scr[...] = jnp.zeros_like(s_scr)
            for k in range(len(streams)):
                for ahead in range(SCAN_BUFS - 1):
                    fetch(k, ahead, ahead).start()

        @pl.when(c + SCAN_BUFS - 1 < nch)
        def _():
            for k in range(len(streams)):
                fetch(k, c + SCAN_BUFS - 1, (c + SCAN_BUFS - 1) % SCAN_BUFS).start()

        slot = c % SCAN_BUFS
        for k in range(len(streams)):
            fetch(k, c, slot).wait()

        def st_copy(chunk, sl_):
            return pltpu.make_async_copy(st_buf.at[sl_], st_hbm.at[chunk], st_sems.at[sl_])

        @pl.when(c >= SCAN_BUFS)
        def _():
            st_copy(c - SCAN_BUFS, slot).wait()
        u_ref, w_ref, qe_ref, kst_ref, p_ref = (buf.at[slot] for _, buf, _ in streams)
        gl_row = gc_ref[CH - 1:CH, :]
        lane = lax.broadcasted_iota(jnp.int32, (1, DN_H), 1)

        def group(grp, carry):
            base = grp * HB
            sls = [pl.ds(pl.multiple_of((base + i) * DK, DK), DK) for i in range(HB)]
            heads = lambda ref: jnp.stack([ref[:, sl] for sl in sls])
            s = s_scr[pl.ds(base, HB)]
            st_buf[slot, pl.ds(base, HB)] = s
            s16 = s.astype(BF16)
            vn = heads(u_ref) - _bdot(heads(w_ref).astype(BF16), s16, "nn")
            vn16 = vn.astype(BF16)
            o = _bdot(heads(qe_ref), s16, "nn") + _bdot(heads(p_ref)[:, :, 0:CH], vn16, "nn")
            egl = jnp.exp(jnp.stack([_pick(gl_row, lane == base + i) for i in range(HB)]))
            s_scr[pl.ds(base, HB)] = s * egl + _bdot(heads(kst_ref)[:, :, 0:CH], vn16, "nn")
            for i in range(HB):
                vn_ref[:, sls[i]] = vn16[i]
                o_ref[:, sls[i]] = o[i]
            return carry

        lax.fori_loop(0, DN_H // HB, group, 0)
        st_copy(c, slot).start()

        @pl.when(c == nch - 1)
        def _():
            for back in range(SCAN_BUFS):
                st_copy(c - back, (c - back) % SCAN_BUFS).wait()

    blk = lambda wd: pl.BlockSpec((CH, wd), lambda c: (c, 0))
    assert nch >= SCAN_BUFS
    return pl.pallas_call(
        body, grid=(nch,),
        in_specs=[ANY] * 5 + [blk(DN_H)],
        out_specs=[blk(2048), blk(2048), ANY],
        out_shape=[SDS((rows, 2048), F32), SDS((rows, 2048), BF16), SDS((nch, DN_H, DK, DK), F32)],
        scratch_shapes=[pltpu.VMEM((DN_H, DK, DK), F32), pltpu.VMEM((SCAN_BUFS, CH, 2048), F32),
                        pltpu.VMEM((SCAN_BUFS, CH, 2048), F32), pltpu.VMEM((SCAN_BUFS, CH, 2048), BF16),
                        pltpu.VMEM((SCAN_BUFS, DK, 2048), BF16), pltpu.VMEM((SCAN_BUFS, CH, 2048), BF16),
                        pltpu.SemaphoreType.DMA((5, SCAN_BUFS)), pltpu.VMEM((SCAN_BUFS, DN_H, DK, DK), F32),
                        pltpu.SemaphoreType.DMA((SCAN_BUFS,))],
        compiler_params=_cp("arbitrary"), name="dn_scan")(u, w, qe, kst, p, gc)


def dn_out_fwd(o, udn, ow, wout, h1, tgt):
    rows = o.shape[0]
    tm = _row_tile(rows)
    nt = rows // tm

    def body(o_ref, z_ref, ow_ref, w_ref, h_ref, t_ref, dh_ref, on_ref, ls_ref):
        for hv in range(DN_H):
            sl = slice(hv * DK, hv * DK + DK)
            oh = o_ref[:, sl]
            on_ref[:, sl] = (oh * _rms(oh) * ow_ref[...] * _silu(z_ref[:, sl])).astype(BF16)
        h2 = h_ref[...] + _dot(on_ref[...], w_ref[...])
        row = pl.program_id(0) * tm + lax.broadcasted_iota(jnp.int32, (tm, 1), 0)
        err = jnp.where(row >= BLK, h2 - t_ref[...], 0.0)
        dh_ref[...] = err * (1.0 / D_MODEL)
        ls_ref[0] = jnp.sum(err * err, axis=0, keepdims=True)

    return pl.pallas_call(
        body, grid=(nt,),
        in_specs=[pl.BlockSpec((tm, 2048), lambda i: (i, 0)), pl.BlockSpec((tm, 2048), lambda i: (i, 2)),
                  pl.BlockSpec((1, DK), lambda i: (0, 0)), pl.BlockSpec((2048, D_MODEL), lambda i: (0, 0)),
                  pl.BlockSpec((tm, D_MODEL), lambda i: (i, 0)), pl.BlockSpec((tm, D_MODEL), lambda i: (i, 0))],
        out_specs=[pl.BlockSpec((tm, D_MODEL), lambda i: (i, 0)), pl.BlockSpec((tm, 2048), lambda i: (i, 0)),
                   pl.BlockSpec((1, 1, D_MODEL), lambda i: (i, 0, 0))],
        out_shape=[SDS((rows, D_MODEL), F32), SDS((rows, 2048), BF16), SDS((nt, 1, D_MODEL), F32)],
        compiler_params=_cp("parallel"), name="dn_out_fwd")(o, udn, ow, wout, h1, tgt)


def dn_out_bwd(dh2, wout, o, udn, ow):
    rows = o.shape[0]
    tm = _row_tile(rows)
    nt = rows // tm

    def body(dh_ref, w_ref, o_ref, z_ref, ow_ref, do_ref, dz_ref, dow_ref):
        don = _dot(dh_ref[...].astype(BF16), w_ref[...], NT)
        ow_ = ow_ref[...]
        dow = jnp.zeros((1, DK), F32)
        for hv in range(DN_H):
            sl = slice(hv * DK, hv * DK + DK)
            oh = o_ref[:, sl]
            r = _rms(oh)
            y = oh * r
            z = z_ref[:, sl]
            dn = don[:, sl] * _silu(z)
            dz_ref[:, sl] = (don[:, sl] * (y * ow_) * _dsilu(z)).astype(BF16)
            dy = dn * ow_
            do_ref[:, sl] = r * (dy - y * jnp.mean(y * dy, axis=-1, keepdims=True))
            dow += jnp.sum(dn * y, axis=0, keepdims=True)
        dow_ref[0] = dow

    return pl.pallas_call(
        body, grid=(nt,),
        in_specs=[pl.BlockSpec((tm, D_MODEL), lambda i: (i, 0)), pl.BlockSpec((2048, D_MODEL), lambda i: (0, 0)),
                  pl.BlockSpec((tm, 2048), lambda i: (i, 0)), pl.BlockSpec((tm, 2048), lambda i: (i, 2)),
                  pl.BlockSpec((1, DK), lambda i: (0, 0))],
        out_specs=[pl.BlockSpec((tm, 2048), lambda i: (i, 0)), pl.BlockSpec((tm, 2048), lambda i: (i, 0)),
                   pl.BlockSpec((1, 1, DK), lambda i: (i, 0, 0))],
        out_shape=[SDS((rows, 2048), F32), SDS((rows, 2048), BF16), SDS((nt, 1, DK), F32)],
        compiler_params=_cp("parallel"), name="dn_out_bwd")(dh2, wout, o, udn, ow)


def dn_scan_bwd(do, qn, kn, sv, gc, beta, at, pt, u, w, vn, qet, wt, ks, st):
    rows = do.shape[0]
    nch = rows // CH

    def body(do_ref, q_ref, k_ref, v_ref, gc_ref, beta_ref, at_ref, pt_ref, u_ref, w_ref, vn_ref, qet_ref, wt_ref,
             ks_ref, st_ref, dq_ref, dk_ref, dv_ref, dbeta_ref, dg_ref, ds_scr, gct):
        @pl.when(pl.program_id(0) == 0)
        def _():
            ds_scr[...] = jnp.zeros_like(ds_scr)

        gc, beta = gc_ref[...], beta_ref[...]
        gct[...] = gc.T
        ii = lax.broadcasted_iota(jnp.int32, (CH, CH), 0)
        jj = lax.broadcasted_iota(jnp.int32, (CH, CH), 1)
        lane = lax.broadcasted_iota(jnp.int32, (CH, DN_H), 1)
        last = lax.broadcasted_iota(jnp.int32, (CH, 1), 0) == CH - 1

        def group(grp, carry):
            dbeta_acc, dgc_acc = carry
            base = grp * HB
            sls = [pl.ds(pl.multiple_of((base + i) * DK, DK), DK) for i in range(HB)]
            ksls = [pl.ds(pl.multiple_of((grp * (HB // 2) + j) * DK, DK), DK) for j in range(HB // 2)]
            heads = lambda ref: jnp.stack([ref[:, sl] for sl in sls])
            kheads = lambda ref: jnp.stack([ref[:, ksls[i // 2]] for i in range(HB)])
            cols = [_head_cols(base + i, beta, gc, gct, lane) for i in range(HB)]
            beta_c, gc_c, gc_r = (jnp.stack([c_[j] for c_ in cols]) for j in range(3))
            k, q, v = kheads(k_ref), kheads(q_ref), heads(v_ref)
            dec = jnp.exp(jnp.where(ii >= jj, gc_c - gc_r, NEG))
            eg = jnp.exp(gc_c)
            gl = gc_c[:, CH - 1:CH, :]
            e2 = jnp.exp(gl - gc_c)
            egl = jnp.exp(gl)
            k16, q16 = k.astype(BF16), q.astype(BF16)
            do16 = heads(do_ref).astype(BF16)
            s = st_ref[0, pl.ds(base, HB)]
            s16 = s.astype(BF16)
            dso = ds_scr[pl.ds(base, HB)]
            dso16 = dso.astype(BF16)
            wf, uf, vn16 = heads(w_ref), heads(u_ref), heads(vn_ref)
            kb = k * beta_c
            kb16 = kb.astype(BF16)
            pm = _bdot(q16, k16, "nt") * dec
            m = jnp.where(ii > jj, _bdot(kb16, k16, "nt") * dec, 0.0)
            dvn = _bdot(heads(pt_ref)[:, :, 0:CH], do16, "nn") + _bdot(heads(ks_ref), dso16, "nn")
            dvn16 = dvn.astype(BF16)
            ds_scr[pl.ds(base, HB)] = (egl * dso + _bdot(heads(qet_ref)[:, :, 0:CH], do16, "nn")
                                       - _bdot(heads(wt_ref)[:, :, 0:CH], dvn16, "nn"))
            dpm = jnp.where(ii >= jj, _bdot(do16, vn16, "nt"), 0.0)
            dqk16 = (dpm * dec).astype(BF16)
            dqe = _bdot(do16, s16, "nt")
            dq = eg * dqe + _bdot(dqk16, k16, "nn")
            dks = _bdot(vn16, dso16, "nt")
            dw = -_bdot(dvn16, s16, "nt")
            dbvk = _bdot(heads(at_ref)[:, :, 0:CH], jnp.concatenate([dvn, dw], axis=2), "nn", True)
            dbv, dbk = dbvk[:, :, :DK], dbvk[:, :, DK:]
            dm = jnp.where(ii > jj, -_bdot(dbvk, jnp.concatenate([uf, wf], axis=2), "nt", True), 0.0)
            g16 = (dm * dec).astype(BF16)
            dkb = _bdot(g16, k16, "nn")
            dk = (_bdot(dqk16, q16, "tn") + e2 * dks + _bdot(g16, kb16, "tn") + beta_c * (eg * dbk + dkb))
            e = dpm * pm + dm * m
            rsum = lambda x: jnp.sum(x, axis=2, keepdims=True)
            r_bk, r_qe, r_beta, r_ks = rsum(dbk * k), rsum(q * dqe), rsum(dbv * v + dkb * k), rsum(dks * k)
            t = r_ks * e2
            dgl = jnp.sum(t, axis=1, keepdims=True) + egl * rsum(jnp.sum(dso * s, axis=1, keepdims=True))
            deg = r_qe + beta_c * r_bk
            dgc = rsum(e) - t + deg * eg + jnp.where(last, dgl, 0.0)
            dgrow = -jnp.sum(e, axis=1, keepdims=True)
            dv = beta_c * dbv
            dbeta = r_beta + eg * r_bk
            for i in range(HB):
                dv_ref[:, sls[i]] = dv[i]
                sel = lane == base + i
                dbeta_acc = jnp.where(sel, dbeta[i], dbeta_acc)
                dgc_acc = jnp.where(sel, dgc[i], dgc_acc)
                gct[pl.ds(base + i, 1), :] = dgrow[i]
            for j in range(HB // 2):
                dq_ref[:, ksls[j]] = dq[2 * j] + dq[2 * j + 1]
                dk_ref[:, ksls[j]] = dk[2 * j] + dk[2 * j + 1]
            return dbeta_acc, dgc_acc

        zero = jnp.zeros((CH, DN_H), F32)
        dbeta_acc, dgc_acc = lax.fori_loop(0, DN_H // HB, group, (zero, zero))
        dbeta_ref[...] = dbeta_acc
        dg_ref[...] = _dot((ii <= jj).astype(F32), dgc_acc + gct[...].T, precision=HI)

    rev = lambda wd: pl.BlockSpec((CH, wd), lambda i: (nch - 1 - i, 0))
    rev_t = pl.BlockSpec((DK, 2048), lambda i: (nch - 1 - i, 0))
    return pl.pallas_call(
        body, grid=(nch,),
        in_specs=[rev(2048), rev(1024), rev(1024), rev(2048), rev(DN_H), rev(DN_H), rev(2048), rev(2048), rev(2048),
                  rev(2048), rev(2048), rev_t, rev_t, rev(2048),
                  pl.BlockSpec((1, DN_H, DK, DK), lambda i: (nch - 1 - i, 0, 0, 0))],
        out_specs=[rev(1024), rev(1024), rev(2048), rev(DN_H), rev(DN_H)],
        out_shape=[SDS((rows, 1024), F32), SDS((rows, 1024), F32), SDS((rows, 2048), F32), SDS((rows, DN_H), F32),
                   SDS((rows, DN_H), F32)],
        scratch_shapes=[pltpu.VMEM((DN_H, DK, DK), F32), pltpu.VMEM((DN_H, CH), F32)],
        compiler_params=_cp("arbitrary"), name="dn_scan_bwd")(
            do, qn, kn, sv, gc, beta, at, pt, u, w, vn, qet, wt, ks, st)


def dn_prep_bwd(udn, yconv, conv_w, a_log, dt_bias, dqn, dkn, dv, dbeta, dg):
    rows = udn.shape[0]
    nch = rows // CH
    ext = CH + 8

    def body(xc_ref, ba_ref, yc_ref, yn_ref, dqn_n, dkn_n, dv_n, cw_ref, al_ref, dtb_ref, dqn_ref, dkn_ref, dv_ref,
             dbeta_ref, dg_ref, dx_ref, dba_ref, dcw_ref, dal_ref, ddtb_ref):
        c = pl.program_id(0)
        first = c == 0
        own = (lax.broadcasted_iota(jnp.int32, (ext, 1), 0) < CH) | (c < nch - 1)

        @pl.when(first)
        def _():
            dcw_ref[...] = jnp.zeros_like(dcw_ref)
            dal_ref[...] = jnp.zeros_like(dal_ref)
            ddtb_ref[...] = jnp.zeros_like(ddtb_ref)

        real, xa, beta, g = _gates(ba_ref[...], al_ref[...], dtb_ref[...], c)
        dgm = jnp.where(real, dg_ref[...], 0.0)
        da = dgm * (-jnp.exp(al_ref[...])) * jax.nn.sigmoid(xa)
        dal_ref[...] += jnp.sum(dgm * g, axis=0, keepdims=True)
        ddtb_ref[...] += jnp.sum(da, axis=0, keepdims=True)
        dba_ref[...] = jnp.zeros_like(dba_ref)
        dba_ref[:, 0:DN_H] = jnp.where(real, dbeta_ref[...] * beta * (1.0 - beta), 0.0)
        dba_ref[:, DN_H:2 * DN_H] = da

        def through_conv(off, g_cur, g_next, grad_fn):
            sl = pl.ds(off, DK)
            y = jnp.concatenate([yc_ref[:, sl], yn_ref[:, sl]], axis=0)
            sg = jax.nn.sigmoid(y)
            dsilu = sg * (1.0 + y * (1.0 - sg))
            dy = jnp.where(own, grad_fn(y * sg, jnp.concatenate([g_cur, g_next], axis=0)) * dsilu, 0.0)
            shifted = [dy[3 - j:3 - j + CH] for j in range(4)]
            x = xc_ref[:, sl]
            dx = cw_ref[0:1, sl] * shifted[0]
            for j in range(1, 4):
                dx += cw_ref[j:j + 1, sl] * shifted[j]
            dx_ref[:, sl] = dx.astype(BF16)
            for j in range(4):
                dcw_ref[j:j + 1, sl] += jnp.sum(shifted[j] * x, axis=0, keepdims=True)

        def l2_bwd(scale):
            def f(s, gin):
                r = lax.rsqrt(jnp.sum(s * s, axis=-1, keepdims=True) + EPS)
                nrm = s * r
                return (r * scale) * (gin - nrm * jnp.sum(nrm * gin, axis=-1, keepdims=True))
            return f

        def qk_body(kh, carry):
            sl = pl.ds(pl.multiple_of(kh * DK, DK), DK)
            through_conv(pl.multiple_of(kh * DK, DK), dqn_ref[:, sl], dqn_n[:, sl], l2_bwd(DK ** -0.5))
            through_conv(pl.multiple_of(1024 + kh * DK, DK), dkn_ref[:, sl], dkn_n[:, sl], l2_bwd(1.0))
            return carry

        lax.fori_loop(0, DN_KH, qk_body, 0, unroll=GROUP_UNROLL)

        def v_body(hv, carry):
            sl = pl.ds(pl.multiple_of(hv * DK, DK), DK)
            through_conv(pl.multiple_of(2048 + hv * DK, DK), dv_ref[:, sl], dv_n[:, sl], lambda s, gin: gin)
            return carry

        lax.fori_loop(0, DN_H, v_body, 0, unroll=GROUP_UNROLL)

    full = lambda shape: pl.BlockSpec(shape, lambda c: (0, 0))
    blk = lambda w: pl.BlockSpec((CH, w), lambda c: (c, 0))
    nxt = lambda w: pl.BlockSpec((8, w), lambda c: (jnp.minimum(8 * c + 8, rows // 8 - 1), 0))
    return pl.pallas_call(
        body, grid=(nch,),
        in_specs=[_chunk_specs(0)[0], _chunk_specs(0)[2], blk(4096), nxt(4096), nxt(1024), nxt(1024), nxt(2048),
                  full((4, 4096)), full((1, DN_H)), full((1, DN_H)), blk(1024), blk(1024), blk(2048), blk(DN_H),
                  blk(DN_H)],
        out_specs=[blk(4096), blk(DK), full((8, 4096)), full((1, DN_H)), full((1, DN_H))],
        out_shape=[SDS((rows, 4096), BF16), SDS((rows, DK), F32), SDS((8, 4096), F32), SDS((1, DN_H), F32),
                   SDS((1, DN_H), F32)],
        compiler_params=_cp("arbitrary"), name="dn_prep_bwd")(
            udn, udn, yconv, yconv, dqn, dkn, dv, conv_w, a_log, dt_bias, dqn, dkn, dv, dbeta, dg)


def local_step(x, target, w):
    seq = x.shape[0]
    bf = lambda a: a.astype(BF16)
    h0 = jnp.concatenate([jnp.zeros((PAD, D_MODEL), F32), w["meta_tokens"], x], axis=0)
    tgt = jnp.concatenate([jnp.zeros((BLK, D_MODEL), F32), target], axis=0)
    win = w["attn_w_in"]
    wq, wkv, wg = win[:, :1024], win[:, 1024:1280], win[:, 1280:]
    wa_in = bf(jnp.concatenate([wq, wg, wkv], axis=1))
    wa_out = bf(w["attn_w_out"])
    wd_in = jnp.concatenate([bf(w["dn_w_in"]), jnp.zeros((D_MODEL, 96), BF16)], axis=1)
    wd_out = bf(w["dn_w_out"])
    qw, kw, sinks = w["attn_q_norm_w"], w["attn_k_norm_w"], w["attn_sinks"]
    cw, al, dtb, ow = w["dn_conv_w"], w["dn_a_log"], w["dn_dt_bias"], w["dn_o_norm_w"]

    ua, xn0 = norm_matmul(h0, w["attn_norm_w"], wa_in, 2304, "attn_in")
    og = attn_fwd(ua, qw, kw, sinks)
    h1 = matmul_residual(og, wa_out, h0, "attn_out")
    ud, xn1 = norm_matmul(h1, w["dn_norm_w"], wd_in, 6272, "dn_in")
    qn, kn, sv, gc, beta, u, wy, qe, ks, p, at, pt, qet, wt, kst, yconv = dn_prep(ud, cw, al, dtb)
    o, vn, st = dn_scan(u, wy, qe, kst, p, gc)
    dh2, on, ls = dn_out_fwd(o, ud, ow, wd_out, h1, tgt)
    loss = (0.5 / D_MODEL) * jnp.sum(ls)

    do, dz, dow = dn_out_bwd(dh2, wd_out, o, ud, ow)
    g_dn_out = wgrad(on, dh2, "dn_out_wgrad")
    dqn, dkn, dv, dbeta, dg = dn_scan_bwd(do, qn, kn, sv, gc, beta, at, pt, u, wy, vn, qet, wt, ks, st)
    dxc, dba, dcw, dal, ddtb = dn_prep_bwd(ud, yconv, cw, al, dtb, dqn, dkn, dv, dbeta, dg)
    dh1, dnw1 = in_proj_bwd([dxc, dz, dba], [wd_in[:, :4096], wd_in[:, 4096:6144], wd_in[:, 6144:]],
                            h1, w["dn_norm_w"], dh2, "dn_in_bwd")
    g_dn_in = jnp.concatenate([wgrad(xn1, dxc, "dn_in_wgrad_qkv"), wgrad(xn1, dz, "dn_in_wgrad_z"),
                               wgrad(xn1, dba, "dn_in_wgrad_ba")[:, :2 * DN_H]], axis=1)

    dog = matmul_nt(dh1, wa_out, "attn_out_bwd")
    g_attn_out = wgrad(og, dh1, "attn_out_wgrad")
    dq, dgate, dkv, dkvm, dqw, dkw, dsk = attn_bwd(ua, qw, kw, sinks, dog)
    dkv = dkv.at[PAD:BLK].add(dkvm)
    dh0, dnw0 = in_proj_bwd([dq, dgate, dkv], [wa_in[:, :1024], wa_in[:, 1024:2048], wa_in[:, 2048:]],
                            h0, w["attn_norm_w"], dh1, "attn_in_bwd")
    g_attn_in = jnp.concatenate([wgrad(xn0, dq, "attn_in_wgrad_q"), wgrad(xn0, dkv, "attn_in_wgrad_kv"),
                                 wgrad(xn0, dgate, "attn_in_wgrad_g")], axis=1)
    grads = {
        "meta_tokens": dh0[PAD:BLK], "attn_norm_w": jnp.sum(dnw0, axis=0), "attn_w_in": g_attn_in,
        "attn_q_norm_w": dqw, "attn_k_norm_w": dkw, "attn_sinks": dsk, "attn_w_out": g_attn_out,
        "dn_norm_w": jnp.sum(dnw1, axis=0), "dn_w_in": g_dn_in, "dn_conv_w": dcw[:4], "dn_a_log": dal,
        "dn_dt_bias": ddtb, "dn_o_norm_w": jnp.sum(dow, axis=0), "dn_w_out": g_dn_out,
    }
    return loss, dh0[BLK:BLK + seq], grads


WEIGHTS = ["meta_tokens", "attn_norm_w", "attn_w_in", "attn_q_norm_w", "attn_k_norm_w", "attn_sinks", "attn_w_out",
           "dn_norm_w", "dn_w_in", "dn_conv_w", "dn_a_log", "dn_dt_bias", "dn_o_norm_w", "dn_w_out"]
SHARDED = {"attn_w_in": ((1024, 2304), 1), "attn_w_out": ((1024, 1024), 0), "dn_w_in": ((1024, 6176), 1),
           "dn_w_out": ((2048, 1024), 0), "dn_conv_w": ((4, 4096), 1), "meta_tokens": ((16, 1024), 1),
           "dn_norm_w": ((1, 1024), 1)}
REPLICATED = {"attn_norm_w": 1024, "attn_q_norm_w": 64, "attn_k_norm_w": 64, "attn_sinks": 16, "dn_a_log": 16,
              "dn_dt_bias": 16, "dn_o_norm_w": 128}
N_CHIPS = 4
PACK_ROWS = 2912
HALF_ROWS = PACK_ROWS // 2
SMALL_ROWS = 8


def _shard_shape(name):
    (r, c), axis = SHARDED[name]
    return (r // N_CHIPS, c) if axis == 0 else (r, c // N_CHIPS)


def _pack(parts, rows):
    flat = jnp.concatenate([p.reshape(-1) for p in parts])
    return jnp.pad(flat, (0, rows * 1024 - flat.shape[0])).reshape(rows, 1024)


def pack_shard(shards):
    return _pack([shards[n] for n in SHARDED], PACK_ROWS)


def unpack_shard(buf):
    flat, out, pos = buf.reshape(-1), {}, 0
    for n in SHARDED:
        shp = _shard_shape(n)
        size = shp[0] * shp[1]
        out[n] = flat[pos:pos + size].reshape(shp)
        pos += size
    return out


MATRICES = ("attn_w_in", "attn_w_out", "dn_w_in", "dn_w_out")


def pack_gather(shards):
    big = [shards[n].astype(BF16).reshape(-1) for n in MATRICES]
    small = jnp.concatenate([shards[n].reshape(-1) for n in SHARDED if n not in MATRICES])
    flat = jnp.concatenate(big + [lax.bitcast_convert_type(small, BF16).reshape(-1)])
    return jnp.pad(flat, (0, PACK_ROWS * 1024 - flat.shape[0])).reshape(PACK_ROWS, 1024)


def unpack_gather(buf):
    PER_F32 = 4 // jnp.dtype(buf.dtype).itemsize
    flat, out, pos = buf.reshape(-1), {}, 0
    for n in MATRICES:
        shp = _shard_shape(n)
        out[n] = flat[pos:pos + shp[0] * shp[1]].reshape(shp)
        pos += shp[0] * shp[1]
    for n in SHARDED:
        if n not in MATRICES:
            shp = _shard_shape(n)
            raw = flat[pos:pos + shp[0] * shp[1] * PER_F32]
            out[n] = lax.bitcast_convert_type(raw.reshape(-1, PER_F32) if PER_F32 > 1 else raw, F32).reshape(shp)
            pos += shp[0] * shp[1] * PER_F32
    return out


def pack_small(vals):
    return _pack([vals[n] for n in REPLICATED], SMALL_ROWS)


def unpack_small(buf):
    flat, out, pos = buf.reshape(-1), {}, 0
    for n, size in REPLICATED.items():
        out[n] = flat[pos:pos + size].reshape(1, size)
        pos += size
    return out


ANY = pl.BlockSpec(memory_space=pl.ANY)


def _place():
    return lax.axis_index("x"), lax.axis_index("y"), lax.axis_index("c")


def chips_exchange(src, gather):
    r = src.shape[-2]

    def body(s_ref, o_ref, send_sems, recv_sems):
        x, y, c = _place()
        me = 2 * x + y
        peers = [(1 - x, y), (x, 1 - y), (1 - x, 1 - y)]

        def copy(k, to_block, from_block):
            px, py = peers[k]
            return pltpu.make_async_remote_copy(
                src_ref=s_ref if gather else s_ref.at[to_block], dst_ref=o_ref.at[from_block],
                send_sem=send_sems.at[k], recv_sem=recv_sems.at[k], device_id=(px, py, c), device_id_type=MESH)

        sends = [copy(k, 2 * px + py, me) for k, (px, py) in enumerate(peers)]
        for cp in sends:
            cp.start()
        for k, (px, py) in enumerate(peers):
            copy(k, me, 2 * px + py).wait_recv()
        for cp in sends:
            cp.wait_send()

    return pl.pallas_call(
        body, in_specs=[ANY], out_specs=ANY, out_shape=SDS((N_CHIPS, r, 1024), src.dtype),
        scratch_shapes=[pltpu.SemaphoreType.DMA((3,)), pltpu.SemaphoreType.DMA((3,))],
        name="chips_gather" if gather else "chips_exchange")(src)


def chip_sum(received, pair, me):
    tm = 208

    def body(me_ref, own_ref, r1_ref, r2_ref, r3_ref, o_ref):
        o_ref[...] = ((own_ref[0] + r1_ref[0].astype(F32)) + r2_ref[0].astype(F32)) + r3_ref[0].astype(F32)

    blk = lambda k: pl.BlockSpec((1, tm, 1024), lambda i, me_ref: ((me_ref[0] + k) % N_CHIPS, i, 0))
    return pl.pallas_call(
        body,
        grid_spec=pltpu.PrefetchScalarGridSpec(
            num_scalar_prefetch=1, grid=(HALF_ROWS // tm,), in_specs=[blk(0), blk(1), blk(2), blk(3)],
            out_specs=pl.BlockSpec((tm, 1024), lambda i, me_ref: (i, 0))),
        out_shape=SDS((HALF_ROWS, 1024), F32), compiler_params=_cp("parallel"), name="chip_sum")(
            me.reshape(1).astype(jnp.int32), pair, received, received, received)


def _rows_at(ref, start, size):
    return ref.at[:, pl.ds(start, size), :] if len(ref.shape) == 3 else ref.at[pl.ds(start, size), :]


def sibling_join(src, name):
    axis = len(src.shape) - 2

    def body(s_ref, o_ref, send_sem, recv_sem):
        x, y, c = _place()
        cp = pltpu.make_async_remote_copy(src_ref=s_ref, dst_ref=o_ref, send_sem=send_sem, recv_sem=recv_sem,
                                          device_id=(x, y, 1 - c), device_id_type=MESH)
        cp.start()
        cp.wait()

    theirs = pl.pallas_call(
        body, in_specs=[ANY], out_specs=ANY, out_shape=SDS(src.shape, src.dtype),
        scratch_shapes=[pltpu.SemaphoreType.DMA, pltpu.SemaphoreType.DMA], name=name)(src)
    first = lax.axis_index("c") == 0
    return jnp.concatenate([jnp.where(first, src, theirs), jnp.where(first, theirs, src)], axis=axis)


def sibling_give(g_all):
    def body(s_ref, o_ref, send_sem, recv_sem):
        x, y, c = _place()
        cp = pltpu.make_async_remote_copy(
            src_ref=_rows_at(s_ref, (1 - c) * HALF_ROWS, HALF_ROWS), dst_ref=o_ref, send_sem=send_sem,
            recv_sem=recv_sem, device_id=(x, y, 1 - c), device_id_type=MESH)
        cp.start()
        cp.wait()

    return pl.pallas_call(
        body, in_specs=[ANY], out_specs=ANY, out_shape=SDS((N_CHIPS, HALF_ROWS, 1024), F32),
        scratch_shapes=[pltpu.SemaphoreType.DMA, pltpu.SemaphoreType.DMA], name="pair_exchange")(g_all)


def pair_sum(g_all, got, c):
    tm = 208
    per_half = HALF_ROWS // tm

    def body(c_ref, a_ref, b_ref, o_ref, o16_ref):
        s = a_ref[...] + b_ref[...]
        o_ref[...] = s
        o16_ref[...] = s.astype(BF16)

    out = pl.BlockSpec((1, tm, 1024), lambda j, i, c_ref: (j, i, 0))
    return pl.pallas_call(
        body,
        grid_spec=pltpu.PrefetchScalarGridSpec(
            num_scalar_prefetch=1, grid=(N_CHIPS, per_half),
            in_specs=[pl.BlockSpec((1, tm, 1024), lambda j, i, c_ref: (j, c_ref[0] * per_half + i, 0)), out],
            out_specs=[out, out]),
        out_shape=[SDS((N_CHIPS, HALF_ROWS, 1024), F32), SDS((N_CHIPS, HALF_ROWS, 1024), BF16)],
        compiler_params=_cp("parallel", "parallel"), name="pair_sum")(c.reshape(1).astype(jnp.int32), g_all, got)


def all_gather_small(src):
    def body(s_ref, o_ref, send_sems, recv_sems, local_sem):
        x, y, c = _place()
        flips = [(fx, fy, fc) for fx in (0, 1) for fy in (0, 1) for fc in (0, 1)][1:]
        idx = lambda px, py, pc: 4 * px + 2 * py + pc
        mine = pltpu.make_async_copy(s_ref, o_ref.at[idx(x, y, c)], local_sem)
        mine.start()

        def peer(k):
            fx, fy, fc = flips[k]
            return (1 - x if fx else x, 1 - y if fy else y, 1 - c if fc else c)

        def copy(k, block):
            return pltpu.make_async_remote_copy(
                src_ref=s_ref, dst_ref=o_ref.at[block], send_sem=send_sems.at[k], recv_sem=recv_sems.at[k],
                device_id=peer(k), device_id_type=MESH)

        sends = [copy(k, idx(x, y, c)) for k in range(7)]
        for cp in sends:
            cp.start()
        for k in range(7):
            copy(k, idx(*peer(k))).wait_recv()
        for cp in sends:
            cp.wait_send()
        mine.wait()

    return pl.pallas_call(
        body, in_specs=[ANY], out_specs=ANY, out_shape=SDS((8,) + src.shape, F32),
        scratch_shapes=[pltpu.SemaphoreType.DMA((7,)), pltpu.SemaphoreType.DMA((7,)), pltpu.SemaphoreType.DMA],
        name="all_gather_small")(src)


def sum_blocks(t, name):
    n, r, _ = t.shape
    tm = 208 if r % 208 == 0 else r

    def body(t_ref, o_ref):
        acc = t_ref[0]
        for i in range(1, n):
            acc = acc + t_ref[i]
        o_ref[...] = acc

    return pl.pallas_call(
        body, grid=(r // tm,), in_specs=[pl.BlockSpec((n, tm, 1024), lambda i: (0, i, 0))],
        out_specs=pl.BlockSpec((tm, 1024), lambda i: (i, 0)), out_shape=SDS((r, 1024), F32),
        compiler_params=_cp("parallel"), name=name)(t)


ADAM_BLOCK_BYTES = 1024 * 1024


def adamw(w, g, m, v, name):
    rows, cols = w.shape
    tm = rows
    while tm * cols * 4 > ADAM_BLOCK_BYTES and tm % 16 == 0:
        tm //= 2

    def body(w_ref, g_ref, m_ref, v_ref, d_ref, nm_ref, nv_ref):
        g_ = g_ref[...]
        m_ = ADAM_B1 * m_ref[...] + (1.0 - ADAM_B1) * g_
        v_ = ADAM_B2 * v_ref[...] + (1.0 - ADAM_B2) * (g_ * g_)
        m_hat = m_ / (1.0 - ADAM_B1 ** ADAM_STEP)
        v_hat = v_ / (1.0 - ADAM_B2 ** ADAM_STEP)
        d_ref[...] = -ADAM_LR * (m_hat / (jnp.sqrt(v_hat) + ADAM_EPS) + ADAM_WD * w_ref[...])
        nm_ref[...] = m_
        nv_ref[...] = v_

    spec = pl.BlockSpec((tm, cols), lambda i: (i, 0))
    return pl.pallas_call(
        body, grid=(rows // tm,), in_specs=[spec] * 4, out_specs=[spec] * 3,
        out_shape=[SDS((rows, cols), F32)] * 3, compiler_params=_cp("parallel"), name=name)(w, g, m, v)


LAYERED = ("attn_w_in", "attn_w_out", "dn_w_in", "dn_conv_w", "dn_w_out")


def _two_d(name, a):
    return a[0] if name in LAYERED else a


def kernel(x, meta_tokens, attn_norm_w, attn_w_in, attn_q_norm_w, attn_k_norm_w, attn_sinks, attn_w_out, dn_norm_w, dn_w_in, dn_conv_w, dn_a_log, dn_dt_bias, dn_o_norm_w, dn_w_out, loss_target, m_meta_tokens, m_attn_norm_w, m_attn_w_in, m_attn_q_norm_w, m_attn_k_norm_w, m_attn_sinks, m_attn_w_out, m_dn_norm_w, m_dn_w_in, m_dn_conv_w, m_dn_a_log, m_dn_dt_bias, m_dn_o_norm_w, m_dn_w_out, v_meta_tokens, v_attn_norm_w, v_attn_w_in, v_attn_q_norm_w, v_attn_k_norm_w, v_attn_sinks, v_attn_w_out, v_dn_norm_w, v_dn_w_in, v_dn_conv_w, v_dn_a_log, v_dn_dt_bias, v_dn_o_norm_w, v_dn_w_out):
    given = dict(zip(WEIGHTS, (meta_tokens, attn_norm_w, attn_w_in, attn_q_norm_w, attn_k_norm_w, attn_sinks,
                               attn_w_out, dn_norm_w, dn_w_in, dn_conv_w, dn_a_log, dn_dt_bias, dn_o_norm_w, dn_w_out)))
    mom1 = dict(zip(WEIGHTS, (m_meta_tokens, m_attn_norm_w, m_attn_w_in, m_attn_q_norm_w, m_attn_k_norm_w,
                              m_attn_sinks, m_attn_w_out, m_dn_norm_w, m_dn_w_in, m_dn_conv_w, m_dn_a_log,
                              m_dn_dt_bias, m_dn_o_norm_w, m_dn_w_out)))
    mom2 = dict(zip(WEIGHTS, (v_meta_tokens, v_attn_norm_w, v_attn_w_in, v_attn_q_norm_w, v_attn_k_norm_w,
                              v_attn_sinks, v_attn_w_out, v_dn_norm_w, v_dn_w_in, v_dn_conv_w, v_dn_a_log,
                              v_dn_dt_bias, v_dn_o_norm_w, v_dn_w_out)))
    two_d = lambda d: {n: _two_d(n, a) for n, a in d.items()}
    given, mom1, mom2 = two_d(given), two_d(mom1), two_d(mom2)
    c = lax.axis_index("c")

    me = 2 * lax.axis_index("x") + lax.axis_index("y")
    own_half = lax.dynamic_slice_in_dim(pack_gather(given), c * HALF_ROWS, HALF_ROWS, axis=0)
    mine = lax.dynamic_update_slice_in_dim(chips_exchange(own_half, True), own_half[None], me, 0)
    gathered = sibling_join(mine, "gather_swap")
    per_chip = [unpack_gather(gathered[j]) for j in range(N_CHIPS)]
    full = {n: jnp.concatenate([pc[n] for pc in per_chip], axis=SHARDED[n][1]) for n in SHARDED}
    full.update({n: given[n] for n in REPLICATED})

    loss, dx, grads = local_step(x[0], loss_target[0], full)

    split = lambda n: jnp.split(grads[n], N_CHIPS, axis=SHARDED[n][1])
    g_all = jnp.stack([pack_shard({n: split(n)[j] for n in SHARDED}) for j in range(N_CHIPS)])
    pair, pair16 = pair_sum(g_all, sibling_give(g_all), c)
    half = chip_sum(chips_exchange(pair16, False), pair, me)
    g_shard = sibling_join(half, "half_exchange")

    g_small = sum_blocks(all_gather_small(pack_small(grads)), "small_sum")

    g_local = unpack_shard(g_shard)
    g_local.update(unpack_small(g_small))
    steps = {n: adamw(given[n], g_local[n], mom1[n], mom2[n], "adamw_" + n) for n in WEIGHTS}
    shaped = lambda n, a: a[None] if n in LAYERED else a
    outs = [[shaped(n, g_local[n]) for n in WEIGHTS]]
    outs += [[shaped(n, steps[n][k]) for n in WEIGHTS] for k in range(3)]

    loss = lax.psum(loss, ("x", "y", "c"))
    return (loss, dx[None], *outs[0], *outs[1], *outs[2], *outs[3])
```
